```python
import jax, jax.numpy as jnp
from jax import lax
import numpy as np

D_MODEL = 1024
BATCH = 8
SEQ = 2048
DEPTH = 2

N_A = DEPTH // 2
N_B = DEPTH - N_A
CHUNK = 128
A_GROUPS = 8
A_GROUP_DIM = D_MODEL // A_GROUPS
N_HEADS = 16
HEAD_DIM = D_MODEL // N_HEADS
Q_BLOCK = 128
D_FF = 4 * D_MODEL
PLE_DIM = 256
EPS = 1e-6

kernel_name = "yoco_gmlp_stickbreaking_hybrid"


def rms_norm(x, g):
    xf = x.astype(jnp.float32)
    y = xf * lax.rsqrt(jnp.mean(xf * xf, axis=-1, keepdims=True) + EPS)
    return (y * g.astype(jnp.float32)).astype(x.dtype)


def sgu_mixer(h, w_in, g_v, w_s, b_s, w_out):
    bsz, seq, _ = h.shape
    z = jax.nn.gelu(h @ w_in)
    u, v = jnp.split(z, 2, axis=-1)
    v = rms_norm(v, g_v)
    v = v.reshape(bsz, seq // CHUNK, CHUNK, A_GROUPS, A_GROUP_DIM)
    causal = jnp.tril(jnp.ones((CHUNK, CHUNK), dtype=w_s.dtype))
    w = w_s * causal[None]
    mix = jnp.einsum('gts,bcsgd->bctgd', w, v) + jnp.transpose(b_s)[None, None, :, :, None]
    y = u * mix.reshape(bsz, seq, D_MODEL)
    return y @ w_out


def sqrelu_mlp(h, w_up, w_down):
    a = jax.nn.relu(h @ w_up)
    return (a * a) @ w_down


def shared_kv(x, ln_kv, w_kv, g_k):
    bsz, seq, _ = x.shape
    h = rms_norm(x, ln_kv)
    k, v = jnp.split(h @ w_kv, 2, axis=-1)
    k = rms_norm(k.reshape(bsz, seq, N_HEADS, HEAD_DIM), g_k)
    v = v.reshape(bsz, seq, N_HEADS, HEAD_DIM)
    return jnp.transpose(k, (0, 2, 1, 3)), jnp.transpose(v, (0, 2, 1, 3))


def stick_breaking(q, k, v):
    seq = q.shape[2]
    scale = HEAD_DIM ** -0.5
    outs = []
    for blk in range(seq // Q_BLOCK):
        t0 = blk * Q_BLOCK
        t1 = t0 + Q_BLOCK
        qb = q[:, :, t0:t1].astype(jnp.float32)
        kb = k[:, :, :t1].astype(jnp.float32)
        vb = v[:, :, :t1].astype(jnp.float32)
        z = jnp.einsum('bhqd,bhkd->bhqk', qb, kb) * scale
        q_idx = t0 + jnp.arange(Q_BLOCK)[:, None]
        k_idx = jnp.arange(t1)[None, :]
        causal = k_idx < q_idx
        log_1m_beta = jnp.where(causal, jax.nn.log_sigmoid(-z), 0.0)
        between = lax.cumsum(log_1m_beta, axis=3, reverse=True) - log_1m_beta
        a = jnp.where(causal, jnp.exp(jax.nn.log_sigmoid(z) + between), 0.0)
        o = jnp.einsum('bhqk,bhkd->bhqd', a, vb)
        outs.append(o.astype(v.dtype))
    return jnp.concatenate(outs, axis=2)


def stick_breaking_mixer(h, w_q, g_q, k, v, w_out):
    bsz, seq, _ = h.shape
    q = rms_norm((h @ w_q).reshape(bsz, seq, N_HEADS, HEAD_DIM), g_q)
    q = jnp.transpose(q, (0, 2, 1, 3))
    o = stick_breaking(q, k, v)
    o = jnp.transpose(o, (0, 2, 1, 3)).reshape(bsz, seq, D_MODEL)
    return o @ w_out


def _fwd_setup_inputs(seed: int = 0) -> dict:
    key = jax.random.key(seed)
    ks = jax.random.split(key, 32)

    def nrm(k, shape, scale):
        return jax.random.normal(k, shape, dtype=jnp.float32) * scale

    def gain(k, shape):
        return 1.0 + nrm(k, shape, 0.02)

    return {
        "x": nrm(ks[0], (BATCH, SEQ, D_MODEL), 1.0),
        "p": nrm(ks[1], (DEPTH, BATCH, SEQ, PLE_DIM), 1.0),
        "ln_mix_a": gain(ks[2], (N_A, D_MODEL)),
        "w_in_a": nrm(ks[3], (N_A, D_MODEL, 2 * D_MODEL), D_MODEL ** -0.5),
        "g_v_a": gain(ks[4], (N_A, D_MODEL)),
        "w_spatial": nrm(ks[5], (N_A, A_GROUPS, CHUNK, CHUNK), CHUNK ** -0.5),
        "b_spatial": 1.0 + nrm(ks[6], (N_A, A_GROUPS, CHUNK), 0.02),
        "w_out_a": nrm(ks[7], (N_A, D_MODEL, D_MODEL), D_MODEL ** -0.5),
        "ln_kv": gain(ks[8], (D_MODEL,)),
        "w_kv": nrm(ks[9], (D_MODEL, 2 * D_MODEL), D_MODEL ** -0.5),
        "g_k": gain(ks[10], (HEAD_DIM,)),
        "ln_mix_b": gain(ks[11], (N_B, D_MODEL)),
        "w_q": nrm(ks[12], (N_B, D_MODEL, D_MODEL), D_MODEL ** -0.5),
        "g_q": gain(ks[13], (N_B, HEAD_DIM)),
        "w_out_b": nrm(ks[14], (N_B, D_MODEL, D_MODEL), D_MODEL ** -0.5),
        "ln_mlp": gain(ks[15], (DEPTH, D_MODEL)),
        "w_up": nrm(ks[16], (DEPTH, D_MODEL, D_FF), D_MODEL ** -0.5),
        "w_down": nrm(ks[17], (DEPTH, D_FF, D_MODEL), D_FF ** -0.5),
        "ln_ple": gain(ks[18], (DEPTH, D_MODEL)),
        "w_ple_gate": nrm(ks[19], (DEPTH, D_MODEL, D_MODEL), D_MODEL ** -0.5),
        "w_ple_proj": nrm(ks[20], (DEPTH, PLE_DIM, D_MODEL), PLE_DIM ** -0.5),
    }


def _fwd_reference(x, p, ln_mix_a, w_in_a, g_v_a, w_spatial, b_spatial, w_out_a,
              ln_kv, w_kv, g_k, ln_mix_b, w_q, g_q, w_out_b,
              ln_mlp, w_up, w_down, ln_ple, w_ple_gate, w_ple_proj):
    k_shared = None
    v_shared = None
    for i in range(DEPTH):
        if i < N_A:
            h = rms_norm(x, ln_mix_a[i])
            x = x + sgu_mixer(h, w_in_a[i], g_v_a[i], w_spatial[i], b_spatial[i], w_out_a[i])
        else:
            j = i - N_A
            h = rms_norm(x, ln_mix_b[j])
            x = x + stick_breaking_mixer(h, w_q[j], g_q[j], k_shared, v_shared, w_out_b[j])
        x = x + sqrelu_mlp(rms_norm(x, ln_mlp[i]), w_up[i], w_down[i])
        gate = jax.nn.sigmoid(rms_norm(x, ln_ple[i]) @ w_ple_gate[i])
        x = x + (p[i] @ w_ple_proj[i]) * gate
        if i == N_A - 1:
            k_shared, v_shared = shared_kv(x, ln_kv, w_kv, g_k)
    return x


import jax as _jax
import jax.numpy as _jnp

TWIN_FORMAT = 'train_step'
FWD_PARAMS = ['x', 'p', 'ln_mix_a', 'w_in_a', 'g_v_a', 'w_spatial', 'b_spatial', 'w_out_a', 'ln_kv', 'w_kv', 'g_k', 'ln_mix_b', 'w_q', 'g_q', 'w_out_b', 'ln_mlp', 'w_up', 'w_down', 'ln_ple', 'w_ple_gate', 'w_ple_proj']
TWIN_WEIGHTS = ['ln_mix_a', 'w_in_a', 'g_v_a', 'w_spatial', 'b_spatial', 'w_out_a', 'ln_kv', 'w_kv', 'g_k', 'ln_mix_b', 'w_q', 'g_q', 'w_out_b', 'ln_mlp', 'w_up', 'w_down', 'ln_ple', 'w_ple_gate', 'w_ple_proj']
TWIN_DIFF_INPUT = 'x'
TWIN_INPUTS = ['x', 'p', 'ln_mix_a', 'w_in_a', 'g_v_a', 'w_spatial', 'b_spatial', 'w_out_a', 'ln_kv', 'w_kv', 'g_k', 'ln_mix_b', 'w_q', 'g_q', 'w_out_b', 'ln_mlp', 'w_up', 'w_down', 'ln_ple', 'w_ple_gate', 'w_ple_proj', 'loss_target', 'm_ln_mix_a', 'm_w_in_a', 'm_g_v_a', 'm_w_spatial', 'm_b_spatial', 'm_w_out_a', 'm_ln_kv', 'm_w_kv', 'm_g_k', 'm_ln_mix_b', 'm_w_q', 'm_g_q', 'm_w_out_b', 'm_ln_mlp', 'm_w_up', 'm_w_down', 'm_ln_ple', 'm_w_ple_gate', 'm_w_ple_proj', 'v_ln_mix_a', 'v_w_in_a', 'v_g_v_a', 'v_w_spatial', 'v_b_spatial', 'v_w_out_a', 'v_ln_kv', 'v_w_kv', 'v_g_k', 'v_ln_mix_b', 'v_w_q', 'v_g_q', 'v_w_out_b', 'v_ln_mlp', 'v_w_up', 'v_w_down', 'v_ln_ple', 'v_w_ple_gate', 'v_w_ple_proj']
TWIN_OUTPUTS = ['loss', 'grad_x', 'grad_ln_mix_a', 'grad_w_in_a', 'grad_g_v_a', 'grad_w_spatial', 'grad_b_spatial', 'grad_w_out_a', 'grad_ln_kv', 'grad_w_kv', 'grad_g_k', 'grad_ln_mix_b', 'grad_w_q', 'grad_g_q', 'grad_w_out_b', 'grad_ln_mlp', 'grad_w_up', 'grad_w_down', 'grad_ln_ple', 'grad_w_ple_gate', 'grad_w_ple_proj', 'delta_ln_mix_a', 'delta_w_in_a', 'delta_g_v_a', 'delta_w_spatial', 'delta_b_spatial', 'delta_w_out_a', 'delta_ln_kv', 'delta_w_kv', 'delta_g_k', 'delta_ln_mix_b', 'delta_w_q', 'delta_g_q', 'delta_w_out_b', 'delta_ln_mlp', 'delta_w_up', 'delta_w_down', 'delta_ln_ple', 'delta_w_ple_gate', 'delta_w_ple_proj', 'new_m_ln_mix_a', 'new_m_w_in_a', 'new_m_g_v_a', 'new_m_w_spatial', 'new_m_b_spatial', 'new_m_w_out_a', 'new_m_ln_kv', 'new_m_w_kv', 'new_m_g_k', 'new_m_ln_mix_b', 'new_m_w_q', 'new_m_g_q', 'new_m_w_out_b', 'new_m_ln_mlp', 'new_m_w_up', 'new_m_w_down', 'new_m_ln_ple', 'new_m_w_ple_gate', 'new_m_w_ple_proj', 'new_v_ln_mix_a', 'new_v_w_in_a', 'new_v_g_v_a', 'new_v_w_spatial', 'new_v_b_spatial', 'new_v_w_out_a', 'new_v_ln_kv', 'new_v_w_kv', 'new_v_g_k', 'new_v_ln_mix_b', 'new_v_w_q', 'new_v_g_q', 'new_v_w_out_b', 'new_v_ln_mlp', 'new_v_w_up', 'new_v_w_down', 'new_v_ln_ple', 'new_v_w_ple_gate', 'new_v_w_ple_proj']
TWIN_LEAF_KINDS = {'loss': 'loss', 'grad_x': 'grad_x', 'grad_ln_mix_a': 'grad_w', 'grad_w_in_a': 'grad_w', 'grad_g_v_a': 'grad_w', 'grad_w_spatial': 'grad_w', 'grad_b_spatial': 'grad_w', 'grad_w_out_a': 'grad_w', 'grad_ln_kv': 'grad_w', 'grad_w_kv': 'grad_w', 'grad_g_k': 'grad_w', 'grad_ln_mix_b': 'grad_w', 'grad_w_q': 'grad_w', 'grad_g_q': 'grad_w', 'grad_w_out_b': 'grad_w', 'grad_ln_mlp': 'grad_w', 'grad_w_up': 'grad_w', 'grad_w_down': 'grad_w', 'grad_ln_ple': 'grad_w', 'grad_w_ple_gate': 'grad_w', 'grad_w_ple_proj': 'grad_w', 'delta_ln_mix_a': 'delta_w', 'delta_w_in_a': 'delta_w', 'delta_g_v_a': 'delta_w', 'delta_w_spatial': 'delta_w', 'delta_b_spatial': 'delta_w', 'delta_w_out_a': 'delta_w', 'delta_ln_kv': 'delta_w', 'delta_w_kv': 'delta_w', 'delta_g_k': 'delta_w', 'delta_ln_mix_b': 'delta_w', 'delta_w_q': 'delta_w', 'delta_g_q': 'delta_w', 'delta_w_out_b': 'delta_w', 'delta_ln_mlp': 'delta_w', 'delta_w_up': 'delta_w', 'delta_w_down': 'delta_w', 'delta_ln_ple': 'delta_w', 'delta_w_ple_gate': 'delta_w', 'delta_w_ple_proj': 'delta_w', 'new_m_ln_mix_a': 'new_m', 'new_m_w_in_a': 'new_m', 'new_m_g_v_a': 'new_m', 'new_m_w_spatial': 'new_m', 'new_m_b_spatial': 'new_m', 'new_m_w_out_a': 'new_m', 'new_m_ln_kv': 'new_m', 'new_m_w_kv': 'new_m', 'new_m_g_k': 'new_m', 'new_m_ln_mix_b': 'new_m', 'new_m_w_q': 'new_m', 'new_m_g_q': 'new_m', 'new_m_w_out_b': 'new_m', 'new_m_ln_mlp': 'new_m', 'new_m_w_up': 'new_m', 'new_m_w_down': 'new_m', 'new_m_ln_ple': 'new_m', 'new_m_w_ple_gate': 'new_m', 'new_m_w_ple_proj': 'new_m', 'new_v_ln_mix_a': 'new_v', 'new_v_w_in_a': 'new_v', 'new_v_g_v_a': 'new_v', 'new_v_w_spatial': 'new_v', 'new_v_b_spatial': 'new_v', 'new_v_w_out_a': 'new_v', 'new_v_ln_kv': 'new_v', 'new_v_w_kv': 'new_v', 'new_v_g_k': 'new_v', 'new_v_ln_mix_b': 'new_v', 'new_v_w_q': 'new_v', 'new_v_g_q': 'new_v', 'new_v_w_out_b': 'new_v', 'new_v_ln_mlp': 'new_v', 'new_v_w_up': 'new_v', 'new_v_w_down': 'new_v', 'new_v_ln_ple': 'new_v', 'new_v_w_ple_gate': 'new_v', 'new_v_w_ple_proj': 'new_v'}


def _forward(args):
    return _fwd_reference(*[args[k] for k in FWD_PARAMS])


def _output_shape():
    out = _jax.eval_shape(lambda: _forward(_fwd_setup_inputs(0)))
    return out.shape, out.dtype

N_MICROBATCH = 1
ADAM_LR = 0.001
ADAM_B1 = 0.9
ADAM_B2 = 0.999
ADAM_EPS = 1e-08
ADAM_WD = 0.01
ADAM_STEP = 10
PER_EXAMPLE_BATCH_AXIS = {'x': 0, 'p': 1, 'loss_target': 0}
SHARED_INPUTS = []
_WEIGHT_DTYPES = {'ln_mix_a': _jnp.float32, 'w_in_a': _jnp.float32, 'g_v_a': _jnp.float32, 'w_spatial': _jnp.float32, 'b_spatial': _jnp.float32, 'w_out_a': _jnp.float32, 'ln_kv': _jnp.float32, 'w_kv': _jnp.float32, 'g_k': _jnp.float32, 'ln_mix_b': _jnp.float32, 'w_q': _jnp.float32, 'g_q': _jnp.float32, 'w_out_b': _jnp.float32, 'ln_mlp': _jnp.float32, 'w_up': _jnp.float32, 'w_down': _jnp.float32, 'ln_ple': _jnp.float32, 'w_ple_gate': _jnp.float32, 'w_ple_proj': _jnp.float32}
MOMENT_SCALE = {'ln_mix_a': 1.137105e+01, 'w_in_a': 6.683692e-01, 'g_v_a': 3.203462e+00, 'w_spatial': 2.093139e+00, 'b_spatial': 6.969716e+00, 'w_out_a': 7.450499e+00, 'ln_kv': 1.069122e+01, 'w_kv': 4.513353e+00, 'g_k': 1.404090e+01, 'ln_mix_b': 1.027475e+00, 'w_q': 1.040095e+00, 'g_q': 1.394375e+01, 'w_out_b': 5.903830e+00, 'ln_mlp': 4.954853e+01, 'w_up': 3.160836e+00, 'w_down': 1.004555e+01, 'ln_ple': 7.643637e-01, 'w_ple_gate': 5.839442e-01, 'w_ple_proj': 3.295109e-01}


def _to_microbatches(a, axis):
    t = _jnp.moveaxis(a, axis, 0)
    t = t.reshape((N_MICROBATCH, t.shape[0] // N_MICROBATCH) + t.shape[1:])
    return _jnp.moveaxis(t, 1, axis + 1)


def setup_inputs(seed: int = 0) -> dict:
    inp = _fwd_setup_inputs(seed)
    key = _jax.random.fold_in(_jax.random.key(seed), 7919)
    shape, _ = _output_shape()
    out = dict(inp)
    out["loss_target"] = _jax.random.normal(_jax.random.fold_in(key, 0), shape, _jnp.float32)
    for i, name in enumerate(TWIN_WEIGHTS):
        w = inp[name].astype(_jnp.float32)
        if MOMENT_SCALE is None:
            s = _jnp.sqrt(_jnp.mean(_jnp.square(w)) + 1e-30)
        else:
            s = MOMENT_SCALE[name]
        km, kv = _jax.random.split(_jax.random.fold_in(key, i + 1))
        out[name] = w
        out["m_" + name] = s * _jax.random.normal(km, w.shape, _jnp.float32)
        out["v_" + name] = (s * s) * _jax.random.uniform(kv, w.shape, _jnp.float32, 0.5, 1.5)
    if N_MICROBATCH > 1:
        for name, axis in PER_EXAMPLE_BATCH_AXIS.items():
            out[name] = _to_microbatches(out[name], axis)
    return {'x': out['x'], 'p': out['p'], 'ln_mix_a': out['ln_mix_a'], 'w_in_a': out['w_in_a'], 'g_v_a': out['g_v_a'], 'w_spatial': out['w_spatial'], 'b_spatial': out['b_spatial'], 'w_out_a': out['w_out_a'], 'ln_kv': out['ln_kv'], 'w_kv': out['w_kv'], 'g_k': out['g_k'], 'ln_mix_b': out['ln_mix_b'], 'w_q': out['w_q'], 'g_q': out['g_q'], 'w_out_b': out['w_out_b'], 'ln_mlp': out['ln_mlp'], 'w_up': out['w_up'], 'w_down': out['w_down'], 'ln_ple': out['ln_ple'], 'w_ple_gate': out['w_ple_gate'], 'w_ple_proj': out['w_ple_proj'], 'loss_target': out['loss_target'], 'm_ln_mix_a': out['m_ln_mix_a'], 'm_w_in_a': out['m_w_in_a'], 'm_g_v_a': out['m_g_v_a'], 'm_w_spatial': out['m_w_spatial'], 'm_b_spatial': out['m_b_spatial'], 'm_w_out_a': out['m_w_out_a'], 'm_ln_kv': out['m_ln_kv'], 'm_w_kv': out['m_w_kv'], 'm_g_k': out['m_g_k'], 'm_ln_mix_b': out['m_ln_mix_b'], 'm_w_q': out['m_w_q'], 'm_g_q': out['m_g_q'], 'm_w_out_b': out['m_w_out_b'], 'm_ln_mlp': out['m_ln_mlp'], 'm_w_up': out['m_w_up'], 'm_w_down': out['m_w_down'], 'm_ln_ple': out['m_ln_ple'], 'm_w_ple_gate': out['m_w_ple_gate'], 'm_w_ple_proj': out['m_w_ple_proj'], 'v_ln_mix_a': out['v_ln_mix_a'], 'v_w_in_a': out['v_w_in_a'], 'v_g_v_a': out['v_g_v_a'], 'v_w_spatial': out['v_w_spatial'], 'v_b_spatial': out['v_b_spatial'], 'v_w_out_a': out['v_w_out_a'], 'v_ln_kv': out['v_ln_kv'], 'v_w_kv': out['v_w_kv'], 'v_g_k': out['v_g_k'], 'v_ln_mix_b': out['v_ln_mix_b'], 'v_w_q': out['v_w_q'], 'v_g_q': out['v_g_q'], 'v_w_out_b': out['v_w_out_b'], 'v_ln_mlp': out['v_ln_mlp'], 'v_w_up': out['v_w_up'], 'v_w_down': out['v_w_down'], 'v_ln_ple': out['v_ln_ple'], 'v_w_ple_gate': out['v_w_ple_gate'], 'v_w_ple_proj': out['v_w_ple_proj']}


def _loss(weights, diff, rest, loss_target):
    with _jax.named_scope("forward"):
        args = {**rest, TWIN_DIFF_INPUT: diff, **{k: w.astype(_WEIGHT_DTYPES[k]) for k, w in weights.items()}}
        y = _forward(args)
    with _jax.named_scope("loss_head"):
        err = _jnp.square(y.astype(_jnp.float32) - loss_target)
        return 0.5 * _jnp.sum(_jnp.mean(err, axis=-1)) if err.ndim else 0.5 * err


def _adamw(w, g, m, v):
    m = ADAM_B1 * m + (1.0 - ADAM_B1) * g
    v = ADAM_B2 * v + (1.0 - ADAM_B2) * _jnp.square(g)
    m_hat = m / (1.0 - ADAM_B1 ** ADAM_STEP)
    v_hat = v / (1.0 - ADAM_B2 ** ADAM_STEP)
    delta = -ADAM_LR * (m_hat / (_jnp.sqrt(v_hat) + ADAM_EPS) + ADAM_WD * w)
    return delta, m, v


def reference(x, p, ln_mix_a, w_in_a, g_v_a, w_spatial, b_spatial, w_out_a, ln_kv, w_kv, g_k, ln_mix_b, w_q, g_q, w_out_b, ln_mlp, w_up, w_down, ln_ple, w_ple_gate, w_ple_proj, loss_target, m_ln_mix_a, m_w_in_a, m_g_v_a, m_w_spatial, m_b_spatial, m_w_out_a, m_ln_kv, m_w_kv, m_g_k, m_ln_mix_b, m_w_q, m_g_q, m_w_out_b, m_ln_mlp, m_w_up, m_w_down, m_ln_ple, m_w_ple_gate, m_w_ple_proj, v_ln_mix_a, v_w_in_a, v_g_v_a, v_w_spatial, v_b_spatial, v_w_out_a, v_ln_kv, v_w_kv, v_g_k, v_ln_mix_b, v_w_q, v_g_q, v_w_out_b, v_ln_mlp, v_w_up, v_w_down, v_ln_ple, v_w_ple_gate, v_w_ple_proj):
    given = dict(x=x, p=p, ln_mix_a=ln_mix_a, w_in_a=w_in_a, g_v_a=g_v_a, w_spatial=w_spatial, b_spatial=b_spatial, w_out_a=w_out_a, ln_kv=ln_kv, w_kv=w_kv, g_k=g_k, ln_mix_b=ln_mix_b, w_q=w_q, g_q=g_q, w_out_b=w_out_b, ln_mlp=ln_mlp, w_up=w_up, w_down=w_down, ln_ple=ln_ple, w_ple_gate=w_ple_gate, w_ple_proj=w_ple_proj, loss_target=loss_target, m_ln_mix_a=m_ln_mix_a, m_w_in_a=m_w_in_a, m_g_v_a=m_g_v_a, m_w_spatial=m_w_spatial, m_b_spatial=m_b_spatial, m_w_out_a=m_w_out_a, m_ln_kv=m_ln_kv, m_w_kv=m_w_kv, m_g_k=m_g_k, m_ln_mix_b=m_ln_mix_b, m_w_q=m_w_q, m_g_q=m_g_q, m_w_out_b=m_w_out_b, m_ln_mlp=m_ln_mlp, m_w_up=m_w_up, m_w_down=m_w_down, m_ln_ple=m_ln_ple, m_w_ple_gate=m_w_ple_gate, m_w_ple_proj=m_w_ple_proj, v_ln_mix_a=v_ln_mix_a, v_w_in_a=v_w_in_a, v_g_v_a=v_g_v_a, v_w_spatial=v_w_spatial, v_b_spatial=v_b_spatial, v_w_out_a=v_w_out_a, v_ln_kv=v_ln_kv, v_w_kv=v_w_kv, v_g_k=v_g_k, v_ln_mix_b=v_ln_mix_b, v_w_q=v_w_q, v_g_q=v_g_q, v_w_out_b=v_w_out_b, v_ln_mlp=v_ln_mlp, v_w_up=v_w_up, v_w_down=v_w_down, v_ln_ple=v_ln_ple, v_w_ple_gate=v_w_ple_gate, v_w_ple_proj=v_w_ple_proj)
    weights = {n: given[n] for n in TWIN_WEIGHTS}
    shared = {n: given[n] for n in SHARED_INPUTS}
    per_example = {n: given[n] for n in ['x', 'p']}
    grad_fn = _jax.value_and_grad(_loss, argnums=(0, 1))

    def one_microbatch(ex, loss_target):
        ex = dict(ex)
        diff = ex.pop(TWIN_DIFF_INPUT)
        return grad_fn(weights, diff, {**shared, **ex}, loss_target)

    if N_MICROBATCH == 1:
        loss, (grad_w, grad_x) = one_microbatch(per_example, given["loss_target"])
    else:
        def body(carry, xs):
            loss_sum, grad_sum = carry
            l_k, (gw_k, gx_k) = one_microbatch(xs[0], xs[1])
            with _jax.named_scope("update"):
                return (loss_sum + l_k, _jax.tree.map(_jnp.add, grad_sum, gw_k)), gx_k

        init = (_jnp.zeros((), _jnp.float32), _jax.tree.map(_jnp.zeros_like, weights))
        (loss, grad_w), grad_x = _jax.lax.scan(body, init, (per_example, given["loss_target"]))
    with _jax.named_scope("update"):
        delta_w, new_m, new_v = {}, {}, {}
        for n in TWIN_WEIGHTS:
            delta_w[n], new_m[n], new_v[n] = _adamw(weights[n], grad_w[n], given["m_" + n], given["v_" + n])
    return (loss, grad_x, *[grad_w[n] for n in TWIN_WEIGHTS], *[delta_w[n] for n in TWIN_WEIGHTS],
            *[new_m[n] for n in TWIN_WEIGHTS], *[new_v[n] for n in TWIN_WEIGHTS])
```

```python
import functools
import math

import jax
import jax.numpy as jnp
from jax import lax
from jax.experimental import pallas as pl
from jax.experimental.pallas import tpu as pltpu

F32 = jnp.float32
MXU = jnp.bfloat16
COMM = jnp.bfloat16
EPS = 1e-6
NDEV = 8
HEAD_DIM = 64
CHUNK = 128
GROUPS = 8
QBLK = 128
SCALE = HEAD_DIM ** -0.5
TOKEN_TILE = 256
ADAM_LR = 0.001
ADAM_B1 = 0.9
ADAM_B2 = 0.999
ADAM_EPS = 1e-08
ADAM_WD = 0.01
ADAM_STEP = 10
MESH = pl.DeviceIdType.MESH


def _call(body, **kw):
    return pl.pallas_call(body, **kw)


def _params(vmem_mb, n_axes=1):
    return pltpu.CompilerParams(dimension_semantics=("arbitrary",) * n_axes,
                                vmem_limit_bytes=vmem_mb << 20)


def _tile(tm, n):
    return pl.BlockSpec((tm, n), lambda i: (i, 0))


def _whole(shape):
    zeros = (0,) * len(shape)
    return pl.BlockSpec(shape, lambda i: zeros, pipeline_mode=pl.Buffered(1))


def _acc(shape):
    zeros = (0,) * len(shape)
    return pl.BlockSpec(shape, lambda i: zeros)


def _mm(a, b):
    return jnp.dot(a.astype(MXU), b.astype(MXU), preferred_element_type=F32)


def _mm_nt(a, b):
    return lax.dot_general(a.astype(MXU), b.astype(MXU), (((1,), (1,)), ((), ())),
                           preferred_element_type=F32)


def _mm_tn(a, b):
    return lax.dot_general(a.astype(MXU), b.astype(MXU), (((0,), (0,)), ((), ())),
                           preferred_element_type=F32)


def _split_dot(x, ones, terms=2):
    out = None
    for _ in range(terms):
        part = x.astype(MXU)
        x = x - part.astype(F32)
        d = jnp.dot(part, ones, preferred_element_type=F32)
        out = d if out is None else out + d
    return out


def _rms(x, g):
    rstd = lax.rsqrt(jnp.mean(x * x, axis=-1, keepdims=True) + EPS)
    xhat = x * rstd
    return xhat * g, xhat, rstd


def _rms_bwd(dh, xhat, rstd, g):
    dxh = dh * g
    dx = rstd * (dxh - xhat * jnp.mean(dxh * xhat, axis=-1, keepdims=True))
    dg = jnp.sum(dh * xhat, axis=0, keepdims=True)
    return dx, dg


_GELU_C = math.sqrt(2.0 / math.pi)


def _gelu(x):
    t = jnp.tanh(_GELU_C * (x + 0.044715 * (x * x * x)))
    return 0.5 * x * (1.0 + t)


def _gelu_and_grad(x):
    x2 = x * x
    t = jnp.tanh(_GELU_C * (x + 0.044715 * (x2 * x)))
    g = 0.5 * x * (1.0 + t)
    dg = 0.5 * (1.0 + t) + 0.5 * x * (1.0 - t * t) * (_GELU_C * (1.0 + 3.0 * 0.044715 * x2))
    return g, dg


def _softplus(z):
    return jnp.maximum(z, 0.0) + jnp.log1p(jnp.exp(-jnp.abs(z)))


def _tril_mask():
    row = lax.broadcasted_iota(jnp.int32, (CHUNK, CHUNK), 0)
    col = lax.broadcasted_iota(jnp.int32, (CHUNK, CHUNK), 1)
    return row >= col, row <= col


def _exchange(arrs, scatter, name):
    n = len(arrs)
    outs_shape = [jax.ShapeDtypeStruct(a.shape if scatter else (NDEV,) + a.shape, a.dtype) for a in arrs]

    def body(*refs):
        ins, outs = refs[:n], refs[n:2 * n]
        send_sems, recv_sems, local_sems = refs[2 * n:]
        x, y, c = lax.axis_index("x"), lax.axis_index("y"), lax.axis_index("c")
        me = 4 * x + 2 * y + c
        peers = []
        for k in range(1, NDEV):
            px = 1 - x if (k >> 2) & 1 else x
            py = 1 - y if (k >> 1) & 1 else y
            pc = 1 - c if k & 1 else c
            peers.append(((px, py, pc), 4 * px + 2 * py + pc))

        def remote(a, k, block, to):
            src = ins[a].at[block] if scatter else ins[a]
            return pltpu.make_async_remote_copy(
                src_ref=src, dst_ref=outs[a].at[me if to is not None else block],
                send_sem=send_sems.at[a, k], recv_sem=recv_sems.at[a, k],
                device_id=to if to is not None else peers[k][0], device_id_type=MESH)

        local = []
        sends = []
        for a in range(n):
            cp = pltpu.make_async_copy(ins[a].at[me] if scatter else ins[a], outs[a].at[me], local_sems.at[a])
            cp.start()
            local.append(cp)
            for k, (dev, idx) in enumerate(peers):
                cp = remote(a, k, idx, dev)
                cp.start()
                sends.append(cp)
        for a in range(n):
            for k, (dev, idx) in enumerate(peers):
                remote(a, k, idx, None).wait_recv()
        for cp in sends:
            cp.wait_send()
        for cp in local:
            cp.wait()

    any_spec = pl.BlockSpec(memory_space=pl.ANY)
    return _call(
        body, name=name, out_shape=outs_shape,
        in_specs=[any_spec] * n, out_specs=[any_spec] * n,
        scratch_shapes=[pltpu.SemaphoreType.DMA((n, NDEV - 1)), pltpu.SemaphoreType.DMA((n, NDEV - 1)),
                        pltpu.SemaphoreType.DMA((n,))],
    )(*arrs)


def _spatial_mix(vnb, ws_ref, bT_ref, mix_ref, tm):
    tri, _ = _tril_mask()
    for g in range(GROUPS):
        wm = jnp.where(tri, ws_ref[g], 0.0).astype(MXU)
        cols = slice(g * CHUNK, (g + 1) * CHUNK)
        for ch in range(tm // CHUNK):
            rows = slice(ch * CHUNK, (ch + 1) * CHUNK)
            mix_ref[rows, cols] = _mm(wm, vnb[rows, cols]) + bT_ref[:, g:g + 1]


def _sgu_fwd(x, ln, w_in, g_v, ws, bT, w_out, name):
    T, D = x.shape
    tm = min(TOKEN_TILE, T)
    nw = w_in.shape[2]

    def body(x_ref, ln_ref, win_ref, gv_ref, ws_ref, bT_ref, wout_ref, xo_ref, z_ref, h_ref, y_ref, mix_ref):
        xv = x_ref[...]
        h, _, _ = _rms(xv, ln_ref[...])
        hb = h.astype(MXU)
        h_ref[...] = hb
        for j in range(NDEV):
            z_ref[:, j * nw:(j + 1) * nw] = _mm(hb, win_ref[j])
        u = _gelu(z_ref[:, :D])
        gv = _gelu(z_ref[:, D:])
        vn, _, _ = _rms(gv, gv_ref[...])
        _spatial_mix(vn.astype(MXU), ws_ref, bT_ref, mix_ref, tm)
        y = (u * mix_ref[...]).astype(MXU)
        y_ref[...] = y
        xo_ref[...] = xv + _mm(y, wout_ref[...])

    return _call(
        body, name=name, grid=(T // tm,),
        out_shape=[jax.ShapeDtypeStruct((T, D), F32), jax.ShapeDtypeStruct((T, 2 * D), F32),
                   jax.ShapeDtypeStruct((T, D), MXU), jax.ShapeDtypeStruct((T, D), MXU)],
        in_specs=[_tile(tm, D), _whole(ln.shape), _whole(w_in.shape), _whole(g_v.shape), _whole(ws.shape),
                  _whole(bT.shape), _whole(w_out.shape)],
        out_specs=[_tile(tm, D), _tile(tm, 2 * D), _tile(tm, D), _tile(tm, D)],
        scratch_shapes=[pltpu.VMEM((tm, D), F32)],
        compiler_params=_params(40),
    )(x, ln, w_in, g_v, ws, bT, w_out)


def _mlp_fwd(x, ln, w_up, w_down, name):
    T, D = x.shape
    tm = min(TOKEN_TILE, T)
    nf = w_up.shape[2]
    F = nf * NDEV

    def body(x_ref, ln_ref, wup_ref, wdown_ref, xo_ref, pre_ref, h_ref):
        xv = x_ref[...]
        h, _, _ = _rms(xv, ln_ref[...])
        hb = h.astype(MXU)
        h_ref[...] = hb
        for j in range(NDEV):
            pre_ref[:, j * nf:(j + 1) * nf] = _mm(hb, wup_ref[j])
        a = jnp.maximum(pre_ref[...], 0.0)
        xo_ref[...] = xv + _mm(a * a, wdown_ref[...])

    return _call(
        body, name=name, grid=(T // tm,),
        out_shape=[jax.ShapeDtypeStruct((T, D), F32), jax.ShapeDtypeStruct((T, F), F32),
                   jax.ShapeDtypeStruct((T, D), MXU)],
        in_specs=[_tile(tm, D), _whole(ln.shape), _whole(w_up.shape), _whole(w_down.shape)],
        out_specs=[_tile(tm, D), _tile(tm, F), _tile(tm, D)],
        compiler_params=_params(52),
    )(x, ln, w_up, w_down)


def _ple_fwd(x, p, ln, w_g, w_pp, name):
    T, D = x.shape
    tm = min(TOKEN_TILE, T)
    npp = w_pp.shape[2]

    def body(x_ref, p_ref, ln_ref, wg_ref, wpp_ref, xo_ref, gate_ref, pp_ref, h_ref):
        xv = x_ref[...]
        h, _, _ = _rms(xv, ln_ref[...])
        hb = h.astype(MXU)
        h_ref[...] = hb
        gate = jax.nn.sigmoid(_mm(hb, wg_ref[...]))
        gate_ref[...] = gate
        pb = p_ref[...].astype(MXU)
        for j in range(NDEV):
            pp_ref[:, j * npp:(j + 1) * npp] = _mm(pb, wpp_ref[j])
        xo_ref[...] = xv + pp_ref[...] * gate

    return _call(
        body, name=name, grid=(T // tm,),
        out_shape=[jax.ShapeDtypeStruct((T, D), F32), jax.ShapeDtypeStruct((T, D), F32),
                   jax.ShapeDtypeStruct((T, D), F32), jax.ShapeDtypeStruct((T, D), MXU)],
        in_specs=[_tile(tm, D), _tile(tm, p.shape[1]), _whole(ln.shape), _whole(w_g.shape), _whole(w_pp.shape)],
        out_specs=[_tile(tm, D), _tile(tm, D), _tile(tm, D), _tile(tm, D)],
        compiler_params=_params(32),
    )(x, p, ln, w_g, w_pp)


def _qkv_fwd(x, ln_q, ln_kv, w_q, w_kv, name):
    T, D = x.shape
    tm = min(TOKEN_TILE, T)
    nk = w_kv.shape[2]
    half = NDEV // 2

    def body(x_ref, lnq_ref, lnkv_ref, wq_ref, wkv_ref, q_ref, k_ref, v_ref, hq_ref, hkv_ref):
        xv = x_ref[...]
        _, xhat, _ = _rms(xv, lnq_ref[...])
        hq = (xhat * lnq_ref[...]).astype(MXU)
        hkv = (xhat * lnkv_ref[...]).astype(MXU)
        hq_ref[...] = hq
        hkv_ref[...] = hkv
        q_ref[...] = _mm(hq, wq_ref[...])
        for j in range(half):
            k_ref[:, j * nk:(j + 1) * nk] = _mm(hkv, wkv_ref[j])
            v_ref[:, j * nk:(j + 1) * nk] = _mm(hkv, wkv_ref[half + j]).astype(MXU)

    return _call(
        body, name=name, grid=(T // tm,),
        out_shape=[jax.ShapeDtypeStruct((T, D), F32), jax.ShapeDtypeStruct((T, D), F32),
                   jax.ShapeDtypeStruct((T, D), MXU), jax.ShapeDtypeStruct((T, D), MXU),
                   jax.ShapeDtypeStruct((T, D), MXU)],
        in_specs=[_tile(tm, D), _whole(ln_q.shape), _whole(ln_kv.shape), _whole(w_q.shape), _whole(w_kv.shape)],
        out_specs=[_tile(tm, D)] * 5,
        compiler_params=_params(40),
    )(x, ln_q, ln_kv, w_q, w_kv)


def _headnorm_fwd(xh, g, name):
    H, T, hd = xh.shape

    def body(x_ref, g_ref, o_ref):
        h, _, _ = _rms(x_ref[...], g_ref[...])
        o_ref[...] = h.astype(MXU)

    blk = pl.BlockSpec((None, T, hd), lambda h: (h, 0, 0))
    return _call(
        body, name=name, grid=(H,), out_shape=jax.ShapeDtypeStruct((H, T, hd), MXU),
        in_specs=[blk, _acc(g.shape)], out_specs=blk, compiler_params=_params(32),
    )(xh, g)


def _headnorm_bwd(dh, xh, g, name):
    H, T, hd = xh.shape

    def body(d_ref, x_ref, g_ref, o_ref, dg_ref):
        @pl.when(pl.program_id(0) == 0)
        def _():
            dg_ref[...] = jnp.zeros_like(dg_ref)
        _, xhat, rstd = _rms(x_ref[...], g_ref[...])
        dx, dg = _rms_bwd(d_ref[...], xhat, rstd, g_ref[...])
        o_ref[...] = dx.astype(MXU)
        dg_ref[...] += dg

    blk = pl.BlockSpec((None, T, hd), lambda h: (h, 0, 0))
    return _call(
        body, name=name, grid=(H,),
        out_shape=[jax.ShapeDtypeStruct((H, T, hd), MXU), jax.ShapeDtypeStruct(g.shape, F32)],
        in_specs=[blk, blk, _acc(g.shape)], out_specs=[blk, _acc(g.shape)], compiler_params=_params(32),
    )(dh, xh, g)


def _sb_masks():
    row = lax.broadcasted_iota(jnp.int32, (QBLK, QBLK), 0)
    col = lax.broadcasted_iota(jnp.int32, (QBLK, QBLK), 1)
    return row, col, (row > col).astype(MXU), (row >= col).astype(MXU)


def _sb_fwd(q, k, v, name):
    H, T, hd = q.shape

    def body(q_ref, k_ref, v_ref, o_ref):
        i = pl.program_id(1)
        qv = q_ref[...]
        row, col, later, _ = _sb_masks()

        def step(jj, carry):
            c_l, acc = carry
            j = i - jj
            off = pl.multiple_of(j * QBLK, QBLK)
            kj = k_ref[pl.ds(off, QBLK), :]
            vj = v_ref[pl.ds(off, QBLK), :]
            z = _mm_nt(qv, kj) * SCALE
            sp = _softplus(z)
            mask = (col + j * QBLK) < (row + i * QBLK)
            l = jnp.where(mask, -sp, 0.0)
            b = _split_dot(l, later) + c_l
            a = jnp.where(mask, jnp.exp(z - sp + b), 0.0)
            return c_l + jnp.sum(l, axis=1, keepdims=True), acc + _mm(a, vj)

        init = (jnp.zeros((QBLK, 1), F32), jnp.zeros((QBLK, hd), F32))
        _, acc = lax.fori_loop(0, i + 1, step, init)
        o_ref[...] = acc

    qblk = pl.BlockSpec((None, QBLK, hd), lambda h, i: (h, i, 0))
    kblk = pl.BlockSpec((None, T, hd), lambda h, i: (h, 0, 0))
    return _call(
        body, name=name, grid=(H, T // QBLK), out_shape=jax.ShapeDtypeStruct((H, T, hd), F32),
        in_specs=[qblk, kblk, kblk], out_specs=qblk, compiler_params=_params(32, 2),
    )(q, k, v)


def _sb_bwd(q, k, v, o, do, name):
    H, T, hd = q.shape

    def body(q_ref, k_ref, v_ref, o_ref, do_ref, dq_ref, dk_ref, dv_ref):
        i = pl.program_id(1)

        @pl.when(i == 0)
        def _():
            dk_ref[...] = jnp.zeros_like(dk_ref)
            dv_ref[...] = jnp.zeros_like(dv_ref)

        qv = q_ref[...]
        dov = do_ref[...]
        dob = dov.astype(MXU)
        total = jnp.sum(o_ref[...] * dob.astype(F32), axis=1, keepdims=True)
        row, col, later, later_eq = _sb_masks()

        def step(jj, carry):
            c_l, c_e, dq = carry
            j = i - jj
            off = pl.multiple_of(j * QBLK, QBLK)
            kj = k_ref[pl.ds(off, QBLK), :]
            vj = v_ref[pl.ds(off, QBLK), :]
            z = _mm_nt(qv, kj) * SCALE
            sp = _softplus(z)
            mask = (col + j * QBLK) < (row + i * QBLK)
            l = jnp.where(mask, -sp, 0.0)
            b = _split_dot(l, later) + c_l
            log_sig = z - sp
            ab = jnp.where(mask, jnp.exp(log_sig + b), 0.0).astype(MXU)
            e = ab.astype(F32) * _mm_nt(dob, vj)
            before = total - (_split_dot(e, later_eq, 3) + c_e)
            dz = jnp.where(mask, e * jnp.exp(-sp) - jnp.exp(log_sig) * before, 0.0) * SCALE
            dzb = dz.astype(MXU)
            dk_ref[pl.ds(off, QBLK), :] += _mm_tn(dzb, qv)
            dv_ref[pl.ds(off, QBLK), :] += _mm_tn(ab, dob)
            return (c_l + jnp.sum(l, axis=1, keepdims=True), c_e + jnp.sum(e, axis=1, keepdims=True),
                    dq + _mm(dzb, kj))

        init = (jnp.zeros((QBLK, 1), F32), jnp.zeros((QBLK, 1), F32), jnp.zeros((QBLK, hd), F32))
        _, _, dq = lax.fori_loop(0, i + 1, step, init)
        dq_ref[...] = dq

    qblk = pl.BlockSpec((None, QBLK, hd), lambda h, i: (h, i, 0))
    kblk = pl.BlockSpec((None, T, hd), lambda h, i: (h, 0, 0))
    full = jax.ShapeDtypeStruct((H, T, hd), F32)
    return _call(
        body, name=name, grid=(H, T // QBLK), out_shape=[full, full, full],
        in_specs=[qblk, kblk, kblk, qblk, qblk], out_specs=[qblk, kblk, kblk], compiler_params=_params(32, 2),
    )(q, k, v, o, do)


def _proj_res(x, a, w, name):
    T, D = x.shape
    tm = min(TOKEN_TILE, T)

    def body(x_ref, a_ref, w_ref, o_ref):
        o_ref[...] = x_ref[...] + _mm(a_ref[...], w_ref[...])

    return _call(
        body, name=name, grid=(T // tm,), out_shape=jax.ShapeDtypeStruct((T, D), F32),
        in_specs=[_tile(tm, D), _tile(tm, a.shape[1]), _whole(w.shape)], out_specs=_tile(tm, D),
        compiler_params=_params(32),
    )(x, a, w)


def _proj_nt(g, w, name):
    T = g.shape[0]
    K = w.shape[0]
    tm = min(TOKEN_TILE, T)

    def body(g_ref, w_ref, o_ref):
        o_ref[...] = _mm_nt(g_ref[...], w_ref[...])

    return _call(
        body, name=name, grid=(T // tm,), out_shape=jax.ShapeDtypeStruct((T, K), F32),
        in_specs=[_tile(tm, g.shape[1]), _whole(w.shape)], out_specs=_tile(tm, K),
        compiler_params=_params(32),
    )(g, w)


def _loss_grad(y, tgt, name):
    T, D = y.shape
    tm = min(TOKEN_TILE, T)

    def body(y_ref, t_ref, dy_ref, loss_ref):
        @pl.when(pl.program_id(0) == 0)
        def _():
            loss_ref[...] = jnp.zeros_like(loss_ref)
        diff = y_ref[...] - t_ref[...]
        dy_ref[...] = diff * (1.0 / D)
        rows = jnp.sum(diff * diff, axis=1, keepdims=True) * (1.0 / D)
        loss_ref[...] += 0.5 * jnp.sum(rows, axis=0, keepdims=True)

    return _call(
        body, name=name, grid=(T // tm,),
        out_shape=[jax.ShapeDtypeStruct((T, D), F32), jax.ShapeDtypeStruct((1, 1), F32)],
        in_specs=[_tile(tm, D), _tile(tm, D)], out_specs=[_tile(tm, D), _acc((1, 1))],
        compiler_params=_params(32),
    )(y, tgt)


def _ple_bwd(dx, x, gate, pp, ln, w_g, name):
    T, D = x.shape
    tm = min(TOKEN_TILE, T)

    def body(dx_ref, x_ref, gate_ref, pp_ref, ln_ref, wg_ref, dxo_ref, dpp_ref, dgp_ref, dln_ref):
        @pl.when(pl.program_id(0) == 0)
        def _():
            dln_ref[...] = jnp.zeros_like(dln_ref)
        dxv = dx_ref[...]
        gate = gate_ref[...]
        _, xhat, rstd = _rms(x_ref[...], ln_ref[...])
        dpp_ref[...] = (dxv * gate).astype(MXU)
        dgp = (dxv * pp_ref[...] * gate * (1.0 - gate)).astype(MXU)
        dgp_ref[...] = dgp
        dxn, dln = _rms_bwd(_mm_nt(dgp, wg_ref[...]), xhat, rstd, ln_ref[...])
        dln_ref[...] += dln
        dxo_ref[...] = dxn + dxv

    return _call(
        body, name=name, grid=(T // tm,),
        out_shape=[jax.ShapeDtypeStruct((T, D), F32), jax.ShapeDtypeStruct((T, D), MXU),
                   jax.ShapeDtypeStruct((T, D), MXU), jax.ShapeDtypeStruct(ln.shape, F32)],
        in_specs=[_tile(tm, D)] * 4 + [_whole(ln.shape), _whole(w_g.shape)],
        out_specs=[_tile(tm, D), _tile(tm, D), _tile(tm, D), _acc(ln.shape)],
        compiler_params=_params(32),
    )(dx, x, gate, pp, ln, w_g)


def _mlp_bwd(dx, x, pre, ln, w_up, w_down, name):
    T, D = x.shape
    tm = min(TOKEN_TILE, T)
    nf = w_up.shape[2]
    F = nf * NDEV

    def body(dx_ref, x_ref, pre_ref, ln_ref, wup_ref, wdown_ref, dxo_ref, dpre_ref, s_ref, dln_ref):
        @pl.when(pl.program_id(0) == 0)
        def _():
            dln_ref[...] = jnp.zeros_like(dln_ref)
        dxv = dx_ref[...]
        _, xhat, rstd = _rms(x_ref[...], ln_ref[...])
        a = jnp.maximum(pre_ref[...], 0.0)
        s_ref[...] = (a * a).astype(MXU)
        dpre_ref[...] = (_mm_nt(dxv, wdown_ref[...]) * (2.0 * a)).astype(MXU)
        dh = _mm_nt(dpre_ref[:, :nf], wup_ref[0])
        for j in range(1, NDEV):
            dh += _mm_nt(dpre_ref[:, j * nf:(j + 1) * nf], wup_ref[j])
        dxn, dln = _rms_bwd(dh, xhat, rstd, ln_ref[...])
        dln_ref[...] += dln
        dxo_ref[...] = dxn + dxv

    return _call(
        body, name=name, grid=(T // tm,),
        out_shape=[jax.ShapeDtypeStruct((T, D), F32), jax.ShapeDtypeStruct((T, F), MXU),
                   jax.ShapeDtypeStruct((T, F), MXU), jax.ShapeDtypeStruct(ln.shape, F32)],
        in_specs=[_tile(tm, D), _tile(tm, D), _tile(tm, F), _whole(ln.shape), _whole(w_up.shape),
                  _whole(w_down.shape)],
        out_specs=[_tile(tm, D), _tile(tm, F), _tile(tm, F), _acc(ln.shape)],
        compiler_params=_params(56),
    )(dx, x, pre, ln, w_up, w_down)


def _qkv_bwd(dx, x, dq, dkv, ln_q, ln_kv, w_q, w_kv, name):
    T, D = x.shape
    tm = min(TOKEN_TILE, T)
    nk = w_kv.shape[2]

    def body(dx_ref, x_ref, dq_ref, dkv_ref, lnq_ref, lnkv_ref, wq_ref, wkv_ref, dxo_ref, dlnq_ref, dlnkv_ref):
        @pl.when(pl.program_id(0) == 0)
        def _():
            dlnq_ref[...] = jnp.zeros_like(dlnq_ref)
            dlnkv_ref[...] = jnp.zeros_like(dlnkv_ref)
        _, xhat, rstd = _rms(x_ref[...], lnq_ref[...])
        dhq = _mm_nt(dq_ref[...], wq_ref[...])
        dhkv = _mm_nt(dkv_ref[:, :nk], wkv_ref[0])
        for j in range(1, NDEV):
            dhkv += _mm_nt(dkv_ref[:, j * nk:(j + 1) * nk], wkv_ref[j])
        dxq, dlnq = _rms_bwd(dhq, xhat, rstd, lnq_ref[...])
        dxkv, dlnkv = _rms_bwd(dhkv, xhat, rstd, lnkv_ref[...])
        dlnq_ref[...] += dlnq
        dlnkv_ref[...] += dlnkv
        dxo_ref[...] = dx_ref[...] + dxq + dxkv

    return _call(
        body, name=name, grid=(T // tm,),
        out_shape=[jax.ShapeDtypeStruct((T, D), F32), jax.ShapeDtypeStruct(ln_q.shape, F32),
                   jax.ShapeDtypeStruct(ln_kv.shape, F32)],
        in_specs=[_tile(tm, D), _tile(tm, D), _tile(tm, D), _tile(tm, 2 * D), _whole(ln_q.shape),
                  _whole(ln_kv.shape), _whole(w_q.shape), _whole(w_kv.shape)],
        out_specs=[_tile(tm, D), _acc(ln_q.shape), _acc(ln_kv.shape)],
        compiler_params=_params(40),
    )(dx, x, dq, dkv, ln_q, ln_kv, w_q, w_kv)


def _sgu_bwd(dx, x, z, ln, w_in, g_v, ws, wsT, bT, w_out, name):
    T, D = x.shape
    tm = min(TOKEN_TILE, T)
    nw = w_in.shape[2]

    def body(dx_ref, x_ref, z_ref, ln_ref, win_ref, gv_ref, ws_ref, wsT_ref, bT_ref, wout_ref,
             dxo_ref, dz_ref, dws_ref, dbT_ref, dln_ref, dgv_ref, mix_ref, dvn_ref):
        @pl.when(pl.program_id(0) == 0)
        def _():
            dws_ref[...] = jnp.zeros_like(dws_ref)
            dbT_ref[...] = jnp.zeros_like(dbT_ref)
            dln_ref[...] = jnp.zeros_like(dln_ref)
            dgv_ref[...] = jnp.zeros_like(dgv_ref)
        dxv = dx_ref[...]
        _, xhat, rstd = _rms(x_ref[...], ln_ref[...])
        u, du = _gelu_and_grad(z_ref[:, :D])
        gv, dgv = _gelu_and_grad(z_ref[:, D:])
        vn, vhat, rstd_v = _rms(gv, gv_ref[...])
        vnb = vn.astype(MXU)
        _spatial_mix(vnb, ws_ref, bT_ref, mix_ref, tm)
        dy = _mm_nt(dxv, wout_ref[...])
        d_u = dy * mix_ref[...]
        d_mix = dy * u
        dmb = d_mix.astype(MXU)
        tri, triT = _tril_mask()
        for g in range(GROUPS):
            wmT = jnp.where(triT, wsT_ref[g], 0.0).astype(MXU)
            cols = slice(g * CHUNK, (g + 1) * CHUNK)
            for ch in range(tm // CHUNK):
                rows = slice(ch * CHUNK, (ch + 1) * CHUNK)
                dm = dmb[rows, cols]
                dws_ref[g] += jnp.where(tri, _mm_nt(dm, vnb[rows, cols]), 0.0)
                dbT_ref[:, g:g + 1] += jnp.sum(d_mix[rows, cols], axis=1, keepdims=True)
                dvn_ref[rows, cols] = _mm(wmT, dm)
        d_gv, dg = _rms_bwd(dvn_ref[...], vhat, rstd_v, gv_ref[...])
        dgv_ref[...] += dg
        dz_ref[:, :D] = (d_u * du).astype(MXU)
        dz_ref[:, D:] = (d_gv * dgv).astype(MXU)
        dh = _mm_nt(dz_ref[:, :nw], win_ref[0])
        for j in range(1, NDEV):
            dh += _mm_nt(dz_ref[:, j * nw:(j + 1) * nw], win_ref[j])
        dxn, dln = _rms_bwd(dh, xhat, rstd, ln_ref[...])
        dln_ref[...] += dln
        dxo_ref[...] = dxn + dxv

    return _call(
        body, name=name, grid=(T // tm,),
        out_shape=[jax.ShapeDtypeStruct((T, D), F32), jax.ShapeDtypeStruct((T, 2 * D), MXU),
                   jax.ShapeDtypeStruct(ws.shape, F32), jax.ShapeDtypeStruct(bT.shape, F32),
                   jax.ShapeDtypeStruct(ln.shape, F32), jax.ShapeDtypeStruct(g_v.shape, F32)],
        in_specs=[_tile(tm, D), _tile(tm, D), _tile(tm, 2 * D), _whole(ln.shape), _whole(w_in.shape),
                  _whole(g_v.shape), _whole(ws.shape), _whole(wsT.shape), _whole(bT.shape), _whole(w_out.shape)],
        out_specs=[_tile(tm, D), _tile(tm, 2 * D), _acc(ws.shape), _acc(bT.shape), _acc(ln.shape),
                   _acc(g_v.shape)],
        scratch_shapes=[pltpu.VMEM((tm, D), F32), pltpu.VMEM((tm, D), F32)],
        compiler_params=_params(48),
    )(dx, x, z, ln, w_in, g_v, ws, wsT, bT, w_out)


def _wgrad_rows(a, g, name):
    T, K = a.shape
    N = g.shape[1]
    kb = K // NDEV

    def body(a_ref, g_ref, o_ref):
        o_ref[...] = _mm_tn(a_ref[...], g_ref[...]).astype(COMM)

    return _call(
        body, name=name, grid=(NDEV,), out_shape=jax.ShapeDtypeStruct((K, N), COMM),
        in_specs=[pl.BlockSpec((T, kb), lambda j: (0, j)), _whole(g.shape)],
        out_specs=pl.BlockSpec((kb, N), lambda j: (j, 0)),
        compiler_params=_params(40),
    )(a, g).reshape(NDEV, kb, N)


def _wgrad_cols(a, g, name):
    T, K = a.shape
    N = g.shape[1]
    nb = N // NDEV

    def body(a_ref, g_ref, o_ref):
        o_ref[...] = _mm_tn(a_ref[...], g_ref[...]).astype(COMM)

    return _call(
        body, name=name, grid=(NDEV,), out_shape=jax.ShapeDtypeStruct((NDEV, K, nb), COMM),
        in_specs=[_whole(a.shape), pl.BlockSpec((T, nb), lambda j: (0, j))],
        out_specs=pl.BlockSpec((None, K, nb), lambda j: (j, 0, 0)),
        compiler_params=_params(40),
    )(a, g)


def _adamw(w, m, v, slots, name):
    R, C = w.shape
    n = slots.shape[0]
    tr = math.gcd(R, max(8, (128 * 1024) // C))
    if tr < 64:
        tr = R
    bc1 = 1.0 - ADAM_B1 ** ADAM_STEP
    bc2 = 1.0 - ADAM_B2 ** ADAM_STEP

    def body(w_ref, m_ref, v_ref, s_ref, g_ref, d_ref, mo_ref, vo_ref):
        g = s_ref[0].astype(F32)
        for j in range(1, n):
            g = g + s_ref[j].astype(F32)
        mn = ADAM_B1 * m_ref[...] + (1.0 - ADAM_B1) * g
        vn = ADAM_B2 * v_ref[...] + (1.0 - ADAM_B2) * (g * g)
        g_ref[...] = g
        mo_ref[...] = mn
        vo_ref[...] = vn
        d_ref[...] = -ADAM_LR * ((mn / bc1) / (jnp.sqrt(vn / bc2) + ADAM_EPS) + ADAM_WD * w_ref[...])

    blk = pl.BlockSpec((tr, C), lambda i: (i, 0))
    out = jax.ShapeDtypeStruct((R, C), F32)
    return _call(
        body, name=name, grid=(R // tr,), out_shape=[out, out, out, out],
        in_specs=[blk, blk, blk, pl.BlockSpec((n, tr, C), lambda i: (0, i, 0))], out_specs=[blk] * 4,
        compiler_params=_params(32),
    )(w, m, v, slots)


def _to_heads(a):
    T, D = a.shape
    return a.reshape(T, D // HEAD_DIM, HEAD_DIM).transpose(1, 0, 2)


def _from_heads(a):
    H, T, hd = a.shape
    return a.transpose(1, 0, 2).reshape(T, H * hd)


def _rows128(a):
    flat = a.reshape(-1)
    rows = -(-flat.shape[0] // 1024) * 8
    flat = jnp.pad(flat, (0, rows * 128 - flat.shape[0]))
    return flat.reshape(rows, 128)


def kernel(x, p, ln_mix_a, w_in_a, g_v_a, w_spatial, b_spatial, w_out_a, ln_kv, w_kv, g_k, ln_mix_b, w_q, g_q, w_out_b, ln_mlp, w_up, w_down, ln_ple, w_ple_gate, w_ple_proj, loss_target, m_ln_mix_a, m_w_in_a, m_g_v_a, m_w_spatial, m_b_spatial, m_w_out_a, m_ln_kv, m_w_kv, m_g_k, m_ln_mix_b, m_w_q, m_g_q, m_w_out_b, m_ln_mlp, m_w_up, m_w_down, m_ln_ple, m_w_ple_gate, m_w_ple_proj, v_ln_mix_a, v_w_in_a, v_g_v_a, v_w_spatial, v_b_spatial, v_w_out_a, v_ln_kv, v_w_kv, v_g_k, v_ln_mix_b, v_w_q, v_g_q, v_w_out_b, v_ln_mlp, v_w_up, v_w_down, v_ln_ple, v_w_ple_gate, v_w_ple_proj):
    me = 4 * lax.axis_index("x") + 2 * lax.axis_index("y") + lax.axis_index("c")
    D = x.shape[2]
    x0, tgt = x[0], loss_target[0]
    n_layers = w_up.shape[0]

    big = [w_in_a[0], w_out_a[0], w_kv, w_q[0], w_out_b[0]]
    for l in range(n_layers):
        big += [w_up[l], w_down[l], w_ple_gate[l], w_ple_proj[l]]
    gathered = _exchange([w.astype(COMM) for w in big] + [ln_mix_a, g_v_a], scatter=False, name="gather_weights")
    W_in, W_out_a, W_kv, W_q, W_out_b = gathered[:5]
    W_out_a, W_q, W_out_b = (w.reshape(-1, D) for w in (W_out_a, W_q, W_out_b))
    W_up = [gathered[5 + 4 * l] for l in range(n_layers)]
    W_down = [gathered[6 + 4 * l].reshape(-1, D) for l in range(n_layers)]
    W_g = [gathered[7 + 4 * l].reshape(-1, D) for l in range(n_layers)]
    W_pp = [gathered[8 + 4 * l] for l in range(n_layers)]
    ln_a = gathered[-2].reshape(1, D)
    gv_a = gathered[-1].reshape(1, D)
    ws = w_spatial[0]
    wsT = jnp.swapaxes(ws, 1, 2)
    bT = b_spatial[0].T
    ln_kv2, ln_b, gk2, gq2 = ln_kv.reshape(1, D), ln_mix_b, g_k.reshape(1, HEAD_DIM), g_q
    ln_m = [ln_mlp[l:l + 1] for l in range(n_layers)]
    ln_p = [ln_ple[l:l + 1] for l in range(n_layers)]

    x1, z, h_a, y_a = _sgu_fwd(x0, ln_a, W_in, gv_a, ws, bT, W_out_a, "sgu_fwd")
    x2, pre0, hm0 = _mlp_fwd(x1, ln_m[0], W_up[0], W_down[0], "mlp_fwd0")
    x3, gate0, pp0, hp0 = _ple_fwd(x2, p[0, 0], ln_p[0], W_g[0], W_pp[0], "ple_fwd0")
    q_pre, k_pre, v2d, h_q, h_kv = _qkv_fwd(x3, ln_b, ln_kv2, W_q, W_kv, "qkv_fwd")
    q_pre_h, k_pre_h = _to_heads(q_pre), _to_heads(k_pre)
    qh = _headnorm_fwd(q_pre_h, gq2, "qnorm_fwd")
    kh = _headnorm_fwd(k_pre_h, gk2, "knorm_fwd")
    vh = _to_heads(v2d)
    oh = _sb_fwd(qh, kh, vh, "sb_fwd")
    o2d = _from_heads(oh).astype(MXU)
    x4 = _proj_res(x3, o2d, W_out_b, "attn_out")
    x5, pre1, hm1 = _mlp_fwd(x4, ln_m[1], W_up[1], W_down[1], "mlp_fwd1")
    x6, gate1, pp1, hp1 = _ple_fwd(x5, p[1, 0], ln_p[1], W_g[1], W_pp[1], "ple_fwd1")
    dy, loss_part = _loss_grad(x6, tgt, "loss_grad")
    loss = lax.psum(loss_part[0, 0], ("x", "y", "c"))

    dx5, dpp1, dgp1, dlnp1 = _ple_bwd(dy, x5, gate1, pp1, ln_p[1], W_g[1], "ple_bwd1")
    dx4, dpre1, s1, dlnm1 = _mlp_bwd(dx5, x4, pre1, ln_m[1], W_up[1], W_down[1], "mlp_bwd1")
    doh = _to_heads(_proj_nt(dx4, W_out_b, "attn_out_bwd"))
    dqh, dkh, dvh = _sb_bwd(qh, kh, vh, oh, doh, "sb_bwd")
    dq_pre_h, dgq = _headnorm_bwd(dqh, q_pre_h, gq2, "qnorm_bwd")
    dk_pre_h, dgk = _headnorm_bwd(dkh, k_pre_h, gk2, "knorm_bwd")
    dq_pre = _from_heads(dq_pre_h)
    dkv = jnp.concatenate([_from_heads(dk_pre_h), _from_heads(dvh).astype(MXU)], axis=1)
    dx3, dlnb, dlnkv = _qkv_bwd(dx4, x3, dq_pre, dkv, ln_b, ln_kv2, W_q, W_kv, "qkv_bwd")
    dx2, dpp0, dgp0, dlnp0 = _ple_bwd(dx3, x2, gate0, pp0, ln_p[0], W_g[0], "ple_bwd0")
    dx1, dpre0, s0, dlnm0 = _mlp_bwd(dx2, x1, pre0, ln_m[0], W_up[0], W_down[0], "mlp_bwd0")
    dx0, dz, dws, dbT, dlna, dgva = _sgu_bwd(dx1, x0, z, ln_a, W_in, gv_a, ws, wsT, bT, W_out_a, "sgu_bwd")

    wg = [_wgrad_cols(h_a, dz, "wg_in_a"), _wgrad_rows(y_a, dx1, "wg_out_a"), _wgrad_cols(h_kv, dkv, "wg_kv"),
          _wgrad_rows(h_q, dq_pre, "wg_q"), _wgrad_rows(o2d, dx4, "wg_out_b")]
    per_layer = [(hm0, dpre0, s0, dx2, hp0, dgp0, p[0, 0], dpp0), (hm1, dpre1, s1, dx5, hp1, dgp1, p[1, 0], dpp1)]
    for l, (hm, dpre, s, dxo, hp, dgp, pl_, dpp) in enumerate(per_layer):
        wg += [_wgrad_cols(hm, dpre, f"wg_up{l}"), _wgrad_rows(s, dxo, f"wg_down{l}"),
               _wgrad_rows(hp, dgp, f"wg_gate{l}"), _wgrad_cols(pl_.astype(MXU), dpp, f"wg_proj{l}")]
    slots = _exchange(wg, scatter=True, name="scatter_grads")

    def upd(w, m, v, s, name):
        shape = w.shape
        outs = _adamw(w.reshape(-1, shape[-1]), m.reshape(-1, shape[-1]), v.reshape(-1, shape[-1]), s, name)
        return [o.reshape(shape) for o in outs]

    res = {}
    res["w_in_a"] = upd(w_in_a, m_w_in_a, v_w_in_a, slots[0], "adam_in_a")
    res["w_out_a"] = upd(w_out_a, m_w_out_a, v_w_out_a, slots[1], "adam_out_a")
    res["w_kv"] = upd(w_kv, m_w_kv, v_w_kv, slots[2], "adam_kv")
    res["w_q"] = upd(w_q, m_w_q, v_w_q, slots[3], "adam_q")
    res["w_out_b"] = upd(w_out_b, m_w_out_b, v_w_out_b, slots[4], "adam_out_b")
    layered = [("w_up", w_up, m_w_up, v_w_up), ("w_down", w_down, m_w_down, v_w_down),
               ("w_ple_gate", w_ple_gate, m_w_ple_gate, v_w_ple_gate),
               ("w_ple_proj", w_ple_proj, m_w_ple_proj, v_w_ple_proj)]
    for i, (nm, w, m, v) in enumerate(layered):
        per = [upd(w[l], m[l], v[l], slots[5 + 4 * l + i], f"adam_{nm}{l}") for l in range(n_layers)]
        res[nm] = [jnp.stack([per[l][t] for l in range(n_layers)]) for t in range(4)]

    small = [("w_spatial", dws[None], w_spatial, m_w_spatial, v_w_spatial),
             ("b_spatial", dbT.T[None], b_spatial, m_b_spatial, v_b_spatial),
             ("ln_kv", dlnkv.reshape(-1), ln_kv, m_ln_kv, v_ln_kv),
             ("g_k", dgk.reshape(-1), g_k, m_g_k, v_g_k),
             ("ln_mix_b", dlnb, ln_mix_b, m_ln_mix_b, v_ln_mix_b),
             ("g_q", dgq, g_q, m_g_q, v_g_q),
             ("ln_mlp", jnp.concatenate([dlnm0, dlnm1]), ln_mlp, m_ln_mlp, v_ln_mlp),
             ("ln_ple", jnp.concatenate([dlnp0, dlnp1]), ln_ple, m_ln_ple, v_ln_ple)]
    sharded_vec = [("ln_mix_a", dlna, ln_mix_a, m_ln_mix_a, v_ln_mix_a),
                   ("g_v_a", dgva, g_v_a, m_g_v_a, v_g_v_a)]
    packs = [[], [], [], []]
    for _, g, w, m, v in small:
        for lst, a in zip(packs, (g, w, m, v)):
            lst.append(_rows128(a))
    for _, g, w, m, v in sharded_vec:
        packs[0].append(g.reshape(NDEV, -1))
        for lst, a in zip(packs[1:], (w, m, v)):
            lst.append(jnp.broadcast_to(a, (NDEV, a.shape[1])))
    g_pack, w_pack, m_pack, v_pack = (jnp.concatenate(lst) for lst in packs)
    (g_all,) = _exchange([g_pack], scatter=False, name="gather_small_grads")
    outs = _adamw(w_pack, m_pack, v_pack, g_all, "adam_small")
    row = 0
    for nm, g, w, m, v in small:
        nrows = _rows128(w).shape[0]
        res[nm] = [o[row:row + nrows].reshape(-1)[:w.size].reshape(w.shape) for o in outs]
        row += nrows
    for nm, g, w, m, v in sharded_vec:
        res[nm] = [lax.dynamic_slice_in_dim(o[row:row + NDEV], me, 1, axis=0) for o in outs]
        row += NDEV

    names = ["ln_mix_a", "w_in_a", "g_v_a", "w_spatial", "b_spatial", "w_out_a", "ln_kv", "w_kv", "g_k", "ln_mix_b",
             "w_q", "g_q", "w_out_b", "ln_mlp", "w_up", "w_down", "ln_ple", "w_ple_gate", "w_ple_proj"]
    out = [loss, dx0[None]]
    for t in range(4):
        out += [res[nm][t] for nm in names]
    return tuple(out)
```

```python
import functools
import math

import jax
import jax.numpy as jnp
from jax import lax
from jax.experimental import pallas as pl
from jax.experimental.pallas import tpu as pltpu

F32 = jnp.float32
MXU = jnp.bfloat16
COMM = jnp.bfloat16
EPS = 1e-6
NDEV = 8
HEAD_DIM = 64
CHUNK = 128
GROUPS = 8
QBLK = 128
SCALE = HEAD_DIM ** -0.5
TOKEN_TILE = 256
ADAM_LR = 0.001
ADAM_B1 = 0.9
ADAM_B2 = 0.999
ADAM_EPS = 1e-08
ADAM_WD = 0.01
ADAM_STEP = 10
MESH = pl.DeviceIdType.MESH


def _call(body, **kw):
    return pl.pallas_call(body, **kw)


def _params(vmem_mb, n_axes=1):
    return pltpu.CompilerParams(dimension_semantics=("arbitrary",) * n_axes,
                                vmem_limit_bytes=vmem_mb << 20)


def _tile(tm, n):
    return pl.BlockSpec((tm, n), lambda i: (i, 0))


def _whole(shape):
    zeros = (0,) * len(shape)
    return pl.BlockSpec(shape, lambda i: zeros, pipeline_mode=pl.Buffered(1))


def _acc(shape):
    zeros = (0,) * len(shape)
    return pl.BlockSpec(shape, lambda i: zeros)


def _mm(a, b):
    return jnp.dot(a.astype(MXU), b.astype(MXU), preferred_element_type=F32)


def _mm_nt(a, b):
    return lax.dot_general(a.astype(MXU), b.astype(MXU), (((1,), (1,)), ((), ())),
                           preferred_element_type=F32)


def _mm_tn(a, b):
    return lax.dot_general(a.astype(MXU), b.astype(MXU), (((0,), (0,)), ((), ())),
                           preferred_element_type=F32)


def _split_dot(x, ones, terms=2):
    out = None
    for _ in range(terms):
        part = x.astype(MXU)
        x = x - part.astype(F32)
        d = jnp.dot(part, ones, preferred_element_type=F32)
        out = d if out is None else out + d
    return out


def _rms(x, g):
    rstd = lax.rsqrt(jnp.mean(x * x, axis=-1, keepdims=True) + EPS)
    xhat = x * rstd
    return xhat * g, xhat, rstd


def _rms_bwd(dh, xhat, rstd, g):
    dxh = dh * g
    dx = rstd * (dxh - xhat * jnp.mean(dxh * xhat, axis=-1, keepdims=True))
    dg = jnp.sum(dh * xhat, axis=0, keepdims=True)
    return dx, dg


_GELU_C = math.sqrt(2.0 / math.pi)


def _gelu(x):
    t = jnp.tanh(_GELU_C * (x + 0.044715 * (x * x * x)))
    return 0.5 * x * (1.0 + t)


def _gelu_and_grad(x):
    x2 = x * x
    t = jnp.tanh(_GELU_C * (x + 0.044715 * (x2 * x)))
    g = 0.5 * x * (1.0 + t)
    dg = 0.5 * (1.0 + t) + 0.5 * x * (1.0 - t * t) * (_GELU_C * (1.0 + 3.0 * 0.044715 * x2))
    return g, dg


def _softplus(z):
    return jnp.maximum(z, 0.0) + jnp.log1p(jnp.exp(-jnp.abs(z)))


def _tril_mask():
    row = lax.broadcasted_iota(jnp.int32, (CHUNK, CHUNK), 0)
    col = lax.broadcasted_iota(jnp.int32, (CHUNK, CHUNK), 1)
    return row >= col, row <= col


def _exchange(arrs, scatter, name):
    n = len(arrs)
    outs_shape = [jax.ShapeDtypeStruct(a.shape if scatter else (NDEV,) + a.shape, a.dtype) for a in arrs]

    def body(*refs):
        ins, outs = refs[:n], refs[n:2 * n]
        send_sems, recv_sems, local_sems = refs[2 * n:]
        x, y, c = lax.axis_index("x"), lax.axis_index("y"), lax.axis_index("c")
        me = 4 * x + 2 * y + c
        peers = []
        for k in range(1, NDEV):
            px = 1 - x if (k >> 2) & 1 else x
            py = 1 - y if (k >> 1) & 1 else y
            pc = 1 - c if k & 1 else c
            peers.append(((px, py, pc), 4 * px + 2 * py + pc))

        def remote(a, k, block, to):
            src = ins[a].at[block] if scatter else ins[a]
            return pltpu.make_async_remote_copy(
                src_ref=src, dst_ref=outs[a].at[me if to is not None else block],
                send_sem=send_sems.at[a, k], recv_sem=recv_sems.at[a, k],
                device_id=to if to is not None else peers[k][0], device_id_type=MESH)

        local = []
        sends = []
        for a in range(n):
            cp = pltpu.make_async_copy(ins[a].at[me] if scatter else ins[a], outs[a].at[me], local_sems.at[a])
            cp.start()
            local.append(cp)
            for k, (dev, idx) in enumerate(peers):
                cp = remote(a, k, idx, dev)
                cp.start()
                sends.append(cp)
        for a in range(n):
            for k, (dev, idx) in enumerate(peers):
                remote(a, k, idx, None).wait_recv()
        for cp in sends:
            cp.wait_send()
        for cp in local:
            cp.wait()

    any_spec = pl.BlockSpec(memory_space=pl.ANY)
    return _call(
        body, name=name, out_shape=outs_shape,
        in_specs=[any_spec] * n, out_specs=[any_spec] * n,
        scratch_shapes=[pltpu.SemaphoreType.DMA((n, NDEV - 1)), pltpu.SemaphoreType.DMA((n, NDEV - 1)),
                        pltpu.SemaphoreType.DMA((n,))],
    )(*arrs)


def _spatial_mix(vnb, ws_ref, bT_ref, mix_ref, tm):
    tri, _ = _tril_mask()
    for g in range(GROUPS):
        wm = jnp.where(tri, ws_ref[g], 0.0).astype(MXU)
        cols = slice(g * CHUNK, (g + 1) * CHUNK)
        for ch in range(tm // CHUNK):
            rows = slice(ch * CHUNK, (ch + 1) * CHUNK)
            mix_ref[rows, cols] = _mm(wm, vnb[rows, cols]) + bT_ref[:, g:g + 1]


def _sgu_fwd(x, ln, w_in, g_v, ws, bT, w_out, name):
    T, D = x.shape
    tm = min(TOKEN_TILE, T)
    nw = w_in.shape[2]

    def body(x_ref, ln_ref, win_ref, gv_ref, ws_ref, bT_ref, wout_ref, xo_ref, z_ref, h_ref, y_ref, mix_ref):
        xv = x_ref[...]
        h, _, _ = _rms(xv, ln_ref[...])
        hb = h.astype(MXU)
        h_ref[...] = hb
        for j in range(NDEV):
            z_ref[:, j * nw:(j + 1) * nw] = _mm(hb, win_ref[j])
        u = _gelu(z_ref[:, :D])
        gv = _gelu(z_ref[:, D:])
        vn, _, _ = _rms(gv, gv_ref[...])
        _spatial_mix(vn.astype(MXU), ws_ref, bT_ref, mix_ref, tm)
        y = (u * mix_ref[...]).astype(MXU)
        y_ref[...] = y
        xo_ref[...] = xv + _mm(y, wout_ref[...])

    return _call(
        body, name=name, grid=(T // tm,),
        out_shape=[jax.ShapeDtypeStruct((T, D), F32), jax.ShapeDtypeStruct((T, 2 * D), F32),
                   jax.ShapeDtypeStruct((T, D), MXU), jax.ShapeDtypeStruct((T, D), MXU)],
        in_specs=[_tile(tm, D), _whole(ln.shape), _whole(w_in.shape), _whole(g_v.shape), _whole(ws.shape),
                  _whole(bT.shape), _whole(w_out.shape)],
        out_specs=[_tile(tm, D), _tile(tm, 2 * D), _tile(tm, D), _tile(tm, D)],
        scratch_shapes=[pltpu.VMEM((tm, D), F32)],
        compiler_params=_params(40),
    )(x, ln, w_in, g_v, ws, bT, w_out)


def _mlp_fwd(x, ln, w_up, w_down, name):
    T, D = x.shape
    tm = min(TOKEN_TILE, T)
    nf = w_up.shape[2]
    F = nf * NDEV

    def body(x_ref, ln_ref, wup_ref, wdown_ref, xo_ref, pre_ref, h_ref):
        xv = x_ref[...]
        h, _, _ = _rms(xv, ln_ref[...])
        hb = h.astype(MXU)
        h_ref[...] = hb
        for j in range(NDEV):
            pre_ref[:, j * nf:(j + 1) * nf] = _mm(hb, wup_ref[j])
        a = jnp.maximum(pre_ref[...], 0.0)
        xo_ref[...] = xv + _mm(a * a, wdown_ref[...])

    return _call(
        body, name=name, grid=(T // tm,),
        out_shape=[jax.ShapeDtypeStruct((T, D), F32), jax.ShapeDtypeStruct((T, F), F32),
                   jax.ShapeDtypeStruct((T, D), MXU)],
        in_specs=[_tile(tm, D), _whole(ln.shape), _whole(w_up.shape), _whole(w_down.shape)],
        out_specs=[_tile(tm, D), _tile(tm, F), _tile(tm, D)],
        compiler_params=_params(52),
    )(x, ln, w_up, w_down)


def _ple_fwd(x, p, ln, w_g, w_pp, name):
    T, D = x.shape
    tm = min(TOKEN_TILE, T)
    npp = w_pp.shape[2]

    def body(x_ref, p_ref, ln_ref, wg_ref, wpp_ref, xo_ref, gate_ref, pp_ref, h_ref):
        xv = x_ref[...]
        h, _, _ = _rms(xv, ln_ref[...])
        hb = h.astype(MXU)
        h_ref[...] = hb
        gate = jax.nn.sigmoid(_mm(hb, wg_ref[...]))
        gate_ref[...] = gate
        pb = p_ref[...].astype(MXU)
        for j in range(NDEV):
            pp_ref[:, j * npp:(j + 1) * npp] = _mm(pb, wpp_ref[j])
        xo_ref[...] = xv + pp_ref[...] * gate

    return _call(
        body, name=name, grid=(T // tm,),
        out_shape=[jax.ShapeDtypeStruct((T, D), F32), jax.ShapeDtypeStruct((T, D), F32),
                   jax.ShapeDtypeStruct((T, D), F32), jax.ShapeDtypeStruct((T, D), MXU)],
        in_specs=[_tile(tm, D), _tile(tm, p.shape[1]), _whole(ln.shape), _whole(w_g.shape), _whole(w_pp.shape)],
        out_specs=[_tile(tm, D), _tile(tm, D), _tile(tm, D), _tile(tm, D)],
        compiler_params=_params(32),
    )(x, p, ln, w_g, w_pp)


def _head_ones():
    row = lax.broadcasted_iota(jnp.int32, (128, 128), 0)
    col = lax.broadcasted_iota(jnp.int32, (128, 128), 1)
    return (jnp.right_shift(row, 6) == jnp.right_shift(col, 6)).astype(MXU)


def _head_rms(x, g, ones):
    rstd = lax.rsqrt(_split_dot(x * x, ones, 3) * (1.0 / HEAD_DIM) + EPS)
    xhat = x * rstd
    return xhat * g, xhat, rstd


def _head_rms_bwd(dh, xhat, rstd, g, ones):
    dxh = dh * g
    mean = _split_dot(dxh * xhat, ones, 3) * (1.0 / HEAD_DIM)
    return rstd * (dxh - xhat * mean), jnp.sum(dh * xhat, axis=0, keepdims=True)


def _qkv_fwd(x, ln_q, ln_kv, g_q, g_k, w_q, w_kv, name):
    T, D = x.shape
    tm = min(TOKEN_TILE, T)
    nk = w_kv.shape[2]
    half = NDEV // 2

    def body(x_ref, lnq_ref, lnkv_ref, gq_ref, gk_ref, wq_ref, wkv_ref,
             q_ref, k_ref, v_ref, qpre_ref, kpre_ref, hq_ref, hkv_ref):
        xv = x_ref[...]
        _, xhat, _ = _rms(xv, lnq_ref[...])
        hq = (xhat * lnq_ref[...]).astype(MXU)
        hkv = (xhat * lnkv_ref[...]).astype(MXU)
        hq_ref[...] = hq
        hkv_ref[...] = hkv
        qpre_ref[...] = _mm(hq, wq_ref[...])
        for j in range(half):
            kpre_ref[:, j * nk:(j + 1) * nk] = _mm(hkv, wkv_ref[j])
            v_ref[:, j * nk:(j + 1) * nk] = _mm(hkv, wkv_ref[half + j]).astype(MXU)
        ones = _head_ones()
        for b in range(D // 128):
            cols = slice(b * 128, (b + 1) * 128)
            qn, _, _ = _head_rms(qpre_ref[:, cols], gq_ref[:, cols], ones)
            q_ref[:, cols] = (qn * SCALE).astype(MXU)
            kn, _, _ = _head_rms(kpre_ref[:, cols], gk_ref[:, cols], ones)
            k_ref[:, cols] = kn.astype(MXU)

    return _call(
        body, name=name, grid=(T // tm,),
        out_shape=[jax.ShapeDtypeStruct((T, D), MXU)] * 3 + [jax.ShapeDtypeStruct((T, D), F32)] * 2
        + [jax.ShapeDtypeStruct((T, D), MXU)] * 2,
        in_specs=[_tile(tm, D), _whole(ln_q.shape), _whole(ln_kv.shape), _whole(g_q.shape), _whole(g_k.shape),
                  _whole(w_q.shape), _whole(w_kv.shape)],
        out_specs=[_tile(tm, D)] * 7,
        compiler_params=_params(40),
    )(x, ln_q, ln_kv, g_q, g_k, w_q, w_kv)


SB_KEYS = 2 * QBLK


def _sb_consts():
    row = lax.broadcasted_iota(jnp.int32, (QBLK, QBLK), 0)
    col = lax.broadcasted_iota(jnp.int32, (QBLK, QBLK), 1)
    lane = lax.broadcasted_iota(jnp.int32, (QBLK, 128), 1)
    return (row > col).astype(MXU), (row >= col).astype(MXU), lane < HEAD_DIM


def _sb_causal(i, off):
    row = lax.broadcasted_iota(jnp.int32, (QBLK, SB_KEYS), 0)
    col = lax.broadcasted_iota(jnp.int32, (QBLK, SB_KEYS), 1)
    return (col + off) < (row + i * QBLK)


def _sb_suffix(x, ones, terms):
    s = _split_dot(jnp.concatenate([x[:, :QBLK], x[:, QBLK:]], axis=0), ones, terms)
    return s[:QBLK], s[QBLK:]


def _sb_fwd(q, k, v, name):
    T, D = q.shape

    def body(q_ref, k_ref, v_ref, o_ref):
        i = pl.program_id(1)
        n_steps = (i + 2) // 2
        later, _, first = _sb_consts()
        qv = q_ref[...]
        zero = jnp.zeros_like(qv)
        qs = (jnp.where(first, qv, zero), jnp.where(first, zero, qv))

        def block(j, carry, masked):
            off = pl.multiple_of(j * SB_KEYS, SB_KEYS)
            kj = k_ref[pl.ds(off, SB_KEYS), :]
            vj = v_ref[pl.ds(off, SB_KEYS), :]
            mask = _sb_causal(i, off) if masked else None
            out = []
            for hh in range(2):
                c_l, acc = carry[hh]
                z = _mm_nt(qs[hh], kj)
                sp = _softplus(z)
                l = jnp.where(mask, -sp, 0.0) if masked else -sp
                s_lo, s_hi = _sb_suffix(l, later, 2)
                r_lo = s_lo[:, :1] + l[:, :1]
                r_hi = s_hi[:, :1] + l[:, QBLK:QBLK + 1]
                b = jnp.concatenate([s_lo + (c_l + r_hi), s_hi + c_l], axis=1)
                a = jnp.exp(z - sp + b)
                if masked:
                    a = jnp.where(mask, a, 0.0)
                out.append((c_l + (r_lo + r_hi), acc + _mm(a, vj)))
            return tuple(out)

        init = tuple((jnp.zeros((QBLK, 1), F32), jnp.zeros((QBLK, 128), F32)) for _ in range(2))
        carry = block(n_steps - 1, init, True)
        carry = lax.fori_loop(0, n_steps - 1, lambda jj, c: block(n_steps - 2 - jj, c, False), carry)
        o_ref[...] = jnp.where(first, carry[0][1], carry[1][1])

    qblk = pl.BlockSpec((QBLK, 128), lambda h, i: (i, h))
    kblk = pl.BlockSpec((T, 128), lambda h, i: (0, h))
    return _call(
        body, name=name, grid=(D // 128, T // QBLK), out_shape=jax.ShapeDtypeStruct((T, D), F32),
        in_specs=[qblk, kblk, kblk], out_specs=qblk, compiler_params=_params(32, 2),
    )(q, k, v)


def _sb_bwd(q, k, v, o, do, name):
    T, D = q.shape

    def body(q_ref, k_ref, v_ref, o_ref, do_ref, dq_ref, dk_ref, dv_ref):
        i = pl.program_id(1)

        @pl.when(i == 0)
        def _():
            dk_ref[...] = jnp.zeros_like(dk_ref)
            dv_ref[...] = jnp.zeros_like(dv_ref)

        n_steps = (i + 2) // 2
        later, later_eq, first = _sb_consts()
        qv = q_ref[...]
        dob = do_ref[...].astype(MXU)
        zero = jnp.zeros_like(qv)
        qs = (jnp.where(first, qv, zero), jnp.where(first, zero, qv))
        dos = (jnp.where(first, dob, zero), jnp.where(first, zero, dob))
        prod = o_ref[...] * dob.astype(F32)
        totals = (jnp.sum(jnp.where(first, prod, 0.0), axis=1, keepdims=True),
                  jnp.sum(jnp.where(first, 0.0, prod), axis=1, keepdims=True))

        def block(j, carry, masked):
            off = pl.multiple_of(j * SB_KEYS, SB_KEYS)
            kj = k_ref[pl.ds(off, SB_KEYS), :]
            vj = v_ref[pl.ds(off, SB_KEYS), :]
            mask = _sb_causal(i, off) if masked else None
            out = []
            dk_blk = dv_blk = None
            for hh in range(2):
                c_l, c_e, dq = carry[hh]
                z = _mm_nt(qs[hh], kj)
                sp = _softplus(z)
                l = jnp.where(mask, -sp, 0.0) if masked else -sp
                s_lo, s_hi = _sb_suffix(l, later, 2)
                r_lo = s_lo[:, :1] + l[:, :1]
                r_hi = s_hi[:, :1] + l[:, QBLK:QBLK + 1]
                b = jnp.concatenate([s_lo + (c_l + r_hi), s_hi + c_l], axis=1)
                log_sig = z - sp
                a = jnp.exp(log_sig + b)
                if masked:
                    a = jnp.where(mask, a, 0.0)
                ab = a.astype(MXU)
                e = ab.astype(F32) * _mm_nt(dos[hh], vj)
                t_lo, t_hi = _sb_suffix(e, later_eq, 3)
                e_lo, e_hi = t_lo[:, :1], t_hi[:, :1]
                before = totals[hh] - jnp.concatenate([t_lo + (c_e + e_hi), t_hi + c_e], axis=1)
                dz = e * jnp.exp(-sp) - jnp.exp(log_sig) * before
                if masked:
                    dz = jnp.where(mask, dz, 0.0)
                dzb = dz.astype(MXU)
                dk_h = _mm_tn(dzb, qs[hh])
                dv_h = _mm_tn(ab, dos[hh])
                dk_blk = dk_h if dk_blk is None else dk_blk + dk_h
                dv_blk = dv_h if dv_blk is None else dv_blk + dv_h
                out.append((c_l + (r_lo + r_hi), c_e + (e_lo + e_hi), dq + _mm(dzb, kj)))
            dk_ref[pl.ds(off, SB_KEYS), :] += dk_blk
            dv_ref[pl.ds(off, SB_KEYS), :] += dv_blk
            return tuple(out)

        init = tuple((jnp.zeros((QBLK, 1), F32), jnp.zeros((QBLK, 1), F32), jnp.zeros((QBLK, 128), F32))
                     for _ in range(2))
        carry = block(n_steps - 1, init, True)
        carry = lax.fori_loop(0, n_steps - 1, lambda jj, c: block(n_steps - 2 - jj, c, False), carry)
        dq_ref[...] = jnp.where(first, carry[0][2], carry[1][2]) * SCALE

    qblk = pl.BlockSpec((QBLK, 128), lambda h, i: (i, h))
    kblk = pl.BlockSpec((T, 128), lambda h, i: (0, h))
    full = jax.ShapeDtypeStruct((T, D), F32)
    return _call(
        body, name=name, grid=(D // 128, T // QBLK), out_shape=[full, full, full],
        in_specs=[qblk, kblk, kblk, qblk, qblk], out_specs=[qblk, kblk, kblk], compiler_params=_params(32, 2),
    )(q, k, v, o, do)


def _proj_res(x, a, w, name):
    T, D = x.shape
    tm = min(TOKEN_TILE, T)

    def body(x_ref, a_ref, w_ref, o_ref):
        o_ref[...] = x_ref[...] + _mm(a_ref[...], w_ref[...])

    return _call(
        body, name=name, grid=(T // tm,), out_shape=jax.ShapeDtypeStruct((T, D), F32),
        in_specs=[_tile(tm, D), _tile(tm, a.shape[1]), _whole(w.shape)], out_specs=_tile(tm, D),
        compiler_params=_params(32),
    )(x, a, w)


def _proj_nt(g, w, name):
    T = g.shape[0]
    K = w.shape[0]
    tm = min(TOKEN_TILE, T)

    def body(g_ref, w_ref, o_ref):
        o_ref[...] = _mm_nt(g_ref[...], w_ref[...])

    return _call(
        body, name=name, grid=(T // tm,), out_shape=jax.ShapeDtypeStruct((T, K), F32),
        in_specs=[_tile(tm, g.shape[1]), _whole(w.shape)], out_specs=_tile(tm, K),
        compiler_params=_params(32),
    )(g, w)


def _loss_grad(y, tgt, name):
    T, D = y.shape
    tm = min(TOKEN_TILE, T)

    def body(y_ref, t_ref, dy_ref, loss_ref):
        @pl.when(pl.program_id(0) == 0)
        def _():
            loss_ref[...] = jnp.zeros_like(loss_ref)
        diff = y_ref[...] - t_ref[...]
        dy_ref[...] = diff * (1.0 / D)
        rows = jnp.sum(diff * diff, axis=1, keepdims=True) * (1.0 / D)
        loss_ref[...] += 0.5 * jnp.sum(rows, axis=0, keepdims=True)

    return _call(
        body, name=name, grid=(T // tm,),
        out_shape=[jax.ShapeDtypeStruct((T, D), F32), jax.ShapeDtypeStruct((1, 1), F32)],
        in_specs=[_tile(tm, D), _tile(tm, D)], out_specs=[_tile(tm, D), _acc((1, 1))],
        compiler_params=_params(32),
    )(y, tgt)


def _ple_bwd(dx, x, gate, pp, ln, w_g, name):
    T, D = x.shape
    tm = min(TOKEN_TILE, T)

    def body(dx_ref, x_ref, gate_ref, pp_ref, ln_ref, wg_ref, dxo_ref, dpp_ref, dgp_ref, dln_ref):
        @pl.when(pl.program_id(0) == 0)
        def _():
            dln_ref[...] = jnp.zeros_like(dln_ref)
        dxv = dx_ref[...]
        gate = gate_ref[...]
        _, xhat, rstd = _rms(x_ref[...], ln_ref[...])
        dpp_ref[...] = (dxv * gate).astype(MXU)
        dgp = (dxv * pp_ref[...] * gate * (1.0 - gate)).astype(MXU)
        dgp_ref[...] = dgp
        dxn, dln = _rms_bwd(_mm_nt(dgp, wg_ref[...]), xhat, rstd, ln_ref[...])
        dln_ref[...] += dln
        dxo_ref[...] = dxn + dxv

    return _call(
        body, name=name, grid=(T // tm,),
        out_shape=[jax.ShapeDtypeStruct((T, D), F32), jax.ShapeDtypeStruct((T, D), MXU),
                   jax.ShapeDtypeStruct((T, D), MXU), jax.ShapeDtypeStruct(ln.shape, F32)],
        in_specs=[_tile(tm, D)] * 4 + [_whole(ln.shape), _whole(w_g.shape)],
        out_specs=[_tile(tm, D), _tile(tm, D), _tile(tm, D), _acc(ln.shape)],
        compiler_params=_params(32),
    )(dx, x, gate, pp, ln, w_g)


def _mlp_bwd(dx, x, pre, ln, w_up, w_down, name):
    T, D = x.shape
    tm = min(TOKEN_TILE, T)
    nf = w_up.shape[2]
    F = nf * NDEV

    def body(dx_ref, x_ref, pre_ref, ln_ref, wup_ref, wdown_ref, dxo_ref, dpre_ref, s_ref, dln_ref):
        @pl.when(pl.program_id(0) == 0)
        def _():
            dln_ref[...] = jnp.zeros_like(dln_ref)
        dxv = dx_ref[...]
        _, xhat, rstd = _rms(x_ref[...], ln_ref[...])
        a = jnp.maximum(pre_ref[...], 0.0)
        s_ref[...] = (a * a).astype(MXU)
        dpre_ref[...] = (_mm_nt(dxv, wdown_ref[...]) * (2.0 * a)).astype(MXU)
        dh = _mm_nt(dpre_ref[:, :nf], wup_ref[0])
        for j in range(1, NDEV):
            dh += _mm_nt(dpre_ref[:, j * nf:(j + 1) * nf], wup_ref[j])
        dxn, dln = _rms_bwd(dh, xhat, rstd, ln_ref[...])
        dln_ref[...] += dln
        dxo_ref[...] = dxn + dxv

    return _call(
        body, name=name, grid=(T // tm,),
        out_shape=[jax.ShapeDtypeStruct((T, D), F32), jax.ShapeDtypeStruct((T, F), MXU),
                   jax.ShapeDtypeStruct((T, F), MXU), jax.ShapeDtypeStruct(ln.shape, F32)],
        in_specs=[_tile(tm, D), _tile(tm, D), _tile(tm, F), _whole(ln.shape), _whole(w_up.shape),
                  _whole(w_down.shape)],
        out_specs=[_tile(tm, D), _tile(tm, F), _tile(tm, F), _acc(ln.shape)],
        compiler_params=_params(56),
    )(dx, x, pre, ln, w_up, w_down)


def _qkv_bwd(dx, x, dq, dk, dv, q_pre, k_pre, ln_q, ln_kv, g_q, g_k, w_q, w_kv, name):
    T, D = x.shape
    tm = min(TOKEN_TILE, T)
    nk = w_kv.shape[2]
    n_tiles = T // tm

    def body(dx_ref, x_ref, dq_ref, dk_ref, dv_ref, qpre_ref, kpre_ref, lnq_ref, lnkv_ref, gq_ref, gk_ref,
             wq_ref, wkv_ref, dxo_ref, dqp_ref, dkv_ref, dlnq_ref, dlnkv_ref, dgq_ref, dgk_ref, gq_acc, gk_acc):
        i = pl.program_id(0)

        @pl.when(i == 0)
        def _():
            dlnq_ref[...] = jnp.zeros_like(dlnq_ref)
            dlnkv_ref[...] = jnp.zeros_like(dlnkv_ref)
            gq_acc[...] = jnp.zeros_like(gq_acc)
            gk_acc[...] = jnp.zeros_like(gk_acc)

        ones = _head_ones()
        for b in range(D // 128):
            cols = slice(b * 128, (b + 1) * 128)
            _, xh, rs = _head_rms(qpre_ref[:, cols], gq_ref[:, cols], ones)
            d, dg = _head_rms_bwd(dq_ref[:, cols], xh, rs, gq_ref[:, cols], ones)
            dqp_ref[:, cols] = d.astype(MXU)
            gq_acc[:, cols] += dg
            _, xh, rs = _head_rms(kpre_ref[:, cols], gk_ref[:, cols], ones)
            d, dg = _head_rms_bwd(dk_ref[:, cols], xh, rs, gk_ref[:, cols], ones)
            dkv_ref[:, cols] = d.astype(MXU)
            gk_acc[:, cols] += dg
        dkv_ref[:, D:] = dv_ref[...].astype(MXU)

        _, xhat, rstd = _rms(x_ref[...], lnq_ref[...])
        dhq = _mm_nt(dqp_ref[...], wq_ref[...])
        dhkv = _mm_nt(dkv_ref[:, :nk], wkv_ref[0])
        for j in range(1, NDEV):
            dhkv += _mm_nt(dkv_ref[:, j * nk:(j + 1) * nk], wkv_ref[j])
        dxq, dlnq = _rms_bwd(dhq, xhat, rstd, lnq_ref[...])
        dxkv, dlnkv = _rms_bwd(dhkv, xhat, rstd, lnkv_ref[...])
        dlnq_ref[...] += dlnq
        dlnkv_ref[...] += dlnkv
        dxo_ref[...] = dx_ref[...] + dxq + dxkv

        @pl.when(i == n_tiles - 1)
        def _():
            row = lax.broadcasted_iota(jnp.int32, (D, 128), 0)
            col = lax.broadcasted_iota(jnp.int32, (D, 128), 1)
            fold = (jnp.bitwise_and(row, HEAD_DIM - 1) == col).astype(MXU)
            dgq_ref[...] = _split_dot(jnp.broadcast_to(gq_acc[...], (8, D)), fold, 3)
            dgk_ref[...] = _split_dot(jnp.broadcast_to(gk_acc[...], (8, D)), fold, 3)

    small = jax.ShapeDtypeStruct((8, 128), F32)
    return _call(
        body, name=name, grid=(n_tiles,),
        out_shape=[jax.ShapeDtypeStruct((T, D), F32), jax.ShapeDtypeStruct((T, D), MXU),
                   jax.ShapeDtypeStruct((T, 2 * D), MXU), jax.ShapeDtypeStruct(ln_q.shape, F32),
                   jax.ShapeDtypeStruct(ln_kv.shape, F32), small, small],
        in_specs=[_tile(tm, D)] * 7 + [_whole(ln_q.shape), _whole(ln_kv.shape), _whole(g_q.shape),
                                       _whole(g_k.shape), _whole(w_q.shape), _whole(w_kv.shape)],
        out_specs=[_tile(tm, D), _tile(tm, D), _tile(tm, 2 * D), _acc(ln_q.shape), _acc(ln_kv.shape),
                   _acc((8, 128)), _acc((8, 128))],
        scratch_shapes=[pltpu.VMEM((1, D), F32), pltpu.VMEM((1, D), F32)],
        compiler_params=_params(48),
    )(dx, x, dq, dk, dv, q_pre, k_pre, ln_q, ln_kv, g_q, g_k, w_q, w_kv)


def _sgu_bwd(dx, x, z, ln, w_in, g_v, ws, wsT, bT, w_out, name):
    T, D = x.shape
    tm = min(TOKEN_TILE, T)
    nw = w_in.shape[2]

    def body(dx_ref, x_ref, z_ref, ln_ref, win_ref, gv_ref, ws_ref, wsT_ref, bT_ref, wout_ref,
             dxo_ref, dz_ref, dws_ref, dbT_ref, dln_ref, dgv_ref, mix_ref, dvn_ref):
        @pl.when(pl.program_id(0) == 0)
        def _():
            dws_ref[...] = jnp.zeros_like(dws_ref)
            dbT_ref[...] = jnp.zeros_like(dbT_ref)
            dln_ref[...] = jnp.zeros_like(dln_ref)
            dgv_ref[...] = jnp.zeros_like(dgv_ref)
        dxv = dx_ref[...]
        _, xhat, rstd = _rms(x_ref[...], ln_ref[...])
        u, du = _gelu_and_grad(z_ref[:, :D])
        gv, dgv = _gelu_and_grad(z_ref[:, D:])
        vn, vhat, rstd_v = _rms(gv, gv_ref[...])
        vnb = vn.astype(MXU)
        _spatial_mix(vnb, ws_ref, bT_ref, mix_ref, tm)
        dy = _mm_nt(dxv, wout_ref[...])
        d_u = dy * mix_ref[...]
        d_mix = dy * u
        dmb = d_mix.astype(MXU)
        tri, triT = _tril_mask()
        for g in range(GROUPS):
            wmT = jnp.where(triT, wsT_ref[g], 0.0).astype(MXU)
            cols = slice(g * CHUNK, (g + 1) * CHUNK)
            for ch in range(tm // CHUNK):
                rows = slice(ch * CHUNK, (ch + 1) * CHUNK)
                dm = dmb[rows, cols]
                dws_ref[g] += jnp.where(tri, _mm_nt(dm, vnb[rows, cols]), 0.0)
                dbT_ref[:, g:g + 1] += jnp.sum(d_mix[rows, cols], axis=1, keepdims=True)
                dvn_ref[rows, cols] = _mm(wmT, dm)
        d_gv, dg = _rms_bwd(dvn_ref[...], vhat, rstd_v, gv_ref[...])
        dgv_ref[...] += dg
        dz_ref[:, :D] = (d_u * du).astype(MXU)
        dz_ref[:, D:] = (d_gv * dgv).astype(MXU)
        dh = _mm_nt(dz_ref[:, :nw], win_ref[0])
        for j in range(1, NDEV):
            dh += _mm_nt(dz_ref[:, j * nw:(j + 1) * nw], win_ref[j])
        dxn, dln = _rms_bwd(dh, xhat, rstd, ln_ref[...])
        dln_ref[...] += dln
        dxo_ref[...] = dxn + dxv

    return _call(
        body, name=name, grid=(T // tm,),
        out_shape=[jax.ShapeDtypeStruct((T, D), F32), jax.ShapeDtypeStruct((T, 2 * D), MXU),
                   jax.ShapeDtypeStruct(ws.shape, F32), jax.ShapeDtypeStruct(bT.shape, F32),
                   jax.ShapeDtypeStruct(ln.shape, F32), jax.ShapeDtypeStruct(g_v.shape, F32)],
        in_specs=[_tile(tm, D), _tile(tm, D), _tile(tm, 2 * D), _whole(ln.shape), _whole(w_in.shape),
                  _whole(g_v.shape), _whole(ws.shape), _whole(wsT.shape), _whole(bT.shape), _whole(w_out.shape)],
        out_specs=[_tile(tm, D), _tile(tm, 2 * D), _acc(ws.shape), _acc(bT.shape), _acc(ln.shape),
                   _acc(g_v.shape)],
        scratch_shapes=[pltpu.VMEM((tm, D), F32), pltpu.VMEM((tm, D), F32)],
        compiler_params=_params(48),
    )(dx, x, z, ln, w_in, g_v, ws, wsT, bT, w_out)


def _wgrad_rows(a, g, name):
    T, K = a.shape
    N = g.shape[1]
    kb = K // NDEV

    def body(a_ref, g_ref, o_ref):
        o_ref[...] = _mm_tn(a_ref[...], g_ref[...]).astype(COMM)

    return _call(
        body, name=name, grid=(NDEV,), out_shape=jax.ShapeDtypeStruct((K, N), COMM),
        in_specs=[pl.BlockSpec((T, kb), lambda j: (0, j)), _whole(g.shape)],
        out_specs=pl.BlockSpec((kb, N), lambda j: (j, 0)),
        compiler_params=_params(40),
    )(a, g).reshape(NDEV, kb, N)


def _wgrad_cols(a, g, name):
    T, K = a.shape
    N = g.shape[1]
    nb = N // NDEV

    def body(a_ref, g_ref, o_ref):
        o_ref[...] = _mm_tn(a_ref[...], g_ref[...]).astype(COMM)

    return _call(
        body, name=name, grid=(NDEV,), out_shape=jax.ShapeDtypeStruct((NDEV, K, nb), COMM),
        in_specs=[_whole(a.shape), pl.BlockSpec((T, nb), lambda j: (0, j))],
        out_specs=pl.BlockSpec((None, K, nb), lambda j: (j, 0, 0)),
        compiler_params=_params(40),
    )(a, g)


def _adamw(w, m, v, slots, name):
    R, C = w.shape
    n = slots.shape[0]
    tr = math.gcd(R, max(8, (128 * 1024) // C))
    if tr < 64:
        tr = R
    bc1 = 1.0 - ADAM_B1 ** ADAM_STEP
    bc2 = 1.0 - ADAM_B2 ** ADAM_STEP

    def body(w_ref, m_ref, v_ref, s_ref, g_ref, d_ref, mo_ref, vo_ref):
        g = s_ref[0].astype(F32)
        for j in range(1, n):
            g = g + s_ref[j].astype(F32)
        mn = ADAM_B1 * m_ref[...] + (1.0 - ADAM_B1) * g
        vn = ADAM_B2 * v_ref[...] + (1.0 - ADAM_B2) * (g * g)
        g_ref[...] = g
        mo_ref[...] = mn
        vo_ref[...] = vn
        d_ref[...] = -ADAM_LR * ((mn / bc1) / (jnp.sqrt(vn / bc2) + ADAM_EPS) + ADAM_WD * w_ref[...])

    blk = pl.BlockSpec((tr, C), lambda i: (i, 0))
    out = jax.ShapeDtypeStruct((R, C), F32)
    return _call(
        body, name=name, grid=(R // tr,), out_shape=[out, out, out, out],
        in_specs=[blk, blk, blk, pl.BlockSpec((n, tr, C), lambda i: (0, i, 0))], out_specs=[blk] * 4,
        compiler_params=_params(32),
    )(w, m, v, slots)


def _rows128(a):
    flat = a.reshape(-1)
    rows = -(-flat.shape[0] // 1024) * 8
    flat = jnp.pad(flat, (0, rows * 128 - flat.shape[0]))
    return flat.reshape(rows, 128)


def kernel(x, p, ln_mix_a, w_in_a, g_v_a, w_spatial, b_spatial, w_out_a, ln_kv, w_kv, g_k, ln_mix_b, w_q, g_q, w_out_b, ln_mlp, w_up, w_down, ln_ple, w_ple_gate, w_ple_proj, loss_target, m_ln_mix_a, m_w_in_a, m_g_v_a, m_w_spatial, m_b_spatial, m_w_out_a, m_ln_kv, m_w_kv, m_g_k, m_ln_mix_b, m_w_q, m_g_q, m_w_out_b, m_ln_mlp, m_w_up, m_w_down, m_ln_ple, m_w_ple_gate, m_w_ple_proj, v_ln_mix_a, v_w_in_a, v_g_v_a, v_w_spatial, v_b_spatial, v_w_out_a, v_ln_kv, v_w_kv, v_g_k, v_ln_mix_b, v_w_q, v_g_q, v_w_out_b, v_ln_mlp, v_w_up, v_w_down, v_ln_ple, v_w_ple_gate, v_w_ple_proj):
    me = 4 * lax.axis_index("x") + 2 * lax.axis_index("y") + lax.axis_index("c")
    D = x.shape[2]
    x0, tgt = x[0], loss_target[0]
    n_layers = w_up.shape[0]

    big = [w_in_a[0], w_out_a[0], w_kv, w_q[0], w_out_b[0]]
    for l in range(n_layers):
        big += [w_up[l], w_down[l], w_ple_gate[l], w_ple_proj[l]]
    gathered = _exchange([w.astype(COMM) for w in big] + [ln_mix_a, g_v_a], scatter=False, name="gather_weights")
    W_in, W_out_a, W_kv, W_q, W_out_b = gathered[:5]
    W_out_a, W_q, W_out_b = (w.reshape(-1, D) for w in (W_out_a, W_q, W_out_b))
    W_up = [gathered[5 + 4 * l] for l in range(n_layers)]
    W_down = [gathered[6 + 4 * l].reshape(-1, D) for l in range(n_layers)]
    W_g = [gathered[7 + 4 * l].reshape(-1, D) for l in range(n_layers)]
    W_pp = [gathered[8 + 4 * l] for l in range(n_layers)]
    ln_a = gathered[-2].reshape(1, D)
    gv_a = gathered[-1].reshape(1, D)
    ws = w_spatial[0]
    wsT = jnp.swapaxes(ws, 1, 2)
    bT = b_spatial[0].T
    ln_kv2, ln_b = ln_kv.reshape(1, D), ln_mix_b
    gk2 = jnp.tile(g_k.reshape(1, HEAD_DIM), (1, D // HEAD_DIM))
    gq2 = jnp.tile(g_q, (1, D // HEAD_DIM))
    ln_m = [ln_mlp[l:l + 1] for l in range(n_layers)]
    ln_p = [ln_ple[l:l + 1] for l in range(n_layers)]

    x1, z, h_a, y_a = _sgu_fwd(x0, ln_a, W_in, gv_a, ws, bT, W_out_a, "sgu_fwd")
    x2, pre0, hm0 = _mlp_fwd(x1, ln_m[0], W_up[0], W_down[0], "mlp_fwd0")
    x3, gate0, pp0, hp0 = _ple_fwd(x2, p[0, 0], ln_p[0], W_g[0], W_pp[0], "ple_fwd0")
    qn, kn, vn, q_pre, k_pre, h_q, h_kv = _qkv_fwd(x3, ln_b, ln_kv2, gq2, gk2, W_q, W_kv, "qkv_fwd")
    o2d = _sb_fwd(qn, kn, vn, "sb_fwd")
    x4 = _proj_res(x3, o2d, W_out_b, "attn_out")
    x5, pre1, hm1 = _mlp_fwd(x4, ln_m[1], W_up[1], W_down[1], "mlp_fwd1")
    x6, gate1, pp1, hp1 = _ple_fwd(x5, p[1, 0], ln_p[1], W_g[1], W_pp[1], "ple_fwd1")
    dy, loss_part = _loss_grad(x6, tgt, "loss_grad")
    loss = lax.psum(loss_part[0, 0], ("x", "y", "c"))

    dx5, dpp1, dgp1, dlnp1 = _ple_bwd(dy, x5, gate1, pp1, ln_p[1], W_g[1], "ple_bwd1")
    dx4, dpre1, s1, dlnm1 = _mlp_bwd(dx5, x4, pre1, ln_m[1], W_up[1], W_down[1], "mlp_bwd1")
    do2d = _proj_nt(dx4, W_out_b, "attn_out_bwd")
    dqn, dkn, dvn = _sb_bwd(qn, kn, vn, o2d, do2d, "sb_bwd")
    dx3, dq_pre, dkv, dlnb, dlnkv, dgq, dgk = _qkv_bwd(dx4, x3, dqn, dkn, dvn, q_pre, k_pre, ln_b, ln_kv2, gq2, gk2,
                                                      W_q, W_kv, "qkv_bwd")
    dgq, dgk = dgq[:1, :HEAD_DIM], dgk[:1, :HEAD_DIM]
    dx2, dpp0, dgp0, dlnp0 = _ple_bwd(dx3, x2, gate0, pp0, ln_p[0], W_g[0], "ple_bwd0")
    dx1, dpre0, s0, dlnm0 = _mlp_bwd(dx2, x1, pre0, ln_m[0], W_up[0], W_down[0], "mlp_bwd0")
    dx0, dz, dws, dbT, dlna, dgva = _sgu_bwd(dx1, x0, z, ln_a, W_in, gv_a, ws, wsT, bT, W_out_a, "sgu_bwd")

    wg = [_wgrad_cols(h_a, dz, "wg_in_a"), _wgrad_rows(y_a, dx1, "wg_out_a"), _wgrad_cols(h_kv, dkv, "wg_kv"),
          _wgrad_rows(h_q, dq_pre, "wg_q"), _wgrad_rows(o2d, dx4, "wg_out_b")]
    per_layer = [(hm0, dpre0, s0, dx2, hp0, dgp0, p[0, 0], dpp0), (hm1, dpre1, s1, dx5, hp1, dgp1, p[1, 0], dpp1)]
    for l, (hm, dpre, s, dxo, hp, dgp, pl_, dpp) in enumerate(per_layer):
        wg += [_wgrad_cols(hm, dpre, f"wg_up{l}"), _wgrad_rows(s, dxo, f"wg_down{l}"),
               _wgrad_rows(hp, dgp, f"wg_gate{l}"), _wgrad_cols(pl_.astype(MXU), dpp, f"wg_proj{l}")]
    slots = _exchange(wg, scatter=True, name="scatter_grads")

    def upd(w, m, v, s, name):
        shape = w.shape
        outs = _adamw(w.reshape(-1, shape[-1]), m.reshape(-1, shape[-1]), v.reshape(-1, shape[-1]), s, name)
        return [o.reshape(shape) for o in outs]

    res = {}
    res["w_in_a"] = upd(w_in_a, m_w_in_a, v_w_in_a, slots[0], "adam_in_a")
    res["w_out_a"] = upd(w_out_a, m_w_out_a, v_w_out_a, slots[1], "adam_out_a")
    res["w_kv"] = upd(w_kv, m_w_kv, v_w_kv, slots[2], "adam_kv")
    res["w_q"] = upd(w_q, m_w_q, v_w_q, slots[3], "adam_q")
    res["w_out_b"] = upd(w_out_b, m_w_out_b, v_w_out_b, slots[4], "adam_out_b")
    layered = [("w_up", w_up, m_w_up, v_w_up), ("w_down", w_down, m_w_down, v_w_down),
               ("w_ple_gate", w_ple_gate, m_w_ple_gate, v_w_ple_gate),
               ("w_ple_proj", w_ple_proj, m_w_ple_proj, v_w_ple_proj)]
    for i, (nm, w, m, v) in enumerate(layered):
        per = [upd(w[l], m[l], v[l], slots[5 + 4 * l + i], f"adam_{nm}{l}") for l in range(n_layers)]
        res[nm] = [jnp.stack([per[l][t] for l in range(n_layers)]) for t in range(4)]

    small = [("w_spatial", dws[None], w_spatial, m_w_spatial, v_w_spatial),
             ("b_spatial", dbT.T[None], b_spatial, m_b_spatial, v_b_spatial),
             ("ln_kv", dlnkv.reshape(-1), ln_kv, m_ln_kv, v_ln_kv),
             ("g_k", dgk.reshape(-1), g_k, m_g_k, v_g_k),
             ("ln_mix_b", dlnb, ln_mix_b, m_ln_mix_b, v_ln_mix_b),
             ("g_q", dgq, g_q, m_g_q, v_g_q),
             ("ln_mlp", jnp.concatenate([dlnm0, dlnm1]), ln_mlp, m_ln_mlp, v_ln_mlp),
             ("ln_ple", jnp.concatenate([dlnp0, dlnp1]), ln_ple, m_ln_ple, v_ln_ple)]
    sharded_vec = [("ln_mix_a", dlna, ln_mix_a, m_ln_mix_a, v_ln_mix_a),
                   ("g_v_a", dgva, g_v_a, m_g_v_a, v_g_v_a)]
    packs = [[], [], [], []]
    for _, g, w, m, v in small:
        for lst, a in zip(packs, (g, w, m, v)):
            lst.append(_rows128(a))
    for _, g, w, m, v in sharded_vec:
        packs[0].append(g.reshape(NDEV, -1))
        for lst, a in zip(packs[1:], (w, m, v)):
            lst.append(jnp.broadcast_to(a, (NDEV, a.shape[1])))
    g_pack, w_pack, m_pack, v_pack = (jnp.concatenate(lst) for lst in packs)
    (g_all,) = _exchange([g_pack], scatter=False, name="gather_small_grads")
    outs = _adamw(w_pack, m_pack, v_pack, g_all, "adam_small")
    row = 0
    for nm, g, w, m, v in small:
        nrows = _rows128(w).shape[0]
        res[nm] = [o[row:row + nrows].reshape(-1)[:w.size].reshape(w.shape) for o in outs]
        row += nrows
    for nm, g, w, m, v in sharded_vec:
        res[nm] = [lax.dynamic_slice_in_dim(o[row:row + NDEV], me, 1, axis=0) for o in outs]
        row += NDEV

    names = ["ln_mix_a", "w_in_a", "g_v_a", "w_spatial", "b_spatial", "w_out_a", "ln_kv", "w_kv", "g_k", "ln_mix_b",
             "w_q", "g_q", "w_out_b", "ln_mlp", "w_up", "w_down", "ln_ple", "w_ple_gate", "w_ple_proj"]
    out = [loss, dx0[None]]
    for t in range(4):
        out += [res[nm][t] for nm in names]
    return tuple(out)
```

```python
import functools
import math

import jax
import jax.numpy as jnp
from jax import lax
from jax.experimental import pallas as pl
from jax.experimental.pallas import tpu as pltpu

F32 = jnp.float32
MXU = jnp.bfloat16
COMM = jnp.bfloat16
EPS = 1e-6
NDEV = 8
HEAD_DIM = 64
CHUNK = 128
GROUPS = 8
QBLK = 128
SCALE = HEAD_DIM ** -0.5
TOKEN_TILE = 256
ADAM_LR = 0.001
ADAM_B1 = 0.9
ADAM_B2 = 0.999
ADAM_EPS = 1e-08
ADAM_WD = 0.01
ADAM_STEP = 10
MESH = pl.DeviceIdType.MESH


def _call(body, **kw):
    return pl.pallas_call(body, **kw)


def _params(vmem_mb, n_axes=1):
    return pltpu.CompilerParams(dimension_semantics=("arbitrary",) * n_axes,
                                vmem_limit_bytes=vmem_mb << 20)


def _tile(tm, n):
    return pl.BlockSpec((tm, n), lambda i: (i, 0))


def _whole(shape):
    zeros = (0,) * len(shape)
    return pl.BlockSpec(shape, lambda i: zeros, pipeline_mode=pl.Buffered(1))


def _acc(shape):
    zeros = (0,) * len(shape)
    return pl.BlockSpec(shape, lambda i: zeros)


def _mm(a, b):
    return jnp.dot(a.astype(MXU), b.astype(MXU), preferred_element_type=F32)


def _mm_nt(a, b):
    return lax.dot_general(a.astype(MXU), b.astype(MXU), (((1,), (1,)), ((), ())),
                           preferred_element_type=F32)


def _mm_tn(a, b):
    return lax.dot_general(a.astype(MXU), b.astype(MXU), (((0,), (0,)), ((), ())),
                           preferred_element_type=F32)


def _split_dot(x, ones, terms=2):
    out = None
    for _ in range(terms):
        part = x.astype(MXU)
        x = x - part.astype(F32)
        d = jnp.dot(part, ones, preferred_element_type=F32)
        out = d if out is None else out + d
    return out


def _rms(x, g):
    rstd = lax.rsqrt(jnp.mean(x * x, axis=-1, keepdims=True) + EPS)
    xhat = x * rstd
    return xhat * g, xhat, rstd


def _rms_bwd(dh, xhat, rstd, g):
    dxh = dh * g
    dx = rstd * (dxh - xhat * jnp.mean(dxh * xhat, axis=-1, keepdims=True))
    dg = jnp.sum(dh * xhat, axis=0, keepdims=True)
    return dx, dg


_GELU_C = math.sqrt(2.0 / math.pi)


def _gelu(x):
    t = jnp.tanh(_GELU_C * (x + 0.044715 * (x * x * x)))
    return 0.5 * x * (1.0 + t)


def _gelu_and_grad(x):
    x2 = x * x
    t = jnp.tanh(_GELU_C * (x + 0.044715 * (x2 * x)))
    g = 0.5 * x * (1.0 + t)
    dg = 0.5 * (1.0 + t) + 0.5 * x * (1.0 - t * t) * (_GELU_C * (1.0 + 3.0 * 0.044715 * x2))
    return g, dg


def _softplus(z):
    return jnp.maximum(z, 0.0) + jnp.log1p(jnp.exp(-jnp.abs(z)))


def _tril_mask():
    row = lax.broadcasted_iota(jnp.int32, (CHUNK, CHUNK), 0)
    col = lax.broadcasted_iota(jnp.int32, (CHUNK, CHUNK), 1)
    return row >= col, row <= col


HBM_SPEC = pl.BlockSpec(memory_space=pltpu.HBM)
SEM_SPEC = pl.BlockSpec(memory_space=pltpu.SEMAPHORE)
ANY_SPEC = pl.BlockSpec(memory_space=pl.ANY)
EFFECT = pltpu.SideEffectType.DATAFLOW_SIDE_EFFECTING


def _my_index():
    return 4 * lax.axis_index("x") + 2 * lax.axis_index("y") + lax.axis_index("c")


def _exchange_copies(srcs, lands, send_sems, recv_sems, scatter, arriving):
    x, y, c = lax.axis_index("x"), lax.axis_index("y"), lax.axis_index("c")
    me = 4 * x + 2 * y + c
    out = []
    for a in range(len(srcs)):
        for k in range(NDEV - 1):
            bits = k + 1
            px = 1 - x if (bits >> 2) & 1 else x
            py = 1 - y if (bits >> 1) & 1 else y
            pc = 1 - c if bits & 1 else c
            peer = 4 * px + 2 * py + pc
            src = srcs[a].at[peer] if scatter else srcs[a]
            out.append(pltpu.make_async_remote_copy(
                src_ref=src, dst_ref=lands[a].at[peer if arriving else me],
                send_sem=send_sems.at[a * (NDEV - 1) + k], recv_sem=recv_sems.at[a * (NDEV - 1) + k],
                device_id=(px, py, pc), device_id_type=MESH))
    return out


def _landing_zone(src, scatter):
    me = _my_index()
    own = lax.dynamic_index_in_dim(src, me, 0, keepdims=True) if scatter else src[None]
    shape = src.shape if scatter else (NDEV,) + src.shape
    return lax.dynamic_update_slice_in_dim(lax.empty(shape, src.dtype), own, me, axis=0)


def _exchange_start(groups, scatter, name):
    lands = [pltpu.with_memory_space_constraint(_landing_zone(s, scatter), pltpu.HBM) for g in groups for s in g]
    srcs = [pltpu.with_memory_space_constraint(s, pltpu.HBM) for g in groups for s in g]
    n, ng = len(srcs), len(groups)

    def body(*refs):
        src_refs, land_refs = refs[:n], refs[n:2 * n]
        sems = refs[2 * n:2 * n + 2 * ng]
        token = refs[-1]
        at = 0
        for g, grp in enumerate(groups):
            span = slice(at, at + len(grp))
            for send in _exchange_copies(src_refs[span], land_refs[span], sems[2 * g], sems[2 * g + 1], scatter, False):
                send.start()
            at += len(grp)
        token[...] = jnp.zeros_like(token)

    sem_shapes = []
    for grp in groups:
        sem_shapes += [pltpu.SemaphoreType.DMA((len(grp) * (NDEV - 1),))] * 2
    outs = _call(
        body, name=name,
        out_shape=tuple(sem_shapes) + tuple(pltpu.HBM(a.shape, a.dtype) for a in srcs + lands)
        + (jax.ShapeDtypeStruct((8, 128), F32),),
        in_specs=[HBM_SPEC] * (2 * n),
        out_specs=tuple([SEM_SPEC] * (2 * ng) + [HBM_SPEC] * (2 * n) + [pl.BlockSpec(memory_space=pltpu.VMEM)]),
        input_output_aliases={a: 2 * ng + a for a in range(2 * n)},
        compiler_params=pltpu.CompilerParams(has_side_effects=EFFECT),
    )(*srcs, *lands)
    sems, thru, token = outs[:2 * ng], outs[2 * ng:-1], outs[-1]
    pending, at = [], 0
    for g, grp in enumerate(groups):
        span = slice(at, at + len(grp))
        pending.append((sems[2 * g], sems[2 * g + 1], list(thru[:n][span]), list(thru[n:][span])))
        at += len(grp)
    return pending, token


def _exchange_wait(pending, after, scatter, name):
    send_sems, recv_sems, srcs, lands = pending
    n = len(srcs)

    def body(*refs):
        src_refs, land_refs = refs[:n], refs[n:2 * n]
        for send in _exchange_copies(src_refs, land_refs, refs[2 * n], refs[2 * n + 1], scatter, False):
            send.wait_send()
        for arrive in _exchange_copies(src_refs, land_refs, refs[2 * n], refs[2 * n + 1], scatter, True):
            arrive.wait_recv()

    outs = _call(
        body, name=name,
        out_shape=tuple(pltpu.HBM(a.shape, a.dtype) for a in srcs + lands),
        in_specs=[HBM_SPEC] * (2 * n) + [SEM_SPEC, SEM_SPEC, ANY_SPEC],
        out_specs=tuple([HBM_SPEC] * (2 * n)),
        input_output_aliases={a: a for a in range(2 * n)},
        compiler_params=pltpu.CompilerParams(has_side_effects=EFFECT),
    )(*srcs, *lands, send_sems, recv_sems, after)
    return list(outs[n:])


def _spatial_mix(vnb, ws_ref, bT_ref, mix_ref, tm):
    tri, _ = _tril_mask()
    for g in range(GROUPS):
        wm = jnp.where(tri, ws_ref[g], 0.0).astype(MXU)
        cols = slice(g * CHUNK, (g + 1) * CHUNK)
        for ch in range(tm // CHUNK):
            rows = slice(ch * CHUNK, (ch + 1) * CHUNK)
            mix_ref[rows, cols] = _mm(wm, vnb[rows, cols]) + bT_ref[:, g:g + 1]


def _sgu_fwd(x, ln, w_in, g_v, ws, bT, w_out, name):
    T, D = x.shape
    tm = min(TOKEN_TILE, T)
    nw = w_in.shape[2]

    def body(x_ref, ln_ref, win_ref, gv_ref, ws_ref, bT_ref, wout_ref, xo_ref, z_ref, h_ref, y_ref, mix_ref):
        xv = x_ref[...]
        h, _, _ = _rms(xv, ln_ref[...])
        hb = h.astype(MXU)
        h_ref[...] = hb
        for j in range(NDEV):
            z_ref[:, j * nw:(j + 1) * nw] = _mm(hb, win_ref[j])
        u = _gelu(z_ref[:, :D])
        gv = _gelu(z_ref[:, D:])
        vn, _, _ = _rms(gv, gv_ref[...])
        _spatial_mix(vn.astype(MXU), ws_ref, bT_ref, mix_ref, tm)
        y = (u * mix_ref[...]).astype(MXU)
        y_ref[...] = y
        xo_ref[...] = xv + _mm(y, wout_ref[...])

    return _call(
        body, name=name, grid=(T // tm,),
        out_shape=[jax.ShapeDtypeStruct((T, D), F32), jax.ShapeDtypeStruct((T, 2 * D), F32),
                   jax.ShapeDtypeStruct((T, D), MXU), jax.ShapeDtypeStruct((T, D), MXU)],
        in_specs=[_tile(tm, D), _whole(ln.shape), _whole(w_in.shape), _whole(g_v.shape), _whole(ws.shape),
                  _whole(bT.shape), _whole(w_out.shape)],
        out_specs=[_tile(tm, D), _tile(tm, 2 * D), _tile(tm, D), _tile(tm, D)],
        scratch_shapes=[pltpu.VMEM((tm, D), F32)],
        compiler_params=_params(40),
    )(x, ln, w_in, g_v, ws, bT, w_out)


def _mlp_fwd(x, ln, w_up, w_down, name):
    T, D = x.shape
    tm = min(TOKEN_TILE, T)
    nf = w_up.shape[2]
    F = nf * NDEV

    def body(x_ref, ln_ref, wup_ref, wdown_ref, xo_ref, pre_ref, h_ref):
        xv = x_ref[...]
        h, _, _ = _rms(xv, ln_ref[...])
        hb = h.astype(MXU)
        h_ref[...] = hb
        for j in range(NDEV):
            pre_ref[:, j * nf:(j + 1) * nf] = _mm(hb, wup_ref[j])
        a = jnp.maximum(pre_ref[...], 0.0)
        xo_ref[...] = xv + _mm(a * a, wdown_ref[...])

    return _call(
        body, name=name, grid=(T // tm,),
        out_shape=[jax.ShapeDtypeStruct((T, D), F32), jax.ShapeDtypeStruct((T, F), F32),
                   jax.ShapeDtypeStruct((T, D), MXU)],
        in_specs=[_tile(tm, D), _whole(ln.shape), _whole(w_up.shape), _whole(w_down.shape)],
        out_specs=[_tile(tm, D), _tile(tm, F), _tile(tm, D)],
        compiler_params=_params(52),
    )(x, ln, w_up, w_down)


def _ple_fwd(x, p, ln, w_g, w_pp, name):
    T, D = x.shape
    tm = min(TOKEN_TILE, T)
    npp = w_pp.shape[2]

    def body(x_ref, p_ref, ln_ref, wg_ref, wpp_ref, xo_ref, gate_ref, pp_ref, h_ref):
        xv = x_ref[...]
        h, _, _ = _rms(xv, ln_ref[...])
        hb = h.astype(MXU)
        h_ref[...] = hb
        gate = jax.nn.sigmoid(_mm(hb, wg_ref[...]))
        gate_ref[...] = gate
        pb = p_ref[...].astype(MXU)
        for j in range(NDEV):
            pp_ref[:, j * npp:(j + 1) * npp] = _mm(pb, wpp_ref[j])
        xo_ref[...] = xv + pp_ref[...] * gate

    return _call(
        body, name=name, grid=(T // tm,),
        out_shape=[jax.ShapeDtypeStruct((T, D), F32), jax.ShapeDtypeStruct((T, D), F32),
                   jax.ShapeDtypeStruct((T, D), F32), jax.ShapeDtypeStruct((T, D), MXU)],
        in_specs=[_tile(tm, D), _tile(tm, p.shape[1]), _whole(ln.shape), _whole(w_g.shape), _whole(w_pp.shape)],
        out_specs=[_tile(tm, D), _tile(tm, D), _tile(tm, D), _tile(tm, D)],
        compiler_params=_params(32),
    )(x, p, ln, w_g, w_pp)


def _head_ones():
    row = lax.broadcasted_iota(jnp.int32, (128, 128), 0)
    col = lax.broadcasted_iota(jnp.int32, (128, 128), 1)
    return (jnp.right_shift(row, 6) == jnp.right_shift(col, 6)).astype(MXU)


def _head_rms(x, g, ones):
    rstd = lax.rsqrt(_split_dot(x * x, ones, 3) * (1.0 / HEAD_DIM) + EPS)
    xhat = x * rstd
    return xhat * g, xhat, rstd


def _head_rms_bwd(dh, xhat, rstd, g, ones):
    dxh = dh * g
    mean = _split_dot(dxh * xhat, ones, 3) * (1.0 / HEAD_DIM)
    return rstd * (dxh - xhat * mean), jnp.sum(dh * xhat, axis=0, keepdims=True)


def _qkv_fwd(x, ln_q, ln_kv, g_q, g_k, w_q, w_kv, name):
    T, D = x.shape
    tm = min(TOKEN_TILE, T)
    nk = w_kv.shape[2]
    half = NDEV // 2

    def body(x_ref, lnq_ref, lnkv_ref, gq_ref, gk_ref, wq_ref, wkv_ref,
             q_ref, k_ref, v_ref, qpre_ref, kpre_ref, hq_ref, hkv_ref):
        xv = x_ref[...]
        _, xhat, _ = _rms(xv, lnq_ref[...])
        hq = (xhat * lnq_ref[...]).astype(MXU)
        hkv = (xhat * lnkv_ref[...]).astype(MXU)
        hq_ref[...] = hq
        hkv_ref[...] = hkv
        qpre_ref[...] = _mm(hq, wq_ref[...])
        for j in range(half):
            kpre_ref[:, j * nk:(j + 1) * nk] = _mm(hkv, wkv_ref[j])
            v_ref[:, j * nk:(j + 1) * nk] = _mm(hkv, wkv_ref[half + j]).astype(MXU)
        ones = _head_ones()
        for b in range(D // 128):
            cols = slice(b * 128, (b + 1) * 128)
            qn, _, _ = _head_rms(qpre_ref[:, cols], gq_ref[:, cols], ones)
            q_ref[:, cols] = (qn * SCALE).astype(MXU)
            kn, _, _ = _head_rms(kpre_ref[:, cols], gk_ref[:, cols], ones)
            k_ref[:, cols] = kn.astype(MXU)

    return _call(
        body, name=name, grid=(T // tm,),
        out_shape=[jax.ShapeDtypeStruct((T, D), MXU)] * 3 + [jax.ShapeDtypeStruct((T, D), F32)] * 2
        + [jax.ShapeDtypeStruct((T, D), MXU)] * 2,
        in_specs=[_tile(tm, D), _whole(ln_q.shape), _whole(ln_kv.shape), _whole(g_q.shape), _whole(g_k.shape),
                  _whole(w_q.shape), _whole(w_kv.shape)],
        out_specs=[_tile(tm, D)] * 7,
        compiler_params=_params(40),
    )(x, ln_q, ln_kv, g_q, g_k, w_q, w_kv)


SB_KEYS = 2 * QBLK


def _sb_consts():
    row = lax.broadcasted_iota(jnp.int32, (QBLK, QBLK), 0)
    col = lax.broadcasted_iota(jnp.int32, (QBLK, QBLK), 1)
    lane = lax.broadcasted_iota(jnp.int32, (QBLK, 128), 1)
    return (row > col).astype(MXU), (row >= col).astype(MXU), lane < HEAD_DIM


def _sb_causal(i, off):
    row = lax.broadcasted_iota(jnp.int32, (QBLK, SB_KEYS), 0)
    col = lax.broadcasted_iota(jnp.int32, (QBLK, SB_KEYS), 1)
    return (col + off) < (row + i * QBLK)


def _sb_suffix(x, ones, terms):
    s = _split_dot(jnp.concatenate([x[:, :QBLK], x[:, QBLK:]], axis=0), ones, terms)
    return s[:QBLK], s[QBLK:]


def _sb_fwd(q, k, v, name):
    T, D = q.shape

    def body(q_ref, k_ref, v_ref, o_ref):
        i = pl.program_id(1)
        n_steps = (i + 2) // 2
        later, _, first = _sb_consts()
        qv = q_ref[...]
        zero = jnp.zeros_like(qv)
        qs = (jnp.where(first, qv, zero), jnp.where(first, zero, qv))

        def block(j, carry, masked):
            off = pl.multiple_of(j * SB_KEYS, SB_KEYS)
            kj = k_ref[pl.ds(off, SB_KEYS), :]
            vj = v_ref[pl.ds(off, SB_KEYS), :]
            mask = _sb_causal(i, off) if masked else None
            out = []
            for hh in range(2):
                c_l, acc = carry[hh]
                z = _mm_nt(qs[hh], kj)
                sp = _softplus(z)
                l = jnp.where(mask, -sp, 0.0) if masked else -sp
                s_lo, s_hi = _sb_suffix(l, later, 2)
                r_lo = s_lo[:, :1] + l[:, :1]
                r_hi = s_hi[:, :1] + l[:, QBLK:QBLK + 1]
                b = jnp.concatenate([s_lo + (c_l + r_hi), s_hi + c_l], axis=1)
                a = jnp.exp(z - sp + b)
                if masked:
                    a = jnp.where(mask, a, 0.0)
                out.append((c_l + (r_lo + r_hi), acc + _mm(a, vj)))
            return tuple(out)

        init = tuple((jnp.zeros((QBLK, 1), F32), jnp.zeros((QBLK, 128), F32)) for _ in range(2))
        carry = block(n_steps - 1, init, True)
        carry = lax.fori_loop(0, n_steps - 1, lambda jj, c: block(n_steps - 2 - jj, c, False), carry)
        o_ref[...] = jnp.where(first, carry[0][1], carry[1][1])

    qblk = pl.BlockSpec((QBLK, 128), lambda h, i: (i, h))
    kblk = pl.BlockSpec((T, 128), lambda h, i: (0, h))
    return _call(
        body, name=name, grid=(D // 128, T // QBLK), out_shape=jax.ShapeDtypeStruct((T, D), F32),
        in_specs=[qblk, kblk, kblk], out_specs=qblk, compiler_params=_params(32, 2),
    )(q, k, v)


def _sb_bwd(q, k, v, o, do, name):
    T, D = q.shape

    def body(q_ref, k_ref, v_ref, o_ref, do_ref, dq_ref, dk_ref, dv_ref):
        i = pl.program_id(1)

        @pl.when(i == 0)
        def _():
            dk_ref[...] = jnp.zeros_like(dk_ref)
            dv_ref[...] = jnp.zeros_like(dv_ref)

        n_steps = (i + 2) // 2
        later, later_eq, first = _sb_consts()
        qv = q_ref[...]
        dob = do_ref[...].astype(MXU)
        zero = jnp.zeros_like(qv)
        qs = (jnp.where(first, qv, zero), jnp.where(first, zero, qv))
        dos = (jnp.where(first, dob, zero), jnp.where(first, zero, dob))
        prod = o_ref[...] * dob.astype(F32)
        totals = (jnp.sum(jnp.where(first, prod, 0.0), axis=1, keepdims=True),
                  jnp.sum(jnp.where(first, 0.0, prod), axis=1, keepdims=True))

        def block(j, carry, masked):
            off = pl.multiple_of(j * SB_KEYS, SB_KEYS)
            kj = k_ref[pl.ds(off, SB_KEYS), :]
            vj = v_ref[pl.ds(off, SB_KEYS), :]
            mask = _sb_causal(i, off) if masked else None
            out = []
            dk_blk = dv_blk = None
            for hh in range(2):
                c_l, c_e, dq = carry[hh]
                z = _mm_nt(qs[hh], kj)
                sp = _softplus(z)
                l = jnp.where(mask, -sp, 0.0) if masked else -sp
                s_lo, s_hi = _sb_suffix(l, later, 2)
                r_lo = s_lo[:, :1] + l[:, :1]
                r_hi = s_hi[:, :1] + l[:, QBLK:QBLK + 1]
                b = jnp.concatenate([s_lo + (c_l + r_hi), s_hi + c_l], axis=1)
                log_sig = z - sp
                a = jnp.exp(log_sig + b)
                if masked:
                    a = jnp.where(mask, a, 0.0)
                ab = a.astype(MXU)
                e = ab.astype(F32) * _mm_nt(dos[hh], vj)
                t_lo, t_hi = _sb_suffix(e, later_eq, 3)
                e_lo, e_hi = t_lo[:, :1], t_hi[:, :1]
                before = totals[hh] - jnp.concatenate([t_lo + (c_e + e_hi), t_hi + c_e], axis=1)
                dz = e * jnp.exp(-sp) - jnp.exp(log_sig) * before
                if masked:
                    dz = jnp.where(mask, dz, 0.0)
                dzb = dz.astype(MXU)
                dk_h = _mm_tn(dzb, qs[hh])
                dv_h = _mm_tn(ab, dos[hh])
                dk_blk = dk_h if dk_blk is None else dk_blk + dk_h
                dv_blk = dv_h if dv_blk is None else dv_blk + dv_h
                out.append((c_l + (r_lo + r_hi), c_e + (e_lo + e_hi), dq + _mm(dzb, kj)))
            dk_ref[pl.ds(off, SB_KEYS), :] += dk_blk
            dv_ref[pl.ds(off, SB_KEYS), :] += dv_blk
            return tuple(out)

        init = tuple((jnp.zeros((QBLK, 1), F32), jnp.zeros((QBLK, 1), F32), jnp.zeros((QBLK, 128), F32))
                     for _ in range(2))
        carry = block(n_steps - 1, init, True)
        carry = lax.fori_loop(0, n_steps - 1, lambda jj, c: block(n_steps - 2 - jj, c, False), carry)
        dq_ref[...] = jnp.where(first, carry[0][2], carry[1][2]) * SCALE

    qblk = pl.BlockSpec((QBLK, 128), lambda h, i: (i, h))
    kblk = pl.BlockSpec((T, 128), lambda h, i: (0, h))
    full = jax.ShapeDtypeStruct((T, D), F32)
    return _call(
        body, name=name, grid=(D // 128, T // QBLK), out_shape=[full, full, full],
        in_specs=[qblk, kblk, kblk, qblk, qblk], out_specs=[qblk, kblk, kblk], compiler_params=_params(32, 2),
    )(q, k, v, o, do)


def _proj_res(x, a, w, name):
    T, D = x.shape
    tm = min(TOKEN_TILE, T)

    def body(x_ref, a_ref, w_ref, o_ref):
        o_ref[...] = x_ref[...] + _mm(a_ref[...], w_ref[...])

    return _call(
        body, name=name, grid=(T // tm,), out_shape=jax.ShapeDtypeStruct((T, D), F32),
        in_specs=[_tile(tm, D), _tile(tm, a.shape[1]), _whole(w.shape)], out_specs=_tile(tm, D),
        compiler_params=_params(32),
    )(x, a, w)


def _proj_nt(g, w, name, dep):
    T = g.shape[0]
    K = w.shape[0]
    tm = min(TOKEN_TILE, T)

    def body(g_ref, w_ref, dep_ref, o_ref):
        o_ref[...] = _mm_nt(g_ref[...], w_ref[...])

    return _call(
        body, name=name, grid=(T // tm,), out_shape=jax.ShapeDtypeStruct((T, K), F32),
        in_specs=[_tile(tm, g.shape[1]), _whole(w.shape), ANY_SPEC], out_specs=_tile(tm, K),
        compiler_params=_params(32),
    )(g, w, dep)


def _loss_grad(y, tgt, name):
    T, D = y.shape
    tm = min(TOKEN_TILE, T)

    def body(y_ref, t_ref, dy_ref, loss_ref):
        @pl.when(pl.program_id(0) == 0)
        def _():
            loss_ref[...] = jnp.zeros_like(loss_ref)
        diff = y_ref[...] - t_ref[...]
        dy_ref[...] = diff * (1.0 / D)
        rows = jnp.sum(diff * diff, axis=1, keepdims=True) * (1.0 / D)
        loss_ref[...] += 0.5 * jnp.sum(rows, axis=0, keepdims=True)

    return _call(
        body, name=name, grid=(T // tm,),
        out_shape=[jax.ShapeDtypeStruct((T, D), F32), jax.ShapeDtypeStruct((1, 1), F32)],
        in_specs=[_tile(tm, D), _tile(tm, D)], out_specs=[_tile(tm, D), _acc((1, 1))],
        compiler_params=_params(32),
    )(y, tgt)


def _ple_bwd(dx, x, gate, pp, ln, w_g, name, dep):
    T, D = x.shape
    tm = min(TOKEN_TILE, T)

    def body(dx_ref, x_ref, gate_ref, pp_ref, ln_ref, wg_ref, dep_ref, dxo_ref, dpp_ref, dgp_ref, dln_ref):
        @pl.when(pl.program_id(0) == 0)
        def _():
            dln_ref[...] = jnp.zeros_like(dln_ref)
        dxv = dx_ref[...]
        gate = gate_ref[...]
        _, xhat, rstd = _rms(x_ref[...], ln_ref[...])
        dpp_ref[...] = (dxv * gate).astype(MXU)
        dgp = (dxv * pp_ref[...] * gate * (1.0 - gate)).astype(MXU)
        dgp_ref[...] = dgp
        dxn, dln = _rms_bwd(_mm_nt(dgp, wg_ref[...]), xhat, rstd, ln_ref[...])
        dln_ref[...] += dln
        dxo_ref[...] = dxn + dxv

    return _call(
        body, name=name, grid=(T // tm,),
        out_shape=[jax.ShapeDtypeStruct((T, D), F32), jax.ShapeDtypeStruct((T, D), MXU),
                   jax.ShapeDtypeStruct((T, D), MXU), jax.ShapeDtypeStruct(ln.shape, F32)],
        in_specs=[_tile(tm, D)] * 4 + [_whole(ln.shape), _whole(w_g.shape), ANY_SPEC],
        out_specs=[_tile(tm, D), _tile(tm, D), _tile(tm, D), _acc(ln.shape)],
        compiler_params=_params(32),
    )(dx, x, gate, pp, ln, w_g, dep)


def _mlp_bwd(dx, x, pre, ln, w_up, w_down, name):
    T, D = x.shape
    tm = min(TOKEN_TILE, T)
    nf = w_up.shape[2]
    F = nf * NDEV

    def body(dx_ref, x_ref, pre_ref, ln_ref, wup_ref, wdown_ref, dxo_ref, dpre_ref, s_ref, dln_ref):
        @pl.when(pl.program_id(0) == 0)
        def _():
            dln_ref[...] = jnp.zeros_like(dln_ref)
        dxv = dx_ref[...]
        _, xhat, rstd = _rms(x_ref[...], ln_ref[...])
        a = jnp.maximum(pre_ref[...], 0.0)
        s_ref[...] = (a * a).astype(MXU)
        dpre_ref[...] = (_mm_nt(dxv, wdown_ref[...]) * (2.0 * a)).astype(MXU)
        dh = _mm_nt(dpre_ref[:, :nf], wup_ref[0])
        for j in range(1, NDEV):
            dh += _mm_nt(dpre_ref[:, j * nf:(j + 1) * nf], wup_ref[j])
        dxn, dln = _rms_bwd(dh, xhat, rstd, ln_ref[...])
        dln_ref[...] += dln
        dxo_ref[...] = dxn + dxv

    return _call(
        body, name=name, grid=(T // tm,),
        out_shape=[jax.ShapeDtypeStruct((T, D), F32), jax.ShapeDtypeStruct((T, F), MXU),
                   jax.ShapeDtypeStruct((T, F), MXU), jax.ShapeDtypeStruct(ln.shape, F32)],
        in_specs=[_tile(tm, D), _tile(tm, D), _tile(tm, F), _whole(ln.shape), _whole(w_up.shape),
                  _whole(w_down.shape)],
        out_specs=[_tile(tm, D), _tile(tm, F), _tile(tm, F), _acc(ln.shape)],
        compiler_params=_params(56),
    )(dx, x, pre, ln, w_up, w_down)


def _qkv_bwd(dx, x, dq, dk, dv, q_pre, k_pre, ln_q, ln_kv, g_q, g_k, w_q, w_kv, name):
    T, D = x.shape
    tm = min(TOKEN_TILE, T)
    nk = w_kv.shape[2]
    n_tiles = T // tm

    def body(dx_ref, x_ref, dq_ref, dk_ref, dv_ref, qpre_ref, kpre_ref, lnq_ref, lnkv_ref, gq_ref, gk_ref,
             wq_ref, wkv_ref, dxo_ref, dqp_ref, dkv_ref, dlnq_ref, dlnkv_ref, dgq_ref, dgk_ref, gq_acc, gk_acc):
        i = pl.program_id(0)

        @pl.when(i == 0)
        def _():
            dlnq_ref[...] = jnp.zeros_like(dlnq_ref)
            dlnkv_ref[...] = jnp.zeros_like(dlnkv_ref)
            gq_acc[...] = jnp.zeros_like(gq_acc)
            gk_acc[...] = jnp.zeros_like(gk_acc)

        ones = _head_ones()
        for b in range(D // 128):
            cols = slice(b * 128, (b + 1) * 128)
            _, xh, rs = _head_rms(qpre_ref[:, cols], gq_ref[:, cols], ones)
            d, dg = _head_rms_bwd(dq_ref[:, cols], xh, rs, gq_ref[:, cols], ones)
            dqp_ref[:, cols] = d.astype(MXU)
            gq_acc[:, cols] += dg
            _, xh, rs = _head_rms(kpre_ref[:, cols], gk_ref[:, cols], ones)
            d, dg = _head_rms_bwd(dk_ref[:, cols], xh, rs, gk_ref[:, cols], ones)
            dkv_ref[:, cols] = d.astype(MXU)
            gk_acc[:, cols] += dg
        dkv_ref[:, D:] = dv_ref[...].astype(MXU)

        _, xhat, rstd = _rms(x_ref[...], lnq_ref[...])
        dhq = _mm_nt(dqp_ref[...], wq_ref[...])
        dhkv = _mm_nt(dkv_ref[:, :nk], wkv_ref[0])
        for j in range(1, NDEV):
            dhkv += _mm_nt(dkv_ref[:, j * nk:(j + 1) * nk], wkv_ref[j])
        dxq, dlnq = _rms_bwd(dhq, xhat, rstd, lnq_ref[...])
        dxkv, dlnkv = _rms_bwd(dhkv, xhat, rstd, lnkv_ref[...])
        dlnq_ref[...] += dlnq
        dlnkv_ref[...] += dlnkv
        dxo_ref[...] = dx_ref[...] + dxq + dxkv

        @pl.when(i == n_tiles - 1)
        def _():
            row = lax.broadcasted_iota(jnp.int32, (D, 128), 0)
            col = lax.broadcasted_iota(jnp.int32, (D, 128), 1)
            fold = (jnp.bitwise_and(row, HEAD_DIM - 1) == col).astype(MXU)
            dgq_ref[...] = _split_dot(jnp.broadcast_to(gq_acc[...], (8, D)), fold, 3)
            dgk_ref[...] = _split_dot(jnp.broadcast_to(gk_acc[...], (8, D)), fold, 3)

    small = jax.ShapeDtypeStruct((8, 128), F32)
    return _call(
        body, name=name, grid=(n_tiles,),
        out_shape=[jax.ShapeDtypeStruct((T, D), F32), jax.ShapeDtypeStruct((T, D), MXU),
                   jax.ShapeDtypeStruct((T, 2 * D), MXU), jax.ShapeDtypeStruct(ln_q.shape, F32),
                   jax.ShapeDtypeStruct(ln_kv.shape, F32), small, small],
        in_specs=[_tile(tm, D)] * 7 + [_whole(ln_q.shape), _whole(ln_kv.shape), _whole(g_q.shape),
                                       _whole(g_k.shape), _whole(w_q.shape), _whole(w_kv.shape)],
        out_specs=[_tile(tm, D), _tile(tm, D), _tile(tm, 2 * D), _acc(ln_q.shape), _acc(ln_kv.shape),
                   _acc((8, 128)), _acc((8, 128))],
        scratch_shapes=[pltpu.VMEM((1, D), F32), pltpu.VMEM((1, D), F32)],
        compiler_params=_params(48),
    )(dx, x, dq, dk, dv, q_pre, k_pre, ln_q, ln_kv, g_q, g_k, w_q, w_kv)


def _sgu_bwd(dx, x, z, ln, w_in, g_v, ws, wsT, bT, w_out, name, dep):
    T, D = x.shape
    tm = min(TOKEN_TILE, T)
    nw = w_in.shape[2]

    def body(dx_ref, x_ref, z_ref, ln_ref, win_ref, gv_ref, ws_ref, wsT_ref, bT_ref, wout_ref, dep_ref,
             dxo_ref, dz_ref, dws_ref, dbT_ref, dln_ref, dgv_ref, mix_ref, dvn_ref):
        @pl.when(pl.program_id(0) == 0)
        def _():
            dws_ref[...] = jnp.zeros_like(dws_ref)
            dbT_ref[...] = jnp.zeros_like(dbT_ref)
            dln_ref[...] = jnp.zeros_like(dln_ref)
            dgv_ref[...] = jnp.zeros_like(dgv_ref)
        dxv = dx_ref[...]
        _, xhat, rstd = _rms(x_ref[...], ln_ref[...])
        u, du = _gelu_and_grad(z_ref[:, :D])
        gv, dgv = _gelu_and_grad(z_ref[:, D:])
        vn, vhat, rstd_v = _rms(gv, gv_ref[...])
        vnb = vn.astype(MXU)
        _spatial_mix(vnb, ws_ref, bT_ref, mix_ref, tm)
        dy = _mm_nt(dxv, wout_ref[...])
        d_u = dy * mix_ref[...]
        d_mix = dy * u
        dmb = d_mix.astype(MXU)
        tri, triT = _tril_mask()
        for g in range(GROUPS):
            wmT = jnp.where(triT, wsT_ref[g], 0.0).astype(MXU)
            cols = slice(g * CHUNK, (g + 1) * CHUNK)
            for ch in range(tm // CHUNK):
                rows = slice(ch * CHUNK, (ch + 1) * CHUNK)
                dm = dmb[rows, cols]
                dws_ref[g] += jnp.where(tri, _mm_nt(dm, vnb[rows, cols]), 0.0)
                dbT_ref[:, g:g + 1] += jnp.sum(d_mix[rows, cols], axis=1, keepdims=True)
                dvn_ref[rows, cols] = _mm(wmT, dm)
        d_gv, dg = _rms_bwd(dvn_ref[...], vhat, rstd_v, gv_ref[...])
        dgv_ref[...] += dg
        dz_ref[:, :D] = (d_u * du).astype(MXU)
        dz_ref[:, D:] = (d_gv * dgv).astype(MXU)
        dh = _mm_nt(dz_ref[:, :nw], win_ref[0])
        for j in range(1, NDEV):
            dh += _mm_nt(dz_ref[:, j * nw:(j + 1) * nw], win_ref[j])
        dxn, dln = _rms_bwd(dh, xhat, rstd, ln_ref[...])
        dln_ref[...] += dln
        dxo_ref[...] = dxn + dxv

    return _call(
        body, name=name, grid=(T // tm,),
        out_shape=[jax.ShapeDtypeStruct((T, D), F32), jax.ShapeDtypeStruct((T, 2 * D), MXU),
                   jax.ShapeDtypeStruct(ws.shape, F32), jax.ShapeDtypeStruct(bT.shape, F32),
                   jax.ShapeDtypeStruct(ln.shape, F32), jax.ShapeDtypeStruct(g_v.shape, F32)],
        in_specs=[_tile(tm, D), _tile(tm, D), _tile(tm, 2 * D), _whole(ln.shape), _whole(w_in.shape),
                  _whole(g_v.shape), _whole(ws.shape), _whole(wsT.shape), _whole(bT.shape), _whole(w_out.shape),
                  ANY_SPEC],
        out_specs=[_tile(tm, D), _tile(tm, 2 * D), _acc(ws.shape), _acc(bT.shape), _acc(ln.shape),
                   _acc(g_v.shape)],
        scratch_shapes=[pltpu.VMEM((tm, D), F32), pltpu.VMEM((tm, D), F32)],
        compiler_params=_params(48),
    )(dx, x, z, ln, w_in, g_v, ws, wsT, bT, w_out, dep)


def _wgrad_rows(a, g, name):
    T, K = a.shape
    N = g.shape[1]
    kb = K // NDEV

    def body(a_ref, g_ref, o_ref):
        o_ref[...] = _mm_tn(a_ref[...], g_ref[...]).astype(COMM)

    return _call(
        body, name=name, grid=(NDEV,), out_shape=jax.ShapeDtypeStruct((K, N), COMM),
        in_specs=[pl.BlockSpec((T, kb), lambda j: (0, j)), _whole(g.shape)],
        out_specs=pl.BlockSpec((kb, N), lambda j: (j, 0)),
        compiler_params=_params(40),
    )(a, g).reshape(NDEV, kb, N)


def _wgrad_cols(a, g, name):
    T, K = a.shape
    N = g.shape[1]
    nb = N // NDEV

    def body(a_ref, g_ref, o_ref):
        o_ref[...] = _mm_tn(a_ref[...], g_ref[...]).astype(COMM)

    return _call(
        body, name=name, grid=(NDEV,), out_shape=jax.ShapeDtypeStruct((NDEV, K, nb), COMM),
        in_specs=[_whole(a.shape), pl.BlockSpec((T, nb), lambda j: (0, j))],
        out_specs=pl.BlockSpec((None, K, nb), lambda j: (j, 0, 0)),
        compiler_params=_params(40),
    )(a, g)


def _adamw(w, m, v, slots, name):
    R, C = w.shape
    n = slots.shape[0]
    tr = math.gcd(R, max(8, (128 * 1024) // C))
    if tr < 64:
        tr = R
    bc1 = 1.0 - ADAM_B1 ** ADAM_STEP
    bc2 = 1.0 - ADAM_B2 ** ADAM_STEP

    def body(w_ref, m_ref, v_ref, s_ref, g_ref, d_ref, mo_ref, vo_ref):
        g = s_ref[0].astype(F32)
        for j in range(1, n):
            g = g + s_ref[j].astype(F32)
        mn = ADAM_B1 * m_ref[...] + (1.0 - ADAM_B1) * g
        vn = ADAM_B2 * v_ref[...] + (1.0 - ADAM_B2) * (g * g)
        g_ref[...] = g
        mo_ref[...] = mn
        vo_ref[...] = vn
        d_ref[...] = -ADAM_LR * ((mn / bc1) / (jnp.sqrt(vn / bc2) + ADAM_EPS) + ADAM_WD * w_ref[...])

    blk = pl.BlockSpec((tr, C), lambda i: (i, 0))
    out = jax.ShapeDtypeStruct((R, C), F32)
    return _call(
        body, name=name, grid=(R // tr,), out_shape=[out, out, out, out],
        in_specs=[blk, blk, blk, pl.BlockSpec((n, tr, C), lambda i: (0, i, 0))], out_specs=[blk] * 4,
        compiler_params=_params(32),
    )(w, m, v, slots)


def _rows128(a):
    flat = a.reshape(-1)
    rows = -(-flat.shape[0] // 1024) * 8
    flat = jnp.pad(flat, (0, rows * 128 - flat.shape[0]))
    return flat.reshape(rows, 128)


def kernel(x, p, ln_mix_a, w_in_a, g_v_a, w_spatial, b_spatial, w_out_a, ln_kv, w_kv, g_k, ln_mix_b, w_q, g_q, w_out_b, ln_mlp, w_up, w_down, ln_ple, w_ple_gate, w_ple_proj, loss_target, m_ln_mix_a, m_w_in_a, m_g_v_a, m_w_spatial, m_b_spatial, m_w_out_a, m_ln_kv, m_w_kv, m_g_k, m_ln_mix_b, m_w_q, m_g_q, m_w_out_b, m_ln_mlp, m_w_up, m_w_down, m_ln_ple, m_w_ple_gate, m_w_ple_proj, v_ln_mix_a, v_w_in_a, v_g_v_a, v_w_spatial, v_b_spatial, v_w_out_a, v_ln_kv, v_w_kv, v_g_k, v_ln_mix_b, v_w_q, v_g_q, v_w_out_b, v_ln_mlp, v_w_up, v_w_down, v_ln_ple, v_w_ple_gate, v_w_ple_proj):
    me = 4 * lax.axis_index("x") + 2 * lax.axis_index("y") + lax.axis_index("c")
    D = x.shape[2]
    x0, tgt = x[0], loss_target[0]
    n_layers = w_up.shape[0]

    c = lambda w: w.astype(COMM)
    groups = [[c(w_in_a[0]), c(w_out_a[0]), ln_mix_a, g_v_a],
              [c(w_up[0]), c(w_down[0])],
              [c(w_ple_gate[0]), c(w_ple_proj[0]), c(w_q[0]), c(w_kv)],
              [c(w_out_b[0]), c(w_up[1]), c(w_down[1]), c(w_ple_gate[1]), c(w_ple_proj[1])]]
    pending_w, _ = _exchange_start(groups, scatter=False, name="gather_start")
    x0 = x[0]
    W_in, W_out_a, ln_a, gv_a = _exchange_wait(pending_w[0], x0, False, "gather_wait0")
    W_out_a, ln_a, gv_a = W_out_a.reshape(-1, D), ln_a.reshape(1, D), gv_a.reshape(1, D)
    ws = w_spatial[0]
    wsT = jnp.swapaxes(ws, 1, 2)
    bT = b_spatial[0].T
    ln_kv2, ln_b = ln_kv.reshape(1, D), ln_mix_b
    gk2 = jnp.tile(g_k.reshape(1, HEAD_DIM), (1, D // HEAD_DIM))
    gq2 = jnp.tile(g_q, (1, D // HEAD_DIM))
    ln_m = [ln_mlp[l:l + 1] for l in range(n_layers)]
    ln_p = [ln_ple[l:l + 1] for l in range(n_layers)]

    x1, z, h_a, y_a = _sgu_fwd(x0, ln_a, W_in, gv_a, ws, bT, W_out_a, "sgu_fwd")
    W_up0, W_down0 = _exchange_wait(pending_w[1], x1, False, "gather_wait1")
    x2, pre0, hm0 = _mlp_fwd(x1, ln_m[0], W_up0, W_down0.reshape(-1, D), "mlp_fwd0")
    W_g0, W_pp0, W_q, W_kv = _exchange_wait(pending_w[2], x2, False, "gather_wait2")
    W_g0, W_q = W_g0.reshape(-1, D), W_q.reshape(-1, D)
    x3, gate0, pp0, hp0 = _ple_fwd(x2, p[0, 0], ln_p[0], W_g0, W_pp0, "ple_fwd0")
    qn, kn, vn, q_pre, k_pre, h_q, h_kv = _qkv_fwd(x3, ln_b, ln_kv2, gq2, gk2, W_q, W_kv, "qkv_fwd")
    o2d = _sb_fwd(qn, kn, vn, "sb_fwd")
    W_out_b, W_up1, W_down1, W_g1, W_pp1 = _exchange_wait(pending_w[3], o2d, False, "gather_wait3")
    W_out_b, W_down1, W_g1 = W_out_b.reshape(-1, D), W_down1.reshape(-1, D), W_g1.reshape(-1, D)
    x4 = _proj_res(x3, o2d, W_out_b, "attn_out")
    x5, pre1, hm1 = _mlp_fwd(x4, ln_m[1], W_up1, W_down1, "mlp_fwd1")
    x6, gate1, pp1, hp1 = _ple_fwd(x5, p[1, 0], ln_p[1], W_g1, W_pp1, "ple_fwd1")
    dy, loss_part = _loss_grad(x6, tgt, "loss_grad")
    loss = lax.psum(loss_part[0, 0], ("x", "y", "c"))

    def scatter_start(arrs, name):
        pending, token = _exchange_start([arrs], scatter=True, name=name)
        return pending[0], token

    dx5, dpp1, dgp1, dlnp1 = _ple_bwd(dy, x5, gate1, pp1, ln_p[1], W_g1, "ple_bwd1", dep=loss_part)
    dx4, dpre1, s1, dlnm1 = _mlp_bwd(dx5, x4, pre1, ln_m[1], W_up1, W_down1, "mlp_bwd1")
    pend_a, tok_a = scatter_start(
        [_wgrad_cols(hm1, dpre1, "wg_up1"), _wgrad_rows(s1, dx5, "wg_down1"), _wgrad_rows(hp1, dgp1, "wg_gate1"),
         _wgrad_cols(p[1, 0].astype(MXU), dpp1, "wg_proj1"), _wgrad_rows(o2d, dx4, "wg_out_b")], "scatter_start_a")
    do2d = _proj_nt(dx4, W_out_b, "attn_out_bwd", dep=tok_a)
    dqn, dkn, dvn = _sb_bwd(qn, kn, vn, o2d, do2d, "sb_bwd")
    dx3, dq_pre, dkv, dlnb, dlnkv, dgq, dgk = _qkv_bwd(dx4, x3, dqn, dkn, dvn, q_pre, k_pre, ln_b, ln_kv2, gq2, gk2,
                                                      W_q, W_kv, "qkv_bwd")
    dgq, dgk = dgq[:1, :HEAD_DIM], dgk[:1, :HEAD_DIM]
    pend_b, tok_b = scatter_start([_wgrad_rows(h_q, dq_pre, "wg_q"), _wgrad_cols(h_kv, dkv, "wg_kv")],
                                  "scatter_start_b")
    dx2, dpp0, dgp0, dlnp0 = _ple_bwd(dx3, x2, gate0, pp0, ln_p[0], W_g0, "ple_bwd0", dep=tok_b)
    dx1, dpre0, s0, dlnm0 = _mlp_bwd(dx2, x1, pre0, ln_m[0], W_up0, W_down0.reshape(-1, D), "mlp_bwd0")
    pend_c, tok_c = scatter_start(
        [_wgrad_cols(hm0, dpre0, "wg_up0"), _wgrad_rows(s0, dx2, "wg_down0"), _wgrad_rows(hp0, dgp0, "wg_gate0"),
         _wgrad_cols(p[0, 0].astype(MXU), dpp0, "wg_proj0")], "scatter_start_c")
    dx0, dz, dws, dbT, dlna, dgva = _sgu_bwd(dx1, x0, z, ln_a, W_in, gv_a, ws, wsT, bT, W_out_a, "sgu_bwd",
                                             dep=tok_c)
    pend_d, tok_d = scatter_start([_wgrad_cols(h_a, dz, "wg_in_a"), _wgrad_rows(y_a, dx1, "wg_out_a")],
                                  "scatter_start_d")

    small = [("w_spatial", dws[None], w_spatial, m_w_spatial, v_w_spatial),
             ("b_spatial", dbT.T[None], b_spatial, m_b_spatial, v_b_spatial),
             ("ln_kv", dlnkv.reshape(-1), ln_kv, m_ln_kv, v_ln_kv),
             ("g_k", dgk.reshape(-1), g_k, m_g_k, v_g_k),
             ("ln_mix_b", dlnb, ln_mix_b, m_ln_mix_b, v_ln_mix_b),
             ("g_q", dgq, g_q, m_g_q, v_g_q),
             ("ln_mlp", jnp.concatenate([dlnm0, dlnm1]), ln_mlp, m_ln_mlp, v_ln_mlp),
             ("ln_ple", jnp.concatenate([dlnp0, dlnp1]), ln_ple, m_ln_ple, v_ln_ple)]
    sharded_vec = [("ln_mix_a", dlna, ln_mix_a, m_ln_mix_a, v_ln_mix_a),
                   ("g_v_a", dgva, g_v_a, m_g_v_a, v_g_v_a)]
    packs = [[], [], [], []]
    for _, g, w, m, v in small:
        for lst, a in zip(packs, (g, w, m, v)):
            lst.append(_rows128(a))
    for _, g, w, m, v in sharded_vec:
        packs[0].append(g.reshape(NDEV, -1))
        for lst, a in zip(packs[1:], (w, m, v)):
            lst.append(jnp.broadcast_to(a, (NDEV, a.shape[1])))
    g_pack, w_pack, m_pack, v_pack = (jnp.concatenate(lst) for lst in packs)
    pend_s, tok_s = _exchange_start([[g_pack]], scatter=False, name="small_grads_start")

    def upd(w, m, v, s, name):
        shape = w.shape
        outs = _adamw(w.reshape(-1, shape[-1]), m.reshape(-1, shape[-1]), v.reshape(-1, shape[-1]), s, name)
        return [o.reshape(shape) for o in outs]

    res = {}
    per = {}
    s_up1, s_down1, s_gate1, s_proj1, s_out_b = _exchange_wait(pend_a, tok_s + tok_d, True, "scatter_wait_a")
    per["w_up", 1] = upd(w_up[1], m_w_up[1], v_w_up[1], s_up1, "adam_up1")
    per["w_down", 1] = upd(w_down[1], m_w_down[1], v_w_down[1], s_down1, "adam_down1")
    per["w_ple_gate", 1] = upd(w_ple_gate[1], m_w_ple_gate[1], v_w_ple_gate[1], s_gate1, "adam_gate1")
    per["w_ple_proj", 1] = upd(w_ple_proj[1], m_w_ple_proj[1], v_w_ple_proj[1], s_proj1, "adam_proj1")
    res["w_out_b"] = upd(w_out_b, m_w_out_b, v_w_out_b, s_out_b, "adam_out_b")
    s_q, s_kv = _exchange_wait(pend_b, res["w_out_b"][0], True, "scatter_wait_b")
    res["w_q"] = upd(w_q, m_w_q, v_w_q, s_q, "adam_q")
    res["w_kv"] = upd(w_kv, m_w_kv, v_w_kv, s_kv, "adam_kv")
    s_up0, s_down0, s_gate0, s_proj0 = _exchange_wait(pend_c, res["w_kv"][0], True, "scatter_wait_c")
    per["w_up", 0] = upd(w_up[0], m_w_up[0], v_w_up[0], s_up0, "adam_up0")
    per["w_down", 0] = upd(w_down[0], m_w_down[0], v_w_down[0], s_down0, "adam_down0")
    per["w_ple_gate", 0] = upd(w_ple_gate[0], m_w_ple_gate[0], v_w_ple_gate[0], s_gate0, "adam_gate0")
    per["w_ple_proj", 0] = upd(w_ple_proj[0], m_w_ple_proj[0], v_w_ple_proj[0], s_proj0, "adam_proj0")
    s_in_a, s_out_a = _exchange_wait(pend_d, per["w_ple_proj", 0][0], True, "scatter_wait_d")
    res["w_in_a"] = upd(w_in_a, m_w_in_a, v_w_in_a, s_in_a, "adam_in_a")
    res["w_out_a"] = upd(w_out_a, m_w_out_a, v_w_out_a, s_out_a, "adam_out_a")
    for nm in ("w_up", "w_down", "w_ple_gate", "w_ple_proj"):
        res[nm] = [jnp.stack([per[nm, l][t] for l in range(n_layers)]) for t in range(4)]

    (g_all,) = _exchange_wait(pend_s[0], res["w_out_a"][0], False, "small_grads_wait")
    outs = _adamw(w_pack, m_pack, v_pack, g_all, "adam_small")
    row = 0
    for nm, g, w, m, v in small:
        nrows = _rows128(w).shape[0]
        res[nm] = [o[row:row + nrows].reshape(-1)[:w.size].reshape(w.shape) for o in outs]
        row += nrows
    for nm, g, w, m, v in sharded_vec:
        res[nm] = [lax.dynamic_slice_in_dim(o[row:row + NDEV], me, 1, axis=0) for o in outs]
        row += NDEV

    names = ["ln_mix_a", "w_in_a", "g_v_a", "w_spatial", "b_spatial", "w_out_a", "ln_kv", "w_kv", "g_k", "ln_mix_b",
             "w_q", "g_q", "w_out_b", "ln_mlp", "w_up", "w_down", "ln_ple", "w_ple_gate", "w_ple_proj"]
    out = [loss, dx0[None]]
    for t in range(4):
        out += [res[nm][t] for nm in names]
    return tuple(out)
```

```python
import functools
import math

import jax
import jax.numpy as jnp
from jax import lax
from jax.experimental import pallas as pl
from jax.experimental.pallas import tpu as pltpu

F32 = jnp.float32
MXU = jnp.bfloat16
COMM = jnp.bfloat16
EPS = 1e-6
NDEV = 8
HEAD_DIM = 64
CHUNK = 128
GROUPS = 8
QBLK = 128
SCALE = HEAD_DIM ** -0.5
TOKEN_TILE = 256
ADAM_LR = 0.001
ADAM_B1 = 0.9
ADAM_B2 = 0.999
ADAM_EPS = 1e-08
ADAM_WD = 0.01
ADAM_STEP = 10
MESH = pl.DeviceIdType.MESH


def _call(body, **kw):
    return pl.pallas_call(body, **kw)


def _params(vmem_mb, n_axes=1):
    return pltpu.CompilerParams(dimension_semantics=("arbitrary",) * n_axes,
                                vmem_limit_bytes=vmem_mb << 20)


def _tile(tm, n):
    return pl.BlockSpec((tm, n), lambda i: (i, 0))


def _whole(shape):
    zeros = (0,) * len(shape)
    return pl.BlockSpec(shape, lambda i: zeros, pipeline_mode=pl.Buffered(1))


def _acc(shape):
    zeros = (0,) * len(shape)
    return pl.BlockSpec(shape, lambda i: zeros)


def _mm(a, b):
    return jnp.dot(a.astype(MXU), b.astype(MXU), preferred_element_type=F32)


def _mm_nt(a, b):
    return lax.dot_general(a.astype(MXU), b.astype(MXU), (((1,), (1,)), ((), ())),
                           preferred_element_type=F32)


def _mm_tn(a, b):
    return lax.dot_general(a.astype(MXU), b.astype(MXU), (((0,), (0,)), ((), ())),
                           preferred_element_type=F32)


def _split_dot(x, ones, terms=2):
    out = None
    for _ in range(terms):
        part = x.astype(MXU)
        x = x - part.astype(F32)
        d = jnp.dot(part, ones, preferred_element_type=F32)
        out = d if out is None else out + d
    return out


def _rms(x, g):
    rstd = lax.rsqrt(jnp.mean(x * x, axis=-1, keepdims=True) + EPS)
    xhat = x * rstd
    return xhat * g, xhat, rstd


def _rms_bwd(dh, xhat, rstd, g):
    dxh = dh * g
    dx = rstd * (dxh - xhat * jnp.mean(dxh * xhat, axis=-1, keepdims=True))
    dg = jnp.sum(dh * xhat, axis=0, keepdims=True)
    return dx, dg


_GELU_C = math.sqrt(2.0 / math.pi)


def _gelu(x):
    t = jnp.tanh(_GELU_C * (x + 0.044715 * (x * x * x)))
    return 0.5 * x * (1.0 + t)


def _gelu_and_grad(x):
    x2 = x * x
    t = jnp.tanh(_GELU_C * (x + 0.044715 * (x2 * x)))
    g = 0.5 * x * (1.0 + t)
    dg = 0.5 * (1.0 + t) + 0.5 * x * (1.0 - t * t) * (_GELU_C * (1.0 + 3.0 * 0.044715 * x2))
    return g, dg


def _softplus(z):
    return jnp.maximum(z, 0.0) + jnp.log1p(jnp.exp(-jnp.abs(z)))


def _tril_mask():
    row = lax.broadcasted_iota(jnp.int32, (CHUNK, CHUNK), 0)
    col = lax.broadcasted_iota(jnp.int32, (CHUNK, CHUNK), 1)
    return row >= col, row <= col


ANY_SPEC = pl.BlockSpec(memory_space=pl.ANY)


def _my_index():
    return 4 * lax.axis_index("x") + 2 * lax.axis_index("y") + lax.axis_index("c")


def _exchange_copies(srcs, lands, send_sems, recv_sems, scatter, arriving):
    x, y, c = lax.axis_index("x"), lax.axis_index("y"), lax.axis_index("c")
    me = 4 * x + 2 * y + c
    out = []
    for a in range(len(srcs)):
        for k in range(NDEV - 1):
            bits = k + 1
            px = 1 - x if (bits >> 2) & 1 else x
            py = 1 - y if (bits >> 1) & 1 else y
            pc = 1 - c if bits & 1 else c
            peer = 4 * px + 2 * py + pc
            src = srcs[a].at[peer] if scatter else srcs[a]
            out.append(pltpu.make_async_remote_copy(
                src_ref=src, dst_ref=lands[a].at[peer if arriving else me],
                send_sem=send_sems.at[a * (NDEV - 1) + k], recv_sem=recv_sems.at[a * (NDEV - 1) + k],
                device_id=(px, py, pc), device_id_type=MESH))
    return out


def _exchange_shapes(arrs, scatter):
    n = len(arrs)
    lands = [jax.ShapeDtypeStruct(a.shape if scatter else (NDEV,) + a.shape, a.dtype) for a in arrs]
    sems = [pltpu.SemaphoreType.DMA((n * (NDEV - 1),)), pltpu.SemaphoreType.DMA((n * (NDEV - 1),)),
            pltpu.SemaphoreType.DMA((n,))]
    return lands, sems


def _exchange_start(srcs, lands, sems, scatter):
    send_sems, recv_sems, local_sems = sems
    me = _my_index()
    for a in range(len(srcs)):
        pltpu.make_async_copy(srcs[a].at[me] if scatter else srcs[a], lands[a].at[me], local_sems.at[a]).start()
    for send in _exchange_copies(srcs, lands, send_sems, recv_sems, scatter, False):
        send.start()


def _exchange_finish(srcs, lands, sems, scatter):
    send_sems, recv_sems, local_sems = sems
    me = _my_index()
    for arrive in _exchange_copies(srcs, lands, send_sems, recv_sems, scatter, True):
        arrive.wait_recv()
    for send in _exchange_copies(srcs, lands, send_sems, recv_sems, scatter, False):
        send.wait_send()
    for a in range(len(srcs)):
        pltpu.make_async_copy(srcs[a].at[me] if scatter else srcs[a], lands[a].at[me], local_sems.at[a]).wait()


def _exchange(arrs, scatter, name):
    n = len(arrs)
    lands, sems = _exchange_shapes(arrs, scatter)

    def body(*refs):
        _exchange_start(refs[:n], refs[n:2 * n], refs[2 * n:], scatter)
        _exchange_finish(refs[:n], refs[n:2 * n], refs[2 * n:], scatter)

    return _call(body, name=name, out_shape=lands, in_specs=[ANY_SPEC] * n, out_specs=[ANY_SPEC] * n,
                 scratch_shapes=sems)(*arrs)


def _spatial_mix(vnb, ws_ref, bT_ref, mix_ref, tm):
    tri, _ = _tril_mask()
    for g in range(GROUPS):
        wm = jnp.where(tri, ws_ref[g], 0.0).astype(MXU)
        cols = slice(g * CHUNK, (g + 1) * CHUNK)
        for ch in range(tm // CHUNK):
            rows = slice(ch * CHUNK, (ch + 1) * CHUNK)
            mix_ref[rows, cols] = _mm(wm, vnb[rows, cols]) + bT_ref[:, g:g + 1]


def _sgu_fwd(x, ln, w_in, g_v, ws, bT, w_out, name):
    T, D = x.shape
    tm = min(TOKEN_TILE, T)
    nw = w_in.shape[2]

    def body(x_ref, ln_ref, win_ref, gv_ref, ws_ref, bT_ref, wout_ref, xo_ref, z_ref, h_ref, y_ref, mix_ref):
        xv = x_ref[...]
        h, _, _ = _rms(xv, ln_ref[...])
        hb = h.astype(MXU)
        h_ref[...] = hb
        for j in range(NDEV):
            z_ref[:, j * nw:(j + 1) * nw] = _mm(hb, win_ref[j])
        u = _gelu(z_ref[:, :D])
        gv = _gelu(z_ref[:, D:])
        vn, _, _ = _rms(gv, gv_ref[...])
        _spatial_mix(vn.astype(MXU), ws_ref, bT_ref, mix_ref, tm)
        y = (u * mix_ref[...]).astype(MXU)
        y_ref[...] = y
        xo_ref[...] = xv + _mm(y, wout_ref[...])

    return _call(
        body, name=name, grid=(T // tm,),
        out_shape=[jax.ShapeDtypeStruct((T, D), F32), jax.ShapeDtypeStruct((T, 2 * D), F32),
                   jax.ShapeDtypeStruct((T, D), MXU), jax.ShapeDtypeStruct((T, D), MXU)],
        in_specs=[_tile(tm, D), _whole(ln.shape), _whole(w_in.shape), _whole(g_v.shape), _whole(ws.shape),
                  _whole(bT.shape), _whole(w_out.shape)],
        out_specs=[_tile(tm, D), _tile(tm, 2 * D), _tile(tm, D), _tile(tm, D)],
        scratch_shapes=[pltpu.VMEM((tm, D), F32)],
        compiler_params=_params(40),
    )(x, ln, w_in, g_v, ws, bT, w_out)


def _mlp_fwd(x, ln, w_up, w_down, name):
    T, D = x.shape
    tm = min(TOKEN_TILE, T)
    nf = w_up.shape[2]
    F = nf * NDEV

    def body(x_ref, ln_ref, wup_ref, wdown_ref, xo_ref, pre_ref, h_ref):
        xv = x_ref[...]
        h, _, _ = _rms(xv, ln_ref[...])
        hb = h.astype(MXU)
        h_ref[...] = hb
        for j in range(NDEV):
            pre_ref[:, j * nf:(j + 1) * nf] = _mm(hb, wup_ref[j])
        a = jnp.maximum(pre_ref[...], 0.0)
        xo_ref[...] = xv + _mm(a * a, wdown_ref[...])

    return _call(
        body, name=name, grid=(T // tm,),
        out_shape=[jax.ShapeDtypeStruct((T, D), F32), jax.ShapeDtypeStruct((T, F), F32),
                   jax.ShapeDtypeStruct((T, D), MXU)],
        in_specs=[_tile(tm, D), _whole(ln.shape), _whole(w_up.shape), _whole(w_down.shape)],
        out_specs=[_tile(tm, D), _tile(tm, F), _tile(tm, D)],
        compiler_params=_params(52),
    )(x, ln, w_up, w_down)


def _ple_fwd(x, p, ln, w_g, w_pp, name):
    T, D = x.shape
    tm = min(TOKEN_TILE, T)
    npp = w_pp.shape[2]

    def body(x_ref, p_ref, ln_ref, wg_ref, wpp_ref, xo_ref, gate_ref, pp_ref, h_ref):
        xv = x_ref[...]
        h, _, _ = _rms(xv, ln_ref[...])
        hb = h.astype(MXU)
        h_ref[...] = hb
        gate = jax.nn.sigmoid(_mm(hb, wg_ref[...]))
        gate_ref[...] = gate
        pb = p_ref[...].astype(MXU)
        for j in range(NDEV):
            pp_ref[:, j * npp:(j + 1) * npp] = _mm(pb, wpp_ref[j])
        xo_ref[...] = xv + pp_ref[...] * gate

    return _call(
        body, name=name, grid=(T // tm,),
        out_shape=[jax.ShapeDtypeStruct((T, D), F32), jax.ShapeDtypeStruct((T, D), F32),
                   jax.ShapeDtypeStruct((T, D), F32), jax.ShapeDtypeStruct((T, D), MXU)],
        in_specs=[_tile(tm, D), _tile(tm, p.shape[1]), _whole(ln.shape), _whole(w_g.shape), _whole(w_pp.shape)],
        out_specs=[_tile(tm, D), _tile(tm, D), _tile(tm, D), _tile(tm, D)],
        compiler_params=_params(32),
    )(x, p, ln, w_g, w_pp)


def _head_ones():
    row = lax.broadcasted_iota(jnp.int32, (128, 128), 0)
    col = lax.broadcasted_iota(jnp.int32, (128, 128), 1)
    return (jnp.right_shift(row, 6) == jnp.right_shift(col, 6)).astype(MXU)


def _head_rms(x, g, ones):
    rstd = lax.rsqrt(_split_dot(x * x, ones, 3) * (1.0 / HEAD_DIM) + EPS)
    xhat = x * rstd
    return xhat * g, xhat, rstd


def _head_rms_bwd(dh, xhat, rstd, g, ones):
    dxh = dh * g
    mean = _split_dot(dxh * xhat, ones, 3) * (1.0 / HEAD_DIM)
    return rstd * (dxh - xhat * mean), jnp.sum(dh * xhat, axis=0, keepdims=True)


def _qkv_fwd(x, ln_q, ln_kv, g_q, g_k, w_q, w_kv, name):
    T, D = x.shape
    tm = min(TOKEN_TILE, T)
    nk = w_kv.shape[2]
    half = NDEV // 2

    def body(x_ref, lnq_ref, lnkv_ref, gq_ref, gk_ref, wq_ref, wkv_ref,
             q_ref, k_ref, v_ref, qpre_ref, kpre_ref, hq_ref, hkv_ref):
        xv = x_ref[...]
        _, xhat, _ = _rms(xv, lnq_ref[...])
        hq = (xhat * lnq_ref[...]).astype(MXU)
        hkv = (xhat * lnkv_ref[...]).astype(MXU)
        hq_ref[...] = hq
        hkv_ref[...] = hkv
        qpre_ref[...] = _mm(hq, wq_ref[...])
        for j in range(half):
            kpre_ref[:, j * nk:(j + 1) * nk] = _mm(hkv, wkv_ref[j])
            v_ref[:, j * nk:(j + 1) * nk] = _mm(hkv, wkv_ref[half + j]).astype(MXU)
        ones = _head_ones()
        for b in range(D // 128):
            cols = slice(b * 128, (b + 1) * 128)
            qn, _, _ = _head_rms(qpre_ref[:, cols], gq_ref[:, cols], ones)
            q_ref[:, cols] = (qn * SCALE).astype(MXU)
            kn, _, _ = _head_rms(kpre_ref[:, cols], gk_ref[:, cols], ones)
            k_ref[:, cols] = kn.astype(MXU)

    return _call(
        body, name=name, grid=(T // tm,),
        out_shape=[jax.ShapeDtypeStruct((T, D), MXU)] * 3 + [jax.ShapeDtypeStruct((T, D), F32)] * 2
        + [jax.ShapeDtypeStruct((T, D), MXU)] * 2,
        in_specs=[_tile(tm, D), _whole(ln_q.shape), _whole(ln_kv.shape), _whole(g_q.shape), _whole(g_k.shape),
                  _whole(w_q.shape), _whole(w_kv.shape)],
        out_specs=[_tile(tm, D)] * 7,
        compiler_params=_params(40),
    )(x, ln_q, ln_kv, g_q, g_k, w_q, w_kv)


SB_KEYS = 2 * QBLK


def _sb_consts():
    row = lax.broadcasted_iota(jnp.int32, (QBLK, QBLK), 0)
    col = lax.broadcasted_iota(jnp.int32, (QBLK, QBLK), 1)
    lane = lax.broadcasted_iota(jnp.int32, (QBLK, 128), 1)
    ones = jnp.ones((QBLK, QBLK), MXU)
    later = jnp.concatenate([(row > col).astype(MXU), ones], axis=1)
    later_eq = jnp.concatenate([(row >= col).astype(MXU), ones], axis=1)
    return later, later_eq, lane < HEAD_DIM


def _sb_causal(i, off):
    row = lax.broadcasted_iota(jnp.int32, (QBLK, SB_KEYS), 0)
    col = lax.broadcasted_iota(jnp.int32, (QBLK, SB_KEYS), 1)
    return (col + off) < (row + i * QBLK)


MASKED_LOG = -1e30


def _sb_terms(x, terms):
    x = jnp.concatenate([x[:, :QBLK], x[:, QBLK:]], axis=0)
    out = []
    for _ in range(terms):
        part = x.astype(MXU)
        x = x - part.astype(F32)
        out.append(part)
    return tuple(out)


def _sb_suffix(parts, ones, carry):
    s = None
    for part in parts:
        d = jnp.dot(part, ones, preferred_element_type=F32)
        s = d if s is None else s + d
    s_lo, sum_lo, s_hi, sum_hi = s[:QBLK, :QBLK], s[:QBLK, QBLK:], s[QBLK:, :QBLK], s[QBLK:, QBLK:]
    return jnp.concatenate([s_lo + (carry + sum_hi), s_hi + carry], axis=1), carry + (sum_lo + sum_hi)


def _sb_scores(z, mask):
    sp = _softplus(z)
    l, log_sig = -sp, z - sp
    if mask is not None:
        l = jnp.where(mask, l, 0.0)
        log_sig = jnp.where(mask, log_sig, MASKED_LOG)
    return log_sig, _sb_terms(l, 2)


def _sb_weights(staged, later, c_l):
    log_sig, parts = staged
    b, c_l = _sb_suffix(parts, later, c_l)
    return jnp.exp(log_sig + b), c_l


DEAD_LOG = -104.0


def _sb_alive(carry):
    return (jnp.max(jnp.maximum(carry[0][0], carry[1][0])) > DEAD_LOG).astype(jnp.int32)


def _ride_along(refs, n, scatter):
    step = pl.program_id(0) * pl.num_programs(1) + pl.program_id(1)
    srcs, lands, sems = refs[:n], refs[n:2 * n], refs[2 * n:]

    @pl.when(step == 0)
    def _():
        _exchange_start(srcs, lands, sems, scatter)

    def finish():
        @pl.when(step == pl.num_programs(0) * pl.num_programs(1) - 1)
        def _():
            _exchange_finish(srcs, lands, sems, scatter)

    return finish


def _sb_fwd(q, k, v, cargo, name):
    T, D = q.shape
    nc = len(cargo)
    lands, sems = _exchange_shapes(cargo, False)

    def body(q_ref, k_ref, v_ref, *rest):
        o_ref = rest[nc]
        finish = _ride_along(rest[:nc] + rest[nc + 1:], nc, False)
        i = pl.program_id(1)
        n_steps = (i + 2) // 2
        later, _, first = _sb_consts()
        qv = q_ref[...]
        zero = jnp.zeros_like(qv)
        qs = (jnp.where(first, qv, zero), jnp.where(first, zero, qv))

        def scores(j, masked):
            off = pl.multiple_of(j * SB_KEYS, SB_KEYS)
            kj = k_ref[pl.ds(off, SB_KEYS), :]
            return tuple(_sb_scores(_mm_nt(qs[hh], kj), _sb_causal(i, off) if masked else None) for hh in range(2))

        def weigh(j, staged, carry):
            off = pl.multiple_of(j * SB_KEYS, SB_KEYS)
            vj = v_ref[pl.ds(off, SB_KEYS), :]
            out = []
            for hh in range(2):
                c_l, acc = carry[hh]
                a, c_l = _sb_weights(staged[hh], later, c_l)
                out.append((c_l, acc + _mm(a, vj)))
            return tuple(out)

        def step(state):
            jj, _, staged, carry = state
            j = n_steps - 1 - jj
            carry = weigh(j, staged, carry)
            return jj + 1, _sb_alive(carry), scores(j - 1, False), carry

        init = tuple((jnp.zeros((QBLK, 128), F32), jnp.zeros((QBLK, 128), F32)) for _ in range(2))
        jj, _, staged, carry = lax.while_loop(lambda s: (s[0] < n_steps - 1) & (s[1] > 0), step,
                                             (jnp.int32(0), jnp.int32(1), scores(n_steps - 1, True), init))
        carry = weigh(n_steps - 1 - jj, staged, carry)
        o_ref[...] = jnp.where(first, carry[0][1], carry[1][1])
        finish()

    qblk = pl.BlockSpec((QBLK, 128), lambda h, i: (i, h))
    kblk = pl.BlockSpec((T, 128), lambda h, i: (0, h))
    outs = _call(
        body, name=name, grid=(D // 128, T // QBLK), out_shape=[jax.ShapeDtypeStruct((T, D), F32)] + lands,
        in_specs=[qblk, kblk, kblk] + [ANY_SPEC] * nc, out_specs=[qblk] + [ANY_SPEC] * nc, scratch_shapes=sems,
        compiler_params=_params(32, 2),
    )(q, k, v, *cargo)
    return outs[0], outs[1:]


def _sb_bwd(q, k, v, o, do, cargo, name):
    T, D = q.shape
    nc = len(cargo)
    lands, sems = _exchange_shapes(cargo, True)

    def body(q_ref, k_ref, v_ref, o_ref, do_ref, *rest):
        dq_ref, dk_ref, dv_ref = rest[nc:nc + 3]
        finish = _ride_along(rest[:nc] + rest[nc + 3:], nc, True)
        i = pl.program_id(1)

        @pl.when(i == 0)
        def _():
            dk_ref[...] = jnp.zeros_like(dk_ref)
            dv_ref[...] = jnp.zeros_like(dv_ref)

        n_steps = (i + 2) // 2
        later, later_eq, first = _sb_consts()
        qv = q_ref[...]
        dob = do_ref[...].astype(MXU)
        zero = jnp.zeros_like(qv)
        qs = (jnp.where(first, qv, zero), jnp.where(first, zero, qv))
        dos = (jnp.where(first, dob, zero), jnp.where(first, zero, dob))
        prod = o_ref[...] * dob.astype(F32)
        ones = jnp.ones((128, 128), MXU)
        totals = (_split_dot(jnp.where(first, prod, 0.0), ones, 3), _split_dot(jnp.where(first, 0.0, prod), ones, 3))

        def scores(j, masked):
            off = pl.multiple_of(j * SB_KEYS, SB_KEYS)
            kj = k_ref[pl.ds(off, SB_KEYS), :]
            vj = v_ref[pl.ds(off, SB_KEYS), :]
            mask = _sb_causal(i, off) if masked else None
            return tuple(_sb_scores(_mm_nt(qs[hh], kj), mask) + (_mm_nt(dos[hh], vj),) for hh in range(2))

        def grads(j, staged, carry):
            off = pl.multiple_of(j * SB_KEYS, SB_KEYS)
            kj = k_ref[pl.ds(off, SB_KEYS), :]
            out = []
            dk_blk = dv_blk = None
            for hh in range(2):
                c_l, c_e, dq = carry[hh]
                a, c_l = _sb_weights(staged[hh][:2], later, c_l)
                ab = a.astype(MXU)
                e = ab.astype(F32) * staged[hh][2]
                from_here, c_e = _sb_suffix(_sb_terms(e, 3), later_eq, c_e)
                before = jnp.concatenate([totals[hh], totals[hh]], axis=1) - from_here
                sig = jnp.exp(staged[hh][0])
                dzb = (e * (1.0 - sig) - sig * before).astype(MXU)
                dk_h = _mm_tn(dzb, qs[hh])
                dv_h = _mm_tn(ab, dos[hh])
                dk_blk = dk_h if dk_blk is None else dk_blk + dk_h
                dv_blk = dv_h if dv_blk is None else dv_blk + dv_h
                out.append((c_l, c_e, dq + _mm(dzb, kj)))
            dk_ref[pl.ds(off, SB_KEYS), :] += dk_blk
            dv_ref[pl.ds(off, SB_KEYS), :] += dv_blk
            return tuple(out)

        def step(state):
            jj, _, staged, carry = state
            j = n_steps - 1 - jj
            carry = grads(j, staged, carry)
            return jj + 1, _sb_alive(carry), scores(j - 1, False), carry

        init = tuple((jnp.zeros((QBLK, 128), F32),) * 3 for _ in range(2))
        jj, _, staged, carry = lax.while_loop(lambda s: (s[0] < n_steps - 1) & (s[1] > 0), step,
                                             (jnp.int32(0), jnp.int32(1), scores(n_steps - 1, True), init))
        carry = grads(n_steps - 1 - jj, staged, carry)
        dq_ref[...] = jnp.where(first, carry[0][2], carry[1][2]) * SCALE
        finish()

    qblk = pl.BlockSpec((QBLK, 128), lambda h, i: (i, h))
    kblk = pl.BlockSpec((T, 128), lambda h, i: (0, h))
    full = jax.ShapeDtypeStruct((T, D), F32)
    outs = _call(
        body, name=name, grid=(D // 128, T // QBLK), out_shape=[full, full, full] + lands,
        in_specs=[qblk, kblk, kblk, qblk, qblk] + [ANY_SPEC] * nc, out_specs=[qblk, kblk, kblk] + [ANY_SPEC] * nc,
        scratch_shapes=sems, compiler_params=_params(32, 2),
    )(q, k, v, o, do, *cargo)
    return outs[0], outs[1], outs[2], outs[3:]


def _proj_res(x, a, w, name):
    T, D = x.shape
    tm = min(TOKEN_TILE, T)

    def body(x_ref, a_ref, w_ref, o_ref):
        o_ref[...] = x_ref[...] + _mm(a_ref[...], w_ref[...])

    return _call(
        body, name=name, grid=(T // tm,), out_shape=jax.ShapeDtypeStruct((T, D), F32),
        in_specs=[_tile(tm, D), _tile(tm, a.shape[1]), _whole(w.shape)], out_specs=_tile(tm, D),
        compiler_params=_params(32),
    )(x, a, w)


def _proj_nt(g, w, name):
    T = g.shape[0]
    K = w.shape[0]
    tm = min(TOKEN_TILE, T)

    def body(g_ref, w_ref, o_ref):
        o_ref[...] = _mm_nt(g_ref[...], w_ref[...])

    return _call(
        body, name=name, grid=(T // tm,), out_shape=jax.ShapeDtypeStruct((T, K), F32),
        in_specs=[_tile(tm, g.shape[1]), _whole(w.shape)], out_specs=_tile(tm, K),
        compiler_params=_params(32),
    )(g, w)


def _loss_grad(y, tgt, name):
    T, D = y.shape
    tm = min(TOKEN_TILE, T)

    def body(y_ref, t_ref, dy_ref, loss_ref):
        @pl.when(pl.program_id(0) == 0)
        def _():
            loss_ref[...] = jnp.zeros_like(loss_ref)
        diff = y_ref[...] - t_ref[...]
        dy_ref[...] = diff * (1.0 / D)
        rows = jnp.sum(diff * diff, axis=1, keepdims=True) * (1.0 / D)
        loss_ref[...] += 0.5 * jnp.sum(rows, axis=0, keepdims=True)

    return _call(
        body, name=name, grid=(T // tm,),
        out_shape=[jax.ShapeDtypeStruct((T, D), F32), jax.ShapeDtypeStruct((1, 1), F32)],
        in_specs=[_tile(tm, D), _tile(tm, D)], out_specs=[_tile(tm, D), _acc((1, 1))],
        compiler_params=_params(32),
    )(y, tgt)


def _ple_bwd(dx, x, gate, pp, ln, w_g, name):
    T, D = x.shape
    tm = min(TOKEN_TILE, T)

    def body(dx_ref, x_ref, gate_ref, pp_ref, ln_ref, wg_ref, dxo_ref, dpp_ref, dgp_ref, dln_ref):
        @pl.when(pl.program_id(0) == 0)
        def _():
            dln_ref[...] = jnp.zeros_like(dln_ref)
        dxv = dx_ref[...]
        gate = gate_ref[...]
        _, xhat, rstd = _rms(x_ref[...], ln_ref[...])
        dpp_ref[...] = (dxv * gate).astype(MXU)
        dgp = (dxv * pp_ref[...] * gate * (1.0 - gate)).astype(MXU)
        dgp_ref[...] = dgp
        dxn, dln = _rms_bwd(_mm_nt(dgp, wg_ref[...]), xhat, rstd, ln_ref[...])
        dln_ref[...] += dln
        dxo_ref[...] = dxn + dxv

    return _call(
        body, name=name, grid=(T // tm,),
        out_shape=[jax.ShapeDtypeStruct((T, D), F32), jax.ShapeDtypeStruct((T, D), MXU),
                   jax.ShapeDtypeStruct((T, D), MXU), jax.ShapeDtypeStruct(ln.shape, F32)],
        in_specs=[_tile(tm, D)] * 4 + [_whole(ln.shape), _whole(w_g.shape)],
        out_specs=[_tile(tm, D), _tile(tm, D), _tile(tm, D), _acc(ln.shape)],
        compiler_params=_params(32),
    )(dx, x, gate, pp, ln, w_g)


def _mlp_bwd(dx, x, pre, ln, w_up, w_down, name):
    T, D = x.shape
    tm = min(TOKEN_TILE, T)
    nf = w_up.shape[2]
    F = nf * NDEV

    def body(dx_ref, x_ref, pre_ref, ln_ref, wup_ref, wdown_ref, dxo_ref, dpre_ref, s_ref, dln_ref):
        @pl.when(pl.program_id(0) == 0)
        def _():
            dln_ref[...] = jnp.zeros_like(dln_ref)
        dxv = dx_ref[...]
        _, xhat, rstd = _rms(x_ref[...], ln_ref[...])
        a = jnp.maximum(pre_ref[...], 0.0)
        s_ref[...] = (a * a).astype(MXU)
        dpre_ref[...] = (_mm_nt(dxv, wdown_ref[...]) * (2.0 * a)).astype(MXU)
        dh = _mm_nt(dpre_ref[:, :nf], wup_ref[0])
        for j in range(1, NDEV):
            dh += _mm_nt(dpre_ref[:, j * nf:(j + 1) * nf], wup_ref[j])
        dxn, dln = _rms_bwd(dh, xhat, rstd, ln_ref[...])
        dln_ref[...] += dln
        dxo_ref[...] = dxn + dxv

    return _call(
        body, name=name, grid=(T // tm,),
        out_shape=[jax.ShapeDtypeStruct((T, D), F32), jax.ShapeDtypeStruct((T, F), MXU),
                   jax.ShapeDtypeStruct((T, F), MXU), jax.ShapeDtypeStruct(ln.shape, F32)],
        in_specs=[_tile(tm, D), _tile(tm, D), _tile(tm, F), _whole(ln.shape), _whole(w_up.shape),
                  _whole(w_down.shape)],
        out_specs=[_tile(tm, D), _tile(tm, F), _tile(tm, F), _acc(ln.shape)],
        compiler_params=_params(56),
    )(dx, x, pre, ln, w_up, w_down)


def _qkv_bwd(dx, x, dq, dk, dv, q_pre, k_pre, ln_q, ln_kv, g_q, g_k, w_q, w_kv, name):
    T, D = x.shape
    tm = min(TOKEN_TILE, T)
    nk = w_kv.shape[2]
    n_tiles = T // tm

    def body(dx_ref, x_ref, dq_ref, dk_ref, dv_ref, qpre_ref, kpre_ref, lnq_ref, lnkv_ref, gq_ref, gk_ref,
             wq_ref, wkv_ref, dxo_ref, dqp_ref, dkv_ref, dlnq_ref, dlnkv_ref, dgq_ref, dgk_ref, gq_acc, gk_acc):
        i = pl.program_id(0)

        @pl.when(i == 0)
        def _():
            dlnq_ref[...] = jnp.zeros_like(dlnq_ref)
            dlnkv_ref[...] = jnp.zeros_like(dlnkv_ref)
            gq_acc[...] = jnp.zeros_like(gq_acc)
            gk_acc[...] = jnp.zeros_like(gk_acc)

        ones = _head_ones()
        for b in range(D // 128):
            cols = slice(b * 128, (b + 1) * 128)
            _, xh, rs = _head_rms(qpre_ref[:, cols], gq_ref[:, cols], ones)
            d, dg = _head_rms_bwd(dq_ref[:, cols], xh, rs, gq_ref[:, cols], ones)
            dqp_ref[:, cols] = d.astype(MXU)
            gq_acc[:, cols] += dg
            _, xh, rs = _head_rms(kpre_ref[:, cols], gk_ref[:, cols], ones)
            d, dg = _head_rms_bwd(dk_ref[:, cols], xh, rs, gk_ref[:, cols], ones)
            dkv_ref[:, cols] = d.astype(MXU)
            gk_acc[:, cols] += dg
        dkv_ref[:, D:] = dv_ref[...].astype(MXU)

        _, xhat, rstd = _rms(x_ref[...], lnq_ref[...])
        dhq = _mm_nt(dqp_ref[...], wq_ref[...])
        dhkv = _mm_nt(dkv_ref[:, :nk], wkv_ref[0])
        for j in range(1, NDEV):
            dhkv += _mm_nt(dkv_ref[:, j * nk:(j + 1) * nk], wkv_ref[j])
        dxq, dlnq = _rms_bwd(dhq, xhat, rstd, lnq_ref[...])
        dxkv, dlnkv = _rms_bwd(dhkv, xhat, rstd, lnkv_ref[...])
        dlnq_ref[...] += dlnq
        dlnkv_ref[...] += dlnkv
        dxo_ref[...] = dx_ref[...] + dxq + dxkv

        @pl.when(i == n_tiles - 1)
        def _():
            row = lax.broadcasted_iota(jnp.int32, (D, 128), 0)
            col = lax.broadcasted_iota(jnp.int32, (D, 128), 1)
            fold = (jnp.bitwise_and(row, HEAD_DIM - 1) == col).astype(MXU)
            dgq_ref[...] = _split_dot(jnp.broadcast_to(gq_acc[...], (8, D)), fold, 3)
            dgk_ref[...] = _split_dot(jnp.broadcast_to(gk_acc[...], (8, D)), fold, 3)

    small = jax.ShapeDtypeStruct((8, 128), F32)
    return _call(
        body, name=name, grid=(n_tiles,),
        out_shape=[jax.ShapeDtypeStruct((T, D), F32), jax.ShapeDtypeStruct((T, D), MXU),
                   jax.ShapeDtypeStruct((T, 2 * D), MXU), jax.ShapeDtypeStruct(ln_q.shape, F32),
                   jax.ShapeDtypeStruct(ln_kv.shape, F32), small, small],
        in_specs=[_tile(tm, D)] * 7 + [_whole(ln_q.shape), _whole(ln_kv.shape), _whole(g_q.shape),
                                       _whole(g_k.shape), _whole(w_q.shape), _whole(w_kv.shape)],
        out_specs=[_tile(tm, D), _tile(tm, D), _tile(tm, 2 * D), _acc(ln_q.shape), _acc(ln_kv.shape),
                   _acc((8, 128)), _acc((8, 128))],
        scratch_shapes=[pltpu.VMEM((1, D), F32), pltpu.VMEM((1, D), F32)],
        compiler_params=_params(48),
    )(dx, x, dq, dk, dv, q_pre, k_pre, ln_q, ln_kv, g_q, g_k, w_q, w_kv)


def _sgu_bwd(dx, x, z, ln, w_in, g_v, ws, wsT, bT, w_out, name):
    T, D = x.shape
    tm = min(TOKEN_TILE, T)
    nw = w_in.shape[2]

    def body(dx_ref, x_ref, z_ref, ln_ref, win_ref, gv_ref, ws_ref, wsT_ref, bT_ref, wout_ref,
             dxo_ref, dz_ref, dws_ref, dbT_ref, dln_ref, dgv_ref, mix_ref, dvn_ref):
        @pl.when(pl.program_id(0) == 0)
        def _():
            dws_ref[...] = jnp.zeros_like(dws_ref)
            dbT_ref[...] = jnp.zeros_like(dbT_ref)
            dln_ref[...] = jnp.zeros_like(dln_ref)
            dgv_ref[...] = jnp.zeros_like(dgv_ref)
        dxv = dx_ref[...]
        _, xhat, rstd = _rms(x_ref[...], ln_ref[...])
        u, du = _gelu_and_grad(z_ref[:, :D])
        gv, dgv = _gelu_and_grad(z_ref[:, D:])
        vn, vhat, rstd_v = _rms(gv, gv_ref[...])
        vnb = vn.astype(MXU)
        _spatial_mix(vnb, ws_ref, bT_ref, mix_ref, tm)
        dy = _mm_nt(dxv, wout_ref[...])
        d_u = dy * mix_ref[...]
        d_mix = dy * u
        dmb = d_mix.astype(MXU)
        tri, triT = _tril_mask()
        for g in range(GROUPS):
            wmT = jnp.where(triT, wsT_ref[g], 0.0).astype(MXU)
            cols = slice(g * CHUNK, (g + 1) * CHUNK)
            for ch in range(tm // CHUNK):
                rows = slice(ch * CHUNK, (ch + 1) * CHUNK)
                dm = dmb[rows, cols]
                dws_ref[g] += jnp.where(tri, _mm_nt(dm, vnb[rows, cols]), 0.0)
                dbT_ref[:, g:g + 1] += jnp.sum(d_mix[rows, cols], axis=1, keepdims=True)
                dvn_ref[rows, cols] = _mm(wmT, dm)
        d_gv, dg = _rms_bwd(dvn_ref[...], vhat, rstd_v, gv_ref[...])
        dgv_ref[...] += dg
        dz_ref[:, :D] = (d_u * du).astype(MXU)
        dz_ref[:, D:] = (d_gv * dgv).astype(MXU)
        dh = _mm_nt(dz_ref[:, :nw], win_ref[0])
        for j in range(1, NDEV):
            dh += _mm_nt(dz_ref[:, j * nw:(j + 1) * nw], win_ref[j])
        dxn, dln = _rms_bwd(dh, xhat, rstd, ln_ref[...])
        dln_ref[...] += dln
        dxo_ref[...] = dxn + dxv

    return _call(
        body, name=name, grid=(T // tm,),
        out_shape=[jax.ShapeDtypeStruct((T, D), F32), jax.ShapeDtypeStruct((T, 2 * D), MXU),
                   jax.ShapeDtypeStruct(ws.shape, F32), jax.ShapeDtypeStruct(bT.shape, F32),
                   jax.ShapeDtypeStruct(ln.shape, F32), jax.ShapeDtypeStruct(g_v.shape, F32)],
        in_specs=[_tile(tm, D), _tile(tm, D), _tile(tm, 2 * D), _whole(ln.shape), _whole(w_in.shape),
                  _whole(g_v.shape), _whole(ws.shape), _whole(wsT.shape), _whole(bT.shape), _whole(w_out.shape)],
        out_specs=[_tile(tm, D), _tile(tm, 2 * D), _acc(ws.shape), _acc(bT.shape), _acc(ln.shape),
                   _acc(g_v.shape)],
        scratch_shapes=[pltpu.VMEM((tm, D), F32), pltpu.VMEM((tm, D), F32)],
        compiler_params=_params(48),
    )(dx, x, z, ln, w_in, g_v, ws, wsT, bT, w_out)


def _wgrad_rows(a, g, name):
    T, K = a.shape
    N = g.shape[1]
    kb = K // NDEV

    def body(a_ref, g_ref, o_ref):
        o_ref[...] = _mm_tn(a_ref[...], g_ref[...]).astype(COMM)

    return _call(
        body, name=name, grid=(NDEV,), out_shape=jax.ShapeDtypeStruct((K, N), COMM),
        in_specs=[pl.BlockSpec((T, kb), lambda j: (0, j)), _whole(g.shape)],
        out_specs=pl.BlockSpec((kb, N), lambda j: (j, 0)),
        compiler_params=_params(40),
    )(a, g).reshape(NDEV, kb, N)


def _wgrad_cols(a, g, name):
    T, K = a.shape
    N = g.shape[1]
    nb = N // NDEV

    def body(a_ref, g_ref, o_ref):
        o_ref[...] = _mm_tn(a_ref[...], g_ref[...]).astype(COMM)

    return _call(
        body, name=name, grid=(NDEV,), out_shape=jax.ShapeDtypeStruct((NDEV, K, nb), COMM),
        in_specs=[_whole(a.shape), pl.BlockSpec((T, nb), lambda j: (0, j))],
        out_specs=pl.BlockSpec((None, K, nb), lambda j: (j, 0, 0)),
        compiler_params=_params(40),
    )(a, g)


def _adamw(w, m, v, slots, name):
    R, C = w.shape
    n = slots.shape[0]
    tr = math.gcd(R, max(8, (128 * 1024) // C))
    if tr < 64:
        tr = R
    bc1 = 1.0 - ADAM_B1 ** ADAM_STEP
    bc2 = 1.0 - ADAM_B2 ** ADAM_STEP

    def body(w_ref, m_ref, v_ref, s_ref, g_ref, d_ref, mo_ref, vo_ref):
        g = s_ref[0].astype(F32)
        for j in range(1, n):
            g = g + s_ref[j].astype(F32)
        mn = ADAM_B1 * m_ref[...] + (1.0 - ADAM_B1) * g
        vn = ADAM_B2 * v_ref[...] + (1.0 - ADAM_B2) * (g * g)
        g_ref[...] = g
        mo_ref[...] = mn
        vo_ref[...] = vn
        d_ref[...] = -ADAM_LR * ((mn / bc1) / (jnp.sqrt(vn / bc2) + ADAM_EPS) + ADAM_WD * w_ref[...])

    blk = pl.BlockSpec((tr, C), lambda i: (i, 0))
    out = jax.ShapeDtypeStruct((R, C), F32)
    return _call(
        body, name=name, grid=(R // tr,), out_shape=[out, out, out, out],
        in_specs=[blk, blk, blk, pl.BlockSpec((n, tr, C), lambda i: (0, i, 0))], out_specs=[blk] * 4,
        compiler_params=_params(32),
    )(w, m, v, slots)


def _rows128(a):
    flat = a.reshape(-1)
    rows = -(-flat.shape[0] // 1024) * 8
    flat = jnp.pad(flat, (0, rows * 128 - flat.shape[0]))
    return flat.reshape(rows, 128)


def kernel(x, p, ln_mix_a, w_in_a, g_v_a, w_spatial, b_spatial, w_out_a, ln_kv, w_kv, g_k, ln_mix_b, w_q, g_q, w_out_b, ln_mlp, w_up, w_down, ln_ple, w_ple_gate, w_ple_proj, loss_target, m_ln_mix_a, m_w_in_a, m_g_v_a, m_w_spatial, m_b_spatial, m_w_out_a, m_ln_kv, m_w_kv, m_g_k, m_ln_mix_b, m_w_q, m_g_q, m_w_out_b, m_ln_mlp, m_w_up, m_w_down, m_ln_ple, m_w_ple_gate, m_w_ple_proj, v_ln_mix_a, v_w_in_a, v_g_v_a, v_w_spatial, v_b_spatial, v_w_out_a, v_ln_kv, v_w_kv, v_g_k, v_ln_mix_b, v_w_q, v_g_q, v_w_out_b, v_ln_mlp, v_w_up, v_w_down, v_ln_ple, v_w_ple_gate, v_w_ple_proj):
    me = 4 * lax.axis_index("x") + 2 * lax.axis_index("y") + lax.axis_index("c")
    D = x.shape[2]
    x0, tgt = x[0], loss_target[0]
    n_layers = w_up.shape[0]

    c = lambda w: w.astype(COMM)
    first = [c(w_in_a[0]), c(w_out_a[0]), ln_mix_a, g_v_a, c(w_up[0]), c(w_down[0]), c(w_ple_gate[0]),
             c(w_ple_proj[0]), c(w_q[0]), c(w_kv)]
    second = [c(w_out_b[0]), c(w_up[1]), c(w_down[1]), c(w_ple_gate[1]), c(w_ple_proj[1])]
    W_in, W_out_a, ln_a, gv_a, W_up0, W_down0, W_g0, W_pp0, W_q, W_kv = _exchange(first, False, "gather_first")
    W_out_a, ln_a, gv_a = W_out_a.reshape(-1, D), ln_a.reshape(1, D), gv_a.reshape(1, D)
    W_down0, W_g0, W_q = W_down0.reshape(-1, D), W_g0.reshape(-1, D), W_q.reshape(-1, D)
    ws = w_spatial[0]
    wsT = jnp.swapaxes(ws, 1, 2)
    bT = b_spatial[0].T
    ln_kv2, ln_b = ln_kv.reshape(1, D), ln_mix_b
    gk2 = jnp.tile(g_k.reshape(1, HEAD_DIM), (1, D // HEAD_DIM))
    gq2 = jnp.tile(g_q, (1, D // HEAD_DIM))
    ln_m = [ln_mlp[l:l + 1] for l in range(n_layers)]
    ln_p = [ln_ple[l:l + 1] for l in range(n_layers)]

    x1, z, h_a, y_a = _sgu_fwd(x0, ln_a, W_in, gv_a, ws, bT, W_out_a, "sgu_fwd")
    x2, pre0, hm0 = _mlp_fwd(x1, ln_m[0], W_up0, W_down0, "mlp_fwd0")
    x3, gate0, pp0, hp0 = _ple_fwd(x2, p[0, 0], ln_p[0], W_g0, W_pp0, "ple_fwd0")
    qn, kn, vn, q_pre, k_pre, h_q, h_kv = _qkv_fwd(x3, ln_b, ln_kv2, gq2, gk2, W_q, W_kv, "qkv_fwd")
    o2d, (W_out_b, W_up1, W_down1, W_g1, W_pp1) = _sb_fwd(qn, kn, vn, second, "sb_fwd")
    W_out_b, W_down1, W_g1 = W_out_b.reshape(-1, D), W_down1.reshape(-1, D), W_g1.reshape(-1, D)
    x4 = _proj_res(x3, o2d, W_out_b, "attn_out")
    x5, pre1, hm1 = _mlp_fwd(x4, ln_m[1], W_up1, W_down1, "mlp_fwd1")
    x6, gate1, pp1, hp1 = _ple_fwd(x5, p[1, 0], ln_p[1], W_g1, W_pp1, "ple_fwd1")
    dy, loss_part = _loss_grad(x6, tgt, "loss_grad")
    loss = lax.psum(loss_part[0, 0], ("x", "y", "c"))

    dx5, dpp1, dgp1, dlnp1 = _ple_bwd(dy, x5, gate1, pp1, ln_p[1], W_g1, "ple_bwd1")
    dx4, dpre1, s1, dlnm1 = _mlp_bwd(dx5, x4, pre1, ln_m[1], W_up1, W_down1, "mlp_bwd1")
    wg_second = [_wgrad_cols(hm1, dpre1, "wg_up1"), _wgrad_rows(s1, dx5, "wg_down1"),
                 _wgrad_rows(hp1, dgp1, "wg_gate1"), _wgrad_cols(p[1, 0].astype(MXU), dpp1, "wg_proj1"),
                 _wgrad_rows(o2d, dx4, "wg_out_b")]
    do2d = _proj_nt(dx4, W_out_b, "attn_out_bwd")
    dqn, dkn, dvn, (s_up1, s_down1, s_gate1, s_proj1, s_out_b) = _sb_bwd(qn, kn, vn, o2d, do2d, wg_second, "sb_bwd")
    dx3, dq_pre, dkv, dlnb, dlnkv, dgq, dgk = _qkv_bwd(dx4, x3, dqn, dkn, dvn, q_pre, k_pre, ln_b, ln_kv2, gq2, gk2,
                                                      W_q, W_kv, "qkv_bwd")
    dgq, dgk = dgq[:1, :HEAD_DIM], dgk[:1, :HEAD_DIM]
    dx2, dpp0, dgp0, dlnp0 = _ple_bwd(dx3, x2, gate0, pp0, ln_p[0], W_g0, "ple_bwd0")
    dx1, dpre0, s0, dlnm0 = _mlp_bwd(dx2, x1, pre0, ln_m[0], W_up0, W_down0, "mlp_bwd0")
    dx0, dz, dws, dbT, dlna, dgva = _sgu_bwd(dx1, x0, z, ln_a, W_in, gv_a, ws, wsT, bT, W_out_a, "sgu_bwd")
    wg_first = [_wgrad_rows(h_q, dq_pre, "wg_q"), _wgrad_cols(h_kv, dkv, "wg_kv"),
                _wgrad_cols(hm0, dpre0, "wg_up0"), _wgrad_rows(s0, dx2, "wg_down0"), _wgrad_rows(hp0, dgp0, "wg_gate0"),
                _wgrad_cols(p[0, 0].astype(MXU), dpp0, "wg_proj0"), _wgrad_cols(h_a, dz, "wg_in_a"),
                _wgrad_rows(y_a, dx1, "wg_out_a")]

    small = [("w_spatial", dws[None], w_spatial, m_w_spatial, v_w_spatial),
             ("b_spatial", dbT.T[None], b_spatial, m_b_spatial, v_b_spatial),
             ("ln_kv", dlnkv.reshape(-1), ln_kv, m_ln_kv, v_ln_kv),
             ("g_k", dgk.reshape(-1), g_k, m_g_k, v_g_k),
             ("ln_mix_b", dlnb, ln_mix_b, m_ln_mix_b, v_ln_mix_b),
             ("g_q", dgq, g_q, m_g_q, v_g_q),
             ("ln_mlp", jnp.concatenate([dlnm0, dlnm1]), ln_mlp, m_ln_mlp, v_ln_mlp),
             ("ln_ple", jnp.concatenate([dlnp0, dlnp1]), ln_ple, m_ln_ple, v_ln_ple)]
    sharded_vec = [("ln_mix_a", dlna, ln_mix_a, m_ln_mix_a, v_ln_mix_a),
                   ("g_v_a", dgva, g_v_a, m_g_v_a, v_g_v_a)]
    packs = [[], [], [], []]
    for _, g, w, m, v in small:
        for lst, a in zip(packs, (g, w, m, v)):
            lst.append(_rows128(a))
    for _, g, w, m, v in sharded_vec:
        packs[0].append(g.reshape(NDEV, -1))
        for lst, a in zip(packs[1:], (w, m, v)):
            lst.append(jnp.broadcast_to(a, (NDEV, a.shape[1])))
    g_pack, w_pack, m_pack, v_pack = (jnp.concatenate(lst) for lst in packs)
    g_pack8 = jnp.broadcast_to(g_pack[None], (NDEV,) + g_pack.shape)
    s_q, s_kv, s_up0, s_down0, s_gate0, s_proj0, s_in_a, s_out_a, g_all = _exchange(
        wg_first + [g_pack8], True, "scatter_first")

    def upd(w, m, v, s, name):
        shape = w.shape
        outs = _adamw(w.reshape(-1, shape[-1]), m.reshape(-1, shape[-1]), v.reshape(-1, shape[-1]), s, name)
        return [o.reshape(shape) for o in outs]

    res = {}
    per = {}
    per["w_up", 1] = upd(w_up[1], m_w_up[1], v_w_up[1], s_up1, "adam_up1")
    per["w_down", 1] = upd(w_down[1], m_w_down[1], v_w_down[1], s_down1, "adam_down1")
    per["w_ple_gate", 1] = upd(w_ple_gate[1], m_w_ple_gate[1], v_w_ple_gate[1], s_gate1, "adam_gate1")
    per["w_ple_proj", 1] = upd(w_ple_proj[1], m_w_ple_proj[1], v_w_ple_proj[1], s_proj1, "adam_proj1")
    res["w_out_b"] = upd(w_out_b, m_w_out_b, v_w_out_b, s_out_b, "adam_out_b")
    res["w_q"] = upd(w_q, m_w_q, v_w_q, s_q, "adam_q")
    res["w_kv"] = upd(w_kv, m_w_kv, v_w_kv, s_kv, "adam_kv")
    per["w_up", 0] = upd(w_up[0], m_w_up[0], v_w_up[0], s_up0, "adam_up0")
    per["w_down", 0] = upd(w_down[0], m_w_down[0], v_w_down[0], s_down0, "adam_down0")
    per["w_ple_gate", 0] = upd(w_ple_gate[0], m_w_ple_gate[0], v_w_ple_gate[0], s_gate0, "adam_gate0")
    per["w_ple_proj", 0] = upd(w_ple_proj[0], m_w_ple_proj[0], v_w_ple_proj[0], s_proj0, "adam_proj0")
    res["w_in_a"] = upd(w_in_a, m_w_in_a, v_w_in_a, s_in_a, "adam_in_a")
    res["w_out_a"] = upd(w_out_a, m_w_out_a, v_w_out_a, s_out_a, "adam_out_a")
    for nm in ("w_up", "w_down", "w_ple_gate", "w_ple_proj"):
        res[nm] = [jnp.stack([per[nm, l][t] for l in range(n_layers)]) for t in range(4)]

    outs = _adamw(w_pack, m_pack, v_pack, g_all, "adam_small")
    row = 0
    for nm, g, w, m, v in small:
        nrows = _rows128(w).shape[0]
        res[nm] = [o[row:row + nrows].reshape(-1)[:w.size].reshape(w.shape) for o in outs]
        row += nrows
    for nm, g, w, m, v in sharded_vec:
        res[nm] = [lax.dynamic_slice_in_dim(o[row:row + NDEV], me, 1, axis=0) for o in outs]
        row += NDEV

    names = ["ln_mix_a", "w_in_a", "g_v_a", "w_spatial", "b_spatial", "w_out_a", "ln_kv", "w_kv", "g_k", "ln_mix_b",
             "w_q", "g_q", "w_out_b", "ln_mlp", "w_up", "w_down", "ln_ple", "w_ple_gate", "w_ple_proj"]
    out = [loss, dx0[None]]
    for t in range(4):
        out += [res[nm][t] for nm in names]
    return tuple(out)
```

```python
import functools
import math

import jax
import jax.numpy as jnp
from jax import lax
from jax.experimental import pallas as pl
from jax.experimental.pallas import tpu as pltpu

F32 = jnp.float32
MXU = jnp.bfloat16
COMM = jnp.bfloat16
EPS = 1e-6
NDEV = 8
HEAD_DIM = 64
CHUNK = 128
GROUPS = 8
QBLK = 128
SCALE = HEAD_DIM ** -0.5
TOKEN_TILE = 256
ADAM_LR = 0.001
ADAM_B1 = 0.9
ADAM_B2 = 0.999
ADAM_EPS = 1e-08
ADAM_WD = 0.01
ADAM_STEP = 10
MESH = pl.DeviceIdType.MESH


def _call(body, **kw):
    return pl.pallas_call(body, **kw)


def _params(vmem_mb, n_axes=1):
    return pltpu.CompilerParams(dimension_semantics=("arbitrary",) * n_axes,
                                vmem_limit_bytes=vmem_mb << 20)


def _tile(tm, n):
    return pl.BlockSpec((tm, n), lambda i: (i, 0))


def _whole(shape):
    zeros = (0,) * len(shape)
    return pl.BlockSpec(shape, lambda i: zeros, pipeline_mode=pl.Buffered(1))


def _acc(shape):
    zeros = (0,) * len(shape)
    return pl.BlockSpec(shape, lambda i: zeros)


def _mm(a, b):
    return jnp.dot(a.astype(MXU), b.astype(MXU), preferred_element_type=F32)


def _mm_nt(a, b):
    return lax.dot_general(a.astype(MXU), b.astype(MXU), (((1,), (1,)), ((), ())),
                           preferred_element_type=F32)


def _mm_tn(a, b):
    return lax.dot_general(a.astype(MXU), b.astype(MXU), (((0,), (0,)), ((), ())),
                           preferred_element_type=F32)


def _split_dot(x, ones, terms=2):
    out = None
    for _ in range(terms):
        part = x.astype(MXU)
        x = x - part.astype(F32)
        d = jnp.dot(part, ones, preferred_element_type=F32)
        out = d if out is None else out + d
    return out


def _rms(x, g):
    rstd = lax.rsqrt(jnp.mean(x * x, axis=-1, keepdims=True) + EPS)
    xhat = x * rstd
    return xhat * g, xhat, rstd


def _rms_bwd(dh, xhat, rstd, g):
    dxh = dh * g
    dx = rstd * (dxh - xhat * jnp.mean(dxh * xhat, axis=-1, keepdims=True))
    dg = jnp.sum(dh * xhat, axis=0, keepdims=True)
    return dx, dg


_GELU_C = math.sqrt(2.0 / math.pi)


def _gelu(x):
    t = jnp.tanh(_GELU_C * (x + 0.044715 * (x * x * x)))
    return 0.5 * x * (1.0 + t)


def _gelu_and_grad(x):
    x2 = x * x
    t = jnp.tanh(_GELU_C * (x + 0.044715 * (x2 * x)))
    g = 0.5 * x * (1.0 + t)
    dg = 0.5 * (1.0 + t) + 0.5 * x * (1.0 - t * t) * (_GELU_C * (1.0 + 3.0 * 0.044715 * x2))
    return g, dg


def _softplus(z):
    return jnp.maximum(z, 0.0) + jnp.log1p(jnp.exp(-jnp.abs(z)))


def _tril_mask():
    row = lax.broadcasted_iota(jnp.int32, (CHUNK, CHUNK), 0)
    col = lax.broadcasted_iota(jnp.int32, (CHUNK, CHUNK), 1)
    return row >= col, row <= col


ANY_SPEC = pl.BlockSpec(memory_space=pl.ANY)


def _my_index():
    return 4 * lax.axis_index("x") + 2 * lax.axis_index("y") + lax.axis_index("c")


def _exchange_copies(srcs, lands, send_sems, recv_sems, scatter, arriving):
    x, y, c = lax.axis_index("x"), lax.axis_index("y"), lax.axis_index("c")
    me = 4 * x + 2 * y + c
    out = []
    for a in range(len(srcs)):
        for k in range(NDEV - 1):
            bits = k + 1
            px = 1 - x if (bits >> 2) & 1 else x
            py = 1 - y if (bits >> 1) & 1 else y
            pc = 1 - c if bits & 1 else c
            peer = 4 * px + 2 * py + pc
            src = srcs[a].at[peer] if scatter else srcs[a]
            out.append(pltpu.make_async_remote_copy(
                src_ref=src, dst_ref=lands[a].at[peer if arriving else me],
                send_sem=send_sems.at[a * (NDEV - 1) + k], recv_sem=recv_sems.at[a * (NDEV - 1) + k],
                device_id=(px, py, pc), device_id_type=MESH))
    return out


def _exchange_shapes(arrs, scatter):
    n = len(arrs)
    lands = [jax.ShapeDtypeStruct(a.shape if scatter else (NDEV,) + a.shape, a.dtype) for a in arrs]
    sems = [pltpu.SemaphoreType.DMA((n * (NDEV - 1),)), pltpu.SemaphoreType.DMA((n * (NDEV - 1),)),
            pltpu.SemaphoreType.DMA((n,))]
    return lands, sems


def _exchange_start(srcs, lands, sems, scatter):
    send_sems, recv_sems, local_sems = sems
    me = _my_index()
    for a in range(len(srcs)):
        pltpu.make_async_copy(srcs[a].at[me] if scatter else srcs[a], lands[a].at[me], local_sems.at[a]).start()
    for send in _exchange_copies(srcs, lands, send_sems, recv_sems, scatter, False):
        send.start()


def _exchange_finish(srcs, lands, sems, scatter):
    send_sems, recv_sems, local_sems = sems
    me = _my_index()
    for arrive in _exchange_copies(srcs, lands, send_sems, recv_sems, scatter, True):
        arrive.wait_recv()
    for send in _exchange_copies(srcs, lands, send_sems, recv_sems, scatter, False):
        send.wait_send()
    for a in range(len(srcs)):
        pltpu.make_async_copy(srcs[a].at[me] if scatter else srcs[a], lands[a].at[me], local_sems.at[a]).wait()


def _gather_two_level(arrs, name):
    n = len(arrs)
    lands = [jax.ShapeDtypeStruct((NDEV,) + a.shape, a.dtype) for a in arrs]

    def body(*refs):
        srcs, outs = refs[:n], refs[n:2 * n]
        send_sems, recv_sems, local_sems = refs[2 * n:]
        x, y, c = lax.axis_index("x"), lax.axis_index("y"), lax.axis_index("c")
        me, sibling = (x, y, c), (x, y, 1 - c)
        chips = [(1 - x, y), (x, 1 - y), (1 - x, 1 - y)]

        def index(dev):
            return 4 * dev[0] + 2 * dev[1] + dev[2]

        def copy(a, k, block, to, src=None):
            dst = outs[a].at[index(block)]
            return pltpu.make_async_remote_copy(
                src_ref=dst if src is None else src, dst_ref=dst, send_sem=send_sems.at[a, k],
                recv_sem=recv_sems.at[a, k], device_id=to, device_id_type=MESH)

        mine, first, passed = [], [], []
        for a in range(n):
            cp = pltpu.make_async_copy(srcs[a], outs[a].at[index(me)], local_sems.at[a])
            cp.start()
            mine.append(cp)
            first.append(copy(a, 0, me, sibling, src=srcs[a]))
            first += [copy(a, 1 + j, me, (*chip, c), src=srcs[a]) for j, chip in enumerate(chips)]
        for cp in first:
            cp.start()
        for a in range(n):
            for j, chip in enumerate(chips):
                copy(a, 1 + j, (*chip, c), me).wait_recv()
                cp = copy(a, 4 + j, (*chip, c), sibling)
                cp.start()
                passed.append(cp)
        for a in range(n):
            copy(a, 0, sibling, me).wait_recv()
            for j, chip in enumerate(chips):
                copy(a, 4 + j, (*chip, 1 - c), me).wait_recv()
        for cp in first + passed:
            cp.wait_send()
        for cp in mine:
            cp.wait()

    return _call(body, name=name, out_shape=lands, in_specs=[ANY_SPEC] * n, out_specs=[ANY_SPEC] * n,
                 scratch_shapes=[pltpu.SemaphoreType.DMA((n, NDEV - 1)), pltpu.SemaphoreType.DMA((n, NDEV - 1)),
                                 pltpu.SemaphoreType.DMA((n,))])(*arrs)


def _exchange(arrs, scatter, name):
    n = len(arrs)
    lands, sems = _exchange_shapes(arrs, scatter)

    def body(*refs):
        _exchange_start(refs[:n], refs[n:2 * n], refs[2 * n:], scatter)
        _exchange_finish(refs[:n], refs[n:2 * n], refs[2 * n:], scatter)

    return _call(body, name=name, out_shape=lands, in_specs=[ANY_SPEC] * n, out_specs=[ANY_SPEC] * n,
                 scratch_shapes=sems)(*arrs)


def _spatial_mix(vnb, ws_ref, bT_ref, mix_ref, tm):
    tri, _ = _tril_mask()
    for g in range(GROUPS):
        wm = jnp.where(tri, ws_ref[g], 0.0).astype(MXU)
        cols = slice(g * CHUNK, (g + 1) * CHUNK)
        for ch in range(tm // CHUNK):
            rows = slice(ch * CHUNK, (ch + 1) * CHUNK)
            mix_ref[rows, cols] = _mm(wm, vnb[rows, cols]) + bT_ref[:, g:g + 1]


def _sgu_fwd(x, ln, w_in, g_v, ws, bT, w_out, name):
    T, D = x.shape
    tm = min(TOKEN_TILE, T)
    nw = w_in.shape[2]

    def body(x_ref, ln_ref, win_ref, gv_ref, ws_ref, bT_ref, wout_ref, xo_ref, z_ref, h_ref, y_ref, mix_ref):
        xv = x_ref[...]
        h, _, _ = _rms(xv, ln_ref[...])
        hb = h.astype(MXU)
        h_ref[...] = hb
        for j in range(NDEV):
            z_ref[:, j * nw:(j + 1) * nw] = _mm(hb, win_ref[j])
        u = _gelu(z_ref[:, :D])
        gv = _gelu(z_ref[:, D:])
        vn, _, _ = _rms(gv, gv_ref[...])
        _spatial_mix(vn.astype(MXU), ws_ref, bT_ref, mix_ref, tm)
        y = (u * mix_ref[...]).astype(MXU)
        y_ref[...] = y
        xo_ref[...] = xv + _mm(y, wout_ref[...])

    return _call(
        body, name=name, grid=(T // tm,),
        out_shape=[jax.ShapeDtypeStruct((T, D), F32), jax.ShapeDtypeStruct((T, 2 * D), F32),
                   jax.ShapeDtypeStruct((T, D), MXU), jax.ShapeDtypeStruct((T, D), MXU)],
        in_specs=[_tile(tm, D), _whole(ln.shape), _whole(w_in.shape), _whole(g_v.shape), _whole(ws.shape),
                  _whole(bT.shape), _whole(w_out.shape)],
        out_specs=[_tile(tm, D), _tile(tm, 2 * D), _tile(tm, D), _tile(tm, D)],
        scratch_shapes=[pltpu.VMEM((tm, D), F32)],
        compiler_params=_params(40),
    )(x, ln, w_in, g_v, ws, bT, w_out)


def _mlp_fwd(x, ln, w_up, w_down, name):
    T, D = x.shape
    tm = min(TOKEN_TILE, T)
    nf = w_up.shape[2]
    F = nf * NDEV

    def body(x_ref, ln_ref, wup_ref, wdown_ref, xo_ref, pre_ref, h_ref):
        xv = x_ref[...]
        h, _, _ = _rms(xv, ln_ref[...])
        hb = h.astype(MXU)
        h_ref[...] = hb
        for j in range(NDEV):
            pre_ref[:, j * nf:(j + 1) * nf] = _mm(hb, wup_ref[j])
        a = jnp.maximum(pre_ref[...], 0.0)
        xo_ref[...] = xv + _mm(a * a, wdown_ref[...])

    return _call(
        body, name=name, grid=(T // tm,),
        out_shape=[jax.ShapeDtypeStruct((T, D), F32), jax.ShapeDtypeStruct((T, F), F32),
                   jax.ShapeDtypeStruct((T, D), MXU)],
        in_specs=[_tile(tm, D), _whole(ln.shape), _whole(w_up.shape), _whole(w_down.shape)],
        out_specs=[_tile(tm, D), _tile(tm, F), _tile(tm, D)],
        compiler_params=_params(52),
    )(x, ln, w_up, w_down)


def _ple_fwd(x, p, ln, w_g, w_pp, name):
    T, D = x.shape
    tm = min(TOKEN_TILE, T)
    npp = w_pp.shape[2]

    def body(x_ref, p_ref, ln_ref, wg_ref, wpp_ref, xo_ref, gate_ref, pp_ref, h_ref):
        xv = x_ref[...]
        h, _, _ = _rms(xv, ln_ref[...])
        hb = h.astype(MXU)
        h_ref[...] = hb
        gate = jax.nn.sigmoid(_mm(hb, wg_ref[...]))
        gate_ref[...] = gate
        pb = p_ref[...].astype(MXU)
        for j in range(NDEV):
            pp_ref[:, j * npp:(j + 1) * npp] = _mm(pb, wpp_ref[j])
        xo_ref[...] = xv + pp_ref[...] * gate

    return _call(
        body, name=name, grid=(T // tm,),
        out_shape=[jax.ShapeDtypeStruct((T, D), F32), jax.ShapeDtypeStruct((T, D), F32),
                   jax.ShapeDtypeStruct((T, D), F32), jax.ShapeDtypeStruct((T, D), MXU)],
        in_specs=[_tile(tm, D), _tile(tm, p.shape[1]), _whole(ln.shape), _whole(w_g.shape), _whole(w_pp.shape)],
        out_specs=[_tile(tm, D), _tile(tm, D), _tile(tm, D), _tile(tm, D)],
        compiler_params=_params(32),
    )(x, p, ln, w_g, w_pp)


def _head_ones():
    row = lax.broadcasted_iota(jnp.int32, (128, 128), 0)
    col = lax.broadcasted_iota(jnp.int32, (128, 128), 1)
    return (jnp.right_shift(row, 6) == jnp.right_shift(col, 6)).astype(MXU)


def _head_rms(x, g, ones):
    rstd = lax.rsqrt(_split_dot(x * x, ones, 3) * (1.0 / HEAD_DIM) + EPS)
    xhat = x * rstd
    return xhat * g, xhat, rstd


def _head_rms_bwd(dh, xhat, rstd, g, ones):
    dxh = dh * g
    mean = _split_dot(dxh * xhat, ones, 3) * (1.0 / HEAD_DIM)
    return rstd * (dxh - xhat * mean), jnp.sum(dh * xhat, axis=0, keepdims=True)


def _qkv_fwd(x, ln_q, ln_kv, g_q, g_k, w_q, w_kv, name):
    T, D = x.shape
    tm = min(TOKEN_TILE, T)
    nk = w_kv.shape[2]
    half = NDEV // 2

    def body(x_ref, lnq_ref, lnkv_ref, gq_ref, gk_ref, wq_ref, wkv_ref,
             q_ref, k_ref, v_ref, qpre_ref, kpre_ref, hq_ref, hkv_ref):
        xv = x_ref[...]
        _, xhat, _ = _rms(xv, lnq_ref[...])
        hq = (xhat * lnq_ref[...]).astype(MXU)
        hkv = (xhat * lnkv_ref[...]).astype(MXU)
        hq_ref[...] = hq
        hkv_ref[...] = hkv
        qpre_ref[...] = _mm(hq, wq_ref[...])
        for j in range(half):
            kpre_ref[:, j * nk:(j + 1) * nk] = _mm(hkv, wkv_ref[j])
            v_ref[:, j * nk:(j + 1) * nk] = _mm(hkv, wkv_ref[half + j]).astype(MXU)
        ones = _head_ones()
        for b in range(D // 128):
            cols = slice(b * 128, (b + 1) * 128)
            qn, _, _ = _head_rms(qpre_ref[:, cols], gq_ref[:, cols], ones)
            q_ref[:, cols] = (qn * SCALE).astype(MXU)
            kn, _, _ = _head_rms(kpre_ref[:, cols], gk_ref[:, cols], ones)
            k_ref[:, cols] = kn.astype(MXU)

    return _call(
        body, name=name, grid=(T // tm,),
        out_shape=[jax.ShapeDtypeStruct((T, D), MXU)] * 3 + [jax.ShapeDtypeStruct((T, D), F32)] * 2
        + [jax.ShapeDtypeStruct((T, D), MXU)] * 2,
        in_specs=[_tile(tm, D), _whole(ln_q.shape), _whole(ln_kv.shape), _whole(g_q.shape), _whole(g_k.shape),
                  _whole(w_q.shape), _whole(w_kv.shape)],
        out_specs=[_tile(tm, D)] * 7,
        compiler_params=_params(40),
    )(x, ln_q, ln_kv, g_q, g_k, w_q, w_kv)


SB_KEYS = 2 * QBLK


def _sb_consts():
    row = lax.broadcasted_iota(jnp.int32, (QBLK, QBLK), 0)
    col = lax.broadcasted_iota(jnp.int32, (QBLK, QBLK), 1)
    lane = lax.broadcasted_iota(jnp.int32, (QBLK, 128), 1)
    ones = jnp.ones((QBLK, QBLK), MXU)
    later = jnp.concatenate([(row > col).astype(MXU), ones], axis=1)
    later_eq = jnp.concatenate([(row >= col).astype(MXU), ones], axis=1)
    return later, later_eq, lane < HEAD_DIM


def _sb_window(i, w, diagonal):
    upper = (i + 1) * QBLK - w * SB_KEYS
    start = pl.multiple_of(jnp.maximum(upper - SB_KEYS, 0), QBLK)
    row = lax.broadcasted_iota(jnp.int32, (QBLK, SB_KEYS), 0)
    key = lax.broadcasted_iota(jnp.int32, (QBLK, SB_KEYS), 1) + start
    return start, (key < row + i * QBLK) if diagonal else (key < upper)


MASKED_LOG = -1e30


def _sb_terms(x, terms):
    x = jnp.concatenate([x[:, :QBLK], x[:, QBLK:]], axis=0)
    out = []
    for _ in range(terms):
        part = x.astype(MXU)
        x = x - part.astype(F32)
        out.append(part)
    return tuple(out)


def _sb_suffix(parts, ones, carry):
    s = None
    for part in parts:
        d = jnp.dot(part, ones, preferred_element_type=F32)
        s = d if s is None else s + d
    s_lo, sum_lo, s_hi, sum_hi = s[:QBLK, :QBLK], s[:QBLK, QBLK:], s[QBLK:, :QBLK], s[QBLK:, QBLK:]
    return jnp.concatenate([s_lo + (carry + sum_hi), s_hi + carry], axis=1), carry + (sum_lo + sum_hi)


def _sb_scores(z, mask):
    sp = _softplus(z)
    l, log_sig = -sp, z - sp
    if mask is not None:
        l = jnp.where(mask, l, 0.0)
        log_sig = jnp.where(mask, log_sig, MASKED_LOG)
    return log_sig, _sb_terms(l, 2)


def _sb_weights(staged, later, c_l):
    log_sig, parts = staged
    b, c_l = _sb_suffix(parts, later, c_l)
    return jnp.exp(log_sig + b), c_l


DEAD_LOG = -88.0


def _sb_alive(carry):
    return (jnp.max(jnp.maximum(carry[0][0], carry[1][0])) > DEAD_LOG).astype(jnp.int32)


def _ride_along(refs, n, scatter):
    step = pl.program_id(0) * pl.num_programs(1) + pl.program_id(1)
    srcs, lands, sems = refs[:n], refs[n:2 * n], refs[2 * n:]

    @pl.when(step == 0)
    def _():
        _exchange_start(srcs, lands, sems, scatter)

    def finish():
        @pl.when(step == pl.num_programs(0) * pl.num_programs(1) - 1)
        def _():
            _exchange_finish(srcs, lands, sems, scatter)

    return finish


def _sb_fwd(q, k, v, cargo, name):
    T, D = q.shape
    nc = len(cargo)
    lands, sems = _exchange_shapes(cargo, False)

    def body(q_ref, k_ref, v_ref, *rest):
        o_ref = rest[nc]
        finish = _ride_along(rest[:nc] + rest[nc + 1:], nc, False)
        i = pl.program_id(1)
        n_steps = (i + 2) // 2
        later, _, first = _sb_consts()
        qv = q_ref[...]
        zero = jnp.zeros_like(qv)
        qs = (jnp.where(first, qv, zero), jnp.where(first, zero, qv))

        def window(w, carry, diagonal):
            start, mask = _sb_window(i, w, diagonal)
            kw = k_ref[pl.ds(start, SB_KEYS), :]
            vw = v_ref[pl.ds(start, SB_KEYS), :]
            out = []
            for hh in range(2):
                c_l, acc = carry[hh]
                a, c_l = _sb_weights(_sb_scores(_mm_nt(qs[hh], kw), mask), later, c_l)
                out.append((c_l, acc + _mm(a, vw)))
            return tuple(out)

        def step(state):
            w, _, carry = state
            carry = window(w, carry, False)
            return w + 1, _sb_alive(carry), carry

        init = tuple((jnp.zeros((QBLK, 128), F32), jnp.zeros((QBLK, 128), F32)) for _ in range(2))
        carry = window(0, init, True)
        _, _, carry = lax.while_loop(lambda s: (s[0] < n_steps) & (s[1] > 0), step,
                                     (jnp.int32(1), _sb_alive(carry), carry))
        o_ref[...] = jnp.where(first, carry[0][1], carry[1][1])
        finish()

    qblk = pl.BlockSpec((QBLK, 128), lambda h, i: (i, h))
    kblk = pl.BlockSpec((T, 128), lambda h, i: (0, h))
    outs = _call(
        body, name=name, grid=(D // 128, T // QBLK), out_shape=[jax.ShapeDtypeStruct((T, D), F32)] + lands,
        in_specs=[qblk, kblk, kblk] + [ANY_SPEC] * nc, out_specs=[qblk] + [ANY_SPEC] * nc, scratch_shapes=sems,
        compiler_params=_params(32, 2),
    )(q, k, v, *cargo)
    return outs[0], outs[1:]


def _sb_bwd(q, k, v, o, do, cargo, name):
    T, D = q.shape
    nc = len(cargo)
    lands, sems = _exchange_shapes(cargo, True)

    def body(q_ref, k_ref, v_ref, o_ref, do_ref, *rest):
        dq_ref, dk_ref, dv_ref = rest[nc:nc + 3]
        finish = _ride_along(rest[:nc] + rest[nc + 3:], nc, True)
        i = pl.program_id(1)

        @pl.when(i == 0)
        def _():
            dk_ref[...] = jnp.zeros_like(dk_ref)
            dv_ref[...] = jnp.zeros_like(dv_ref)

        n_steps = (i + 2) // 2
        later, later_eq, first = _sb_consts()
        qv = q_ref[...]
        dob = do_ref[...].astype(MXU)
        zero = jnp.zeros_like(qv)
        qs = (jnp.where(first, qv, zero), jnp.where(first, zero, qv))
        dos = (jnp.where(first, dob, zero), jnp.where(first, zero, dob))
        prod = o_ref[...] * dob.astype(F32)
        ones = jnp.ones((128, 128), MXU)
        totals = (_split_dot(jnp.where(first, prod, 0.0), ones, 3), _split_dot(jnp.where(first, 0.0, prod), ones, 3))

        def window(w, carry, diagonal):
            start, mask = _sb_window(i, w, diagonal)
            kw = k_ref[pl.ds(start, SB_KEYS), :]
            vw = v_ref[pl.ds(start, SB_KEYS), :]
            out = []
            dk_blk = dv_blk = None
            for hh in range(2):
                c_l, c_e, dq = carry[hh]
                log_sig, parts = _sb_scores(_mm_nt(qs[hh], kw), mask)
                a, c_l = _sb_weights((log_sig, parts), later, c_l)
                ab = a.astype(MXU)
                e = ab.astype(F32) * _mm_nt(dos[hh], vw)
                from_here, c_e = _sb_suffix(_sb_terms(e, 3), later_eq, c_e)
                before = jnp.concatenate([totals[hh], totals[hh]], axis=1) - from_here
                sig = jnp.exp(log_sig)
                dzb = (e * (1.0 - sig) - sig * before).astype(MXU)
                dk_h = _mm_tn(dzb, qs[hh])
                dv_h = _mm_tn(ab, dos[hh])
                dk_blk = dk_h if dk_blk is None else dk_blk + dk_h
                dv_blk = dv_h if dv_blk is None else dv_blk + dv_h
                out.append((c_l, c_e, dq + _mm(dzb, kw)))
            dk_ref[pl.ds(start, SB_KEYS), :] += dk_blk
            dv_ref[pl.ds(start, SB_KEYS), :] += dv_blk
            return tuple(out)

        def step(state):
            w, _, carry = state
            carry = window(w, carry, False)
            return w + 1, _sb_alive(carry), carry

        init = tuple((jnp.zeros((QBLK, 128), F32),) * 3 for _ in range(2))
        carry = window(0, init, True)
        _, _, carry = lax.while_loop(lambda s: (s[0] < n_steps) & (s[1] > 0), step,
                                     (jnp.int32(1), _sb_alive(carry), carry))
        dq_ref[...] = jnp.where(first, carry[0][2], carry[1][2]) * SCALE
        finish()

    qblk = pl.BlockSpec((QBLK, 128), lambda h, i: (i, h))
    kblk = pl.BlockSpec((T, 128), lambda h, i: (0, h))
    full = jax.ShapeDtypeStruct((T, D), F32)
    outs = _call(
        body, name=name, grid=(D // 128, T // QBLK), out_shape=[full, full, full] + lands,
        in_specs=[qblk, kblk, kblk, qblk, qblk] + [ANY_SPEC] * nc, out_specs=[qblk, kblk, kblk] + [ANY_SPEC] * nc,
        scratch_shapes=sems, compiler_params=_params(32, 2),
    )(q, k, v, o, do, *cargo)
    return outs[0], outs[1], outs[2], outs[3:]


def _proj_res(x, a, w, name):
    T, D = x.shape
    tm = min(TOKEN_TILE, T)

    def body(x_ref, a_ref, w_ref, o_ref):
        o_ref[...] = x_ref[...] + _mm(a_ref[...], w_ref[...])

    return _call(
        body, name=name, grid=(T // tm,), out_shape=jax.ShapeDtypeStruct((T, D), F32),
        in_specs=[_tile(tm, D), _tile(tm, a.shape[1]), _whole(w.shape)], out_specs=_tile(tm, D),
        compiler_params=_params(32),
    )(x, a, w)


def _proj_nt(g, w, name):
    T = g.shape[0]
    K = w.shape[0]
    tm = min(TOKEN_TILE, T)

    def body(g_ref, w_ref, o_ref):
        o_ref[...] = _mm_nt(g_ref[...], w_ref[...])

    return _call(
        body, name=name, grid=(T // tm,), out_shape=jax.ShapeDtypeStruct((T, K), F32),
        in_specs=[_tile(tm, g.shape[1]), _whole(w.shape)], out_specs=_tile(tm, K),
        compiler_params=_params(32),
    )(g, w)


def _loss_grad(y, tgt, name):
    T, D = y.shape
    tm = min(TOKEN_TILE, T)

    def body(y_ref, t_ref, dy_ref, loss_ref):
        @pl.when(pl.program_id(0) == 0)
        def _():
            loss_ref[...] = jnp.zeros_like(loss_ref)
        diff = y_ref[...] - t_ref[...]
        dy_ref[...] = diff * (1.0 / D)
        rows = jnp.sum(diff * diff, axis=1, keepdims=True) * (1.0 / D)
        loss_ref[...] += 0.5 * jnp.sum(rows, axis=0, keepdims=True)

    return _call(
        body, name=name, grid=(T // tm,),
        out_shape=[jax.ShapeDtypeStruct((T, D), F32), jax.ShapeDtypeStruct((1, 1), F32)],
        in_specs=[_tile(tm, D), _tile(tm, D)], out_specs=[_tile(tm, D), _acc((1, 1))],
        compiler_params=_params(32),
    )(y, tgt)


def _ple_bwd(dx, x, gate, pp, ln, w_g, name):
    T, D = x.shape
    tm = min(TOKEN_TILE, T)

    def body(dx_ref, x_ref, gate_ref, pp_ref, ln_ref, wg_ref, dxo_ref, dpp_ref, dgp_ref, dln_ref):
        @pl.when(pl.program_id(0) == 0)
        def _():
            dln_ref[...] = jnp.zeros_like(dln_ref)
        dxv = dx_ref[...]
        gate = gate_ref[...]
        _, xhat, rstd = _rms(x_ref[...], ln_ref[...])
        dpp_ref[...] = (dxv * gate).astype(MXU)
        dgp = (dxv * pp_ref[...] * gate * (1.0 - gate)).astype(MXU)
        dgp_ref[...] = dgp
        dxn, dln = _rms_bwd(_mm_nt(dgp, wg_ref[...]), xhat, rstd, ln_ref[...])
        dln_ref[...] += dln
        dxo_ref[...] = dxn + dxv

    return _call(
        body, name=name, grid=(T // tm,),
        out_shape=[jax.ShapeDtypeStruct((T, D), F32), jax.ShapeDtypeStruct((T, D), MXU),
                   jax.ShapeDtypeStruct((T, D), MXU), jax.ShapeDtypeStruct(ln.shape, F32)],
        in_specs=[_tile(tm, D)] * 4 + [_whole(ln.shape), _whole(w_g.shape)],
        out_specs=[_tile(tm, D), _tile(tm, D), _tile(tm, D), _acc(ln.shape)],
        compiler_params=_params(32),
    )(dx, x, gate, pp, ln, w_g)


def _mlp_bwd(dx, x, pre, ln, w_up, w_down, name):
    T, D = x.shape
    tm = min(TOKEN_TILE, T)
    nf = w_up.shape[2]
    F = nf * NDEV

    def body(dx_ref, x_ref, pre_ref, ln_ref, wup_ref, wdown_ref, dxo_ref, dpre_ref, s_ref, dln_ref):
        @pl.when(pl.program_id(0) == 0)
        def _():
            dln_ref[...] = jnp.zeros_like(dln_ref)
        dxv = dx_ref[...]
        _, xhat, rstd = _rms(x_ref[...], ln_ref[...])
        a = jnp.maximum(pre_ref[...], 0.0)
        s_ref[...] = (a * a).astype(MXU)
        dpre_ref[...] = (_mm_nt(dxv, wdown_ref[...]) * (2.0 * a)).astype(MXU)
        dh = _mm_nt(dpre_ref[:, :nf], wup_ref[0])
        for j in range(1, NDEV):
            dh += _mm_nt(dpre_ref[:, j * nf:(j + 1) * nf], wup_ref[j])
        dxn, dln = _rms_bwd(dh, xhat, rstd, ln_ref[...])
        dln_ref[...] += dln
        dxo_ref[...] = dxn + dxv

    return _call(
        body, name=name, grid=(T // tm,),
        out_shape=[jax.ShapeDtypeStruct((T, D), F32), jax.ShapeDtypeStruct((T, F), MXU),
                   jax.ShapeDtypeStruct((T, F), MXU), jax.ShapeDtypeStruct(ln.shape, F32)],
        in_specs=[_tile(tm, D), _tile(tm, D), _tile(tm, F), _whole(ln.shape), _whole(w_up.shape),
                  _whole(w_down.shape)],
        out_specs=[_tile(tm, D), _tile(tm, F), _tile(tm, F), _acc(ln.shape)],
        compiler_params=_params(56),
    )(dx, x, pre, ln, w_up, w_down)


def _qkv_bwd(dx, x, dq, dk, dv, q_pre, k_pre, ln_q, ln_kv, g_q, g_k, w_q, w_kv, name):
    T, D = x.shape
    tm = min(TOKEN_TILE, T)
    nk = w_kv.shape[2]
    n_tiles = T // tm

    def body(dx_ref, x_ref, dq_ref, dk_ref, dv_ref, qpre_ref, kpre_ref, lnq_ref, lnkv_ref, gq_ref, gk_ref,
             wq_ref, wkv_ref, dxo_ref, dqp_ref, dkv_ref, dlnq_ref, dlnkv_ref, dgq_ref, dgk_ref, gq_acc, gk_acc):
        i = pl.program_id(0)

        @pl.when(i == 0)
        def _():
            dlnq_ref[...] = jnp.zeros_like(dlnq_ref)
            dlnkv_ref[...] = jnp.zeros_like(dlnkv_ref)
            gq_acc[...] = jnp.zeros_like(gq_acc)
            gk_acc[...] = jnp.zeros_like(gk_acc)

        ones = _head_ones()
        for b in range(D // 128):
            cols = slice(b * 128, (b + 1) * 128)
            _, xh, rs = _head_rms(qpre_ref[:, cols], gq_ref[:, cols], ones)
            d, dg = _head_rms_bwd(dq_ref[:, cols], xh, rs, gq_ref[:, cols], ones)
            dqp_ref[:, cols] = d.astype(MXU)
            gq_acc[:, cols] += dg
            _, xh, rs = _head_rms(kpre_ref[:, cols], gk_ref[:, cols], ones)
            d, dg = _head_rms_bwd(dk_ref[:, cols], xh, rs, gk_ref[:, cols], ones)
            dkv_ref[:, cols] = d.astype(MXU)
            gk_acc[:, cols] += dg
        dkv_ref[:, D:] = dv_ref[...].astype(MXU)

        _, xhat, rstd = _rms(x_ref[...], lnq_ref[...])
        dhq = _mm_nt(dqp_ref[...], wq_ref[...])
        dhkv = _mm_nt(dkv_ref[:, :nk], wkv_ref[0])
        for j in range(1, NDEV):
            dhkv += _mm_nt(dkv_ref[:, j * nk:(j + 1) * nk], wkv_ref[j])
        dxq, dlnq = _rms_bwd(dhq, xhat, rstd, lnq_ref[...])
        dxkv, dlnkv = _rms_bwd(dhkv, xhat, rstd, lnkv_ref[...])
        dlnq_ref[...] += dlnq
        dlnkv_ref[...] += dlnkv
        dxo_ref[...] = dx_ref[...] + dxq + dxkv

        @pl.when(i == n_tiles - 1)
        def _():
            row = lax.broadcasted_iota(jnp.int32, (D, 128), 0)
            col = lax.broadcasted_iota(jnp.int32, (D, 128), 1)
            fold = (jnp.bitwise_and(row, HEAD_DIM - 1) == col).astype(MXU)
            dgq_ref[...] = _split_dot(jnp.broadcast_to(gq_acc[...], (8, D)), fold, 3)
            dgk_ref[...] = _split_dot(jnp.broadcast_to(gk_acc[...], (8, D)), fold, 3)

    small = jax.ShapeDtypeStruct((8, 128), F32)
    return _call(
        body, name=name, grid=(n_tiles,),
        out_shape=[jax.ShapeDtypeStruct((T, D), F32), jax.ShapeDtypeStruct((T, D), MXU),
                   jax.ShapeDtypeStruct((T, 2 * D), MXU), jax.ShapeDtypeStruct(ln_q.shape, F32),
                   jax.ShapeDtypeStruct(ln_kv.shape, F32), small, small],
        in_specs=[_tile(tm, D)] * 7 + [_whole(ln_q.shape), _whole(ln_kv.shape), _whole(g_q.shape),
                                       _whole(g_k.shape), _whole(w_q.shape), _whole(w_kv.shape)],
        out_specs=[_tile(tm, D), _tile(tm, D), _tile(tm, 2 * D), _acc(ln_q.shape), _acc(ln_kv.shape),
                   _acc((8, 128)), _acc((8, 128))],
        scratch_shapes=[pltpu.VMEM((1, D), F32), pltpu.VMEM((1, D), F32)],
        compiler_params=_params(48),
    )(dx, x, dq, dk, dv, q_pre, k_pre, ln_q, ln_kv, g_q, g_k, w_q, w_kv)


def _sgu_bwd(dx, x, z, ln, w_in, g_v, ws, wsT, bT, w_out, name):
    T, D = x.shape
    tm = min(TOKEN_TILE, T)
    nw = w_in.shape[2]

    def body(dx_ref, x_ref, z_ref, ln_ref, win_ref, gv_ref, ws_ref, wsT_ref, bT_ref, wout_ref,
             dxo_ref, dz_ref, dws_ref, dbT_ref, dln_ref, dgv_ref, mix_ref, dvn_ref):
        @pl.when(pl.program_id(0) == 0)
        def _():
            dws_ref[...] = jnp.zeros_like(dws_ref)
            dbT_ref[...] = jnp.zeros_like(dbT_ref)
            dln_ref[...] = jnp.zeros_like(dln_ref)
            dgv_ref[...] = jnp.zeros_like(dgv_ref)
        dxv = dx_ref[...]
        _, xhat, rstd = _rms(x_ref[...], ln_ref[...])
        u, du = _gelu_and_grad(z_ref[:, :D])
        gv, dgv = _gelu_and_grad(z_ref[:, D:])
        vn, vhat, rstd_v = _rms(gv, gv_ref[...])
        vnb = vn.astype(MXU)
        _spatial_mix(vnb, ws_ref, bT_ref, mix_ref, tm)
        dy = _mm_nt(dxv, wout_ref[...])
        d_u = dy * mix_ref[...]
        d_mix = dy * u
        dmb = d_mix.astype(MXU)
        tri, triT = _tril_mask()
        for g in range(GROUPS):
            wmT = jnp.where(triT, wsT_ref[g], 0.0).astype(MXU)
            cols = slice(g * CHUNK, (g + 1) * CHUNK)
            for ch in range(tm // CHUNK):
                rows = slice(ch * CHUNK, (ch + 1) * CHUNK)
                dm = dmb[rows, cols]
                dws_ref[g] += jnp.where(tri, _mm_nt(dm, vnb[rows, cols]), 0.0)
                dbT_ref[:, g:g + 1] += jnp.sum(d_mix[rows, cols], axis=1, keepdims=True)
                dvn_ref[rows, cols] = _mm(wmT, dm)
        d_gv, dg = _rms_bwd(dvn_ref[...], vhat, rstd_v, gv_ref[...])
        dgv_ref[...] += dg
        dz_ref[:, :D] = (d_u * du).astype(MXU)
        dz_ref[:, D:] = (d_gv * dgv).astype(MXU)
        dh = _mm_nt(dz_ref[:, :nw], win_ref[0])
        for j in range(1, NDEV):
            dh += _mm_nt(dz_ref[:, j * nw:(j + 1) * nw], win_ref[j])
        dxn, dln = _rms_bwd(dh, xhat, rstd, ln_ref[...])
        dln_ref[...] += dln
        dxo_ref[...] = dxn + dxv

    return _call(
        body, name=name, grid=(T // tm,),
        out_shape=[jax.ShapeDtypeStruct((T, D), F32), jax.ShapeDtypeStruct((T, 2 * D), MXU),
                   jax.ShapeDtypeStruct(ws.shape, F32), jax.ShapeDtypeStruct(bT.shape, F32),
                   jax.ShapeDtypeStruct(ln.shape, F32), jax.ShapeDtypeStruct(g_v.shape, F32)],
        in_specs=[_tile(tm, D), _tile(tm, D), _tile(tm, 2 * D), _whole(ln.shape), _whole(w_in.shape),
                  _whole(g_v.shape), _whole(ws.shape), _whole(wsT.shape), _whole(bT.shape), _whole(w_out.shape)],
        out_specs=[_tile(tm, D), _tile(tm, 2 * D), _acc(ws.shape), _acc(bT.shape), _acc(ln.shape),
                   _acc(g_v.shape)],
        scratch_shapes=[pltpu.VMEM((tm, D), F32), pltpu.VMEM((tm, D), F32)],
        compiler_params=_params(48),
    )(dx, x, z, ln, w_in, g_v, ws, wsT, bT, w_out)


def _wgrad_rows(a, g, name):
    T, K = a.shape
    N = g.shape[1]
    kb = K // NDEV

    def body(a_ref, g_ref, o_ref):
        o_ref[...] = _mm_tn(a_ref[...], g_ref[...]).astype(COMM)

    return _call(
        body, name=name, grid=(NDEV,), out_shape=jax.ShapeDtypeStruct((K, N), COMM),
        in_specs=[pl.BlockSpec((T, kb), lambda j: (0, j)), _whole(g.shape)],
        out_specs=pl.BlockSpec((kb, N), lambda j: (j, 0)),
        compiler_params=_params(40),
    )(a, g).reshape(NDEV, kb, N)


def _wgrad_cols(a, g, name):
    T, K = a.shape
    N = g.shape[1]
    nb = N // NDEV

    def body(a_ref, g_ref, o_ref):
        o_ref[...] = _mm_tn(a_ref[...], g_ref[...]).astype(COMM)

    return _call(
        body, name=name, grid=(NDEV,), out_shape=jax.ShapeDtypeStruct((NDEV, K, nb), COMM),
        in_specs=[_whole(a.shape), pl.BlockSpec((T, nb), lambda j: (0, j))],
        out_specs=pl.BlockSpec((None, K, nb), lambda j: (j, 0, 0)),
        compiler_params=_params(40),
    )(a, g)


def _adamw(w, m, v, slots, name):
    R, C = w.shape
    n = slots.shape[0]
    tr = math.gcd(R, max(8, (128 * 1024) // C))
    if tr < 64:
        tr = R
    bc1 = 1.0 - ADAM_B1 ** ADAM_STEP
    bc2 = 1.0 - ADAM_B2 ** ADAM_STEP

    def body(w_ref, m_ref, v_ref, s_ref, g_ref, d_ref, mo_ref, vo_ref):
        g = s_ref[0].astype(F32)
        for j in range(1, n):
            g = g + s_ref[j].astype(F32)
        mn = ADAM_B1 * m_ref[...] + (1.0 - ADAM_B1) * g
        vn = ADAM_B2 * v_ref[...] + (1.0 - ADAM_B2) * (g * g)
        g_ref[...] = g
        mo_ref[...] = mn
        vo_ref[...] = vn
        d_ref[...] = -ADAM_LR * ((mn / bc1) / (jnp.sqrt(vn / bc2) + ADAM_EPS) + ADAM_WD * w_ref[...])

    blk = pl.BlockSpec((tr, C), lambda i: (i, 0))
    out = jax.ShapeDtypeStruct((R, C), F32)
    return _call(
        body, name=name, grid=(R // tr,), out_shape=[out, out, out, out],
        in_specs=[blk, blk, blk, pl.BlockSpec((n, tr, C), lambda i: (0, i, 0))], out_specs=[blk] * 4,
        compiler_params=_params(32),
    )(w, m, v, slots)


def _rows128(a):
    flat = a.reshape(-1)
    rows = -(-flat.shape[0] // 1024) * 8
    flat = jnp.pad(flat, (0, rows * 128 - flat.shape[0]))
    return flat.reshape(rows, 128)


def kernel(x, p, ln_mix_a, w_in_a, g_v_a, w_spatial, b_spatial, w_out_a, ln_kv, w_kv, g_k, ln_mix_b, w_q, g_q, w_out_b, ln_mlp, w_up, w_down, ln_ple, w_ple_gate, w_ple_proj, loss_target, m_ln_mix_a, m_w_in_a, m_g_v_a, m_w_spatial, m_b_spatial, m_w_out_a, m_ln_kv, m_w_kv, m_g_k, m_ln_mix_b, m_w_q, m_g_q, m_w_out_b, m_ln_mlp, m_w_up, m_w_down, m_ln_ple, m_w_ple_gate, m_w_ple_proj, v_ln_mix_a, v_w_in_a, v_g_v_a, v_w_spatial, v_b_spatial, v_w_out_a, v_ln_kv, v_w_kv, v_g_k, v_ln_mix_b, v_w_q, v_g_q, v_w_out_b, v_ln_mlp, v_w_up, v_w_down, v_ln_ple, v_w_ple_gate, v_w_ple_proj):
    me = 4 * lax.axis_index("x") + 2 * lax.axis_index("y") + lax.axis_index("c")
    D = x.shape[2]
    x0, tgt = x[0], loss_target[0]
    n_layers = w_up.shape[0]

    c = lambda w: w.astype(COMM)
    first = [c(w_in_a[0]), c(w_out_a[0]), ln_mix_a, g_v_a, c(w_up[0]), c(w_down[0]), c(w_ple_gate[0]),
             c(w_ple_proj[0]), c(w_q[0]), c(w_kv)]
    second = [c(w_out_b[0]), c(w_up[1]), c(w_down[1]), c(w_ple_gate[1]), c(w_ple_proj[1])]
    W_in, W_out_a, ln_a, gv_a, W_up0, W_down0, W_g0, W_pp0, W_q, W_kv = _gather_two_level(first, "gather_first")
    W_out_a, ln_a, gv_a = W_out_a.reshape(-1, D), ln_a.reshape(1, D), gv_a.reshape(1, D)
    W_down0, W_g0, W_q = W_down0.reshape(-1, D), W_g0.reshape(-1, D), W_q.reshape(-1, D)
    ws = w_spatial[0]
    wsT = jnp.swapaxes(ws, 1, 2)
    bT = b_spatial[0].T
    ln_kv2, ln_b = ln_kv.reshape(1, D), ln_mix_b
    gk2 = jnp.tile(g_k.reshape(1, HEAD_DIM), (1, D // HEAD_DIM))
    gq2 = jnp.tile(g_q, (1, D // HEAD_DIM))
    ln_m = [ln_mlp[l:l + 1] for l in range(n_layers)]
    ln_p = [ln_ple[l:l + 1] for l in range(n_layers)]

    x1, z, h_a, y_a = _sgu_fwd(x0, ln_a, W_in, gv_a, ws, bT, W_out_a, "sgu_fwd")
    x2, pre0, hm0 = _mlp_fwd(x1, ln_m[0], W_up0, W_down0, "mlp_fwd0")
    x3, gate0, pp0, hp0 = _ple_fwd(x2, p[0, 0], ln_p[0], W_g0, W_pp0, "ple_fwd0")
    qn, kn, vn, q_pre, k_pre, h_q, h_kv = _qkv_fwd(x3, ln_b, ln_kv2, gq2, gk2, W_q, W_kv, "qkv_fwd")
    o2d, (W_out_b, W_up1, W_down1, W_g1, W_pp1) = _sb_fwd(qn, kn, vn, second, "sb_fwd")
    W_out_b, W_down1, W_g1 = W_out_b.reshape(-1, D), W_down1.reshape(-1, D), W_g1.reshape(-1, D)
    x4 = _proj_res(x3, o2d, W_out_b, "attn_out")
    x5, pre1, hm1 = _mlp_fwd(x4, ln_m[1], W_up1, W_down1, "mlp_fwd1")
    x6, gate1, pp1, hp1 = _ple_fwd(x5, p[1, 0], ln_p[1], W_g1, W_pp1, "ple_fwd1")
    dy, loss_part = _loss_grad(x6, tgt, "loss_grad")
    loss = lax.psum(loss_part[0, 0], ("x", "y", "c"))

    dx5, dpp1, dgp1, dlnp1 = _ple_bwd(dy, x5, gate1, pp1, ln_p[1], W_g1, "ple_bwd1")
    dx4, dpre1, s1, dlnm1 = _mlp_bwd(dx5, x4, pre1, ln_m[1], W_up1, W_down1, "mlp_bwd1")
    wg_second = [_wgrad_cols(hm1, dpre1, "wg_up1"), _wgrad_rows(s1, dx5, "wg_down1"),
                 _wgrad_rows(hp1, dgp1, "wg_gate1"), _wgrad_cols(p[1, 0].astype(MXU), dpp1, "wg_proj1"),
                 _wgrad_rows(o2d, dx4, "wg_out_b")]
    do2d = _proj_nt(dx4, W_out_b, "attn_out_bwd")
    dqn, dkn, dvn, (s_up1, s_down1, s_gate1, s_proj1, s_out_b) = _sb_bwd(qn, kn, vn, o2d, do2d, wg_second, "sb_bwd")
    dx3, dq_pre, dkv, dlnb, dlnkv, dgq, dgk = _qkv_bwd(dx4, x3, dqn, dkn, dvn, q_pre, k_pre, ln_b, ln_kv2, gq2, gk2,
                                                      W_q, W_kv, "qkv_bwd")
    dgq, dgk = dgq[:1, :HEAD_DIM], dgk[:1, :HEAD_DIM]
    dx2, dpp0, dgp0, dlnp0 = _ple_bwd(dx3, x2, gate0, pp0, ln_p[0], W_g0, "ple_bwd0")
    dx1, dpre0, s0, dlnm0 = _mlp_bwd(dx2, x1, pre0, ln_m[0], W_up0, W_down0, "mlp_bwd0")
    dx0, dz, dws, dbT, dlna, dgva = _sgu_bwd(dx1, x0, z, ln_a, W_in, gv_a, ws, wsT, bT, W_out_a, "sgu_bwd")
    wg_first = [_wgrad_rows(h_q, dq_pre, "wg_q"), _wgrad_cols(h_kv, dkv, "wg_kv"),
                _wgrad_cols(hm0, dpre0, "wg_up0"), _wgrad_rows(s0, dx2, "wg_down0"), _wgrad_rows(hp0, dgp0, "wg_gate0"),
                _wgrad_cols(p[0, 0].astype(MXU), dpp0, "wg_proj0"), _wgrad_cols(h_a, dz, "wg_in_a"),
                _wgrad_rows(y_a, dx1, "wg_out_a")]

    small = [("w_spatial", dws[None], w_spatial, m_w_spatial, v_w_spatial),
             ("b_spatial", dbT.T[None], b_spatial, m_b_spatial, v_b_spatial),
             ("ln_kv", dlnkv.reshape(-1), ln_kv, m_ln_kv, v_ln_kv),
             ("g_k", dgk.reshape(-1), g_k, m_g_k, v_g_k),
             ("ln_mix_b", dlnb, ln_mix_b, m_ln_mix_b, v_ln_mix_b),
             ("g_q", dgq, g_q, m_g_q, v_g_q),
             ("ln_mlp", jnp.concatenate([dlnm0, dlnm1]), ln_mlp, m_ln_mlp, v_ln_mlp),
             ("ln_ple", jnp.concatenate([dlnp0, dlnp1]), ln_ple, m_ln_ple, v_ln_ple)]
    sharded_vec = [("ln_mix_a", dlna, ln_mix_a, m_ln_mix_a, v_ln_mix_a),
                   ("g_v_a", dgva, g_v_a, m_g_v_a, v_g_v_a)]
    packs = [[], [], [], []]
    for _, g, w, m, v in small:
        for lst, a in zip(packs, (g, w, m, v)):
            lst.append(_rows128(a))
    for _, g, w, m, v in sharded_vec:
        packs[0].append(g.reshape(NDEV, -1))
        for lst, a in zip(packs[1:], (w, m, v)):
            lst.append(jnp.broadcast_to(a, (NDEV, a.shape[1])))
    g_pack, w_pack, m_pack, v_pack = (jnp.concatenate(lst) for lst in packs)
    g_pack8 = jnp.broadcast_to(g_pack[None], (NDEV,) + g_pack.shape)
    s_q, s_kv, s_up0, s_down0, s_gate0, s_proj0, s_in_a, s_out_a, g_all = _exchange(
        wg_first + [g_pack8], True, "scatter_first")

    def upd(w, m, v, s, name):
        shape = w.shape
        outs = _adamw(w.reshape(-1, shape[-1]), m.reshape(-1, shape[-1]), v.reshape(-1, shape[-1]), s, name)
        return [o.reshape(shape) for o in outs]

    res = {}
    per = {}
    per["w_up", 1] = upd(w_up[1], m_w_up[1], v_w_up[1], s_up1, "adam_up1")
    per["w_down", 1] = upd(w_down[1], m_w_down[1], v_w_down[1], s_down1, "adam_down1")
    per["w_ple_gate", 1] = upd(w_ple_gate[1], m_w_ple_gate[1], v_w_ple_gate[1], s_gate1, "adam_gate1")
    per["w_ple_proj", 1] = upd(w_ple_proj[1], m_w_ple_proj[1], v_w_ple_proj[1], s_proj1, "adam_proj1")
    res["w_out_b"] = upd(w_out_b, m_w_out_b, v_w_out_b, s_out_b, "adam_out_b")
    res["w_q"] = upd(w_q, m_w_q, v_w_q, s_q, "adam_q")
    res["w_kv"] = upd(w_kv, m_w_kv, v_w_kv, s_kv, "adam_kv")
    per["w_up", 0] = upd(w_up[0], m_w_up[0], v_w_up[0], s_up0, "adam_up0")
    per["w_down", 0] = upd(w_down[0], m_w_down[0], v_w_down[0], s_down0, "adam_down0")
    per["w_ple_gate", 0] = upd(w_ple_gate[0], m_w_ple_gate[0], v_w_ple_gate[0], s_gate0, "adam_gate0")
    per["w_ple_proj", 0] = upd(w_ple_proj[0], m_w_ple_proj[0], v_w_ple_proj[0], s_proj0, "adam_proj0")
    res["w_in_a"] = upd(w_in_a, m_w_in_a, v_w_in_a, s_in_a, "adam_in_a")
    res["w_out_a"] = upd(w_out_a, m_w_out_a, v_w_out_a, s_out_a, "adam_out_a")
    for nm in ("w_up", "w_down", "w_ple_gate", "w_ple_proj"):
        res[nm] = [jnp.stack([per[nm, l][t] for l in range(n_layers)]) for t in range(4)]

    outs = _adamw(w_pack, m_pack, v_pack, g_all, "adam_small")
    row = 0
    for nm, g, w, m, v in small:
        nrows = _rows128(w).shape[0]
        res[nm] = [o[row:row + nrows].reshape(-1)[:w.size].reshape(w.shape) for o in outs]
        row += nrows
    for nm, g, w, m, v in sharded_vec:
        res[nm] = [lax.dynamic_slice_in_dim(o[row:row + NDEV], me, 1, axis=0) for o in outs]
        row += NDEV

    names = ["ln_mix_a", "w_in_a", "g_v_a", "w_spatial", "b_spatial", "w_out_a", "ln_kv", "w_kv", "g_k", "ln_mix_b",
             "w_q", "g_q", "w_out_b", "ln_mlp", "w_up", "w_down", "ln_ple", "w_ple_gate", "w_ple_proj"]
    out = [loss, dx0[None]]
    for t in range(4):
        out += [res[nm][t] for nm in names]
    return tuple(out)
```

```python
import functools
import math

import jax
import jax.numpy as jnp
from jax import lax
from jax.experimental import pallas as pl
from jax.experimental.pallas import tpu as pltpu

F32 = jnp.float32
MXU = jnp.bfloat16
COMM = jnp.bfloat16
EPS = 1e-6
NDEV = 8
HEAD_DIM = 64
CHUNK = 128
GROUPS = 8
QBLK = 128
SCALE = HEAD_DIM ** -0.5
TOKEN_TILE = 256
ADAM_LR = 0.001
ADAM_B1 = 0.9
ADAM_B2 = 0.999
ADAM_EPS = 1e-08
ADAM_WD = 0.01
ADAM_STEP = 10
MESH = pl.DeviceIdType.MESH


def _call(body, **kw):
    call = pl.pallas_call(body, **kw)
    return lambda *args: call(*(pltpu.with_memory_space_constraint(a, pltpu.HBM) for a in args))


def _params(vmem_mb, n_axes=1):
    return pltpu.CompilerParams(dimension_semantics=("arbitrary",) * n_axes,
                                vmem_limit_bytes=vmem_mb << 20)


def _tile(tm, n):
    return pl.BlockSpec((tm, n), lambda i: (i, 0))


def _whole(shape):
    zeros = (0,) * len(shape)
    return pl.BlockSpec(shape, lambda i: zeros, pipeline_mode=pl.Buffered(1))


def _acc(shape):
    zeros = (0,) * len(shape)
    return pl.BlockSpec(shape, lambda i: zeros)


def _mm(a, b):
    return jnp.dot(a.astype(MXU), b.astype(MXU), preferred_element_type=F32)


def _mm_nt(a, b):
    return lax.dot_general(a.astype(MXU), b.astype(MXU), (((1,), (1,)), ((), ())),
                           preferred_element_type=F32)


def _mm_tn(a, b):
    return lax.dot_general(a.astype(MXU), b.astype(MXU), (((0,), (0,)), ((), ())),
                           preferred_element_type=F32)


def _split_dot(x, ones, terms=2):
    out = None
    for _ in range(terms):
        part = x.astype(MXU)
        x = x - part.astype(F32)
        d = jnp.dot(part, ones, preferred_element_type=F32)
        out = d if out is None else out + d
    return out


def _rms(x, g):
    rstd = lax.rsqrt(jnp.mean(x * x, axis=-1, keepdims=True) + EPS)
    xhat = x * rstd
    return xhat * g, xhat, rstd


def _rms_bwd(dh, xhat, rstd, g):
    dxh = dh * g
    dx = rstd * (dxh - xhat * jnp.mean(dxh * xhat, axis=-1, keepdims=True))
    dg = jnp.sum(dh * xhat, axis=0, keepdims=True)
    return dx, dg


_GELU_C = math.sqrt(2.0 / math.pi)


def _gelu(x):
    t = jnp.tanh(_GELU_C * (x + 0.044715 * (x * x * x)))
    return 0.5 * x * (1.0 + t)


def _gelu_and_grad(x):
    x2 = x * x
    t = jnp.tanh(_GELU_C * (x + 0.044715 * (x2 * x)))
    g = 0.5 * x * (1.0 + t)
    dg = 0.5 * (1.0 + t) + 0.5 * x * (1.0 - t * t) * (_GELU_C * (1.0 + 3.0 * 0.044715 * x2))
    return g, dg


def _softplus(z):
    return jnp.maximum(z, 0.0) + jnp.log1p(jnp.exp(-jnp.abs(z)))


def _tril_mask():
    row = lax.broadcasted_iota(jnp.int32, (CHUNK, CHUNK), 0)
    col = lax.broadcasted_iota(jnp.int32, (CHUNK, CHUNK), 1)
    return row >= col, row <= col


ANY_SPEC = pl.BlockSpec(memory_space=pl.ANY)


def _my_index():
    return 4 * lax.axis_index("x") + 2 * lax.axis_index("y") + lax.axis_index("c")


def _exchange_copies(srcs, lands, send_sems, recv_sems, scatter, arriving):
    x, y, c = lax.axis_index("x"), lax.axis_index("y"), lax.axis_index("c")
    me = 4 * x + 2 * y + c
    out = []
    for a in range(len(srcs)):
        for k in range(NDEV - 1):
            bits = k + 1
            px = 1 - x if (bits >> 2) & 1 else x
            py = 1 - y if (bits >> 1) & 1 else y
            pc = 1 - c if bits & 1 else c
            peer = 4 * px + 2 * py + pc
            src = srcs[a].at[peer] if scatter else srcs[a]
            out.append(pltpu.make_async_remote_copy(
                src_ref=src, dst_ref=lands[a].at[peer if arriving else me],
                send_sem=send_sems.at[a * (NDEV - 1) + k], recv_sem=recv_sems.at[a * (NDEV - 1) + k],
                device_id=(px, py, pc), device_id_type=MESH))
    return out


def _exchange_shapes(arrs, scatter):
    n = len(arrs)
    lands = [jax.ShapeDtypeStruct(a.shape if scatter else (NDEV,) + a.shape, a.dtype) for a in arrs]
    sems = [pltpu.SemaphoreType.DMA((n * (NDEV - 1),)), pltpu.SemaphoreType.DMA((n * (NDEV - 1),)),
            pltpu.SemaphoreType.DMA((n,))]
    return lands, sems


def _exchange_start(srcs, lands, sems, scatter):
    send_sems, recv_sems, local_sems = sems
    me = _my_index()
    for a in range(len(srcs)):
        pltpu.make_async_copy(srcs[a].at[me] if scatter else srcs[a], lands[a].at[me], local_sems.at[a]).start()
    for send in _exchange_copies(srcs, lands, send_sems, recv_sems, scatter, False):
        send.start()


def _exchange_finish(srcs, lands, sems, scatter):
    send_sems, recv_sems, local_sems = sems
    me = _my_index()
    for arrive in _exchange_copies(srcs, lands, send_sems, recv_sems, scatter, True):
        arrive.wait_recv()
    for send in _exchange_copies(srcs, lands, send_sems, recv_sems, scatter, False):
        send.wait_send()
    for a in range(len(srcs)):
        pltpu.make_async_copy(srcs[a].at[me] if scatter else srcs[a], lands[a].at[me], local_sems.at[a]).wait()


def _gather_two_level(arrs, name):
    n = len(arrs)
    lands = [jax.ShapeDtypeStruct((NDEV,) + a.shape, a.dtype) for a in arrs]

    def body(*refs):
        srcs, outs = refs[:n], refs[n:2 * n]
        send_sems, recv_sems, local_sems = refs[2 * n:]
        x, y, c = lax.axis_index("x"), lax.axis_index("y"), lax.axis_index("c")
        me, sibling = (x, y, c), (x, y, 1 - c)
        chips = [(1 - x, y), (x, 1 - y), (1 - x, 1 - y)]

        def index(dev):
            return 4 * dev[0] + 2 * dev[1] + dev[2]

        def copy(a, k, block, to, src=None):
            dst = outs[a].at[index(block)]
            return pltpu.make_async_remote_copy(
                src_ref=dst if src is None else src, dst_ref=dst, send_sem=send_sems.at[a, k],
                recv_sem=recv_sems.at[a, k], device_id=to, device_id_type=MESH)

        mine, first, passed = [], [], []
        for a in range(n):
            cp = pltpu.make_async_copy(srcs[a], outs[a].at[index(me)], local_sems.at[a])
            cp.start()
            mine.append(cp)
            first.append(copy(a, 0, me, sibling, src=srcs[a]))
            first += [copy(a, 1 + j, me, (*chip, c), src=srcs[a]) for j, chip in enumerate(chips)]
        for cp in first:
            cp.start()
        for a in range(n):
            for j, chip in enumerate(chips):
                copy(a, 1 + j, (*chip, c), me).wait_recv()
                cp = copy(a, 4 + j, (*chip, c), sibling)
                cp.start()
                passed.append(cp)
        for a in range(n):
            copy(a, 0, sibling, me).wait_recv()
            for j, chip in enumerate(chips):
                copy(a, 4 + j, (*chip, 1 - c), me).wait_recv()
        for cp in first + passed:
            cp.wait_send()
        for cp in mine:
            cp.wait()

    return _call(body, name=name, out_shape=lands, in_specs=[ANY_SPEC] * n, out_specs=[ANY_SPEC] * n,
                 scratch_shapes=[pltpu.SemaphoreType.DMA((n, NDEV - 1)), pltpu.SemaphoreType.DMA((n, NDEV - 1)),
                                 pltpu.SemaphoreType.DMA((n,))])(*arrs)


def _scatter_pair(arrs, extra, name):
    n, ne = len(arrs), len(extra)
    lands = [jax.ShapeDtypeStruct((4,) + a.shape[2:], a.dtype) for a in arrs]
    extra_lands, extra_sems = _exchange_shapes(extra, True)

    def body(*refs):
        srcs, xsrc = refs[:n], refs[n:n + ne]
        outs, xout = refs[n + ne:2 * n + ne], refs[2 * n + ne:2 * (n + ne)]
        send_sems, recv_sems = refs[2 * (n + ne)], refs[2 * (n + ne) + 1]
        xsems = refs[2 * (n + ne) + 2:]
        x, y, c = lax.axis_index("x"), lax.axis_index("y"), lax.axis_index("c")
        _exchange_start(xsrc, xout, xsems, True)
        copies = [pltpu.make_async_remote_copy(
            src_ref=srcs[a].at[k, 1 - c], dst_ref=outs[a].at[k], send_sem=send_sems.at[a, k],
            recv_sem=recv_sems.at[a, k], device_id=(x, y, 1 - c), device_id_type=MESH)
            for a in range(n) for k in range(4)]
        for cp in copies:
            cp.start()
        for cp in copies:
            cp.wait()
        _exchange_finish(xsrc, xout, xsems, True)

    outs = _call(
        body, name=name, out_shape=lands + extra_lands, in_specs=[ANY_SPEC] * (n + ne), out_specs=[ANY_SPEC] * (n + ne),
        scratch_shapes=[pltpu.SemaphoreType.DMA((n, 4)), pltpu.SemaphoreType.DMA((n, 4))] + extra_sems,
    )(*arrs, *extra)
    return outs[:n], outs[n:]


def _pair_sum(own, other, name):
    _, R, C = own.shape
    tr = math.gcd(R, max(8, (128 * 1024) // C))

    def body(a_ref, b_ref, o_ref):
        o_ref[...] = (a_ref[...].astype(F32) + b_ref[...].astype(F32)).astype(COMM)

    blk = pl.BlockSpec((4, tr, C), lambda i: (0, i, 0))
    return _call(body, name=name, grid=(R // tr,), out_shape=jax.ShapeDtypeStruct(own.shape, COMM),
                 in_specs=[blk, blk], out_specs=blk, compiler_params=_params(32))(own, other)


def _scatter_chips(arrs, name):
    n = len(arrs)
    lands = [jax.ShapeDtypeStruct(a.shape, a.dtype) for a in arrs]

    def body(*refs):
        srcs, outs = refs[:n], refs[n:2 * n]
        send_sems, recv_sems, local_sems = refs[2 * n:]
        x, y, c = lax.axis_index("x"), lax.axis_index("y"), lax.axis_index("c")
        chip = 2 * x + y
        others = [(1 - x, y), (x, 1 - y), (1 - x, 1 - y)]
        local = [pltpu.make_async_copy(srcs[a].at[chip], outs[a].at[chip], local_sems.at[a]) for a in range(n)]
        for cp in local:
            cp.start()

        def copies(arriving):
            return [pltpu.make_async_remote_copy(
                src_ref=srcs[a].at[2 * px + py], dst_ref=outs[a].at[2 * px + py if arriving else chip],
                send_sem=send_sems.at[a, j], recv_sem=recv_sems.at[a, j], device_id=(px, py, c), device_id_type=MESH)
                for a in range(n) for j, (px, py) in enumerate(others)]

        for cp in copies(False):
            cp.start()
        for cp in copies(True):
            cp.wait_recv()
        for cp in copies(False):
            cp.wait_send()
        for cp in local:
            cp.wait()

    return _call(body, name=name, out_shape=lands, in_specs=[ANY_SPEC] * n, out_specs=[ANY_SPEC] * n,
                 scratch_shapes=[pltpu.SemaphoreType.DMA((n, 3)), pltpu.SemaphoreType.DMA((n, 3)),
                                 pltpu.SemaphoreType.DMA((n,))])(*arrs)


def _exchange(arrs, scatter, name):
    n = len(arrs)
    lands, sems = _exchange_shapes(arrs, scatter)

    def body(*refs):
        _exchange_start(refs[:n], refs[n:2 * n], refs[2 * n:], scatter)
        _exchange_finish(refs[:n], refs[n:2 * n], refs[2 * n:], scatter)

    return _call(body, name=name, out_shape=lands, in_specs=[ANY_SPEC] * n, out_specs=[ANY_SPEC] * n,
                 scratch_shapes=sems)(*arrs)


def _spatial_mix(vnb, ws_ref, bT_ref, mix_ref, tm):
    tri, _ = _tril_mask()
    for g in range(GROUPS):
        wm = jnp.where(tri, ws_ref[g], 0.0).astype(MXU)
        cols = slice(g * CHUNK, (g + 1) * CHUNK)
        for ch in range(tm // CHUNK):
            rows = slice(ch * CHUNK, (ch + 1) * CHUNK)
            mix_ref[rows, cols] = _mm(wm, vnb[rows, cols]) + bT_ref[:, g:g + 1]


def _sgu_fwd(x, ln, w_in, g_v, ws, bT, w_out, name):
    T, D = x.shape
    tm = min(TOKEN_TILE, T)
    nw = w_in.shape[2]

    def body(x_ref, ln_ref, win_ref, gv_ref, ws_ref, bT_ref, wout_ref, xo_ref, z_ref, h_ref, y_ref, mix_ref):
        xv = x_ref[...]
        h, _, _ = _rms(xv, ln_ref[...])
        hb = h.astype(MXU)
        h_ref[...] = hb
        for j in range(NDEV):
            z_ref[:, j * nw:(j + 1) * nw] = _mm(hb, win_ref[j])
        u = _gelu(z_ref[:, :D])
        gv = _gelu(z_ref[:, D:])
        vn, _, _ = _rms(gv, gv_ref[...])
        _spatial_mix(vn.astype(MXU), ws_ref, bT_ref, mix_ref, tm)
        y = (u * mix_ref[...]).astype(MXU)
        y_ref[...] = y
        xo_ref[...] = xv + _mm(y, wout_ref[...])

    return _call(
        body, name=name, grid=(T // tm,),
        out_shape=[jax.ShapeDtypeStruct((T, D), F32), jax.ShapeDtypeStruct((T, 2 * D), F32),
                   jax.ShapeDtypeStruct((T, D), MXU), jax.ShapeDtypeStruct((T, D), MXU)],
        in_specs=[_tile(tm, D), _whole(ln.shape), _whole(w_in.shape), _whole(g_v.shape), _whole(ws.shape),
                  _whole(bT.shape), _whole(w_out.shape)],
        out_specs=[_tile(tm, D), _tile(tm, 2 * D), _tile(tm, D), _tile(tm, D)],
        scratch_shapes=[pltpu.VMEM((tm, D), F32)],
        compiler_params=_params(40),
    )(x, ln, w_in, g_v, ws, bT, w_out)


def _mlp_fwd(x, ln, w_up, w_down, name):
    T, D = x.shape
    tm = min(TOKEN_TILE, T)
    nf = w_up.shape[2]
    F = nf * NDEV

    def body(x_ref, ln_ref, wup_ref, wdown_ref, xo_ref, pre_ref, h_ref):
        xv = x_ref[...]
        h, _, _ = _rms(xv, ln_ref[...])
        hb = h.astype(MXU)
        h_ref[...] = hb
        for j in range(NDEV):
            pre_ref[:, j * nf:(j + 1) * nf] = _mm(hb, wup_ref[j])
        a = jnp.maximum(pre_ref[...], 0.0)
        xo_ref[...] = xv + _mm(a * a, wdown_ref[...])

    return _call(
        body, name=name, grid=(T // tm,),
        out_shape=[jax.ShapeDtypeStruct((T, D), F32), jax.ShapeDtypeStruct((T, F), F32),
                   jax.ShapeDtypeStruct((T, D), MXU)],
        in_specs=[_tile(tm, D), _whole(ln.shape), _whole(w_up.shape), _whole(w_down.shape)],
        out_specs=[_tile(tm, D), _tile(tm, F), _tile(tm, D)],
        compiler_params=_params(52),
    )(x, ln, w_up, w_down)


def _ple_fwd(x, p, ln, w_g, w_pp, name):
    T, D = x.shape
    tm = min(TOKEN_TILE, T)
    npp = w_pp.shape[2]

    def body(x_ref, p_ref, ln_ref, wg_ref, wpp_ref, xo_ref, gate_ref, pp_ref, h_ref):
        xv = x_ref[...]
        h, _, _ = _rms(xv, ln_ref[...])
        hb = h.astype(MXU)
        h_ref[...] = hb
        gate = jax.nn.sigmoid(_mm(hb, wg_ref[...]))
        gate_ref[...] = gate
        pb = p_ref[...].astype(MXU)
        for j in range(NDEV):
            pp_ref[:, j * npp:(j + 1) * npp] = _mm(pb, wpp_ref[j])
        xo_ref[...] = xv + pp_ref[...] * gate

    return _call(
        body, name=name, grid=(T // tm,),
        out_shape=[jax.ShapeDtypeStruct((T, D), F32), jax.ShapeDtypeStruct((T, D), F32),
                   jax.ShapeDtypeStruct((T, D), F32), jax.ShapeDtypeStruct((T, D), MXU)],
        in_specs=[_tile(tm, D), _tile(tm, p.shape[1]), _whole(ln.shape), _whole(w_g.shape), _whole(w_pp.shape)],
        out_specs=[_tile(tm, D), _tile(tm, D), _tile(tm, D), _tile(tm, D)],
        compiler_params=_params(32),
    )(x, p, ln, w_g, w_pp)


def _head_ones():
    row = lax.broadcasted_iota(jnp.int32, (128, 128), 0)
    col = lax.broadcasted_iota(jnp.int32, (128, 128), 1)
    return (jnp.right_shift(row, 6) == jnp.right_shift(col, 6)).astype(MXU)


def _head_rms(x, g, ones):
    rstd = lax.rsqrt(_split_dot(x * x, ones, 3) * (1.0 / HEAD_DIM) + EPS)
    xhat = x * rstd
    return xhat * g, xhat, rstd


def _head_rms_bwd(dh, xhat, rstd, g, ones):
    dxh = dh * g
    mean = _split_dot(dxh * xhat, ones, 3) * (1.0 / HEAD_DIM)
    return rstd * (dxh - xhat * mean), jnp.sum(dh * xhat, axis=0, keepdims=True)


def _qkv_fwd(x, ln_q, ln_kv, g_q, g_k, w_q, w_kv, name):
    T, D = x.shape
    tm = min(TOKEN_TILE, T)
    nk = w_kv.shape[2]
    half = NDEV // 2

    def body(x_ref, lnq_ref, lnkv_ref, gq_ref, gk_ref, wq_ref, wkv_ref,
             q_ref, k_ref, v_ref, qpre_ref, kpre_ref, hq_ref, hkv_ref):
        xv = x_ref[...]
        _, xhat, _ = _rms(xv, lnq_ref[...])
        hq = (xhat * lnq_ref[...]).astype(MXU)
        hkv = (xhat * lnkv_ref[...]).astype(MXU)
        hq_ref[...] = hq
        hkv_ref[...] = hkv
        qpre_ref[...] = _mm(hq, wq_ref[...])
        for j in range(half):
            kpre_ref[:, j * nk:(j + 1) * nk] = _mm(hkv, wkv_ref[j])
            v_ref[:, j * nk:(j + 1) * nk] = _mm(hkv, wkv_ref[half + j]).astype(MXU)
        ones = _head_ones()
        for b in range(D // 128):
            cols = slice(b * 128, (b + 1) * 128)
            qn, _, _ = _head_rms(qpre_ref[:, cols], gq_ref[:, cols], ones)
            q_ref[:, cols] = (qn * SCALE).astype(MXU)
            kn, _, _ = _head_rms(kpre_ref[:, cols], gk_ref[:, cols], ones)
            k_ref[:, cols] = kn.astype(MXU)

    return _call(
        body, name=name, grid=(T // tm,),
        out_shape=[jax.ShapeDtypeStruct((T, D), MXU)] * 3 + [jax.ShapeDtypeStruct((T, D), F32)] * 2
        + [jax.ShapeDtypeStruct((T, D), MXU)] * 2,
        in_specs=[_tile(tm, D), _whole(ln_q.shape), _whole(ln_kv.shape), _whole(g_q.shape), _whole(g_k.shape),
                  _whole(w_q.shape), _whole(w_kv.shape)],
        out_specs=[_tile(tm, D)] * 7,
        compiler_params=_params(40),
    )(x, ln_q, ln_kv, g_q, g_k, w_q, w_kv)


SB_KEYS = 2 * QBLK


def _sb_consts():
    row = lax.broadcasted_iota(jnp.int32, (QBLK, QBLK), 0)
    col = lax.broadcasted_iota(jnp.int32, (QBLK, QBLK), 1)
    lane = lax.broadcasted_iota(jnp.int32, (QBLK, 128), 1)
    ones = jnp.ones((QBLK, QBLK), MXU)
    later = jnp.concatenate([(row > col).astype(MXU), ones], axis=1)
    later_eq = jnp.concatenate([(row >= col).astype(MXU), ones], axis=1)
    return later, later_eq, lane < HEAD_DIM


def _sb_window(i, w, diagonal):
    upper = (i + 1) * QBLK - w * SB_KEYS
    start = pl.multiple_of(jnp.maximum(upper - SB_KEYS, 0), QBLK)
    row = lax.broadcasted_iota(jnp.int32, (QBLK, SB_KEYS), 0)
    key = lax.broadcasted_iota(jnp.int32, (QBLK, SB_KEYS), 1) + start
    return start, (key < row + i * QBLK) if diagonal else (key < upper)


MASKED_LOG = -1e30


def _sb_terms(x, terms):
    x = jnp.concatenate([x[:, :QBLK], x[:, QBLK:]], axis=0)
    out = []
    for _ in range(terms):
        part = x.astype(MXU)
        x = x - part.astype(F32)
        out.append(part)
    return tuple(out)


def _sb_suffix(parts, ones, carry):
    s = None
    for part in parts:
        d = jnp.dot(part, ones, preferred_element_type=F32)
        s = d if s is None else s + d
    s_lo, sum_lo, s_hi, sum_hi = s[:QBLK, :QBLK], s[:QBLK, QBLK:], s[QBLK:, :QBLK], s[QBLK:, QBLK:]
    return jnp.concatenate([s_lo + (carry + sum_hi), s_hi + carry], axis=1), carry + (sum_lo + sum_hi)


def _sb_scores(z, mask):
    sp = _softplus(z)
    l, log_sig = -sp, z - sp
    if mask is not None:
        l = jnp.where(mask, l, 0.0)
        log_sig = jnp.where(mask, log_sig, MASKED_LOG)
    return log_sig, _sb_terms(l, 2)


def _sb_weights(staged, later, c_l):
    log_sig, parts = staged
    b, c_l = _sb_suffix(parts, later, c_l)
    return jnp.exp(log_sig + b), c_l


DEAD_LOG = -88.0


def _sb_alive(carry):
    return (jnp.max(jnp.maximum(carry[0][0], carry[1][0])) > DEAD_LOG).astype(jnp.int32)


def _ride_along(refs, n, scatter):
    step = pl.program_id(0) * pl.num_programs(1) + pl.program_id(1)
    srcs, lands, sems = refs[:n], refs[n:2 * n], refs[2 * n:]

    @pl.when(step == 0)
    def _():
        _exchange_start(srcs, lands, sems, scatter)

    def finish():
        @pl.when(step == pl.num_programs(0) * pl.num_programs(1) - 1)
        def _():
            _exchange_finish(srcs, lands, sems, scatter)

    return finish


def _sb_fwd(q, k, v, cargo, name):
    T, D = q.shape
    nc = len(cargo)
    lands, sems = _exchange_shapes(cargo, False)

    def body(q_ref, k_ref, v_ref, *rest):
        o_ref = rest[nc]
        finish = _ride_along(rest[:nc] + rest[nc + 1:], nc, False)
        i = pl.program_id(1)
        n_steps = (i + 2) // 2
        later, _, first = _sb_consts()
        qv = q_ref[...]
        zero = jnp.zeros_like(qv)
        qs = (jnp.where(first, qv, zero), jnp.where(first, zero, qv))

        def window(w, carry, diagonal):
            start, mask = _sb_window(i, w, diagonal)
            kw = k_ref[pl.ds(start, SB_KEYS), :]
            vw = v_ref[pl.ds(start, SB_KEYS), :]
            out = []
            for hh in range(2):
                c_l, acc = carry[hh]
                a, c_l = _sb_weights(_sb_scores(_mm_nt(qs[hh], kw), mask), later, c_l)
                out.append((c_l, acc + _mm(a, vw)))
            return tuple(out)

        def step(state):
            w, _, carry = state
            carry = window(w, carry, False)
            return w + 1, _sb_alive(carry), carry

        init = tuple((jnp.zeros((QBLK, 128), F32), jnp.zeros((QBLK, 128), F32)) for _ in range(2))
        carry = window(0, init, True)
        _, _, carry = lax.while_loop(lambda s: (s[0] < n_steps) & (s[1] > 0), step,
                                     (jnp.int32(1), _sb_alive(carry), carry))
        o_ref[...] = jnp.where(first, carry[0][1], carry[1][1])
        finish()

    qblk = pl.BlockSpec((QBLK, 128), lambda h, i: (i, h))
    kblk = pl.BlockSpec((T, 128), lambda h, i: (0, h))
    outs = _call(
        body, name=name, grid=(D // 128, T // QBLK), out_shape=[jax.ShapeDtypeStruct((T, D), F32)] + lands,
        in_specs=[qblk, kblk, kblk] + [ANY_SPEC] * nc, out_specs=[qblk] + [ANY_SPEC] * nc, scratch_shapes=sems,
        compiler_params=_params(32, 2),
    )(q, k, v, *cargo)
    return outs[0], outs[1:]


def _sb_bwd(q, k, v, o, do, cargo, name):
    T, D = q.shape
    nc = len(cargo)
    lands, sems = _exchange_shapes(cargo, True)

    def body(q_ref, k_ref, v_ref, o_ref, do_ref, *rest):
        dq_ref, dk_ref, dv_ref = rest[nc:nc + 3]
        finish = _ride_along(rest[:nc] + rest[nc + 3:], nc, True)
        i = pl.program_id(1)

        @pl.when(i == 0)
        def _():
            dk_ref[...] = jnp.zeros_like(dk_ref)
            dv_ref[...] = jnp.zeros_like(dv_ref)

        n_steps = (i + 2) // 2
        later, later_eq, first = _sb_consts()
        qv = q_ref[...]
        dob = do_ref[...].astype(MXU)
        zero = jnp.zeros_like(qv)
        qs = (jnp.where(first, qv, zero), jnp.where(first, zero, qv))
        dos = (jnp.where(first, dob, zero), jnp.where(first, zero, dob))
        prod = o_ref[...] * dob.astype(F32)
        ones = jnp.ones((128, 128), MXU)
        totals = (_split_dot(jnp.where(first, prod, 0.0), ones, 3), _split_dot(jnp.where(first, 0.0, prod), ones, 3))

        def window(w, carry, diagonal):
            start, mask = _sb_window(i, w, diagonal)
            kw = k_ref[pl.ds(start, SB_KEYS), :]
            vw = v_ref[pl.ds(start, SB_KEYS), :]
            out = []
            dk_blk = dv_blk = None
            for hh in range(2):
                c_l, c_e, dq = carry[hh]
                log_sig, parts = _sb_scores(_mm_nt(qs[hh], kw), mask)
                a, c_l = _sb_weights((log_sig, parts), later, c_l)
                ab = a.astype(MXU)
                e = ab.astype(F32) * _mm_nt(dos[hh], vw)
                from_here, c_e = _sb_suffix(_sb_terms(e, 3), later_eq, c_e)
                before = jnp.concatenate([totals[hh], totals[hh]], axis=1) - from_here
                sig = jnp.exp(log_sig)
                dzb = (e * (1.0 - sig) - sig * before).astype(MXU)
                dk_h = _mm_tn(dzb, qs[hh])
                dv_h = _mm_tn(ab, dos[hh])
                dk_blk = dk_h if dk_blk is None else dk_blk + dk_h
                dv_blk = dv_h if dv_blk is None else dv_blk + dv_h
                out.append((c_l, c_e, dq + _mm(dzb, kw)))
            dk_ref[pl.ds(start, SB_KEYS), :] += dk_blk
            dv_ref[pl.ds(start, SB_KEYS), :] += dv_blk
            return tuple(out)

        def step(state):
            w, _, carry = state
            carry = window(w, carry, False)
            return w + 1, _sb_alive(carry), carry

        init = tuple((jnp.zeros((QBLK, 128), F32),) * 3 for _ in range(2))
        carry = window(0, init, True)
        _, _, carry = lax.while_loop(lambda s: (s[0] < n_steps) & (s[1] > 0), step,
                                     (jnp.int32(1), _sb_alive(carry), carry))
        dq_ref[...] = jnp.where(first, carry[0][2], carry[1][2]) * SCALE
        finish()

    qblk = pl.BlockSpec((QBLK, 128), lambda h, i: (i, h))
    kblk = pl.BlockSpec((T, 128), lambda h, i: (0, h))
    full = jax.ShapeDtypeStruct((T, D), F32)
    outs = _call(
        body, name=name, grid=(D // 128, T // QBLK), out_shape=[full, full, full] + lands,
        in_specs=[qblk, kblk, kblk, qblk, qblk] + [ANY_SPEC] * nc, out_specs=[qblk, kblk, kblk] + [ANY_SPEC] * nc,
        scratch_shapes=sems, compiler_params=_params(32, 2),
    )(q, k, v, o, do, *cargo)
    return outs[0], outs[1], outs[2], outs[3:]


def _proj_res(x, a, w, name):
    T, D = x.shape
    tm = min(TOKEN_TILE, T)

    def body(x_ref, a_ref, w_ref, o_ref):
        o_ref[...] = x_ref[...] + _mm(a_ref[...], w_ref[...])

    return _call(
        body, name=name, grid=(T // tm,), out_shape=jax.ShapeDtypeStruct((T, D), F32),
        in_specs=[_tile(tm, D), _tile(tm, a.shape[1]), _whole(w.shape)], out_specs=_tile(tm, D),
        compiler_params=_params(32),
    )(x, a, w)


def _proj_nt(g, w, name):
    T = g.shape[0]
    K = w.shape[0]
    tm = min(TOKEN_TILE, T)

    def body(g_ref, w_ref, o_ref):
        o_ref[...] = _mm_nt(g_ref[...], w_ref[...])

    return _call(
        body, name=name, grid=(T // tm,), out_shape=jax.ShapeDtypeStruct((T, K), F32),
        in_specs=[_tile(tm, g.shape[1]), _whole(w.shape)], out_specs=_tile(tm, K),
        compiler_params=_params(32),
    )(g, w)


def _loss_grad(y, tgt, name):
    T, D = y.shape
    tm = min(TOKEN_TILE, T)

    def body(y_ref, t_ref, dy_ref, loss_ref):
        @pl.when(pl.program_id(0) == 0)
        def _():
            loss_ref[...] = jnp.zeros_like(loss_ref)
        diff = y_ref[...] - t_ref[...]
        dy_ref[...] = diff * (1.0 / D)
        rows = jnp.sum(diff * diff, axis=1, keepdims=True) * (1.0 / D)
        loss_ref[...] += 0.5 * jnp.sum(rows, axis=0, keepdims=True)

    return _call(
        body, name=name, grid=(T // tm,),
        out_shape=[jax.ShapeDtypeStruct((T, D), F32), jax.ShapeDtypeStruct((1, 1), F32)],
        in_specs=[_tile(tm, D), _tile(tm, D)], out_specs=[_tile(tm, D), _acc((1, 1))],
        compiler_params=_params(32),
    )(y, tgt)


def _ple_bwd(dx, x, gate, pp, ln, w_g, name):
    T, D = x.shape
    tm = min(TOKEN_TILE, T)

    def body(dx_ref, x_ref, gate_ref, pp_ref, ln_ref, wg_ref, dxo_ref, dpp_ref, dgp_ref, dln_ref):
        @pl.when(pl.program_id(0) == 0)
        def _():
            dln_ref[...] = jnp.zeros_like(dln_ref)
        dxv = dx_ref[...]
        gate = gate_ref[...]
        _, xhat, rstd = _rms(x_ref[...], ln_ref[...])
        dpp_ref[...] = (dxv * gate).astype(MXU)
        dgp = (dxv * pp_ref[...] * gate * (1.0 - gate)).astype(MXU)
        dgp_ref[...] = dgp
        dxn, dln = _rms_bwd(_mm_nt(dgp, wg_ref[...]), xhat, rstd, ln_ref[...])
        dln_ref[...] += dln
        dxo_ref[...] = dxn + dxv

    return _call(
        body, name=name, grid=(T // tm,),
        out_shape=[jax.ShapeDtypeStruct((T, D), F32), jax.ShapeDtypeStruct((T, D), MXU),
                   jax.ShapeDtypeStruct((T, D), MXU), jax.ShapeDtypeStruct(ln.shape, F32)],
        in_specs=[_tile(tm, D)] * 4 + [_whole(ln.shape), _whole(w_g.shape)],
        out_specs=[_tile(tm, D), _tile(tm, D), _tile(tm, D), _acc(ln.shape)],
        compiler_params=_params(32),
    )(dx, x, gate, pp, ln, w_g)


def _mlp_bwd(dx, x, pre, ln, w_up, w_down, name):
    T, D = x.shape
    tm = min(TOKEN_TILE, T)
    nf = w_up.shape[2]
    F = nf * NDEV

    def body(dx_ref, x_ref, pre_ref, ln_ref, wup_ref, wdown_ref, dxo_ref, dpre_ref, s_ref, dln_ref):
        @pl.when(pl.program_id(0) == 0)
        def _():
            dln_ref[...] = jnp.zeros_like(dln_ref)
        dxv = dx_ref[...]
        _, xhat, rstd = _rms(x_ref[...], ln_ref[...])
        a = jnp.maximum(pre_ref[...], 0.0)
        s_ref[...] = (a * a).astype(MXU)
        dpre_ref[...] = (_mm_nt(dxv, wdown_ref[...]) * (2.0 * a)).astype(MXU)
        dh = _mm_nt(dpre_ref[:, :nf], wup_ref[0])
        for j in range(1, NDEV):
            dh += _mm_nt(dpre_ref[:, j * nf:(j + 1) * nf], wup_ref[j])
        dxn, dln = _rms_bwd(dh, xhat, rstd, ln_ref[...])
        dln_ref[...] += dln
        dxo_ref[...] = dxn + dxv

    return _call(
        body, name=name, grid=(T // tm,),
        out_shape=[jax.ShapeDtypeStruct((T, D), F32), jax.ShapeDtypeStruct((T, F), MXU),
                   jax.ShapeDtypeStruct((T, F), MXU), jax.ShapeDtypeStruct(ln.shape, F32)],
        in_specs=[_tile(tm, D), _tile(tm, D), _tile(tm, F), _whole(ln.shape), _whole(w_up.shape),
                  _whole(w_down.shape)],
        out_specs=[_tile(tm, D), _tile(tm, F), _tile(tm, F), _acc(ln.shape)],
        compiler_params=_params(56),
    )(dx, x, pre, ln, w_up, w_down)


def _qkv_bwd(dx, x, dq, dk, dv, q_pre, k_pre, ln_q, ln_kv, g_q, g_k, w_q, w_kv, name):
    T, D = x.shape
    tm = min(TOKEN_TILE, T)
    nk = w_kv.shape[2]
    n_tiles = T // tm

    def body(dx_ref, x_ref, dq_ref, dk_ref, dv_ref, qpre_ref, kpre_ref, lnq_ref, lnkv_ref, gq_ref, gk_ref,
             wq_ref, wkv_ref, dxo_ref, dqp_ref, dkv_ref, dlnq_ref, dlnkv_ref, dgq_ref, dgk_ref, gq_acc, gk_acc):
        i = pl.program_id(0)

        @pl.when(i == 0)
        def _():
            dlnq_ref[...] = jnp.zeros_like(dlnq_ref)
            dlnkv_ref[...] = jnp.zeros_like(dlnkv_ref)
            gq_acc[...] = jnp.zeros_like(gq_acc)
            gk_acc[...] = jnp.zeros_like(gk_acc)

        ones = _head_ones()
        for b in range(D // 128):
            cols = slice(b * 128, (b + 1) * 128)
            _, xh, rs = _head_rms(qpre_ref[:, cols], gq_ref[:, cols], ones)
            d, dg = _head_rms_bwd(dq_ref[:, cols], xh, rs, gq_ref[:, cols], ones)
            dqp_ref[:, cols] = d.astype(MXU)
            gq_acc[:, cols] += dg
            _, xh, rs = _head_rms(kpre_ref[:, cols], gk_ref[:, cols], ones)
            d, dg = _head_rms_bwd(dk_ref[:, cols], xh, rs, gk_ref[:, cols], ones)
            dkv_ref[:, cols] = d.astype(MXU)
            gk_acc[:, cols] += dg
        dkv_ref[:, D:] = dv_ref[...].astype(MXU)

        _, xhat, rstd = _rms(x_ref[...], lnq_ref[...])
        dhq = _mm_nt(dqp_ref[...], wq_ref[...])
        dhkv = _mm_nt(dkv_ref[:, :nk], wkv_ref[0])
        for j in range(1, NDEV):
            dhkv += _mm_nt(dkv_ref[:, j * nk:(j + 1) * nk], wkv_ref[j])
        dxq, dlnq = _rms_bwd(dhq, xhat, rstd, lnq_ref[...])
        dxkv, dlnkv = _rms_bwd(dhkv, xhat, rstd, lnkv_ref[...])
        dlnq_ref[...] += dlnq
        dlnkv_ref[...] += dlnkv
        dxo_ref[...] = dx_ref[...] + dxq + dxkv

        @pl.when(i == n_tiles - 1)
        def _():
            row = lax.broadcasted_iota(jnp.int32, (D, 128), 0)
            col = lax.broadcasted_iota(jnp.int32, (D, 128), 1)
            fold = (jnp.bitwise_and(row, HEAD_DIM - 1) == col).astype(MXU)
            dgq_ref[...] = _split_dot(jnp.broadcast_to(gq_acc[...], (8, D)), fold, 3)
            dgk_ref[...] = _split_dot(jnp.broadcast_to(gk_acc[...], (8, D)), fold, 3)

    small = jax.ShapeDtypeStruct((8, 128), F32)
    return _call(
        body, name=name, grid=(n_tiles,),
        out_shape=[jax.ShapeDtypeStruct((T, D), F32), jax.ShapeDtypeStruct((T, D), MXU),
                   jax.ShapeDtypeStruct((T, 2 * D), MXU), jax.ShapeDtypeStruct(ln_q.shape, F32),
                   jax.ShapeDtypeStruct(ln_kv.shape, F32), small, small],
        in_specs=[_tile(tm, D)] * 7 + [_whole(ln_q.shape), _whole(ln_kv.shape), _whole(g_q.shape),
                                       _whole(g_k.shape), _whole(w_q.shape), _whole(w_kv.shape)],
        out_specs=[_tile(tm, D), _tile(tm, D), _tile(tm, 2 * D), _acc(ln_q.shape), _acc(ln_kv.shape),
                   _acc((8, 128)), _acc((8, 128))],
        scratch_shapes=[pltpu.VMEM((1, D), F32), pltpu.VMEM((1, D), F32)],
        compiler_params=_params(48),
    )(dx, x, dq, dk, dv, q_pre, k_pre, ln_q, ln_kv, g_q, g_k, w_q, w_kv)


def _sgu_bwd(dx, x, z, ln, w_in, g_v, ws, wsT, bT, w_out, name):
    T, D = x.shape
    tm = min(TOKEN_TILE, T)
    nw = w_in.shape[2]

    def body(dx_ref, x_ref, z_ref, ln_ref, win_ref, gv_ref, ws_ref, wsT_ref, bT_ref, wout_ref,
             dxo_ref, dz_ref, dws_ref, dbT_ref, dln_ref, dgv_ref, mix_ref, dvn_ref):
        @pl.when(pl.program_id(0) == 0)
        def _():
            dws_ref[...] = jnp.zeros_like(dws_ref)
            dbT_ref[...] = jnp.zeros_like(dbT_ref)
            dln_ref[...] = jnp.zeros_like(dln_ref)
            dgv_ref[...] = jnp.zeros_like(dgv_ref)
        dxv = dx_ref[...]
        _, xhat, rstd = _rms(x_ref[...], ln_ref[...])
        u, du = _gelu_and_grad(z_ref[:, :D])
        gv, dgv = _gelu_and_grad(z_ref[:, D:])
        vn, vhat, rstd_v = _rms(gv, gv_ref[...])
        vnb = vn.astype(MXU)
        _spatial_mix(vnb, ws_ref, bT_ref, mix_ref, tm)
        dy = _mm_nt(dxv, wout_ref[...])
        d_u = dy * mix_ref[...]
        d_mix = dy * u
        dmb = d_mix.astype(MXU)
        tri, triT = _tril_mask()
        for g in range(GROUPS):
            wmT = jnp.where(triT, wsT_ref[g], 0.0).astype(MXU)
            cols = slice(g * CHUNK, (g + 1) * CHUNK)
            for ch in range(tm // CHUNK):
                rows = slice(ch * CHUNK, (ch + 1) * CHUNK)
                dm = dmb[rows, cols]
                dws_ref[g] += jnp.where(tri, _mm_nt(dm, vnb[rows, cols]), 0.0)
                dbT_ref[:, g:g + 1] += jnp.sum(d_mix[rows, cols], axis=1, keepdims=True)
                dvn_ref[rows, cols] = _mm(wmT, dm)
        d_gv, dg = _rms_bwd(dvn_ref[...], vhat, rstd_v, gv_ref[...])
        dgv_ref[...] += dg
        dz_ref[:, :D] = (d_u * du).astype(MXU)
        dz_ref[:, D:] = (d_gv * dgv).astype(MXU)
        dh = _mm_nt(dz_ref[:, :nw], win_ref[0])
        for j in range(1, NDEV):
            dh += _mm_nt(dz_ref[:, j * nw:(j + 1) * nw], win_ref[j])
        dxn, dln = _rms_bwd(dh, xhat, rstd, ln_ref[...])
        dln_ref[...] += dln
        dxo_ref[...] = dxn + dxv

    return _call(
        body, name=name, grid=(T // tm,),
        out_shape=[jax.ShapeDtypeStruct((T, D), F32), jax.ShapeDtypeStruct((T, 2 * D), MXU),
                   jax.ShapeDtypeStruct(ws.shape, F32), jax.ShapeDtypeStruct(bT.shape, F32),
                   jax.ShapeDtypeStruct(ln.shape, F32), jax.ShapeDtypeStruct(g_v.shape, F32)],
        in_specs=[_tile(tm, D), _tile(tm, D), _tile(tm, 2 * D), _whole(ln.shape), _whole(w_in.shape),
                  _whole(g_v.shape), _whole(ws.shape), _whole(wsT.shape), _whole(bT.shape), _whole(w_out.shape)],
        out_specs=[_tile(tm, D), _tile(tm, 2 * D), _acc(ws.shape), _acc(bT.shape), _acc(ln.shape),
                   _acc(g_v.shape)],
        scratch_shapes=[pltpu.VMEM((tm, D), F32), pltpu.VMEM((tm, D), F32)],
        compiler_params=_params(48),
    )(dx, x, z, ln, w_in, g_v, ws, wsT, bT, w_out)


def _wgrad_rows(a, g, name):
    T, K = a.shape
    N = g.shape[1]
    kb = K // NDEV

    def body(a_ref, g_ref, o_ref):
        o_ref[...] = _mm_tn(a_ref[...], g_ref[...]).astype(COMM)

    return _call(
        body, name=name, grid=(NDEV,), out_shape=jax.ShapeDtypeStruct((K, N), COMM),
        in_specs=[pl.BlockSpec((T, kb), lambda j: (0, j)), _whole(g.shape)],
        out_specs=pl.BlockSpec((kb, N), lambda j: (j, 0)),
        compiler_params=_params(40),
    )(a, g).reshape(NDEV, kb, N)


def _wgrad_cols(a, g, name):
    T, K = a.shape
    N = g.shape[1]
    nb = N // NDEV

    def body(a_ref, g_ref, o_ref):
        o_ref[...] = _mm_tn(a_ref[...], g_ref[...]).astype(COMM)

    return _call(
        body, name=name, grid=(NDEV,), out_shape=jax.ShapeDtypeStruct((NDEV, K, nb), COMM),
        in_specs=[_whole(a.shape), pl.BlockSpec((T, nb), lambda j: (0, j))],
        out_specs=pl.BlockSpec((None, K, nb), lambda j: (j, 0, 0)),
        compiler_params=_params(40),
    )(a, g)


def _adamw(w, m, v, slots, name):
    R, C = w.shape
    n = slots.shape[0]
    tr = math.gcd(R, max(8, (128 * 1024) // C))
    if tr < 64:
        tr = R
    bc1 = 1.0 - ADAM_B1 ** ADAM_STEP
    bc2 = 1.0 - ADAM_B2 ** ADAM_STEP

    def body(w_ref, m_ref, v_ref, s_ref, g_ref, d_ref, mo_ref, vo_ref):
        g = s_ref[0].astype(F32)
        for j in range(1, n):
            g = g + s_ref[j].astype(F32)
        mn = ADAM_B1 * m_ref[...] + (1.0 - ADAM_B1) * g
        vn = ADAM_B2 * v_ref[...] + (1.0 - ADAM_B2) * (g * g)
        g_ref[...] = g
        mo_ref[...] = mn
        vo_ref[...] = vn
        d_ref[...] = -ADAM_LR * ((mn / bc1) / (jnp.sqrt(vn / bc2) + ADAM_EPS) + ADAM_WD * w_ref[...])

    blk = pl.BlockSpec((tr, C), lambda i: (i, 0))
    out = jax.ShapeDtypeStruct((R, C), F32)
    return _call(
        body, name=name, grid=(R // tr,), out_shape=[out, out, out, out],
        in_specs=[blk, blk, blk, pl.BlockSpec((n, tr, C), lambda i: (0, i, 0))], out_specs=[blk] * 4,
        compiler_params=_params(32),
    )(w, m, v, slots)


def _rows128(a):
    flat = a.reshape(-1)
    rows = -(-flat.shape[0] // 1024) * 8
    flat = jnp.pad(flat, (0, rows * 128 - flat.shape[0]))
    return flat.reshape(rows, 128)


def kernel(x, p, ln_mix_a, w_in_a, g_v_a, w_spatial, b_spatial, w_out_a, ln_kv, w_kv, g_k, ln_mix_b, w_q, g_q, w_out_b, ln_mlp, w_up, w_down, ln_ple, w_ple_gate, w_ple_proj, loss_target, m_ln_mix_a, m_w_in_a, m_g_v_a, m_w_spatial, m_b_spatial, m_w_out_a, m_ln_kv, m_w_kv, m_g_k, m_ln_mix_b, m_w_q, m_g_q, m_w_out_b, m_ln_mlp, m_w_up, m_w_down, m_ln_ple, m_w_ple_gate, m_w_ple_proj, v_ln_mix_a, v_w_in_a, v_g_v_a, v_w_spatial, v_b_spatial, v_w_out_a, v_ln_kv, v_w_kv, v_g_k, v_ln_mix_b, v_w_q, v_g_q, v_w_out_b, v_ln_mlp, v_w_up, v_w_down, v_ln_ple, v_w_ple_gate, v_w_ple_proj):
    me = 4 * lax.axis_index("x") + 2 * lax.axis_index("y") + lax.axis_index("c")
    D = x.shape[2]
    x0, tgt = x[0], loss_target[0]
    n_layers = w_up.shape[0]

    c = lambda w: w.astype(COMM)
    first = [c(w_in_a[0]), c(w_out_a[0]), ln_mix_a, g_v_a, c(w_up[0]), c(w_down[0]), c(w_ple_gate[0]),
             c(w_ple_proj[0]), c(w_q[0]), c(w_kv)]
    second = [c(w_out_b[0]), c(w_up[1]), c(w_down[1]), c(w_ple_gate[1]), c(w_ple_proj[1])]
    W_in, W_out_a, ln_a, gv_a, W_up0, W_down0, W_g0, W_pp0, W_q, W_kv = _gather_two_level(first, "gather_first")
    W_out_a, ln_a, gv_a = W_out_a.reshape(-1, D), ln_a.reshape(1, D), gv_a.reshape(1, D)
    W_down0, W_g0, W_q = W_down0.reshape(-1, D), W_g0.reshape(-1, D), W_q.reshape(-1, D)
    ws = w_spatial[0]
    wsT = jnp.swapaxes(ws, 1, 2)
    bT = b_spatial[0].T
    ln_kv2, ln_b = ln_kv.reshape(1, D), ln_mix_b
    gk2 = jnp.tile(g_k.reshape(1, HEAD_DIM), (1, D // HEAD_DIM))
    gq2 = jnp.tile(g_q, (1, D // HEAD_DIM))
    ln_m = [ln_mlp[l:l + 1] for l in range(n_layers)]
    ln_p = [ln_ple[l:l + 1] for l in range(n_layers)]

    x1, z, h_a, y_a = _sgu_fwd(x0, ln_a, W_in, gv_a, ws, bT, W_out_a, "sgu_fwd")
    x2, pre0, hm0 = _mlp_fwd(x1, ln_m[0], W_up0, W_down0, "mlp_fwd0")
    x3, gate0, pp0, hp0 = _ple_fwd(x2, p[0, 0], ln_p[0], W_g0, W_pp0, "ple_fwd0")
    qn, kn, vn, q_pre, k_pre, h_q, h_kv = _qkv_fwd(x3, ln_b, ln_kv2, gq2, gk2, W_q, W_kv, "qkv_fwd")
    o2d, (W_out_b, W_up1, W_down1, W_g1, W_pp1) = _sb_fwd(qn, kn, vn, second, "sb_fwd")
    W_out_b, W_down1, W_g1 = W_out_b.reshape(-1, D), W_down1.reshape(-1, D), W_g1.reshape(-1, D)
    x4 = _proj_res(x3, o2d, W_out_b, "attn_out")
    x5, pre1, hm1 = _mlp_fwd(x4, ln_m[1], W_up1, W_down1, "mlp_fwd1")
    x6, gate1, pp1, hp1 = _ple_fwd(x5, p[1, 0], ln_p[1], W_g1, W_pp1, "ple_fwd1")
    dy, loss_part = _loss_grad(x6, tgt, "loss_grad")
    loss = lax.psum(loss_part[0, 0], ("x", "y", "c"))

    dx5, dpp1, dgp1, dlnp1 = _ple_bwd(dy, x5, gate1, pp1, ln_p[1], W_g1, "ple_bwd1")
    dx4, dpre1, s1, dlnm1 = _mlp_bwd(dx5, x4, pre1, ln_m[1], W_up1, W_down1, "mlp_bwd1")
    wg_second = [_wgrad_cols(hm1, dpre1, "wg_up1"), _wgrad_rows(s1, dx5, "wg_down1"),
                 _wgrad_rows(hp1, dgp1, "wg_gate1"), _wgrad_cols(p[1, 0].astype(MXU), dpp1, "wg_proj1"),
                 _wgrad_rows(o2d, dx4, "wg_out_b")]
    do2d = _proj_nt(dx4, W_out_b, "attn_out_bwd")
    dqn, dkn, dvn, (s_up1, s_down1, s_gate1, s_proj1, s_out_b) = _sb_bwd(qn, kn, vn, o2d, do2d, wg_second, "sb_bwd")
    dx3, dq_pre, dkv, dlnb, dlnkv, dgq, dgk = _qkv_bwd(dx4, x3, dqn, dkn, dvn, q_pre, k_pre, ln_b, ln_kv2, gq2, gk2,
                                                      W_q, W_kv, "qkv_bwd")
    dgq, dgk = dgq[:1, :HEAD_DIM], dgk[:1, :HEAD_DIM]
    dx2, dpp0, dgp0, dlnp0 = _ple_bwd(dx3, x2, gate0, pp0, ln_p[0], W_g0, "ple_bwd0")
    dx1, dpre0, s0, dlnm0 = _mlp_bwd(dx2, x1, pre0, ln_m[0], W_up0, W_down0, "mlp_bwd0")
    dx0, dz, dws, dbT, dlna, dgva = _sgu_bwd(dx1, x0, z, ln_a, W_in, gv_a, ws, wsT, bT, W_out_a, "sgu_bwd")
    wg_first = [_wgrad_rows(h_q, dq_pre, "wg_q"), _wgrad_cols(h_kv, dkv, "wg_kv"),
                _wgrad_cols(hm0, dpre0, "wg_up0"), _wgrad_rows(s0, dx2, "wg_down0"), _wgrad_rows(hp0, dgp0, "wg_gate0"),
                _wgrad_cols(p[0, 0].astype(MXU), dpp0, "wg_proj0"), _wgrad_cols(h_a, dz, "wg_in_a"),
                _wgrad_rows(y_a, dx1, "wg_out_a")]

    small = [("w_spatial", dws[None], w_spatial, m_w_spatial, v_w_spatial),
             ("b_spatial", dbT.T[None], b_spatial, m_b_spatial, v_b_spatial),
             ("ln_kv", dlnkv.reshape(-1), ln_kv, m_ln_kv, v_ln_kv),
             ("g_k", dgk.reshape(-1), g_k, m_g_k, v_g_k),
             ("ln_mix_b", dlnb, ln_mix_b, m_ln_mix_b, v_ln_mix_b),
             ("g_q", dgq, g_q, m_g_q, v_g_q),
             ("ln_mlp", jnp.concatenate([dlnm0, dlnm1]), ln_mlp, m_ln_mlp, v_ln_mlp),
             ("ln_ple", jnp.concatenate([dlnp0, dlnp1]), ln_ple, m_ln_ple, v_ln_ple)]
    sharded_vec = [("ln_mix_a", dlna, ln_mix_a, m_ln_mix_a, v_ln_mix_a),
                   ("g_v_a", dgva, g_v_a, m_g_v_a, v_g_v_a)]
    packs = [[], [], [], []]
    for _, g, w, m, v in small:
        for lst, a in zip(packs, (g, w, m, v)):
            lst.append(_rows128(a))
    for _, g, w, m, v in sharded_vec:
        packs[0].append(g.reshape(NDEV, -1))
        for lst, a in zip(packs[1:], (w, m, v)):
            lst.append(jnp.broadcast_to(a, (NDEV, a.shape[1])))
    g_pack, w_pack, m_pack, v_pack = (jnp.concatenate(lst) for lst in packs)
    g_pack8 = jnp.broadcast_to(g_pack[None], (NDEV,) + g_pack.shape)
    by_chip = [a.reshape((4, 2) + a.shape[1:]) for a in wg_first]
    from_sibling, (g_all,) = _scatter_pair(by_chip, [g_pack8], "scatter_pair")
    my_core = lax.axis_index("c")
    chip_sums = [_pair_sum(lax.dynamic_index_in_dim(a, my_core, 1, keepdims=False), o, f"pair_sum{j}")
                 for j, (a, o) in enumerate(zip(by_chip, from_sibling))]
    s_q, s_kv, s_up0, s_down0, s_gate0, s_proj0, s_in_a, s_out_a = _scatter_chips(chip_sums, "scatter_chips")

    def upd(w, m, v, s, name):
        shape = w.shape
        outs = _adamw(w.reshape(-1, shape[-1]), m.reshape(-1, shape[-1]), v.reshape(-1, shape[-1]), s, name)
        return [o.reshape(shape) for o in outs]

    res = {}
    per = {}
    per["w_up", 1] = upd(w_up[1], m_w_up[1], v_w_up[1], s_up1, "adam_up1")
    per["w_down", 1] = upd(w_down[1], m_w_down[1], v_w_down[1], s_down1, "adam_down1")
    per["w_ple_gate", 1] = upd(w_ple_gate[1], m_w_ple_gate[1], v_w_ple_gate[1], s_gate1, "adam_gate1")
    per["w_ple_proj", 1] = upd(w_ple_proj[1], m_w_ple_proj[1], v_w_ple_proj[1], s_proj1, "adam_proj1")
    res["w_out_b"] = upd(w_out_b, m_w_out_b, v_w_out_b, s_out_b, "adam_out_b")
    res["w_q"] = upd(w_q, m_w_q, v_w_q, s_q, "adam_q")
    res["w_kv"] = upd(w_kv, m_w_kv, v_w_kv, s_kv, "adam_kv")
    per["w_up", 0] = upd(w_up[0], m_w_up[0], v_w_up[0], s_up0, "adam_up0")
    per["w_down", 0] = upd(w_down[0], m_w_down[0], v_w_down[0], s_down0, "adam_down0")
    per["w_ple_gate", 0] = upd(w_ple_gate[0], m_w_ple_gate[0], v_w_ple_gate[0], s_gate0, "adam_gate0")
    per["w_ple_proj", 0] = upd(w_ple_proj[0], m_w_ple_proj[0], v_w_ple_proj[0], s_proj0, "adam_proj0")
    res["w_in_a"] = upd(w_in_a, m_w_in_a, v_w_in_a, s_in_a, "adam_in_a")
    res["w_out_a"] = upd(w_out_a, m_w_out_a, v_w_out_a, s_out_a, "adam_out_a")
    for nm in ("w_up", "w_down", "w_ple_gate", "w_ple_proj"):
        res[nm] = [jnp.stack([per[nm, l][t] for l in range(n_layers)]) for t in range(4)]

    outs = _adamw(w_pack, m_pack, v_pack, g_all, "adam_small")
    row = 0
    for nm, g, w, m, v in small:
        nrows = _rows128(w).shape[0]
        res[nm] = [o[row:row + nrows].reshape(-1)[:w.size].reshape(w.shape) for o in outs]
        row += nrows
    for nm, g, w, m, v in sharded_vec:
        res[nm] = [lax.dynamic_slice_in_dim(o[row:row + NDEV], me, 1, axis=0) for o in outs]
        row += NDEV

    names = ["ln_mix_a", "w_in_a", "g_v_a", "w_spatial", "b_spatial", "w_out_a", "ln_kv", "w_kv", "g_k", "ln_mix_b",
             "w_q", "g_q", "w_out_b", "ln_mlp", "w_up", "w_down", "ln_ple", "w_ple_gate", "w_ple_proj"]
    out = [loss, dx0[None]]
    for t in range(4):
        out += [res[nm][t] for nm in names]
    return tuple(out)
```

```python
import functools
import math

import jax
import jax.numpy as jnp
from jax import lax
from jax.experimental import pallas as pl
from jax.experimental.pallas import tpu as pltpu

F32 = jnp.float32
MXU = jnp.bfloat16
COMM = jnp.bfloat16
EPS = 1e-6
NDEV = 8
HEAD_DIM = 64
CHUNK = 128
GROUPS = 8
QBLK = 128
SCALE = HEAD_DIM ** -0.5
TOKEN_TILE = 256
ADAM_LR = 0.001
ADAM_B1 = 0.9
ADAM_B2 = 0.999
ADAM_EPS = 1e-08
ADAM_WD = 0.01
ADAM_STEP = 10
MESH = pl.DeviceIdType.MESH


def _call(body, **kw):
    return pl.pallas_call(body, **kw)


def _params(vmem_mb, n_axes=1):
    return pltpu.CompilerParams(dimension_semantics=("arbitrary",) * n_axes,
                                vmem_limit_bytes=vmem_mb << 20)


def _tile(tm, n):
    return pl.BlockSpec((tm, n), lambda i: (i, 0))


def _whole(shape):
    zeros = (0,) * len(shape)
    return pl.BlockSpec(shape, lambda i: zeros, pipeline_mode=pl.Buffered(1))


def _acc(shape):
    zeros = (0,) * len(shape)
    return pl.BlockSpec(shape, lambda i: zeros)


def _mm(a, b):
    return jnp.dot(a.astype(MXU), b.astype(MXU), preferred_element_type=F32)


def _mm_nt(a, b):
    return lax.dot_general(a.astype(MXU), b.astype(MXU), (((1,), (1,)), ((), ())),
                           preferred_element_type=F32)


def _mm_tn(a, b):
    return lax.dot_general(a.astype(MXU), b.astype(MXU), (((0,), (0,)), ((), ())),
                           preferred_element_type=F32)


def _split_dot(x, ones, terms=2):
    out = None
    for _ in range(terms):
        part = x.astype(MXU)
        x = x - part.astype(F32)
        d = jnp.dot(part, ones, preferred_element_type=F32)
        out = d if out is None else out + d
    return out


def _rms(x, g):
    rstd = lax.rsqrt(jnp.mean(x * x, axis=-1, keepdims=True) + EPS)
    xhat = x * rstd
    return xhat * g, xhat, rstd


def _rms_bwd(dh, xhat, rstd, g):
    dxh = dh * g
    dx = rstd * (dxh - xhat * jnp.mean(dxh * xhat, axis=-1, keepdims=True))
    dg = jnp.sum(dh * xhat, axis=0, keepdims=True)
    return dx, dg


_GELU_C = math.sqrt(2.0 / math.pi)


def _gelu(x):
    t = jnp.tanh(_GELU_C * (x + 0.044715 * (x * x * x)))
    return 0.5 * x * (1.0 + t)


def _gelu_and_grad(x):
    x2 = x * x
    t = jnp.tanh(_GELU_C * (x + 0.044715 * (x2 * x)))
    g = 0.5 * x * (1.0 + t)
    dg = 0.5 * (1.0 + t) + 0.5 * x * (1.0 - t * t) * (_GELU_C * (1.0 + 3.0 * 0.044715 * x2))
    return g, dg


def _softplus(z):
    return jnp.maximum(z, 0.0) + jnp.log1p(jnp.exp(-jnp.abs(z)))


def _tril_mask():
    row = lax.broadcasted_iota(jnp.int32, (CHUNK, CHUNK), 0)
    col = lax.broadcasted_iota(jnp.int32, (CHUNK, CHUNK), 1)
    return row >= col, row <= col


ANY_SPEC = pl.BlockSpec(memory_space=pl.ANY)


def _my_index():
    return 4 * lax.axis_index("x") + 2 * lax.axis_index("y") + lax.axis_index("c")


def _exchange_copies(srcs, lands, send_sems, recv_sems, scatter, arriving):
    x, y, c = lax.axis_index("x"), lax.axis_index("y"), lax.axis_index("c")
    me = 4 * x + 2 * y + c
    out = []
    for a in range(len(srcs)):
        for k in range(NDEV - 1):
            bits = k + 1
            px = 1 - x if (bits >> 2) & 1 else x
            py = 1 - y if (bits >> 1) & 1 else y
            pc = 1 - c if bits & 1 else c
            peer = 4 * px + 2 * py + pc
            src = srcs[a].at[peer] if scatter else srcs[a]
            out.append(pltpu.make_async_remote_copy(
                src_ref=src, dst_ref=lands[a].at[peer if arriving else me],
                send_sem=send_sems.at[a * (NDEV - 1) + k], recv_sem=recv_sems.at[a * (NDEV - 1) + k],
                device_id=(px, py, pc), device_id_type=MESH))
    return out


def _exchange_shapes(arrs, scatter):
    n = len(arrs)
    lands = [jax.ShapeDtypeStruct(a.shape if scatter else (NDEV,) + a.shape, a.dtype) for a in arrs]
    sems = [pltpu.SemaphoreType.DMA((n * (NDEV - 1),)), pltpu.SemaphoreType.DMA((n * (NDEV - 1),)),
            pltpu.SemaphoreType.DMA((n,))]
    return lands, sems


def _exchange_start(srcs, lands, sems, scatter):
    send_sems, recv_sems, local_sems = sems
    me = _my_index()
    for a in range(len(srcs)):
        pltpu.make_async_copy(srcs[a].at[me] if scatter else srcs[a], lands[a].at[me], local_sems.at[a]).start()
    for send in _exchange_copies(srcs, lands, send_sems, recv_sems, scatter, False):
        send.start()


def _exchange_finish(srcs, lands, sems, scatter):
    send_sems, recv_sems, local_sems = sems
    me = _my_index()
    for arrive in _exchange_copies(srcs, lands, send_sems, recv_sems, scatter, True):
        arrive.wait_recv()
    for send in _exchange_copies(srcs, lands, send_sems, recv_sems, scatter, False):
        send.wait_send()
    for a in range(len(srcs)):
        pltpu.make_async_copy(srcs[a].at[me] if scatter else srcs[a], lands[a].at[me], local_sems.at[a]).wait()


def _gather_two_level(arrs, name):
    n = len(arrs)
    lands = [jax.ShapeDtypeStruct((NDEV,) + a.shape, a.dtype) for a in arrs]

    def body(*refs):
        srcs, outs = refs[:n], refs[n:2 * n]
        send_sems, recv_sems, local_sems = refs[2 * n:]
        x, y, c = lax.axis_index("x"), lax.axis_index("y"), lax.axis_index("c")
        me, sibling = (x, y, c), (x, y, 1 - c)
        chips = [(1 - x, y), (x, 1 - y), (1 - x, 1 - y)]

        def index(dev):
            return 4 * dev[0] + 2 * dev[1] + dev[2]

        def copy(a, k, block, to, src=None):
            dst = outs[a].at[index(block)]
            return pltpu.make_async_remote_copy(
                src_ref=dst if src is None else src, dst_ref=dst, send_sem=send_sems.at[a, k],
                recv_sem=recv_sems.at[a, k], device_id=to, device_id_type=MESH)

        mine, first, passed = [], [], []
        for a in range(n):
            cp = pltpu.make_async_copy(srcs[a], outs[a].at[index(me)], local_sems.at[a])
            cp.start()
            mine.append(cp)
            first.append(copy(a, 0, me, sibling, src=srcs[a]))
            first += [copy(a, 1 + j, me, (*chip, c), src=srcs[a]) for j, chip in enumerate(chips)]
        for cp in first:
            cp.start()
        for a in range(n):
            for j, chip in enumerate(chips):
                copy(a, 1 + j, (*chip, c), me).wait_recv()
                cp = copy(a, 4 + j, (*chip, c), sibling)
                cp.start()
                passed.append(cp)
        for a in range(n):
            copy(a, 0, sibling, me).wait_recv()
            for j, chip in enumerate(chips):
                copy(a, 4 + j, (*chip, 1 - c), me).wait_recv()
        for cp in first + passed:
            cp.wait_send()
        for cp in mine:
            cp.wait()

    return _call(body, name=name, out_shape=lands, in_specs=[ANY_SPEC] * n, out_specs=[ANY_SPEC] * n,
                 scratch_shapes=[pltpu.SemaphoreType.DMA((n, NDEV - 1)), pltpu.SemaphoreType.DMA((n, NDEV - 1)),
                                 pltpu.SemaphoreType.DMA((n,))])(*arrs)


def _scatter_pair(arrs, extra, name):
    n, ne = len(arrs), len(extra)
    lands = [jax.ShapeDtypeStruct((4,) + a.shape[2:], a.dtype) for a in arrs]
    extra_lands, extra_sems = _exchange_shapes(extra, True)

    def body(*refs):
        srcs, xsrc = refs[:n], refs[n:n + ne]
        outs, xout = refs[n + ne:2 * n + ne], refs[2 * n + ne:2 * (n + ne)]
        send_sems, recv_sems = refs[2 * (n + ne)], refs[2 * (n + ne) + 1]
        xsems = refs[2 * (n + ne) + 2:]
        x, y, c = lax.axis_index("x"), lax.axis_index("y"), lax.axis_index("c")
        _exchange_start(xsrc, xout, xsems, True)
        copies = [pltpu.make_async_remote_copy(
            src_ref=srcs[a].at[k, 1 - c], dst_ref=outs[a].at[k], send_sem=send_sems.at[a, k],
            recv_sem=recv_sems.at[a, k], device_id=(x, y, 1 - c), device_id_type=MESH)
            for a in range(n) for k in range(4)]
        for cp in copies:
            cp.start()
        for cp in copies:
            cp.wait()
        _exchange_finish(xsrc, xout, xsems, True)

    outs = _call(
        body, name=name, out_shape=lands + extra_lands, in_specs=[ANY_SPEC] * (n + ne), out_specs=[ANY_SPEC] * (n + ne),
        scratch_shapes=[pltpu.SemaphoreType.DMA((n, 4)), pltpu.SemaphoreType.DMA((n, 4))] + extra_sems,
    )(*arrs, *extra)
    return outs[:n], outs[n:]


def _pair_sum(own, other, name):
    _, R, C = own.shape
    tr = math.gcd(R, max(8, (128 * 1024) // C))

    def body(a_ref, b_ref, o_ref):
        o_ref[...] = (a_ref[...].astype(F32) + b_ref[...].astype(F32)).astype(COMM)

    blk = pl.BlockSpec((4, tr, C), lambda i: (0, i, 0))
    return _call(body, name=name, grid=(R // tr,), out_shape=jax.ShapeDtypeStruct(own.shape, COMM),
                 in_specs=[blk, blk], out_specs=blk, compiler_params=_params(32))(own, other)


def _scatter_chips(arrs, name):
    n = len(arrs)
    lands = [jax.ShapeDtypeStruct(a.shape, a.dtype) for a in arrs]

    def body(*refs):
        srcs, outs = refs[:n], refs[n:2 * n]
        send_sems, recv_sems, local_sems = refs[2 * n:]
        x, y, c = lax.axis_index("x"), lax.axis_index("y"), lax.axis_index("c")
        chip = 2 * x + y
        others = [(1 - x, y), (x, 1 - y), (1 - x, 1 - y)]
        local = [pltpu.make_async_copy(srcs[a].at[chip], outs[a].at[chip], local_sems.at[a]) for a in range(n)]
        for cp in local:
            cp.start()

        def copies(arriving):
            return [pltpu.make_async_remote_copy(
                src_ref=srcs[a].at[2 * px + py], dst_ref=outs[a].at[2 * px + py if arriving else chip],
                send_sem=send_sems.at[a, j], recv_sem=recv_sems.at[a, j], device_id=(px, py, c), device_id_type=MESH)
                for a in range(n) for j, (px, py) in enumerate(others)]

        for cp in copies(False):
            cp.start()
        for cp in copies(True):
            cp.wait_recv()
        for cp in copies(False):
            cp.wait_send()
        for cp in local:
            cp.wait()

    return _call(body, name=name, out_shape=lands, in_specs=[ANY_SPEC] * n, out_specs=[ANY_SPEC] * n,
                 scratch_shapes=[pltpu.SemaphoreType.DMA((n, 3)), pltpu.SemaphoreType.DMA((n, 3)),
                                 pltpu.SemaphoreType.DMA((n,))])(*arrs)


def _exchange(arrs, scatter, name):
    n = len(arrs)
    lands, sems = _exchange_shapes(arrs, scatter)

    def body(*refs):
        _exchange_start(refs[:n], refs[n:2 * n], refs[2 * n:], scatter)
        _exchange_finish(refs[:n], refs[n:2 * n], refs[2 * n:], scatter)

    return _call(body, name=name, out_shape=lands, in_specs=[ANY_SPEC] * n, out_specs=[ANY_SPEC] * n,
                 scratch_shapes=sems)(*arrs)


def _spatial_mix(vnb, ws_ref, bT_ref, mix_ref, tm):
    tri, _ = _tril_mask()
    for g in range(GROUPS):
        wm = jnp.where(tri, ws_ref[g], 0.0).astype(MXU)
        cols = slice(g * CHUNK, (g + 1) * CHUNK)
        for ch in range(tm // CHUNK):
            rows = slice(ch * CHUNK, (ch + 1) * CHUNK)
            mix_ref[rows, cols] = _mm(wm, vnb[rows, cols]) + bT_ref[:, g:g + 1]


def _sgu_fwd(x, ln, w_in, g_v, ws, bT, w_out, name):
    T, D = x.shape
    tm = min(TOKEN_TILE, T)
    nw = w_in.shape[2]

    def body(x_ref, ln_ref, win_ref, gv_ref, ws_ref, bT_ref, wout_ref, xo_ref, z_ref, h_ref, y_ref, mix_ref):
        xv = x_ref[...]
        h, _, _ = _rms(xv, ln_ref[...])
        hb = h.astype(MXU)
        h_ref[...] = hb
        for j in range(NDEV):
            z_ref[:, j * nw:(j + 1) * nw] = _mm(hb, win_ref[j])
        u = _gelu(z_ref[:, :D])
        gv = _gelu(z_ref[:, D:])
        vn, _, _ = _rms(gv, gv_ref[...])
        _spatial_mix(vn.astype(MXU), ws_ref, bT_ref, mix_ref, tm)
        y = (u * mix_ref[...]).astype(MXU)
        y_ref[...] = y
        xo_ref[...] = xv + _mm(y, wout_ref[...])

    return _call(
        body, name=name, grid=(T // tm,),
        out_shape=[jax.ShapeDtypeStruct((T, D), F32), jax.ShapeDtypeStruct((T, 2 * D), F32),
                   jax.ShapeDtypeStruct((T, D), MXU), jax.ShapeDtypeStruct((T, D), MXU)],
        in_specs=[_tile(tm, D), _whole(ln.shape), _whole(w_in.shape), _whole(g_v.shape), _whole(ws.shape),
                  _whole(bT.shape), _whole(w_out.shape)],
        out_specs=[_tile(tm, D), _tile(tm, 2 * D), _tile(tm, D), _tile(tm, D)],
        scratch_shapes=[pltpu.VMEM((tm, D), F32)],
        compiler_params=_params(40),
    )(x, ln, w_in, g_v, ws, bT, w_out)


def _mlp_fwd(x, ln, w_up, w_down, name):
    T, D = x.shape
    tm = min(TOKEN_TILE, T)
    nf = w_up.shape[2]
    F = nf * NDEV

    def body(x_ref, ln_ref, wup_ref, wdown_ref, xo_ref, pre_ref, h_ref):
        xv = x_ref[...]
        h, _, _ = _rms(xv, ln_ref[...])
        hb = h.astype(MXU)
        h_ref[...] = hb
        for j in range(NDEV):
            pre_ref[:, j * nf:(j + 1) * nf] = _mm(hb, wup_ref[j])
        a = jnp.maximum(pre_ref[...], 0.0)
        xo_ref[...] = xv + _mm(a * a, wdown_ref[...])

    return _call(
        body, name=name, grid=(T // tm,),
        out_shape=[jax.ShapeDtypeStruct((T, D), F32), jax.ShapeDtypeStruct((T, F), F32),
                   jax.ShapeDtypeStruct((T, D), MXU)],
        in_specs=[_tile(tm, D), _whole(ln.shape), _whole(w_up.shape), _whole(w_down.shape)],
        out_specs=[_tile(tm, D), _tile(tm, F), _tile(tm, D)],
        compiler_params=_params(52),
    )(x, ln, w_up, w_down)


def _ple_fwd(x, p, ln, w_g, w_pp, name):
    T, D = x.shape
    tm = min(TOKEN_TILE, T)
    npp = w_pp.shape[2]

    def body(x_ref, p_ref, ln_ref, wg_ref, wpp_ref, xo_ref, gate_ref, pp_ref, h_ref):
        xv = x_ref[...]
        h, _, _ = _rms(xv, ln_ref[...])
        hb = h.astype(MXU)
        h_ref[...] = hb
        gate = jax.nn.sigmoid(_mm(hb, wg_ref[...]))
        gate_ref[...] = gate
        pb = p_ref[...].astype(MXU)
        for j in range(NDEV):
            pp_ref[:, j * npp:(j + 1) * npp] = _mm(pb, wpp_ref[j])
        xo_ref[...] = xv + pp_ref[...] * gate

    return _call(
        body, name=name, grid=(T // tm,),
        out_shape=[jax.ShapeDtypeStruct((T, D), F32), jax.ShapeDtypeStruct((T, D), F32),
                   jax.ShapeDtypeStruct((T, D), F32), jax.ShapeDtypeStruct((T, D), MXU)],
        in_specs=[_tile(tm, D), _tile(tm, p.shape[1]), _whole(ln.shape), _whole(w_g.shape), _whole(w_pp.shape)],
        out_specs=[_tile(tm, D), _tile(tm, D), _tile(tm, D), _tile(tm, D)],
        compiler_params=_params(32),
    )(x, p, ln, w_g, w_pp)


def _head_ones():
    row = lax.broadcasted_iota(jnp.int32, (128, 128), 0)
    col = lax.broadcasted_iota(jnp.int32, (128, 128), 1)
    return (jnp.right_shift(row, 6) == jnp.right_shift(col, 6)).astype(MXU)


def _head_rms(x, g, ones):
    rstd = lax.rsqrt(_split_dot(x * x, ones, 3) * (1.0 / HEAD_DIM) + EPS)
    xhat = x * rstd
    return xhat * g, xhat, rstd


def _head_rms_bwd(dh, xhat, rstd, g, ones):
    dxh = dh * g
    mean = _split_dot(dxh * xhat, ones, 3) * (1.0 / HEAD_DIM)
    return rstd * (dxh - xhat * mean), jnp.sum(dh * xhat, axis=0, keepdims=True)


def _qkv_fwd(x, ln_q, ln_kv, g_q, g_k, w_q, w_kv, name):
    T, D = x.shape
    tm = min(TOKEN_TILE, T)
    nk = w_kv.shape[2]
    half = NDEV // 2

    def body(x_ref, lnq_ref, lnkv_ref, gq_ref, gk_ref, wq_ref, wkv_ref,
             q_ref, k_ref, v_ref, qpre_ref, kpre_ref, hq_ref, hkv_ref):
        xv = x_ref[...]
        _, xhat, _ = _rms(xv, lnq_ref[...])
        hq = (xhat * lnq_ref[...]).astype(MXU)
        hkv = (xhat * lnkv_ref[...]).astype(MXU)
        hq_ref[...] = hq
        hkv_ref[...] = hkv
        qpre_ref[...] = _mm(hq, wq_ref[...])
        for j in range(half):
            kpre_ref[:, j * nk:(j + 1) * nk] = _mm(hkv, wkv_ref[j])
            v_ref[:, j * nk:(j + 1) * nk] = _mm(hkv, wkv_ref[half + j]).astype(MXU)
        ones = _head_ones()
        for b in range(D // 128):
            cols = slice(b * 128, (b + 1) * 128)
            qn, _, _ = _head_rms(qpre_ref[:, cols], gq_ref[:, cols], ones)
            q_ref[:, cols] = (qn * SCALE).astype(MXU)
            kn, _, _ = _head_rms(kpre_ref[:, cols], gk_ref[:, cols], ones)
            k_ref[:, cols] = kn.astype(MXU)

    return _call(
        body, name=name, grid=(T // tm,),
        out_shape=[jax.ShapeDtypeStruct((T, D), MXU)] * 3 + [jax.ShapeDtypeStruct((T, D), F32)] * 2
        + [jax.ShapeDtypeStruct((T, D), MXU)] * 2,
        in_specs=[_tile(tm, D), _whole(ln_q.shape), _whole(ln_kv.shape), _whole(g_q.shape), _whole(g_k.shape),
                  _whole(w_q.shape), _whole(w_kv.shape)],
        out_specs=[_tile(tm, D)] * 7,
        compiler_params=_params(40),
    )(x, ln_q, ln_kv, g_q, g_k, w_q, w_kv)


SB_KEYS = 2 * QBLK


def _sb_consts():
    row = lax.broadcasted_iota(jnp.int32, (QBLK, QBLK), 0)
    col = lax.broadcasted_iota(jnp.int32, (QBLK, QBLK), 1)
    lane = lax.broadcasted_iota(jnp.int32, (QBLK, 128), 1)
    ones = jnp.ones((QBLK, QBLK), MXU)
    later = jnp.concatenate([(row > col).astype(MXU), ones], axis=1)
    later_eq = jnp.concatenate([(row >= col).astype(MXU), ones], axis=1)
    return later, later_eq, lane < HEAD_DIM


def _sb_window(i, w, diagonal):
    upper = (i + 1) * QBLK - w * SB_KEYS
    start = pl.multiple_of(jnp.maximum(upper - SB_KEYS, 0), QBLK)
    row = jnp.bitwise_and(lax.broadcasted_iota(jnp.int32, (2 * QBLK, SB_KEYS), 0), QBLK - 1)
    key = lax.broadcasted_iota(jnp.int32, (2 * QBLK, SB_KEYS), 1) + start
    return start, (key < row + i * QBLK) if diagonal else (key < upper)


MASKED_LOG = -1e30


def _sb_terms(x, terms):
    x = jnp.concatenate([x[:, :QBLK], x[:, QBLK:]], axis=0)
    out = []
    for _ in range(terms):
        part = x.astype(MXU)
        x = x - part.astype(F32)
        out.append(part)
    return tuple(out)


def _sb_suffix(parts, ones, carry):
    s = None
    for part in parts:
        d = jnp.dot(part, ones, preferred_element_type=F32)
        s = d if s is None else s + d
    rows = s.shape[0] // 2
    s_lo, sum_lo, s_hi, sum_hi = s[:rows, :QBLK], s[:rows, QBLK:], s[rows:, :QBLK], s[rows:, QBLK:]
    return jnp.concatenate([s_lo + (carry + sum_hi), s_hi + carry], axis=1), carry + (sum_lo + sum_hi)


def _sb_scores(z, mask):
    sp = _softplus(z)
    l, log_sig = -sp, z - sp
    if mask is not None:
        l = jnp.where(mask, l, 0.0)
        log_sig = jnp.where(mask, log_sig, MASKED_LOG)
    return log_sig, _sb_terms(l, 2)


def _sb_weights(staged, later, c_l):
    log_sig, parts = staged
    b, c_l = _sb_suffix(parts, later, c_l)
    return jnp.exp(log_sig + b), c_l


DEAD_LOG = -88.0


def _sb_alive(carry):
    return (jnp.max(carry[0]) > DEAD_LOG).astype(jnp.int32)


def _ride_along(refs, n, scatter):
    step = pl.program_id(0) * pl.num_programs(1) + pl.program_id(1)
    srcs, lands, sems = refs[:n], refs[n:2 * n], refs[2 * n:]

    @pl.when(step == 0)
    def _():
        _exchange_start(srcs, lands, sems, scatter)

    def finish():
        @pl.when(step == pl.num_programs(0) * pl.num_programs(1) - 1)
        def _():
            _exchange_finish(srcs, lands, sems, scatter)

    return finish


def _sb_fwd(q, k, v, cargo, name):
    T, D = q.shape
    nc = len(cargo)
    lands, sems = _exchange_shapes(cargo, False)

    def body(q_ref, k_ref, v_ref, *rest):
        o_ref = rest[nc]
        finish = _ride_along(rest[:nc] + rest[nc + 1:], nc, False)
        i = pl.program_id(1)
        n_steps = (i + 2) // 2
        later, _, first = _sb_consts()
        qv = q_ref[...]
        zero = jnp.zeros_like(qv)
        q2 = jnp.concatenate([jnp.where(first, qv, zero), jnp.where(first, zero, qv)], axis=0)

        def window(w, carry, diagonal):
            start, mask = _sb_window(i, w, diagonal)
            kw = k_ref[pl.ds(start, SB_KEYS), :]
            vw = v_ref[pl.ds(start, SB_KEYS), :]
            c_l, acc = carry
            a, c_l = _sb_weights(_sb_scores(_mm_nt(q2, kw), mask), later, c_l)
            return c_l, acc + _mm(a, vw)

        def step(state):
            w, _, carry = state
            carry = window(w, carry, False)
            return w + 1, _sb_alive(carry), carry

        carry = window(0, (jnp.zeros((2 * QBLK, 128), F32),) * 2, True)
        _, _, carry = lax.while_loop(lambda s: (s[0] < n_steps) & (s[1] > 0), step,
                                     (jnp.int32(1), _sb_alive(carry), carry))
        o_ref[...] = jnp.where(first, carry[1][:QBLK], carry[1][QBLK:])
        finish()

    qblk = pl.BlockSpec((QBLK, 128), lambda h, i: (i, h))
    kblk = pl.BlockSpec((T, 128), lambda h, i: (0, h))
    outs = _call(
        body, name=name, grid=(D // 128, T // QBLK), out_shape=[jax.ShapeDtypeStruct((T, D), F32)] + lands,
        in_specs=[qblk, kblk, kblk] + [ANY_SPEC] * nc, out_specs=[qblk] + [ANY_SPEC] * nc, scratch_shapes=sems,
        compiler_params=_params(32, 2),
    )(q, k, v, *cargo)
    return outs[0], outs[1:]


def _sb_bwd(q, k, v, o, do, cargo, name):
    T, D = q.shape
    nc = len(cargo)
    lands, sems = _exchange_shapes(cargo, True)

    def body(q_ref, k_ref, v_ref, o_ref, do_ref, *rest):
        dq_ref, dk_ref, dv_ref = rest[nc:nc + 3]
        finish = _ride_along(rest[:nc] + rest[nc + 3:], nc, True)
        i = pl.program_id(1)

        @pl.when(i == 0)
        def _():
            dk_ref[...] = jnp.zeros_like(dk_ref)
            dv_ref[...] = jnp.zeros_like(dv_ref)

        n_steps = (i + 2) // 2
        later, later_eq, first = _sb_consts()
        qv = q_ref[...]
        dob = do_ref[...].astype(MXU)
        zero = jnp.zeros_like(qv)
        q2 = jnp.concatenate([jnp.where(first, qv, zero), jnp.where(first, zero, qv)], axis=0)
        do2 = jnp.concatenate([jnp.where(first, dob, zero), jnp.where(first, zero, dob)], axis=0)
        prod = o_ref[...] * dob.astype(F32)
        prod2 = jnp.concatenate([jnp.where(first, prod, 0.0), jnp.where(first, 0.0, prod)], axis=0)
        total = _split_dot(prod2, jnp.ones((128, 128), MXU), 3)
        total = jnp.concatenate([total, total], axis=1)

        def window(w, carry, diagonal):
            start, mask = _sb_window(i, w, diagonal)
            kw = k_ref[pl.ds(start, SB_KEYS), :]
            vw = v_ref[pl.ds(start, SB_KEYS), :]
            c_l, c_e, dq = carry
            log_sig, parts = _sb_scores(_mm_nt(q2, kw), mask)
            a, c_l = _sb_weights((log_sig, parts), later, c_l)
            ab = a.astype(MXU)
            e = ab.astype(F32) * _mm_nt(do2, vw)
            from_here, c_e = _sb_suffix(_sb_terms(e, 3), later_eq, c_e)
            sig = jnp.exp(log_sig)
            dzb = (e * (1.0 - sig) - sig * (total - from_here)).astype(MXU)
            dk_ref[pl.ds(start, SB_KEYS), :] += _mm_tn(dzb, q2)
            dv_ref[pl.ds(start, SB_KEYS), :] += _mm_tn(ab, do2)
            return c_l, c_e, dq + _mm(dzb, kw)

        def step(state):
            w, _, carry = state
            carry = window(w, carry, False)
            return w + 1, _sb_alive(carry), carry

        carry = window(0, (jnp.zeros((2 * QBLK, 128), F32),) * 3, True)
        _, _, carry = lax.while_loop(lambda s: (s[0] < n_steps) & (s[1] > 0), step,
                                     (jnp.int32(1), _sb_alive(carry), carry))
        dq_ref[...] = jnp.where(first, carry[2][:QBLK], carry[2][QBLK:]) * SCALE
        finish()

    qblk = pl.BlockSpec((QBLK, 128), lambda h, i: (i, h))
    kblk = pl.BlockSpec((T, 128), lambda h, i: (0, h))
    full = jax.ShapeDtypeStruct((T, D), F32)
    outs = _call(
        body, name=name, grid=(D // 128, T // QBLK), out_shape=[full, full, full] + lands,
        in_specs=[qblk, kblk, kblk, qblk, qblk] + [ANY_SPEC] * nc, out_specs=[qblk, kblk, kblk] + [ANY_SPEC] * nc,
        scratch_shapes=sems, compiler_params=_params(32, 2),
    )(q, k, v, o, do, *cargo)
    return outs[0], outs[1], outs[2], outs[3:]


def _proj_res(x, a, w, name):
    T, D = x.shape
    tm = min(TOKEN_TILE, T)

    def body(x_ref, a_ref, w_ref, o_ref):
        o_ref[...] = x_ref[...] + _mm(a_ref[...], w_ref[...])

    return _call(
        body, name=name, grid=(T // tm,), out_shape=jax.ShapeDtypeStruct((T, D), F32),
        in_specs=[_tile(tm, D), _tile(tm, a.shape[1]), _whole(w.shape)], out_specs=_tile(tm, D),
        compiler_params=_params(32),
    )(x, a, w)


def _proj_nt(g, w, name):
    T = g.shape[0]
    K = w.shape[0]
    tm = min(TOKEN_TILE, T)

    def body(g_ref, w_ref, o_ref):
        o_ref[...] = _mm_nt(g_ref[...], w_ref[...])

    return _call(
        body, name=name, grid=(T // tm,), out_shape=jax.ShapeDtypeStruct((T, K), F32),
        in_specs=[_tile(tm, g.shape[1]), _whole(w.shape)], out_specs=_tile(tm, K),
        compiler_params=_params(32),
    )(g, w)


def _loss_grad(y, tgt, name):
    T, D = y.shape
    tm = min(TOKEN_TILE, T)

    def body(y_ref, t_ref, dy_ref, loss_ref):
        @pl.when(pl.program_id(0) == 0)
        def _():
            loss_ref[...] = jnp.zeros_like(loss_ref)
        diff = y_ref[...] - t_ref[...]
        dy_ref[...] = diff * (1.0 / D)
        rows = jnp.sum(diff * diff, axis=1, keepdims=True) * (1.0 / D)
        loss_ref[...] += 0.5 * jnp.sum(rows, axis=0, keepdims=True)

    return _call(
        body, name=name, grid=(T // tm,),
        out_shape=[jax.ShapeDtypeStruct((T, D), F32), jax.ShapeDtypeStruct((1, 1), F32)],
        in_specs=[_tile(tm, D), _tile(tm, D)], out_specs=[_tile(tm, D), _acc((1, 1))],
        compiler_params=_params(32),
    )(y, tgt)


def _ple_bwd(dx, x, gate, pp, ln, w_g, name):
    T, D = x.shape
    tm = min(TOKEN_TILE, T)

    def body(dx_ref, x_ref, gate_ref, pp_ref, ln_ref, wg_ref, dxo_ref, dpp_ref, dgp_ref, dln_ref):
        @pl.when(pl.program_id(0) == 0)
        def _():
            dln_ref[...] = jnp.zeros_like(dln_ref)
        dxv = dx_ref[...]
        gate = gate_ref[...]
        _, xhat, rstd = _rms(x_ref[...], ln_ref[...])
        dpp_ref[...] = (dxv * gate).astype(MXU)
        dgp = (dxv * pp_ref[...] * gate * (1.0 - gate)).astype(MXU)
        dgp_ref[...] = dgp
        dxn, dln = _rms_bwd(_mm_nt(dgp, wg_ref[...]), xhat, rstd, ln_ref[...])
        dln_ref[...] += dln
        dxo_ref[...] = dxn + dxv

    return _call(
        body, name=name, grid=(T // tm,),
        out_shape=[jax.ShapeDtypeStruct((T, D), F32), jax.ShapeDtypeStruct((T, D), MXU),
                   jax.ShapeDtypeStruct((T, D), MXU), jax.ShapeDtypeStruct(ln.shape, F32)],
        in_specs=[_tile(tm, D)] * 4 + [_whole(ln.shape), _whole(w_g.shape)],
        out_specs=[_tile(tm, D), _tile(tm, D), _tile(tm, D), _acc(ln.shape)],
        compiler_params=_params(32),
    )(dx, x, gate, pp, ln, w_g)


def _mlp_bwd(dx, x, pre, ln, w_up, w_down, name):
    T, D = x.shape
    tm = min(TOKEN_TILE, T)
    nf = w_up.shape[2]
    F = nf * NDEV

    def body(dx_ref, x_ref, pre_ref, ln_ref, wup_ref, wdown_ref, dxo_ref, dpre_ref, s_ref, dln_ref):
        @pl.when(pl.program_id(0) == 0)
        def _():
            dln_ref[...] = jnp.zeros_like(dln_ref)
        dxv = dx_ref[...]
        _, xhat, rstd = _rms(x_ref[...], ln_ref[...])
        a = jnp.maximum(pre_ref[...], 0.0)
        s_ref[...] = (a * a).astype(MXU)
        dpre_ref[...] = (_mm_nt(dxv, wdown_ref[...]) * (2.0 * a)).astype(MXU)
        dh = _mm_nt(dpre_ref[:, :nf], wup_ref[0])
        for j in range(1, NDEV):
            dh += _mm_nt(dpre_ref[:, j * nf:(j + 1) * nf], wup_ref[j])
        dxn, dln = _rms_bwd(dh, xhat, rstd, ln_ref[...])
        dln_ref[...] += dln
        dxo_ref[...] = dxn + dxv

    return _call(
        body, name=name, grid=(T // tm,),
        out_shape=[jax.ShapeDtypeStruct((T, D), F32), jax.ShapeDtypeStruct((T, F), MXU),
                   jax.ShapeDtypeStruct((T, F), MXU), jax.ShapeDtypeStruct(ln.shape, F32)],
        in_specs=[_tile(tm, D), _tile(tm, D), _tile(tm, F), _whole(ln.shape), _whole(w_up.shape),
                  _whole(w_down.shape)],
        out_specs=[_tile(tm, D), _tile(tm, F), _tile(tm, F), _acc(ln.shape)],
        compiler_params=_params(56),
    )(dx, x, pre, ln, w_up, w_down)


def _qkv_bwd(dx, x, dq, dk, dv, q_pre, k_pre, ln_q, ln_kv, g_q, g_k, w_q, w_kv, name):
    T, D = x.shape
    tm = min(TOKEN_TILE, T)
    nk = w_kv.shape[2]
    n_tiles = T // tm

    def body(dx_ref, x_ref, dq_ref, dk_ref, dv_ref, qpre_ref, kpre_ref, lnq_ref, lnkv_ref, gq_ref, gk_ref,
             wq_ref, wkv_ref, dxo_ref, dqp_ref, dkv_ref, dlnq_ref, dlnkv_ref, dgq_ref, dgk_ref, gq_acc, gk_acc):
        i = pl.program_id(0)

        @pl.when(i == 0)
        def _():
            dlnq_ref[...] = jnp.zeros_like(dlnq_ref)
            dlnkv_ref[...] = jnp.zeros_like(dlnkv_ref)
            gq_acc[...] = jnp.zeros_like(gq_acc)
            gk_acc[...] = jnp.zeros_like(gk_acc)

        ones = _head_ones()
        for b in range(D // 128):
            cols = slice(b * 128, (b + 1) * 128)
            _, xh, rs = _head_rms(qpre_ref[:, cols], gq_ref[:, cols], ones)
            d, dg = _head_rms_bwd(dq_ref[:, cols], xh, rs, gq_ref[:, cols], ones)
            dqp_ref[:, cols] = d.astype(MXU)
            gq_acc[:, cols] += dg
            _, xh, rs = _head_rms(kpre_ref[:, cols], gk_ref[:, cols], ones)
            d, dg = _head_rms_bwd(dk_ref[:, cols], xh, rs, gk_ref[:, cols], ones)
            dkv_ref[:, cols] = d.astype(MXU)
            gk_acc[:, cols] += dg
        dkv_ref[:, D:] = dv_ref[...].astype(MXU)

        _, xhat, rstd = _rms(x_ref[...], lnq_ref[...])
        dhq = _mm_nt(dqp_ref[...], wq_ref[...])
        dhkv = _mm_nt(dkv_ref[:, :nk], wkv_ref[0])
        for j in range(1, NDEV):
            dhkv += _mm_nt(dkv_ref[:, j * nk:(j + 1) * nk], wkv_ref[j])
        dxq, dlnq = _rms_bwd(dhq, xhat, rstd, lnq_ref[...])
        dxkv, dlnkv = _rms_bwd(dhkv, xhat, rstd, lnkv_ref[...])
        dlnq_ref[...] += dlnq
        dlnkv_ref[...] += dlnkv
        dxo_ref[...] = dx_ref[...] + dxq + dxkv

        @pl.when(i == n_tiles - 1)
        def _():
            row = lax.broadcasted_iota(jnp.int32, (D, 128), 0)
            col = lax.broadcasted_iota(jnp.int32, (D, 128), 1)
            fold = (jnp.bitwise_and(row, HEAD_DIM - 1) == col).astype(MXU)
            dgq_ref[...] = _split_dot(jnp.broadcast_to(gq_acc[...], (8, D)), fold, 3)
            dgk_ref[...] = _split_dot(jnp.broadcast_to(gk_acc[...], (8, D)), fold, 3)

    small = jax.ShapeDtypeStruct((8, 128), F32)
    return _call(
        body, name=name, grid=(n_tiles,),
        out_shape=[jax.ShapeDtypeStruct((T, D), F32), jax.ShapeDtypeStruct((T, D), MXU),
                   jax.ShapeDtypeStruct((T, 2 * D), MXU), jax.ShapeDtypeStruct(ln_q.shape, F32),
                   jax.ShapeDtypeStruct(ln_kv.shape, F32), small, small],
        in_specs=[_tile(tm, D)] * 7 + [_whole(ln_q.shape), _whole(ln_kv.shape), _whole(g_q.shape),
                                       _whole(g_k.shape), _whole(w_q.shape), _whole(w_kv.shape)],
        out_specs=[_tile(tm, D), _tile(tm, D), _tile(tm, 2 * D), _acc(ln_q.shape), _acc(ln_kv.shape),
                   _acc((8, 128)), _acc((8, 128))],
        scratch_shapes=[pltpu.VMEM((1, D), F32), pltpu.VMEM((1, D), F32)],
        compiler_params=_params(48),
    )(dx, x, dq, dk, dv, q_pre, k_pre, ln_q, ln_kv, g_q, g_k, w_q, w_kv)


def _sgu_bwd(dx, x, z, ln, w_in, g_v, ws, wsT, bT, w_out, name):
    T, D = x.shape
    tm = min(TOKEN_TILE, T)
    nw = w_in.shape[2]

    def body(dx_ref, x_ref, z_ref, ln_ref, win_ref, gv_ref, ws_ref, wsT_ref, bT_ref, wout_ref,
             dxo_ref, dz_ref, dws_ref, dbT_ref, dln_ref, dgv_ref, mix_ref, dvn_ref):
        @pl.when(pl.program_id(0) == 0)
        def _():
            dws_ref[...] = jnp.zeros_like(dws_ref)
            dbT_ref[...] = jnp.zeros_like(dbT_ref)
            dln_ref[...] = jnp.zeros_like(dln_ref)
            dgv_ref[...] = jnp.zeros_like(dgv_ref)
        dxv = dx_ref[...]
        _, xhat, rstd = _rms(x_ref[...], ln_ref[...])
        u, du = _gelu_and_grad(z_ref[:, :D])
        gv, dgv = _gelu_and_grad(z_ref[:, D:])
        vn, vhat, rstd_v = _rms(gv, gv_ref[...])
        vnb = vn.astype(MXU)
        _spatial_mix(vnb, ws_ref, bT_ref, mix_ref, tm)
        dy = _mm_nt(dxv, wout_ref[...])
        d_u = dy * mix_ref[...]
        d_mix = dy * u
        dmb = d_mix.astype(MXU)
        tri, triT = _tril_mask()
        for g in range(GROUPS):
            wmT = jnp.where(triT, wsT_ref[g], 0.0).astype(MXU)
            cols = slice(g * CHUNK, (g + 1) * CHUNK)
            for ch in range(tm // CHUNK):
                rows = slice(ch * CHUNK, (ch + 1) * CHUNK)
                dm = dmb[rows, cols]
                dws_ref[g] += jnp.where(tri, _mm_nt(dm, vnb[rows, cols]), 0.0)
                dbT_ref[:, g:g + 1] += jnp.sum(d_mix[rows, cols], axis=1, keepdims=True)
                dvn_ref[rows, cols] = _mm(wmT, dm)
        d_gv, dg = _rms_bwd(dvn_ref[...], vhat, rstd_v, gv_ref[...])
        dgv_ref[...] += dg
        dz_ref[:, :D] = (d_u * du).astype(MXU)
        dz_ref[:, D:] = (d_gv * dgv).astype(MXU)
        dh = _mm_nt(dz_ref[:, :nw], win_ref[0])
        for j in range(1, NDEV):
            dh += _mm_nt(dz_ref[:, j * nw:(j + 1) * nw], win_ref[j])
        dxn, dln = _rms_bwd(dh, xhat, rstd, ln_ref[...])
        dln_ref[...] += dln
        dxo_ref[...] = dxn + dxv

    return _call(
        body, name=name, grid=(T // tm,),
        out_shape=[jax.ShapeDtypeStruct((T, D), F32), jax.ShapeDtypeStruct((T, 2 * D), MXU),
                   jax.ShapeDtypeStruct(ws.shape, F32), jax.ShapeDtypeStruct(bT.shape, F32),
                   jax.ShapeDtypeStruct(ln.shape, F32), jax.ShapeDtypeStruct(g_v.shape, F32)],
        in_specs=[_tile(tm, D), _tile(tm, D), _tile(tm, 2 * D), _whole(ln.shape), _whole(w_in.shape),
                  _whole(g_v.shape), _whole(ws.shape), _whole(wsT.shape), _whole(bT.shape), _whole(w_out.shape)],
        out_specs=[_tile(tm, D), _tile(tm, 2 * D), _acc(ws.shape), _acc(bT.shape), _acc(ln.shape),
                   _acc(g_v.shape)],
        scratch_shapes=[pltpu.VMEM((tm, D), F32), pltpu.VMEM((tm, D), F32)],
        compiler_params=_params(48),
    )(dx, x, z, ln, w_in, g_v, ws, wsT, bT, w_out)


def _wgrad_rows(a, g, name):
    T, K = a.shape
    N = g.shape[1]
    kb = K // NDEV

    def body(a_ref, g_ref, o_ref):
        o_ref[...] = _mm_tn(a_ref[...], g_ref[...]).astype(COMM)

    return _call(
        body, name=name, grid=(NDEV,), out_shape=jax.ShapeDtypeStruct((K, N), COMM),
        in_specs=[pl.BlockSpec((T, kb), lambda j: (0, j)), _whole(g.shape)],
        out_specs=pl.BlockSpec((kb, N), lambda j: (j, 0)),
        compiler_params=_params(40),
    )(a, g).reshape(NDEV, kb, N)


def _wgrad_cols(a, g, name):
    T, K = a.shape
    N = g.shape[1]
    nb = N // NDEV

    def body(a_ref, g_ref, o_ref):
        o_ref[...] = _mm_tn(a_ref[...], g_ref[...]).astype(COMM)

    return _call(
        body, name=name, grid=(NDEV,), out_shape=jax.ShapeDtypeStruct((NDEV, K, nb), COMM),
        in_specs=[_whole(a.shape), pl.BlockSpec((T, nb), lambda j: (0, j))],
        out_specs=pl.BlockSpec((None, K, nb), lambda j: (j, 0, 0)),
        compiler_params=_params(40),
    )(a, g)


def _adamw(w, m, v, slots, name):
    R, C = w.shape
    n = slots.shape[0]
    tr = math.gcd(R, max(8, (128 * 1024) // C))
    if tr < 64:
        tr = R
    bc1 = 1.0 - ADAM_B1 ** ADAM_STEP
    bc2 = 1.0 - ADAM_B2 ** ADAM_STEP

    def body(w_ref, m_ref, v_ref, s_ref, g_ref, d_ref, mo_ref, vo_ref):
        g = s_ref[0].astype(F32)
        for j in range(1, n):
            g = g + s_ref[j].astype(F32)
        mn = ADAM_B1 * m_ref[...] + (1.0 - ADAM_B1) * g
        vn = ADAM_B2 * v_ref[...] + (1.0 - ADAM_B2) * (g * g)
        g_ref[...] = g
        mo_ref[...] = mn
        vo_ref[...] = vn
        d_ref[...] = -ADAM_LR * ((mn / bc1) / (jnp.sqrt(vn / bc2) + ADAM_EPS) + ADAM_WD * w_ref[...])

    blk = pl.BlockSpec((tr, C), lambda i: (i, 0))
    out = jax.ShapeDtypeStruct((R, C), F32)
    return _call(
        body, name=name, grid=(R // tr,), out_shape=[out, out, out, out],
        in_specs=[blk, blk, blk, pl.BlockSpec((n, tr, C), lambda i: (0, i, 0))], out_specs=[blk] * 4,
        compiler_params=_params(32),
    )(w, m, v, slots)


def _rows128(a):
    flat = a.reshape(-1)
    rows = -(-flat.shape[0] // 1024) * 8
    flat = jnp.pad(flat, (0, rows * 128 - flat.shape[0]))
    return flat.reshape(rows, 128)


def kernel(x, p, ln_mix_a, w_in_a, g_v_a, w_spatial, b_spatial, w_out_a, ln_kv, w_kv, g_k, ln_mix_b, w_q, g_q, w_out_b, ln_mlp, w_up, w_down, ln_ple, w_ple_gate, w_ple_proj, loss_target, m_ln_mix_a, m_w_in_a, m_g_v_a, m_w_spatial, m_b_spatial, m_w_out_a, m_ln_kv, m_w_kv, m_g_k, m_ln_mix_b, m_w_q, m_g_q, m_w_out_b, m_ln_mlp, m_w_up, m_w_down, m_ln_ple, m_w_ple_gate, m_w_ple_proj, v_ln_mix_a, v_w_in_a, v_g_v_a, v_w_spatial, v_b_spatial, v_w_out_a, v_ln_kv, v_w_kv, v_g_k, v_ln_mix_b, v_w_q, v_g_q, v_w_out_b, v_ln_mlp, v_w_up, v_w_down, v_ln_ple, v_w_ple_gate, v_w_ple_proj):
    me = 4 * lax.axis_index("x") + 2 * lax.axis_index("y") + lax.axis_index("c")
    D = x.shape[2]
    x0, tgt = x[0], loss_target[0]
    n_layers = w_up.shape[0]

    c = lambda w: w.astype(COMM)
    first = [c(w_in_a[0]), c(w_out_a[0]), ln_mix_a, g_v_a, c(w_up[0]), c(w_down[0]), c(w_ple_gate[0]),
             c(w_ple_proj[0]), c(w_q[0]), c(w_kv)]
    second = [c(w_out_b[0]), c(w_up[1]), c(w_down[1]), c(w_ple_gate[1]), c(w_ple_proj[1])]
    W_in, W_out_a, ln_a, gv_a, W_up0, W_down0, W_g0, W_pp0, W_q, W_kv = _gather_two_level(first, "gather_first")
    W_out_a, ln_a, gv_a = W_out_a.reshape(-1, D), ln_a.reshape(1, D), gv_a.reshape(1, D)
    W_down0, W_g0, W_q = W_down0.reshape(-1, D), W_g0.reshape(-1, D), W_q.reshape(-1, D)
    ws = w_spatial[0]
    wsT = jnp.swapaxes(ws, 1, 2)
    bT = b_spatial[0].T
    ln_kv2, ln_b = ln_kv.reshape(1, D), ln_mix_b
    gk2 = jnp.tile(g_k.reshape(1, HEAD_DIM), (1, D // HEAD_DIM))
    gq2 = jnp.tile(g_q, (1, D // HEAD_DIM))
    ln_m = [ln_mlp[l:l + 1] for l in range(n_layers)]
    ln_p = [ln_ple[l:l + 1] for l in range(n_layers)]

    x1, z, h_a, y_a = _sgu_fwd(x0, ln_a, W_in, gv_a, ws, bT, W_out_a, "sgu_fwd")
    x2, pre0, hm0 = _mlp_fwd(x1, ln_m[0], W_up0, W_down0, "mlp_fwd0")
    x3, gate0, pp0, hp0 = _ple_fwd(x2, p[0, 0], ln_p[0], W_g0, W_pp0, "ple_fwd0")
    qn, kn, vn, q_pre, k_pre, h_q, h_kv = _qkv_fwd(x3, ln_b, ln_kv2, gq2, gk2, W_q, W_kv, "qkv_fwd")
    o2d, (W_out_b, W_up1, W_down1, W_g1, W_pp1) = _sb_fwd(qn, kn, vn, second, "sb_fwd")
    W_out_b, W_down1, W_g1 = W_out_b.reshape(-1, D), W_down1.reshape(-1, D), W_g1.reshape(-1, D)
    x4 = _proj_res(x3, o2d, W_out_b, "attn_out")
    x5, pre1, hm1 = _mlp_fwd(x4, ln_m[1], W_up1, W_down1, "mlp_fwd1")
    x6, gate1, pp1, hp1 = _ple_fwd(x5, p[1, 0], ln_p[1], W_g1, W_pp1, "ple_fwd1")
    dy, loss_part = _loss_grad(x6, tgt, "loss_grad")
    loss = lax.psum(loss_part[0, 0], ("x", "y", "c"))

    dx5, dpp1, dgp1, dlnp1 = _ple_bwd(dy, x5, gate1, pp1, ln_p[1], W_g1, "ple_bwd1")
    dx4, dpre1, s1, dlnm1 = _mlp_bwd(dx5, x4, pre1, ln_m[1], W_up1, W_down1, "mlp_bwd1")
    wg_second = [_wgrad_cols(hm1, dpre1, "wg_up1"), _wgrad_rows(s1, dx5, "wg_down1"),
                 _wgrad_rows(hp1, dgp1, "wg_gate1"), _wgrad_cols(p[1, 0].astype(MXU), dpp1, "wg_proj1"),
                 _wgrad_rows(o2d, dx4, "wg_out_b")]
    do2d = _proj_nt(dx4, W_out_b, "attn_out_bwd")
    dqn, dkn, dvn, (s_up1, s_down1, s_gate1, s_proj1, s_out_b) = _sb_bwd(qn, kn, vn, o2d, do2d, wg_second, "sb_bwd")
    dx3, dq_pre, dkv, dlnb, dlnkv, dgq, dgk = _qkv_bwd(dx4, x3, dqn, dkn, dvn, q_pre, k_pre, ln_b, ln_kv2, gq2, gk2,
                                                      W_q, W_kv, "qkv_bwd")
    dgq, dgk = dgq[:1, :HEAD_DIM], dgk[:1, :HEAD_DIM]
    dx2, dpp0, dgp0, dlnp0 = _ple_bwd(dx3, x2, gate0, pp0, ln_p[0], W_g0, "ple_bwd0")
    dx1, dpre0, s0, dlnm0 = _mlp_bwd(dx2, x1, pre0, ln_m[0], W_up0, W_down0, "mlp_bwd0")
    dx0, dz, dws, dbT, dlna, dgva = _sgu_bwd(dx1, x0, z, ln_a, W_in, gv_a, ws, wsT, bT, W_out_a, "sgu_bwd")
    wg_first = [_wgrad_rows(h_q, dq_pre, "wg_q"), _wgrad_cols(h_kv, dkv, "wg_kv"),
                _wgrad_cols(hm0, dpre0, "wg_up0"), _wgrad_rows(s0, dx2, "wg_down0"), _wgrad_rows(hp0, dgp0, "wg_gate0"),
                _wgrad_cols(p[0, 0].astype(MXU), dpp0, "wg_proj0"), _wgrad_cols(h_a, dz, "wg_in_a"),
                _wgrad_rows(y_a, dx1, "wg_out_a")]

    small = [("w_spatial", dws[None], w_spatial, m_w_spatial, v_w_spatial),
             ("b_spatial", dbT.T[None], b_spatial, m_b_spatial, v_b_spatial),
             ("ln_kv", dlnkv.reshape(-1), ln_kv, m_ln_kv, v_ln_kv),
             ("g_k", dgk.reshape(-1), g_k, m_g_k, v_g_k),
             ("ln_mix_b", dlnb, ln_mix_b, m_ln_mix_b, v_ln_mix_b),
             ("g_q", dgq, g_q, m_g_q, v_g_q),
             ("ln_mlp", jnp.concatenate([dlnm0, dlnm1]), ln_mlp, m_ln_mlp, v_ln_mlp),
             ("ln_ple", jnp.concatenate([dlnp0, dlnp1]), ln_ple, m_ln_ple, v_ln_ple)]
    sharded_vec = [("ln_mix_a", dlna, ln_mix_a, m_ln_mix_a, v_ln_mix_a),
                   ("g_v_a", dgva, g_v_a, m_g_v_a, v_g_v_a)]
    packs = [[], [], [], []]
    for _, g, w, m, v in small:
        for lst, a in zip(packs, (g, w, m, v)):
            lst.append(_rows128(a))
    for _, g, w, m, v in sharded_vec:
        packs[0].append(g.reshape(NDEV, -1))
        for lst, a in zip(packs[1:], (w, m, v)):
            lst.append(jnp.broadcast_to(a, (NDEV, a.shape[1])))
    g_pack, w_pack, m_pack, v_pack = (jnp.concatenate(lst) for lst in packs)
    g_pack8 = jnp.broadcast_to(g_pack[None], (NDEV,) + g_pack.shape)
    by_chip = [a.reshape((4, 2) + a.shape[1:]) for a in wg_first]
    from_sibling, (g_all,) = _scatter_pair(by_chip, [g_pack8], "scatter_pair")
    my_core = lax.axis_index("c")
    chip_sums = [_pair_sum(lax.dynamic_index_in_dim(a, my_core, 1, keepdims=False), o, f"pair_sum{j}")
                 for j, (a, o) in enumerate(zip(by_chip, from_sibling))]
    s_q, s_kv, s_up0, s_down0, s_gate0, s_proj0, s_in_a, s_out_a = _scatter_chips(chip_sums, "scatter_chips")

    def upd(w, m, v, s, name):
        shape = w.shape
        outs = _adamw(w.reshape(-1, shape[-1]), m.reshape(-1, shape[-1]), v.reshape(-1, shape[-1]), s, name)
        return [o.reshape(shape) for o in outs]

    res = {}
    per = {}
    per["w_up", 1] = upd(w_up[1], m_w_up[1], v_w_up[1], s_up1, "adam_up1")
    per["w_down", 1] = upd(w_down[1], m_w_down[1], v_w_down[1], s_down1, "adam_down1")
    per["w_ple_gate", 1] = upd(w_ple_gate[1], m_w_ple_gate[1], v_w_ple_gate[1], s_gate1, "adam_gate1")
    per["w_ple_proj", 1] = upd(w_ple_proj[1], m_w_ple_proj[1], v_w_ple_proj[1], s_proj1, "adam_proj1")
    res["w_out_b"] = upd(w_out_b, m_w_out_b, v_w_out_b, s_out_b, "adam_out_b")
    res["w_q"] = upd(w_q, m_w_q, v_w_q, s_q, "adam_q")
    res["w_kv"] = upd(w_kv, m_w_kv, v_w_kv, s_kv, "adam_kv")
    per["w_up", 0] = upd(w_up[0], m_w_up[0], v_w_up[0], s_up0, "adam_up0")
    per["w_down", 0] = upd(w_down[0], m_w_down[0], v_w_down[0], s_down0, "adam_down0")
    per["w_ple_gate", 0] = upd(w_ple_gate[0], m_w_ple_gate[0], v_w_ple_gate[0], s_gate0, "adam_gate0")
    per["w_ple_proj", 0] = upd(w_ple_proj[0], m_w_ple_proj[0], v_w_ple_proj[0], s_proj0, "adam_proj0")
    res["w_in_a"] = upd(w_in_a, m_w_in_a, v_w_in_a, s_in_a, "adam_in_a")
    res["w_out_a"] = upd(w_out_a, m_w_out_a, v_w_out_a, s_out_a, "adam_out_a")
    for nm in ("w_up", "w_down", "w_ple_gate", "w_ple_proj"):
        res[nm] = [jnp.stack([per[nm, l][t] for l in range(n_layers)]) for t in range(4)]

    outs = _adamw(w_pack, m_pack, v_pack, g_all, "adam_small")
    row = 0
    for nm, g, w, m, v in small:
        nrows = _rows128(w).shape[0]
        res[nm] = [o[row:row + nrows].reshape(-1)[:w.size].reshape(w.shape) for o in outs]
        row += nrows
    for nm, g, w, m, v in sharded_vec:
        res[nm] = [lax.dynamic_slice_in_dim(o[row:row + NDEV], me, 1, axis=0) for o in outs]
        row += NDEV

    names = ["ln_mix_a", "w_in_a", "g_v_a", "w_spatial", "b_spatial", "w_out_a", "ln_kv", "w_kv", "g_k", "ln_mix_b",
             "w_q", "g_q", "w_out_b", "ln_mlp", "w_up", "w_down", "ln_ple", "w_ple_gate", "w_ple_proj"]
    out = [loss, dx0[None]]
    for t in range(4):
        out += [res[nm][t] for nm in names]
    return tuple(out)
```

```python
import functools
import math

import jax
import jax.numpy as jnp
from jax import lax
from jax.experimental import pallas as pl
from jax.experimental.pallas import tpu as pltpu

F32 = jnp.float32
MXU = jnp.bfloat16
COMM = jnp.bfloat16
EPS = 1e-6
NDEV = 8
HEAD_DIM = 64
CHUNK = 128
GROUPS = 8
QBLK = 128
SCALE = HEAD_DIM ** -0.5
TOKEN_TILE = 256
ADAM_LR = 0.001
ADAM_B1 = 0.9
ADAM_B2 = 0.999
ADAM_EPS = 1e-08
ADAM_WD = 0.01
ADAM_STEP = 10
MESH = pl.DeviceIdType.MESH


def _call(body, **kw):
    return pl.pallas_call(body, **kw)


def _params(vmem_mb, n_axes=1):
    return pltpu.CompilerParams(dimension_semantics=("arbitrary",) * n_axes,
                                vmem_limit_bytes=vmem_mb << 20)


def _tile(tm, n):
    return pl.BlockSpec((tm, n), lambda i: (i, 0))


def _whole(shape):
    zeros = (0,) * len(shape)
    return pl.BlockSpec(shape, lambda i: zeros, pipeline_mode=pl.Buffered(1))


def _acc(shape):
    zeros = (0,) * len(shape)
    return pl.BlockSpec(shape, lambda i: zeros)


def _mm(a, b):
    return jnp.dot(a.astype(MXU), b.astype(MXU), preferred_element_type=F32)


def _mm_nt(a, b):
    return lax.dot_general(a.astype(MXU), b.astype(MXU), (((1,), (1,)), ((), ())),
                           preferred_element_type=F32)


def _mm_tn(a, b):
    return lax.dot_general(a.astype(MXU), b.astype(MXU), (((0,), (0,)), ((), ())),
                           preferred_element_type=F32)


def _split_dot(x, ones, terms=2):
    out = None
    for _ in range(terms):
        part = x.astype(MXU)
        x = x - part.astype(F32)
        d = jnp.dot(part, ones, preferred_element_type=F32)
        out = d if out is None else out + d
    return out


def _rms(x, g):
    rstd = lax.rsqrt(jnp.mean(x * x, axis=-1, keepdims=True) + EPS)
    xhat = x * rstd
    return xhat * g, xhat, rstd


def _rms_bwd(dh, xhat, rstd, g):
    dxh = dh * g
    dx = rstd * (dxh - xhat * jnp.mean(dxh * xhat, axis=-1, keepdims=True))
    dg = jnp.sum(dh * xhat, axis=0, keepdims=True)
    return dx, dg


_GELU_C = math.sqrt(2.0 / math.pi)


def _gelu(x):
    t = jnp.tanh(_GELU_C * (x + 0.044715 * (x * x * x)))
    return 0.5 * x * (1.0 + t)


def _gelu_and_grad(x):
    x2 = x * x
    t = jnp.tanh(_GELU_C * (x + 0.044715 * (x2 * x)))
    g = 0.5 * x * (1.0 + t)
    dg = 0.5 * (1.0 + t) + 0.5 * x * (1.0 - t * t) * (_GELU_C * (1.0 + 3.0 * 0.044715 * x2))
    return g, dg


def _softplus(z):
    return jnp.maximum(z, 0.0) + jnp.log(1.0 + jnp.exp(-jnp.abs(z)))


def _tril_mask():
    row = lax.broadcasted_iota(jnp.int32, (CHUNK, CHUNK), 0)
    col = lax.broadcasted_iota(jnp.int32, (CHUNK, CHUNK), 1)
    return row >= col, row <= col


ANY_SPEC = pl.BlockSpec(memory_space=pl.ANY)


def _my_index():
    return 4 * lax.axis_index("x") + 2 * lax.axis_index("y") + lax.axis_index("c")


def _exchange_copies(srcs, lands, send_sems, recv_sems, scatter, arriving):
    x, y, c = lax.axis_index("x"), lax.axis_index("y"), lax.axis_index("c")
    me = 4 * x + 2 * y + c
    out = []
    for a in range(len(srcs)):
        for k in range(NDEV - 1):
            bits = k + 1
            px = 1 - x if (bits >> 2) & 1 else x
            py = 1 - y if (bits >> 1) & 1 else y
            pc = 1 - c if bits & 1 else c
            peer = 4 * px + 2 * py + pc
            src = srcs[a].at[peer] if scatter else srcs[a]
            out.append(pltpu.make_async_remote_copy(
                src_ref=src, dst_ref=lands[a].at[peer if arriving else me],
                send_sem=send_sems.at[a * (NDEV - 1) + k], recv_sem=recv_sems.at[a * (NDEV - 1) + k],
                device_id=(px, py, pc), device_id_type=MESH))
    return out


def _exchange_shapes(arrs, scatter):
    n = len(arrs)
    lands = [jax.ShapeDtypeStruct(a.shape if scatter else (NDEV,) + a.shape, a.dtype) for a in arrs]
    sems = [pltpu.SemaphoreType.DMA((n * (NDEV - 1),)), pltpu.SemaphoreType.DMA((n * (NDEV - 1),)),
            pltpu.SemaphoreType.DMA((n,))]
    return lands, sems


def _exchange_start(srcs, lands, sems, scatter):
    send_sems, recv_sems, local_sems = sems
    me = _my_index()
    for a in range(len(srcs)):
        pltpu.make_async_copy(srcs[a].at[me] if scatter else srcs[a], lands[a].at[me], local_sems.at[a]).start()
    for send in _exchange_copies(srcs, lands, send_sems, recv_sems, scatter, False):
        send.start()


def _exchange_finish(srcs, lands, sems, scatter):
    send_sems, recv_sems, local_sems = sems
    me = _my_index()
    for arrive in _exchange_copies(srcs, lands, send_sems, recv_sems, scatter, True):
        arrive.wait_recv()
    for send in _exchange_copies(srcs, lands, send_sems, recv_sems, scatter, False):
        send.wait_send()
    for a in range(len(srcs)):
        pltpu.make_async_copy(srcs[a].at[me] if scatter else srcs[a], lands[a].at[me], local_sems.at[a]).wait()


def _gather_two_level(arrs, name):
    n = len(arrs)
    lands = [jax.ShapeDtypeStruct((NDEV,) + a.shape, a.dtype) for a in arrs]

    def body(*refs):
        srcs, outs = refs[:n], refs[n:2 * n]
        send_sems, recv_sems, local_sems = refs[2 * n:]
        x, y, c = lax.axis_index("x"), lax.axis_index("y"), lax.axis_index("c")
        me, sibling = (x, y, c), (x, y, 1 - c)
        chips = [(1 - x, y), (x, 1 - y), (1 - x, 1 - y)]

        def index(dev):
            return 4 * dev[0] + 2 * dev[1] + dev[2]

        def copy(a, k, block, to, src=None):
            dst = outs[a].at[index(block)]
            return pltpu.make_async_remote_copy(
                src_ref=dst if src is None else src, dst_ref=dst, send_sem=send_sems.at[a, k],
                recv_sem=recv_sems.at[a, k], device_id=to, device_id_type=MESH)

        mine, first, passed = [], [], []
        for a in range(n):
            cp = pltpu.make_async_copy(srcs[a], outs[a].at[index(me)], local_sems.at[a])
            cp.start()
            mine.append(cp)
            first.append(copy(a, 0, me, sibling, src=srcs[a]))
            first += [copy(a, 1 + j, me, (*chip, c), src=srcs[a]) for j, chip in enumerate(chips)]
        for cp in first:
            cp.start()
        for a in range(n):
            for j, chip in enumerate(chips):
                copy(a, 1 + j, (*chip, c), me).wait_recv()
                cp = copy(a, 4 + j, (*chip, c), sibling)
                cp.start()
                passed.append(cp)
        for a in range(n):
            copy(a, 0, sibling, me).wait_recv()
            for j, chip in enumerate(chips):
                copy(a, 4 + j, (*chip, 1 - c), me).wait_recv()
        for cp in first + passed:
            cp.wait_send()
        for cp in mine:
            cp.wait()

    return _call(body, name=name, out_shape=lands, in_specs=[ANY_SPEC] * n, out_specs=[ANY_SPEC] * n,
                 scratch_shapes=[pltpu.SemaphoreType.DMA((n, NDEV - 1)), pltpu.SemaphoreType.DMA((n, NDEV - 1)),
                                 pltpu.SemaphoreType.DMA((n,))])(*arrs)


def _scatter_pair(arrs, extra, name):
    n, ne = len(arrs), len(extra)
    lands = [jax.ShapeDtypeStruct((4,) + a.shape[2:], a.dtype) for a in arrs]
    extra_lands, extra_sems = _exchange_shapes(extra, True)

    def body(*refs):
        srcs, xsrc = refs[:n], refs[n:n + ne]
        outs, xout = refs[n + ne:2 * n + ne], refs[2 * n + ne:2 * (n + ne)]
        send_sems, recv_sems = refs[2 * (n + ne)], refs[2 * (n + ne) + 1]
        xsems = refs[2 * (n + ne) + 2:]
        x, y, c = lax.axis_index("x"), lax.axis_index("y"), lax.axis_index("c")
        _exchange_start(xsrc, xout, xsems, True)
        copies = [pltpu.make_async_remote_copy(
            src_ref=srcs[a].at[k, 1 - c], dst_ref=outs[a].at[k], send_sem=send_sems.at[a, k],
            recv_sem=recv_sems.at[a, k], device_id=(x, y, 1 - c), device_id_type=MESH)
            for a in range(n) for k in range(4)]
        for cp in copies:
            cp.start()
        for cp in copies:
            cp.wait()
        _exchange_finish(xsrc, xout, xsems, True)

    outs = _call(
        body, name=name, out_shape=lands + extra_lands, in_specs=[ANY_SPEC] * (n + ne), out_specs=[ANY_SPEC] * (n + ne),
        scratch_shapes=[pltpu.SemaphoreType.DMA((n, 4)), pltpu.SemaphoreType.DMA((n, 4))] + extra_sems,
    )(*arrs, *extra)
    return outs[:n], outs[n:]


def _pair_sum(own, other, name):
    _, R, C = own.shape
    tr = math.gcd(R, max(8, (128 * 1024) // C))

    def body(a_ref, b_ref, o_ref):
        o_ref[...] = (a_ref[...].astype(F32) + b_ref[...].astype(F32)).astype(COMM)

    blk = pl.BlockSpec((4, tr, C), lambda i: (0, i, 0))
    return _call(body, name=name, grid=(R // tr,), out_shape=jax.ShapeDtypeStruct(own.shape, COMM),
                 in_specs=[blk, blk], out_specs=blk, compiler_params=_params(32))(own, other)


def _scatter_chips(arrs, name):
    n = len(arrs)
    lands = [jax.ShapeDtypeStruct(a.shape, a.dtype) for a in arrs]

    def body(*refs):
        srcs, outs = refs[:n], refs[n:2 * n]
        send_sems, recv_sems, local_sems = refs[2 * n:]
        x, y, c = lax.axis_index("x"), lax.axis_index("y"), lax.axis_index("c")
        chip = 2 * x + y
        others = [(1 - x, y), (x, 1 - y), (1 - x, 1 - y)]
        local = [pltpu.make_async_copy(srcs[a].at[chip], outs[a].at[chip], local_sems.at[a]) for a in range(n)]
        for cp in local:
            cp.start()

        def copies(arriving):
            return [pltpu.make_async_remote_copy(
                src_ref=srcs[a].at[2 * px + py], dst_ref=outs[a].at[2 * px + py if arriving else chip],
                send_sem=send_sems.at[a, j], recv_sem=recv_sems.at[a, j], device_id=(px, py, c), device_id_type=MESH)
                for a in range(n) for j, (px, py) in enumerate(others)]

        for cp in copies(False):
            cp.start()
        for cp in copies(True):
            cp.wait_recv()
        for cp in copies(False):
            cp.wait_send()
        for cp in local:
            cp.wait()

    return _call(body, name=name, out_shape=lands, in_specs=[ANY_SPEC] * n, out_specs=[ANY_SPEC] * n,
                 scratch_shapes=[pltpu.SemaphoreType.DMA((n, 3)), pltpu.SemaphoreType.DMA((n, 3)),
                                 pltpu.SemaphoreType.DMA((n,))])(*arrs)


def _exchange(arrs, scatter, name):
    n = len(arrs)
    lands, sems = _exchange_shapes(arrs, scatter)

    def body(*refs):
        _exchange_start(refs[:n], refs[n:2 * n], refs[2 * n:], scatter)
        _exchange_finish(refs[:n], refs[n:2 * n], refs[2 * n:], scatter)

    return _call(body, name=name, out_shape=lands, in_specs=[ANY_SPEC] * n, out_specs=[ANY_SPEC] * n,
                 scratch_shapes=sems)(*arrs)


def _spatial_mix(vnb, ws_ref, bT_ref, mix_ref, tm):
    tri, _ = _tril_mask()
    for g in range(GROUPS):
        wm = jnp.where(tri, ws_ref[g], 0.0).astype(MXU)
        cols = slice(g * CHUNK, (g + 1) * CHUNK)
        for ch in range(tm // CHUNK):
            rows = slice(ch * CHUNK, (ch + 1) * CHUNK)
            mix_ref[rows, cols] = _mm(wm, vnb[rows, cols]) + bT_ref[:, g:g + 1]


def _sgu_fwd(x, ln, w_in, g_v, ws, bT, w_out, name):
    T, D = x.shape
    tm = min(TOKEN_TILE, T)
    nw = w_in.shape[2]

    def body(x_ref, ln_ref, win_ref, gv_ref, ws_ref, bT_ref, wout_ref, xo_ref, z_ref, h_ref, y_ref, mix_ref):
        xv = x_ref[...]
        h, _, _ = _rms(xv, ln_ref[...])
        hb = h.astype(MXU)
        h_ref[...] = hb
        for j in range(NDEV):
            z_ref[:, j * nw:(j + 1) * nw] = _mm(hb, win_ref[j])
        u = _gelu(z_ref[:, :D])
        gv = _gelu(z_ref[:, D:])
        vn, _, _ = _rms(gv, gv_ref[...])
        _spatial_mix(vn.astype(MXU), ws_ref, bT_ref, mix_ref, tm)
        y = (u * mix_ref[...]).astype(MXU)
        y_ref[...] = y
        xo_ref[...] = xv + _mm(y, wout_ref[...])

    return _call(
        body, name=name, grid=(T // tm,),
        out_shape=[jax.ShapeDtypeStruct((T, D), F32), jax.ShapeDtypeStruct((T, 2 * D), F32),
                   jax.ShapeDtypeStruct((T, D), MXU), jax.ShapeDtypeStruct((T, D), MXU)],
        in_specs=[_tile(tm, D), _whole(ln.shape), _whole(w_in.shape), _whole(g_v.shape), _whole(ws.shape),
                  _whole(bT.shape), _whole(w_out.shape)],
        out_specs=[_tile(tm, D), _tile(tm, 2 * D), _tile(tm, D), _tile(tm, D)],
        scratch_shapes=[pltpu.VMEM((tm, D), F32)],
        compiler_params=_params(40),
    )(x, ln, w_in, g_v, ws, bT, w_out)


def _mlp_fwd(x, ln, w_up, w_down, name):
    T, D = x.shape
    tm = min(TOKEN_TILE, T)
    nf = w_up.shape[2]
    F = nf * NDEV

    def body(x_ref, ln_ref, wup_ref, wdown_ref, xo_ref, pre_ref, h_ref):
        xv = x_ref[...]
        h, _, _ = _rms(xv, ln_ref[...])
        hb = h.astype(MXU)
        h_ref[...] = hb
        for j in range(NDEV):
            pre_ref[:, j * nf:(j + 1) * nf] = _mm(hb, wup_ref[j])
        a = jnp.maximum(pre_ref[...], 0.0)
        xo_ref[...] = xv + _mm(a * a, wdown_ref[...])

    return _call(
        body, name=name, grid=(T // tm,),
        out_shape=[jax.ShapeDtypeStruct((T, D), F32), jax.ShapeDtypeStruct((T, F), F32),
                   jax.ShapeDtypeStruct((T, D), MXU)],
        in_specs=[_tile(tm, D), _whole(ln.shape), _whole(w_up.shape), _whole(w_down.shape)],
        out_specs=[_tile(tm, D), _tile(tm, F), _tile(tm, D)],
        compiler_params=_params(52),
    )(x, ln, w_up, w_down)


def _ple_fwd(x, p, ln, w_g, w_pp, name):
    T, D = x.shape
    tm = min(TOKEN_TILE, T)
    npp = w_pp.shape[2]

    def body(x_ref, p_ref, ln_ref, wg_ref, wpp_ref, xo_ref, gate_ref, pp_ref, h_ref):
        xv = x_ref[...]
        h, _, _ = _rms(xv, ln_ref[...])
        hb = h.astype(MXU)
        h_ref[...] = hb
        gate = jax.nn.sigmoid(_mm(hb, wg_ref[...]))
        gate_ref[...] = gate
        pb = p_ref[...].astype(MXU)
        for j in range(NDEV):
            pp_ref[:, j * npp:(j + 1) * npp] = _mm(pb, wpp_ref[j])
        xo_ref[...] = xv + pp_ref[...] * gate

    return _call(
        body, name=name, grid=(T // tm,),
        out_shape=[jax.ShapeDtypeStruct((T, D), F32), jax.ShapeDtypeStruct((T, D), F32),
                   jax.ShapeDtypeStruct((T, D), F32), jax.ShapeDtypeStruct((T, D), MXU)],
        in_specs=[_tile(tm, D), _tile(tm, p.shape[1]), _whole(ln.shape), _whole(w_g.shape), _whole(w_pp.shape)],
        out_specs=[_tile(tm, D), _tile(tm, D), _tile(tm, D), _tile(tm, D)],
        compiler_params=_params(32),
    )(x, p, ln, w_g, w_pp)


def _head_ones():
    row = lax.broadcasted_iota(jnp.int32, (128, 128), 0)
    col = lax.broadcasted_iota(jnp.int32, (128, 128), 1)
    return (jnp.right_shift(row, 6) == jnp.right_shift(col, 6)).astype(MXU)


def _head_rms(x, g, ones):
    rstd = lax.rsqrt(_split_dot(x * x, ones, 3) * (1.0 / HEAD_DIM) + EPS)
    xhat = x * rstd
    return xhat * g, xhat, rstd


def _head_rms_bwd(dh, xhat, rstd, g, ones):
    dxh = dh * g
    mean = _split_dot(dxh * xhat, ones, 3) * (1.0 / HEAD_DIM)
    return rstd * (dxh - xhat * mean), jnp.sum(dh * xhat, axis=0, keepdims=True)


def _qkv_fwd(x, ln_q, ln_kv, g_q, g_k, w_q, w_kv, name):
    T, D = x.shape
    tm = min(TOKEN_TILE, T)
    nk = w_kv.shape[2]
    half = NDEV // 2

    def body(x_ref, lnq_ref, lnkv_ref, gq_ref, gk_ref, wq_ref, wkv_ref,
             q_ref, k_ref, v_ref, qpre_ref, kpre_ref, hq_ref, hkv_ref):
        xv = x_ref[...]
        _, xhat, _ = _rms(xv, lnq_ref[...])
        hq = (xhat * lnq_ref[...]).astype(MXU)
        hkv = (xhat * lnkv_ref[...]).astype(MXU)
        hq_ref[...] = hq
        hkv_ref[...] = hkv
        qpre_ref[...] = _mm(hq, wq_ref[...])
        for j in range(half):
            kpre_ref[:, j * nk:(j + 1) * nk] = _mm(hkv, wkv_ref[j])
            v_ref[:, j * nk:(j + 1) * nk] = _mm(hkv, wkv_ref[half + j]).astype(MXU)
        ones = _head_ones()
        for b in range(D // 128):
            cols = slice(b * 128, (b + 1) * 128)
            qn, _, _ = _head_rms(qpre_ref[:, cols], gq_ref[:, cols], ones)
            q_ref[:, cols] = (qn * SCALE).astype(MXU)
            kn, _, _ = _head_rms(kpre_ref[:, cols], gk_ref[:, cols], ones)
            k_ref[:, cols] = kn.astype(MXU)

    return _call(
        body, name=name, grid=(T // tm,),
        out_shape=[jax.ShapeDtypeStruct((T, D), MXU)] * 3 + [jax.ShapeDtypeStruct((T, D), F32)] * 2
        + [jax.ShapeDtypeStruct((T, D), MXU)] * 2,
        in_specs=[_tile(tm, D), _whole(ln_q.shape), _whole(ln_kv.shape), _whole(g_q.shape), _whole(g_k.shape),
                  _whole(w_q.shape), _whole(w_kv.shape)],
        out_specs=[_tile(tm, D)] * 7,
        compiler_params=_params(40),
    )(x, ln_q, ln_kv, g_q, g_k, w_q, w_kv)


SB_KEYS = 2 * QBLK


def _sb_consts():
    row = lax.broadcasted_iota(jnp.int32, (QBLK, QBLK), 0)
    col = lax.broadcasted_iota(jnp.int32, (QBLK, QBLK), 1)
    lane = lax.broadcasted_iota(jnp.int32, (QBLK, 128), 1)
    ones = jnp.ones((QBLK, QBLK), MXU)
    later = jnp.concatenate([(row > col).astype(MXU), ones], axis=1)
    later_eq = jnp.concatenate([(row >= col).astype(MXU), ones], axis=1)
    return later, later_eq, lane < HEAD_DIM


MASKED_LOG = -1e30


def _sb_window(i, w):
    upper = (i + 1) * QBLK - w * SB_KEYS
    start = pl.multiple_of(jnp.maximum(upper - SB_KEYS, 0), QBLK)
    key = lax.broadcasted_iota(jnp.int32, (2 * QBLK, SB_KEYS), 1) + start
    return start, key < upper


def _sb_diagonal():
    row = jnp.bitwise_and(lax.broadcasted_iota(jnp.int32, (2 * QBLK, SB_KEYS), 0), QBLK - 1)
    key = lax.broadcasted_iota(jnp.int32, (2 * QBLK, SB_KEYS), 1)
    cases = []
    for shift in (0, QBLK):
        seen = key < row + shift
        cases.append(jnp.stack([jnp.where(seen, 1.0, 0.0), jnp.where(seen, 0.0, MASKED_LOG)]))
    return jnp.stack(cases).astype(F32)


_SB_DIAG_SPEC = pl.BlockSpec((None, 2, 2 * QBLK, SB_KEYS), lambda h, i: (jnp.minimum(i, 1), 0, 0, 0))


def _sb_terms(x, terms):
    x = jnp.concatenate([x[:, :QBLK], x[:, QBLK:]], axis=0)
    out = []
    for _ in range(terms):
        part = x.astype(MXU)
        x = x - part.astype(F32)
        out.append(part)
    return tuple(out)


def _sb_suffix(parts, ones, carry):
    s = jnp.dot(jnp.concatenate(parts[:2], axis=1), jnp.concatenate([ones, ones], axis=0),
                preferred_element_type=F32)
    for part in parts[2:]:
        s = s + jnp.dot(part, ones, preferred_element_type=F32)
    rows = s.shape[0] // 2
    s_lo, sum_lo, s_hi, sum_hi = s[:rows, :QBLK], s[:rows, QBLK:], s[rows:, :QBLK], s[rows:, QBLK:]
    return jnp.concatenate([s_lo + (carry + sum_hi), s_hi + carry], axis=1), carry + (sum_lo + sum_hi)


def _sb_scores(z, mask):
    sp = _softplus(z)
    l, log_sig = -sp, z - sp
    if isinstance(mask, tuple):
        keep, bias = mask
        l, log_sig = l * keep, log_sig + bias
    else:
        l = jnp.where(mask, l, 0.0)
        log_sig = jnp.where(mask, log_sig, MASKED_LOG)
    return log_sig, _sb_terms(l, 2)


def _sb_weights(staged, later, c_l):
    log_sig, parts = staged
    b, c_l = _sb_suffix(parts, later, c_l)
    return jnp.exp(log_sig + b), c_l


DEAD_LOG = -88.0


def _sb_alive(carry):
    return (jnp.max(carry[0]) > DEAD_LOG).astype(jnp.int32)


def _ride_along(refs, n, scatter):
    step = pl.program_id(0) * pl.num_programs(1) + pl.program_id(1)
    srcs, lands, sems = refs[:n], refs[n:2 * n], refs[2 * n:]

    @pl.when(step == 0)
    def _():
        _exchange_start(srcs, lands, sems, scatter)

    def finish():
        @pl.when(step == pl.num_programs(0) * pl.num_programs(1) - 1)
        def _():
            _exchange_finish(srcs, lands, sems, scatter)

    return finish


def _sb_fwd(q, k, v, cargo, name):
    T, D = q.shape
    nc = len(cargo)
    lands, sems = _exchange_shapes(cargo, False)

    def body(diag_ref, q_ref, k_ref, v_ref, *rest):
        o_ref = rest[nc]
        finish = _ride_along(rest[:nc] + rest[nc + 1:], nc, False)
        i = pl.program_id(1)
        n_steps = (i + 2) // 2
        later, _, first = _sb_consts()
        qv = q_ref[...]
        zero = jnp.zeros_like(qv)
        q2 = jnp.concatenate([jnp.where(first, qv, zero), jnp.where(first, zero, qv)], axis=0)

        def window(w, carry, diagonal):
            start, mask = _sb_window(i, w)
            if diagonal:
                mask = (diag_ref[0], diag_ref[1])
            kw = k_ref[pl.ds(start, SB_KEYS), :]
            vw = v_ref[pl.ds(start, SB_KEYS), :]
            c_l, acc = carry
            a, c_l = _sb_weights(_sb_scores(_mm_nt(q2, kw), mask), later, c_l)
            return c_l, acc + _mm(a, vw)

        def step(state):
            w, _, carry = state
            carry = window(w, carry, False)
            return w + 1, _sb_alive(carry), carry

        carry = window(0, (jnp.zeros((2 * QBLK, 128), F32),) * 2, True)
        _, _, carry = lax.while_loop(lambda s: (s[0] < n_steps) & (s[1] > 0), step,
                                     (jnp.int32(1), _sb_alive(carry), carry))
        o_ref[...] = jnp.where(first, carry[1][:QBLK], carry[1][QBLK:])
        finish()

    qblk = pl.BlockSpec((QBLK, 128), lambda h, i: (i, h))
    kblk = pl.BlockSpec((T, 128), lambda h, i: (0, h))
    outs = _call(
        body, name=name, grid=(D // 128, T // QBLK), out_shape=[jax.ShapeDtypeStruct((T, D), F32)] + lands,
        in_specs=[_SB_DIAG_SPEC, qblk, kblk, kblk] + [ANY_SPEC] * nc, out_specs=[qblk] + [ANY_SPEC] * nc,
        scratch_shapes=sems, compiler_params=_params(32, 2),
    )(_sb_diagonal(), q, k, v, *cargo)
    return outs[0], outs[1:]


def _sb_bwd(q, k, v, o, do, cargo, name):
    T, D = q.shape
    nc = len(cargo)
    lands, sems = _exchange_shapes(cargo, True)

    def body(diag_ref, q_ref, k_ref, v_ref, o_ref, do_ref, *rest):
        dq_ref, dk_ref, dv_ref = rest[nc:nc + 3]
        finish = _ride_along(rest[:nc] + rest[nc + 3:], nc, True)
        i = pl.program_id(1)

        @pl.when(i == 0)
        def _():
            dk_ref[...] = jnp.zeros_like(dk_ref)
            dv_ref[...] = jnp.zeros_like(dv_ref)

        n_steps = (i + 2) // 2
        later, later_eq, first = _sb_consts()
        qv = q_ref[...]
        dob = do_ref[...].astype(MXU)
        zero = jnp.zeros_like(qv)
        q2 = jnp.concatenate([jnp.where(first, qv, zero), jnp.where(first, zero, qv)], axis=0)
        do2 = jnp.concatenate([jnp.where(first, dob, zero), jnp.where(first, zero, dob)], axis=0)
        prod = o_ref[...] * dob.astype(F32)
        prod2 = jnp.concatenate([jnp.where(first, prod, 0.0), jnp.where(first, 0.0, prod)], axis=0)
        total = _split_dot(prod2, jnp.ones((128, 128), MXU), 3)
        total = jnp.concatenate([total, total], axis=1)

        def window(w, carry, diagonal):
            start, mask = _sb_window(i, w)
            if diagonal:
                mask = (diag_ref[0], diag_ref[1])
            kw = k_ref[pl.ds(start, SB_KEYS), :]
            vw = v_ref[pl.ds(start, SB_KEYS), :]
            c_l, c_e, dq = carry
            log_sig, parts = _sb_scores(_mm_nt(q2, kw), mask)
            a, c_l = _sb_weights((log_sig, parts), later, c_l)
            ab = a.astype(MXU)
            e = ab.astype(F32) * _mm_nt(do2, vw)
            from_here, c_e = _sb_suffix(_sb_terms(e, 3), later_eq, c_e)
            sig = jnp.exp(log_sig)
            dzb = (e * (1.0 - sig) - sig * (total - from_here)).astype(MXU)
            dk_ref[pl.ds(start, SB_KEYS), :] += _mm_tn(dzb, q2)
            dv_ref[pl.ds(start, SB_KEYS), :] += _mm_tn(ab, do2)
            return c_l, c_e, dq + _mm(dzb, kw)

        def step(state):
            w, _, carry = state
            carry = window(w, carry, False)
            return w + 1, _sb_alive(carry), carry

        carry = window(0, (jnp.zeros((2 * QBLK, 128), F32),) * 3, True)
        _, _, carry = lax.while_loop(lambda s: (s[0] < n_steps) & (s[1] > 0), step,
                                     (jnp.int32(1), _sb_alive(carry), carry))
        dq_ref[...] = jnp.where(first, carry[2][:QBLK], carry[2][QBLK:]) * SCALE
        finish()

    qblk = pl.BlockSpec((QBLK, 128), lambda h, i: (i, h))
    kblk = pl.BlockSpec((T, 128), lambda h, i: (0, h))
    full = jax.ShapeDtypeStruct((T, D), F32)
    outs = _call(
        body, name=name, grid=(D // 128, T // QBLK), out_shape=[full, full, full] + lands,
        in_specs=[_SB_DIAG_SPEC, qblk, kblk, kblk, qblk, qblk] + [ANY_SPEC] * nc,
        out_specs=[qblk, kblk, kblk] + [ANY_SPEC] * nc, scratch_shapes=sems, compiler_params=_params(32, 2),
    )(_sb_diagonal(), q, k, v, o, do, *cargo)
    return outs[0], outs[1], outs[2], outs[3:]


def _proj_res(x, a, w, name):
    T, D = x.shape
    tm = min(TOKEN_TILE, T)

    def body(x_ref, a_ref, w_ref, o_ref):
        o_ref[...] = x_ref[...] + _mm(a_ref[...], w_ref[...])

    return _call(
        body, name=name, grid=(T // tm,), out_shape=jax.ShapeDtypeStruct((T, D), F32),
        in_specs=[_tile(tm, D), _tile(tm, a.shape[1]), _whole(w.shape)], out_specs=_tile(tm, D),
        compiler_params=_params(32),
    )(x, a, w)


def _proj_nt(g, w, name):
    T = g.shape[0]
    K = w.shape[0]
    tm = min(TOKEN_TILE, T)

    def body(g_ref, w_ref, o_ref):
        o_ref[...] = _mm_nt(g_ref[...], w_ref[...])

    return _call(
        body, name=name, grid=(T // tm,), out_shape=jax.ShapeDtypeStruct((T, K), F32),
        in_specs=[_tile(tm, g.shape[1]), _whole(w.shape)], out_specs=_tile(tm, K),
        compiler_params=_params(32),
    )(g, w)


def _loss_grad(y, tgt, name):
    T, D = y.shape
    tm = min(TOKEN_TILE, T)

    def body(y_ref, t_ref, dy_ref, loss_ref):
        @pl.when(pl.program_id(0) == 0)
        def _():
            loss_ref[...] = jnp.zeros_like(loss_ref)
        diff = y_ref[...] - t_ref[...]
        dy_ref[...] = diff * (1.0 / D)
        rows = jnp.sum(diff * diff, axis=1, keepdims=True) * (1.0 / D)
        loss_ref[...] += 0.5 * jnp.sum(rows, axis=0, keepdims=True)

    return _call(
        body, name=name, grid=(T // tm,),
        out_shape=[jax.ShapeDtypeStruct((T, D), F32), jax.ShapeDtypeStruct((1, 1), F32)],
        in_specs=[_tile(tm, D), _tile(tm, D)], out_specs=[_tile(tm, D), _acc((1, 1))],
        compiler_params=_params(32),
    )(y, tgt)


def _ple_bwd(dx, x, gate, pp, ln, w_g, name):
    T, D = x.shape
    tm = min(TOKEN_TILE, T)

    def body(dx_ref, x_ref, gate_ref, pp_ref, ln_ref, wg_ref, dxo_ref, dpp_ref, dgp_ref, dln_ref):
        @pl.when(pl.program_id(0) == 0)
        def _():
            dln_ref[...] = jnp.zeros_like(dln_ref)
        dxv = dx_ref[...]
        gate = gate_ref[...]
        _, xhat, rstd = _rms(x_ref[...], ln_ref[...])
        dpp_ref[...] = (dxv * gate).astype(MXU)
        dgp = (dxv * pp_ref[...] * gate * (1.0 - gate)).astype(MXU)
        dgp_ref[...] = dgp
        dxn, dln = _rms_bwd(_mm_nt(dgp, wg_ref[...]), xhat, rstd, ln_ref[...])
        dln_ref[...] += dln
        dxo_ref[...] = dxn + dxv

    return _call(
        body, name=name, grid=(T // tm,),
        out_shape=[jax.ShapeDtypeStruct((T, D), F32), jax.ShapeDtypeStruct((T, D), MXU),
                   jax.ShapeDtypeStruct((T, D), MXU), jax.ShapeDtypeStruct(ln.shape, F32)],
        in_specs=[_tile(tm, D)] * 4 + [_whole(ln.shape), _whole(w_g.shape)],
        out_specs=[_tile(tm, D), _tile(tm, D), _tile(tm, D), _acc(ln.shape)],
        compiler_params=_params(32),
    )(dx, x, gate, pp, ln, w_g)


def _mlp_bwd(dx, x, pre, ln, w_up, w_down, name):
    T, D = x.shape
    tm = min(TOKEN_TILE, T)
    nf = w_up.shape[2]
    F = nf * NDEV

    def body(dx_ref, x_ref, pre_ref, ln_ref, wup_ref, wdown_ref, dxo_ref, dpre_ref, s_ref, dln_ref):
        @pl.when(pl.program_id(0) == 0)
        def _():
            dln_ref[...] = jnp.zeros_like(dln_ref)
        dxv = dx_ref[...]
        _, xhat, rstd = _rms(x_ref[...], ln_ref[...])
        a = jnp.maximum(pre_ref[...], 0.0)
        s_ref[...] = (a * a).astype(MXU)
        dpre_ref[...] = (_mm_nt(dxv, wdown_ref[...]) * (2.0 * a)).astype(MXU)
        dh = _mm_nt(dpre_ref[:, :nf], wup_ref[0])
        for j in range(1, NDEV):
            dh += _mm_nt(dpre_ref[:, j * nf:(j + 1) * nf], wup_ref[j])
        dxn, dln = _rms_bwd(dh, xhat, rstd, ln_ref[...])
        dln_ref[...] += dln
        dxo_ref[...] = dxn + dxv

    return _call(
        body, name=name, grid=(T // tm,),
        out_shape=[jax.ShapeDtypeStruct((T, D), F32), jax.ShapeDtypeStruct((T, F), MXU),
                   jax.ShapeDtypeStruct((T, F), MXU), jax.ShapeDtypeStruct(ln.shape, F32)],
        in_specs=[_tile(tm, D), _tile(tm, D), _tile(tm, F), _whole(ln.shape), _whole(w_up.shape),
                  _whole(w_down.shape)],
        out_specs=[_tile(tm, D), _tile(tm, F), _tile(tm, F), _acc(ln.shape)],
        compiler_params=_params(56),
    )(dx, x, pre, ln, w_up, w_down)


def _qkv_bwd(dx, x, dq, dk, dv, q_pre, k_pre, ln_q, ln_kv, g_q, g_k, w_q, w_kv, name):
    T, D = x.shape
    tm = min(TOKEN_TILE, T)
    nk = w_kv.shape[2]
    n_tiles = T // tm

    def body(dx_ref, x_ref, dq_ref, dk_ref, dv_ref, qpre_ref, kpre_ref, lnq_ref, lnkv_ref, gq_ref, gk_ref,
             wq_ref, wkv_ref, dxo_ref, dqp_ref, dkv_ref, dlnq_ref, dlnkv_ref, dgq_ref, dgk_ref, gq_acc, gk_acc):
        i = pl.program_id(0)

        @pl.when(i == 0)
        def _():
            dlnq_ref[...] = jnp.zeros_like(dlnq_ref)
            dlnkv_ref[...] = jnp.zeros_like(dlnkv_ref)
            gq_acc[...] = jnp.zeros_like(gq_acc)
            gk_acc[...] = jnp.zeros_like(gk_acc)

        ones = _head_ones()
        for b in range(D // 128):
            cols = slice(b * 128, (b + 1) * 128)
            _, xh, rs = _head_rms(qpre_ref[:, cols], gq_ref[:, cols], ones)
            d, dg = _head_rms_bwd(dq_ref[:, cols], xh, rs, gq_ref[:, cols], ones)
            dqp_ref[:, cols] = d.astype(MXU)
            gq_acc[:, cols] += dg
            _, xh, rs = _head_rms(kpre_ref[:, cols], gk_ref[:, cols], ones)
            d, dg = _head_rms_bwd(dk_ref[:, cols], xh, rs, gk_ref[:, cols], ones)
            dkv_ref[:, cols] = d.astype(MXU)
            gk_acc[:, cols] += dg
        dkv_ref[:, D:] = dv_ref[...].astype(MXU)

        _, xhat, rstd = _rms(x_ref[...], lnq_ref[...])
        dhq = _mm_nt(dqp_ref[...], wq_ref[...])
        dhkv = _mm_nt(dkv_ref[:, :nk], wkv_ref[0])
        for j in range(1, NDEV):
            dhkv += _mm_nt(dkv_ref[:, j * nk:(j + 1) * nk], wkv_ref[j])
        dxq, dlnq = _rms_bwd(dhq, xhat, rstd, lnq_ref[...])
        dxkv, dlnkv = _rms_bwd(dhkv, xhat, rstd, lnkv_ref[...])
        dlnq_ref[...] += dlnq
        dlnkv_ref[...] += dlnkv
        dxo_ref[...] = dx_ref[...] + dxq + dxkv

        @pl.when(i == n_tiles - 1)
        def _():
            row = lax.broadcasted_iota(jnp.int32, (D, 128), 0)
            col = lax.broadcasted_iota(jnp.int32, (D, 128), 1)
            fold = (jnp.bitwise_and(row, HEAD_DIM - 1) == col).astype(MXU)
            dgq_ref[...] = _split_dot(jnp.broadcast_to(gq_acc[...], (8, D)), fold, 3)
            dgk_ref[...] = _split_dot(jnp.broadcast_to(gk_acc[...], (8, D)), fold, 3)

    small = jax.ShapeDtypeStruct((8, 128), F32)
    return _call(
        body, name=name, grid=(n_tiles,),
        out_shape=[jax.ShapeDtypeStruct((T, D), F32), jax.ShapeDtypeStruct((T, D), MXU),
                   jax.ShapeDtypeStruct((T, 2 * D), MXU), jax.ShapeDtypeStruct(ln_q.shape, F32),
                   jax.ShapeDtypeStruct(ln_kv.shape, F32), small, small],
        in_specs=[_tile(tm, D)] * 7 + [_whole(ln_q.shape), _whole(ln_kv.shape), _whole(g_q.shape),
                                       _whole(g_k.shape), _whole(w_q.shape), _whole(w_kv.shape)],
        out_specs=[_tile(tm, D), _tile(tm, D), _tile(tm, 2 * D), _acc(ln_q.shape), _acc(ln_kv.shape),
                   _acc((8, 128)), _acc((8, 128))],
        scratch_shapes=[pltpu.VMEM((1, D), F32), pltpu.VMEM((1, D), F32)],
        compiler_params=_params(48),
    )(dx, x, dq, dk, dv, q_pre, k_pre, ln_q, ln_kv, g_q, g_k, w_q, w_kv)


def _sgu_bwd(dx, x, z, ln, w_in, g_v, ws, wsT, bT, w_out, name):
    T, D = x.shape
    tm = min(TOKEN_TILE, T)
    nw = w_in.shape[2]

    def body(dx_ref, x_ref, z_ref, ln_ref, win_ref, gv_ref, ws_ref, wsT_ref, bT_ref, wout_ref,
             dxo_ref, dz_ref, dws_ref, dbT_ref, dln_ref, dgv_ref, mix_ref, dvn_ref):
        @pl.when(pl.program_id(0) == 0)
        def _():
            dws_ref[...] = jnp.zeros_like(dws_ref)
            dbT_ref[...] = jnp.zeros_like(dbT_ref)
            dln_ref[...] = jnp.zeros_like(dln_ref)
            dgv_ref[...] = jnp.zeros_like(dgv_ref)
        dxv = dx_ref[...]
        _, xhat, rstd = _rms(x_ref[...], ln_ref[...])
        u, du = _gelu_and_grad(z_ref[:, :D])
        gv, dgv = _gelu_and_grad(z_ref[:, D:])
        vn, vhat, rstd_v = _rms(gv, gv_ref[...])
        vnb = vn.astype(MXU)
        _spatial_mix(vnb, ws_ref, bT_ref, mix_ref, tm)
        dy = _mm_nt(dxv, wout_ref[...])
        d_u = dy * mix_ref[...]
        d_mix = dy * u
        dmb = d_mix.astype(MXU)
        tri, triT = _tril_mask()
        for g in range(GROUPS):
            wmT = jnp.where(triT, wsT_ref[g], 0.0).astype(MXU)
            cols = slice(g * CHUNK, (g + 1) * CHUNK)
            for ch in range(tm // CHUNK):
                rows = slice(ch * CHUNK, (ch + 1) * CHUNK)
                dm = dmb[rows, cols]
                dws_ref[g] += jnp.where(tri, _mm_nt(dm, vnb[rows, cols]), 0.0)
                dbT_ref[:, g:g + 1] += jnp.sum(d_mix[rows, cols], axis=1, keepdims=True)
                dvn_ref[rows, cols] = _mm(wmT, dm)
        d_gv, dg = _rms_bwd(dvn_ref[...], vhat, rstd_v, gv_ref[...])
        dgv_ref[...] += dg
        dz_ref[:, :D] = (d_u * du).astype(MXU)
        dz_ref[:, D:] = (d_gv * dgv).astype(MXU)
        dh = _mm_nt(dz_ref[:, :nw], win_ref[0])
        for j in range(1, NDEV):
            dh += _mm_nt(dz_ref[:, j * nw:(j + 1) * nw], win_ref[j])
        dxn, dln = _rms_bwd(dh, xhat, rstd, ln_ref[...])
        dln_ref[...] += dln
        dxo_ref[...] = dxn + dxv

    return _call(
        body, name=name, grid=(T // tm,),
        out_shape=[jax.ShapeDtypeStruct((T, D), F32), jax.ShapeDtypeStruct((T, 2 * D), MXU),
                   jax.ShapeDtypeStruct(ws.shape, F32), jax.ShapeDtypeStruct(bT.shape, F32),
                   jax.ShapeDtypeStruct(ln.shape, F32), jax.ShapeDtypeStruct(g_v.shape, F32)],
        in_specs=[_tile(tm, D), _tile(tm, D), _tile(tm, 2 * D), _whole(ln.shape), _whole(w_in.shape),
                  _whole(g_v.shape), _whole(ws.shape), _whole(wsT.shape), _whole(bT.shape), _whole(w_out.shape)],
        out_specs=[_tile(tm, D), _tile(tm, 2 * D), _acc(ws.shape), _acc(bT.shape), _acc(ln.shape),
                   _acc(g_v.shape)],
        scratch_shapes=[pltpu.VMEM((tm, D), F32), pltpu.VMEM((tm, D), F32)],
        compiler_params=_params(48),
    )(dx, x, z, ln, w_in, g_v, ws, wsT, bT, w_out)


def _wgrad_rows(a, g, name):
    T, K = a.shape
    N = g.shape[1]
    kb = K // NDEV

    def body(a_ref, g_ref, o_ref):
        o_ref[...] = _mm_tn(a_ref[...], g_ref[...]).astype(COMM)

    return _call(
        body, name=name, grid=(NDEV,), out_shape=jax.ShapeDtypeStruct((K, N), COMM),
        in_specs=[pl.BlockSpec((T, kb), lambda j: (0, j)), _whole(g.shape)],
        out_specs=pl.BlockSpec((kb, N), lambda j: (j, 0)),
        compiler_params=_params(40),
    )(a, g).reshape(NDEV, kb, N)


def _wgrad_cols(a, g, name):
    T, K = a.shape
    N = g.shape[1]
    nb = N // NDEV

    def body(a_ref, g_ref, o_ref):
        o_ref[...] = _mm_tn(a_ref[...], g_ref[...]).astype(COMM)

    return _call(
        body, name=name, grid=(NDEV,), out_shape=jax.ShapeDtypeStruct((NDEV, K, nb), COMM),
        in_specs=[_whole(a.shape), pl.BlockSpec((T, nb), lambda j: (0, j))],
        out_specs=pl.BlockSpec((None, K, nb), lambda j: (j, 0, 0)),
        compiler_params=_params(40),
    )(a, g)


def _adamw(w, m, v, slots, name):
    R, C = w.shape
    n = slots.shape[0]
    tr = math.gcd(R, max(8, (128 * 1024) // C))
    if tr < 64:
        tr = R
    bc1 = 1.0 - ADAM_B1 ** ADAM_STEP
    bc2 = 1.0 - ADAM_B2 ** ADAM_STEP

    def body(w_ref, m_ref, v_ref, s_ref, g_ref, d_ref, mo_ref, vo_ref):
        g = s_ref[0].astype(F32)
        for j in range(1, n):
            g = g + s_ref[j].astype(F32)
        mn = ADAM_B1 * m_ref[...] + (1.0 - ADAM_B1) * g
        vn = ADAM_B2 * v_ref[...] + (1.0 - ADAM_B2) * (g * g)
        g_ref[...] = g
        mo_ref[...] = mn
        vo_ref[...] = vn
        d_ref[...] = -ADAM_LR * ((mn / bc1) / (jnp.sqrt(vn / bc2) + ADAM_EPS) + ADAM_WD * w_ref[...])

    blk = pl.BlockSpec((tr, C), lambda i: (i, 0))
    out = jax.ShapeDtypeStruct((R, C), F32)
    return _call(
        body, name=name, grid=(R // tr,), out_shape=[out, out, out, out],
        in_specs=[blk, blk, blk, pl.BlockSpec((n, tr, C), lambda i: (0, i, 0))], out_specs=[blk] * 4,
        compiler_params=_params(32),
    )(w, m, v, slots)


def _rows128(a):
    flat = a.reshape(-1)
    rows = -(-flat.shape[0] // 1024) * 8
    flat = jnp.pad(flat, (0, rows * 128 - flat.shape[0]))
    return flat.reshape(rows, 128)


def kernel(x, p, ln_mix_a, w_in_a, g_v_a, w_spatial, b_spatial, w_out_a, ln_kv, w_kv, g_k, ln_mix_b, w_q, g_q, w_out_b, ln_mlp, w_up, w_down, ln_ple, w_ple_gate, w_ple_proj, loss_target, m_ln_mix_a, m_w_in_a, m_g_v_a, m_w_spatial, m_b_spatial, m_w_out_a, m_ln_kv, m_w_kv, m_g_k, m_ln_mix_b, m_w_q, m_g_q, m_w_out_b, m_ln_mlp, m_w_up, m_w_down, m_ln_ple, m_w_ple_gate, m_w_ple_proj, v_ln_mix_a, v_w_in_a, v_g_v_a, v_w_spatial, v_b_spatial, v_w_out_a, v_ln_kv, v_w_kv, v_g_k, v_ln_mix_b, v_w_q, v_g_q, v_w_out_b, v_ln_mlp, v_w_up, v_w_down, v_ln_ple, v_w_ple_gate, v_w_ple_proj):
    me = 4 * lax.axis_index("x") + 2 * lax.axis_index("y") + lax.axis_index("c")
    D = x.shape[2]
    x0, tgt = x[0], loss_target[0]
    n_layers = w_up.shape[0]

    c = lambda w: w.astype(COMM)
    first = [c(w_in_a[0]), c(w_out_a[0]), ln_mix_a, g_v_a, c(w_up[0]), c(w_down[0]), c(w_ple_gate[0]),
             c(w_ple_proj[0]), c(w_q[0]), c(w_kv)]
    second = [c(w_out_b[0]), c(w_up[1]), c(w_down[1]), c(w_ple_gate[1]), c(w_ple_proj[1])]
    W_in, W_out_a, ln_a, gv_a, W_up0, W_down0, W_g0, W_pp0, W_q, W_kv = _gather_two_level(first, "gather_first")
    W_out_a, ln_a, gv_a = W_out_a.reshape(-1, D), ln_a.reshape(1, D), gv_a.reshape(1, D)
    W_down0, W_g0, W_q = W_down0.reshape(-1, D), W_g0.reshape(-1, D), W_q.reshape(-1, D)
    ws = w_spatial[0]
    wsT = jnp.swapaxes(ws, 1, 2)
    bT = b_spatial[0].T
    ln_kv2, ln_b = ln_kv.reshape(1, D), ln_mix_b
    gk2 = jnp.tile(g_k.reshape(1, HEAD_DIM), (1, D // HEAD_DIM))
    gq2 = jnp.tile(g_q, (1, D // HEAD_DIM))
    ln_m = [ln_mlp[l:l + 1] for l in range(n_layers)]
    ln_p = [ln_ple[l:l + 1] for l in range(n_layers)]

    x1, z, h_a, y_a = _sgu_fwd(x0, ln_a, W_in, gv_a, ws, bT, W_out_a, "sgu_fwd")
    x2, pre0, hm0 = _mlp_fwd(x1, ln_m[0], W_up0, W_down0, "mlp_fwd0")
    x3, gate0, pp0, hp0 = _ple_fwd(x2, p[0, 0], ln_p[0], W_g0, W_pp0, "ple_fwd0")
    qn, kn, vn, q_pre, k_pre, h_q, h_kv = _qkv_fwd(x3, ln_b, ln_kv2, gq2, gk2, W_q, W_kv, "qkv_fwd")
    o2d, (W_out_b, W_up1, W_down1, W_g1, W_pp1) = _sb_fwd(qn, kn, vn, second, "sb_fwd")
    W_out_b, W_down1, W_g1 = W_out_b.reshape(-1, D), W_down1.reshape(-1, D), W_g1.reshape(-1, D)
    x4 = _proj_res(x3, o2d, W_out_b, "attn_out")
    x5, pre1, hm1 = _mlp_fwd(x4, ln_m[1], W_up1, W_down1, "mlp_fwd1")
    x6, gate1, pp1, hp1 = _ple_fwd(x5, p[1, 0], ln_p[1], W_g1, W_pp1, "ple_fwd1")
    dy, loss_part = _loss_grad(x6, tgt, "loss_grad")
    loss = lax.psum(loss_part[0, 0], ("x", "y", "c"))

    dx5, dpp1, dgp1, dlnp1 = _ple_bwd(dy, x5, gate1, pp1, ln_p[1], W_g1, "ple_bwd1")
    dx4, dpre1, s1, dlnm1 = _mlp_bwd(dx5, x4, pre1, ln_m[1], W_up1, W_down1, "mlp_bwd1")
    wg_second = [_wgrad_cols(hm1, dpre1, "wg_up1"), _wgrad_rows(s1, dx5, "wg_down1"),
                 _wgrad_rows(hp1, dgp1, "wg_gate1"), _wgrad_cols(p[1, 0].astype(MXU), dpp1, "wg_proj1"),
                 _wgrad_rows(o2d, dx4, "wg_out_b")]
    do2d = _proj_nt(dx4, W_out_b, "attn_out_bwd")
    dqn, dkn, dvn, (s_up1, s_down1, s_gate1, s_proj1, s_out_b) = _sb_bwd(qn, kn, vn, o2d, do2d, wg_second, "sb_bwd")
    dx3, dq_pre, dkv, dlnb, dlnkv, dgq, dgk = _qkv_bwd(dx4, x3, dqn, dkn, dvn, q_pre, k_pre, ln_b, ln_kv2, gq2, gk2,
                                                      W_q, W_kv, "qkv_bwd")
    dgq, dgk = dgq[:1, :HEAD_DIM], dgk[:1, :HEAD_DIM]
    dx2, dpp0, dgp0, dlnp0 = _ple_bwd(dx3, x2, gate0, pp0, ln_p[0], W_g0, "ple_bwd0")
    dx1, dpre0, s0, dlnm0 = _mlp_bwd(dx2, x1, pre0, ln_m[0], W_up0, W_down0, "mlp_bwd0")
    dx0, dz, dws, dbT, dlna, dgva = _sgu_bwd(dx1, x0, z, ln_a, W_in, gv_a, ws, wsT, bT, W_out_a, "sgu_bwd")
    wg_first = [_wgrad_rows(h_q, dq_pre, "wg_q"), _wgrad_cols(h_kv, dkv, "wg_kv"),
                _wgrad_cols(hm0, dpre0, "wg_up0"), _wgrad_rows(s0, dx2, "wg_down0"), _wgrad_rows(hp0, dgp0, "wg_gate0"),
                _wgrad_cols(p[0, 0].astype(MXU), dpp0, "wg_proj0"), _wgrad_cols(h_a, dz, "wg_in_a"),
                _wgrad_rows(y_a, dx1, "wg_out_a")]

    small = [("w_spatial", dws[None], w_spatial, m_w_spatial, v_w_spatial),
             ("b_spatial", dbT.T[None], b_spatial, m_b_spatial, v_b_spatial),
             ("ln_kv", dlnkv.reshape(-1), ln_kv, m_ln_kv, v_ln_kv),
             ("g_k", dgk.reshape(-1), g_k, m_g_k, v_g_k),
             ("ln_mix_b", dlnb, ln_mix_b, m_ln_mix_b, v_ln_mix_b),
             ("g_q", dgq, g_q, m_g_q, v_g_q),
             ("ln_mlp", jnp.concatenate([dlnm0, dlnm1]), ln_mlp, m_ln_mlp, v_ln_mlp),
             ("ln_ple", jnp.concatenate([dlnp0, dlnp1]), ln_ple, m_ln_ple, v_ln_ple)]
    sharded_vec = [("ln_mix_a", dlna, ln_mix_a, m_ln_mix_a, v_ln_mix_a),
                   ("g_v_a", dgva, g_v_a, m_g_v_a, v_g_v_a)]
    packs = [[], [], [], []]
    for _, g, w, m, v in small:
        for lst, a in zip(packs, (g, w, m, v)):
            lst.append(_rows128(a))
    for _, g, w, m, v in sharded_vec:
        packs[0].append(g.reshape(NDEV, -1))
        for lst, a in zip(packs[1:], (w, m, v)):
            lst.append(jnp.broadcast_to(a, (NDEV, a.shape[1])))
    g_pack, w_pack, m_pack, v_pack = (jnp.concatenate(lst) for lst in packs)
    g_pack8 = jnp.broadcast_to(g_pack[None], (NDEV,) + g_pack.shape)
    by_chip = [a.reshape((4, 2) + a.shape[1:]) for a in wg_first]
    from_sibling, (g_all,) = _scatter_pair(by_chip, [g_pack8], "scatter_pair")
    my_core = lax.axis_index("c")
    chip_sums = [_pair_sum(lax.dynamic_index_in_dim(a, my_core, 1, keepdims=False), o, f"pair_sum{j}")
                 for j, (a, o) in enumerate(zip(by_chip, from_sibling))]
    s_q, s_kv, s_up0, s_down0, s_gate0, s_proj0, s_in_a, s_out_a = _scatter_chips(chip_sums, "scatter_chips")

    def upd(w, m, v, s, name):
        shape = w.shape
        outs = _adamw(w.reshape(-1, shape[-1]), m.reshape(-1, shape[-1]), v.reshape(-1, shape[-1]), s, name)
        return [o.reshape(shape) for o in outs]

    res = {}
    per = {}
    per["w_up", 1] = upd(w_up[1], m_w_up[1], v_w_up[1], s_up1, "adam_up1")
    per["w_down", 1] = upd(w_down[1], m_w_down[1], v_w_down[1], s_down1, "adam_down1")
    per["w_ple_gate", 1] = upd(w_ple_gate[1], m_w_ple_gate[1], v_w_ple_gate[1], s_gate1, "adam_gate1")
    per["w_ple_proj", 1] = upd(w_ple_proj[1], m_w_ple_proj[1], v_w_ple_proj[1], s_proj1, "adam_proj1")
    res["w_out_b"] = upd(w_out_b, m_w_out_b, v_w_out_b, s_out_b, "adam_out_b")
    res["w_q"] = upd(w_q, m_w_q, v_w_q, s_q, "adam_q")
    res["w_kv"] = upd(w_kv, m_w_kv, v_w_kv, s_kv, "adam_kv")
    per["w_up", 0] = upd(w_up[0], m_w_up[0], v_w_up[0], s_up0, "adam_up0")
    per["w_down", 0] = upd(w_down[0], m_w_down[0], v_w_down[0], s_down0, "adam_down0")
    per["w_ple_gate", 0] = upd(w_ple_gate[0], m_w_ple_gate[0], v_w_ple_gate[0], s_gate0, "adam_gate0")
    per["w_ple_proj", 0] = upd(w_ple_proj[0], m_w_ple_proj[0], v_w_ple_proj[0], s_proj0, "adam_proj0")
    res["w_in_a"] = upd(w_in_a, m_w_in_a, v_w_in_a, s_in_a, "adam_in_a")
    res["w_out_a"] = upd(w_out_a, m_w_out_a, v_w_out_a, s_out_a, "adam_out_a")
    for nm in ("w_up", "w_down", "w_ple_gate", "w_ple_proj"):
        res[nm] = [jnp.stack([per[nm, l][t] for l in range(n_layers)]) for t in range(4)]

    outs = _adamw(w_pack, m_pack, v_pack, g_all, "adam_small")
    row = 0
    for nm, g, w, m, v in small:
        nrows = _rows128(w).shape[0]
        res[nm] = [o[row:row + nrows].reshape(-1)[:w.size].reshape(w.shape) for o in outs]
        row += nrows
    for nm, g, w, m, v in sharded_vec:
        res[nm] = [lax.dynamic_slice_in_dim(o[row:row + NDEV], me, 1, axis=0) for o in outs]
        row += NDEV

    names = ["ln_mix_a", "w_in_a", "g_v_a", "w_spatial", "b_spatial", "w_out_a", "ln_kv", "w_kv", "g_k", "ln_mix_b",
             "w_q", "g_q", "w_out_b", "ln_mlp", "w_up", "w_down", "ln_ple", "w_ple_gate", "w_ple_proj"]
    out = [loss, dx0[None]]
    for t in range(4):
        out += [res[nm][t] for nm in names]
    return tuple(out)
```

```python
import functools
import math

import jax
import jax.numpy as jnp
from jax import lax
from jax.experimental import pallas as pl
from jax.experimental.pallas import tpu as pltpu

F32 = jnp.float32
MXU = jnp.bfloat16
COMM = jnp.bfloat16
EPS = 1e-6
NDEV = 8
HEAD_DIM = 64
CHUNK = 128
GROUPS = 8
QBLK = 128
SCALE = HEAD_DIM ** -0.5
TOKEN_TILE = 256
ADAM_LR = 0.001
ADAM_B1 = 0.9
ADAM_B2 = 0.999
ADAM_EPS = 1e-08
ADAM_WD = 0.01
ADAM_STEP = 10
MESH = pl.DeviceIdType.MESH


def _call(body, **kw):
    return pl.pallas_call(body, **kw)


def _params(vmem_mb, n_axes=1):
    return pltpu.CompilerParams(dimension_semantics=("arbitrary",) * n_axes,
                                vmem_limit_bytes=vmem_mb << 20)


def _tile(tm, n):
    return pl.BlockSpec((tm, n), lambda i: (i, 0))


def _whole(shape):
    zeros = (0,) * len(shape)
    return pl.BlockSpec(shape, lambda i: zeros, pipeline_mode=pl.Buffered(1))


def _acc(shape):
    zeros = (0,) * len(shape)
    return pl.BlockSpec(shape, lambda i: zeros)


def _mm(a, b):
    return jnp.dot(a.astype(MXU), b.astype(MXU), preferred_element_type=F32)


def _mm_nt(a, b):
    return lax.dot_general(a.astype(MXU), b.astype(MXU), (((1,), (1,)), ((), ())),
                           preferred_element_type=F32)


def _mm_tn(a, b):
    return lax.dot_general(a.astype(MXU), b.astype(MXU), (((0,), (0,)), ((), ())),
                           preferred_element_type=F32)


def _split_dot(x, ones, terms=2):
    out = None
    for _ in range(terms):
        part = x.astype(MXU)
        x = x - part.astype(F32)
        d = jnp.dot(part, ones, preferred_element_type=F32)
        out = d if out is None else out + d
    return out


def _rms(x, g):
    rstd = lax.rsqrt(jnp.mean(x * x, axis=-1, keepdims=True) + EPS)
    xhat = x * rstd
    return xhat * g, xhat, rstd


def _rms_bwd(dh, xhat, rstd, g):
    dxh = dh * g
    dx = rstd * (dxh - xhat * jnp.mean(dxh * xhat, axis=-1, keepdims=True))
    dg = jnp.sum(dh * xhat, axis=0, keepdims=True)
    return dx, dg


_GELU_C = math.sqrt(2.0 / math.pi)


def _gelu(x):
    t = jnp.tanh(_GELU_C * (x + 0.044715 * (x * x * x)))
    return 0.5 * x * (1.0 + t)


def _gelu_and_grad(x):
    x2 = x * x
    t = jnp.tanh(_GELU_C * (x + 0.044715 * (x2 * x)))
    g = 0.5 * x * (1.0 + t)
    dg = 0.5 * (1.0 + t) + 0.5 * x * (1.0 - t * t) * (_GELU_C * (1.0 + 3.0 * 0.044715 * x2))
    return g, dg


def _softplus(z):
    return jnp.maximum(z, 0.0) + jnp.log(1.0 + jnp.exp(-jnp.abs(z)))


def _tril_mask():
    row = lax.broadcasted_iota(jnp.int32, (CHUNK, CHUNK), 0)
    col = lax.broadcasted_iota(jnp.int32, (CHUNK, CHUNK), 1)
    return row >= col, row <= col


ANY_SPEC = pl.BlockSpec(memory_space=pl.ANY)


def _my_index():
    return 4 * lax.axis_index("x") + 2 * lax.axis_index("y") + lax.axis_index("c")


def _exchange_copies(srcs, lands, send_sems, recv_sems, scatter, arriving):
    x, y, c = lax.axis_index("x"), lax.axis_index("y"), lax.axis_index("c")
    me = 4 * x + 2 * y + c
    out = []
    for a in range(len(srcs)):
        for k in range(NDEV - 1):
            bits = k + 1
            px = 1 - x if (bits >> 2) & 1 else x
            py = 1 - y if (bits >> 1) & 1 else y
            pc = 1 - c if bits & 1 else c
            peer = 4 * px + 2 * py + pc
            src = srcs[a].at[peer] if scatter else srcs[a]
            out.append(pltpu.make_async_remote_copy(
                src_ref=src, dst_ref=lands[a].at[peer if arriving else me],
                send_sem=send_sems.at[a * (NDEV - 1) + k], recv_sem=recv_sems.at[a * (NDEV - 1) + k],
                device_id=(px, py, pc), device_id_type=MESH))
    return out


def _exchange_shapes(arrs, scatter):
    n = len(arrs)
    if n == 0:
        return [], []
    lands = [jax.ShapeDtypeStruct(a.shape if scatter else (NDEV,) + a.shape, a.dtype) for a in arrs]
    sems = [pltpu.SemaphoreType.DMA((n * (NDEV - 1),)), pltpu.SemaphoreType.DMA((n * (NDEV - 1),)),
            pltpu.SemaphoreType.DMA((n,))]
    return lands, sems


def _exchange_start(srcs, lands, sems, scatter):
    send_sems, recv_sems, local_sems = sems
    me = _my_index()
    for a in range(len(srcs)):
        pltpu.make_async_copy(srcs[a].at[me] if scatter else srcs[a], lands[a].at[me], local_sems.at[a]).start()
    for send in _exchange_copies(srcs, lands, send_sems, recv_sems, scatter, False):
        send.start()


def _exchange_finish(srcs, lands, sems, scatter):
    send_sems, recv_sems, local_sems = sems
    me = _my_index()
    for arrive in _exchange_copies(srcs, lands, send_sems, recv_sems, scatter, True):
        arrive.wait_recv()
    for send in _exchange_copies(srcs, lands, send_sems, recv_sems, scatter, False):
        send.wait_send()
    for a in range(len(srcs)):
        pltpu.make_async_copy(srcs[a].at[me] if scatter else srcs[a], lands[a].at[me], local_sems.at[a]).wait()


def _gather_two_level(arrs, name):
    n = len(arrs)
    lands = [jax.ShapeDtypeStruct((NDEV,) + a.shape, a.dtype) for a in arrs]

    def body(*refs):
        srcs, outs = refs[:n], refs[n:2 * n]
        send_sems, recv_sems, local_sems = refs[2 * n:]
        x, y, c = lax.axis_index("x"), lax.axis_index("y"), lax.axis_index("c")
        me, sibling = (x, y, c), (x, y, 1 - c)
        chips = [(1 - x, y), (x, 1 - y), (1 - x, 1 - y)]

        def index(dev):
            return 4 * dev[0] + 2 * dev[1] + dev[2]

        def copy(a, k, block, to, src=None):
            dst = outs[a].at[index(block)]
            return pltpu.make_async_remote_copy(
                src_ref=dst if src is None else src, dst_ref=dst, send_sem=send_sems.at[a, k],
                recv_sem=recv_sems.at[a, k], device_id=to, device_id_type=MESH)

        mine, first, passed = [], [], []
        for a in range(n):
            cp = pltpu.make_async_copy(srcs[a], outs[a].at[index(me)], local_sems.at[a])
            cp.start()
            mine.append(cp)
            first.append(copy(a, 0, me, sibling, src=srcs[a]))
            first += [copy(a, 1 + j, me, (*chip, c), src=srcs[a]) for j, chip in enumerate(chips)]
        for cp in first:
            cp.start()
        for a in range(n):
            for j, chip in enumerate(chips):
                copy(a, 1 + j, (*chip, c), me).wait_recv()
                cp = copy(a, 4 + j, (*chip, c), sibling)
                cp.start()
                passed.append(cp)
        for a in range(n):
            copy(a, 0, sibling, me).wait_recv()
            for j, chip in enumerate(chips):
                copy(a, 4 + j, (*chip, 1 - c), me).wait_recv()
        for cp in first + passed:
            cp.wait_send()
        for cp in mine:
            cp.wait()

    return _call(body, name=name, out_shape=lands, in_specs=[ANY_SPEC] * n, out_specs=[ANY_SPEC] * n,
                 scratch_shapes=[pltpu.SemaphoreType.DMA((n, NDEV - 1)), pltpu.SemaphoreType.DMA((n, NDEV - 1)),
                                 pltpu.SemaphoreType.DMA((n,))])(*arrs)


def _scatter_pair(arrs, extra, name):
    n, ne = len(arrs), len(extra)
    lands = [jax.ShapeDtypeStruct((4,) + a.shape[2:], a.dtype) for a in arrs]
    extra_lands, extra_sems = _exchange_shapes(extra, True)

    def body(*refs):
        srcs, xsrc = refs[:n], refs[n:n + ne]
        outs, xout = refs[n + ne:2 * n + ne], refs[2 * n + ne:2 * (n + ne)]
        send_sems, recv_sems = refs[2 * (n + ne)], refs[2 * (n + ne) + 1]
        xsems = refs[2 * (n + ne) + 2:]
        x, y, c = lax.axis_index("x"), lax.axis_index("y"), lax.axis_index("c")
        _exchange_start(xsrc, xout, xsems, True)
        copies = [pltpu.make_async_remote_copy(
            src_ref=srcs[a].at[k, 1 - c], dst_ref=outs[a].at[k], send_sem=send_sems.at[a, k],
            recv_sem=recv_sems.at[a, k], device_id=(x, y, 1 - c), device_id_type=MESH)
            for a in range(n) for k in range(4)]
        for cp in copies:
            cp.start()
        for cp in copies:
            cp.wait()
        _exchange_finish(xsrc, xout, xsems, True)

    outs = _call(
        body, name=name, out_shape=lands + extra_lands, in_specs=[ANY_SPEC] * (n + ne), out_specs=[ANY_SPEC] * (n + ne),
        scratch_shapes=[pltpu.SemaphoreType.DMA((n, 4)), pltpu.SemaphoreType.DMA((n, 4))] + extra_sems,
    )(*arrs, *extra)
    return outs[:n], outs[n:]


def _pair_sum(own, other, name):
    _, R, C = own.shape
    tr = math.gcd(R, max(8, (128 * 1024) // C))

    def body(a_ref, b_ref, o_ref):
        o_ref[...] = (a_ref[...].astype(F32) + b_ref[...].astype(F32)).astype(COMM)

    blk = pl.BlockSpec((4, tr, C), lambda i: (0, i, 0))
    return _call(body, name=name, grid=(R // tr,), out_shape=jax.ShapeDtypeStruct(own.shape, COMM),
                 in_specs=[blk, blk], out_specs=blk, compiler_params=_params(32))(own, other)


def _scatter_chips(arrs, name):
    n = len(arrs)
    lands = [jax.ShapeDtypeStruct(a.shape, a.dtype) for a in arrs]

    def body(*refs):
        srcs, outs = refs[:n], refs[n:2 * n]
        send_sems, recv_sems, local_sems = refs[2 * n:]
        x, y, c = lax.axis_index("x"), lax.axis_index("y"), lax.axis_index("c")
        chip = 2 * x + y
        others = [(1 - x, y), (x, 1 - y), (1 - x, 1 - y)]
        local = [pltpu.make_async_copy(srcs[a].at[chip], outs[a].at[chip], local_sems.at[a]) for a in range(n)]
        for cp in local:
            cp.start()

        def copies(arriving):
            return [pltpu.make_async_remote_copy(
                src_ref=srcs[a].at[2 * px + py], dst_ref=outs[a].at[2 * px + py if arriving else chip],
                send_sem=send_sems.at[a, j], recv_sem=recv_sems.at[a, j], device_id=(px, py, c), device_id_type=MESH)
                for a in range(n) for j, (px, py) in enumerate(others)]

        for cp in copies(False):
            cp.start()
        for cp in copies(True):
            cp.wait_recv()
        for cp in copies(False):
            cp.wait_send()
        for cp in local:
            cp.wait()

    return _call(body, name=name, out_shape=lands, in_specs=[ANY_SPEC] * n, out_specs=[ANY_SPEC] * n,
                 scratch_shapes=[pltpu.SemaphoreType.DMA((n, 3)), pltpu.SemaphoreType.DMA((n, 3)),
                                 pltpu.SemaphoreType.DMA((n,))])(*arrs)


def _exchange(arrs, scatter, name):
    n = len(arrs)
    lands, sems = _exchange_shapes(arrs, scatter)

    def body(*refs):
        _exchange_start(refs[:n], refs[n:2 * n], refs[2 * n:], scatter)
        _exchange_finish(refs[:n], refs[n:2 * n], refs[2 * n:], scatter)

    return _call(body, name=name, out_shape=lands, in_specs=[ANY_SPEC] * n, out_specs=[ANY_SPEC] * n,
                 scratch_shapes=sems)(*arrs)


def _spatial_mix(vnb, ws_ref, bT_ref, mix_ref, tm):
    tri, _ = _tril_mask()
    for g in range(GROUPS):
        wm = jnp.where(tri, ws_ref[g], 0.0).astype(MXU)
        cols = slice(g * CHUNK, (g + 1) * CHUNK)
        for ch in range(tm // CHUNK):
            rows = slice(ch * CHUNK, (ch + 1) * CHUNK)
            mix_ref[rows, cols] = _mm(wm, vnb[rows, cols]) + bT_ref[:, g:g + 1]


def _sgu_fwd(x, ln, w_in, g_v, ws, bT, w_out, name):
    T, D = x.shape
    tm = min(TOKEN_TILE, T)
    nw = w_in.shape[2]

    def body(x_ref, ln_ref, win_ref, gv_ref, ws_ref, bT_ref, wout_ref, xo_ref, z_ref, h_ref, y_ref, mix_ref):
        xv = x_ref[...]
        h, _, _ = _rms(xv, ln_ref[...])
        hb = h.astype(MXU)
        h_ref[...] = hb
        for j in range(NDEV):
            z_ref[:, j * nw:(j + 1) * nw] = _mm(hb, win_ref[j])
        u = _gelu(z_ref[:, :D])
        gv = _gelu(z_ref[:, D:])
        vn, _, _ = _rms(gv, gv_ref[...])
        _spatial_mix(vn.astype(MXU), ws_ref, bT_ref, mix_ref, tm)
        y = (u * mix_ref[...]).astype(MXU)
        y_ref[...] = y
        xo_ref[...] = xv + _mm(y, wout_ref[...])

    return _call(
        body, name=name, grid=(T // tm,),
        out_shape=[jax.ShapeDtypeStruct((T, D), F32), jax.ShapeDtypeStruct((T, 2 * D), F32),
                   jax.ShapeDtypeStruct((T, D), MXU), jax.ShapeDtypeStruct((T, D), MXU)],
        in_specs=[_tile(tm, D), _whole(ln.shape), _whole(w_in.shape), _whole(g_v.shape), _whole(ws.shape),
                  _whole(bT.shape), _whole(w_out.shape)],
        out_specs=[_tile(tm, D), _tile(tm, 2 * D), _tile(tm, D), _tile(tm, D)],
        scratch_shapes=[pltpu.VMEM((tm, D), F32)],
        compiler_params=_params(40),
    )(x, ln, w_in, g_v, ws, bT, w_out)


def _mlp_fwd(x, ln, w_up, w_down, cargo, name):
    T, D = x.shape
    tm = min(TOKEN_TILE, T)
    nf = w_up.shape[2]
    F = nf * NDEV
    nc = len(cargo)
    lands, sems = _exchange_shapes(cargo, False)

    def body(x_ref, ln_ref, wup_ref, wdown_ref, *rest):
        xo_ref, pre_ref, h_ref = rest[nc:nc + 3]
        finish = _ride_along(rest[:nc] + rest[nc + 3:], nc, False, rank=1)
        xv = x_ref[...]
        h, _, _ = _rms(xv, ln_ref[...])
        hb = h.astype(MXU)
        h_ref[...] = hb
        for j in range(NDEV):
            pre_ref[:, j * nf:(j + 1) * nf] = _mm(hb, wup_ref[j])
        a = jnp.maximum(pre_ref[...], 0.0)
        xo_ref[...] = xv + _mm(a * a, wdown_ref[...])
        finish()

    outs = _call(
        body, name=name, grid=(T // tm,),
        out_shape=[jax.ShapeDtypeStruct((T, D), F32), jax.ShapeDtypeStruct((T, F), F32),
                   jax.ShapeDtypeStruct((T, D), MXU)] + lands,
        in_specs=[_tile(tm, D), _whole(ln.shape), _whole(w_up.shape), _whole(w_down.shape)] + [ANY_SPEC] * nc,
        out_specs=[_tile(tm, D), _tile(tm, F), _tile(tm, D)] + [ANY_SPEC] * nc,
        scratch_shapes=sems, compiler_params=_params(52),
    )(x, ln, w_up, w_down, *cargo)
    return outs[0], outs[1], outs[2], outs[3:]


def _ple_fwd(x, p, ln, w_g, w_pp, name):
    T, D = x.shape
    tm = min(TOKEN_TILE, T)
    npp = w_pp.shape[2]

    def body(x_ref, p_ref, ln_ref, wg_ref, wpp_ref, xo_ref, gate_ref, pp_ref, h_ref):
        xv = x_ref[...]
        h, _, _ = _rms(xv, ln_ref[...])
        hb = h.astype(MXU)
        h_ref[...] = hb
        gate = jax.nn.sigmoid(_mm(hb, wg_ref[...]))
        gate_ref[...] = gate
        pb = p_ref[...].astype(MXU)
        for j in range(NDEV):
            pp_ref[:, j * npp:(j + 1) * npp] = _mm(pb, wpp_ref[j])
        xo_ref[...] = xv + pp_ref[...] * gate

    return _call(
        body, name=name, grid=(T // tm,),
        out_shape=[jax.ShapeDtypeStruct((T, D), F32), jax.ShapeDtypeStruct((T, D), F32),
                   jax.ShapeDtypeStruct((T, D), F32), jax.ShapeDtypeStruct((T, D), MXU)],
        in_specs=[_tile(tm, D), _tile(tm, p.shape[1]), _whole(ln.shape), _whole(w_g.shape), _whole(w_pp.shape)],
        out_specs=[_tile(tm, D), _tile(tm, D), _tile(tm, D), _tile(tm, D)],
        compiler_params=_params(32),
    )(x, p, ln, w_g, w_pp)


def _head_ones():
    row = lax.broadcasted_iota(jnp.int32, (128, 128), 0)
    col = lax.broadcasted_iota(jnp.int32, (128, 128), 1)
    return (jnp.right_shift(row, 6) == jnp.right_shift(col, 6)).astype(MXU)


def _head_rms(x, g, ones):
    rstd = lax.rsqrt(_split_dot(x * x, ones, 3) * (1.0 / HEAD_DIM) + EPS)
    xhat = x * rstd
    return xhat * g, xhat, rstd


def _head_rms_bwd(dh, xhat, rstd, g, ones):
    dxh = dh * g
    mean = _split_dot(dxh * xhat, ones, 3) * (1.0 / HEAD_DIM)
    return rstd * (dxh - xhat * mean), jnp.sum(dh * xhat, axis=0, keepdims=True)


def _qkv_fwd(x, ln_q, ln_kv, g_q, g_k, w_q, w_kv, name):
    T, D = x.shape
    tm = min(TOKEN_TILE, T)
    nk = w_kv.shape[2]
    half = NDEV // 2

    def body(x_ref, lnq_ref, lnkv_ref, gq_ref, gk_ref, wq_ref, wkv_ref,
             q_ref, k_ref, v_ref, qpre_ref, kpre_ref, hq_ref, hkv_ref):
        xv = x_ref[...]
        _, xhat, _ = _rms(xv, lnq_ref[...])
        hq = (xhat * lnq_ref[...]).astype(MXU)
        hkv = (xhat * lnkv_ref[...]).astype(MXU)
        hq_ref[...] = hq
        hkv_ref[...] = hkv
        qpre_ref[...] = _mm(hq, wq_ref[...])
        for j in range(half):
            kpre_ref[:, j * nk:(j + 1) * nk] = _mm(hkv, wkv_ref[j])
            v_ref[:, j * nk:(j + 1) * nk] = _mm(hkv, wkv_ref[half + j]).astype(MXU)
        ones = _head_ones()
        for b in range(D // 128):
            cols = slice(b * 128, (b + 1) * 128)
            qn, _, _ = _head_rms(qpre_ref[:, cols], gq_ref[:, cols], ones)
            q_ref[:, cols] = (qn * SCALE).astype(MXU)
            kn, _, _ = _head_rms(kpre_ref[:, cols], gk_ref[:, cols], ones)
            k_ref[:, cols] = kn.astype(MXU)

    return _call(
        body, name=name, grid=(T // tm,),
        out_shape=[jax.ShapeDtypeStruct((T, D), MXU)] * 3 + [jax.ShapeDtypeStruct((T, D), F32)] * 2
        + [jax.ShapeDtypeStruct((T, D), MXU)] * 2,
        in_specs=[_tile(tm, D), _whole(ln_q.shape), _whole(ln_kv.shape), _whole(g_q.shape), _whole(g_k.shape),
                  _whole(w_q.shape), _whole(w_kv.shape)],
        out_specs=[_tile(tm, D)] * 7,
        compiler_params=_params(40),
    )(x, ln_q, ln_kv, g_q, g_k, w_q, w_kv)


SB_KEYS = 2 * QBLK


def _sb_consts():
    row = lax.broadcasted_iota(jnp.int32, (QBLK, QBLK), 0)
    col = lax.broadcasted_iota(jnp.int32, (QBLK, QBLK), 1)
    lane = lax.broadcasted_iota(jnp.int32, (QBLK, 128), 1)
    ones = jnp.ones((QBLK, QBLK), MXU)
    later = jnp.concatenate([(row > col).astype(MXU), ones], axis=1)
    later_eq = jnp.concatenate([(row >= col).astype(MXU), ones], axis=1)
    return later, later_eq, lane < HEAD_DIM


MASKED_LOG = -1e30


def _sb_window(i, w):
    upper = (i + 1) * QBLK - w * SB_KEYS
    start = pl.multiple_of(jnp.maximum(upper - SB_KEYS, 0), QBLK)
    key = lax.broadcasted_iota(jnp.int32, (2 * QBLK, SB_KEYS), 1) + start
    return start, key < upper


def _sb_diagonal():
    row = jnp.bitwise_and(lax.broadcasted_iota(jnp.int32, (2 * QBLK, SB_KEYS), 0), QBLK - 1)
    key = lax.broadcasted_iota(jnp.int32, (2 * QBLK, SB_KEYS), 1)
    cases = []
    for shift in (0, QBLK):
        seen = key < row + shift
        cases.append(jnp.stack([jnp.where(seen, 1.0, 0.0), jnp.where(seen, 0.0, MASKED_LOG)]))
    return jnp.stack(cases).astype(F32)


_SB_DIAG_SPEC = pl.BlockSpec((None, 2, 2 * QBLK, SB_KEYS), lambda h, i: (jnp.minimum(i, 1), 0, 0, 0))


def _sb_terms(x, terms):
    x = jnp.concatenate([x[:, :QBLK], x[:, QBLK:]], axis=0)
    out = []
    for _ in range(terms):
        part = x.astype(MXU)
        x = x - part.astype(F32)
        out.append(part)
    return tuple(out)


def _sb_suffix(parts, ones, carry):
    s = jnp.dot(jnp.concatenate(parts[:2], axis=1), jnp.concatenate([ones, ones], axis=0),
                preferred_element_type=F32)
    for part in parts[2:]:
        s = s + jnp.dot(part, ones, preferred_element_type=F32)
    rows = s.shape[0] // 2
    s_lo, sum_lo, s_hi, sum_hi = s[:rows, :QBLK], s[:rows, QBLK:], s[rows:, :QBLK], s[rows:, QBLK:]
    return jnp.concatenate([s_lo + (carry + sum_hi), s_hi + carry], axis=1), carry + (sum_lo + sum_hi)


def _sb_scores(z, mask):
    sp = _softplus(z)
    l, log_sig = -sp, z - sp
    if isinstance(mask, tuple):
        keep, bias = mask
        l, log_sig = l * keep, log_sig + bias
    else:
        l = jnp.where(mask, l, 0.0)
        log_sig = jnp.where(mask, log_sig, MASKED_LOG)
    return log_sig, _sb_terms(l, 2)


def _sb_weights(staged, later, c_l):
    log_sig, parts = staged
    b, c_l = _sb_suffix(parts, later, c_l)
    return jnp.exp(log_sig + b), c_l


DEAD_LOG = -88.0


def _sb_alive(carry):
    return (jnp.max(carry[0]) > DEAD_LOG).astype(jnp.int32)


def _ride_along(refs, n, scatter, rank=2):
    if n == 0:
        return lambda: None
    step, steps = 0, 1
    for d in range(rank):
        step = step * pl.num_programs(d) + pl.program_id(d)
        steps = steps * pl.num_programs(d)
    srcs, lands, sems = refs[:n], refs[n:2 * n], refs[2 * n:]

    @pl.when(step == 0)
    def _():
        _exchange_start(srcs, lands, sems, scatter)

    def finish():
        @pl.when(step == steps - 1)
        def _():
            _exchange_finish(srcs, lands, sems, scatter)

    return finish


def _sb_fwd(q, k, v, cargo, name):
    T, D = q.shape
    nc = len(cargo)
    lands, sems = _exchange_shapes(cargo, False)

    def body(diag_ref, q_ref, k_ref, v_ref, *rest):
        o_ref = rest[nc]
        finish = _ride_along(rest[:nc] + rest[nc + 1:], nc, False)
        i = pl.program_id(1)
        n_steps = (i + 2) // 2
        later, _, first = _sb_consts()
        qv = q_ref[...]
        zero = jnp.zeros_like(qv)
        q2 = jnp.concatenate([jnp.where(first, qv, zero), jnp.where(first, zero, qv)], axis=0)

        def window(w, carry, diagonal):
            start, mask = _sb_window(i, w)
            if diagonal:
                mask = (diag_ref[0], diag_ref[1])
            kw = k_ref[pl.ds(start, SB_KEYS), :]
            vw = v_ref[pl.ds(start, SB_KEYS), :]
            c_l, acc = carry
            a, c_l = _sb_weights(_sb_scores(_mm_nt(q2, kw), mask), later, c_l)
            return c_l, acc + _mm(a, vw)

        def step(state):
            w, _, carry = state
            carry = window(w, carry, False)
            return w + 1, _sb_alive(carry), carry

        carry = window(0, (jnp.zeros((2 * QBLK, 128), F32),) * 2, True)
        _, _, carry = lax.while_loop(lambda s: (s[0] < n_steps) & (s[1] > 0), step,
                                     (jnp.int32(1), _sb_alive(carry), carry))
        o_ref[...] = jnp.where(first, carry[1][:QBLK], carry[1][QBLK:])
        finish()

    qblk = pl.BlockSpec((QBLK, 128), lambda h, i: (i, h))
    kblk = pl.BlockSpec((T, 128), lambda h, i: (0, h))
    outs = _call(
        body, name=name, grid=(D // 128, T // QBLK), out_shape=[jax.ShapeDtypeStruct((T, D), F32)] + lands,
        in_specs=[_SB_DIAG_SPEC, qblk, kblk, kblk] + [ANY_SPEC] * nc, out_specs=[qblk] + [ANY_SPEC] * nc,
        scratch_shapes=sems, compiler_params=_params(32, 2),
    )(_sb_diagonal(), q, k, v, *cargo)
    return outs[0], outs[1:]


def _sb_bwd(q, k, v, o, do, cargo, name):
    T, D = q.shape
    nc = len(cargo)
    lands, sems = _exchange_shapes(cargo, True)

    def body(diag_ref, q_ref, k_ref, v_ref, o_ref, do_ref, *rest):
        dq_ref, dk_ref, dv_ref = rest[nc:nc + 3]
        finish = _ride_along(rest[:nc] + rest[nc + 3:], nc, True)
        i = pl.program_id(1)

        @pl.when(i == 0)
        def _():
            dk_ref[...] = jnp.zeros_like(dk_ref)
            dv_ref[...] = jnp.zeros_like(dv_ref)

        n_steps = (i + 2) // 2
        later, later_eq, first = _sb_consts()
        qv = q_ref[...]
        dob = do_ref[...].astype(MXU)
        zero = jnp.zeros_like(qv)
        q2 = jnp.concatenate([jnp.where(first, qv, zero), jnp.where(first, zero, qv)], axis=0)
        do2 = jnp.concatenate([jnp.where(first, dob, zero), jnp.where(first, zero, dob)], axis=0)
        prod = o_ref[...] * dob.astype(F32)
        prod2 = jnp.concatenate([jnp.where(first, prod, 0.0), jnp.where(first, 0.0, prod)], axis=0)
        total = _split_dot(prod2, jnp.ones((128, 128), MXU), 3)
        total = jnp.concatenate([total, total], axis=1)

        def window(w, carry, diagonal):
            start, mask = _sb_window(i, w)
            if diagonal:
                mask = (diag_ref[0], diag_ref[1])
            kw = k_ref[pl.ds(start, SB_KEYS), :]
            vw = v_ref[pl.ds(start, SB_KEYS), :]
            c_l, c_e, dq = carry
            log_sig, parts = _sb_scores(_mm_nt(q2, kw), mask)
            a, c_l = _sb_weights((log_sig, parts), later, c_l)
            ab = a.astype(MXU)
            e = ab.astype(F32) * _mm_nt(do2, vw)
            from_here, c_e = _sb_suffix(_sb_terms(e, 3), later_eq, c_e)
            sig = jnp.exp(log_sig)
            dzb = (e * (1.0 - sig) - sig * (total - from_here)).astype(MXU)
            dk_ref[pl.ds(start, SB_KEYS), :] += _mm_tn(dzb, q2)
            dv_ref[pl.ds(start, SB_KEYS), :] += _mm_tn(ab, do2)
            return c_l, c_e, dq + _mm(dzb, kw)

        def step(state):
            w, _, carry = state
            carry = window(w, carry, False)
            return w + 1, _sb_alive(carry), carry

        carry = window(0, (jnp.zeros((2 * QBLK, 128), F32),) * 3, True)
        _, _, carry = lax.while_loop(lambda s: (s[0] < n_steps) & (s[1] > 0), step,
                                     (jnp.int32(1), _sb_alive(carry), carry))
        dq_ref[...] = jnp.where(first, carry[2][:QBLK], carry[2][QBLK:]) * SCALE
        finish()

    qblk = pl.BlockSpec((QBLK, 128), lambda h, i: (i, h))
    kblk = pl.BlockSpec((T, 128), lambda h, i: (0, h))
    full = jax.ShapeDtypeStruct((T, D), F32)
    outs = _call(
        body, name=name, grid=(D // 128, T // QBLK), out_shape=[full, full, full] + lands,
        in_specs=[_SB_DIAG_SPEC, qblk, kblk, kblk, qblk, qblk] + [ANY_SPEC] * nc,
        out_specs=[qblk, kblk, kblk] + [ANY_SPEC] * nc, scratch_shapes=sems, compiler_params=_params(32, 2),
    )(_sb_diagonal(), q, k, v, o, do, *cargo)
    return outs[0], outs[1], outs[2], outs[3:]


def _proj_res(x, a, w, name):
    T, D = x.shape
    tm = min(TOKEN_TILE, T)

    def body(x_ref, a_ref, w_ref, o_ref):
        o_ref[...] = x_ref[...] + _mm(a_ref[...], w_ref[...])

    return _call(
        body, name=name, grid=(T // tm,), out_shape=jax.ShapeDtypeStruct((T, D), F32),
        in_specs=[_tile(tm, D), _tile(tm, a.shape[1]), _whole(w.shape)], out_specs=_tile(tm, D),
        compiler_params=_params(32),
    )(x, a, w)


def _proj_nt(g, w, name):
    T = g.shape[0]
    K = w.shape[0]
    tm = min(TOKEN_TILE, T)

    def body(g_ref, w_ref, o_ref):
        o_ref[...] = _mm_nt(g_ref[...], w_ref[...])

    return _call(
        body, name=name, grid=(T // tm,), out_shape=jax.ShapeDtypeStruct((T, K), F32),
        in_specs=[_tile(tm, g.shape[1]), _whole(w.shape)], out_specs=_tile(tm, K),
        compiler_params=_params(32),
    )(g, w)


def _loss_grad(y, tgt, name):
    T, D = y.shape
    tm = min(TOKEN_TILE, T)

    def body(y_ref, t_ref, dy_ref, loss_ref):
        @pl.when(pl.program_id(0) == 0)
        def _():
            loss_ref[...] = jnp.zeros_like(loss_ref)
        diff = y_ref[...] - t_ref[...]
        dy_ref[...] = diff * (1.0 / D)
        rows = jnp.sum(diff * diff, axis=1, keepdims=True) * (1.0 / D)
        loss_ref[...] += 0.5 * jnp.sum(rows, axis=0, keepdims=True)

    return _call(
        body, name=name, grid=(T // tm,),
        out_shape=[jax.ShapeDtypeStruct((T, D), F32), jax.ShapeDtypeStruct((1, 1), F32)],
        in_specs=[_tile(tm, D), _tile(tm, D)], out_specs=[_tile(tm, D), _acc((1, 1))],
        compiler_params=_params(32),
    )(y, tgt)


def _ple_bwd(dx, x, gate, pp, ln, w_g, name):
    T, D = x.shape
    tm = min(TOKEN_TILE, T)

    def body(dx_ref, x_ref, gate_ref, pp_ref, ln_ref, wg_ref, dxo_ref, dpp_ref, dgp_ref, dln_ref):
        @pl.when(pl.program_id(0) == 0)
        def _():
            dln_ref[...] = jnp.zeros_like(dln_ref)
        dxv = dx_ref[...]
        gate = gate_ref[...]
        _, xhat, rstd = _rms(x_ref[...], ln_ref[...])
        dpp_ref[...] = (dxv * gate).astype(MXU)
        dgp = (dxv * pp_ref[...] * gate * (1.0 - gate)).astype(MXU)
        dgp_ref[...] = dgp
        dxn, dln = _rms_bwd(_mm_nt(dgp, wg_ref[...]), xhat, rstd, ln_ref[...])
        dln_ref[...] += dln
        dxo_ref[...] = dxn + dxv

    return _call(
        body, name=name, grid=(T // tm,),
        out_shape=[jax.ShapeDtypeStruct((T, D), F32), jax.ShapeDtypeStruct((T, D), MXU),
                   jax.ShapeDtypeStruct((T, D), MXU), jax.ShapeDtypeStruct(ln.shape, F32)],
        in_specs=[_tile(tm, D)] * 4 + [_whole(ln.shape), _whole(w_g.shape)],
        out_specs=[_tile(tm, D), _tile(tm, D), _tile(tm, D), _acc(ln.shape)],
        compiler_params=_params(32),
    )(dx, x, gate, pp, ln, w_g)


def _mlp_bwd(dx, x, pre, ln, w_up, w_down, name):
    T, D = x.shape
    tm = min(TOKEN_TILE, T)
    nf = w_up.shape[2]
    F = nf * NDEV

    def body(dx_ref, x_ref, pre_ref, ln_ref, wup_ref, wdown_ref, dxo_ref, dpre_ref, s_ref, dln_ref):
        @pl.when(pl.program_id(0) == 0)
        def _():
            dln_ref[...] = jnp.zeros_like(dln_ref)
        dxv = dx_ref[...]
        _, xhat, rstd = _rms(x_ref[...], ln_ref[...])
        a = jnp.maximum(pre_ref[...], 0.0)
        s_ref[...] = (a * a).astype(MXU)
        dpre_ref[...] = (_mm_nt(dxv, wdown_ref[...]) * (2.0 * a)).astype(MXU)
        dh = _mm_nt(dpre_ref[:, :nf], wup_ref[0])
        for j in range(1, NDEV):
            dh += _mm_nt(dpre_ref[:, j * nf:(j + 1) * nf], wup_ref[j])
        dxn, dln = _rms_bwd(dh, xhat, rstd, ln_ref[...])
        dln_ref[...] += dln
        dxo_ref[...] = dxn + dxv

    return _call(
        body, name=name, grid=(T // tm,),
        out_shape=[jax.ShapeDtypeStruct((T, D), F32), jax.ShapeDtypeStruct((T, F), MXU),
                   jax.ShapeDtypeStruct((T, F), MXU), jax.ShapeDtypeStruct(ln.shape, F32)],
        in_specs=[_tile(tm, D), _tile(tm, D), _tile(tm, F), _whole(ln.shape), _whole(w_up.shape),
                  _whole(w_down.shape)],
        out_specs=[_tile(tm, D), _tile(tm, F), _tile(tm, F), _acc(ln.shape)],
        compiler_params=_params(56),
    )(dx, x, pre, ln, w_up, w_down)


def _qkv_bwd(dx, x, dq, dk, dv, q_pre, k_pre, ln_q, ln_kv, g_q, g_k, w_q, w_kv, cargo, name):
    T, D = x.shape
    tm = min(TOKEN_TILE, T)
    nk = w_kv.shape[2]
    n_tiles = T // tm
    nc = len(cargo)
    lands, sems = _exchange_shapes(cargo, True)

    def body(dx_ref, x_ref, dq_ref, dk_ref, dv_ref, qpre_ref, kpre_ref, lnq_ref, lnkv_ref, gq_ref, gk_ref,
             wq_ref, wkv_ref, *rest):
        dxo_ref, dqp_ref, dkv_ref, dlnq_ref, dlnkv_ref, dgq_ref, dgk_ref = rest[nc:nc + 7]
        gq_acc, gk_acc = rest[2 * nc + 7:2 * nc + 9]
        finish = _ride_along(rest[:nc] + rest[nc + 7:2 * nc + 7] + rest[2 * nc + 9:], nc, True, rank=1)
        i = pl.program_id(0)

        @pl.when(i == 0)
        def _():
            dlnq_ref[...] = jnp.zeros_like(dlnq_ref)
            dlnkv_ref[...] = jnp.zeros_like(dlnkv_ref)
            gq_acc[...] = jnp.zeros_like(gq_acc)
            gk_acc[...] = jnp.zeros_like(gk_acc)

        ones = _head_ones()
        for b in range(D // 128):
            cols = slice(b * 128, (b + 1) * 128)
            _, xh, rs = _head_rms(qpre_ref[:, cols], gq_ref[:, cols], ones)
            d, dg = _head_rms_bwd(dq_ref[:, cols], xh, rs, gq_ref[:, cols], ones)
            dqp_ref[:, cols] = d.astype(MXU)
            gq_acc[:, cols] += dg
            _, xh, rs = _head_rms(kpre_ref[:, cols], gk_ref[:, cols], ones)
            d, dg = _head_rms_bwd(dk_ref[:, cols], xh, rs, gk_ref[:, cols], ones)
            dkv_ref[:, cols] = d.astype(MXU)
            gk_acc[:, cols] += dg
        dkv_ref[:, D:] = dv_ref[...].astype(MXU)

        _, xhat, rstd = _rms(x_ref[...], lnq_ref[...])
        dhq = _mm_nt(dqp_ref[...], wq_ref[...])
        dhkv = _mm_nt(dkv_ref[:, :nk], wkv_ref[0])
        for j in range(1, NDEV):
            dhkv += _mm_nt(dkv_ref[:, j * nk:(j + 1) * nk], wkv_ref[j])
        dxq, dlnq = _rms_bwd(dhq, xhat, rstd, lnq_ref[...])
        dxkv, dlnkv = _rms_bwd(dhkv, xhat, rstd, lnkv_ref[...])
        dlnq_ref[...] += dlnq
        dlnkv_ref[...] += dlnkv
        dxo_ref[...] = dx_ref[...] + dxq + dxkv

        @pl.when(i == n_tiles - 1)
        def _():
            row = lax.broadcasted_iota(jnp.int32, (D, 128), 0)
            col = lax.broadcasted_iota(jnp.int32, (D, 128), 1)
            fold = (jnp.bitwise_and(row, HEAD_DIM - 1) == col).astype(MXU)
            dgq_ref[...] = _split_dot(jnp.broadcast_to(gq_acc[...], (8, D)), fold, 3)
            dgk_ref[...] = _split_dot(jnp.broadcast_to(gk_acc[...], (8, D)), fold, 3)

        finish()

    small = jax.ShapeDtypeStruct((8, 128), F32)
    outs = _call(
        body, name=name, grid=(n_tiles,),
        out_shape=[jax.ShapeDtypeStruct((T, D), F32), jax.ShapeDtypeStruct((T, D), MXU),
                   jax.ShapeDtypeStruct((T, 2 * D), MXU), jax.ShapeDtypeStruct(ln_q.shape, F32),
                   jax.ShapeDtypeStruct(ln_kv.shape, F32), small, small] + lands,
        in_specs=[_tile(tm, D)] * 7 + [_whole(ln_q.shape), _whole(ln_kv.shape), _whole(g_q.shape),
                                       _whole(g_k.shape), _whole(w_q.shape), _whole(w_kv.shape)] + [ANY_SPEC] * nc,
        out_specs=[_tile(tm, D), _tile(tm, D), _tile(tm, 2 * D), _acc(ln_q.shape), _acc(ln_kv.shape),
                   _acc((8, 128)), _acc((8, 128))] + [ANY_SPEC] * nc,
        scratch_shapes=[pltpu.VMEM((1, D), F32), pltpu.VMEM((1, D), F32)] + sems,
        compiler_params=_params(48),
    )(dx, x, dq, dk, dv, q_pre, k_pre, ln_q, ln_kv, g_q, g_k, w_q, w_kv, *cargo)
    return outs[:7], outs[7:]


def _sgu_bwd(dx, x, z, ln, w_in, g_v, ws, wsT, bT, w_out, name):
    T, D = x.shape
    tm = min(TOKEN_TILE, T)
    nw = w_in.shape[2]

    def body(dx_ref, x_ref, z_ref, ln_ref, win_ref, gv_ref, ws_ref, wsT_ref, bT_ref, wout_ref,
             dxo_ref, dz_ref, dws_ref, dbT_ref, dln_ref, dgv_ref, mix_ref, dvn_ref):
        @pl.when(pl.program_id(0) == 0)
        def _():
            dws_ref[...] = jnp.zeros_like(dws_ref)
            dbT_ref[...] = jnp.zeros_like(dbT_ref)
            dln_ref[...] = jnp.zeros_like(dln_ref)
            dgv_ref[...] = jnp.zeros_like(dgv_ref)
        dxv = dx_ref[...]
        _, xhat, rstd = _rms(x_ref[...], ln_ref[...])
        u, du = _gelu_and_grad(z_ref[:, :D])
        gv, dgv = _gelu_and_grad(z_ref[:, D:])
        vn, vhat, rstd_v = _rms(gv, gv_ref[...])
        vnb = vn.astype(MXU)
        _spatial_mix(vnb, ws_ref, bT_ref, mix_ref, tm)
        dy = _mm_nt(dxv, wout_ref[...])
        d_u = dy * mix_ref[...]
        d_mix = dy * u
        dmb = d_mix.astype(MXU)
        tri, triT = _tril_mask()
        for g in range(GROUPS):
            wmT = jnp.where(triT, wsT_ref[g], 0.0).astype(MXU)
            cols = slice(g * CHUNK, (g + 1) * CHUNK)
            for ch in range(tm // CHUNK):
                rows = slice(ch * CHUNK, (ch + 1) * CHUNK)
                dm = dmb[rows, cols]
                dws_ref[g] += jnp.where(tri, _mm_nt(dm, vnb[rows, cols]), 0.0)
                dbT_ref[:, g:g + 1] += jnp.sum(d_mix[rows, cols], axis=1, keepdims=True)
                dvn_ref[rows, cols] = _mm(wmT, dm)
        d_gv, dg = _rms_bwd(dvn_ref[...], vhat, rstd_v, gv_ref[...])
        dgv_ref[...] += dg
        dz_ref[:, :D] = (d_u * du).astype(MXU)
        dz_ref[:, D:] = (d_gv * dgv).astype(MXU)
        dh = _mm_nt(dz_ref[:, :nw], win_ref[0])
        for j in range(1, NDEV):
            dh += _mm_nt(dz_ref[:, j * nw:(j + 1) * nw], win_ref[j])
        dxn, dln = _rms_bwd(dh, xhat, rstd, ln_ref[...])
        dln_ref[...] += dln
        dxo_ref[...] = dxn + dxv

    return _call(
        body, name=name, grid=(T // tm,),
        out_shape=[jax.ShapeDtypeStruct((T, D), F32), jax.ShapeDtypeStruct((T, 2 * D), MXU),
                   jax.ShapeDtypeStruct(ws.shape, F32), jax.ShapeDtypeStruct(bT.shape, F32),
                   jax.ShapeDtypeStruct(ln.shape, F32), jax.ShapeDtypeStruct(g_v.shape, F32)],
        in_specs=[_tile(tm, D), _tile(tm, D), _tile(tm, 2 * D), _whole(ln.shape), _whole(w_in.shape),
                  _whole(g_v.shape), _whole(ws.shape), _whole(wsT.shape), _whole(bT.shape), _whole(w_out.shape)],
        out_specs=[_tile(tm, D), _tile(tm, 2 * D), _acc(ws.shape), _acc(bT.shape), _acc(ln.shape),
                   _acc(g_v.shape)],
        scratch_shapes=[pltpu.VMEM((tm, D), F32), pltpu.VMEM((tm, D), F32)],
        compiler_params=_params(48),
    )(dx, x, z, ln, w_in, g_v, ws, wsT, bT, w_out)


def _wgrad_rows(a, g, name):
    T, K = a.shape
    N = g.shape[1]
    kb = K // NDEV

    def body(a_ref, g_ref, o_ref):
        o_ref[...] = _mm_tn(a_ref[...], g_ref[...]).astype(COMM)

    return _call(
        body, name=name, grid=(NDEV,), out_shape=jax.ShapeDtypeStruct((K, N), COMM),
        in_specs=[pl.BlockSpec((T, kb), lambda j: (0, j)), _whole(g.shape)],
        out_specs=pl.BlockSpec((kb, N), lambda j: (j, 0)),
        compiler_params=_params(40),
    )(a, g).reshape(NDEV, kb, N)


def _wgrad_cols(a, g, name):
    T, K = a.shape
    N = g.shape[1]
    nb = N // NDEV

    def body(a_ref, g_ref, o_ref):
        o_ref[...] = _mm_tn(a_ref[...], g_ref[...]).astype(COMM)

    return _call(
        body, name=name, grid=(NDEV,), out_shape=jax.ShapeDtypeStruct((NDEV, K, nb), COMM),
        in_specs=[_whole(a.shape), pl.BlockSpec((T, nb), lambda j: (0, j))],
        out_specs=pl.BlockSpec((None, K, nb), lambda j: (j, 0, 0)),
        compiler_params=_params(40),
    )(a, g)


def _adamw(w, m, v, slots, name):
    R, C = w.shape
    n = slots.shape[0]
    tr = math.gcd(R, max(8, (128 * 1024) // C))
    if tr < 64:
        tr = R
    bc1 = 1.0 - ADAM_B1 ** ADAM_STEP
    bc2 = 1.0 - ADAM_B2 ** ADAM_STEP

    def body(w_ref, m_ref, v_ref, s_ref, g_ref, d_ref, mo_ref, vo_ref):
        g = s_ref[0].astype(F32)
        for j in range(1, n):
            g = g + s_ref[j].astype(F32)
        mn = ADAM_B1 * m_ref[...] + (1.0 - ADAM_B1) * g
        vn = ADAM_B2 * v_ref[...] + (1.0 - ADAM_B2) * (g * g)
        g_ref[...] = g
        mo_ref[...] = mn
        vo_ref[...] = vn
        d_ref[...] = -ADAM_LR * ((mn / bc1) / (jnp.sqrt(vn / bc2) + ADAM_EPS) + ADAM_WD * w_ref[...])

    blk = pl.BlockSpec((tr, C), lambda i: (i, 0))
    out = jax.ShapeDtypeStruct((R, C), F32)
    return _call(
        body, name=name, grid=(R // tr,), out_shape=[out, out, out, out],
        in_specs=[blk, blk, blk, pl.BlockSpec((n, tr, C), lambda i: (0, i, 0))], out_specs=[blk] * 4,
        compiler_params=_params(32),
    )(w, m, v, slots)


def _rows128(a):
    flat = a.reshape(-1)
    rows = -(-flat.shape[0] // 1024) * 8
    flat = jnp.pad(flat, (0, rows * 128 - flat.shape[0]))
    return flat.reshape(rows, 128)


def kernel(x, p, ln_mix_a, w_in_a, g_v_a, w_spatial, b_spatial, w_out_a, ln_kv, w_kv, g_k, ln_mix_b, w_q, g_q, w_out_b, ln_mlp, w_up, w_down, ln_ple, w_ple_gate, w_ple_proj, loss_target, m_ln_mix_a, m_w_in_a, m_g_v_a, m_w_spatial, m_b_spatial, m_w_out_a, m_ln_kv, m_w_kv, m_g_k, m_ln_mix_b, m_w_q, m_g_q, m_w_out_b, m_ln_mlp, m_w_up, m_w_down, m_ln_ple, m_w_ple_gate, m_w_ple_proj, v_ln_mix_a, v_w_in_a, v_g_v_a, v_w_spatial, v_b_spatial, v_w_out_a, v_ln_kv, v_w_kv, v_g_k, v_ln_mix_b, v_w_q, v_g_q, v_w_out_b, v_ln_mlp, v_w_up, v_w_down, v_ln_ple, v_w_ple_gate, v_w_ple_proj):
    me = 4 * lax.axis_index("x") + 2 * lax.axis_index("y") + lax.axis_index("c")
    D = x.shape[2]
    x0, tgt = x[0], loss_target[0]
    n_layers = w_up.shape[0]

    c = lambda w: w.astype(COMM)
    first = [c(w_in_a[0]), c(w_out_a[0]), ln_mix_a, g_v_a, c(w_up[0]), c(w_down[0]), c(w_ple_gate[0]),
             c(w_ple_proj[0]), c(w_q[0]), c(w_kv)]
    second_small = [c(w_out_b[0]), c(w_ple_gate[1]), c(w_ple_proj[1])]
    second_big = [c(w_up[1]), c(w_down[1])]
    W_in, W_out_a, ln_a, gv_a, W_up0, W_down0, W_g0, W_pp0, W_q, W_kv = _gather_two_level(first, "gather_first")
    W_out_a, ln_a, gv_a = W_out_a.reshape(-1, D), ln_a.reshape(1, D), gv_a.reshape(1, D)
    W_down0, W_g0, W_q = W_down0.reshape(-1, D), W_g0.reshape(-1, D), W_q.reshape(-1, D)
    ws = w_spatial[0]
    wsT = jnp.swapaxes(ws, 1, 2)
    bT = b_spatial[0].T
    ln_kv2, ln_b = ln_kv.reshape(1, D), ln_mix_b
    gk2 = jnp.tile(g_k.reshape(1, HEAD_DIM), (1, D // HEAD_DIM))
    gq2 = jnp.tile(g_q, (1, D // HEAD_DIM))
    ln_m = [ln_mlp[l:l + 1] for l in range(n_layers)]
    ln_p = [ln_ple[l:l + 1] for l in range(n_layers)]

    x1, z, h_a, y_a = _sgu_fwd(x0, ln_a, W_in, gv_a, ws, bT, W_out_a, "sgu_fwd")
    x2, pre0, hm0, (W_out_b, W_g1, W_pp1) = _mlp_fwd(x1, ln_m[0], W_up0, W_down0, second_small, "mlp_fwd0")
    x3, gate0, pp0, hp0 = _ple_fwd(x2, p[0, 0], ln_p[0], W_g0, W_pp0, "ple_fwd0")
    qn, kn, vn, q_pre, k_pre, h_q, h_kv = _qkv_fwd(x3, ln_b, ln_kv2, gq2, gk2, W_q, W_kv, "qkv_fwd")
    o2d, (W_up1, W_down1) = _sb_fwd(qn, kn, vn, second_big, "sb_fwd")
    W_out_b, W_down1, W_g1 = W_out_b.reshape(-1, D), W_down1.reshape(-1, D), W_g1.reshape(-1, D)
    x4 = _proj_res(x3, o2d, W_out_b, "attn_out")
    x5, pre1, hm1, _ = _mlp_fwd(x4, ln_m[1], W_up1, W_down1, [], "mlp_fwd1")
    x6, gate1, pp1, hp1 = _ple_fwd(x5, p[1, 0], ln_p[1], W_g1, W_pp1, "ple_fwd1")
    dy, loss_part = _loss_grad(x6, tgt, "loss_grad")
    loss = lax.psum(loss_part[0, 0], ("x", "y", "c"))

    dx5, dpp1, dgp1, dlnp1 = _ple_bwd(dy, x5, gate1, pp1, ln_p[1], W_g1, "ple_bwd1")
    dx4, dpre1, s1, dlnm1 = _mlp_bwd(dx5, x4, pre1, ln_m[1], W_up1, W_down1, "mlp_bwd1")
    wg_big = [_wgrad_cols(hm1, dpre1, "wg_up1"), _wgrad_rows(s1, dx5, "wg_down1")]
    wg_small = [_wgrad_rows(hp1, dgp1, "wg_gate1"), _wgrad_cols(p[1, 0].astype(MXU), dpp1, "wg_proj1"),
                _wgrad_rows(o2d, dx4, "wg_out_b")]
    do2d = _proj_nt(dx4, W_out_b, "attn_out_bwd")
    dqn, dkn, dvn, (s_up1, s_down1) = _sb_bwd(qn, kn, vn, o2d, do2d, wg_big, "sb_bwd")
    (dx3, dq_pre, dkv, dlnb, dlnkv, dgq, dgk), (s_gate1, s_proj1, s_out_b) = _qkv_bwd(
        dx4, x3, dqn, dkn, dvn, q_pre, k_pre, ln_b, ln_kv2, gq2, gk2, W_q, W_kv, wg_small, "qkv_bwd")
    dgq, dgk = dgq[:1, :HEAD_DIM], dgk[:1, :HEAD_DIM]
    dx2, dpp0, dgp0, dlnp0 = _ple_bwd(dx3, x2, gate0, pp0, ln_p[0], W_g0, "ple_bwd0")
    dx1, dpre0, s0, dlnm0 = _mlp_bwd(dx2, x1, pre0, ln_m[0], W_up0, W_down0, "mlp_bwd0")
    dx0, dz, dws, dbT, dlna, dgva = _sgu_bwd(dx1, x0, z, ln_a, W_in, gv_a, ws, wsT, bT, W_out_a, "sgu_bwd")
    wg_first = [_wgrad_rows(h_q, dq_pre, "wg_q"), _wgrad_cols(h_kv, dkv, "wg_kv"),
                _wgrad_cols(hm0, dpre0, "wg_up0"), _wgrad_rows(s0, dx2, "wg_down0"), _wgrad_rows(hp0, dgp0, "wg_gate0"),
                _wgrad_cols(p[0, 0].astype(MXU), dpp0, "wg_proj0"), _wgrad_cols(h_a, dz, "wg_in_a"),
                _wgrad_rows(y_a, dx1, "wg_out_a")]

    small = [("w_spatial", dws[None], w_spatial, m_w_spatial, v_w_spatial),
             ("b_spatial", dbT.T[None], b_spatial, m_b_spatial, v_b_spatial),
             ("ln_kv", dlnkv.reshape(-1), ln_kv, m_ln_kv, v_ln_kv),
             ("g_k", dgk.reshape(-1), g_k, m_g_k, v_g_k),
             ("ln_mix_b", dlnb, ln_mix_b, m_ln_mix_b, v_ln_mix_b),
             ("g_q", dgq, g_q, m_g_q, v_g_q),
             ("ln_mlp", jnp.concatenate([dlnm0, dlnm1]), ln_mlp, m_ln_mlp, v_ln_mlp),
             ("ln_ple", jnp.concatenate([dlnp0, dlnp1]), ln_ple, m_ln_ple, v_ln_ple)]
    sharded_vec = [("ln_mix_a", dlna, ln_mix_a, m_ln_mix_a, v_ln_mix_a),
                   ("g_v_a", dgva, g_v_a, m_g_v_a, v_g_v_a)]
    packs = [[], [], [], []]
    for _, g, w, m, v in small:
        for lst, a in zip(packs, (g, w, m, v)):
            lst.append(_rows128(a))
    for _, g, w, m, v in sharded_vec:
        packs[0].append(g.reshape(NDEV, -1))
        for lst, a in zip(packs[1:], (w, m, v)):
            lst.append(jnp.broadcast_to(a, (NDEV, a.shape[1])))
    g_pack, w_pack, m_pack, v_pack = (jnp.concatenate(lst) for lst in packs)
    g_pack8 = jnp.broadcast_to(g_pack[None], (NDEV,) + g_pack.shape)
    by_chip = [a.reshape((4, 2) + a.shape[1:]) for a in wg_first]
    from_sibling, (g_all,) = _scatter_pair(by_chip, [g_pack8], "scatter_pair")
    my_core = lax.axis_index("c")
    chip_sums = [_pair_sum(lax.dynamic_index_in_dim(a, my_core, 1, keepdims=False), o, f"pair_sum{j}")
                 for j, (a, o) in enumerate(zip(by_chip, from_sibling))]
    s_q, s_kv, s_up0, s_down0, s_gate0, s_proj0, s_in_a, s_out_a = _scatter_chips(chip_sums, "scatter_chips")

    def upd(w, m, v, s, name):
        shape = w.shape
        outs = _adamw(w.reshape(-1, shape[-1]), m.reshape(-1, shape[-1]), v.reshape(-1, shape[-1]), s, name)
        return [o.reshape(shape) for o in outs]

    res = {}
    per = {}
    per["w_up", 1] = upd(w_up[1], m_w_up[1], v_w_up[1], s_up1, "adam_up1")
    per["w_down", 1] = upd(w_down[1], m_w_down[1], v_w_down[1], s_down1, "adam_down1")
    per["w_ple_gate", 1] = upd(w_ple_gate[1], m_w_ple_gate[1], v_w_ple_gate[1], s_gate1, "adam_gate1")
    per["w_ple_proj", 1] = upd(w_ple_proj[1], m_w_ple_proj[1], v_w_ple_proj[1], s_proj1, "adam_proj1")
    res["w_out_b"] = upd(w_out_b, m_w_out_b, v_w_out_b, s_out_b, "adam_out_b")
    res["w_q"] = upd(w_q, m_w_q, v_w_q, s_q, "adam_q")
    res["w_kv"] = upd(w_kv, m_w_kv, v_w_kv, s_kv, "adam_kv")
    per["w_up", 0] = upd(w_up[0], m_w_up[0], v_w_up[0], s_up0, "adam_up0")
    per["w_down", 0] = upd(w_down[0], m_w_down[0], v_w_down[0], s_down0, "adam_down0")
    per["w_ple_gate", 0] = upd(w_ple_gate[0], m_w_ple_gate[0], v_w_ple_gate[0], s_gate0, "adam_gate0")
    per["w_ple_proj", 0] = upd(w_ple_proj[0], m_w_ple_proj[0], v_w_ple_proj[0], s_proj0, "adam_proj0")
    res["w_in_a"] = upd(w_in_a, m_w_in_a, v_w_in_a, s_in_a, "adam_in_a")
    res["w_out_a"] = upd(w_out_a, m_w_out_a, v_w_out_a, s_out_a, "adam_out_a")
    for nm in ("w_up", "w_down", "w_ple_gate", "w_ple_proj"):
        res[nm] = [jnp.stack([per[nm, l][t] for l in range(n_layers)]) for t in range(4)]

    outs = _adamw(w_pack, m_pack, v_pack, g_all, "adam_small")
    row = 0
    for nm, g, w, m, v in small:
        nrows = _rows128(w).shape[0]
        res[nm] = [o[row:row + nrows].reshape(-1)[:w.size].reshape(w.shape) for o in outs]
        row += nrows
    for nm, g, w, m, v in sharded_vec:
        res[nm] = [lax.dynamic_slice_in_dim(o[row:row + NDEV], me, 1, axis=0) for o in outs]
        row += NDEV

    names = ["ln_mix_a", "w_in_a", "g_v_a", "w_spatial", "b_spatial", "w_out_a", "ln_kv", "w_kv", "g_k", "ln_mix_b",
             "w_q", "g_q", "w_out_b", "ln_mlp", "w_up", "w_down", "ln_ple", "w_ple_gate", "w_ple_proj"]
    out = [loss, dx0[None]]
    for t in range(4):
        out += [res[nm][t] for nm in names]
    return tuple(out)
```

```python
import functools
import math

import jax
import jax.numpy as jnp
from jax import lax
from jax.experimental import pallas as pl
from jax.experimental.pallas import tpu as pltpu

F32 = jnp.float32
MXU = jnp.bfloat16
COMM = jnp.bfloat16
EPS = 1e-6
NDEV = 8
HEAD_DIM = 64
CHUNK = 128
GROUPS = 8
QBLK = 128
SCALE = HEAD_DIM ** -0.5
TOKEN_TILE = 256
ADAM_LR = 0.001
ADAM_B1 = 0.9
ADAM_B2 = 0.999
ADAM_EPS = 1e-08
ADAM_WD = 0.01
ADAM_STEP = 10
MESH = pl.DeviceIdType.MESH


def _call(body, **kw):
    return pl.pallas_call(body, **kw)


def _params(vmem_mb, n_axes=1):
    return pltpu.CompilerParams(dimension_semantics=("arbitrary",) * n_axes,
                                vmem_limit_bytes=vmem_mb << 20)


def _tile(tm, n):
    return pl.BlockSpec((tm, n), lambda i: (i, 0))


def _whole(shape):
    zeros = (0,) * len(shape)
    return pl.BlockSpec(shape, lambda i: zeros, pipeline_mode=pl.Buffered(1))


def _acc(shape):
    zeros = (0,) * len(shape)
    return pl.BlockSpec(shape, lambda i: zeros)


def _mm(a, b):
    return jnp.dot(a.astype(MXU), b.astype(MXU), preferred_element_type=F32)


def _mm_nt(a, b):
    return lax.dot_general(a.astype(MXU), b.astype(MXU), (((1,), (1,)), ((), ())),
                           preferred_element_type=F32)


def _mm_tn(a, b):
    return lax.dot_general(a.astype(MXU), b.astype(MXU), (((0,), (0,)), ((), ())),
                           preferred_element_type=F32)


def _split_dot(x, ones, terms=2):
    out = None
    for _ in range(terms):
        part = x.astype(MXU)
        x = x - part.astype(F32)
        d = jnp.dot(part, ones, preferred_element_type=F32)
        out = d if out is None else out + d
    return out


def _rms(x, g):
    rstd = lax.rsqrt(jnp.mean(x * x, axis=-1, keepdims=True) + EPS)
    xhat = x * rstd
    return xhat * g, xhat, rstd


def _rms_bwd(dh, xhat, rstd, g):
    dxh = dh * g
    dx = rstd * (dxh - xhat * jnp.mean(dxh * xhat, axis=-1, keepdims=True))
    dg = jnp.sum(dh * xhat, axis=0, keepdims=True)
    return dx, dg


_GELU_C = math.sqrt(2.0 / math.pi)


def _gelu(x):
    t = jnp.tanh(_GELU_C * (x + 0.044715 * (x * x * x)))
    return 0.5 * x * (1.0 + t)


def _gelu_and_grad(x):
    x2 = x * x
    t = jnp.tanh(_GELU_C * (x + 0.044715 * (x2 * x)))
    g = 0.5 * x * (1.0 + t)
    dg = 0.5 * (1.0 + t) + 0.5 * x * (1.0 - t * t) * (_GELU_C * (1.0 + 3.0 * 0.044715 * x2))
    return g, dg


def _softplus(z):
    return jnp.maximum(z, 0.0) + jnp.log(1.0 + jnp.exp(-jnp.abs(z)))


def _tril_mask():
    row = lax.broadcasted_iota(jnp.int32, (CHUNK, CHUNK), 0)
    col = lax.broadcasted_iota(jnp.int32, (CHUNK, CHUNK), 1)
    return row >= col, row <= col


ANY_SPEC = pl.BlockSpec(memory_space=pl.ANY)


def _my_index():
    return 4 * lax.axis_index("x") + 2 * lax.axis_index("y") + lax.axis_index("c")


def _exchange_copies(srcs, lands, send_sems, recv_sems, scatter, arriving):
    x, y, c = lax.axis_index("x"), lax.axis_index("y"), lax.axis_index("c")
    me = 4 * x + 2 * y + c
    out = []
    for a in range(len(srcs)):
        for k in range(NDEV - 1):
            bits = k + 1
            px = 1 - x if (bits >> 2) & 1 else x
            py = 1 - y if (bits >> 1) & 1 else y
            pc = 1 - c if bits & 1 else c
            peer = 4 * px + 2 * py + pc
            src = srcs[a].at[peer] if scatter else srcs[a]
            out.append(pltpu.make_async_remote_copy(
                src_ref=src, dst_ref=lands[a].at[peer if arriving else me],
                send_sem=send_sems.at[a * (NDEV - 1) + k], recv_sem=recv_sems.at[a * (NDEV - 1) + k],
                device_id=(px, py, pc), device_id_type=MESH))
    return out


def _exchange_shapes(arrs, scatter):
    n = len(arrs)
    if n == 0:
        return [], []
    lands = [jax.ShapeDtypeStruct(a.shape if scatter else (NDEV,) + a.shape, a.dtype) for a in arrs]
    sems = [pltpu.SemaphoreType.DMA((n * (NDEV - 1),)), pltpu.SemaphoreType.DMA((n * (NDEV - 1),)),
            pltpu.SemaphoreType.DMA((n,))]
    return lands, sems


def _exchange_start(srcs, lands, sems, scatter):
    send_sems, recv_sems, local_sems = sems
    me = _my_index()
    for a in range(len(srcs)):
        pltpu.make_async_copy(srcs[a].at[me] if scatter else srcs[a], lands[a].at[me], local_sems.at[a]).start()
    for send in _exchange_copies(srcs, lands, send_sems, recv_sems, scatter, False):
        send.start()


def _exchange_finish(srcs, lands, sems, scatter):
    send_sems, recv_sems, local_sems = sems
    me = _my_index()
    for arrive in _exchange_copies(srcs, lands, send_sems, recv_sems, scatter, True):
        arrive.wait_recv()
    for send in _exchange_copies(srcs, lands, send_sems, recv_sems, scatter, False):
        send.wait_send()
    for a in range(len(srcs)):
        pltpu.make_async_copy(srcs[a].at[me] if scatter else srcs[a], lands[a].at[me], local_sems.at[a]).wait()


def _gather_two_level(arrs, name):
    n = len(arrs)
    lands = [jax.ShapeDtypeStruct((NDEV,) + a.shape, a.dtype) for a in arrs]

    def body(*refs):
        srcs, outs = refs[:n], refs[n:2 * n]
        send_sems, recv_sems, local_sems = refs[2 * n:]
        x, y, c = lax.axis_index("x"), lax.axis_index("y"), lax.axis_index("c")
        me, sibling = (x, y, c), (x, y, 1 - c)
        chips = [(1 - x, y), (x, 1 - y), (1 - x, 1 - y)]

        def index(dev):
            return 4 * dev[0] + 2 * dev[1] + dev[2]

        def copy(a, k, block, to, src=None):
            dst = outs[a].at[index(block)]
            return pltpu.make_async_remote_copy(
                src_ref=dst if src is None else src, dst_ref=dst, send_sem=send_sems.at[a, k],
                recv_sem=recv_sems.at[a, k], device_id=to, device_id_type=MESH)

        mine, first, passed = [], [], []
        for a in range(n):
            cp = pltpu.make_async_copy(srcs[a], outs[a].at[index(me)], local_sems.at[a])
            cp.start()
            mine.append(cp)
            first.append(copy(a, 0, me, sibling, src=srcs[a]))
            first += [copy(a, 1 + j, me, (*chip, c), src=srcs[a]) for j, chip in enumerate(chips)]
        for cp in first:
            cp.start()
        for a in range(n):
            for j, chip in enumerate(chips):
                copy(a, 1 + j, (*chip, c), me).wait_recv()
                cp = copy(a, 4 + j, (*chip, c), sibling)
                cp.start()
                passed.append(cp)
        for a in range(n):
            copy(a, 0, sibling, me).wait_recv()
            for j, chip in enumerate(chips):
                copy(a, 4 + j, (*chip, 1 - c), me).wait_recv()
        for cp in first + passed:
            cp.wait_send()
        for cp in mine:
            cp.wait()

    return _call(body, name=name, out_shape=lands, in_specs=[ANY_SPEC] * n, out_specs=[ANY_SPEC] * n,
                 scratch_shapes=[pltpu.SemaphoreType.DMA((n, NDEV - 1)), pltpu.SemaphoreType.DMA((n, NDEV - 1)),
                                 pltpu.SemaphoreType.DMA((n,))])(*arrs)


def _scatter_pair(arrs, extra, name):
    n, ne = len(arrs), len(extra)
    lands = [jax.ShapeDtypeStruct((4,) + a.shape[2:], a.dtype) for a in arrs]
    extra_lands, extra_sems = _exchange_shapes(extra, True)

    def body(*refs):
        srcs, xsrc = refs[:n], refs[n:n + ne]
        outs, xout = refs[n + ne:2 * n + ne], refs[2 * n + ne:2 * (n + ne)]
        send_sems, recv_sems = refs[2 * (n + ne)], refs[2 * (n + ne) + 1]
        xsems = refs[2 * (n + ne) + 2:]
        x, y, c = lax.axis_index("x"), lax.axis_index("y"), lax.axis_index("c")
        _exchange_start(xsrc, xout, xsems, True)
        copies = [pltpu.make_async_remote_copy(
            src_ref=srcs[a].at[k, 1 - c], dst_ref=outs[a].at[k], send_sem=send_sems.at[a, k],
            recv_sem=recv_sems.at[a, k], device_id=(x, y, 1 - c), device_id_type=MESH)
            for a in range(n) for k in range(4)]
        for cp in copies:
            cp.start()
        for cp in copies:
            cp.wait()
        _exchange_finish(xsrc, xout, xsems, True)

    outs = _call(
        body, name=name, out_shape=lands + extra_lands, in_specs=[ANY_SPEC] * (n + ne), out_specs=[ANY_SPEC] * (n + ne),
        scratch_shapes=[pltpu.SemaphoreType.DMA((n, 4)), pltpu.SemaphoreType.DMA((n, 4))] + extra_sems,
    )(*arrs, *extra)
    return outs[:n], outs[n:]


def _pair_sum(own, other, name):
    _, R, C = own.shape
    tr = math.gcd(R, max(8, (128 * 1024) // C))

    def body(a_ref, b_ref, o_ref):
        o_ref[...] = (a_ref[...].astype(F32) + b_ref[...].astype(F32)).astype(COMM)

    blk = pl.BlockSpec((4, tr, C), lambda i: (0, i, 0))
    return _call(body, name=name, grid=(R // tr,), out_shape=jax.ShapeDtypeStruct(own.shape, COMM),
                 in_specs=[blk, blk], out_specs=blk, compiler_params=_params(32))(own, other)


def _scatter_chips(arrs, name):
    n = len(arrs)
    lands = [jax.ShapeDtypeStruct(a.shape, a.dtype) for a in arrs]

    def body(*refs):
        srcs, outs = refs[:n], refs[n:2 * n]
        send_sems, recv_sems, local_sems = refs[2 * n:]
        x, y, c = lax.axis_index("x"), lax.axis_index("y"), lax.axis_index("c")
        chip = 2 * x + y
        others = [(1 - x, y), (x, 1 - y), (1 - x, 1 - y)]
        local = [pltpu.make_async_copy(srcs[a].at[chip], outs[a].at[chip], local_sems.at[a]) for a in range(n)]
        for cp in local:
            cp.start()

        def copies(arriving):
            return [pltpu.make_async_remote_copy(
                src_ref=srcs[a].at[2 * px + py], dst_ref=outs[a].at[2 * px + py if arriving else chip],
                send_sem=send_sems.at[a, j], recv_sem=recv_sems.at[a, j], device_id=(px, py, c), device_id_type=MESH)
                for a in range(n) for j, (px, py) in enumerate(others)]

        for cp in copies(False):
            cp.start()
        for cp in copies(True):
            cp.wait_recv()
        for cp in copies(False):
            cp.wait_send()
        for cp in local:
            cp.wait()

    return _call(body, name=name, out_shape=lands, in_specs=[ANY_SPEC] * n, out_specs=[ANY_SPEC] * n,
                 scratch_shapes=[pltpu.SemaphoreType.DMA((n, 3)), pltpu.SemaphoreType.DMA((n, 3)),
                                 pltpu.SemaphoreType.DMA((n,))])(*arrs)


def _exchange(arrs, scatter, name):
    n = len(arrs)
    lands, sems = _exchange_shapes(arrs, scatter)

    def body(*refs):
        _exchange_start(refs[:n], refs[n:2 * n], refs[2 * n:], scatter)
        _exchange_finish(refs[:n], refs[n:2 * n], refs[2 * n:], scatter)

    return _call(body, name=name, out_shape=lands, in_specs=[ANY_SPEC] * n, out_specs=[ANY_SPEC] * n,
                 scratch_shapes=sems)(*arrs)


def _spatial_mix(vnb, ws_ref, bT_ref, mix_ref, tm):
    tri, _ = _tril_mask()
    for g in range(GROUPS):
        wm = jnp.where(tri, ws_ref[g], 0.0).astype(MXU)
        cols = slice(g * CHUNK, (g + 1) * CHUNK)
        for ch in range(tm // CHUNK):
            rows = slice(ch * CHUNK, (ch + 1) * CHUNK)
            mix_ref[rows, cols] = _mm(wm, vnb[rows, cols]) + bT_ref[:, g:g + 1]


def _sgu_fwd(x, ln, w_in, g_v, ws, bT, w_out, name):
    T, D = x.shape
    tm = min(TOKEN_TILE, T)
    nw = w_in.shape[2]

    def body(x_ref, ln_ref, win_ref, gv_ref, ws_ref, bT_ref, wout_ref, xo_ref, z_ref, h_ref, y_ref, mix_ref):
        xv = x_ref[...]
        h, _, _ = _rms(xv, ln_ref[...])
        hb = h.astype(MXU)
        h_ref[...] = hb
        for j in range(NDEV):
            z_ref[:, j * nw:(j + 1) * nw] = _mm(hb, win_ref[j])
        u = _gelu(z_ref[:, :D])
        gv = _gelu(z_ref[:, D:])
        vn, _, _ = _rms(gv, gv_ref[...])
        _spatial_mix(vn.astype(MXU), ws_ref, bT_ref, mix_ref, tm)
        y = (u * mix_ref[...]).astype(MXU)
        y_ref[...] = y
        xo_ref[...] = xv + _mm(y, wout_ref[...])

    return _call(
        body, name=name, grid=(T // tm,),
        out_shape=[jax.ShapeDtypeStruct((T, D), F32), jax.ShapeDtypeStruct((T, 2 * D), F32),
                   jax.ShapeDtypeStruct((T, D), MXU), jax.ShapeDtypeStruct((T, D), MXU)],
        in_specs=[_tile(tm, D), _whole(ln.shape), _whole(w_in.shape), _whole(g_v.shape), _whole(ws.shape),
                  _whole(bT.shape), _whole(w_out.shape)],
        out_specs=[_tile(tm, D), _tile(tm, 2 * D), _tile(tm, D), _tile(tm, D)],
        scratch_shapes=[pltpu.VMEM((tm, D), F32)],
        compiler_params=_params(40),
    )(x, ln, w_in, g_v, ws, bT, w_out)


def _mlp_fwd(x, ln, w_up, w_down, cargo, name):
    T, D = x.shape
    tm = min(TOKEN_TILE, T)
    nf = w_up.shape[2]
    F = nf * NDEV
    nc = len(cargo)
    lands, sems = _exchange_shapes(cargo, False)

    def body(x_ref, ln_ref, wup_ref, wdown_ref, *rest):
        xo_ref, pre_ref, h_ref = rest[nc:nc + 3]
        finish = _ride_along(rest[:nc] + rest[nc + 3:], nc, False, rank=1)
        xv = x_ref[...]
        h, _, _ = _rms(xv, ln_ref[...])
        hb = h.astype(MXU)
        h_ref[...] = hb
        for j in range(NDEV):
            pre_ref[:, j * nf:(j + 1) * nf] = _mm(hb, wup_ref[j])
        a = jnp.maximum(pre_ref[...], 0.0)
        xo_ref[...] = xv + _mm(a * a, wdown_ref[...])
        finish()

    outs = _call(
        body, name=name, grid=(T // tm,),
        out_shape=[jax.ShapeDtypeStruct((T, D), F32), jax.ShapeDtypeStruct((T, F), F32),
                   jax.ShapeDtypeStruct((T, D), MXU)] + lands,
        in_specs=[_tile(tm, D), _whole(ln.shape), _whole(w_up.shape), _whole(w_down.shape)] + [ANY_SPEC] * nc,
        out_specs=[_tile(tm, D), _tile(tm, F), _tile(tm, D)] + [ANY_SPEC] * nc,
        scratch_shapes=sems, compiler_params=_params(52),
    )(x, ln, w_up, w_down, *cargo)
    return outs[0], outs[1], outs[2], outs[3:]


def _ple_fwd(x, p, ln, w_g, w_pp, name):
    T, D = x.shape
    tm = min(TOKEN_TILE, T)
    npp = w_pp.shape[2]

    def body(x_ref, p_ref, ln_ref, wg_ref, wpp_ref, xo_ref, gate_ref, pp_ref, h_ref):
        xv = x_ref[...]
        h, _, _ = _rms(xv, ln_ref[...])
        hb = h.astype(MXU)
        h_ref[...] = hb
        gate = jax.nn.sigmoid(_mm(hb, wg_ref[...]))
        gate_ref[...] = gate
        pb = p_ref[...].astype(MXU)
        for j in range(NDEV):
            pp_ref[:, j * npp:(j + 1) * npp] = _mm(pb, wpp_ref[j])
        xo_ref[...] = xv + pp_ref[...] * gate

    return _call(
        body, name=name, grid=(T // tm,),
        out_shape=[jax.ShapeDtypeStruct((T, D), F32), jax.ShapeDtypeStruct((T, D), F32),
                   jax.ShapeDtypeStruct((T, D), F32), jax.ShapeDtypeStruct((T, D), MXU)],
        in_specs=[_tile(tm, D), _tile(tm, p.shape[1]), _whole(ln.shape), _whole(w_g.shape), _whole(w_pp.shape)],
        out_specs=[_tile(tm, D), _tile(tm, D), _tile(tm, D), _tile(tm, D)],
        compiler_params=_params(32),
    )(x, p, ln, w_g, w_pp)


def _head_ones():
    row = lax.broadcasted_iota(jnp.int32, (128, 128), 0)
    col = lax.broadcasted_iota(jnp.int32, (128, 128), 1)
    return (jnp.right_shift(row, 6) == jnp.right_shift(col, 6)).astype(MXU)


def _head_rms(x, g, ones):
    rstd = lax.rsqrt(_split_dot(x * x, ones, 3) * (1.0 / HEAD_DIM) + EPS)
    xhat = x * rstd
    return xhat * g, xhat, rstd


def _head_rms_bwd(dh, xhat, rstd, g, ones):
    dxh = dh * g
    mean = _split_dot(dxh * xhat, ones, 3) * (1.0 / HEAD_DIM)
    return rstd * (dxh - xhat * mean), jnp.sum(dh * xhat, axis=0, keepdims=True)


def _qkv_fwd(x, ln_q, ln_kv, g_q, g_k, w_q, w_kv, name):
    T, D = x.shape
    tm = min(TOKEN_TILE, T)
    nk = w_kv.shape[2]
    half = NDEV // 2

    def body(x_ref, lnq_ref, lnkv_ref, gq_ref, gk_ref, wq_ref, wkv_ref,
             q_ref, k_ref, v_ref, qpre_ref, kpre_ref, hq_ref, hkv_ref):
        xv = x_ref[...]
        _, xhat, _ = _rms(xv, lnq_ref[...])
        hq = (xhat * lnq_ref[...]).astype(MXU)
        hkv = (xhat * lnkv_ref[...]).astype(MXU)
        hq_ref[...] = hq
        hkv_ref[...] = hkv
        qpre_ref[...] = _mm(hq, wq_ref[...])
        for j in range(half):
            kpre_ref[:, j * nk:(j + 1) * nk] = _mm(hkv, wkv_ref[j])
            v_ref[:, j * nk:(j + 1) * nk] = _mm(hkv, wkv_ref[half + j]).astype(MXU)
        ones = _head_ones()
        for b in range(D // 128):
            cols = slice(b * 128, (b + 1) * 128)
            qn, _, _ = _head_rms(qpre_ref[:, cols], gq_ref[:, cols], ones)
            q_ref[:, cols] = (qn * SCALE).astype(MXU)
            kn, _, _ = _head_rms(kpre_ref[:, cols], gk_ref[:, cols], ones)
            k_ref[:, cols] = kn.astype(MXU)

    return _call(
        body, name=name, grid=(T // tm,),
        out_shape=[jax.ShapeDtypeStruct((T, D), MXU)] * 3 + [jax.ShapeDtypeStruct((T, D), F32)] * 2
        + [jax.ShapeDtypeStruct((T, D), MXU)] * 2,
        in_specs=[_tile(tm, D), _whole(ln_q.shape), _whole(ln_kv.shape), _whole(g_q.shape), _whole(g_k.shape),
                  _whole(w_q.shape), _whole(w_kv.shape)],
        out_specs=[_tile(tm, D)] * 7,
        compiler_params=_params(40),
    )(x, ln_q, ln_kv, g_q, g_k, w_q, w_kv)


SB_KEYS = 2 * QBLK


def _sb_consts():
    row = lax.broadcasted_iota(jnp.int32, (QBLK, QBLK), 0)
    col = lax.broadcasted_iota(jnp.int32, (QBLK, QBLK), 1)
    lane = lax.broadcasted_iota(jnp.int32, (QBLK, 128), 1)
    ones = jnp.ones((QBLK, QBLK), MXU)
    later = jnp.concatenate([(row > col).astype(MXU), ones], axis=1)
    later_eq = jnp.concatenate([(row >= col).astype(MXU), ones], axis=1)
    return later, later_eq, lane < HEAD_DIM


MASKED_LOG = -1e30


def _sb_window(i, w):
    upper = (i + 1) * QBLK - w * SB_KEYS
    start = pl.multiple_of(jnp.maximum(upper - SB_KEYS, 0), QBLK)
    key = lax.broadcasted_iota(jnp.int32, (2 * QBLK, SB_KEYS), 1) + start
    return start, key < upper


def _sb_diagonal():
    row = jnp.bitwise_and(lax.broadcasted_iota(jnp.int32, (2 * QBLK, SB_KEYS), 0), QBLK - 1)
    key = lax.broadcasted_iota(jnp.int32, (2 * QBLK, SB_KEYS), 1)
    cases = []
    for shift in (0, QBLK):
        seen = key < row + shift
        cases.append(jnp.stack([jnp.where(seen, 1.0, 0.0), jnp.where(seen, 0.0, MASKED_LOG)]))
    return jnp.stack(cases).astype(F32)


_SB_DIAG_SPEC = pl.BlockSpec((None, 2, 2 * QBLK, SB_KEYS), lambda h, i: (jnp.minimum(i, 1), 0, 0, 0))


def _sb_terms(x, terms):
    x = jnp.concatenate([x[:, :QBLK], x[:, QBLK:]], axis=0)
    out = []
    for _ in range(terms):
        part = x.astype(MXU)
        x = x - part.astype(F32)
        out.append(part)
    return tuple(out)


def _sb_suffix(parts, ones, carry):
    s = jnp.dot(jnp.concatenate(parts[:2], axis=1), jnp.concatenate([ones, ones], axis=0),
                preferred_element_type=F32)
    for part in parts[2:]:
        s = s + jnp.dot(part, ones, preferred_element_type=F32)
    rows = s.shape[0] // 2
    s_lo, sum_lo, s_hi, sum_hi = s[:rows, :QBLK], s[:rows, QBLK:], s[rows:, :QBLK], s[rows:, QBLK:]
    return jnp.concatenate([s_lo + (carry + sum_hi), s_hi + carry], axis=1), carry + (sum_lo + sum_hi)


def _sb_scores(z, mask):
    sp = _softplus(z)
    l, log_sig = -sp, z - sp
    if isinstance(mask, tuple):
        keep, bias = mask
        l, log_sig = l * keep, log_sig + bias
    else:
        l = jnp.where(mask, l, 0.0)
        log_sig = jnp.where(mask, log_sig, MASKED_LOG)
    return log_sig, _sb_terms(l, 2)


def _sb_weights(staged, later, c_l):
    log_sig, parts = staged
    b, c_l = _sb_suffix(parts, later, c_l)
    return jnp.exp(log_sig + b), c_l


DEAD_LOG = -88.0


def _sb_alive(carry):
    return (jnp.max(carry[0]) > DEAD_LOG).astype(jnp.int32)


def _ride_along(refs, n, scatter, rank=2):
    if n == 0:
        return lambda: None
    step, steps = 0, 1
    for d in range(rank):
        step = step * pl.num_programs(d) + pl.program_id(d)
        steps = steps * pl.num_programs(d)
    srcs, lands, sems = refs[:n], refs[n:2 * n], refs[2 * n:]

    @pl.when(step == 0)
    def _():
        _exchange_start(srcs, lands, sems, scatter)

    def finish():
        @pl.when(step == steps - 1)
        def _():
            _exchange_finish(srcs, lands, sems, scatter)

    return finish


def _sb_fwd(q, k, v, cargo, name):
    T, D = q.shape
    nc = len(cargo)
    lands, sems = _exchange_shapes(cargo, False)

    def body(diag_ref, q_ref, k_ref, v_ref, *rest):
        o_ref = rest[nc]
        finish = _ride_along(rest[:nc] + rest[nc + 1:], nc, False)
        i = pl.program_id(1)
        n_steps = (i + 2) // 2
        later, _, first = _sb_consts()
        qv = q_ref[...]
        zero = jnp.zeros_like(qv)
        q2 = jnp.concatenate([jnp.where(first, qv, zero), jnp.where(first, zero, qv)], axis=0)

        def window(w, carry, diagonal):
            start, mask = _sb_window(i, w)
            if diagonal:
                mask = (diag_ref[0], diag_ref[1])
            kw = k_ref[pl.ds(start, SB_KEYS), :]
            vw = v_ref[pl.ds(start, SB_KEYS), :]
            c_l, acc = carry
            a, c_l = _sb_weights(_sb_scores(_mm_nt(q2, kw), mask), later, c_l)
            return c_l, acc + _mm(a, vw)

        def step(state):
            w, _, carry = state
            carry = window(w, carry, False)
            return w + 1, _sb_alive(carry), carry

        carry = window(0, (jnp.zeros((2 * QBLK, 128), F32),) * 2, True)
        _, _, carry = lax.while_loop(lambda s: (s[0] < n_steps) & (s[1] > 0), step,
                                     (jnp.int32(1), _sb_alive(carry), carry))
        o_ref[...] = jnp.where(first, carry[1][:QBLK], carry[1][QBLK:])
        finish()

    qblk = pl.BlockSpec((QBLK, 128), lambda h, i: (i, h))
    kblk = pl.BlockSpec((T, 128), lambda h, i: (0, h))
    outs = _call(
        body, name=name, grid=(D // 128, T // QBLK), out_shape=[jax.ShapeDtypeStruct((T, D), F32)] + lands,
        in_specs=[_SB_DIAG_SPEC, qblk, kblk, kblk] + [ANY_SPEC] * nc, out_specs=[qblk] + [ANY_SPEC] * nc,
        scratch_shapes=sems, compiler_params=_params(32, 2),
    )(_sb_diagonal(), q, k, v, *cargo)
    return outs[0], outs[1:]


def _sb_bwd(q, k, v, o, do, cargo, name):
    T, D = q.shape
    nc = len(cargo)
    lands, sems = _exchange_shapes(cargo, True)

    def body(diag_ref, q_ref, k_ref, v_ref, o_ref, do_ref, *rest):
        dq_ref, dk_ref, dv_ref = rest[nc:nc + 3]
        finish = _ride_along(rest[:nc] + rest[nc + 3:], nc, True)
        i = pl.program_id(1)

        @pl.when(i == 0)
        def _():
            dk_ref[...] = jnp.zeros_like(dk_ref)
            dv_ref[...] = jnp.zeros_like(dv_ref)

        n_steps = (i + 2) // 2
        later, later_eq, first = _sb_consts()
        qv = q_ref[...]
        dob = do_ref[...].astype(MXU)
        zero = jnp.zeros_like(qv)
        q2 = jnp.concatenate([jnp.where(first, qv, zero), jnp.where(first, zero, qv)], axis=0)
        do2 = jnp.concatenate([jnp.where(first, dob, zero), jnp.where(first, zero, dob)], axis=0)
        prod = o_ref[...] * dob.astype(F32)
        prod2 = jnp.concatenate([jnp.where(first, prod, 0.0), jnp.where(first, 0.0, prod)], axis=0)
        total = _split_dot(prod2, jnp.ones((128, 128), MXU), 3)
        total = jnp.concatenate([total, total], axis=1)

        def window(w, carry, diagonal):
            start, mask = _sb_window(i, w)
            if diagonal:
                mask = (diag_ref[0], diag_ref[1])
            kw = k_ref[pl.ds(start, SB_KEYS), :]
            vw = v_ref[pl.ds(start, SB_KEYS), :]
            c_l, c_e, dq = carry
            log_sig, parts = _sb_scores(_mm_nt(q2, kw), mask)
            a, c_l = _sb_weights((log_sig, parts), later, c_l)
            ab = a.astype(MXU)
            e = ab.astype(F32) * _mm_nt(do2, vw)
            from_here, c_e = _sb_suffix(_sb_terms(e, 2), later_eq, c_e)
            sig = jnp.exp(log_sig)
            dzb = (e * (1.0 - sig) - sig * (total - from_here)).astype(MXU)
            dk_ref[pl.ds(start, SB_KEYS), :] += _mm_tn(dzb, q2)
            dv_ref[pl.ds(start, SB_KEYS), :] += _mm_tn(ab, do2)
            return c_l, c_e, dq + _mm(dzb, kw)

        def step(state):
            w, _, carry = state
            carry = window(w, carry, False)
            return w + 1, _sb_alive(carry), carry

        carry = window(0, (jnp.zeros((2 * QBLK, 128), F32),) * 3, True)
        _, _, carry = lax.while_loop(lambda s: (s[0] < n_steps) & (s[1] > 0), step,
                                     (jnp.int32(1), _sb_alive(carry), carry))
        dq_ref[...] = jnp.where(first, carry[2][:QBLK], carry[2][QBLK:]) * SCALE
        finish()

    qblk = pl.BlockSpec((QBLK, 128), lambda h, i: (i, h))
    kblk = pl.BlockSpec((T, 128), lambda h, i: (0, h))
    full = jax.ShapeDtypeStruct((T, D), F32)
    outs = _call(
        body, name=name, grid=(D // 128, T // QBLK), out_shape=[full, full, full] + lands,
        in_specs=[_SB_DIAG_SPEC, qblk, kblk, kblk, qblk, qblk] + [ANY_SPEC] * nc,
        out_specs=[qblk, kblk, kblk] + [ANY_SPEC] * nc, scratch_shapes=sems, compiler_params=_params(32, 2),
    )(_sb_diagonal(), q, k, v, o, do, *cargo)
    return outs[0], outs[1], outs[2], outs[3:]


def _proj_res(x, a, w, name):
    T, D = x.shape
    tm = min(TOKEN_TILE, T)

    def body(x_ref, a_ref, w_ref, o_ref):
        o_ref[...] = x_ref[...] + _mm(a_ref[...], w_ref[...])

    return _call(
        body, name=name, grid=(T // tm,), out_shape=jax.ShapeDtypeStruct((T, D), F32),
        in_specs=[_tile(tm, D), _tile(tm, a.shape[1]), _whole(w.shape)], out_specs=_tile(tm, D),
        compiler_params=_params(32),
    )(x, a, w)


def _proj_nt(g, w, name):
    T = g.shape[0]
    K = w.shape[0]
    tm = min(TOKEN_TILE, T)

    def body(g_ref, w_ref, o_ref):
        o_ref[...] = _mm_nt(g_ref[...], w_ref[...])

    return _call(
        body, name=name, grid=(T // tm,), out_shape=jax.ShapeDtypeStruct((T, K), F32),
        in_specs=[_tile(tm, g.shape[1]), _whole(w.shape)], out_specs=_tile(tm, K),
        compiler_params=_params(32),
    )(g, w)


def _loss_grad(y, tgt, name):
    T, D = y.shape
    tm = min(TOKEN_TILE, T)

    def body(y_ref, t_ref, dy_ref, loss_ref):
        @pl.when(pl.program_id(0) == 0)
        def _():
            loss_ref[...] = jnp.zeros_like(loss_ref)
        diff = y_ref[...] - t_ref[...]
        dy_ref[...] = diff * (1.0 / D)
        rows = jnp.sum(diff * diff, axis=1, keepdims=True) * (1.0 / D)
        loss_ref[...] += 0.5 * jnp.sum(rows, axis=0, keepdims=True)

    return _call(
        body, name=name, grid=(T // tm,),
        out_shape=[jax.ShapeDtypeStruct((T, D), F32), jax.ShapeDtypeStruct((1, 1), F32)],
        in_specs=[_tile(tm, D), _tile(tm, D)], out_specs=[_tile(tm, D), _acc((1, 1))],
        compiler_params=_params(32),
    )(y, tgt)


def _ple_bwd(dx, x, gate, pp, ln, w_g, name):
    T, D = x.shape
    tm = min(TOKEN_TILE, T)

    def body(dx_ref, x_ref, gate_ref, pp_ref, ln_ref, wg_ref, dxo_ref, dpp_ref, dgp_ref, dln_ref):
        @pl.when(pl.program_id(0) == 0)
        def _():
            dln_ref[...] = jnp.zeros_like(dln_ref)
        dxv = dx_ref[...]
        gate = gate_ref[...]
        _, xhat, rstd = _rms(x_ref[...], ln_ref[...])
        dpp_ref[...] = (dxv * gate).astype(MXU)
        dgp = (dxv * pp_ref[...] * gate * (1.0 - gate)).astype(MXU)
        dgp_ref[...] = dgp
        dxn, dln = _rms_bwd(_mm_nt(dgp, wg_ref[...]), xhat, rstd, ln_ref[...])
        dln_ref[...] += dln
        dxo_ref[...] = dxn + dxv

    return _call(
        body, name=name, grid=(T // tm,),
        out_shape=[jax.ShapeDtypeStruct((T, D), F32), jax.ShapeDtypeStruct((T, D), MXU),
                   jax.ShapeDtypeStruct((T, D), MXU), jax.ShapeDtypeStruct(ln.shape, F32)],
        in_specs=[_tile(tm, D)] * 4 + [_whole(ln.shape), _whole(w_g.shape)],
        out_specs=[_tile(tm, D), _tile(tm, D), _tile(tm, D), _acc(ln.shape)],
        compiler_params=_params(32),
    )(dx, x, gate, pp, ln, w_g)


def _mlp_bwd(dx, x, pre, ln, w_up, w_down, name):
    T, D = x.shape
    tm = min(TOKEN_TILE, T)
    nf = w_up.shape[2]
    F = nf * NDEV

    def body(dx_ref, x_ref, pre_ref, ln_ref, wup_ref, wdown_ref, dxo_ref, dpre_ref, s_ref, dln_ref):
        @pl.when(pl.program_id(0) == 0)
        def _():
            dln_ref[...] = jnp.zeros_like(dln_ref)
        dxv = dx_ref[...]
        _, xhat, rstd = _rms(x_ref[...], ln_ref[...])
        a = jnp.maximum(pre_ref[...], 0.0)
        s_ref[...] = (a * a).astype(MXU)
        dpre_ref[...] = (_mm_nt(dxv, wdown_ref[...]) * (2.0 * a)).astype(MXU)
        dh = _mm_nt(dpre_ref[:, :nf], wup_ref[0])
        for j in range(1, NDEV):
            dh += _mm_nt(dpre_ref[:, j * nf:(j + 1) * nf], wup_ref[j])
        dxn, dln = _rms_bwd(dh, xhat, rstd, ln_ref[...])
        dln_ref[...] += dln
        dxo_ref[...] = dxn + dxv

    return _call(
        body, name=name, grid=(T // tm,),
        out_shape=[jax.ShapeDtypeStruct((T, D), F32), jax.ShapeDtypeStruct((T, F), MXU),
                   jax.ShapeDtypeStruct((T, F), MXU), jax.ShapeDtypeStruct(ln.shape, F32)],
        in_specs=[_tile(tm, D), _tile(tm, D), _tile(tm, F), _whole(ln.shape), _whole(w_up.shape),
                  _whole(w_down.shape)],
        out_specs=[_tile(tm, D), _tile(tm, F), _tile(tm, F), _acc(ln.shape)],
        compiler_params=_params(56),
    )(dx, x, pre, ln, w_up, w_down)


def _qkv_bwd(dx, x, dq, dk, dv, q_pre, k_pre, ln_q, ln_kv, g_q, g_k, w_q, w_kv, cargo, name):
    T, D = x.shape
    tm = min(TOKEN_TILE, T)
    nk = w_kv.shape[2]
    n_tiles = T // tm
    nc = len(cargo)
    lands, sems = _exchange_shapes(cargo, True)

    def body(dx_ref, x_ref, dq_ref, dk_ref, dv_ref, qpre_ref, kpre_ref, lnq_ref, lnkv_ref, gq_ref, gk_ref,
             wq_ref, wkv_ref, *rest):
        dxo_ref, dqp_ref, dkv_ref, dlnq_ref, dlnkv_ref, dgq_ref, dgk_ref = rest[nc:nc + 7]
        gq_acc, gk_acc = rest[2 * nc + 7:2 * nc + 9]
        finish = _ride_along(rest[:nc] + rest[nc + 7:2 * nc + 7] + rest[2 * nc + 9:], nc, True, rank=1)
        i = pl.program_id(0)

        @pl.when(i == 0)
        def _():
            dlnq_ref[...] = jnp.zeros_like(dlnq_ref)
            dlnkv_ref[...] = jnp.zeros_like(dlnkv_ref)
            gq_acc[...] = jnp.zeros_like(gq_acc)
            gk_acc[...] = jnp.zeros_like(gk_acc)

        ones = _head_ones()
        for b in range(D // 128):
            cols = slice(b * 128, (b + 1) * 128)
            _, xh, rs = _head_rms(qpre_ref[:, cols], gq_ref[:, cols], ones)
            d, dg = _head_rms_bwd(dq_ref[:, cols], xh, rs, gq_ref[:, cols], ones)
            dqp_ref[:, cols] = d.astype(MXU)
            gq_acc[:, cols] += dg
            _, xh, rs = _head_rms(kpre_ref[:, cols], gk_ref[:, cols], ones)
            d, dg = _head_rms_bwd(dk_ref[:, cols], xh, rs, gk_ref[:, cols], ones)
            dkv_ref[:, cols] = d.astype(MXU)
            gk_acc[:, cols] += dg
        dkv_ref[:, D:] = dv_ref[...].astype(MXU)

        _, xhat, rstd = _rms(x_ref[...], lnq_ref[...])
        dhq = _mm_nt(dqp_ref[...], wq_ref[...])
        dhkv = _mm_nt(dkv_ref[:, :nk], wkv_ref[0])
        for j in range(1, NDEV):
            dhkv += _mm_nt(dkv_ref[:, j * nk:(j + 1) * nk], wkv_ref[j])
        dxq, dlnq = _rms_bwd(dhq, xhat, rstd, lnq_ref[...])
        dxkv, dlnkv = _rms_bwd(dhkv, xhat, rstd, lnkv_ref[...])
        dlnq_ref[...] += dlnq
        dlnkv_ref[...] += dlnkv
        dxo_ref[...] = dx_ref[...] + dxq + dxkv

        @pl.when(i == n_tiles - 1)
        def _():
            row = lax.broadcasted_iota(jnp.int32, (D, 128), 0)
            col = lax.broadcasted_iota(jnp.int32, (D, 128), 1)
            fold = (jnp.bitwise_and(row, HEAD_DIM - 1) == col).astype(MXU)
            dgq_ref[...] = _split_dot(jnp.broadcast_to(gq_acc[...], (8, D)), fold, 3)
            dgk_ref[...] = _split_dot(jnp.broadcast_to(gk_acc[...], (8, D)), fold, 3)

        finish()

    small = jax.ShapeDtypeStruct((8, 128), F32)
    outs = _call(
        body, name=name, grid=(n_tiles,),
        out_shape=[jax.ShapeDtypeStruct((T, D), F32), jax.ShapeDtypeStruct((T, D), MXU),
                   jax.ShapeDtypeStruct((T, 2 * D), MXU), jax.ShapeDtypeStruct(ln_q.shape, F32),
                   jax.ShapeDtypeStruct(ln_kv.shape, F32), small, small] + lands,
        in_specs=[_tile(tm, D)] * 7 + [_whole(ln_q.shape), _whole(ln_kv.shape), _whole(g_q.shape),
                                       _whole(g_k.shape), _whole(w_q.shape), _whole(w_kv.shape)] + [ANY_SPEC] * nc,
        out_specs=[_tile(tm, D), _tile(tm, D), _tile(tm, 2 * D), _acc(ln_q.shape), _acc(ln_kv.shape),
                   _acc((8, 128)), _acc((8, 128))] + [ANY_SPEC] * nc,
        scratch_shapes=[pltpu.VMEM((1, D), F32), pltpu.VMEM((1, D), F32)] + sems,
        compiler_params=_params(48),
    )(dx, x, dq, dk, dv, q_pre, k_pre, ln_q, ln_kv, g_q, g_k, w_q, w_kv, *cargo)
    return outs[:7], outs[7:]


def _sgu_bwd(dx, x, z, ln, w_in, g_v, ws, wsT, bT, w_out, name):
    T, D = x.shape
    tm = min(TOKEN_TILE, T)
    nw = w_in.shape[2]

    def body(dx_ref, x_ref, z_ref, ln_ref, win_ref, gv_ref, ws_ref, wsT_ref, bT_ref, wout_ref,
             dxo_ref, dz_ref, dws_ref, dbT_ref, dln_ref, dgv_ref, mix_ref, dvn_ref):
        @pl.when(pl.program_id(0) == 0)
        def _():
            dws_ref[...] = jnp.zeros_like(dws_ref)
            dbT_ref[...] = jnp.zeros_like(dbT_ref)
            dln_ref[...] = jnp.zeros_like(dln_ref)
            dgv_ref[...] = jnp.zeros_like(dgv_ref)
        dxv = dx_ref[...]
        _, xhat, rstd = _rms(x_ref[...], ln_ref[...])
        u, du = _gelu_and_grad(z_ref[:, :D])
        gv, dgv = _gelu_and_grad(z_ref[:, D:])
        vn, vhat, rstd_v = _rms(gv, gv_ref[...])
        vnb = vn.astype(MXU)
        _spatial_mix(vnb, ws_ref, bT_ref, mix_ref, tm)
        dy = _mm_nt(dxv, wout_ref[...])
        d_u = dy * mix_ref[...]
        d_mix = dy * u
        dmb = d_mix.astype(MXU)
        tri, triT = _tril_mask()
        for g in range(GROUPS):
            wmT = jnp.where(triT, wsT_ref[g], 0.0).astype(MXU)
            cols = slice(g * CHUNK, (g + 1) * CHUNK)
            for ch in range(tm // CHUNK):
                rows = slice(ch * CHUNK, (ch + 1) * CHUNK)
                dm = dmb[rows, cols]
                dws_ref[g] += jnp.where(tri, _mm_nt(dm, vnb[rows, cols]), 0.0)
                dbT_ref[:, g:g + 1] += jnp.sum(d_mix[rows, cols], axis=1, keepdims=True)
                dvn_ref[rows, cols] = _mm(wmT, dm)
        d_gv, dg = _rms_bwd(dvn_ref[...], vhat, rstd_v, gv_ref[...])
        dgv_ref[...] += dg
        dz_ref[:, :D] = (d_u * du).astype(MXU)
        dz_ref[:, D:] = (d_gv * dgv).astype(MXU)
        dh = _mm_nt(dz_ref[:, :nw], win_ref[0])
        for j in range(1, NDEV):
            dh += _mm_nt(dz_ref[:, j * nw:(j + 1) * nw], win_ref[j])
        dxn, dln = _rms_bwd(dh, xhat, rstd, ln_ref[...])
        dln_ref[...] += dln
        dxo_ref[...] = dxn + dxv

    return _call(
        body, name=name, grid=(T // tm,),
        out_shape=[jax.ShapeDtypeStruct((T, D), F32), jax.ShapeDtypeStruct((T, 2 * D), MXU),
                   jax.ShapeDtypeStruct(ws.shape, F32), jax.ShapeDtypeStruct(bT.shape, F32),
                   jax.ShapeDtypeStruct(ln.shape, F32), jax.ShapeDtypeStruct(g_v.shape, F32)],
        in_specs=[_tile(tm, D), _tile(tm, D), _tile(tm, 2 * D), _whole(ln.shape), _whole(w_in.shape),
                  _whole(g_v.shape), _whole(ws.shape), _whole(wsT.shape), _whole(bT.shape), _whole(w_out.shape)],
        out_specs=[_tile(tm, D), _tile(tm, 2 * D), _acc(ws.shape), _acc(bT.shape), _acc(ln.shape),
                   _acc(g_v.shape)],
        scratch_shapes=[pltpu.VMEM((tm, D), F32), pltpu.VMEM((tm, D), F32)],
        compiler_params=_params(48),
    )(dx, x, z, ln, w_in, g_v, ws, wsT, bT, w_out)


def _wgrad_rows(a, g, name):
    T, K = a.shape
    N = g.shape[1]
    kb = K // NDEV

    def body(a_ref, g_ref, o_ref):
        o_ref[...] = _mm_tn(a_ref[...], g_ref[...]).astype(COMM)

    return _call(
        body, name=name, grid=(NDEV,), out_shape=jax.ShapeDtypeStruct((K, N), COMM),
        in_specs=[pl.BlockSpec((T, kb), lambda j: (0, j)), _whole(g.shape)],
        out_specs=pl.BlockSpec((kb, N), lambda j: (j, 0)),
        compiler_params=_params(40),
    )(a, g).reshape(NDEV, kb, N)


def _wgrad_cols(a, g, name):
    T, K = a.shape
    N = g.shape[1]
    nb = N // NDEV

    def body(a_ref, g_ref, o_ref):
        o_ref[...] = _mm_tn(a_ref[...], g_ref[...]).astype(COMM)

    return _call(
        body, name=name, grid=(NDEV,), out_shape=jax.ShapeDtypeStruct((NDEV, K, nb), COMM),
        in_specs=[_whole(a.shape), pl.BlockSpec((T, nb), lambda j: (0, j))],
        out_specs=pl.BlockSpec((None, K, nb), lambda j: (j, 0, 0)),
        compiler_params=_params(40),
    )(a, g)


def _adamw_rows(R, C):
    tr = math.gcd(R, max(8, (128 * 1024) // C))
    return R if tr < 64 else tr


def _adamw_update(w_ref, m_ref, v_ref, s_ref, g_ref, d_ref, mo_ref, vo_ref):
    g = s_ref[0].astype(F32)
    for j in range(1, s_ref.shape[0]):
        g = g + s_ref[j].astype(F32)
    mn = ADAM_B1 * m_ref[...] + (1.0 - ADAM_B1) * g
    vn = ADAM_B2 * v_ref[...] + (1.0 - ADAM_B2) * (g * g)
    g_ref[...] = g
    mo_ref[...] = mn
    vo_ref[...] = vn
    m_hat = mn / (1.0 - ADAM_B1 ** ADAM_STEP)
    v_hat = vn / (1.0 - ADAM_B2 ** ADAM_STEP)
    d_ref[...] = -ADAM_LR * (m_hat / (jnp.sqrt(v_hat) + ADAM_EPS) + ADAM_WD * w_ref[...])


def _adamw_layers(w, m, v, slots0, slots1, name):
    _, R, C = w.shape
    tr = _adamw_rows(R, C)
    last = R // tr - 1

    def body(w_ref, m_ref, v_ref, s0_ref, s1_ref, *outs):
        @pl.when(pl.program_id(0) == 0)
        def _():
            _adamw_update(w_ref, m_ref, v_ref, s0_ref, *outs)

        @pl.when(pl.program_id(0) == 1)
        def _():
            _adamw_update(w_ref, m_ref, v_ref, s1_ref, *outs)

    blk = pl.BlockSpec((None, tr, C), lambda l, i: (l, i, 0))
    s0_blk = pl.BlockSpec((slots0.shape[0], tr, C), lambda l, i: (0, jnp.where(l == 0, i, last), 0))
    s1_blk = pl.BlockSpec((slots1.shape[0], tr, C), lambda l, i: (0, jnp.where(l == 1, i, 0), 0))
    out = jax.ShapeDtypeStruct(w.shape, F32)
    return _call(
        body, name=name, grid=(2, R // tr), out_shape=[out, out, out, out],
        in_specs=[blk, blk, blk, s0_blk, s1_blk], out_specs=[blk] * 4, compiler_params=_params(32, 2),
    )(w, m, v, slots0, slots1)


def _adamw(w, m, v, slots, name):
    R, C = w.shape
    n = slots.shape[0]
    tr = _adamw_rows(R, C)

    def body(*refs):
        _adamw_update(*refs)

    blk = pl.BlockSpec((tr, C), lambda i: (i, 0))
    out = jax.ShapeDtypeStruct((R, C), F32)
    return _call(
        body, name=name, grid=(R // tr,), out_shape=[out, out, out, out],
        in_specs=[blk, blk, blk, pl.BlockSpec((n, tr, C), lambda i: (0, i, 0))], out_specs=[blk] * 4,
        compiler_params=_params(32),
    )(w, m, v, slots)


def _rows128(a):
    flat = a.reshape(-1)
    rows = -(-flat.shape[0] // 1024) * 8
    flat = jnp.pad(flat, (0, rows * 128 - flat.shape[0]))
    return flat.reshape(rows, 128)


def kernel(x, p, ln_mix_a, w_in_a, g_v_a, w_spatial, b_spatial, w_out_a, ln_kv, w_kv, g_k, ln_mix_b, w_q, g_q, w_out_b, ln_mlp, w_up, w_down, ln_ple, w_ple_gate, w_ple_proj, loss_target, m_ln_mix_a, m_w_in_a, m_g_v_a, m_w_spatial, m_b_spatial, m_w_out_a, m_ln_kv, m_w_kv, m_g_k, m_ln_mix_b, m_w_q, m_g_q, m_w_out_b, m_ln_mlp, m_w_up, m_w_down, m_ln_ple, m_w_ple_gate, m_w_ple_proj, v_ln_mix_a, v_w_in_a, v_g_v_a, v_w_spatial, v_b_spatial, v_w_out_a, v_ln_kv, v_w_kv, v_g_k, v_ln_mix_b, v_w_q, v_g_q, v_w_out_b, v_ln_mlp, v_w_up, v_w_down, v_ln_ple, v_w_ple_gate, v_w_ple_proj):
    me = 4 * lax.axis_index("x") + 2 * lax.axis_index("y") + lax.axis_index("c")
    D = x.shape[2]
    x0, tgt = x[0], loss_target[0]
    n_layers = w_up.shape[0]

    c = lambda w: w.astype(COMM)
    first = [c(w_in_a[0]), c(w_out_a[0]), ln_mix_a, g_v_a, c(w_up[0]), c(w_down[0]), c(w_ple_gate[0]),
             c(w_ple_proj[0]), c(w_q[0]), c(w_kv)]
    second_small = [c(w_out_b[0]), c(w_ple_gate[1]), c(w_ple_proj[1])]
    second_big = [c(w_up[1]), c(w_down[1])]
    W_in, W_out_a, ln_a, gv_a, W_up0, W_down0, W_g0, W_pp0, W_q, W_kv = _gather_two_level(first, "gather_first")
    W_out_a, ln_a, gv_a = W_out_a.reshape(-1, D), ln_a.reshape(1, D), gv_a.reshape(1, D)
    W_down0, W_g0, W_q = W_down0.reshape(-1, D), W_g0.reshape(-1, D), W_q.reshape(-1, D)
    ws = w_spatial[0]
    wsT = jnp.swapaxes(ws, 1, 2)
    bT = b_spatial[0].T
    ln_kv2, ln_b = ln_kv.reshape(1, D), ln_mix_b
    gk2 = jnp.tile(g_k.reshape(1, HEAD_DIM), (1, D // HEAD_DIM))
    gq2 = jnp.tile(g_q, (1, D // HEAD_DIM))
    ln_m = [ln_mlp[l:l + 1] for l in range(n_layers)]
    ln_p = [ln_ple[l:l + 1] for l in range(n_layers)]

    x1, z, h_a, y_a = _sgu_fwd(x0, ln_a, W_in, gv_a, ws, bT, W_out_a, "sgu_fwd")
    x2, pre0, hm0, (W_out_b, W_g1, W_pp1) = _mlp_fwd(x1, ln_m[0], W_up0, W_down0, second_small, "mlp_fwd0")
    x3, gate0, pp0, hp0 = _ple_fwd(x2, p[0, 0], ln_p[0], W_g0, W_pp0, "ple_fwd0")
    qn, kn, vn, q_pre, k_pre, h_q, h_kv = _qkv_fwd(x3, ln_b, ln_kv2, gq2, gk2, W_q, W_kv, "qkv_fwd")
    o2d, (W_up1, W_down1) = _sb_fwd(qn, kn, vn, second_big, "sb_fwd")
    W_out_b, W_down1, W_g1 = W_out_b.reshape(-1, D), W_down1.reshape(-1, D), W_g1.reshape(-1, D)
    x4 = _proj_res(x3, o2d, W_out_b, "attn_out")
    x5, pre1, hm1, _ = _mlp_fwd(x4, ln_m[1], W_up1, W_down1, [], "mlp_fwd1")
    x6, gate1, pp1, hp1 = _ple_fwd(x5, p[1, 0], ln_p[1], W_g1, W_pp1, "ple_fwd1")
    dy, loss_part = _loss_grad(x6, tgt, "loss_grad")

    dx5, dpp1, dgp1, dlnp1 = _ple_bwd(dy, x5, gate1, pp1, ln_p[1], W_g1, "ple_bwd1")
    dx4, dpre1, s1, dlnm1 = _mlp_bwd(dx5, x4, pre1, ln_m[1], W_up1, W_down1, "mlp_bwd1")
    wg_big = [_wgrad_cols(hm1, dpre1, "wg_up1"), _wgrad_rows(s1, dx5, "wg_down1")]
    wg_small = [_wgrad_rows(hp1, dgp1, "wg_gate1"), _wgrad_cols(p[1, 0].astype(MXU), dpp1, "wg_proj1"),
                _wgrad_rows(o2d, dx4, "wg_out_b")]
    do2d = _proj_nt(dx4, W_out_b, "attn_out_bwd")
    dqn, dkn, dvn, (s_up1, s_down1) = _sb_bwd(qn, kn, vn, o2d, do2d, wg_big, "sb_bwd")
    (dx3, dq_pre, dkv, dlnb, dlnkv, dgq, dgk), (s_gate1, s_proj1, s_out_b) = _qkv_bwd(
        dx4, x3, dqn, dkn, dvn, q_pre, k_pre, ln_b, ln_kv2, gq2, gk2, W_q, W_kv, wg_small, "qkv_bwd")
    dgq, dgk = dgq[:1, :HEAD_DIM], dgk[:1, :HEAD_DIM]
    dx2, dpp0, dgp0, dlnp0 = _ple_bwd(dx3, x2, gate0, pp0, ln_p[0], W_g0, "ple_bwd0")
    dx1, dpre0, s0, dlnm0 = _mlp_bwd(dx2, x1, pre0, ln_m[0], W_up0, W_down0, "mlp_bwd0")
    dx0, dz, dws, dbT, dlna, dgva = _sgu_bwd(dx1, x0, z, ln_a, W_in, gv_a, ws, wsT, bT, W_out_a, "sgu_bwd")
    wg_first = [_wgrad_rows(h_q, dq_pre, "wg_q"), _wgrad_cols(h_kv, dkv, "wg_kv"),
                _wgrad_cols(hm0, dpre0, "wg_up0"), _wgrad_rows(s0, dx2, "wg_down0"), _wgrad_rows(hp0, dgp0, "wg_gate0"),
                _wgrad_cols(p[0, 0].astype(MXU), dpp0, "wg_proj0"), _wgrad_cols(h_a, dz, "wg_in_a"),
                _wgrad_rows(y_a, dx1, "wg_out_a")]

    small = [("w_spatial", dws[None], w_spatial, m_w_spatial, v_w_spatial),
             ("b_spatial", dbT.T[None], b_spatial, m_b_spatial, v_b_spatial),
             ("ln_kv", dlnkv.reshape(-1), ln_kv, m_ln_kv, v_ln_kv),
             ("g_k", dgk.reshape(-1), g_k, m_g_k, v_g_k),
             ("ln_mix_b", dlnb, ln_mix_b, m_ln_mix_b, v_ln_mix_b),
             ("g_q", dgq, g_q, m_g_q, v_g_q),
             ("ln_mlp", jnp.concatenate([dlnm0, dlnm1]), ln_mlp, m_ln_mlp, v_ln_mlp),
             ("ln_ple", jnp.concatenate([dlnp0, dlnp1]), ln_ple, m_ln_ple, v_ln_ple)]
    sharded_vec = [("ln_mix_a", dlna, ln_mix_a, m_ln_mix_a, v_ln_mix_a),
                   ("g_v_a", dgva, g_v_a, m_g_v_a, v_g_v_a)]
    packs = [[], [], [], []]
    for _, g, w, m, v in small:
        for lst, a in zip(packs, (g, w, m, v)):
            lst.append(_rows128(a))
    for _, g, w, m, v in sharded_vec:
        packs[0].append(g.reshape(NDEV, -1))
        for lst, a in zip(packs[1:], (w, m, v)):
            lst.append(jnp.broadcast_to(a, (NDEV, a.shape[1])))
    packs[0].append(_rows128(loss_part))
    for lst in packs[1:]:
        lst.append(jnp.zeros((8, 128), F32))
    g_pack, w_pack, m_pack, v_pack = (jnp.concatenate(lst) for lst in packs)
    g_pack8 = jnp.broadcast_to(g_pack[None], (NDEV,) + g_pack.shape)
    by_chip = [a.reshape((4, 2) + a.shape[1:]) for a in wg_first]
    from_sibling, (g_all,) = _scatter_pair(by_chip, [g_pack8], "scatter_pair")
    my_core = lax.axis_index("c")
    chip_sums = [_pair_sum(lax.dynamic_index_in_dim(a, my_core, 1, keepdims=False), o, f"pair_sum{j}")
                 for j, (a, o) in enumerate(zip(by_chip, from_sibling))]
    s_q, s_kv, s_up0, s_down0, s_gate0, s_proj0, s_in_a, s_out_a = _scatter_chips(chip_sums, "scatter_chips")

    def upd(w, m, v, s, name):
        shape = w.shape
        outs = _adamw(w.reshape(-1, shape[-1]), m.reshape(-1, shape[-1]), v.reshape(-1, shape[-1]), s, name)
        return [o.reshape(shape) for o in outs]

    res = {}
    res["w_out_b"] = upd(w_out_b, m_w_out_b, v_w_out_b, s_out_b, "adam_out_b")
    res["w_q"] = upd(w_q, m_w_q, v_w_q, s_q, "adam_q")
    res["w_kv"] = upd(w_kv, m_w_kv, v_w_kv, s_kv, "adam_kv")
    res["w_in_a"] = upd(w_in_a, m_w_in_a, v_w_in_a, s_in_a, "adam_in_a")
    res["w_out_a"] = upd(w_out_a, m_w_out_a, v_w_out_a, s_out_a, "adam_out_a")
    res["w_up"] = _adamw_layers(w_up, m_w_up, v_w_up, s_up0, s_up1, "adam_up")
    res["w_down"] = _adamw_layers(w_down, m_w_down, v_w_down, s_down0, s_down1, "adam_down")
    res["w_ple_gate"] = _adamw_layers(w_ple_gate, m_w_ple_gate, v_w_ple_gate, s_gate0, s_gate1, "adam_gate")
    res["w_ple_proj"] = _adamw_layers(w_ple_proj, m_w_ple_proj, v_w_ple_proj, s_proj0, s_proj1, "adam_proj")

    outs = _adamw(w_pack, m_pack, v_pack, g_all, "adam_small")
    loss = outs[0][-8, 0]
    row = 0
    for nm, g, w, m, v in small:
        nrows = _rows128(w).shape[0]
        res[nm] = [o[row:row + nrows].reshape(-1)[:w.size].reshape(w.shape) for o in outs]
        row += nrows
    for nm, g, w, m, v in sharded_vec:
        res[nm] = [lax.dynamic_slice_in_dim(o[row:row + NDEV], me, 1, axis=0) for o in outs]
        row += NDEV

    names = ["ln_mix_a", "w_in_a", "g_v_a", "w_spatial", "b_spatial", "w_out_a", "ln_kv", "w_kv", "g_k", "ln_mix_b",
             "w_q", "g_q", "w_out_b", "ln_mlp", "w_up", "w_down", "ln_ple", "w_ple_gate", "w_ple_proj"]
    out = [loss, dx0[None]]
    for t in range(4):
        out += [res[nm][t] for nm in names]
    return tuple(out)
```

```python
import functools
import math

import jax
import jax.numpy as jnp
from jax import lax
from jax.experimental import pallas as pl
from jax.experimental.pallas import tpu as pltpu

F32 = jnp.float32
MXU = jnp.bfloat16
COMM = jnp.bfloat16
EPS = 1e-6
NDEV = 8
HEAD_DIM = 64
CHUNK = 128
GROUPS = 8
QBLK = 128
SCALE = HEAD_DIM ** -0.5
TOKEN_TILE = 256
ADAM_LR = 0.001
ADAM_B1 = 0.9
ADAM_B2 = 0.999
ADAM_EPS = 1e-08
ADAM_WD = 0.01
ADAM_STEP = 10
MESH = pl.DeviceIdType.MESH


def _call(body, **kw):
    return pl.pallas_call(body, **kw)


def _params(vmem_mb, n_axes=1):
    return pltpu.CompilerParams(dimension_semantics=("arbitrary",) * n_axes,
                                vmem_limit_bytes=vmem_mb << 20)


def _tile(tm, n):
    return pl.BlockSpec((tm, n), lambda i: (i, 0))


def _whole(shape):
    zeros = (0,) * len(shape)
    return pl.BlockSpec(shape, lambda i: zeros, pipeline_mode=pl.Buffered(1))


def _acc(shape):
    zeros = (0,) * len(shape)
    return pl.BlockSpec(shape, lambda i: zeros)


def _mm(a, b):
    return jnp.dot(a.astype(MXU), b.astype(MXU), preferred_element_type=F32)


def _mm_nt(a, b):
    return lax.dot_general(a.astype(MXU), b.astype(MXU), (((1,), (1,)), ((), ())),
                           preferred_element_type=F32)


def _mm_tn(a, b):
    return lax.dot_general(a.astype(MXU), b.astype(MXU), (((0,), (0,)), ((), ())),
                           preferred_element_type=F32)


def _split_dot(x, ones, terms=2):
    out = None
    for _ in range(terms):
        part = x.astype(MXU)
        x = x - part.astype(F32)
        d = jnp.dot(part, ones, preferred_element_type=F32)
        out = d if out is None else out + d
    return out


def _rms(x, g):
    rstd = lax.rsqrt(jnp.mean(x * x, axis=-1, keepdims=True) + EPS)
    xhat = x * rstd
    return xhat * g, xhat, rstd


def _rms_bwd(dh, xhat, rstd, g):
    dxh = dh * g
    dx = rstd * (dxh - xhat * jnp.mean(dxh * xhat, axis=-1, keepdims=True))
    dg = jnp.sum(dh * xhat, axis=0, keepdims=True)
    return dx, dg


_GELU_C = math.sqrt(2.0 / math.pi)


def _gelu(x):
    t = jnp.tanh(_GELU_C * (x + 0.044715 * (x * x * x)))
    return 0.5 * x * (1.0 + t)


def _gelu_and_grad(x):
    x2 = x * x
    t = jnp.tanh(_GELU_C * (x + 0.044715 * (x2 * x)))
    g = 0.5 * x * (1.0 + t)
    dg = 0.5 * (1.0 + t) + 0.5 * x * (1.0 - t * t) * (_GELU_C * (1.0 + 3.0 * 0.044715 * x2))
    return g, dg


def _softplus(z):
    return jnp.maximum(z, 0.0) + jnp.log(1.0 + jnp.exp(-jnp.abs(z)))


def _tril_mask():
    row = lax.broadcasted_iota(jnp.int32, (CHUNK, CHUNK), 0)
    col = lax.broadcasted_iota(jnp.int32, (CHUNK, CHUNK), 1)
    return row >= col, row <= col


ANY_SPEC = pl.BlockSpec(memory_space=pl.ANY)


def _my_index():
    return 4 * lax.axis_index("x") + 2 * lax.axis_index("y") + lax.axis_index("c")


def _exchange_copies(srcs, lands, send_sems, recv_sems, scatter, arriving):
    x, y, c = lax.axis_index("x"), lax.axis_index("y"), lax.axis_index("c")
    me = 4 * x + 2 * y + c
    out = []
    for a in range(len(srcs)):
        for k in range(NDEV - 1):
            bits = k + 1
            px = 1 - x if (bits >> 2) & 1 else x
            py = 1 - y if (bits >> 1) & 1 else y
            pc = 1 - c if bits & 1 else c
            peer = 4 * px + 2 * py + pc
            src = srcs[a].at[peer] if scatter else srcs[a]
            out.append(pltpu.make_async_remote_copy(
                src_ref=src, dst_ref=lands[a].at[peer if arriving else me],
                send_sem=send_sems.at[a * (NDEV - 1) + k], recv_sem=recv_sems.at[a * (NDEV - 1) + k],
                device_id=(px, py, pc), device_id_type=MESH))
    return out


def _exchange_shapes(arrs, scatter):
    n = len(arrs)
    if n == 0:
        return [], []
    lands = [jax.ShapeDtypeStruct(a.shape if scatter else (NDEV,) + a.shape, a.dtype) for a in arrs]
    sems = [pltpu.SemaphoreType.DMA((n * (NDEV - 1),)), pltpu.SemaphoreType.DMA((n * (NDEV - 1),)),
            pltpu.SemaphoreType.DMA((n,))]
    return lands, sems


def _exchange_start(srcs, lands, sems, scatter):
    send_sems, recv_sems, local_sems = sems
    me = _my_index()
    for a in range(len(srcs)):
        pltpu.make_async_copy(srcs[a].at[me] if scatter else srcs[a], lands[a].at[me], local_sems.at[a]).start()
    for send in _exchange_copies(srcs, lands, send_sems, recv_sems, scatter, False):
        send.start()


def _exchange_finish(srcs, lands, sems, scatter):
    send_sems, recv_sems, local_sems = sems
    me = _my_index()
    for arrive in _exchange_copies(srcs, lands, send_sems, recv_sems, scatter, True):
        arrive.wait_recv()
    for send in _exchange_copies(srcs, lands, send_sems, recv_sems, scatter, False):
        send.wait_send()
    for a in range(len(srcs)):
        pltpu.make_async_copy(srcs[a].at[me] if scatter else srcs[a], lands[a].at[me], local_sems.at[a]).wait()


def _gather_two_level(arrs, name):
    n = len(arrs)
    lands = [jax.ShapeDtypeStruct((NDEV,) + a.shape, a.dtype) for a in arrs]

    def body(*refs):
        srcs, outs = refs[:n], refs[n:2 * n]
        send_sems, recv_sems, local_sems = refs[2 * n:]
        x, y, c = lax.axis_index("x"), lax.axis_index("y"), lax.axis_index("c")
        me, sibling = (x, y, c), (x, y, 1 - c)
        chips = [(1 - x, y), (x, 1 - y), (1 - x, 1 - y)]

        def index(dev):
            return 4 * dev[0] + 2 * dev[1] + dev[2]

        def copy(a, k, block, to, src=None):
            dst = outs[a].at[index(block)]
            return pltpu.make_async_remote_copy(
                src_ref=dst if src is None else src, dst_ref=dst, send_sem=send_sems.at[a, k],
                recv_sem=recv_sems.at[a, k], device_id=to, device_id_type=MESH)

        mine, first, passed = [], [], []
        for a in range(n):
            cp = pltpu.make_async_copy(srcs[a], outs[a].at[index(me)], local_sems.at[a])
            cp.start()
            mine.append(cp)
            first.append(copy(a, 0, me, sibling, src=srcs[a]))
            first += [copy(a, 1 + j, me, (*chip, c), src=srcs[a]) for j, chip in enumerate(chips)]
        for cp in first:
            cp.start()
        for a in range(n):
            for j, chip in enumerate(chips):
                copy(a, 1 + j, (*chip, c), me).wait_recv()
                cp = copy(a, 4 + j, (*chip, c), sibling)
                cp.start()
                passed.append(cp)
        for a in range(n):
            copy(a, 0, sibling, me).wait_recv()
            for j, chip in enumerate(chips):
                copy(a, 4 + j, (*chip, 1 - c), me).wait_recv()
        for cp in first + passed:
            cp.wait_send()
        for cp in mine:
            cp.wait()

    return _call(body, name=name, out_shape=lands, in_specs=[ANY_SPEC] * n, out_specs=[ANY_SPEC] * n,
                 scratch_shapes=[pltpu.SemaphoreType.DMA((n, NDEV - 1)), pltpu.SemaphoreType.DMA((n, NDEV - 1)),
                                 pltpu.SemaphoreType.DMA((n,))])(*arrs)


def _scatter_pair(arrs, extra, name):
    n, ne = len(arrs), len(extra)
    lands = [jax.ShapeDtypeStruct((4,) + a.shape[2:], a.dtype) for a in arrs]
    extra_lands, extra_sems = _exchange_shapes(extra, True)

    def body(*refs):
        srcs, xsrc = refs[:n], refs[n:n + ne]
        outs, xout = refs[n + ne:2 * n + ne], refs[2 * n + ne:2 * (n + ne)]
        send_sems, recv_sems = refs[2 * (n + ne)], refs[2 * (n + ne) + 1]
        xsems = refs[2 * (n + ne) + 2:]
        x, y, c = lax.axis_index("x"), lax.axis_index("y"), lax.axis_index("c")
        _exchange_start(xsrc, xout, xsems, True)
        copies = [pltpu.make_async_remote_copy(
            src_ref=srcs[a].at[k, 1 - c], dst_ref=outs[a].at[k], send_sem=send_sems.at[a, k],
            recv_sem=recv_sems.at[a, k], device_id=(x, y, 1 - c), device_id_type=MESH)
            for a in range(n) for k in range(4)]
        for cp in copies:
            cp.start()
        for cp in copies:
            cp.wait()
        _exchange_finish(xsrc, xout, xsems, True)

    outs = _call(
        body, name=name, out_shape=lands + extra_lands, in_specs=[ANY_SPEC] * (n + ne), out_specs=[ANY_SPEC] * (n + ne),
        scratch_shapes=[pltpu.SemaphoreType.DMA((n, 4)), pltpu.SemaphoreType.DMA((n, 4))] + extra_sems,
    )(*arrs, *extra)
    return outs[:n], outs[n:]


def _pair_sum(own, other, name):
    _, R, C = own.shape
    tr = math.gcd(R, max(8, (128 * 1024) // C))

    def body(a_ref, b_ref, o_ref):
        o_ref[...] = (a_ref[...].astype(F32) + b_ref[...].astype(F32)).astype(COMM)

    blk = pl.BlockSpec((4, tr, C), lambda i: (0, i, 0))
    return _call(body, name=name, grid=(R // tr,), out_shape=jax.ShapeDtypeStruct(own.shape, COMM),
                 in_specs=[blk, blk], out_specs=blk, compiler_params=_params(32))(own, other)


def _scatter_chips(arrs, name):
    n = len(arrs)
    lands = [jax.ShapeDtypeStruct(a.shape, a.dtype) for a in arrs]

    def body(*refs):
        srcs, outs = refs[:n], refs[n:2 * n]
        send_sems, recv_sems, local_sems = refs[2 * n:]
        x, y, c = lax.axis_index("x"), lax.axis_index("y"), lax.axis_index("c")
        chip = 2 * x + y
        others = [(1 - x, y), (x, 1 - y), (1 - x, 1 - y)]
        local = [pltpu.make_async_copy(srcs[a].at[chip], outs[a].at[chip], local_sems.at[a]) for a in range(n)]
        for cp in local:
            cp.start()

        def copies(arriving):
            return [pltpu.make_async_remote_copy(
                src_ref=srcs[a].at[2 * px + py], dst_ref=outs[a].at[2 * px + py if arriving else chip],
                send_sem=send_sems.at[a, j], recv_sem=recv_sems.at[a, j], device_id=(px, py, c), device_id_type=MESH)
                for a in range(n) for j, (px, py) in enumerate(others)]

        for cp in copies(False):
            cp.start()
        for cp in copies(True):
            cp.wait_recv()
        for cp in copies(False):
            cp.wait_send()
        for cp in local:
            cp.wait()

    return _call(body, name=name, out_shape=lands, in_specs=[ANY_SPEC] * n, out_specs=[ANY_SPEC] * n,
                 scratch_shapes=[pltpu.SemaphoreType.DMA((n, 3)), pltpu.SemaphoreType.DMA((n, 3)),
                                 pltpu.SemaphoreType.DMA((n,))])(*arrs)


def _exchange(arrs, scatter, name):
    n = len(arrs)
    lands, sems = _exchange_shapes(arrs, scatter)

    def body(*refs):
        _exchange_start(refs[:n], refs[n:2 * n], refs[2 * n:], scatter)
        _exchange_finish(refs[:n], refs[n:2 * n], refs[2 * n:], scatter)

    return _call(body, name=name, out_shape=lands, in_specs=[ANY_SPEC] * n, out_specs=[ANY_SPEC] * n,
                 scratch_shapes=sems)(*arrs)


def _spatial_mix(vnb, ws_ref, bT_ref, mix_ref, tm):
    tri, _ = _tril_mask()
    for g in range(GROUPS):
        wm = jnp.where(tri, ws_ref[g], 0.0).astype(MXU)
        cols = slice(g * CHUNK, (g + 1) * CHUNK)
        for ch in range(tm // CHUNK):
            rows = slice(ch * CHUNK, (ch + 1) * CHUNK)
            mix_ref[rows, cols] = _mm(wm, vnb[rows, cols]) + bT_ref[:, g:g + 1]


def _sgu_fwd(x, ln, w_in, g_v, ws, bT, w_out, name):
    T, D = x.shape
    tm = min(TOKEN_TILE, T)
    nw = w_in.shape[2]

    def body(x_ref, ln_ref, win_ref, gv_ref, ws_ref, bT_ref, wout_ref, xo_ref, z_ref, h_ref, y_ref, mix_ref):
        xv = x_ref[...]
        h, _, _ = _rms(xv, ln_ref[...])
        hb = h.astype(MXU)
        h_ref[...] = hb
        for j in range(NDEV):
            z_ref[:, j * nw:(j + 1) * nw] = _mm(hb, win_ref[j])
        u = _gelu(z_ref[:, :D])
        gv = _gelu(z_ref[:, D:])
        vn, _, _ = _rms(gv, gv_ref[...])
        _spatial_mix(vn.astype(MXU), ws_ref, bT_ref, mix_ref, tm)
        y = (u * mix_ref[...]).astype(MXU)
        y_ref[...] = y
        xo_ref[...] = xv + _mm(y, wout_ref[...])

    return _call(
        body, name=name, grid=(T // tm,),
        out_shape=[jax.ShapeDtypeStruct((T, D), F32), jax.ShapeDtypeStruct((T, 2 * D), F32),
                   jax.ShapeDtypeStruct((T, D), MXU), jax.ShapeDtypeStruct((T, D), MXU)],
        in_specs=[_tile(tm, D), _whole(ln.shape), _whole(w_in.shape), _whole(g_v.shape), _whole(ws.shape),
                  _whole(bT.shape), _whole(w_out.shape)],
        out_specs=[_tile(tm, D), _tile(tm, 2 * D), _tile(tm, D), _tile(tm, D)],
        scratch_shapes=[pltpu.VMEM((tm, D), F32)],
        compiler_params=_params(40),
    )(x, ln, w_in, g_v, ws, bT, w_out)


def _mlp_fwd(x, ln, w_up, w_down, cargo, name):
    T, D = x.shape
    tm = min(TOKEN_TILE, T)
    nf = w_up.shape[2]
    F = nf * NDEV
    nc = len(cargo)
    lands, sems = _exchange_shapes(cargo, False)

    def body(x_ref, ln_ref, wup_ref, wdown_ref, *rest):
        xo_ref, pre_ref, h_ref = rest[nc:nc + 3]
        finish = _ride_along(rest[:nc] + rest[nc + 3:], nc, False, rank=1)
        xv = x_ref[...]
        h, _, _ = _rms(xv, ln_ref[...])
        hb = h.astype(MXU)
        h_ref[...] = hb
        for j in range(NDEV):
            pre_ref[:, j * nf:(j + 1) * nf] = _mm(hb, wup_ref[j])
        a = jnp.maximum(pre_ref[...], 0.0)
        xo_ref[...] = xv + _mm(a * a, wdown_ref[...])
        finish()

    outs = _call(
        body, name=name, grid=(T // tm,),
        out_shape=[jax.ShapeDtypeStruct((T, D), F32), jax.ShapeDtypeStruct((T, F), F32),
                   jax.ShapeDtypeStruct((T, D), MXU)] + lands,
        in_specs=[_tile(tm, D), _whole(ln.shape), _whole(w_up.shape), _whole(w_down.shape)] + [ANY_SPEC] * nc,
        out_specs=[_tile(tm, D), _tile(tm, F), _tile(tm, D)] + [ANY_SPEC] * nc,
        scratch_shapes=sems, compiler_params=_params(52),
    )(x, ln, w_up, w_down, *cargo)
    return outs[0], outs[1], outs[2], outs[3:]


def _ple_fwd(x, p, ln, w_g, w_pp, name):
    T, D = x.shape
    tm = min(TOKEN_TILE, T)
    npp = w_pp.shape[2]

    def body(x_ref, p_ref, ln_ref, wg_ref, wpp_ref, xo_ref, gate_ref, pp_ref, h_ref):
        xv = x_ref[...]
        h, _, _ = _rms(xv, ln_ref[...])
        hb = h.astype(MXU)
        h_ref[...] = hb
        gate = jax.nn.sigmoid(_mm(hb, wg_ref[...]))
        gate_ref[...] = gate
        pb = p_ref[...].astype(MXU)
        for j in range(NDEV):
            pp_ref[:, j * npp:(j + 1) * npp] = _mm(pb, wpp_ref[j])
        xo_ref[...] = xv + pp_ref[...] * gate

    return _call(
        body, name=name, grid=(T // tm,),
        out_shape=[jax.ShapeDtypeStruct((T, D), F32), jax.ShapeDtypeStruct((T, D), F32),
                   jax.ShapeDtypeStruct((T, D), F32), jax.ShapeDtypeStruct((T, D), MXU)],
        in_specs=[_tile(tm, D), _tile(tm, p.shape[1]), _whole(ln.shape), _whole(w_g.shape), _whole(w_pp.shape)],
        out_specs=[_tile(tm, D), _tile(tm, D), _tile(tm, D), _tile(tm, D)],
        compiler_params=_params(32),
    )(x, p, ln, w_g, w_pp)


def _head_ones():
    row = lax.broadcasted_iota(jnp.int32, (128, 128), 0)
    col = lax.broadcasted_iota(jnp.int32, (128, 128), 1)
    return (jnp.right_shift(row, 6) == jnp.right_shift(col, 6)).astype(MXU)


def _head_rms(x, g, ones):
    rstd = lax.rsqrt(_split_dot(x * x, ones, 3) * (1.0 / HEAD_DIM) + EPS)
    xhat = x * rstd
    return xhat * g, xhat, rstd


def _head_rms_bwd(dh, xhat, rstd, g, ones):
    dxh = dh * g
    mean = _split_dot(dxh * xhat, ones, 3) * (1.0 / HEAD_DIM)
    return rstd * (dxh - xhat * mean), jnp.sum(dh * xhat, axis=0, keepdims=True)


def _qkv_fwd(x, ln_q, ln_kv, g_q, g_k, w_q, w_kv, name):
    T, D = x.shape
    tm = min(TOKEN_TILE, T)
    nk = w_kv.shape[2]
    half = NDEV // 2

    def body(x_ref, lnq_ref, lnkv_ref, gq_ref, gk_ref, wq_ref, wkv_ref,
             q_ref, k_ref, v_ref, qpre_ref, kpre_ref, hq_ref, hkv_ref):
        xv = x_ref[...]
        _, xhat, _ = _rms(xv, lnq_ref[...])
        hq = (xhat * lnq_ref[...]).astype(MXU)
        hkv = (xhat * lnkv_ref[...]).astype(MXU)
        hq_ref[...] = hq
        hkv_ref[...] = hkv
        qpre_ref[...] = _mm(hq, wq_ref[...])
        for j in range(half):
            kpre_ref[:, j * nk:(j + 1) * nk] = _mm(hkv, wkv_ref[j])
            v_ref[:, j * nk:(j + 1) * nk] = _mm(hkv, wkv_ref[half + j]).astype(MXU)
        ones = _head_ones()
        for b in range(D // 128):
            cols = slice(b * 128, (b + 1) * 128)
            qn, _, _ = _head_rms(qpre_ref[:, cols], gq_ref[:, cols], ones)
            q_ref[:, cols] = (qn * SCALE).astype(MXU)
            kn, _, _ = _head_rms(kpre_ref[:, cols], gk_ref[:, cols], ones)
            k_ref[:, cols] = kn.astype(MXU)

    return _call(
        body, name=name, grid=(T // tm,),
        out_shape=[jax.ShapeDtypeStruct((T, D), MXU)] * 3 + [jax.ShapeDtypeStruct((T, D), F32)] * 2
        + [jax.ShapeDtypeStruct((T, D), MXU)] * 2,
        in_specs=[_tile(tm, D), _whole(ln_q.shape), _whole(ln_kv.shape), _whole(g_q.shape), _whole(g_k.shape),
                  _whole(w_q.shape), _whole(w_kv.shape)],
        out_specs=[_tile(tm, D)] * 7,
        compiler_params=_params(40),
    )(x, ln_q, ln_kv, g_q, g_k, w_q, w_kv)


SB_KEYS = 2 * QBLK


def _sb_consts():
    row = lax.broadcasted_iota(jnp.int32, (QBLK, QBLK), 0)
    col = lax.broadcasted_iota(jnp.int32, (QBLK, QBLK), 1)
    lane = lax.broadcasted_iota(jnp.int32, (QBLK, 128), 1)
    ones = jnp.ones((QBLK, QBLK), MXU)
    later = jnp.concatenate([(row > col).astype(MXU), ones], axis=1)
    later_eq = jnp.concatenate([(row >= col).astype(MXU), ones], axis=1)
    return later, later_eq, lane < HEAD_DIM


MASKED_LOG = -1e30


def _sb_window(i, w):
    upper = (i + 1) * QBLK - w * SB_KEYS
    start = pl.multiple_of(jnp.maximum(upper - SB_KEYS, 0), QBLK)
    key = lax.broadcasted_iota(jnp.int32, (2 * QBLK, SB_KEYS), 1) + start
    return start, key < upper


def _sb_diagonal():
    row = jnp.bitwise_and(lax.broadcasted_iota(jnp.int32, (2 * QBLK, SB_KEYS), 0), QBLK - 1)
    key = lax.broadcasted_iota(jnp.int32, (2 * QBLK, SB_KEYS), 1)
    cases = []
    for shift in (0, QBLK):
        seen = key < row + shift
        cases.append(jnp.stack([jnp.where(seen, 1.0, 0.0), jnp.where(seen, 0.0, MASKED_LOG)]))
    return jnp.stack(cases).astype(F32)


_SB_DIAG_SPEC = pl.BlockSpec((None, 2, 2 * QBLK, SB_KEYS), lambda h, i: (jnp.minimum(i, 1), 0, 0, 0))


def _sb_terms(x, terms):
    x = jnp.concatenate([x[:, :QBLK], x[:, QBLK:]], axis=0)
    out = []
    for _ in range(terms):
        part = x.astype(MXU)
        x = x - part.astype(F32)
        out.append(part)
    return tuple(out)


def _sb_suffix(parts, ones, carry):
    s = jnp.dot(jnp.concatenate(parts[:2], axis=1), jnp.concatenate([ones, ones], axis=0),
                preferred_element_type=F32)
    for part in parts[2:]:
        s = s + jnp.dot(part, ones, preferred_element_type=F32)
    rows = s.shape[0] // 2
    s_lo, sum_lo, s_hi, sum_hi = s[:rows, :QBLK], s[:rows, QBLK:], s[rows:, :QBLK], s[rows:, QBLK:]
    return jnp.concatenate([s_lo + (carry + sum_hi), s_hi + carry], axis=1), carry + (sum_lo + sum_hi)


def _sb_scores(z, mask):
    sp = _softplus(z)
    l, log_sig = -sp, z - sp
    if isinstance(mask, tuple):
        keep, bias = mask
        l, log_sig = l * keep, log_sig + bias
    else:
        l = jnp.where(mask, l, 0.0)
        log_sig = jnp.where(mask, log_sig, MASKED_LOG)
    return log_sig, _sb_terms(l, 2)


def _sb_weights(staged, later, c_l):
    log_sig, parts = staged
    b, c_l = _sb_suffix(parts, later, c_l)
    return jnp.exp(log_sig + b), c_l


DEAD_LOG = -88.0


def _sb_alive(carry):
    return (jnp.max(carry[0]) > DEAD_LOG).astype(jnp.int32)


def _ride_along(refs, n, scatter, rank=2):
    if n == 0:
        return lambda: None
    step, steps = 0, 1
    for d in range(rank):
        step = step * pl.num_programs(d) + pl.program_id(d)
        steps = steps * pl.num_programs(d)
    srcs, lands, sems = refs[:n], refs[n:2 * n], refs[2 * n:]

    @pl.when(step == 0)
    def _():
        _exchange_start(srcs, lands, sems, scatter)

    def finish():
        @pl.when(step == steps - 1)
        def _():
            _exchange_finish(srcs, lands, sems, scatter)

    return finish


def _sb_fwd(q, k, v, cargo, name):
    T, D = q.shape
    nc = len(cargo)
    lands, sems = _exchange_shapes(cargo, False)

    def body(diag_ref, q_ref, k_ref, v_ref, *rest):
        o_ref = rest[nc]
        finish = _ride_along(rest[:nc] + rest[nc + 1:], nc, False)
        i = pl.program_id(1)
        n_steps = (i + 2) // 2
        later, _, first = _sb_consts()
        qv = q_ref[...]
        zero = jnp.zeros_like(qv)
        q2 = jnp.concatenate([jnp.where(first, qv, zero), jnp.where(first, zero, qv)], axis=0)

        def window(w, carry, diagonal):
            start, mask = _sb_window(i, w)
            if diagonal:
                mask = (diag_ref[0], diag_ref[1])
            kw = k_ref[pl.ds(start, SB_KEYS), :]
            vw = v_ref[pl.ds(start, SB_KEYS), :]
            c_l, acc = carry
            a, c_l = _sb_weights(_sb_scores(_mm_nt(q2, kw), mask), later, c_l)
            return c_l, acc + _mm(a, vw)

        def step(state):
            w, _, carry = state
            carry = window(w, carry, False)
            return w + 1, _sb_alive(carry), carry

        carry = window(0, (jnp.zeros((2 * QBLK, 128), F32),) * 2, True)
        _, _, carry = lax.while_loop(lambda s: (s[0] < n_steps) & (s[1] > 0), step,
                                     (jnp.int32(1), _sb_alive(carry), carry))
        o_ref[...] = jnp.where(first, carry[1][:QBLK], carry[1][QBLK:])
        finish()

    qblk = pl.BlockSpec((QBLK, 128), lambda h, i: (i, h))
    kblk = pl.BlockSpec((T, 128), lambda h, i: (0, h))
    outs = _call(
        body, name=name, grid=(D // 128, T // QBLK), out_shape=[jax.ShapeDtypeStruct((T, D), F32)] + lands,
        in_specs=[_SB_DIAG_SPEC, qblk, kblk, kblk] + [ANY_SPEC] * nc, out_specs=[qblk] + [ANY_SPEC] * nc,
        scratch_shapes=sems, compiler_params=_params(32, 2),
    )(_sb_diagonal(), q, k, v, *cargo)
    return outs[0], outs[1:]


def _sb_bwd(q, k, v, o, do, cargo, name):
    T, D = q.shape
    nc = len(cargo)
    lands, sems = _exchange_shapes(cargo, True)

    def body(diag_ref, q_ref, k_ref, v_ref, o_ref, do_ref, *rest):
        dq_ref, dk_ref, dv_ref = rest[nc:nc + 3]
        finish = _ride_along(rest[:nc] + rest[nc + 3:], nc, True)
        i = pl.program_id(1)

        @pl.when(i == 0)
        def _():
            dk_ref[...] = jnp.zeros_like(dk_ref)
            dv_ref[...] = jnp.zeros_like(dv_ref)

        n_steps = (i + 2) // 2
        later, later_eq, first = _sb_consts()
        qv = q_ref[...]
        dob = do_ref[...].astype(MXU)
        zero = jnp.zeros_like(qv)
        q2 = jnp.concatenate([jnp.where(first, qv, zero), jnp.where(first, zero, qv)], axis=0)
        do2 = jnp.concatenate([jnp.where(first, dob, zero), jnp.where(first, zero, dob)], axis=0)
        prod = o_ref[...] * dob.astype(F32)
        prod2 = jnp.concatenate([jnp.where(first, prod, 0.0), jnp.where(first, 0.0, prod)], axis=0)
        total = _split_dot(prod2, jnp.ones((128, 128), MXU), 3)
        total = jnp.concatenate([total, total], axis=1)

        def window(w, carry, diagonal):
            start, mask = _sb_window(i, w)
            if diagonal:
                mask = (diag_ref[0], diag_ref[1])
            kw = k_ref[pl.ds(start, SB_KEYS), :]
            vw = v_ref[pl.ds(start, SB_KEYS), :]
            c_l, c_e, dq = carry
            log_sig, parts = _sb_scores(_mm_nt(q2, kw), mask)
            a, c_l = _sb_weights((log_sig, parts), later, c_l)
            ab = a.astype(MXU)
            e = ab.astype(F32) * _mm_nt(do2, vw)
            from_here, c_e = _sb_suffix(_sb_terms(e, 2), later_eq, c_e)
            sig = jnp.exp(log_sig)
            dzb = (e * (1.0 - sig) - sig * (total - from_here)).astype(MXU)
            dk_ref[pl.ds(start, SB_KEYS), :] += _mm_tn(dzb, q2)
            dv_ref[pl.ds(start, SB_KEYS), :] += _mm_tn(ab, do2)
            return c_l, c_e, dq + _mm(dzb, kw)

        def step(state):
            w, _, carry = state
            carry = window(w, carry, False)
            return w + 1, _sb_alive(carry), carry

        carry = window(0, (jnp.zeros((2 * QBLK, 128), F32),) * 3, True)
        _, _, carry = lax.while_loop(lambda s: (s[0] < n_steps) & (s[1] > 0), step,
                                     (jnp.int32(1), _sb_alive(carry), carry))
        dq_ref[...] = jnp.where(first, carry[2][:QBLK], carry[2][QBLK:]) * SCALE
        finish()

    qblk = pl.BlockSpec((QBLK, 128), lambda h, i: (i, h))
    kblk = pl.BlockSpec((T, 128), lambda h, i: (0, h))
    full = jax.ShapeDtypeStruct((T, D), F32)
    outs = _call(
        body, name=name, grid=(D // 128, T // QBLK), out_shape=[full, full, full] + lands,
        in_specs=[_SB_DIAG_SPEC, qblk, kblk, kblk, qblk, qblk] + [ANY_SPEC] * nc,
        out_specs=[qblk, kblk, kblk] + [ANY_SPEC] * nc, scratch_shapes=sems, compiler_params=_params(32, 2),
    )(_sb_diagonal(), q, k, v, o, do, *cargo)
    return outs[0], outs[1], outs[2], outs[3:]


def _proj_res(x, a, w, name):
    T, D = x.shape
    tm = min(TOKEN_TILE, T)

    def body(x_ref, a_ref, w_ref, o_ref):
        o_ref[...] = x_ref[...] + _mm(a_ref[...], w_ref[...])

    return _call(
        body, name=name, grid=(T // tm,), out_shape=jax.ShapeDtypeStruct((T, D), F32),
        in_specs=[_tile(tm, D), _tile(tm, a.shape[1]), _whole(w.shape)], out_specs=_tile(tm, D),
        compiler_params=_params(32),
    )(x, a, w)


def _proj_nt(g, w, name):
    T = g.shape[0]
    K = w.shape[0]
    tm = min(TOKEN_TILE, T)

    def body(g_ref, w_ref, o_ref):
        o_ref[...] = _mm_nt(g_ref[...], w_ref[...])

    return _call(
        body, name=name, grid=(T // tm,), out_shape=jax.ShapeDtypeStruct((T, K), F32),
        in_specs=[_tile(tm, g.shape[1]), _whole(w.shape)], out_specs=_tile(tm, K),
        compiler_params=_params(32),
    )(g, w)


def _loss_grad(y, tgt, name):
    T, D = y.shape
    tm = min(TOKEN_TILE, T)

    def body(y_ref, t_ref, dy_ref, loss_ref):
        @pl.when(pl.program_id(0) == 0)
        def _():
            loss_ref[...] = jnp.zeros_like(loss_ref)
        diff = y_ref[...] - t_ref[...]
        dy_ref[...] = diff * (1.0 / D)
        rows = jnp.sum(diff * diff, axis=1, keepdims=True) * (1.0 / D)
        loss_ref[...] += 0.5 * jnp.sum(rows, axis=0, keepdims=True)

    return _call(
        body, name=name, grid=(T // tm,),
        out_shape=[jax.ShapeDtypeStruct((T, D), F32), jax.ShapeDtypeStruct((1, 1), F32)],
        in_specs=[_tile(tm, D), _tile(tm, D)], out_specs=[_tile(tm, D), _acc((1, 1))],
        compiler_params=_params(32),
    )(y, tgt)


def _ple_bwd(dx, x, gate, pp, ln, w_g, name):
    T, D = x.shape
    tm = min(TOKEN_TILE, T)

    def body(dx_ref, x_ref, gate_ref, pp_ref, ln_ref, wg_ref, dxo_ref, dpp_ref, dgp_ref, dln_ref):
        @pl.when(pl.program_id(0) == 0)
        def _():
            dln_ref[...] = jnp.zeros_like(dln_ref)
        dxv = dx_ref[...]
        gate = gate_ref[...]
        _, xhat, rstd = _rms(x_ref[...], ln_ref[...])
        dpp_ref[...] = (dxv * gate).astype(MXU)
        dgp = (dxv * pp_ref[...] * gate * (1.0 - gate)).astype(MXU)
        dgp_ref[...] = dgp
        dxn, dln = _rms_bwd(_mm_nt(dgp, wg_ref[...]), xhat, rstd, ln_ref[...])
        dln_ref[...] += dln
        dxo_ref[...] = dxn + dxv

    return _call(
        body, name=name, grid=(T // tm,),
        out_shape=[jax.ShapeDtypeStruct((T, D), F32), jax.ShapeDtypeStruct((T, D), MXU),
                   jax.ShapeDtypeStruct((T, D), MXU), jax.ShapeDtypeStruct(ln.shape, F32)],
        in_specs=[_tile(tm, D)] * 4 + [_whole(ln.shape), _whole(w_g.shape)],
        out_specs=[_tile(tm, D), _tile(tm, D), _tile(tm, D), _acc(ln.shape)],
        compiler_params=_params(32),
    )(dx, x, gate, pp, ln, w_g)


def _mlp_bwd(dx, x, pre, ln, w_up, w_down, cargo, name):
    T, D = x.shape
    tm = min(TOKEN_TILE, T)
    nf = w_up.shape[2]
    F = nf * NDEV
    nc = len(cargo)
    lands, sems = _exchange_shapes(cargo, True)

    def body(dx_ref, x_ref, pre_ref, ln_ref, wup_ref, wdown_ref, *rest):
        dxo_ref, dpre_ref, s_ref, dln_ref = rest[nc:nc + 4]
        finish = _ride_along(rest[:nc] + rest[nc + 4:], nc, True, rank=1)

        @pl.when(pl.program_id(0) == 0)
        def _():
            dln_ref[...] = jnp.zeros_like(dln_ref)
        dxv = dx_ref[...]
        _, xhat, rstd = _rms(x_ref[...], ln_ref[...])
        a = jnp.maximum(pre_ref[...], 0.0)
        s_ref[...] = (a * a).astype(MXU)
        dpre_ref[...] = (_mm_nt(dxv, wdown_ref[...]) * (2.0 * a)).astype(MXU)
        dh = _mm_nt(dpre_ref[:, :nf], wup_ref[0])
        for j in range(1, NDEV):
            dh += _mm_nt(dpre_ref[:, j * nf:(j + 1) * nf], wup_ref[j])
        dxn, dln = _rms_bwd(dh, xhat, rstd, ln_ref[...])
        dln_ref[...] += dln
        dxo_ref[...] = dxn + dxv
        finish()

    outs = _call(
        body, name=name, grid=(T // tm,),
        out_shape=[jax.ShapeDtypeStruct((T, D), F32), jax.ShapeDtypeStruct((T, F), MXU),
                   jax.ShapeDtypeStruct((T, F), MXU), jax.ShapeDtypeStruct(ln.shape, F32)] + lands,
        in_specs=[_tile(tm, D), _tile(tm, D), _tile(tm, F), _whole(ln.shape), _whole(w_up.shape),
                  _whole(w_down.shape)] + [ANY_SPEC] * nc,
        out_specs=[_tile(tm, D), _tile(tm, F), _tile(tm, F), _acc(ln.shape)] + [ANY_SPEC] * nc,
        scratch_shapes=sems, compiler_params=_params(56),
    )(dx, x, pre, ln, w_up, w_down, *cargo)
    return outs[0], outs[1], outs[2], outs[3], outs[4:]


def _qkv_bwd(dx, x, dq, dk, dv, q_pre, k_pre, ln_q, ln_kv, g_q, g_k, w_q, w_kv, cargo, name):
    T, D = x.shape
    tm = min(TOKEN_TILE, T)
    nk = w_kv.shape[2]
    n_tiles = T // tm
    nc = len(cargo)
    lands, sems = _exchange_shapes(cargo, True)

    def body(dx_ref, x_ref, dq_ref, dk_ref, dv_ref, qpre_ref, kpre_ref, lnq_ref, lnkv_ref, gq_ref, gk_ref,
             wq_ref, wkv_ref, *rest):
        dxo_ref, dqp_ref, dkv_ref, dlnq_ref, dlnkv_ref, dgq_ref, dgk_ref = rest[nc:nc + 7]
        gq_acc, gk_acc = rest[2 * nc + 7:2 * nc + 9]
        finish = _ride_along(rest[:nc] + rest[nc + 7:2 * nc + 7] + rest[2 * nc + 9:], nc, True, rank=1)
        i = pl.program_id(0)

        @pl.when(i == 0)
        def _():
            dlnq_ref[...] = jnp.zeros_like(dlnq_ref)
            dlnkv_ref[...] = jnp.zeros_like(dlnkv_ref)
            gq_acc[...] = jnp.zeros_like(gq_acc)
            gk_acc[...] = jnp.zeros_like(gk_acc)

        ones = _head_ones()
        for b in range(D // 128):
            cols = slice(b * 128, (b + 1) * 128)
            _, xh, rs = _head_rms(qpre_ref[:, cols], gq_ref[:, cols], ones)
            d, dg = _head_rms_bwd(dq_ref[:, cols], xh, rs, gq_ref[:, cols], ones)
            dqp_ref[:, cols] = d.astype(MXU)
            gq_acc[:, cols] += dg
            _, xh, rs = _head_rms(kpre_ref[:, cols], gk_ref[:, cols], ones)
            d, dg = _head_rms_bwd(dk_ref[:, cols], xh, rs, gk_ref[:, cols], ones)
            dkv_ref[:, cols] = d.astype(MXU)
            gk_acc[:, cols] += dg
        dkv_ref[:, D:] = dv_ref[...].astype(MXU)

        _, xhat, rstd = _rms(x_ref[...], lnq_ref[...])
        dhq = _mm_nt(dqp_ref[...], wq_ref[...])
        dhkv = _mm_nt(dkv_ref[:, :nk], wkv_ref[0])
        for j in range(1, NDEV):
            dhkv += _mm_nt(dkv_ref[:, j * nk:(j + 1) * nk], wkv_ref[j])
        dxq, dlnq = _rms_bwd(dhq, xhat, rstd, lnq_ref[...])
        dxkv, dlnkv = _rms_bwd(dhkv, xhat, rstd, lnkv_ref[...])
        dlnq_ref[...] += dlnq
        dlnkv_ref[...] += dlnkv
        dxo_ref[...] = dx_ref[...] + dxq + dxkv

        @pl.when(i == n_tiles - 1)
        def _():
            row = lax.broadcasted_iota(jnp.int32, (D, 128), 0)
            col = lax.broadcasted_iota(jnp.int32, (D, 128), 1)
            fold = (jnp.bitwise_and(row, HEAD_DIM - 1) == col).astype(MXU)
            dgq_ref[...] = _split_dot(jnp.broadcast_to(gq_acc[...], (8, D)), fold, 3)
            dgk_ref[...] = _split_dot(jnp.broadcast_to(gk_acc[...], (8, D)), fold, 3)

        finish()

    small = jax.ShapeDtypeStruct((8, 128), F32)
    outs = _call(
        body, name=name, grid=(n_tiles,),
        out_shape=[jax.ShapeDtypeStruct((T, D), F32), jax.ShapeDtypeStruct((T, D), MXU),
                   jax.ShapeDtypeStruct((T, 2 * D), MXU), jax.ShapeDtypeStruct(ln_q.shape, F32),
                   jax.ShapeDtypeStruct(ln_kv.shape, F32), small, small] + lands,
        in_specs=[_tile(tm, D)] * 7 + [_whole(ln_q.shape), _whole(ln_kv.shape), _whole(g_q.shape),
                                       _whole(g_k.shape), _whole(w_q.shape), _whole(w_kv.shape)] + [ANY_SPEC] * nc,
        out_specs=[_tile(tm, D), _tile(tm, D), _tile(tm, 2 * D), _acc(ln_q.shape), _acc(ln_kv.shape),
                   _acc((8, 128)), _acc((8, 128))] + [ANY_SPEC] * nc,
        scratch_shapes=[pltpu.VMEM((1, D), F32), pltpu.VMEM((1, D), F32)] + sems,
        compiler_params=_params(48),
    )(dx, x, dq, dk, dv, q_pre, k_pre, ln_q, ln_kv, g_q, g_k, w_q, w_kv, *cargo)
    return outs[:7], outs[7:]


def _sgu_bwd(dx, x, z, ln, w_in, g_v, ws, wsT, bT, w_out, cargo, name):
    T, D = x.shape
    tm = min(TOKEN_TILE, T)
    nw = w_in.shape[2]
    nc = len(cargo)
    lands, sems = _exchange_shapes(cargo, True)

    def body(dx_ref, x_ref, z_ref, ln_ref, win_ref, gv_ref, ws_ref, wsT_ref, bT_ref, wout_ref, *rest):
        dxo_ref, dz_ref, dws_ref, dbT_ref, dln_ref, dgv_ref = rest[nc:nc + 6]
        mix_ref, dvn_ref = rest[2 * nc + 6:2 * nc + 8]
        finish = _ride_along(rest[:nc] + rest[nc + 6:2 * nc + 6] + rest[2 * nc + 8:], nc, True, rank=1)

        @pl.when(pl.program_id(0) == 0)
        def _():
            dws_ref[...] = jnp.zeros_like(dws_ref)
            dbT_ref[...] = jnp.zeros_like(dbT_ref)
            dln_ref[...] = jnp.zeros_like(dln_ref)
            dgv_ref[...] = jnp.zeros_like(dgv_ref)
        dxv = dx_ref[...]
        _, xhat, rstd = _rms(x_ref[...], ln_ref[...])
        u, du = _gelu_and_grad(z_ref[:, :D])
        gv, dgv = _gelu_and_grad(z_ref[:, D:])
        vn, vhat, rstd_v = _rms(gv, gv_ref[...])
        vnb = vn.astype(MXU)
        _spatial_mix(vnb, ws_ref, bT_ref, mix_ref, tm)
        dy = _mm_nt(dxv, wout_ref[...])
        d_u = dy * mix_ref[...]
        d_mix = dy * u
        dmb = d_mix.astype(MXU)
        tri, triT = _tril_mask()
        for g in range(GROUPS):
            wmT = jnp.where(triT, wsT_ref[g], 0.0).astype(MXU)
            cols = slice(g * CHUNK, (g + 1) * CHUNK)
            for ch in range(tm // CHUNK):
                rows = slice(ch * CHUNK, (ch + 1) * CHUNK)
                dm = dmb[rows, cols]
                dws_ref[g] += jnp.where(tri, _mm_nt(dm, vnb[rows, cols]), 0.0)
                dbT_ref[:, g:g + 1] += jnp.sum(d_mix[rows, cols], axis=1, keepdims=True)
                dvn_ref[rows, cols] = _mm(wmT, dm)
        d_gv, dg = _rms_bwd(dvn_ref[...], vhat, rstd_v, gv_ref[...])
        dgv_ref[...] += dg
        dz_ref[:, :D] = (d_u * du).astype(MXU)
        dz_ref[:, D:] = (d_gv * dgv).astype(MXU)
        dh = _mm_nt(dz_ref[:, :nw], win_ref[0])
        for j in range(1, NDEV):
            dh += _mm_nt(dz_ref[:, j * nw:(j + 1) * nw], win_ref[j])
        dxn, dln = _rms_bwd(dh, xhat, rstd, ln_ref[...])
        dln_ref[...] += dln
        dxo_ref[...] = dxn + dxv
        finish()

    outs = _call(
        body, name=name, grid=(T // tm,),
        out_shape=[jax.ShapeDtypeStruct((T, D), F32), jax.ShapeDtypeStruct((T, 2 * D), MXU),
                   jax.ShapeDtypeStruct(ws.shape, F32), jax.ShapeDtypeStruct(bT.shape, F32),
                   jax.ShapeDtypeStruct(ln.shape, F32), jax.ShapeDtypeStruct(g_v.shape, F32)] + lands,
        in_specs=[_tile(tm, D), _tile(tm, D), _tile(tm, 2 * D), _whole(ln.shape), _whole(w_in.shape),
                  _whole(g_v.shape), _whole(ws.shape), _whole(wsT.shape), _whole(bT.shape), _whole(w_out.shape)]
        + [ANY_SPEC] * nc,
        out_specs=[_tile(tm, D), _tile(tm, 2 * D), _acc(ws.shape), _acc(bT.shape), _acc(ln.shape),
                   _acc(g_v.shape)] + [ANY_SPEC] * nc,
        scratch_shapes=[pltpu.VMEM((tm, D), F32), pltpu.VMEM((tm, D), F32)] + sems,
        compiler_params=_params(48),
    )(dx, x, z, ln, w_in, g_v, ws, wsT, bT, w_out, *cargo)
    return outs[:6], outs[6:]


def _wgrad_rows(a, g, name):
    T, K = a.shape
    N = g.shape[1]
    kb = K // NDEV

    def body(a_ref, g_ref, o_ref):
        o_ref[...] = _mm_tn(a_ref[...], g_ref[...]).astype(COMM)

    return _call(
        body, name=name, grid=(NDEV,), out_shape=jax.ShapeDtypeStruct((K, N), COMM),
        in_specs=[pl.BlockSpec((T, kb), lambda j: (0, j)), _whole(g.shape)],
        out_specs=pl.BlockSpec((kb, N), lambda j: (j, 0)),
        compiler_params=_params(40),
    )(a, g).reshape(NDEV, kb, N)


def _wgrad_cols(a, g, name):
    T, K = a.shape
    N = g.shape[1]
    nb = N // NDEV

    def body(a_ref, g_ref, o_ref):
        o_ref[...] = _mm_tn(a_ref[...], g_ref[...]).astype(COMM)

    return _call(
        body, name=name, grid=(NDEV,), out_shape=jax.ShapeDtypeStruct((NDEV, K, nb), COMM),
        in_specs=[_whole(a.shape), pl.BlockSpec((T, nb), lambda j: (0, j))],
        out_specs=pl.BlockSpec((None, K, nb), lambda j: (j, 0, 0)),
        compiler_params=_params(40),
    )(a, g)


def _adamw_rows(R, C):
    tr = math.gcd(R, max(8, (128 * 1024) // C))
    return R if tr < 64 else tr


def _adamw_update(w_ref, m_ref, v_ref, s_ref, g_ref, d_ref, mo_ref, vo_ref):
    g = s_ref[0].astype(F32)
    for j in range(1, s_ref.shape[0]):
        g = g + s_ref[j].astype(F32)
    mn = ADAM_B1 * m_ref[...] + (1.0 - ADAM_B1) * g
    vn = ADAM_B2 * v_ref[...] + (1.0 - ADAM_B2) * (g * g)
    g_ref[...] = g
    mo_ref[...] = mn
    vo_ref[...] = vn
    m_hat = mn / (1.0 - ADAM_B1 ** ADAM_STEP)
    v_hat = vn / (1.0 - ADAM_B2 ** ADAM_STEP)
    d_ref[...] = -ADAM_LR * (m_hat / (jnp.sqrt(v_hat) + ADAM_EPS) + ADAM_WD * w_ref[...])


def _adamw_layers(w, m, v, slots0, slots1, name):
    _, R, C = w.shape
    tr = _adamw_rows(R, C)
    last = R // tr - 1

    def body(w_ref, m_ref, v_ref, s0_ref, s1_ref, *outs):
        @pl.when(pl.program_id(0) == 0)
        def _():
            _adamw_update(w_ref, m_ref, v_ref, s0_ref, *outs)

        @pl.when(pl.program_id(0) == 1)
        def _():
            _adamw_update(w_ref, m_ref, v_ref, s1_ref, *outs)

    blk = pl.BlockSpec((None, tr, C), lambda l, i: (l, i, 0))
    s0_blk = pl.BlockSpec((slots0.shape[0], tr, C), lambda l, i: (0, jnp.where(l == 0, i, last), 0))
    s1_blk = pl.BlockSpec((slots1.shape[0], tr, C), lambda l, i: (0, jnp.where(l == 1, i, 0), 0))
    out = jax.ShapeDtypeStruct(w.shape, F32)
    return _call(
        body, name=name, grid=(2, R // tr), out_shape=[out, out, out, out],
        in_specs=[blk, blk, blk, s0_blk, s1_blk], out_specs=[blk] * 4, compiler_params=_params(32, 2),
    )(w, m, v, slots0, slots1)


def _adamw(w, m, v, slots, name):
    R, C = w.shape
    n = slots.shape[0]
    tr = _adamw_rows(R, C)

    def body(*refs):
        _adamw_update(*refs)

    blk = pl.BlockSpec((tr, C), lambda i: (i, 0))
    out = jax.ShapeDtypeStruct((R, C), F32)
    return _call(
        body, name=name, grid=(R // tr,), out_shape=[out, out, out, out],
        in_specs=[blk, blk, blk, pl.BlockSpec((n, tr, C), lambda i: (0, i, 0))], out_specs=[blk] * 4,
        compiler_params=_params(32),
    )(w, m, v, slots)


def _rows128(a):
    flat = a.reshape(-1)
    rows = -(-flat.shape[0] // 1024) * 8
    flat = jnp.pad(flat, (0, rows * 128 - flat.shape[0]))
    return flat.reshape(rows, 128)


def kernel(x, p, ln_mix_a, w_in_a, g_v_a, w_spatial, b_spatial, w_out_a, ln_kv, w_kv, g_k, ln_mix_b, w_q, g_q, w_out_b, ln_mlp, w_up, w_down, ln_ple, w_ple_gate, w_ple_proj, loss_target, m_ln_mix_a, m_w_in_a, m_g_v_a, m_w_spatial, m_b_spatial, m_w_out_a, m_ln_kv, m_w_kv, m_g_k, m_ln_mix_b, m_w_q, m_g_q, m_w_out_b, m_ln_mlp, m_w_up, m_w_down, m_ln_ple, m_w_ple_gate, m_w_ple_proj, v_ln_mix_a, v_w_in_a, v_g_v_a, v_w_spatial, v_b_spatial, v_w_out_a, v_ln_kv, v_w_kv, v_g_k, v_ln_mix_b, v_w_q, v_g_q, v_w_out_b, v_ln_mlp, v_w_up, v_w_down, v_ln_ple, v_w_ple_gate, v_w_ple_proj):
    me = 4 * lax.axis_index("x") + 2 * lax.axis_index("y") + lax.axis_index("c")
    D = x.shape[2]
    x0, tgt = x[0], loss_target[0]
    n_layers = w_up.shape[0]

    c = lambda w: w.astype(COMM)
    first = [c(w_in_a[0]), c(w_out_a[0]), ln_mix_a, g_v_a, c(w_up[0]), c(w_down[0]), c(w_ple_gate[0]),
             c(w_ple_proj[0]), c(w_q[0]), c(w_kv)]
    second_small = [c(w_out_b[0]), c(w_ple_gate[1]), c(w_ple_proj[1])]
    second_big = [c(w_up[1]), c(w_down[1])]
    W_in, W_out_a, ln_a, gv_a, W_up0, W_down0, W_g0, W_pp0, W_q, W_kv = _gather_two_level(first, "gather_first")
    W_out_a, ln_a, gv_a = W_out_a.reshape(-1, D), ln_a.reshape(1, D), gv_a.reshape(1, D)
    W_down0, W_g0, W_q = W_down0.reshape(-1, D), W_g0.reshape(-1, D), W_q.reshape(-1, D)
    ws = w_spatial[0]
    wsT = jnp.swapaxes(ws, 1, 2)
    bT = b_spatial[0].T
    ln_kv2, ln_b = ln_kv.reshape(1, D), ln_mix_b
    gk2 = jnp.tile(g_k.reshape(1, HEAD_DIM), (1, D // HEAD_DIM))
    gq2 = jnp.tile(g_q, (1, D // HEAD_DIM))
    ln_m = [ln_mlp[l:l + 1] for l in range(n_layers)]
    ln_p = [ln_ple[l:l + 1] for l in range(n_layers)]

    x1, z, h_a, y_a = _sgu_fwd(x0, ln_a, W_in, gv_a, ws, bT, W_out_a, "sgu_fwd")
    x2, pre0, hm0, (W_out_b, W_g1, W_pp1) = _mlp_fwd(x1, ln_m[0], W_up0, W_down0, second_small, "mlp_fwd0")
    x3, gate0, pp0, hp0 = _ple_fwd(x2, p[0, 0], ln_p[0], W_g0, W_pp0, "ple_fwd0")
    qn, kn, vn, q_pre, k_pre, h_q, h_kv = _qkv_fwd(x3, ln_b, ln_kv2, gq2, gk2, W_q, W_kv, "qkv_fwd")
    o2d, (W_up1, W_down1) = _sb_fwd(qn, kn, vn, second_big, "sb_fwd")
    W_out_b, W_down1, W_g1 = W_out_b.reshape(-1, D), W_down1.reshape(-1, D), W_g1.reshape(-1, D)
    x4 = _proj_res(x3, o2d, W_out_b, "attn_out")
    x5, pre1, hm1, _ = _mlp_fwd(x4, ln_m[1], W_up1, W_down1, [], "mlp_fwd1")
    x6, gate1, pp1, hp1 = _ple_fwd(x5, p[1, 0], ln_p[1], W_g1, W_pp1, "ple_fwd1")
    dy, loss_part = _loss_grad(x6, tgt, "loss_grad")

    dx5, dpp1, dgp1, dlnp1 = _ple_bwd(dy, x5, gate1, pp1, ln_p[1], W_g1, "ple_bwd1")
    dx4, dpre1, s1, dlnm1, _ = _mlp_bwd(dx5, x4, pre1, ln_m[1], W_up1, W_down1, [], "mlp_bwd1")
    wg_big = [_wgrad_cols(hm1, dpre1, "wg_up1"), _wgrad_rows(s1, dx5, "wg_down1")]
    wg_small = [_wgrad_rows(hp1, dgp1, "wg_gate1"), _wgrad_cols(p[1, 0].astype(MXU), dpp1, "wg_proj1"),
                _wgrad_rows(o2d, dx4, "wg_out_b")]
    do2d = _proj_nt(dx4, W_out_b, "attn_out_bwd")
    dqn, dkn, dvn, (s_up1, s_down1) = _sb_bwd(qn, kn, vn, o2d, do2d, wg_big, "sb_bwd")
    (dx3, dq_pre, dkv, dlnb, dlnkv, dgq, dgk), (s_gate1, s_proj1, s_out_b) = _qkv_bwd(
        dx4, x3, dqn, dkn, dvn, q_pre, k_pre, ln_b, ln_kv2, gq2, gk2, W_q, W_kv, wg_small, "qkv_bwd")
    dgq, dgk = dgq[:1, :HEAD_DIM], dgk[:1, :HEAD_DIM]
    dx2, dpp0, dgp0, dlnp0 = _ple_bwd(dx3, x2, gate0, pp0, ln_p[0], W_g0, "ple_bwd0")
    wg_qkv = [_wgrad_rows(h_q, dq_pre, "wg_q"), _wgrad_cols(h_kv, dkv, "wg_kv")]
    dx1, dpre0, s0, dlnm0, (s_q, s_kv) = _mlp_bwd(dx2, x1, pre0, ln_m[0], W_up0, W_down0, wg_qkv, "mlp_bwd0")
    wg_ple = [_wgrad_rows(hp0, dgp0, "wg_gate0"), _wgrad_cols(p[0, 0].astype(MXU), dpp0, "wg_proj0")]
    (dx0, dz, dws, dbT, dlna, dgva), (s_gate0, s_proj0) = _sgu_bwd(
        dx1, x0, z, ln_a, W_in, gv_a, ws, wsT, bT, W_out_a, wg_ple, "sgu_bwd")
    wg_first = [_wgrad_cols(hm0, dpre0, "wg_up0"), _wgrad_rows(s0, dx2, "wg_down0"), _wgrad_cols(h_a, dz, "wg_in_a"),
                _wgrad_rows(y_a, dx1, "wg_out_a")]

    small = [("w_spatial", dws[None], w_spatial, m_w_spatial, v_w_spatial),
             ("b_spatial", dbT.T[None], b_spatial, m_b_spatial, v_b_spatial),
             ("ln_kv", dlnkv.reshape(-1), ln_kv, m_ln_kv, v_ln_kv),
             ("g_k", dgk.reshape(-1), g_k, m_g_k, v_g_k),
             ("ln_mix_b", dlnb, ln_mix_b, m_ln_mix_b, v_ln_mix_b),
             ("g_q", dgq, g_q, m_g_q, v_g_q),
             ("ln_mlp", jnp.concatenate([dlnm0, dlnm1]), ln_mlp, m_ln_mlp, v_ln_mlp),
             ("ln_ple", jnp.concatenate([dlnp0, dlnp1]), ln_ple, m_ln_ple, v_ln_ple)]
    sharded_vec = [("ln_mix_a", dlna, ln_mix_a, m_ln_mix_a, v_ln_mix_a),
                   ("g_v_a", dgva, g_v_a, m_g_v_a, v_g_v_a)]
    packs = [[], [], [], []]
    for _, g, w, m, v in small:
        for lst, a in zip(packs, (g, w, m, v)):
            lst.append(_rows128(a))
    for _, g, w, m, v in sharded_vec:
        packs[0].append(g.reshape(NDEV, -1))
        for lst, a in zip(packs[1:], (w, m, v)):
            lst.append(jnp.broadcast_to(a, (NDEV, a.shape[1])))
    packs[0].append(_rows128(loss_part))
    for lst in packs[1:]:
        lst.append(jnp.zeros((8, 128), F32))
    g_pack, w_pack, m_pack, v_pack = (jnp.concatenate(lst) for lst in packs)
    g_pack8 = jnp.broadcast_to(g_pack[None], (NDEV,) + g_pack.shape)
    by_chip = [a.reshape((4, 2) + a.shape[1:]) for a in wg_first]
    from_sibling, (g_all,) = _scatter_pair(by_chip, [g_pack8], "scatter_pair")
    my_core = lax.axis_index("c")
    chip_sums = [_pair_sum(lax.dynamic_index_in_dim(a, my_core, 1, keepdims=False), o, f"pair_sum{j}")
                 for j, (a, o) in enumerate(zip(by_chip, from_sibling))]
    s_up0, s_down0, s_in_a, s_out_a = _scatter_chips(chip_sums, "scatter_chips")

    def upd(w, m, v, s, name):
        shape = w.shape
        outs = _adamw(w.reshape(-1, shape[-1]), m.reshape(-1, shape[-1]), v.reshape(-1, shape[-1]), s, name)
        return [o.reshape(shape) for o in outs]

    res = {}
    res["w_out_b"] = upd(w_out_b, m_w_out_b, v_w_out_b, s_out_b, "adam_out_b")
    res["w_q"] = upd(w_q, m_w_q, v_w_q, s_q, "adam_q")
    res["w_kv"] = upd(w_kv, m_w_kv, v_w_kv, s_kv, "adam_kv")
    res["w_in_a"] = upd(w_in_a, m_w_in_a, v_w_in_a, s_in_a, "adam_in_a")
    res["w_out_a"] = upd(w_out_a, m_w_out_a, v_w_out_a, s_out_a, "adam_out_a")
    res["w_up"] = _adamw_layers(w_up, m_w_up, v_w_up, s_up0, s_up1, "adam_up")
    res["w_down"] = _adamw_layers(w_down, m_w_down, v_w_down, s_down0, s_down1, "adam_down")
    res["w_ple_gate"] = _adamw_layers(w_ple_gate, m_w_ple_gate, v_w_ple_gate, s_gate0, s_gate1, "adam_gate")
    res["w_ple_proj"] = _adamw_layers(w_ple_proj, m_w_ple_proj, v_w_ple_proj, s_proj0, s_proj1, "adam_proj")

    outs = _adamw(w_pack, m_pack, v_pack, g_all, "adam_small")
    loss = outs[0][-8, 0]
    row = 0
    for nm, g, w, m, v in small:
        nrows = _rows128(w).shape[0]
        res[nm] = [o[row:row + nrows].reshape(-1)[:w.size].reshape(w.shape) for o in outs]
        row += nrows
    for nm, g, w, m, v in sharded_vec:
        res[nm] = [lax.dynamic_slice_in_dim(o[row:row + NDEV], me, 1, axis=0) for o in outs]
        row += NDEV

    names = ["ln_mix_a", "w_in_a", "g_v_a", "w_spatial", "b_spatial", "w_out_a", "ln_kv", "w_kv", "g_k", "ln_mix_b",
             "w_q", "g_q", "w_out_b", "ln_mlp", "w_up", "w_down", "ln_ple", "w_ple_gate", "w_ple_proj"]
    out = [loss, dx0[None]]
    for t in range(4):
        out += [res[nm][t] for nm in names]
    return tuple(out)
```

```python
import math

import jax
import jax.numpy as jnp
from jax import lax
from jax.experimental import pallas as pl
from jax.experimental.pallas import tpu as pltpu

F32 = jnp.float32
MXU = jnp.bfloat16
COMM = jnp.bfloat16
EPS = 1e-6
NDEV = 8
HEAD_DIM = 64
CHUNK = 128
GROUPS = 8
QBLK = 128
SCALE = HEAD_DIM ** -0.5
TOKEN_TILE = 256
ADAM_LR = 0.001
ADAM_B1 = 0.9
ADAM_B2 = 0.999
ADAM_EPS = 1e-08
ADAM_WD = 0.01
ADAM_STEP = 10
MESH = pl.DeviceIdType.MESH


def _call(body, **kw):
    return pl.pallas_call(body, **kw)


def _params(vmem_mb, n_axes=1):
    return pltpu.CompilerParams(dimension_semantics=("arbitrary",) * n_axes,
                                vmem_limit_bytes=vmem_mb << 20)


def _tile(tm, n):
    return pl.BlockSpec((tm, n), lambda i: (i, 0))


def _whole(shape):
    zeros = (0,) * len(shape)
    return pl.BlockSpec(shape, lambda i: zeros, pipeline_mode=pl.Buffered(1))


def _acc(shape):
    zeros = (0,) * len(shape)
    return pl.BlockSpec(shape, lambda i: zeros)


def _mm(a, b):
    return jnp.dot(a.astype(MXU), b.astype(MXU), preferred_element_type=F32)


def _mm_nt(a, b):
    return lax.dot_general(a.astype(MXU), b.astype(MXU), (((1,), (1,)), ((), ())),
                           preferred_element_type=F32)


def _mm_tn(a, b):
    return lax.dot_general(a.astype(MXU), b.astype(MXU), (((0,), (0,)), ((), ())),
                           preferred_element_type=F32)


def _split_dot(x, ones, terms=2):
    out = None
    for _ in range(terms):
        part = x.astype(MXU)
        x = x - part.astype(F32)
        d = jnp.dot(part, ones, preferred_element_type=F32)
        out = d if out is None else out + d
    return out


def _rms(x, g):
    rstd = lax.rsqrt(jnp.mean(x * x, axis=-1, keepdims=True) + EPS)
    xhat = x * rstd
    return xhat * g, xhat, rstd


def _rms_bwd(dh, xhat, rstd, g):
    dxh = dh * g
    dx = rstd * (dxh - xhat * jnp.mean(dxh * xhat, axis=-1, keepdims=True))
    dg = jnp.sum(dh * xhat, axis=0, keepdims=True)
    return dx, dg


_GELU_C = math.sqrt(2.0 / math.pi)


def _gelu(x):
    t = jnp.tanh(_GELU_C * (x + 0.044715 * (x * x * x)))
    return 0.5 * x * (1.0 + t)


def _gelu_and_grad(x):
    x2 = x * x
    t = jnp.tanh(_GELU_C * (x + 0.044715 * (x2 * x)))
    g = 0.5 * x * (1.0 + t)
    dg = 0.5 * (1.0 + t) + 0.5 * x * (1.0 - t * t) * (_GELU_C * (1.0 + 3.0 * 0.044715 * x2))
    return g, dg


def _softplus(z):
    return jnp.maximum(z, 0.0) + jnp.log(1.0 + jnp.exp(-jnp.abs(z)))


def _tril_mask():
    row = lax.broadcasted_iota(jnp.int32, (CHUNK, CHUNK), 0)
    col = lax.broadcasted_iota(jnp.int32, (CHUNK, CHUNK), 1)
    return row >= col, row <= col


ANY_SPEC = pl.BlockSpec(memory_space=pl.ANY)


def _my_index():
    return 4 * lax.axis_index("x") + 2 * lax.axis_index("y") + lax.axis_index("c")


def _exchange_copies(srcs, lands, send_sems, recv_sems, scatter, arriving):
    x, y, c = lax.axis_index("x"), lax.axis_index("y"), lax.axis_index("c")
    me = 4 * x + 2 * y + c
    out = []
    for a in range(len(srcs)):
        for k in range(NDEV - 1):
            bits = k + 1
            px = 1 - x if (bits >> 2) & 1 else x
            py = 1 - y if (bits >> 1) & 1 else y
            pc = 1 - c if bits & 1 else c
            peer = 4 * px + 2 * py + pc
            src = srcs[a].at[peer] if scatter else srcs[a]
            out.append(pltpu.make_async_remote_copy(
                src_ref=src, dst_ref=lands[a].at[peer if arriving else me],
                send_sem=send_sems.at[a * (NDEV - 1) + k], recv_sem=recv_sems.at[a * (NDEV - 1) + k],
                device_id=(px, py, pc), device_id_type=MESH))
    return out


def _exchange_shapes(arrs, scatter):
    n = len(arrs)
    if n == 0:
        return [], []
    lands = [jax.ShapeDtypeStruct(a.shape if scatter else (NDEV,) + a.shape, a.dtype) for a in arrs]
    sems = [pltpu.SemaphoreType.DMA((n * (NDEV - 1),)), pltpu.SemaphoreType.DMA((n * (NDEV - 1),)),
            pltpu.SemaphoreType.DMA((n,))]
    return lands, sems


def _exchange_start(srcs, lands, sems, scatter):
    send_sems, recv_sems, local_sems = sems
    me = _my_index()
    for a in range(len(srcs)):
        pltpu.make_async_copy(srcs[a].at[me] if scatter else srcs[a], lands[a].at[me], local_sems.at[a]).start()
    for send in _exchange_copies(srcs, lands, send_sems, recv_sems, scatter, False):
        send.start()


def _exchange_finish(srcs, lands, sems, scatter):
    send_sems, recv_sems, local_sems = sems
    me = _my_index()
    for arrive in _exchange_copies(srcs, lands, send_sems, recv_sems, scatter, True):
        arrive.wait_recv()
    for send in _exchange_copies(srcs, lands, send_sems, recv_sems, scatter, False):
        send.wait_send()
    for a in range(len(srcs)):
        pltpu.make_async_copy(srcs[a].at[me] if scatter else srcs[a], lands[a].at[me], local_sems.at[a]).wait()


def _gather_two_level(arrs, name):
    n = len(arrs)
    lands = [jax.ShapeDtypeStruct((NDEV,) + a.shape, a.dtype) for a in arrs]

    def body(*refs):
        srcs, outs = refs[:n], refs[n:2 * n]
        send_sems, recv_sems, local_sems = refs[2 * n:]
        x, y, c = lax.axis_index("x"), lax.axis_index("y"), lax.axis_index("c")
        me, sibling = (x, y, c), (x, y, 1 - c)
        chips = [(1 - x, y), (x, 1 - y), (1 - x, 1 - y)]

        def index(dev):
            return 4 * dev[0] + 2 * dev[1] + dev[2]

        def copy(a, k, block, to, src=None):
            dst = outs[a].at[index(block)]
            return pltpu.make_async_remote_copy(
                src_ref=dst if src is None else src, dst_ref=dst, send_sem=send_sems.at[a, k],
                recv_sem=recv_sems.at[a, k], device_id=to, device_id_type=MESH)

        mine, first, passed = [], [], []
        for a in range(n):
            cp = pltpu.make_async_copy(srcs[a], outs[a].at[index(me)], local_sems.at[a])
            cp.start()
            mine.append(cp)
            first.append(copy(a, 0, me, sibling, src=srcs[a]))
            first += [copy(a, 1 + j, me, (*chip, c), src=srcs[a]) for j, chip in enumerate(chips)]
        for cp in first:
            cp.start()
        for a in range(n):
            for j, chip in enumerate(chips):
                copy(a, 1 + j, (*chip, c), me).wait_recv()
                cp = copy(a, 4 + j, (*chip, c), sibling)
                cp.start()
                passed.append(cp)
        for a in range(n):
            copy(a, 0, sibling, me).wait_recv()
            for j, chip in enumerate(chips):
                copy(a, 4 + j, (*chip, 1 - c), me).wait_recv()
        for cp in first + passed:
            cp.wait_send()
        for cp in mine:
            cp.wait()

    return _call(body, name=name, out_shape=lands, in_specs=[ANY_SPEC] * n, out_specs=[ANY_SPEC] * n,
                 scratch_shapes=[pltpu.SemaphoreType.DMA((n, NDEV - 1)), pltpu.SemaphoreType.DMA((n, NDEV - 1)),
                                 pltpu.SemaphoreType.DMA((n,))])(*arrs)


def _scatter_pair(arrs, extra, name):
    n, ne = len(arrs), len(extra)
    lands = [jax.ShapeDtypeStruct((4,) + a.shape[2:], a.dtype) for a in arrs]
    extra_lands, extra_sems = _exchange_shapes(extra, True)

    def body(*refs):
        srcs, xsrc = refs[:n], refs[n:n + ne]
        outs, xout = refs[n + ne:2 * n + ne], refs[2 * n + ne:2 * (n + ne)]
        send_sems, recv_sems = refs[2 * (n + ne)], refs[2 * (n + ne) + 1]
        xsems = refs[2 * (n + ne) + 2:]
        x, y, c = lax.axis_index("x"), lax.axis_index("y"), lax.axis_index("c")
        _exchange_start(xsrc, xout, xsems, True)
        copies = [pltpu.make_async_remote_copy(
            src_ref=srcs[a].at[k, 1 - c], dst_ref=outs[a].at[k], send_sem=send_sems.at[a, k],
            recv_sem=recv_sems.at[a, k], device_id=(x, y, 1 - c), device_id_type=MESH)
            for a in range(n) for k in range(4)]
        for cp in copies:
            cp.start()
        for cp in copies:
            cp.wait()
        _exchange_finish(xsrc, xout, xsems, True)

    outs = _call(
        body, name=name, out_shape=lands + extra_lands, in_specs=[ANY_SPEC] * (n + ne), out_specs=[ANY_SPEC] * (n + ne),
        scratch_shapes=[pltpu.SemaphoreType.DMA((n, 4)), pltpu.SemaphoreType.DMA((n, 4))] + extra_sems,
    )(*arrs, *extra)
    return outs[:n], outs[n:]


def _pair_sum(own, other, name):
    _, R, C = own.shape
    tr = math.gcd(R, max(8, (128 * 1024) // C))

    def body(a_ref, b_ref, o_ref):
        o_ref[...] = (a_ref[...].astype(F32) + b_ref[...].astype(F32)).astype(COMM)

    blk = pl.BlockSpec((4, tr, C), lambda i: (0, i, 0))
    return _call(body, name=name, grid=(R // tr,), out_shape=jax.ShapeDtypeStruct(own.shape, COMM),
                 in_specs=[blk, blk], out_specs=blk, compiler_params=_params(32))(own, other)


def _scatter_chips(arrs, name):
    n = len(arrs)
    lands = [jax.ShapeDtypeStruct(a.shape, a.dtype) for a in arrs]

    def body(*refs):
        srcs, outs = refs[:n], refs[n:2 * n]
        send_sems, recv_sems, local_sems = refs[2 * n:]
        x, y, c = lax.axis_index("x"), lax.axis_index("y"), lax.axis_index("c")
        chip = 2 * x + y
        others = [(1 - x, y), (x, 1 - y), (1 - x, 1 - y)]
        local = [pltpu.make_async_copy(srcs[a].at[chip], outs[a].at[chip], local_sems.at[a]) for a in range(n)]
        for cp in local:
            cp.start()

        def copies(arriving):
            return [pltpu.make_async_remote_copy(
                src_ref=srcs[a].at[2 * px + py], dst_ref=outs[a].at[2 * px + py if arriving else chip],
                send_sem=send_sems.at[a, j], recv_sem=recv_sems.at[a, j], device_id=(px, py, c), device_id_type=MESH)
                for a in range(n) for j, (px, py) in enumerate(others)]

        for cp in copies(False):
            cp.start()
        for cp in copies(True):
            cp.wait_recv()
        for cp in copies(False):
            cp.wait_send()
        for cp in local:
            cp.wait()

    return _call(body, name=name, out_shape=lands, in_specs=[ANY_SPEC] * n, out_specs=[ANY_SPEC] * n,
                 scratch_shapes=[pltpu.SemaphoreType.DMA((n, 3)), pltpu.SemaphoreType.DMA((n, 3)),
                                 pltpu.SemaphoreType.DMA((n,))])(*arrs)


def _spatial_mix(vnb, ws_ref, bT_ref, mix_ref, tm):
    tri, _ = _tril_mask()
    for g in range(GROUPS):
        wm = jnp.where(tri, ws_ref[g], 0.0).astype(MXU)
        cols = slice(g * CHUNK, (g + 1) * CHUNK)
        for ch in range(tm // CHUNK):
            rows = slice(ch * CHUNK, (ch + 1) * CHUNK)
            mix_ref[rows, cols] = _mm(wm, vnb[rows, cols]) + bT_ref[:, g:g + 1]


def _sgu_fwd(x, ln, w_in, g_v, ws, bT, w_out, name):
    T, D = x.shape
    tm = min(TOKEN_TILE, T)
    nw = w_in.shape[2]

    def body(x_ref, ln_ref, win_ref, gv_ref, ws_ref, bT_ref, wout_ref, xo_ref, z_ref, h_ref, y_ref, mix_ref):
        xv = x_ref[...]
        h, _, _ = _rms(xv, ln_ref[...])
        hb = h.astype(MXU)
        h_ref[...] = hb
        for j in range(NDEV):
            z_ref[:, j * nw:(j + 1) * nw] = _mm(hb, win_ref[j])
        u = _gelu(z_ref[:, :D])
        gv = _gelu(z_ref[:, D:])
        vn, _, _ = _rms(gv, gv_ref[...])
        _spatial_mix(vn.astype(MXU), ws_ref, bT_ref, mix_ref, tm)
        y = (u * mix_ref[...]).astype(MXU)
        y_ref[...] = y
        xo_ref[...] = xv + _mm(y, wout_ref[...])

    return _call(
        body, name=name, grid=(T // tm,),
        out_shape=[jax.ShapeDtypeStruct((T, D), F32), jax.ShapeDtypeStruct((T, 2 * D), F32),
                   jax.ShapeDtypeStruct((T, D), MXU), jax.ShapeDtypeStruct((T, D), MXU)],
        in_specs=[_tile(tm, D), _whole(ln.shape), _whole(w_in.shape), _whole(g_v.shape), _whole(ws.shape),
                  _whole(bT.shape), _whole(w_out.shape)],
        out_specs=[_tile(tm, D), _tile(tm, 2 * D), _tile(tm, D), _tile(tm, D)],
        scratch_shapes=[pltpu.VMEM((tm, D), F32)],
        compiler_params=_params(40),
    )(x, ln, w_in, g_v, ws, bT, w_out)


def _mlp_fwd(x, ln, w_up, w_down, cargo, name):
    T, D = x.shape
    tm = min(TOKEN_TILE, T)
    nf = w_up.shape[2]
    F = nf * NDEV
    nc = len(cargo)
    lands, sems = _exchange_shapes(cargo, False)

    def body(x_ref, ln_ref, wup_ref, wdown_ref, *rest):
        xo_ref, pre_ref, h_ref = rest[nc:nc + 3]
        finish = _ride_along(rest[:nc] + rest[nc + 3:], nc, False, rank=1)
        xv = x_ref[...]
        h, _, _ = _rms(xv, ln_ref[...])
        hb = h.astype(MXU)
        h_ref[...] = hb
        for j in range(NDEV):
            pre_ref[:, j * nf:(j + 1) * nf] = _mm(hb, wup_ref[j])
        a = jnp.maximum(pre_ref[...], 0.0)
        xo_ref[...] = xv + _mm(a * a, wdown_ref[...])
        finish()

    outs = _call(
        body, name=name, grid=(T // tm,),
        out_shape=[jax.ShapeDtypeStruct((T, D), F32), jax.ShapeDtypeStruct((T, F), F32),
                   jax.ShapeDtypeStruct((T, D), MXU)] + lands,
        in_specs=[_tile(tm, D), _whole(ln.shape), _whole(w_up.shape), _whole(w_down.shape)] + [ANY_SPEC] * nc,
        out_specs=[_tile(tm, D), _tile(tm, F), _tile(tm, D)] + [ANY_SPEC] * nc,
        scratch_shapes=sems, compiler_params=_params(52),
    )(x, ln, w_up, w_down, *cargo)
    return outs[0], outs[1], outs[2], outs[3:]


def _ple_fwd(x, p, ln, w_g, w_pp, name):
    T, D = x.shape
    tm = min(TOKEN_TILE, T)
    npp = w_pp.shape[2]

    def body(x_ref, p_ref, ln_ref, wg_ref, wpp_ref, xo_ref, gate_ref, pp_ref, h_ref):
        xv = x_ref[...]
        h, _, _ = _rms(xv, ln_ref[...])
        hb = h.astype(MXU)
        h_ref[...] = hb
        gate = jax.nn.sigmoid(_mm(hb, wg_ref[...]))
        gate_ref[...] = gate
        pb = p_ref[...].astype(MXU)
        for j in range(NDEV):
            pp_ref[:, j * npp:(j + 1) * npp] = _mm(pb, wpp_ref[j])
        xo_ref[...] = xv + pp_ref[...] * gate

    return _call(
        body, name=name, grid=(T // tm,),
        out_shape=[jax.ShapeDtypeStruct((T, D), F32), jax.ShapeDtypeStruct((T, D), F32),
                   jax.ShapeDtypeStruct((T, D), F32), jax.ShapeDtypeStruct((T, D), MXU)],
        in_specs=[_tile(tm, D), _tile(tm, p.shape[1]), _whole(ln.shape), _whole(w_g.shape), _whole(w_pp.shape)],
        out_specs=[_tile(tm, D), _tile(tm, D), _tile(tm, D), _tile(tm, D)],
        compiler_params=_params(32),
    )(x, p, ln, w_g, w_pp)


def _head_ones():
    row = lax.broadcasted_iota(jnp.int32, (128, 128), 0)
    col = lax.broadcasted_iota(jnp.int32, (128, 128), 1)
    return (jnp.right_shift(row, 6) == jnp.right_shift(col, 6)).astype(MXU)


def _head_rms(x, g, ones):
    rstd = lax.rsqrt(_split_dot(x * x, ones, 3) * (1.0 / HEAD_DIM) + EPS)
    xhat = x * rstd
    return xhat * g, xhat, rstd


def _head_rms_bwd(dh, xhat, rstd, g, ones):
    dxh = dh * g
    mean = _split_dot(dxh * xhat, ones, 3) * (1.0 / HEAD_DIM)
    return rstd * (dxh - xhat * mean), jnp.sum(dh * xhat, axis=0, keepdims=True)


def _qkv_fwd(x, ln_q, ln_kv, g_q, g_k, w_q, w_kv, name):
    T, D = x.shape
    tm = min(TOKEN_TILE, T)
    nk = w_kv.shape[2]
    half = NDEV // 2

    def body(x_ref, lnq_ref, lnkv_ref, gq_ref, gk_ref, wq_ref, wkv_ref,
             q_ref, k_ref, v_ref, qpre_ref, kpre_ref, hq_ref, hkv_ref):
        xv = x_ref[...]
        _, xhat, _ = _rms(xv, lnq_ref[...])
        hq = (xhat * lnq_ref[...]).astype(MXU)
        hkv = (xhat * lnkv_ref[...]).astype(MXU)
        hq_ref[...] = hq
        hkv_ref[...] = hkv
        qpre_ref[...] = _mm(hq, wq_ref[...])
        for j in range(half):
            kpre_ref[:, j * nk:(j + 1) * nk] = _mm(hkv, wkv_ref[j])
            v_ref[:, j * nk:(j + 1) * nk] = _mm(hkv, wkv_ref[half + j]).astype(MXU)
        ones = _head_ones()
        for b in range(D // 128):
            cols = slice(b * 128, (b + 1) * 128)
            qn, _, _ = _head_rms(qpre_ref[:, cols], gq_ref[:, cols], ones)
            q_ref[:, cols] = (qn * SCALE).astype(MXU)
            kn, _, _ = _head_rms(kpre_ref[:, cols], gk_ref[:, cols], ones)
            k_ref[:, cols] = kn.astype(MXU)

    return _call(
        body, name=name, grid=(T // tm,),
        out_shape=[jax.ShapeDtypeStruct((T, D), MXU)] * 3 + [jax.ShapeDtypeStruct((T, D), F32)] * 2
        + [jax.ShapeDtypeStruct((T, D), MXU)] * 2,
        in_specs=[_tile(tm, D), _whole(ln_q.shape), _whole(ln_kv.shape), _whole(g_q.shape), _whole(g_k.shape),
                  _whole(w_q.shape), _whole(w_kv.shape)],
        out_specs=[_tile(tm, D)] * 7,
        compiler_params=_params(40),
    )(x, ln_q, ln_kv, g_q, g_k, w_q, w_kv)


SB_KEYS = 2 * QBLK


def _sb_consts():
    row = lax.broadcasted_iota(jnp.int32, (QBLK, QBLK), 0)
    col = lax.broadcasted_iota(jnp.int32, (QBLK, QBLK), 1)
    lane = lax.broadcasted_iota(jnp.int32, (QBLK, 128), 1)
    ones = jnp.ones((QBLK, QBLK), MXU)
    later = jnp.concatenate([(row > col).astype(MXU), ones], axis=1)
    later_eq = jnp.concatenate([(row >= col).astype(MXU), ones], axis=1)
    return later, later_eq, lane < HEAD_DIM


MASKED_LOG = -1e30


def _sb_window(i, w):
    upper = (i + 1) * QBLK - w * SB_KEYS
    start = pl.multiple_of(jnp.maximum(upper - SB_KEYS, 0), QBLK)
    key = lax.broadcasted_iota(jnp.int32, (2 * QBLK, SB_KEYS), 1) + start
    return start, key < upper


def _sb_diagonal():
    row = jnp.bitwise_and(lax.broadcasted_iota(jnp.int32, (2 * QBLK, SB_KEYS), 0), QBLK - 1)
    key = lax.broadcasted_iota(jnp.int32, (2 * QBLK, SB_KEYS), 1)
    cases = []
    for shift in (0, QBLK):
        seen = key < row + shift
        cases.append(jnp.stack([jnp.where(seen, 1.0, 0.0), jnp.where(seen, 0.0, MASKED_LOG)]))
    return jnp.stack(cases).astype(F32)


_SB_DIAG_SPEC = pl.BlockSpec((None, 2, 2 * QBLK, SB_KEYS), lambda h, i: (jnp.minimum(i, 1), 0, 0, 0))


def _sb_terms(x, terms):
    x = jnp.concatenate([x[:, :QBLK], x[:, QBLK:]], axis=0)
    out = []
    for _ in range(terms):
        part = x.astype(MXU)
        x = x - part.astype(F32)
        out.append(part)
    return tuple(out)


def _sb_suffix(parts, ones, carry):
    s = jnp.dot(jnp.concatenate(parts[:2], axis=1), jnp.concatenate([ones, ones], axis=0),
                preferred_element_type=F32)
    for part in parts[2:]:
        s = s + jnp.dot(part, ones, preferred_element_type=F32)
    rows = s.shape[0] // 2
    s_lo, sum_lo, s_hi, sum_hi = s[:rows, :QBLK], s[:rows, QBLK:], s[rows:, :QBLK], s[rows:, QBLK:]
    return jnp.concatenate([s_lo + (carry + sum_hi), s_hi + carry], axis=1), carry + (sum_lo + sum_hi)


def _sb_scores(z, mask):
    sp = _softplus(z)
    l, log_sig = -sp, z - sp
    if isinstance(mask, tuple):
        keep, bias = mask
        l, log_sig = l * keep, log_sig + bias
    else:
        l = jnp.where(mask, l, 0.0)
        log_sig = jnp.where(mask, log_sig, MASKED_LOG)
    return log_sig, _sb_terms(l, 2)


def _sb_weights(staged, later, c_l):
    log_sig, parts = staged
    b, c_l = _sb_suffix(parts, later, c_l)
    return jnp.exp(log_sig + b), c_l


DEAD_LOG = -88.0


def _sb_alive(carry):
    return (jnp.max(carry[0]) > DEAD_LOG).astype(jnp.int32)


def _ride_along(refs, n, scatter, rank=2):
    if n == 0:
        return lambda: None
    step, steps = 0, 1
    for d in range(rank):
        step = step * pl.num_programs(d) + pl.program_id(d)
        steps = steps * pl.num_programs(d)
    srcs, lands, sems = refs[:n], refs[n:2 * n], refs[2 * n:]

    @pl.when(step == 0)
    def _():
        _exchange_start(srcs, lands, sems, scatter)

    def finish():
        @pl.when(step == steps - 1)
        def _():
            _exchange_finish(srcs, lands, sems, scatter)

    return finish


def _sb_fwd(q, k, v, cargo, name):
    T, D = q.shape
    nc = len(cargo)
    lands, sems = _exchange_shapes(cargo, False)

    def body(diag_ref, q_ref, k_ref, v_ref, *rest):
        o_ref = rest[nc]
        finish = _ride_along(rest[:nc] + rest[nc + 1:], nc, False)
        i = pl.program_id(1)
        n_steps = (i + 2) // 2
        later, _, first = _sb_consts()
        qv = q_ref[...]
        zero = jnp.zeros_like(qv)
        q2 = jnp.concatenate([jnp.where(first, qv, zero), jnp.where(first, zero, qv)], axis=0)

        def window(w, carry, diagonal):
            start, mask = _sb_window(i, w)
            if diagonal:
                mask = (diag_ref[0], diag_ref[1])
            kw = k_ref[pl.ds(start, SB_KEYS), :]
            vw = v_ref[pl.ds(start, SB_KEYS), :]
            c_l, acc = carry
            a, c_l = _sb_weights(_sb_scores(_mm_nt(q2, kw), mask), later, c_l)
            return c_l, acc + _mm(a, vw)

        def step(state):
            w, _, carry = state
            carry = window(w, carry, False)
            return w + 1, _sb_alive(carry), carry

        carry = window(0, (jnp.zeros((2 * QBLK, 128), F32),) * 2, True)
        _, _, carry = lax.while_loop(lambda s: (s[0] < n_steps) & (s[1] > 0), step,
                                     (jnp.int32(1), _sb_alive(carry), carry))
        o_ref[...] = jnp.where(first, carry[1][:QBLK], carry[1][QBLK:])
        finish()

    qblk = pl.BlockSpec((QBLK, 128), lambda h, i: (i, h))
    kblk = pl.BlockSpec((T, 128), lambda h, i: (0, h))
    outs = _call(
        body, name=name, grid=(D // 128, T // QBLK), out_shape=[jax.ShapeDtypeStruct((T, D), F32)] + lands,
        in_specs=[_SB_DIAG_SPEC, qblk, kblk, kblk] + [ANY_SPEC] * nc, out_specs=[qblk] + [ANY_SPEC] * nc,
        scratch_shapes=sems, compiler_params=_params(32, 2),
    )(_sb_diagonal(), q, k, v, *cargo)
    return outs[0], outs[1:]


def _sb_bwd(q, k, v, o, do, cargo, name):
    T, D = q.shape
    nc = len(cargo)
    lands, sems = _exchange_shapes(cargo, True)

    def body(diag_ref, q_ref, k_ref, v_ref, o_ref, do_ref, *rest):
        dq_ref, dk_ref, dv_ref = rest[nc:nc + 3]
        finish = _ride_along(rest[:nc] + rest[nc + 3:], nc, True)
        i = pl.program_id(1)

        @pl.when(i == 0)
        def _():
            dk_ref[...] = jnp.zeros_like(dk_ref)
            dv_ref[...] = jnp.zeros_like(dv_ref)

        n_steps = (i + 2) // 2
        later, later_eq, first = _sb_consts()
        qv = q_ref[...]
        dob = do_ref[...].astype(MXU)
        zero = jnp.zeros_like(qv)
        q2 = jnp.concatenate([jnp.where(first, qv, zero), jnp.where(first, zero, qv)], axis=0)
        do2 = jnp.concatenate([jnp.where(first, dob, zero), jnp.where(first, zero, dob)], axis=0)
        prod = o_ref[...] * dob.astype(F32)
        prod2 = jnp.concatenate([jnp.where(first, prod, 0.0), jnp.where(first, 0.0, prod)], axis=0)
        total = _split_dot(prod2, jnp.ones((128, 128), MXU), 3)
        total = jnp.concatenate([total, total], axis=1)

        def window(w, carry, diagonal):
            start, mask = _sb_window(i, w)
            if diagonal:
                mask = (diag_ref[0], diag_ref[1])
            kw = k_ref[pl.ds(start, SB_KEYS), :]
            vw = v_ref[pl.ds(start, SB_KEYS), :]
            c_l, c_e, dq = carry
            log_sig, parts = _sb_scores(_mm_nt(q2, kw), mask)
            a, c_l = _sb_weights((log_sig, parts), later, c_l)
            ab = a.astype(MXU)
            e = ab.astype(F32) * _mm_nt(do2, vw)
            from_here, c_e = _sb_suffix(_sb_terms(e, 2), later_eq, c_e)
            sig = jnp.exp(log_sig)
            dzb = (e * (1.0 - sig) - sig * (total - from_here)).astype(MXU)
            dk_ref[pl.ds(start, SB_KEYS), :] += _mm_tn(dzb, q2)
            dv_ref[pl.ds(start, SB_KEYS), :] += _mm_tn(ab, do2)
            return c_l, c_e, dq + _mm(dzb, kw)

        def step(state):
            w, _, carry = state
            carry = window(w, carry, False)
            return w + 1, _sb_alive(carry), carry

        carry = window(0, (jnp.zeros((2 * QBLK, 128), F32),) * 3, True)
        _, _, carry = lax.while_loop(lambda s: (s[0] < n_steps) & (s[1] > 0), step,
                                     (jnp.int32(1), _sb_alive(carry), carry))
        dq_ref[...] = jnp.where(first, carry[2][:QBLK], carry[2][QBLK:]) * SCALE
        finish()

    qblk = pl.BlockSpec((QBLK, 128), lambda h, i: (i, h))
    kblk = pl.BlockSpec((T, 128), lambda h, i: (0, h))
    full = jax.ShapeDtypeStruct((T, D), F32)
    outs = _call(
        body, name=name, grid=(D // 128, T // QBLK), out_shape=[full, full, full] + lands,
        in_specs=[_SB_DIAG_SPEC, qblk, kblk, kblk, qblk, qblk] + [ANY_SPEC] * nc,
        out_specs=[qblk, kblk, kblk] + [ANY_SPEC] * nc, scratch_shapes=sems, compiler_params=_params(32, 2),
    )(_sb_diagonal(), q, k, v, o, do, *cargo)
    return outs[0], outs[1], outs[2], outs[3:]


def _proj_res(x, a, w, name):
    T, D = x.shape
    tm = min(TOKEN_TILE, T)

    def body(x_ref, a_ref, w_ref, o_ref):
        o_ref[...] = x_ref[...] + _mm(a_ref[...], w_ref[...])

    return _call(
        body, name=name, grid=(T // tm,), out_shape=jax.ShapeDtypeStruct((T, D), F32),
        in_specs=[_tile(tm, D), _tile(tm, a.shape[1]), _whole(w.shape)], out_specs=_tile(tm, D),
        compiler_params=_params(32),
    )(x, a, w)


def _proj_nt(g, w, name):
    T = g.shape[0]
    K = w.shape[0]
    tm = min(TOKEN_TILE, T)

    def body(g_ref, w_ref, o_ref):
        o_ref[...] = _mm_nt(g_ref[...], w_ref[...])

    return _call(
        body, name=name, grid=(T // tm,), out_shape=jax.ShapeDtypeStruct((T, K), F32),
        in_specs=[_tile(tm, g.shape[1]), _whole(w.shape)], out_specs=_tile(tm, K),
        compiler_params=_params(32),
    )(g, w)


def _loss_grad(y, tgt, name):
    T, D = y.shape
    tm = min(TOKEN_TILE, T)

    def body(y_ref, t_ref, dy_ref, loss_ref):
        @pl.when(pl.program_id(0) == 0)
        def _():
            loss_ref[...] = jnp.zeros_like(loss_ref)
        diff = y_ref[...] - t_ref[...]
        dy_ref[...] = diff * (1.0 / D)
        rows = jnp.sum(diff * diff, axis=1, keepdims=True) * (1.0 / D)
        loss_ref[...] += 0.5 * jnp.sum(rows, axis=0, keepdims=True)

    return _call(
        body, name=name, grid=(T // tm,),
        out_shape=[jax.ShapeDtypeStruct((T, D), F32), jax.ShapeDtypeStruct((1, 1), F32)],
        in_specs=[_tile(tm, D), _tile(tm, D)], out_specs=[_tile(tm, D), _acc((1, 1))],
        compiler_params=_params(32),
    )(y, tgt)


def _ple_bwd(dx, x, gate, pp, ln, w_g, name):
    T, D = x.shape
    tm = min(TOKEN_TILE, T)

    def body(dx_ref, x_ref, gate_ref, pp_ref, ln_ref, wg_ref, dxo_ref, dpp_ref, dgp_ref, dln_ref):
        @pl.when(pl.program_id(0) == 0)
        def _():
            dln_ref[...] = jnp.zeros_like(dln_ref)
        dxv = dx_ref[...]
        gate = gate_ref[...]
        _, xhat, rstd = _rms(x_ref[...], ln_ref[...])
        dpp_ref[...] = (dxv * gate).astype(MXU)
        dgp = (dxv * pp_ref[...] * gate * (1.0 - gate)).astype(MXU)
        dgp_ref[...] = dgp
        dxn, dln = _rms_bwd(_mm_nt(dgp, wg_ref[...]), xhat, rstd, ln_ref[...])
        dln_ref[...] += dln
        dxo_ref[...] = dxn + dxv

    return _call(
        body, name=name, grid=(T // tm,),
        out_shape=[jax.ShapeDtypeStruct((T, D), F32), jax.ShapeDtypeStruct((T, D), MXU),
                   jax.ShapeDtypeStruct((T, D), MXU), jax.ShapeDtypeStruct(ln.shape, F32)],
        in_specs=[_tile(tm, D)] * 4 + [_whole(ln.shape), _whole(w_g.shape)],
        out_specs=[_tile(tm, D), _tile(tm, D), _tile(tm, D), _acc(ln.shape)],
        compiler_params=_params(32),
    )(dx, x, gate, pp, ln, w_g)


def _mlp_bwd(dx, x, pre, ln, w_up, w_down, cargo, name):
    T, D = x.shape
    tm = min(TOKEN_TILE, T)
    nf = w_up.shape[2]
    F = nf * NDEV
    nc = len(cargo)
    lands, sems = _exchange_shapes(cargo, True)

    def body(dx_ref, x_ref, pre_ref, ln_ref, wup_ref, wdown_ref, *rest):
        dxo_ref, dpre_ref, s_ref, dln_ref = rest[nc:nc + 4]
        finish = _ride_along(rest[:nc] + rest[nc + 4:], nc, True, rank=1)

        @pl.when(pl.program_id(0) == 0)
        def _():
            dln_ref[...] = jnp.zeros_like(dln_ref)
        dxv = dx_ref[...]
        _, xhat, rstd = _rms(x_ref[...], ln_ref[...])
        a = jnp.maximum(pre_ref[...], 0.0)
        s_ref[...] = (a * a).astype(MXU)
        dpre_ref[...] = (_mm_nt(dxv, wdown_ref[...]) * (2.0 * a)).astype(MXU)
        dh = _mm_nt(dpre_ref[:, :nf], wup_ref[0])
        for j in range(1, NDEV):
            dh += _mm_nt(dpre_ref[:, j * nf:(j + 1) * nf], wup_ref[j])
        dxn, dln = _rms_bwd(dh, xhat, rstd, ln_ref[...])
        dln_ref[...] += dln
        dxo_ref[...] = dxn + dxv
        finish()

    outs = _call(
        body, name=name, grid=(T // tm,),
        out_shape=[jax.ShapeDtypeStruct((T, D), F32), jax.ShapeDtypeStruct((T, F), MXU),
                   jax.ShapeDtypeStruct((T, F), MXU), jax.ShapeDtypeStruct(ln.shape, F32)] + lands,
        in_specs=[_tile(tm, D), _tile(tm, D), _tile(tm, F), _whole(ln.shape), _whole(w_up.shape),
                  _whole(w_down.shape)] + [ANY_SPEC] * nc,
        out_specs=[_tile(tm, D), _tile(tm, F), _tile(tm, F), _acc(ln.shape)] + [ANY_SPEC] * nc,
        scratch_shapes=sems, compiler_params=_params(56),
    )(dx, x, pre, ln, w_up, w_down, *cargo)
    return outs[0], outs[1], outs[2], outs[3], outs[4:]


def _qkv_bwd(dx, x, dq, dk, dv, q_pre, k_pre, ln_q, ln_kv, g_q, g_k, w_q, w_kv, cargo, name):
    T, D = x.shape
    tm = min(TOKEN_TILE, T)
    nk = w_kv.shape[2]
    n_tiles = T // tm
    nc = len(cargo)
    lands, sems = _exchange_shapes(cargo, True)

    def body(dx_ref, x_ref, dq_ref, dk_ref, dv_ref, qpre_ref, kpre_ref, lnq_ref, lnkv_ref, gq_ref, gk_ref,
             wq_ref, wkv_ref, *rest):
        dxo_ref, dqp_ref, dkv_ref, dlnq_ref, dlnkv_ref, dgq_ref, dgk_ref = rest[nc:nc + 7]
        gq_acc, gk_acc = rest[2 * nc + 7:2 * nc + 9]
        finish = _ride_along(rest[:nc] + rest[nc + 7:2 * nc + 7] + rest[2 * nc + 9:], nc, True, rank=1)
        i = pl.program_id(0)

        @pl.when(i == 0)
        def _():
            dlnq_ref[...] = jnp.zeros_like(dlnq_ref)
            dlnkv_ref[...] = jnp.zeros_like(dlnkv_ref)
            gq_acc[...] = jnp.zeros_like(gq_acc)
            gk_acc[...] = jnp.zeros_like(gk_acc)

        ones = _head_ones()
        for b in range(D // 128):
            cols = slice(b * 128, (b + 1) * 128)
            _, xh, rs = _head_rms(qpre_ref[:, cols], gq_ref[:, cols], ones)
            d, dg = _head_rms_bwd(dq_ref[:, cols], xh, rs, gq_ref[:, cols], ones)
            dqp_ref[:, cols] = d.astype(MXU)
            gq_acc[:, cols] += dg
            _, xh, rs = _head_rms(kpre_ref[:, cols], gk_ref[:, cols], ones)
            d, dg = _head_rms_bwd(dk_ref[:, cols], xh, rs, gk_ref[:, cols], ones)
            dkv_ref[:, cols] = d.astype(MXU)
            gk_acc[:, cols] += dg
        dkv_ref[:, D:] = dv_ref[...].astype(MXU)

        _, xhat, rstd = _rms(x_ref[...], lnq_ref[...])
        dhq = _mm_nt(dqp_ref[...], wq_ref[...])
        dhkv = _mm_nt(dkv_ref[:, :nk], wkv_ref[0])
        for j in range(1, NDEV):
            dhkv += _mm_nt(dkv_ref[:, j * nk:(j + 1) * nk], wkv_ref[j])
        dxq, dlnq = _rms_bwd(dhq, xhat, rstd, lnq_ref[...])
        dxkv, dlnkv = _rms_bwd(dhkv, xhat, rstd, lnkv_ref[...])
        dlnq_ref[...] += dlnq
        dlnkv_ref[...] += dlnkv
        dxo_ref[...] = dx_ref[...] + dxq + dxkv

        @pl.when(i == n_tiles - 1)
        def _():
            row = lax.broadcasted_iota(jnp.int32, (D, 128), 0)
            col = lax.broadcasted_iota(jnp.int32, (D, 128), 1)
            fold = (jnp.bitwise_and(row, HEAD_DIM - 1) == col).astype(MXU)
            dgq_ref[...] = _split_dot(jnp.broadcast_to(gq_acc[...], (8, D)), fold, 3)
            dgk_ref[...] = _split_dot(jnp.broadcast_to(gk_acc[...], (8, D)), fold, 3)

        finish()

    small = jax.ShapeDtypeStruct((8, 128), F32)
    outs = _call(
        body, name=name, grid=(n_tiles,),
        out_shape=[jax.ShapeDtypeStruct((T, D), F32), jax.ShapeDtypeStruct((T, D), MXU),
                   jax.ShapeDtypeStruct((T, 2 * D), MXU), jax.ShapeDtypeStruct(ln_q.shape, F32),
                   jax.ShapeDtypeStruct(ln_kv.shape, F32), small, small] + lands,
        in_specs=[_tile(tm, D)] * 7 + [_whole(ln_q.shape), _whole(ln_kv.shape), _whole(g_q.shape),
                                       _whole(g_k.shape), _whole(w_q.shape), _whole(w_kv.shape)] + [ANY_SPEC] * nc,
        out_specs=[_tile(tm, D), _tile(tm, D), _tile(tm, 2 * D), _acc(ln_q.shape), _acc(ln_kv.shape),
                   _acc((8, 128)), _acc((8, 128))] + [ANY_SPEC] * nc,
        scratch_shapes=[pltpu.VMEM((1, D), F32), pltpu.VMEM((1, D), F32)] + sems,
        compiler_params=_params(48),
    )(dx, x, dq, dk, dv, q_pre, k_pre, ln_q, ln_kv, g_q, g_k, w_q, w_kv, *cargo)
    return outs[:7], outs[7:]


def _sgu_bwd(dx, x, z, ln, w_in, g_v, ws, wsT, bT, w_out, cargo, name):
    T, D = x.shape
    tm = min(TOKEN_TILE, T)
    nw = w_in.shape[2]
    nc = len(cargo)
    lands, sems = _exchange_shapes(cargo, True)

    def body(dx_ref, x_ref, z_ref, ln_ref, win_ref, gv_ref, ws_ref, wsT_ref, bT_ref, wout_ref, *rest):
        dxo_ref, dz_ref, dws_ref, dbT_ref, dln_ref, dgv_ref = rest[nc:nc + 6]
        mix_ref, dvn_ref = rest[2 * nc + 6:2 * nc + 8]
        finish = _ride_along(rest[:nc] + rest[nc + 6:2 * nc + 6] + rest[2 * nc + 8:], nc, True, rank=1)

        @pl.when(pl.program_id(0) == 0)
        def _():
            dws_ref[...] = jnp.zeros_like(dws_ref)
            dbT_ref[...] = jnp.zeros_like(dbT_ref)
            dln_ref[...] = jnp.zeros_like(dln_ref)
            dgv_ref[...] = jnp.zeros_like(dgv_ref)
        dxv = dx_ref[...]
        _, xhat, rstd = _rms(x_ref[...], ln_ref[...])
        u, du = _gelu_and_grad(z_ref[:, :D])
        gv, dgv = _gelu_and_grad(z_ref[:, D:])
        vn, vhat, rstd_v = _rms(gv, gv_ref[...])
        vnb = vn.astype(MXU)
        _spatial_mix(vnb, ws_ref, bT_ref, mix_ref, tm)
        dy = _mm_nt(dxv, wout_ref[...])
        d_u = dy * mix_ref[...]
        d_mix = dy * u
        dmb = d_mix.astype(MXU)
        tri, triT = _tril_mask()
        for g in range(GROUPS):
            wmT = jnp.where(triT, wsT_ref[g], 0.0).astype(MXU)
            cols = slice(g * CHUNK, (g + 1) * CHUNK)
            for ch in range(tm // CHUNK):
                rows = slice(ch * CHUNK, (ch + 1) * CHUNK)
                dm = dmb[rows, cols]
                dws_ref[g] += jnp.where(tri, _mm_nt(dm, vnb[rows, cols]), 0.0)
                dbT_ref[:, g:g + 1] += jnp.sum(d_mix[rows, cols], axis=1, keepdims=True)
                dvn_ref[rows, cols] = _mm(wmT, dm)
        d_gv, dg = _rms_bwd(dvn_ref[...], vhat, rstd_v, gv_ref[...])
        dgv_ref[...] += dg
        dz_ref[:, :D] = (d_u * du).astype(MXU)
        dz_ref[:, D:] = (d_gv * dgv).astype(MXU)
        dh = _mm_nt(dz_ref[:, :nw], win_ref[0])
        for j in range(1, NDEV):
            dh += _mm_nt(dz_ref[:, j * nw:(j + 1) * nw], win_ref[j])
        dxn, dln = _rms_bwd(dh, xhat, rstd, ln_ref[...])
        dln_ref[...] += dln
        dxo_ref[...] = dxn + dxv
        finish()

    outs = _call(
        body, name=name, grid=(T // tm,),
        out_shape=[jax.ShapeDtypeStruct((T, D), F32), jax.ShapeDtypeStruct((T, 2 * D), MXU),
                   jax.ShapeDtypeStruct(ws.shape, F32), jax.ShapeDtypeStruct(bT.shape, F32),
                   jax.ShapeDtypeStruct(ln.shape, F32), jax.ShapeDtypeStruct(g_v.shape, F32)] + lands,
        in_specs=[_tile(tm, D), _tile(tm, D), _tile(tm, 2 * D), _whole(ln.shape), _whole(w_in.shape),
                  _whole(g_v.shape), _whole(ws.shape), _whole(wsT.shape), _whole(bT.shape), _whole(w_out.shape)]
        + [ANY_SPEC] * nc,
        out_specs=[_tile(tm, D), _tile(tm, 2 * D), _acc(ws.shape), _acc(bT.shape), _acc(ln.shape),
                   _acc(g_v.shape)] + [ANY_SPEC] * nc,
        scratch_shapes=[pltpu.VMEM((tm, D), F32), pltpu.VMEM((tm, D), F32)] + sems,
        compiler_params=_params(48),
    )(dx, x, z, ln, w_in, g_v, ws, wsT, bT, w_out, *cargo)
    return outs[:6], outs[6:]


def _wgrad_rows(a, g, name):
    T, K = a.shape
    N = g.shape[1]
    kb = K // NDEV

    def body(a_ref, g_ref, o_ref):
        o_ref[...] = _mm_tn(a_ref[...], g_ref[...]).astype(COMM)

    return _call(
        body, name=name, grid=(NDEV,), out_shape=jax.ShapeDtypeStruct((K, N), COMM),
        in_specs=[pl.BlockSpec((T, kb), lambda j: (0, j)), _whole(g.shape)],
        out_specs=pl.BlockSpec((kb, N), lambda j: (j, 0)),
        compiler_params=_params(40),
    )(a, g).reshape(NDEV, kb, N)


def _wgrad_cols(a, g, name):
    T, K = a.shape
    N = g.shape[1]
    nb = N // NDEV

    def body(a_ref, g_ref, o_ref):
        o_ref[...] = _mm_tn(a_ref[...], g_ref[...]).astype(COMM)

    return _call(
        body, name=name, grid=(NDEV,), out_shape=jax.ShapeDtypeStruct((NDEV, K, nb), COMM),
        in_specs=[_whole(a.shape), pl.BlockSpec((T, nb), lambda j: (0, j))],
        out_specs=pl.BlockSpec((None, K, nb), lambda j: (j, 0, 0)),
        compiler_params=_params(40),
    )(a, g)


def _adamw_rows(R, C):
    tr = math.gcd(R, max(8, (128 * 1024) // C))
    return R if tr < 64 else tr


def _adamw_update(w_ref, m_ref, v_ref, s_ref, g_ref, d_ref, mo_ref, vo_ref):
    g = s_ref[0].astype(F32)
    for j in range(1, s_ref.shape[0]):
        g = g + s_ref[j].astype(F32)
    mn = ADAM_B1 * m_ref[...] + (1.0 - ADAM_B1) * g
    vn = ADAM_B2 * v_ref[...] + (1.0 - ADAM_B2) * (g * g)
    g_ref[...] = g
    mo_ref[...] = mn
    vo_ref[...] = vn
    m_hat = mn / (1.0 - ADAM_B1 ** ADAM_STEP)
    v_hat = vn / (1.0 - ADAM_B2 ** ADAM_STEP)
    d_ref[...] = -ADAM_LR * (m_hat / (jnp.sqrt(v_hat) + ADAM_EPS) + ADAM_WD * w_ref[...])


def _adamw_layers(w, m, v, slots0, slots1, name):
    _, R, C = w.shape
    tr = _adamw_rows(R, C)
    last = R // tr - 1

    def body(w_ref, m_ref, v_ref, s0_ref, s1_ref, *outs):
        @pl.when(pl.program_id(0) == 0)
        def _():
            _adamw_update(w_ref, m_ref, v_ref, s0_ref, *outs)

        @pl.when(pl.program_id(0) == 1)
        def _():
            _adamw_update(w_ref, m_ref, v_ref, s1_ref, *outs)

    blk = pl.BlockSpec((None, tr, C), lambda l, i: (l, i, 0))
    s0_blk = pl.BlockSpec((slots0.shape[0], tr, C), lambda l, i: (0, jnp.where(l == 0, i, last), 0))
    s1_blk = pl.BlockSpec((slots1.shape[0], tr, C), lambda l, i: (0, jnp.where(l == 1, i, 0), 0))
    out = jax.ShapeDtypeStruct(w.shape, F32)
    return _call(
        body, name=name, grid=(2, R // tr), out_shape=[out, out, out, out],
        in_specs=[blk, blk, blk, s0_blk, s1_blk], out_specs=[blk] * 4, compiler_params=_params(32, 2),
    )(w, m, v, slots0, slots1)


def _adamw(w, m, v, slots, name):
    R, C = w.shape
    n = slots.shape[0]
    tr = _adamw_rows(R, C)

    def body(*refs):
        _adamw_update(*refs)

    blk = pl.BlockSpec((tr, C), lambda i: (i, 0))
    out = jax.ShapeDtypeStruct((R, C), F32)
    return _call(
        body, name=name, grid=(R // tr,), out_shape=[out, out, out, out],
        in_specs=[blk, blk, blk, pl.BlockSpec((n, tr, C), lambda i: (0, i, 0))], out_specs=[blk] * 4,
        compiler_params=_params(32),
    )(w, m, v, slots)


def _rows128(a):
    flat = a.reshape(-1)
    rows = -(-flat.shape[0] // 1024) * 8
    flat = jnp.pad(flat, (0, rows * 128 - flat.shape[0]))
    return flat.reshape(rows, 128)


def kernel(x, p, ln_mix_a, w_in_a, g_v_a, w_spatial, b_spatial, w_out_a, ln_kv, w_kv, g_k, ln_mix_b, w_q, g_q, w_out_b, ln_mlp, w_up, w_down, ln_ple, w_ple_gate, w_ple_proj, loss_target, m_ln_mix_a, m_w_in_a, m_g_v_a, m_w_spatial, m_b_spatial, m_w_out_a, m_ln_kv, m_w_kv, m_g_k, m_ln_mix_b, m_w_q, m_g_q, m_w_out_b, m_ln_mlp, m_w_up, m_w_down, m_ln_ple, m_w_ple_gate, m_w_ple_proj, v_ln_mix_a, v_w_in_a, v_g_v_a, v_w_spatial, v_b_spatial, v_w_out_a, v_ln_kv, v_w_kv, v_g_k, v_ln_mix_b, v_w_q, v_g_q, v_w_out_b, v_ln_mlp, v_w_up, v_w_down, v_ln_ple, v_w_ple_gate, v_w_ple_proj):
    me = 4 * lax.axis_index("x") + 2 * lax.axis_index("y") + lax.axis_index("c")
    D = x.shape[2]
    x0, tgt = x[0], loss_target[0]
    n_layers = w_up.shape[0]

    c = lambda w: w.astype(COMM)
    first = [c(w_in_a[0]), c(w_out_a[0]), ln_mix_a, g_v_a, c(w_up[0]), c(w_down[0]), c(w_ple_gate[0]),
             c(w_ple_proj[0]), c(w_q[0]), c(w_kv)]
    second_small = [c(w_out_b[0]), c(w_ple_gate[1]), c(w_ple_proj[1])]
    second_big = [c(w_up[1]), c(w_down[1])]
    W_in, W_out_a, ln_a, gv_a, W_up0, W_down0, W_g0, W_pp0, W_q, W_kv = _gather_two_level(first, "gather_first")
    W_out_a, ln_a, gv_a = W_out_a.reshape(-1, D), ln_a.reshape(1, D), gv_a.reshape(1, D)
    W_down0, W_g0, W_q = W_down0.reshape(-1, D), W_g0.reshape(-1, D), W_q.reshape(-1, D)
    ws = w_spatial[0]
    wsT = jnp.swapaxes(ws, 1, 2)
    bT = b_spatial[0].T
    ln_kv2, ln_b = ln_kv.reshape(1, D), ln_mix_b
    gk2 = jnp.tile(g_k.reshape(1, HEAD_DIM), (1, D // HEAD_DIM))
    gq2 = jnp.tile(g_q, (1, D // HEAD_DIM))
    ln_m = [ln_mlp[l:l + 1] for l in range(n_layers)]
    ln_p = [ln_ple[l:l + 1] for l in range(n_layers)]

    x1, z, h_a, y_a = _sgu_fwd(x0, ln_a, W_in, gv_a, ws, bT, W_out_a, "sgu_fwd")
    x2, pre0, hm0, (W_out_b, W_g1, W_pp1) = _mlp_fwd(x1, ln_m[0], W_up0, W_down0, second_small, "mlp_fwd0")
    x3, gate0, pp0, hp0 = _ple_fwd(x2, p[0, 0], ln_p[0], W_g0, W_pp0, "ple_fwd0")
    qn, kn, vn, q_pre, k_pre, h_q, h_kv = _qkv_fwd(x3, ln_b, ln_kv2, gq2, gk2, W_q, W_kv, "qkv_fwd")
    o2d, (W_up1, W_down1) = _sb_fwd(qn, kn, vn, second_big, "sb_fwd")
    W_out_b, W_down1, W_g1 = W_out_b.reshape(-1, D), W_down1.reshape(-1, D), W_g1.reshape(-1, D)
    x4 = _proj_res(x3, o2d, W_out_b, "attn_out")
    x5, pre1, hm1, _ = _mlp_fwd(x4, ln_m[1], W_up1, W_down1, [], "mlp_fwd1")
    x6, gate1, pp1, hp1 = _ple_fwd(x5, p[1, 0], ln_p[1], W_g1, W_pp1, "ple_fwd1")
    dy, loss_part = _loss_grad(x6, tgt, "loss_grad")

    dx5, dpp1, dgp1, dlnp1 = _ple_bwd(dy, x5, gate1, pp1, ln_p[1], W_g1, "ple_bwd1")
    dx4, dpre1, s1, dlnm1, _ = _mlp_bwd(dx5, x4, pre1, ln_m[1], W_up1, W_down1, [], "mlp_bwd1")
    wg_big = [_wgrad_cols(hm1, dpre1, "wg_up1"), _wgrad_rows(s1, dx5, "wg_down1")]
    wg_small = [_wgrad_rows(hp1, dgp1, "wg_gate1"), _wgrad_cols(p[1, 0].astype(MXU), dpp1, "wg_proj1"),
                _wgrad_rows(o2d, dx4, "wg_out_b")]
    do2d = _proj_nt(dx4, W_out_b, "attn_out_bwd")
    dqn, dkn, dvn, (s_up1, s_down1) = _sb_bwd(qn, kn, vn, o2d, do2d, wg_big, "sb_bwd")
    (dx3, dq_pre, dkv, dlnb, dlnkv, dgq, dgk), (s_gate1, s_proj1, s_out_b) = _qkv_bwd(
        dx4, x3, dqn, dkn, dvn, q_pre, k_pre, ln_b, ln_kv2, gq2, gk2, W_q, W_kv, wg_small, "qkv_bwd")
    dgq, dgk = dgq[:1, :HEAD_DIM], dgk[:1, :HEAD_DIM]
    dx2, dpp0, dgp0, dlnp0 = _ple_bwd(dx3, x2, gate0, pp0, ln_p[0], W_g0, "ple_bwd0")
    wg_qkv = [_wgrad_rows(h_q, dq_pre, "wg_q"), _wgrad_cols(h_kv, dkv, "wg_kv")]
    dx1, dpre0, s0, dlnm0, (s_q, s_kv) = _mlp_bwd(dx2, x1, pre0, ln_m[0], W_up0, W_down0, wg_qkv, "mlp_bwd0")
    wg_ple = [_wgrad_rows(hp0, dgp0, "wg_gate0"), _wgrad_cols(p[0, 0].astype(MXU), dpp0, "wg_proj0")]
    (dx0, dz, dws, dbT, dlna, dgva), (s_gate0, s_proj0) = _sgu_bwd(
        dx1, x0, z, ln_a, W_in, gv_a, ws, wsT, bT, W_out_a, wg_ple, "sgu_bwd")
    wg_first = [_wgrad_cols(hm0, dpre0, "wg_up0"), _wgrad_rows(s0, dx2, "wg_down0"), _wgrad_cols(h_a, dz, "wg_in_a"),
                _wgrad_rows(y_a, dx1, "wg_out_a")]

    small = [("b_spatial", dbT.T[None], b_spatial, m_b_spatial, v_b_spatial),
             ("ln_kv", dlnkv.reshape(-1), ln_kv, m_ln_kv, v_ln_kv),
             ("g_k", dgk.reshape(-1), g_k, m_g_k, v_g_k),
             ("ln_mix_b", dlnb, ln_mix_b, m_ln_mix_b, v_ln_mix_b),
             ("g_q", dgq, g_q, m_g_q, v_g_q),
             ("ln_mlp", jnp.concatenate([dlnm0, dlnm1]), ln_mlp, m_ln_mlp, v_ln_mlp),
             ("ln_ple", jnp.concatenate([dlnp0, dlnp1]), ln_ple, m_ln_ple, v_ln_ple)]
    sharded_vec = [("ln_mix_a", dlna, ln_mix_a, m_ln_mix_a, v_ln_mix_a),
                   ("g_v_a", dgva, g_v_a, m_g_v_a, v_g_v_a)]
    packs = [[], [], [], []]
    for _, g, w, m, v in small:
        for lst, a in zip(packs, (g, w, m, v)):
            lst.append(_rows128(a))
    for _, g, w, m, v in sharded_vec:
        packs[0].append(g.reshape(NDEV, -1))
        for lst, a in zip(packs[1:], (w, m, v)):
            lst.append(jnp.broadcast_to(a, (NDEV, a.shape[1])))
    packs[0].append(_rows128(loss_part))
    for lst in packs[1:]:
        lst.append(jnp.zeros((8, 128), F32))
    g_pack, w_pack, m_pack, v_pack = (jnp.concatenate(lst) for lst in packs)
    g_pack8 = jnp.broadcast_to(g_pack[None], (NDEV,) + g_pack.shape)
    dws8 = jnp.broadcast_to(dws.reshape(1, -1, 128).astype(COMM), (NDEV, dws.size // 128, 128))
    by_chip = [a.reshape((4, 2) + a.shape[1:]) for a in wg_first]
    from_sibling, (g_all, s_ws) = _scatter_pair(by_chip, [g_pack8, dws8], "scatter_pair")
    my_core = lax.axis_index("c")
    chip_sums = [_pair_sum(lax.dynamic_index_in_dim(a, my_core, 1, keepdims=False), o, f"pair_sum{j}")
                 for j, (a, o) in enumerate(zip(by_chip, from_sibling))]
    s_up0, s_down0, s_in_a, s_out_a = _scatter_chips(chip_sums, "scatter_chips")

    def upd(w, m, v, s, name):
        shape = w.shape
        outs = _adamw(w.reshape(-1, shape[-1]), m.reshape(-1, shape[-1]), v.reshape(-1, shape[-1]), s, name)
        return [o.reshape(shape) for o in outs]

    res = {}
    res["w_out_b"] = upd(w_out_b, m_w_out_b, v_w_out_b, s_out_b, "adam_out_b")
    res["w_q"] = upd(w_q, m_w_q, v_w_q, s_q, "adam_q")
    res["w_kv"] = upd(w_kv, m_w_kv, v_w_kv, s_kv, "adam_kv")
    res["w_in_a"] = upd(w_in_a, m_w_in_a, v_w_in_a, s_in_a, "adam_in_a")
    res["w_out_a"] = upd(w_out_a, m_w_out_a, v_w_out_a, s_out_a, "adam_out_a")
    res["w_up"] = _adamw_layers(w_up, m_w_up, v_w_up, s_up0, s_up1, "adam_up")
    res["w_down"] = _adamw_layers(w_down, m_w_down, v_w_down, s_down0, s_down1, "adam_down")
    res["w_ple_gate"] = _adamw_layers(w_ple_gate, m_w_ple_gate, v_w_ple_gate, s_gate0, s_gate1, "adam_gate")
    res["w_ple_proj"] = _adamw_layers(w_ple_proj, m_w_ple_proj, v_w_ple_proj, s_proj0, s_proj1, "adam_proj")

    res["w_spatial"] = [o.reshape(w_spatial.shape) for o in _adamw(
        w_spatial.reshape(-1, 128), m_w_spatial.reshape(-1, 128), v_w_spatial.reshape(-1, 128), s_ws, "adam_spatial")]
    outs = _adamw(w_pack, m_pack, v_pack, g_all, "adam_small")
    loss = outs[0][-8, 0]
    row = 0
    for nm, g, w, m, v in small:
        nrows = _rows128(w).shape[0]
        res[nm] = [o[row:row + nrows].reshape(-1)[:w.size].reshape(w.shape) for o in outs]
        row += nrows
    for nm, g, w, m, v in sharded_vec:
        res[nm] = [lax.dynamic_slice_in_dim(o[row:row + NDEV], me, 1, axis=0) for o in outs]
        row += NDEV

    names = ["ln_mix_a", "w_in_a", "g_v_a", "w_spatial", "b_spatial", "w_out_a", "ln_kv", "w_kv", "g_k", "ln_mix_b",
             "w_q", "g_q", "w_out_b", "ln_mlp", "w_up", "w_down", "ln_ple", "w_ple_gate", "w_ple_proj"]
    out = [loss, dx0[None]]
    for t in range(4):
        out += [res[nm][t] for nm in names]
    return tuple(out)
```

```python
import math

import jax
import jax.numpy as jnp
from jax import lax
from jax.experimental import pallas as pl
from jax.experimental.pallas import tpu as pltpu

F32 = jnp.float32
MXU = jnp.bfloat16
COMM = jnp.bfloat16
EPS = 1e-6
NDEV = 8
HEAD_DIM = 64
CHUNK = 128
GROUPS = 8
QBLK = 128
SCALE = HEAD_DIM ** -0.5
TOKEN_TILE = 256
ADAM_LR = 0.001
ADAM_B1 = 0.9
ADAM_B2 = 0.999
ADAM_EPS = 1e-08
ADAM_WD = 0.01
ADAM_STEP = 10
MESH = pl.DeviceIdType.MESH


def _call(body, **kw):
    return pl.pallas_call(body, **kw)


def _params(vmem_mb, n_axes=1):
    return pltpu.CompilerParams(dimension_semantics=("arbitrary",) * n_axes,
                                vmem_limit_bytes=vmem_mb << 20)


def _tile(tm, n):
    return pl.BlockSpec((tm, n), lambda i: (i, 0))


def _whole(shape):
    zeros = (0,) * len(shape)
    return pl.BlockSpec(shape, lambda i: zeros, pipeline_mode=pl.Buffered(1))


def _acc(shape):
    zeros = (0,) * len(shape)
    return pl.BlockSpec(shape, lambda i: zeros)


def _mm(a, b):
    return jnp.dot(a.astype(MXU), b.astype(MXU), preferred_element_type=F32)


def _mm_nt(a, b):
    return lax.dot_general(a.astype(MXU), b.astype(MXU), (((1,), (1,)), ((), ())),
                           preferred_element_type=F32)


def _mm_tn(a, b):
    return lax.dot_general(a.astype(MXU), b.astype(MXU), (((0,), (0,)), ((), ())),
                           preferred_element_type=F32)


def _split_dot(x, ones, terms=2):
    out = None
    for _ in range(terms):
        part = x.astype(MXU)
        x = x - part.astype(F32)
        d = jnp.dot(part, ones, preferred_element_type=F32)
        out = d if out is None else out + d
    return out


def _rms(x, g):
    rstd = lax.rsqrt(jnp.mean(x * x, axis=-1, keepdims=True) + EPS)
    xhat = x * rstd
    return xhat * g, xhat, rstd


def _rms_bwd(dh, xhat, rstd, g):
    dxh = dh * g
    dx = rstd * (dxh - xhat * jnp.mean(dxh * xhat, axis=-1, keepdims=True))
    dg = jnp.sum(dh * xhat, axis=0, keepdims=True)
    return dx, dg


_GELU_C = math.sqrt(2.0 / math.pi)


def _gelu(x):
    t = jnp.tanh(_GELU_C * (x + 0.044715 * (x * x * x)))
    return 0.5 * x * (1.0 + t)


def _gelu_and_grad(x):
    x2 = x * x
    t = jnp.tanh(_GELU_C * (x + 0.044715 * (x2 * x)))
    g = 0.5 * x * (1.0 + t)
    dg = 0.5 * (1.0 + t) + 0.5 * x * (1.0 - t * t) * (_GELU_C * (1.0 + 3.0 * 0.044715 * x2))
    return g, dg


def _softplus(z):
    return jnp.maximum(z, 0.0) + jnp.log(1.0 + jnp.exp(-jnp.abs(z)))


def _tril_mask():
    row = lax.broadcasted_iota(jnp.int32, (CHUNK, CHUNK), 0)
    col = lax.broadcasted_iota(jnp.int32, (CHUNK, CHUNK), 1)
    return row >= col, row <= col


ANY_SPEC = pl.BlockSpec(memory_space=pl.ANY)


def _my_index():
    return 4 * lax.axis_index("x") + 2 * lax.axis_index("y") + lax.axis_index("c")


def _exchange_copies(srcs, lands, send_sems, recv_sems, scatter, arriving):
    x, y, c = lax.axis_index("x"), lax.axis_index("y"), lax.axis_index("c")
    me = 4 * x + 2 * y + c
    out = []
    for a in range(len(srcs)):
        for k in range(NDEV - 1):
            bits = k + 1
            px = 1 - x if (bits >> 2) & 1 else x
            py = 1 - y if (bits >> 1) & 1 else y
            pc = 1 - c if bits & 1 else c
            peer = 4 * px + 2 * py + pc
            src = srcs[a].at[peer] if scatter else srcs[a]
            out.append(pltpu.make_async_remote_copy(
                src_ref=src, dst_ref=lands[a].at[peer if arriving else me],
                send_sem=send_sems.at[a * (NDEV - 1) + k], recv_sem=recv_sems.at[a * (NDEV - 1) + k],
                device_id=(px, py, pc), device_id_type=MESH))
    return out


def _exchange_shapes(arrs, scatter):
    n = len(arrs)
    if n == 0:
        return [], []
    lands = [jax.ShapeDtypeStruct(a.shape if scatter else (NDEV,) + a.shape, a.dtype) for a in arrs]
    sems = [pltpu.SemaphoreType.DMA((n * (NDEV - 1),)), pltpu.SemaphoreType.DMA((n * (NDEV - 1),)),
            pltpu.SemaphoreType.DMA((n,))]
    return lands, sems


def _exchange_start(srcs, lands, sems, scatter):
    send_sems, recv_sems, local_sems = sems
    me = _my_index()
    for a in range(len(srcs)):
        pltpu.make_async_copy(srcs[a].at[me] if scatter else srcs[a], lands[a].at[me], local_sems.at[a]).start()
    for send in _exchange_copies(srcs, lands, send_sems, recv_sems, scatter, False):
        send.start()


def _exchange_finish(srcs, lands, sems, scatter):
    send_sems, recv_sems, local_sems = sems
    me = _my_index()
    for arrive in _exchange_copies(srcs, lands, send_sems, recv_sems, scatter, True):
        arrive.wait_recv()
    for send in _exchange_copies(srcs, lands, send_sems, recv_sems, scatter, False):
        send.wait_send()
    for a in range(len(srcs)):
        pltpu.make_async_copy(srcs[a].at[me] if scatter else srcs[a], lands[a].at[me], local_sems.at[a]).wait()


def _gather_two_level(arrs, name):
    n = len(arrs)
    lands = [jax.ShapeDtypeStruct((NDEV,) + a.shape, a.dtype) for a in arrs]

    def body(*refs):
        srcs, outs = refs[:n], refs[n:2 * n]
        send_sems, recv_sems, local_sems = refs[2 * n:]
        x, y, c = lax.axis_index("x"), lax.axis_index("y"), lax.axis_index("c")
        me, sibling = (x, y, c), (x, y, 1 - c)
        chips = [(1 - x, y), (x, 1 - y), (1 - x, 1 - y)]

        def index(dev):
            return 4 * dev[0] + 2 * dev[1] + dev[2]

        def copy(a, k, block, to, src=None):
            dst = outs[a].at[index(block)]
            return pltpu.make_async_remote_copy(
                src_ref=dst if src is None else src, dst_ref=dst, send_sem=send_sems.at[a, k],
                recv_sem=recv_sems.at[a, k], device_id=to, device_id_type=MESH)

        mine, first, passed = [], [], []
        for a in range(n):
            cp = pltpu.make_async_copy(srcs[a], outs[a].at[index(me)], local_sems.at[a])
            cp.start()
            mine.append(cp)
            first.append(copy(a, 0, me, sibling, src=srcs[a]))
            first += [copy(a, 1 + j, me, (*chip, c), src=srcs[a]) for j, chip in enumerate(chips)]
        for cp in first:
            cp.start()
        for a in range(n):
            for j, chip in enumerate(chips):
                copy(a, 1 + j, (*chip, c), me).wait_recv()
                cp = copy(a, 4 + j, (*chip, c), sibling)
                cp.start()
                passed.append(cp)
        for a in range(n):
            copy(a, 0, sibling, me).wait_recv()
            for j, chip in enumerate(chips):
                copy(a, 4 + j, (*chip, 1 - c), me).wait_recv()
        for cp in first + passed:
            cp.wait_send()
        for cp in mine:
            cp.wait()

    return _call(body, name=name, out_shape=lands, in_specs=[ANY_SPEC] * n, out_specs=[ANY_SPEC] * n,
                 scratch_shapes=[pltpu.SemaphoreType.DMA((n, NDEV - 1)), pltpu.SemaphoreType.DMA((n, NDEV - 1)),
                                 pltpu.SemaphoreType.DMA((n,))])(*arrs)


def _scatter_pair(arrs, extra, name):
    n, ne = len(arrs), len(extra)
    lands = [jax.ShapeDtypeStruct((4,) + a.shape[2:], a.dtype) for a in arrs]
    extra_lands, extra_sems = _exchange_shapes(extra, True)

    def body(*refs):
        srcs, xsrc = refs[:n], refs[n:n + ne]
        outs, xout = refs[n + ne:2 * n + ne], refs[2 * n + ne:2 * (n + ne)]
        send_sems, recv_sems = refs[2 * (n + ne)], refs[2 * (n + ne) + 1]
        xsems = refs[2 * (n + ne) + 2:]
        x, y, c = lax.axis_index("x"), lax.axis_index("y"), lax.axis_index("c")
        _exchange_start(xsrc, xout, xsems, True)
        copies = [pltpu.make_async_remote_copy(
            src_ref=srcs[a].at[k, 1 - c], dst_ref=outs[a].at[k], send_sem=send_sems.at[a, k],
            recv_sem=recv_sems.at[a, k], device_id=(x, y, 1 - c), device_id_type=MESH)
            for a in range(n) for k in range(4)]
        for cp in copies:
            cp.start()
        for cp in copies:
            cp.wait()
        _exchange_finish(xsrc, xout, xsems, True)

    outs = _call(
        body, name=name, out_shape=lands + extra_lands, in_specs=[ANY_SPEC] * (n + ne), out_specs=[ANY_SPEC] * (n + ne),
        scratch_shapes=[pltpu.SemaphoreType.DMA((n, 4)), pltpu.SemaphoreType.DMA((n, 4))] + extra_sems,
    )(*arrs, *extra)
    return outs[:n], outs[n:]


def _pair_sum(own, other, name):
    _, R, C = own.shape
    tr = math.gcd(R, max(8, (128 * 1024) // C))

    def body(a_ref, b_ref, o_ref):
        o_ref[...] = (a_ref[...].astype(F32) + b_ref[...].astype(F32)).astype(COMM)

    blk = pl.BlockSpec((4, tr, C), lambda i: (0, i, 0))
    return _call(body, name=name, grid=(R // tr,), out_shape=jax.ShapeDtypeStruct(own.shape, COMM),
                 in_specs=[blk, blk], out_specs=blk, compiler_params=_params(32))(own, other)


def _scatter_chips(arrs, name):
    n = len(arrs)
    lands = [jax.ShapeDtypeStruct(a.shape, a.dtype) for a in arrs]

    def body(*refs):
        srcs, outs = refs[:n], refs[n:2 * n]
        send_sems, recv_sems, local_sems = refs[2 * n:]
        x, y, c = lax.axis_index("x"), lax.axis_index("y"), lax.axis_index("c")
        chip = 2 * x + y
        others = [(1 - x, y), (x, 1 - y), (1 - x, 1 - y)]
        local = [pltpu.make_async_copy(srcs[a].at[chip], outs[a].at[chip], local_sems.at[a]) for a in range(n)]
        for cp in local:
            cp.start()

        def copies(arriving):
            return [pltpu.make_async_remote_copy(
                src_ref=srcs[a].at[2 * px + py], dst_ref=outs[a].at[2 * px + py if arriving else chip],
                send_sem=send_sems.at[a, j], recv_sem=recv_sems.at[a, j], device_id=(px, py, c), device_id_type=MESH)
                for a in range(n) for j, (px, py) in enumerate(others)]

        for cp in copies(False):
            cp.start()
        for cp in copies(True):
            cp.wait_recv()
        for cp in copies(False):
            cp.wait_send()
        for cp in local:
            cp.wait()

    return _call(body, name=name, out_shape=lands, in_specs=[ANY_SPEC] * n, out_specs=[ANY_SPEC] * n,
                 scratch_shapes=[pltpu.SemaphoreType.DMA((n, 3)), pltpu.SemaphoreType.DMA((n, 3)),
                                 pltpu.SemaphoreType.DMA((n,))])(*arrs)


def _spatial_mix(vnb, ws_ref, bT_ref, mix_ref, tm):
    tri, _ = _tril_mask()
    for g in range(GROUPS):
        wm = jnp.where(tri, ws_ref[g], 0.0).astype(MXU)
        cols = slice(g * CHUNK, (g + 1) * CHUNK)
        for ch in range(tm // CHUNK):
            rows = slice(ch * CHUNK, (ch + 1) * CHUNK)
            mix_ref[rows, cols] = _mm(wm, vnb[rows, cols]) + bT_ref[:, g:g + 1]


def _sgu_fwd(x, ln, w_in, g_v, ws, bT, w_out, name):
    T, D = x.shape
    tm = min(TOKEN_TILE, T)
    nw = w_in.shape[2]

    def body(x_ref, ln_ref, win_ref, gv_ref, ws_ref, bT_ref, wout_ref, xo_ref, z_ref, h_ref, y_ref, mix_ref):
        xv = x_ref[...]
        h, _, _ = _rms(xv, ln_ref[...])
        hb = h.astype(MXU)
        h_ref[...] = hb
        for j in range(NDEV):
            z_ref[:, j * nw:(j + 1) * nw] = _mm(hb, win_ref[j])
        u = _gelu(z_ref[:, :D])
        gv = _gelu(z_ref[:, D:])
        vn, _, _ = _rms(gv, gv_ref[...])
        _spatial_mix(vn.astype(MXU), ws_ref, bT_ref, mix_ref, tm)
        y = (u * mix_ref[...]).astype(MXU)
        y_ref[...] = y
        xo_ref[...] = xv + _mm(y, wout_ref[...])

    return _call(
        body, name=name, grid=(T // tm,),
        out_shape=[jax.ShapeDtypeStruct((T, D), F32), jax.ShapeDtypeStruct((T, 2 * D), F32),
                   jax.ShapeDtypeStruct((T, D), MXU), jax.ShapeDtypeStruct((T, D), MXU)],
        in_specs=[_tile(tm, D), _whole(ln.shape), _whole(w_in.shape), _whole(g_v.shape), _whole(ws.shape),
                  _whole(bT.shape), _whole(w_out.shape)],
        out_specs=[_tile(tm, D), _tile(tm, 2 * D), _tile(tm, D), _tile(tm, D)],
        scratch_shapes=[pltpu.VMEM((tm, D), F32)],
        compiler_params=_params(40),
    )(x, ln, w_in, g_v, ws, bT, w_out)


def _mlp_fwd(x, ln, w_up, w_down, cargo, name):
    T, D = x.shape
    tm = min(TOKEN_TILE, T)
    nf = w_up.shape[2]
    F = nf * NDEV
    nc = len(cargo)
    lands, sems = _gather_ride_shapes(cargo)

    def body(x_ref, ln_ref, wup_ref, wdown_ref, *rest):
        xo_ref, pre_ref, h_ref = rest[nc:nc + 3]
        finish = _ride_along_gather(rest[:nc] + rest[nc + 3:], nc, 1)
        xv = x_ref[...]
        h, _, _ = _rms(xv, ln_ref[...])
        hb = h.astype(MXU)
        h_ref[...] = hb
        for j in range(NDEV):
            pre_ref[:, j * nf:(j + 1) * nf] = _mm(hb, wup_ref[j])
        a = jnp.maximum(pre_ref[...], 0.0)
        xo_ref[...] = xv + _mm(a * a, wdown_ref[...])
        finish()

    outs = _call(
        body, name=name, grid=(T // tm,),
        out_shape=[jax.ShapeDtypeStruct((T, D), F32), jax.ShapeDtypeStruct((T, F), F32),
                   jax.ShapeDtypeStruct((T, D), MXU)] + lands,
        in_specs=[_tile(tm, D), _whole(ln.shape), _whole(w_up.shape), _whole(w_down.shape)] + [ANY_SPEC] * nc,
        out_specs=[_tile(tm, D), _tile(tm, F), _tile(tm, D)] + [ANY_SPEC] * nc,
        scratch_shapes=sems, compiler_params=_params(52),
    )(x, ln, w_up, w_down, *cargo)
    return outs[0], outs[1], outs[2], outs[3:]


def _ple_fwd(x, p, ln, w_g, w_pp, name):
    T, D = x.shape
    tm = min(TOKEN_TILE, T)
    npp = w_pp.shape[2]

    def body(x_ref, p_ref, ln_ref, wg_ref, wpp_ref, xo_ref, gate_ref, pp_ref, h_ref):
        xv = x_ref[...]
        h, _, _ = _rms(xv, ln_ref[...])
        hb = h.astype(MXU)
        h_ref[...] = hb
        gate = jax.nn.sigmoid(_mm(hb, wg_ref[...]))
        gate_ref[...] = gate
        pb = p_ref[...].astype(MXU)
        for j in range(NDEV):
            pp_ref[:, j * npp:(j + 1) * npp] = _mm(pb, wpp_ref[j])
        xo_ref[...] = xv + pp_ref[...] * gate

    return _call(
        body, name=name, grid=(T // tm,),
        out_shape=[jax.ShapeDtypeStruct((T, D), F32), jax.ShapeDtypeStruct((T, D), F32),
                   jax.ShapeDtypeStruct((T, D), F32), jax.ShapeDtypeStruct((T, D), MXU)],
        in_specs=[_tile(tm, D), _tile(tm, p.shape[1]), _whole(ln.shape), _whole(w_g.shape), _whole(w_pp.shape)],
        out_specs=[_tile(tm, D), _tile(tm, D), _tile(tm, D), _tile(tm, D)],
        compiler_params=_params(32),
    )(x, p, ln, w_g, w_pp)


def _head_ones():
    row = lax.broadcasted_iota(jnp.int32, (128, 128), 0)
    col = lax.broadcasted_iota(jnp.int32, (128, 128), 1)
    return (jnp.right_shift(row, 6) == jnp.right_shift(col, 6)).astype(MXU)


def _head_rms(x, g, ones):
    rstd = lax.rsqrt(_split_dot(x * x, ones, 3) * (1.0 / HEAD_DIM) + EPS)
    xhat = x * rstd
    return xhat * g, xhat, rstd


def _head_rms_bwd(dh, xhat, rstd, g, ones):
    dxh = dh * g
    mean = _split_dot(dxh * xhat, ones, 3) * (1.0 / HEAD_DIM)
    return rstd * (dxh - xhat * mean), jnp.sum(dh * xhat, axis=0, keepdims=True)


def _qkv_fwd(x, ln_q, ln_kv, g_q, g_k, w_q, w_kv, name):
    T, D = x.shape
    tm = min(TOKEN_TILE, T)
    nk = w_kv.shape[2]
    half = NDEV // 2

    def body(x_ref, lnq_ref, lnkv_ref, gq_ref, gk_ref, wq_ref, wkv_ref,
             q_ref, k_ref, v_ref, qpre_ref, kpre_ref, hq_ref, hkv_ref):
        xv = x_ref[...]
        _, xhat, _ = _rms(xv, lnq_ref[...])
        hq = (xhat * lnq_ref[...]).astype(MXU)
        hkv = (xhat * lnkv_ref[...]).astype(MXU)
        hq_ref[...] = hq
        hkv_ref[...] = hkv
        qpre_ref[...] = _mm(hq, wq_ref[...])
        for j in range(half):
            kpre_ref[:, j * nk:(j + 1) * nk] = _mm(hkv, wkv_ref[j])
            v_ref[:, j * nk:(j + 1) * nk] = _mm(hkv, wkv_ref[half + j]).astype(MXU)
        ones = _head_ones()
        for b in range(D // 128):
            cols = slice(b * 128, (b + 1) * 128)
            qn, _, _ = _head_rms(qpre_ref[:, cols], gq_ref[:, cols], ones)
            q_ref[:, cols] = (qn * SCALE).astype(MXU)
            kn, _, _ = _head_rms(kpre_ref[:, cols], gk_ref[:, cols], ones)
            k_ref[:, cols] = kn.astype(MXU)

    return _call(
        body, name=name, grid=(T // tm,),
        out_shape=[jax.ShapeDtypeStruct((T, D), MXU)] * 3 + [jax.ShapeDtypeStruct((T, D), F32)] * 2
        + [jax.ShapeDtypeStruct((T, D), MXU)] * 2,
        in_specs=[_tile(tm, D), _whole(ln_q.shape), _whole(ln_kv.shape), _whole(g_q.shape), _whole(g_k.shape),
                  _whole(w_q.shape), _whole(w_kv.shape)],
        out_specs=[_tile(tm, D)] * 7,
        compiler_params=_params(40),
    )(x, ln_q, ln_kv, g_q, g_k, w_q, w_kv)


SB_KEYS = 2 * QBLK


def _sb_consts():
    row = lax.broadcasted_iota(jnp.int32, (QBLK, QBLK), 0)
    col = lax.broadcasted_iota(jnp.int32, (QBLK, QBLK), 1)
    lane = lax.broadcasted_iota(jnp.int32, (QBLK, 128), 1)
    ones = jnp.ones((QBLK, QBLK), MXU)
    later = jnp.concatenate([(row > col).astype(MXU), ones], axis=1)
    later_eq = jnp.concatenate([(row >= col).astype(MXU), ones], axis=1)
    return later, later_eq, lane < HEAD_DIM


MASKED_LOG = -1e30


def _sb_window(i, w):
    upper = (i + 1) * QBLK - w * SB_KEYS
    start = pl.multiple_of(jnp.maximum(upper - SB_KEYS, 0), QBLK)
    key = lax.broadcasted_iota(jnp.int32, (2 * QBLK, SB_KEYS), 1) + start
    return start, key < upper


def _sb_diagonal():
    row = jnp.bitwise_and(lax.broadcasted_iota(jnp.int32, (2 * QBLK, SB_KEYS), 0), QBLK - 1)
    key = lax.broadcasted_iota(jnp.int32, (2 * QBLK, SB_KEYS), 1)
    cases = []
    for shift in (0, QBLK):
        seen = key < row + shift
        cases.append(jnp.stack([jnp.where(seen, 1.0, 0.0), jnp.where(seen, 0.0, MASKED_LOG)]))
    return jnp.stack(cases).astype(F32)


_SB_DIAG_SPEC = pl.BlockSpec((None, 2, 2 * QBLK, SB_KEYS), lambda h, i: (jnp.minimum(i, 1), 0, 0, 0))


def _sb_terms(x, terms):
    x = jnp.concatenate([x[:, :QBLK], x[:, QBLK:]], axis=0)
    out = []
    for _ in range(terms):
        part = x.astype(MXU)
        x = x - part.astype(F32)
        out.append(part)
    return tuple(out)


def _sb_suffix(parts, ones, carry):
    s = jnp.dot(jnp.concatenate(parts[:2], axis=1), jnp.concatenate([ones, ones], axis=0),
                preferred_element_type=F32)
    for part in parts[2:]:
        s = s + jnp.dot(part, ones, preferred_element_type=F32)
    rows = s.shape[0] // 2
    s_lo, sum_lo, s_hi, sum_hi = s[:rows, :QBLK], s[:rows, QBLK:], s[rows:, :QBLK], s[rows:, QBLK:]
    return jnp.concatenate([s_lo + (carry + sum_hi), s_hi + carry], axis=1), carry + (sum_lo + sum_hi)


def _sb_scores(z, mask):
    sp = _softplus(z)
    l, log_sig = -sp, z - sp
    if isinstance(mask, tuple):
        keep, bias = mask
        l, log_sig = l * keep, log_sig + bias
    else:
        l = jnp.where(mask, l, 0.0)
        log_sig = jnp.where(mask, log_sig, MASKED_LOG)
    return log_sig, _sb_terms(l, 2)


def _sb_weights(staged, later, c_l):
    log_sig, parts = staged
    b, c_l = _sb_suffix(parts, later, c_l)
    return jnp.exp(log_sig + b), c_l


DEAD_LOG = -88.0


def _sb_alive(carry):
    return (jnp.max(carry[0]) > DEAD_LOG).astype(jnp.int32)


def _ride_along(refs, n, scatter, rank=2):
    if n == 0:
        return lambda: None
    step, steps = 0, 1
    for d in range(rank):
        step = step * pl.num_programs(d) + pl.program_id(d)
        steps = steps * pl.num_programs(d)
    srcs, lands, sems = refs[:n], refs[n:2 * n], refs[2 * n:]

    @pl.when(step == 0)
    def _():
        _exchange_start(srcs, lands, sems, scatter)

    def finish():
        @pl.when(step == steps - 1)
        def _():
            _exchange_finish(srcs, lands, sems, scatter)

    return finish


def _gather_ride_shapes(arrs):
    n = len(arrs)
    if n == 0:
        return [], []
    return ([jax.ShapeDtypeStruct((NDEV,) + a.shape, a.dtype) for a in arrs],
            [pltpu.SemaphoreType.DMA((n, NDEV - 1)), pltpu.SemaphoreType.DMA((n, NDEV - 1)),
             pltpu.SemaphoreType.DMA((n,))])


def _ride_along_gather(refs, n, rank):
    if n == 0:
        return lambda: None
    step, steps = 0, 1
    for d in range(rank):
        step = step * pl.num_programs(d) + pl.program_id(d)
        steps = steps * pl.num_programs(d)
    srcs, outs = refs[:n], refs[n:2 * n]
    send_sems, recv_sems, local_sems = refs[2 * n:]
    x, y, c = lax.axis_index("x"), lax.axis_index("y"), lax.axis_index("c")
    me, sibling = (x, y, c), (x, y, 1 - c)
    chips = [(1 - x, y), (x, 1 - y), (1 - x, 1 - y)]

    def index(dev):
        return 4 * dev[0] + 2 * dev[1] + dev[2]

    def copy(a, k, block, to, src=None):
        dst = outs[a].at[index(block)]
        return pltpu.make_async_remote_copy(
            src_ref=dst if src is None else src, dst_ref=dst, send_sem=send_sems.at[a, k],
            recv_sem=recv_sems.at[a, k], device_id=to, device_id_type=MESH)

    def mine(a):
        return pltpu.make_async_copy(srcs[a], outs[a].at[index(me)], local_sems.at[a])

    def first(a):
        return [copy(a, 0, me, sibling, src=srcs[a])] + [copy(a, 1 + j, me, (*chip, c), src=srcs[a])
                                                         for j, chip in enumerate(chips)]

    @pl.when(step == 0)
    def _():
        for a in range(n):
            mine(a).start()
            for cp in first(a):
                cp.start()

    @pl.when(step == steps // 2)
    def _():
        for a in range(n):
            for j, chip in enumerate(chips):
                copy(a, 1 + j, (*chip, c), me).wait_recv()
                copy(a, 4 + j, (*chip, c), sibling).start()

    def finish():
        @pl.when(step == steps - 1)
        def _():
            for a in range(n):
                copy(a, 0, sibling, me).wait_recv()
                for j, chip in enumerate(chips):
                    copy(a, 4 + j, (*chip, 1 - c), me).wait_recv()
                for cp in first(a):
                    cp.wait_send()
                for j, chip in enumerate(chips):
                    copy(a, 4 + j, (*chip, c), sibling).wait_send()
                mine(a).wait()

    return finish


def _sb_fwd(q, k, v, cargo, name):
    T, D = q.shape
    nc = len(cargo)
    lands, sems = _gather_ride_shapes(cargo)

    def body(diag_ref, q_ref, k_ref, v_ref, *rest):
        o_ref = rest[nc]
        finish = _ride_along_gather(rest[:nc] + rest[nc + 1:], nc, 2)
        i = pl.program_id(1)
        n_steps = (i + 2) // 2
        later, _, first = _sb_consts()
        qv = q_ref[...]
        zero = jnp.zeros_like(qv)
        q2 = jnp.concatenate([jnp.where(first, qv, zero), jnp.where(first, zero, qv)], axis=0)

        def window(w, carry, diagonal):
            start, mask = _sb_window(i, w)
            if diagonal:
                mask = (diag_ref[0], diag_ref[1])
            kw = k_ref[pl.ds(start, SB_KEYS), :]
            vw = v_ref[pl.ds(start, SB_KEYS), :]
            c_l, acc = carry
            a, c_l = _sb_weights(_sb_scores(_mm_nt(q2, kw), mask), later, c_l)
            return c_l, acc + _mm(a, vw)

        def step(state):
            w, _, carry = state
            carry = window(w, carry, False)
            return w + 1, _sb_alive(carry), carry

        carry = window(0, (jnp.zeros((2 * QBLK, 128), F32),) * 2, True)
        _, _, carry = lax.while_loop(lambda s: (s[0] < n_steps) & (s[1] > 0), step,
                                     (jnp.int32(1), _sb_alive(carry), carry))
        o_ref[...] = jnp.where(first, carry[1][:QBLK], carry[1][QBLK:])
        finish()

    qblk = pl.BlockSpec((QBLK, 128), lambda h, i: (i, h))
    kblk = pl.BlockSpec((T, 128), lambda h, i: (0, h))
    outs = _call(
        body, name=name, grid=(D // 128, T // QBLK), out_shape=[jax.ShapeDtypeStruct((T, D), F32)] + lands,
        in_specs=[_SB_DIAG_SPEC, qblk, kblk, kblk] + [ANY_SPEC] * nc, out_specs=[qblk] + [ANY_SPEC] * nc,
        scratch_shapes=sems, compiler_params=_params(32, 2),
    )(_sb_diagonal(), q, k, v, *cargo)
    return outs[0], outs[1:]


def _sb_bwd(q, k, v, o, do, cargo, name):
    T, D = q.shape
    nc = len(cargo)
    lands, sems = _exchange_shapes(cargo, True)

    def body(diag_ref, q_ref, k_ref, v_ref, o_ref, do_ref, *rest):
        dq_ref, dk_ref, dv_ref = rest[nc:nc + 3]
        finish = _ride_along(rest[:nc] + rest[nc + 3:], nc, True)
        i = pl.program_id(1)

        @pl.when(i == 0)
        def _():
            dk_ref[...] = jnp.zeros_like(dk_ref)
            dv_ref[...] = jnp.zeros_like(dv_ref)

        n_steps = (i + 2) // 2
        later, later_eq, first = _sb_consts()
        qv = q_ref[...]
        dob = do_ref[...].astype(MXU)
        zero = jnp.zeros_like(qv)
        q2 = jnp.concatenate([jnp.where(first, qv, zero), jnp.where(first, zero, qv)], axis=0)
        do2 = jnp.concatenate([jnp.where(first, dob, zero), jnp.where(first, zero, dob)], axis=0)
        prod = o_ref[...] * dob.astype(F32)
        prod2 = jnp.concatenate([jnp.where(first, prod, 0.0), jnp.where(first, 0.0, prod)], axis=0)
        total = _split_dot(prod2, jnp.ones((128, 128), MXU), 3)
        total = jnp.concatenate([total, total], axis=1)

        def window(w, carry, diagonal):
            start, mask = _sb_window(i, w)
            if diagonal:
                mask = (diag_ref[0], diag_ref[1])
            kw = k_ref[pl.ds(start, SB_KEYS), :]
            vw = v_ref[pl.ds(start, SB_KEYS), :]
            c_l, c_e, dq = carry
            log_sig, parts = _sb_scores(_mm_nt(q2, kw), mask)
            a, c_l = _sb_weights((log_sig, parts), later, c_l)
            ab = a.astype(MXU)
            e = ab.astype(F32) * _mm_nt(do2, vw)
            from_here, c_e = _sb_suffix(_sb_terms(e, 2), later_eq, c_e)
            sig = jnp.exp(log_sig)
            dzb = (e * (1.0 - sig) - sig * (total - from_here)).astype(MXU)
            dk_ref[pl.ds(start, SB_KEYS), :] += _mm_tn(dzb, q2)
            dv_ref[pl.ds(start, SB_KEYS), :] += _mm_tn(ab, do2)
            return c_l, c_e, dq + _mm(dzb, kw)

        def step(state):
            w, _, carry = state
            carry = window(w, carry, False)
            return w + 1, _sb_alive(carry), carry

        carry = window(0, (jnp.zeros((2 * QBLK, 128), F32),) * 3, True)
        _, _, carry = lax.while_loop(lambda s: (s[0] < n_steps) & (s[1] > 0), step,
                                     (jnp.int32(1), _sb_alive(carry), carry))
        dq_ref[...] = jnp.where(first, carry[2][:QBLK], carry[2][QBLK:]) * SCALE
        finish()

    qblk = pl.BlockSpec((QBLK, 128), lambda h, i: (i, h))
    kblk = pl.BlockSpec((T, 128), lambda h, i: (0, h))
    full = jax.ShapeDtypeStruct((T, D), F32)
    outs = _call(
        body, name=name, grid=(D // 128, T // QBLK), out_shape=[full, full, full] + lands,
        in_specs=[_SB_DIAG_SPEC, qblk, kblk, kblk, qblk, qblk] + [ANY_SPEC] * nc,
        out_specs=[qblk, kblk, kblk] + [ANY_SPEC] * nc, scratch_shapes=sems, compiler_params=_params(32, 2),
    )(_sb_diagonal(), q, k, v, o, do, *cargo)
    return outs[0], outs[1], outs[2], outs[3:]


def _proj_res(x, a, w, name):
    T, D = x.shape
    tm = min(TOKEN_TILE, T)

    def body(x_ref, a_ref, w_ref, o_ref):
        o_ref[...] = x_ref[...] + _mm(a_ref[...], w_ref[...])

    return _call(
        body, name=name, grid=(T // tm,), out_shape=jax.ShapeDtypeStruct((T, D), F32),
        in_specs=[_tile(tm, D), _tile(tm, a.shape[1]), _whole(w.shape)], out_specs=_tile(tm, D),
        compiler_params=_params(32),
    )(x, a, w)


def _proj_nt(g, w, name):
    T = g.shape[0]
    K = w.shape[0]
    tm = min(TOKEN_TILE, T)

    def body(g_ref, w_ref, o_ref):
        o_ref[...] = _mm_nt(g_ref[...], w_ref[...])

    return _call(
        body, name=name, grid=(T // tm,), out_shape=jax.ShapeDtypeStruct((T, K), F32),
        in_specs=[_tile(tm, g.shape[1]), _whole(w.shape)], out_specs=_tile(tm, K),
        compiler_params=_params(32),
    )(g, w)


def _loss_grad(y, tgt, name):
    T, D = y.shape
    tm = min(TOKEN_TILE, T)

    def body(y_ref, t_ref, dy_ref, loss_ref):
        @pl.when(pl.program_id(0) == 0)
        def _():
            loss_ref[...] = jnp.zeros_like(loss_ref)
        diff = y_ref[...] - t_ref[...]
        dy_ref[...] = diff * (1.0 / D)
        rows = jnp.sum(diff * diff, axis=1, keepdims=True) * (1.0 / D)
        loss_ref[...] += 0.5 * jnp.sum(rows, axis=0, keepdims=True)

    return _call(
        body, name=name, grid=(T // tm,),
        out_shape=[jax.ShapeDtypeStruct((T, D), F32), jax.ShapeDtypeStruct((1, 1), F32)],
        in_specs=[_tile(tm, D), _tile(tm, D)], out_specs=[_tile(tm, D), _acc((1, 1))],
        compiler_params=_params(32),
    )(y, tgt)


def _ple_bwd(dx, x, gate, pp, ln, w_g, name):
    T, D = x.shape
    tm = min(TOKEN_TILE, T)

    def body(dx_ref, x_ref, gate_ref, pp_ref, ln_ref, wg_ref, dxo_ref, dpp_ref, dgp_ref, dln_ref):
        @pl.when(pl.program_id(0) == 0)
        def _():
            dln_ref[...] = jnp.zeros_like(dln_ref)
        dxv = dx_ref[...]
        gate = gate_ref[...]
        _, xhat, rstd = _rms(x_ref[...], ln_ref[...])
        dpp_ref[...] = (dxv * gate).astype(MXU)
        dgp = (dxv * pp_ref[...] * gate * (1.0 - gate)).astype(MXU)
        dgp_ref[...] = dgp
        dxn, dln = _rms_bwd(_mm_nt(dgp, wg_ref[...]), xhat, rstd, ln_ref[...])
        dln_ref[...] += dln
        dxo_ref[...] = dxn + dxv

    return _call(
        body, name=name, grid=(T // tm,),
        out_shape=[jax.ShapeDtypeStruct((T, D), F32), jax.ShapeDtypeStruct((T, D), MXU),
                   jax.ShapeDtypeStruct((T, D), MXU), jax.ShapeDtypeStruct(ln.shape, F32)],
        in_specs=[_tile(tm, D)] * 4 + [_whole(ln.shape), _whole(w_g.shape)],
        out_specs=[_tile(tm, D), _tile(tm, D), _tile(tm, D), _acc(ln.shape)],
        compiler_params=_params(32),
    )(dx, x, gate, pp, ln, w_g)


def _mlp_bwd(dx, x, pre, ln, w_up, w_down, cargo, name):
    T, D = x.shape
    tm = min(TOKEN_TILE, T)
    nf = w_up.shape[2]
    F = nf * NDEV
    nc = len(cargo)
    lands, sems = _exchange_shapes(cargo, True)

    def body(dx_ref, x_ref, pre_ref, ln_ref, wup_ref, wdown_ref, *rest):
        dxo_ref, dpre_ref, s_ref, dln_ref = rest[nc:nc + 4]
        finish = _ride_along(rest[:nc] + rest[nc + 4:], nc, True, rank=1)

        @pl.when(pl.program_id(0) == 0)
        def _():
            dln_ref[...] = jnp.zeros_like(dln_ref)
        dxv = dx_ref[...]
        _, xhat, rstd = _rms(x_ref[...], ln_ref[...])
        a = jnp.maximum(pre_ref[...], 0.0)
        s_ref[...] = (a * a).astype(MXU)
        dpre_ref[...] = (_mm_nt(dxv, wdown_ref[...]) * (2.0 * a)).astype(MXU)
        dh = _mm_nt(dpre_ref[:, :nf], wup_ref[0])
        for j in range(1, NDEV):
            dh += _mm_nt(dpre_ref[:, j * nf:(j + 1) * nf], wup_ref[j])
        dxn, dln = _rms_bwd(dh, xhat, rstd, ln_ref[...])
        dln_ref[...] += dln
        dxo_ref[...] = dxn + dxv
        finish()

    outs = _call(
        body, name=name, grid=(T // tm,),
        out_shape=[jax.ShapeDtypeStruct((T, D), F32), jax.ShapeDtypeStruct((T, F), MXU),
                   jax.ShapeDtypeStruct((T, F), MXU), jax.ShapeDtypeStruct(ln.shape, F32)] + lands,
        in_specs=[_tile(tm, D), _tile(tm, D), _tile(tm, F), _whole(ln.shape), _whole(w_up.shape),
                  _whole(w_down.shape)] + [ANY_SPEC] * nc,
        out_specs=[_tile(tm, D), _tile(tm, F), _tile(tm, F), _acc(ln.shape)] + [ANY_SPEC] * nc,
        scratch_shapes=sems, compiler_params=_params(56),
    )(dx, x, pre, ln, w_up, w_down, *cargo)
    return outs[0], outs[1], outs[2], outs[3], outs[4:]


def _qkv_bwd(dx, x, dq, dk, dv, q_pre, k_pre, ln_q, ln_kv, g_q, g_k, w_q, w_kv, cargo, name):
    T, D = x.shape
    tm = min(TOKEN_TILE, T)
    nk = w_kv.shape[2]
    n_tiles = T // tm
    nc = len(cargo)
    lands, sems = _exchange_shapes(cargo, True)

    def body(dx_ref, x_ref, dq_ref, dk_ref, dv_ref, qpre_ref, kpre_ref, lnq_ref, lnkv_ref, gq_ref, gk_ref,
             wq_ref, wkv_ref, *rest):
        dxo_ref, dqp_ref, dkv_ref, dlnq_ref, dlnkv_ref, dgq_ref, dgk_ref = rest[nc:nc + 7]
        gq_acc, gk_acc = rest[2 * nc + 7:2 * nc + 9]
        finish = _ride_along(rest[:nc] + rest[nc + 7:2 * nc + 7] + rest[2 * nc + 9:], nc, True, rank=1)
        i = pl.program_id(0)

        @pl.when(i == 0)
        def _():
            dlnq_ref[...] = jnp.zeros_like(dlnq_ref)
            dlnkv_ref[...] = jnp.zeros_like(dlnkv_ref)
            gq_acc[...] = jnp.zeros_like(gq_acc)
            gk_acc[...] = jnp.zeros_like(gk_acc)

        ones = _head_ones()
        for b in range(D // 128):
            cols = slice(b * 128, (b + 1) * 128)
            _, xh, rs = _head_rms(qpre_ref[:, cols], gq_ref[:, cols], ones)
            d, dg = _head_rms_bwd(dq_ref[:, cols], xh, rs, gq_ref[:, cols], ones)
            dqp_ref[:, cols] = d.astype(MXU)
            gq_acc[:, cols] += dg
            _, xh, rs = _head_rms(kpre_ref[:, cols], gk_ref[:, cols], ones)
            d, dg = _head_rms_bwd(dk_ref[:, cols], xh, rs, gk_ref[:, cols], ones)
            dkv_ref[:, cols] = d.astype(MXU)
            gk_acc[:, cols] += dg
        dkv_ref[:, D:] = dv_ref[...].astype(MXU)

        _, xhat, rstd = _rms(x_ref[...], lnq_ref[...])
        dhq = _mm_nt(dqp_ref[...], wq_ref[...])
        dhkv = _mm_nt(dkv_ref[:, :nk], wkv_ref[0])
        for j in range(1, NDEV):
            dhkv += _mm_nt(dkv_ref[:, j * nk:(j + 1) * nk], wkv_ref[j])
        dxq, dlnq = _rms_bwd(dhq, xhat, rstd, lnq_ref[...])
        dxkv, dlnkv = _rms_bwd(dhkv, xhat, rstd, lnkv_ref[...])
        dlnq_ref[...] += dlnq
        dlnkv_ref[...] += dlnkv
        dxo_ref[...] = dx_ref[...] + dxq + dxkv

        @pl.when(i == n_tiles - 1)
        def _():
            row = lax.broadcasted_iota(jnp.int32, (D, 128), 0)
            col = lax.broadcasted_iota(jnp.int32, (D, 128), 1)
            fold = (jnp.bitwise_and(row, HEAD_DIM - 1) == col).astype(MXU)
            dgq_ref[...] = _split_dot(jnp.broadcast_to(gq_acc[...], (8, D)), fold, 3)
            dgk_ref[...] = _split_dot(jnp.broadcast_to(gk_acc[...], (8, D)), fold, 3)

        finish()

    small = jax.ShapeDtypeStruct((8, 128), F32)
    outs = _call(
        body, name=name, grid=(n_tiles,),
        out_shape=[jax.ShapeDtypeStruct((T, D), F32), jax.ShapeDtypeStruct((T, D), MXU),
                   jax.ShapeDtypeStruct((T, 2 * D), MXU), jax.ShapeDtypeStruct(ln_q.shape, F32),
                   jax.ShapeDtypeStruct(ln_kv.shape, F32), small, small] + lands,
        in_specs=[_tile(tm, D)] * 7 + [_whole(ln_q.shape), _whole(ln_kv.shape), _whole(g_q.shape),
                                       _whole(g_k.shape), _whole(w_q.shape), _whole(w_kv.shape)] + [ANY_SPEC] * nc,
        out_specs=[_tile(tm, D), _tile(tm, D), _tile(tm, 2 * D), _acc(ln_q.shape), _acc(ln_kv.shape),
                   _acc((8, 128)), _acc((8, 128))] + [ANY_SPEC] * nc,
        scratch_shapes=[pltpu.VMEM((1, D), F32), pltpu.VMEM((1, D), F32)] + sems,
        compiler_params=_params(48),
    )(dx, x, dq, dk, dv, q_pre, k_pre, ln_q, ln_kv, g_q, g_k, w_q, w_kv, *cargo)
    return outs[:7], outs[7:]


def _sgu_bwd(dx, x, z, ln, w_in, g_v, ws, wsT, bT, w_out, cargo, name):
    T, D = x.shape
    tm = min(TOKEN_TILE, T)
    nw = w_in.shape[2]
    nc = len(cargo)
    lands, sems = _exchange_shapes(cargo, True)

    def body(dx_ref, x_ref, z_ref, ln_ref, win_ref, gv_ref, ws_ref, wsT_ref, bT_ref, wout_ref, *rest):
        dxo_ref, dz_ref, dws_ref, dbT_ref, dln_ref, dgv_ref = rest[nc:nc + 6]
        mix_ref, dvn_ref = rest[2 * nc + 6:2 * nc + 8]
        finish = _ride_along(rest[:nc] + rest[nc + 6:2 * nc + 6] + rest[2 * nc + 8:], nc, True, rank=1)

        @pl.when(pl.program_id(0) == 0)
        def _():
            dws_ref[...] = jnp.zeros_like(dws_ref)
            dbT_ref[...] = jnp.zeros_like(dbT_ref)
            dln_ref[...] = jnp.zeros_like(dln_ref)
            dgv_ref[...] = jnp.zeros_like(dgv_ref)
        dxv = dx_ref[...]
        _, xhat, rstd = _rms(x_ref[...], ln_ref[...])
        u, du = _gelu_and_grad(z_ref[:, :D])
        gv, dgv = _gelu_and_grad(z_ref[:, D:])
        vn, vhat, rstd_v = _rms(gv, gv_ref[...])
        vnb = vn.astype(MXU)
        _spatial_mix(vnb, ws_ref, bT_ref, mix_ref, tm)
        dy = _mm_nt(dxv, wout_ref[...])
        d_u = dy * mix_ref[...]
        d_mix = dy * u
        dmb = d_mix.astype(MXU)
        tri, triT = _tril_mask()
        for g in range(GROUPS):
            wmT = jnp.where(triT, wsT_ref[g], 0.0).astype(MXU)
            cols = slice(g * CHUNK, (g + 1) * CHUNK)
            for ch in range(tm // CHUNK):
                rows = slice(ch * CHUNK, (ch + 1) * CHUNK)
                dm = dmb[rows, cols]
                dws_ref[g] += jnp.where(tri, _mm_nt(dm, vnb[rows, cols]), 0.0)
                dbT_ref[:, g:g + 1] += jnp.sum(d_mix[rows, cols], axis=1, keepdims=True)
                dvn_ref[rows, cols] = _mm(wmT, dm)
        d_gv, dg = _rms_bwd(dvn_ref[...], vhat, rstd_v, gv_ref[...])
        dgv_ref[...] += dg
        dz_ref[:, :D] = (d_u * du).astype(MXU)
        dz_ref[:, D:] = (d_gv * dgv).astype(MXU)
        dh = _mm_nt(dz_ref[:, :nw], win_ref[0])
        for j in range(1, NDEV):
            dh += _mm_nt(dz_ref[:, j * nw:(j + 1) * nw], win_ref[j])
        dxn, dln = _rms_bwd(dh, xhat, rstd, ln_ref[...])
        dln_ref[...] += dln
        dxo_ref[...] = dxn + dxv
        finish()

    outs = _call(
        body, name=name, grid=(T // tm,),
        out_shape=[jax.ShapeDtypeStruct((T, D), F32), jax.ShapeDtypeStruct((T, 2 * D), MXU),
                   jax.ShapeDtypeStruct(ws.shape, F32), jax.ShapeDtypeStruct(bT.shape, F32),
                   jax.ShapeDtypeStruct(ln.shape, F32), jax.ShapeDtypeStruct(g_v.shape, F32)] + lands,
        in_specs=[_tile(tm, D), _tile(tm, D), _tile(tm, 2 * D), _whole(ln.shape), _whole(w_in.shape),
                  _whole(g_v.shape), _whole(ws.shape), _whole(wsT.shape), _whole(bT.shape), _whole(w_out.shape)]
        + [ANY_SPEC] * nc,
        out_specs=[_tile(tm, D), _tile(tm, 2 * D), _acc(ws.shape), _acc(bT.shape), _acc(ln.shape),
                   _acc(g_v.shape)] + [ANY_SPEC] * nc,
        scratch_shapes=[pltpu.VMEM((tm, D), F32), pltpu.VMEM((tm, D), F32)] + sems,
        compiler_params=_params(48),
    )(dx, x, z, ln, w_in, g_v, ws, wsT, bT, w_out, *cargo)
    return outs[:6], outs[6:]


def _wgrad_rows(a, g, name):
    T, K = a.shape
    N = g.shape[1]
    kb = K // NDEV

    def body(a_ref, g_ref, o_ref):
        o_ref[...] = _mm_tn(a_ref[...], g_ref[...]).astype(COMM)

    return _call(
        body, name=name, grid=(NDEV,), out_shape=jax.ShapeDtypeStruct((K, N), COMM),
        in_specs=[pl.BlockSpec((T, kb), lambda j: (0, j)), _whole(g.shape)],
        out_specs=pl.BlockSpec((kb, N), lambda j: (j, 0)),
        compiler_params=_params(40),
    )(a, g).reshape(NDEV, kb, N)


def _wgrad_cols(a, g, name):
    T, K = a.shape
    N = g.shape[1]
    nb = N // NDEV

    def body(a_ref, g_ref, o_ref):
        o_ref[...] = _mm_tn(a_ref[...], g_ref[...]).astype(COMM)

    return _call(
        body, name=name, grid=(NDEV,), out_shape=jax.ShapeDtypeStruct((NDEV, K, nb), COMM),
        in_specs=[_whole(a.shape), pl.BlockSpec((T, nb), lambda j: (0, j))],
        out_specs=pl.BlockSpec((None, K, nb), lambda j: (j, 0, 0)),
        compiler_params=_params(40),
    )(a, g)


def _adamw_rows(R, C):
    tr = math.gcd(R, max(8, (128 * 1024) // C))
    return R if tr < 64 else tr


def _adamw_update(w_ref, m_ref, v_ref, s_ref, g_ref, d_ref, mo_ref, vo_ref):
    g = s_ref[0].astype(F32)
    for j in range(1, s_ref.shape[0]):
        g = g + s_ref[j].astype(F32)
    mn = ADAM_B1 * m_ref[...] + (1.0 - ADAM_B1) * g
    vn = ADAM_B2 * v_ref[...] + (1.0 - ADAM_B2) * (g * g)
    g_ref[...] = g
    mo_ref[...] = mn
    vo_ref[...] = vn
    m_hat = mn / (1.0 - ADAM_B1 ** ADAM_STEP)
    v_hat = vn / (1.0 - ADAM_B2 ** ADAM_STEP)
    d_ref[...] = -ADAM_LR * (m_hat / (jnp.sqrt(v_hat) + ADAM_EPS) + ADAM_WD * w_ref[...])


def _adamw_layers(w, m, v, slots0, slots1, name):
    _, R, C = w.shape
    tr = _adamw_rows(R, C)
    last = R // tr - 1

    def body(w_ref, m_ref, v_ref, s0_ref, s1_ref, *outs):
        @pl.when(pl.program_id(0) == 0)
        def _():
            _adamw_update(w_ref, m_ref, v_ref, s0_ref, *outs)

        @pl.when(pl.program_id(0) == 1)
        def _():
            _adamw_update(w_ref, m_ref, v_ref, s1_ref, *outs)

    blk = pl.BlockSpec((None, tr, C), lambda l, i: (l, i, 0))
    s0_blk = pl.BlockSpec((slots0.shape[0], tr, C), lambda l, i: (0, jnp.where(l == 0, i, last), 0))
    s1_blk = pl.BlockSpec((slots1.shape[0], tr, C), lambda l, i: (0, jnp.where(l == 1, i, 0), 0))
    out = jax.ShapeDtypeStruct(w.shape, F32)
    return _call(
        body, name=name, grid=(2, R // tr), out_shape=[out, out, out, out],
        in_specs=[blk, blk, blk, s0_blk, s1_blk], out_specs=[blk] * 4, compiler_params=_params(32, 2),
    )(w, m, v, slots0, slots1)


def _adamw(w, m, v, slots, name):
    R, C = w.shape
    n = slots.shape[0]
    tr = _adamw_rows(R, C)

    def body(*refs):
        _adamw_update(*refs)

    blk = pl.BlockSpec((tr, C), lambda i: (i, 0))
    out = jax.ShapeDtypeStruct((R, C), F32)
    return _call(
        body, name=name, grid=(R // tr,), out_shape=[out, out, out, out],
        in_specs=[blk, blk, blk, pl.BlockSpec((n, tr, C), lambda i: (0, i, 0))], out_specs=[blk] * 4,
        compiler_params=_params(32),
    )(w, m, v, slots)


def _rows128(a):
    flat = a.reshape(-1)
    rows = -(-flat.shape[0] // 1024) * 8
    flat = jnp.pad(flat, (0, rows * 128 - flat.shape[0]))
    return flat.reshape(rows, 128)


def kernel(x, p, ln_mix_a, w_in_a, g_v_a, w_spatial, b_spatial, w_out_a, ln_kv, w_kv, g_k, ln_mix_b, w_q, g_q, w_out_b, ln_mlp, w_up, w_down, ln_ple, w_ple_gate, w_ple_proj, loss_target, m_ln_mix_a, m_w_in_a, m_g_v_a, m_w_spatial, m_b_spatial, m_w_out_a, m_ln_kv, m_w_kv, m_g_k, m_ln_mix_b, m_w_q, m_g_q, m_w_out_b, m_ln_mlp, m_w_up, m_w_down, m_ln_ple, m_w_ple_gate, m_w_ple_proj, v_ln_mix_a, v_w_in_a, v_g_v_a, v_w_spatial, v_b_spatial, v_w_out_a, v_ln_kv, v_w_kv, v_g_k, v_ln_mix_b, v_w_q, v_g_q, v_w_out_b, v_ln_mlp, v_w_up, v_w_down, v_ln_ple, v_w_ple_gate, v_w_ple_proj):
    me = 4 * lax.axis_index("x") + 2 * lax.axis_index("y") + lax.axis_index("c")
    D = x.shape[2]
    x0, tgt = x[0], loss_target[0]
    n_layers = w_up.shape[0]

    c = lambda w: w.astype(COMM)
    first = [c(w_in_a[0]), c(w_out_a[0]), ln_mix_a, g_v_a, c(w_up[0]), c(w_down[0]), c(w_ple_gate[0]),
             c(w_ple_proj[0]), c(w_q[0]), c(w_kv)]
    second_small = [c(w_out_b[0]), c(w_ple_gate[1]), c(w_ple_proj[1])]
    second_big = [c(w_up[1]), c(w_down[1])]
    W_in, W_out_a, ln_a, gv_a, W_up0, W_down0, W_g0, W_pp0, W_q, W_kv = _gather_two_level(first, "gather_first")
    W_out_a, ln_a, gv_a = W_out_a.reshape(-1, D), ln_a.reshape(1, D), gv_a.reshape(1, D)
    W_down0, W_g0, W_q = W_down0.reshape(-1, D), W_g0.reshape(-1, D), W_q.reshape(-1, D)
    ws = w_spatial[0]
    wsT = jnp.swapaxes(ws, 1, 2)
    bT = b_spatial[0].T
    ln_kv2, ln_b = ln_kv.reshape(1, D), ln_mix_b
    gk2 = jnp.tile(g_k.reshape(1, HEAD_DIM), (1, D // HEAD_DIM))
    gq2 = jnp.tile(g_q, (1, D // HEAD_DIM))
    ln_m = [ln_mlp[l:l + 1] for l in range(n_layers)]
    ln_p = [ln_ple[l:l + 1] for l in range(n_layers)]

    x1, z, h_a, y_a = _sgu_fwd(x0, ln_a, W_in, gv_a, ws, bT, W_out_a, "sgu_fwd")
    x2, pre0, hm0, (W_out_b, W_g1, W_pp1) = _mlp_fwd(x1, ln_m[0], W_up0, W_down0, second_small, "mlp_fwd0")
    x3, gate0, pp0, hp0 = _ple_fwd(x2, p[0, 0], ln_p[0], W_g0, W_pp0, "ple_fwd0")
    qn, kn, vn, q_pre, k_pre, h_q, h_kv = _qkv_fwd(x3, ln_b, ln_kv2, gq2, gk2, W_q, W_kv, "qkv_fwd")
    o2d, (W_up1, W_down1) = _sb_fwd(qn, kn, vn, second_big, "sb_fwd")
    W_out_b, W_down1, W_g1 = W_out_b.reshape(-1, D), W_down1.reshape(-1, D), W_g1.reshape(-1, D)
    x4 = _proj_res(x3, o2d, W_out_b, "attn_out")
    x5, pre1, hm1, _ = _mlp_fwd(x4, ln_m[1], W_up1, W_down1, [], "mlp_fwd1")
    x6, gate1, pp1, hp1 = _ple_fwd(x5, p[1, 0], ln_p[1], W_g1, W_pp1, "ple_fwd1")
    dy, loss_part = _loss_grad(x6, tgt, "loss_grad")

    dx5, dpp1, dgp1, dlnp1 = _ple_bwd(dy, x5, gate1, pp1, ln_p[1], W_g1, "ple_bwd1")
    dx4, dpre1, s1, dlnm1, _ = _mlp_bwd(dx5, x4, pre1, ln_m[1], W_up1, W_down1, [], "mlp_bwd1")
    wg_big = [_wgrad_cols(hm1, dpre1, "wg_up1"), _wgrad_rows(s1, dx5, "wg_down1")]
    wg_small = [_wgrad_rows(hp1, dgp1, "wg_gate1"), _wgrad_cols(p[1, 0].astype(MXU), dpp1, "wg_proj1"),
                _wgrad_rows(o2d, dx4, "wg_out_b")]
    do2d = _proj_nt(dx4, W_out_b, "attn_out_bwd")
    dqn, dkn, dvn, (s_up1, s_down1) = _sb_bwd(qn, kn, vn, o2d, do2d, wg_big, "sb_bwd")
    (dx3, dq_pre, dkv, dlnb, dlnkv, dgq, dgk), (s_gate1, s_proj1, s_out_b) = _qkv_bwd(
        dx4, x3, dqn, dkn, dvn, q_pre, k_pre, ln_b, ln_kv2, gq2, gk2, W_q, W_kv, wg_small, "qkv_bwd")
    dgq, dgk = dgq[:1, :HEAD_DIM], dgk[:1, :HEAD_DIM]
    dx2, dpp0, dgp0, dlnp0 = _ple_bwd(dx3, x2, gate0, pp0, ln_p[0], W_g0, "ple_bwd0")
    wg_qkv = [_wgrad_rows(h_q, dq_pre, "wg_q"), _wgrad_cols(h_kv, dkv, "wg_kv")]
    dx1, dpre0, s0, dlnm0, (s_q, s_kv) = _mlp_bwd(dx2, x1, pre0, ln_m[0], W_up0, W_down0, wg_qkv, "mlp_bwd0")
    wg_ple = [_wgrad_rows(hp0, dgp0, "wg_gate0"), _wgrad_cols(p[0, 0].astype(MXU), dpp0, "wg_proj0")]
    (dx0, dz, dws, dbT, dlna, dgva), (s_gate0, s_proj0) = _sgu_bwd(
        dx1, x0, z, ln_a, W_in, gv_a, ws, wsT, bT, W_out_a, wg_ple, "sgu_bwd")
    wg_first = [_wgrad_cols(hm0, dpre0, "wg_up0"), _wgrad_rows(s0, dx2, "wg_down0"), _wgrad_cols(h_a, dz, "wg_in_a"),
                _wgrad_rows(y_a, dx1, "wg_out_a")]

    small = [("b_spatial", dbT.T[None], b_spatial, m_b_spatial, v_b_spatial),
             ("ln_kv", dlnkv.reshape(-1), ln_kv, m_ln_kv, v_ln_kv),
             ("g_k", dgk.reshape(-1), g_k, m_g_k, v_g_k),
             ("ln_mix_b", dlnb, ln_mix_b, m_ln_mix_b, v_ln_mix_b),
             ("g_q", dgq, g_q, m_g_q, v_g_q),
             ("ln_mlp", jnp.concatenate([dlnm0, dlnm1]), ln_mlp, m_ln_mlp, v_ln_mlp),
             ("ln_ple", jnp.concatenate([dlnp0, dlnp1]), ln_ple, m_ln_ple, v_ln_ple)]
    sharded_vec = [("ln_mix_a", dlna, ln_mix_a, m_ln_mix_a, v_ln_mix_a),
                   ("g_v_a", dgva, g_v_a, m_g_v_a, v_g_v_a)]
    packs = [[], [], [], []]
    for _, g, w, m, v in small:
        for lst, a in zip(packs, (g, w, m, v)):
            lst.append(_rows128(a))
    for _, g, w, m, v in sharded_vec:
        packs[0].append(g.reshape(NDEV, -1))
        for lst, a in zip(packs[1:], (w, m, v)):
            lst.append(jnp.broadcast_to(a, (NDEV, a.shape[1])))
    packs[0].append(_rows128(loss_part))
    for lst in packs[1:]:
        lst.append(jnp.zeros((8, 128), F32))
    g_pack, w_pack, m_pack, v_pack = (jnp.concatenate(lst) for lst in packs)
    g_pack8 = jnp.broadcast_to(g_pack[None], (NDEV,) + g_pack.shape)
    dws8 = jnp.broadcast_to(dws.reshape(1, -1, 128).astype(COMM), (NDEV, dws.size // 128, 128))
    by_chip = [a.reshape((4, 2) + a.shape[1:]) for a in wg_first]
    from_sibling, (g_all, s_ws) = _scatter_pair(by_chip, [g_pack8, dws8], "scatter_pair")
    my_core = lax.axis_index("c")
    chip_sums = [_pair_sum(lax.dynamic_index_in_dim(a, my_core, 1, keepdims=False), o, f"pair_sum{j}")
                 for j, (a, o) in enumerate(zip(by_chip, from_sibling))]
    s_up0, s_down0, s_in_a, s_out_a = _scatter_chips(chip_sums, "scatter_chips")

    def upd(w, m, v, s, name):
        shape = w.shape
        outs = _adamw(w.reshape(-1, shape[-1]), m.reshape(-1, shape[-1]), v.reshape(-1, shape[-1]), s, name)
        return [o.reshape(shape) for o in outs]

    res = {}
    res["w_out_b"] = upd(w_out_b, m_w_out_b, v_w_out_b, s_out_b, "adam_out_b")
    res["w_q"] = upd(w_q, m_w_q, v_w_q, s_q, "adam_q")
    res["w_kv"] = upd(w_kv, m_w_kv, v_w_kv, s_kv, "adam_kv")
    res["w_in_a"] = upd(w_in_a, m_w_in_a, v_w_in_a, s_in_a, "adam_in_a")
    res["w_out_a"] = upd(w_out_a, m_w_out_a, v_w_out_a, s_out_a, "adam_out_a")
    res["w_up"] = _adamw_layers(w_up, m_w_up, v_w_up, s_up0, s_up1, "adam_up")
    res["w_down"] = _adamw_layers(w_down, m_w_down, v_w_down, s_down0, s_down1, "adam_down")
    res["w_ple_gate"] = _adamw_layers(w_ple_gate, m_w_ple_gate, v_w_ple_gate, s_gate0, s_gate1, "adam_gate")
    res["w_ple_proj"] = _adamw_layers(w_ple_proj, m_w_ple_proj, v_w_ple_proj, s_proj0, s_proj1, "adam_proj")

    res["w_spatial"] = [o.reshape(w_spatial.shape) for o in _adamw(
        w_spatial.reshape(-1, 128), m_w_spatial.reshape(-1, 128), v_w_spatial.reshape(-1, 128), s_ws, "adam_spatial")]
    outs = _adamw(w_pack, m_pack, v_pack, g_all, "adam_small")
    loss = outs[0][-8, 0]
    row = 0
    for nm, g, w, m, v in small:
        nrows = _rows128(w).shape[0]
        res[nm] = [o[row:row + nrows].reshape(-1)[:w.size].reshape(w.shape) for o in outs]
        row += nrows
    for nm, g, w, m, v in sharded_vec:
        res[nm] = [lax.dynamic_slice_in_dim(o[row:row + NDEV], me, 1, axis=0) for o in outs]
        row += NDEV

    names = ["ln_mix_a", "w_in_a", "g_v_a", "w_spatial", "b_spatial", "w_out_a", "ln_kv", "w_kv", "g_k", "ln_mix_b",
             "w_q", "g_q", "w_out_b", "ln_mlp", "w_up", "w_down", "ln_ple", "w_ple_gate", "w_ple_proj"]
    out = [loss, dx0[None]]
    for t in range(4):
        out += [res[nm][t] for nm in names]
    return tuple(out)
```

```python
import math

import jax
import jax.numpy as jnp
from jax import lax
from jax.experimental import pallas as pl
from jax.experimental.pallas import tpu as pltpu

F32 = jnp.float32
MXU = jnp.bfloat16
COMM = jnp.bfloat16
EPS = 1e-6
NDEV = 8
HEAD_DIM = 64
CHUNK = 128
GROUPS = 8
QBLK = 128
SCALE = HEAD_DIM ** -0.5
TOKEN_TILE = 256
ADAM_LR = 0.001
ADAM_B1 = 0.9
ADAM_B2 = 0.999
ADAM_EPS = 1e-08
ADAM_WD = 0.01
ADAM_STEP = 10
MESH = pl.DeviceIdType.MESH


def _call(body, **kw):
    return pl.pallas_call(body, **kw)


def _params(vmem_mb, n_axes=1):
    return pltpu.CompilerParams(dimension_semantics=("arbitrary",) * n_axes,
                                vmem_limit_bytes=vmem_mb << 20)


def _tile(tm, n):
    return pl.BlockSpec((tm, n), lambda i: (i, 0))


def _whole(shape):
    zeros = (0,) * len(shape)
    return pl.BlockSpec(shape, lambda i: zeros, pipeline_mode=pl.Buffered(1))


def _acc(shape):
    zeros = (0,) * len(shape)
    return pl.BlockSpec(shape, lambda i: zeros)


def _mm(a, b):
    return jnp.dot(a.astype(MXU), b.astype(MXU), preferred_element_type=F32)


def _mm_nt(a, b):
    return lax.dot_general(a.astype(MXU), b.astype(MXU), (((1,), (1,)), ((), ())),
                           preferred_element_type=F32)


def _mm_tn(a, b):
    return lax.dot_general(a.astype(MXU), b.astype(MXU), (((0,), (0,)), ((), ())),
                           preferred_element_type=F32)


def _split_dot(x, ones, terms=2):
    out = None
    for _ in range(terms):
        part = x.astype(MXU)
        x = x - part.astype(F32)
        d = jnp.dot(part, ones, preferred_element_type=F32)
        out = d if out is None else out + d
    return out


def _rms(x, g):
    rstd = lax.rsqrt(jnp.mean(x * x, axis=-1, keepdims=True) + EPS)
    xhat = x * rstd
    return xhat * g, xhat, rstd


def _rms_bwd(dh, xhat, rstd, g):
    dxh = dh * g
    dx = rstd * (dxh - xhat * jnp.mean(dxh * xhat, axis=-1, keepdims=True))
    dg = jnp.sum(dh * xhat, axis=0, keepdims=True)
    return dx, dg


_GELU_C = math.sqrt(2.0 / math.pi)


def _gelu(x):
    t = jnp.tanh(_GELU_C * (x + 0.044715 * (x * x * x)))
    return 0.5 * x * (1.0 + t)


def _gelu_and_grad(x):
    x2 = x * x
    t = jnp.tanh(_GELU_C * (x + 0.044715 * (x2 * x)))
    g = 0.5 * x * (1.0 + t)
    dg = 0.5 * (1.0 + t) + 0.5 * x * (1.0 - t * t) * (_GELU_C * (1.0 + 3.0 * 0.044715 * x2))
    return g, dg


def _softplus(z):
    return jnp.maximum(z, 0.0) + jnp.log(1.0 + jnp.exp(-jnp.abs(z)))


def _tril_mask():
    row = lax.broadcasted_iota(jnp.int32, (CHUNK, CHUNK), 0)
    col = lax.broadcasted_iota(jnp.int32, (CHUNK, CHUNK), 1)
    return row >= col, row <= col


ANY_SPEC = pl.BlockSpec(memory_space=pl.ANY)


def _my_index():
    return 4 * lax.axis_index("x") + 2 * lax.axis_index("y") + lax.axis_index("c")


def _exchange_copies(srcs, lands, send_sems, recv_sems, scatter, arriving):
    x, y, c = lax.axis_index("x"), lax.axis_index("y"), lax.axis_index("c")
    me = 4 * x + 2 * y + c
    out = []
    for a in range(len(srcs)):
        for k in range(NDEV - 1):
            bits = k + 1
            px = 1 - x if (bits >> 2) & 1 else x
            py = 1 - y if (bits >> 1) & 1 else y
            pc = 1 - c if bits & 1 else c
            peer = 4 * px + 2 * py + pc
            src = srcs[a].at[peer] if scatter else srcs[a]
            out.append(pltpu.make_async_remote_copy(
                src_ref=src, dst_ref=lands[a].at[peer if arriving else me],
                send_sem=send_sems.at[a * (NDEV - 1) + k], recv_sem=recv_sems.at[a * (NDEV - 1) + k],
                device_id=(px, py, pc), device_id_type=MESH))
    return out


def _exchange_shapes(arrs, scatter):
    n = len(arrs)
    if n == 0:
        return [], []
    lands = [jax.ShapeDtypeStruct(a.shape if scatter else (NDEV,) + a.shape, a.dtype) for a in arrs]
    sems = [pltpu.SemaphoreType.DMA((n * (NDEV - 1),)), pltpu.SemaphoreType.DMA((n * (NDEV - 1),)),
            pltpu.SemaphoreType.DMA((n,))]
    return lands, sems


def _exchange_start(srcs, lands, sems, scatter):
    send_sems, recv_sems, local_sems = sems
    me = _my_index()
    for a in range(len(srcs)):
        pltpu.make_async_copy(srcs[a].at[me] if scatter else srcs[a], lands[a].at[me], local_sems.at[a]).start()
    for send in _exchange_copies(srcs, lands, send_sems, recv_sems, scatter, False):
        send.start()


def _exchange_finish(srcs, lands, sems, scatter):
    send_sems, recv_sems, local_sems = sems
    me = _my_index()
    for arrive in _exchange_copies(srcs, lands, send_sems, recv_sems, scatter, True):
        arrive.wait_recv()
    for send in _exchange_copies(srcs, lands, send_sems, recv_sems, scatter, False):
        send.wait_send()
    for a in range(len(srcs)):
        pltpu.make_async_copy(srcs[a].at[me] if scatter else srcs[a], lands[a].at[me], local_sems.at[a]).wait()


def _gather_two_level(arrs, name):
    n = len(arrs)
    lands = [jax.ShapeDtypeStruct((NDEV,) + a.shape, a.dtype) for a in arrs]

    def body(*refs):
        srcs, outs = refs[:n], refs[n:2 * n]
        send_sems, recv_sems, local_sems = refs[2 * n:]
        x, y, c = lax.axis_index("x"), lax.axis_index("y"), lax.axis_index("c")
        me, sibling = (x, y, c), (x, y, 1 - c)
        chips = [(1 - x, y), (x, 1 - y), (1 - x, 1 - y)]

        def index(dev):
            return 4 * dev[0] + 2 * dev[1] + dev[2]

        def copy(a, k, block, to, src=None):
            dst = outs[a].at[index(block)]
            return pltpu.make_async_remote_copy(
                src_ref=dst if src is None else src, dst_ref=dst, send_sem=send_sems.at[a, k],
                recv_sem=recv_sems.at[a, k], device_id=to, device_id_type=MESH)

        mine, first, passed = [], [], []
        for a in range(n):
            cp = pltpu.make_async_copy(srcs[a], outs[a].at[index(me)], local_sems.at[a])
            cp.start()
            mine.append(cp)
            first.append(copy(a, 0, me, sibling, src=srcs[a]))
            first += [copy(a, 1 + j, me, (*chip, c), src=srcs[a]) for j, chip in enumerate(chips)]
        for cp in first:
            cp.start()
        for a in range(n):
            for j, chip in enumerate(chips):
                copy(a, 1 + j, (*chip, c), me).wait_recv()
                cp = copy(a, 4 + j, (*chip, c), sibling)
                cp.start()
                passed.append(cp)
        for a in range(n):
            copy(a, 0, sibling, me).wait_recv()
            for j, chip in enumerate(chips):
                copy(a, 4 + j, (*chip, 1 - c), me).wait_recv()
        for cp in first + passed:
            cp.wait_send()
        for cp in mine:
            cp.wait()

    return _call(body, name=name, out_shape=lands, in_specs=[ANY_SPEC] * n, out_specs=[ANY_SPEC] * n,
                 scratch_shapes=[pltpu.SemaphoreType.DMA((n, NDEV - 1)), pltpu.SemaphoreType.DMA((n, NDEV - 1)),
                                 pltpu.SemaphoreType.DMA((n,))])(*arrs)


def _scatter_pair(arrs, extra, name):
    n, ne = len(arrs), len(extra)
    lands = [jax.ShapeDtypeStruct((4,) + a.shape[2:], a.dtype) for a in arrs]
    extra_lands, extra_sems = _exchange_shapes(extra, True)

    def body(*refs):
        srcs, xsrc = refs[:n], refs[n:n + ne]
        outs, xout = refs[n + ne:2 * n + ne], refs[2 * n + ne:2 * (n + ne)]
        send_sems, recv_sems = refs[2 * (n + ne)], refs[2 * (n + ne) + 1]
        xsems = refs[2 * (n + ne) + 2:]
        x, y, c = lax.axis_index("x"), lax.axis_index("y"), lax.axis_index("c")
        _exchange_start(xsrc, xout, xsems, True)
        copies = [pltpu.make_async_remote_copy(
            src_ref=srcs[a].at[k, 1 - c], dst_ref=outs[a].at[k], send_sem=send_sems.at[a, k],
            recv_sem=recv_sems.at[a, k], device_id=(x, y, 1 - c), device_id_type=MESH)
            for a in range(n) for k in range(4)]
        for cp in copies:
            cp.start()
        for cp in copies:
            cp.wait()
        _exchange_finish(xsrc, xout, xsems, True)

    outs = _call(
        body, name=name, out_shape=lands + extra_lands, in_specs=[ANY_SPEC] * (n + ne), out_specs=[ANY_SPEC] * (n + ne),
        scratch_shapes=[pltpu.SemaphoreType.DMA((n, 4)), pltpu.SemaphoreType.DMA((n, 4))] + extra_sems,
    )(*arrs, *extra)
    return outs[:n], outs[n:]


def _pair_sum(own, other, name):
    _, R, C = own.shape
    tr = math.gcd(R, max(8, (128 * 1024) // C))

    def body(a_ref, b_ref, o_ref):
        o_ref[...] = (a_ref[...].astype(F32) + b_ref[...].astype(F32)).astype(COMM)

    blk = pl.BlockSpec((4, tr, C), lambda i: (0, i, 0))
    return _call(body, name=name, grid=(R // tr,), out_shape=jax.ShapeDtypeStruct(own.shape, COMM),
                 in_specs=[blk, blk], out_specs=blk, compiler_params=_params(32))(own, other)


def _scatter_chips(arrs, name):
    n = len(arrs)
    lands = [jax.ShapeDtypeStruct(a.shape, a.dtype) for a in arrs]

    def body(*refs):
        srcs, outs = refs[:n], refs[n:2 * n]
        send_sems, recv_sems, local_sems = refs[2 * n:]
        x, y, c = lax.axis_index("x"), lax.axis_index("y"), lax.axis_index("c")
        chip = 2 * x + y
        others = [(1 - x, y), (x, 1 - y), (1 - x, 1 - y)]
        local = [pltpu.make_async_copy(srcs[a].at[chip], outs[a].at[chip], local_sems.at[a]) for a in range(n)]
        for cp in local:
            cp.start()

        def copies(arriving):
            return [pltpu.make_async_remote_copy(
                src_ref=srcs[a].at[2 * px + py], dst_ref=outs[a].at[2 * px + py if arriving else chip],
                send_sem=send_sems.at[a, j], recv_sem=recv_sems.at[a, j], device_id=(px, py, c), device_id_type=MESH)
                for a in range(n) for j, (px, py) in enumerate(others)]

        for cp in copies(False):
            cp.start()
        for cp in copies(True):
            cp.wait_recv()
        for cp in copies(False):
            cp.wait_send()
        for cp in local:
            cp.wait()

    return _call(body, name=name, out_shape=lands, in_specs=[ANY_SPEC] * n, out_specs=[ANY_SPEC] * n,
                 scratch_shapes=[pltpu.SemaphoreType.DMA((n, 3)), pltpu.SemaphoreType.DMA((n, 3)),
                                 pltpu.SemaphoreType.DMA((n,))])(*arrs)


def _spatial_mix(vnb, ws_ref, bT_ref, mix_ref, tm):
    tri, _ = _tril_mask()
    for g in range(GROUPS):
        wm = jnp.where(tri, ws_ref[g], 0.0).astype(MXU)
        cols = slice(g * CHUNK, (g + 1) * CHUNK)
        for ch in range(tm // CHUNK):
            rows = slice(ch * CHUNK, (ch + 1) * CHUNK)
            mix_ref[rows, cols] = _mm(wm, vnb[rows, cols]) + bT_ref[:, g:g + 1]


def _sgu_fwd(x, ln, w_in, g_v, ws, bT, w_out, name):
    T, D = x.shape
    tm = min(TOKEN_TILE, T)
    nw = w_in.shape[2]

    def body(x_ref, ln_ref, win_ref, gv_ref, ws_ref, bT_ref, wout_ref, xo_ref, z_ref, h_ref, y_ref, mix_ref):
        xv = x_ref[...]
        h, _, _ = _rms(xv, ln_ref[...])
        hb = h.astype(MXU)
        h_ref[...] = hb
        for j in range(NDEV):
            z_ref[:, j * nw:(j + 1) * nw] = _mm(hb, win_ref[j])
        u = _gelu(z_ref[:, :D])
        gv = _gelu(z_ref[:, D:])
        vn, _, _ = _rms(gv, gv_ref[...])
        _spatial_mix(vn.astype(MXU), ws_ref, bT_ref, mix_ref, tm)
        y = (u * mix_ref[...]).astype(MXU)
        y_ref[...] = y
        xo_ref[...] = xv + _mm(y, wout_ref[...])

    return _call(
        body, name=name, grid=(T // tm,),
        out_shape=[jax.ShapeDtypeStruct((T, D), F32), jax.ShapeDtypeStruct((T, 2 * D), F32),
                   jax.ShapeDtypeStruct((T, D), MXU), jax.ShapeDtypeStruct((T, D), MXU)],
        in_specs=[_tile(tm, D), _whole(ln.shape), _whole(w_in.shape), _whole(g_v.shape), _whole(ws.shape),
                  _whole(bT.shape), _whole(w_out.shape)],
        out_specs=[_tile(tm, D), _tile(tm, 2 * D), _tile(tm, D), _tile(tm, D)],
        scratch_shapes=[pltpu.VMEM((tm, D), F32)],
        compiler_params=_params(40),
    )(x, ln, w_in, g_v, ws, bT, w_out)


def _mlp_fwd(x, ln, w_up, w_down, cargo, name):
    T, D = x.shape
    tm = min(TOKEN_TILE, T)
    nf = w_up.shape[2]
    F = nf * NDEV
    nc = len(cargo)
    lands, sems = _gather_ride_shapes(cargo)

    def body(x_ref, ln_ref, wup_ref, wdown_ref, *rest):
        xo_ref, pre_ref, h_ref = rest[nc:nc + 3]
        finish = _ride_along_gather(rest[:nc] + rest[nc + 3:], nc, 1)
        xv = x_ref[...]
        h, _, _ = _rms(xv, ln_ref[...])
        hb = h.astype(MXU)
        h_ref[...] = hb
        for j in range(NDEV):
            pre_ref[:, j * nf:(j + 1) * nf] = _mm(hb, wup_ref[j])
        a = jnp.maximum(pre_ref[...], 0.0)
        xo_ref[...] = xv + _mm(a * a, wdown_ref[...])
        finish()

    outs = _call(
        body, name=name, grid=(T // tm,),
        out_shape=[jax.ShapeDtypeStruct((T, D), F32), jax.ShapeDtypeStruct((T, F), F32),
                   jax.ShapeDtypeStruct((T, D), MXU)] + lands,
        in_specs=[_tile(tm, D), _whole(ln.shape), _whole(w_up.shape), _whole(w_down.shape)] + [ANY_SPEC] * nc,
        out_specs=[_tile(tm, D), _tile(tm, F), _tile(tm, D)] + [ANY_SPEC] * nc,
        scratch_shapes=sems, compiler_params=_params(52),
    )(x, ln, w_up, w_down, *cargo)
    return outs[0], outs[1], outs[2], outs[3:]


def _ple_fwd(x, p, ln, w_g, w_pp, name):
    T, D = x.shape
    tm = min(TOKEN_TILE, T)
    npp = w_pp.shape[2]

    def body(x_ref, p_ref, ln_ref, wg_ref, wpp_ref, xo_ref, gate_ref, pp_ref, h_ref):
        xv = x_ref[...]
        h, _, _ = _rms(xv, ln_ref[...])
        hb = h.astype(MXU)
        h_ref[...] = hb
        gate = jax.nn.sigmoid(_mm(hb, wg_ref[...]))
        gate_ref[...] = gate
        pb = p_ref[...].astype(MXU)
        for j in range(NDEV):
            pp_ref[:, j * npp:(j + 1) * npp] = _mm(pb, wpp_ref[j])
        xo_ref[...] = xv + pp_ref[...] * gate

    return _call(
        body, name=name, grid=(T // tm,),
        out_shape=[jax.ShapeDtypeStruct((T, D), F32), jax.ShapeDtypeStruct((T, D), F32),
                   jax.ShapeDtypeStruct((T, D), F32), jax.ShapeDtypeStruct((T, D), MXU)],
        in_specs=[_tile(tm, D), _tile(tm, p.shape[1]), _whole(ln.shape), _whole(w_g.shape), _whole(w_pp.shape)],
        out_specs=[_tile(tm, D), _tile(tm, D), _tile(tm, D), _tile(tm, D)],
        compiler_params=_params(32),
    )(x, p, ln, w_g, w_pp)


def _head_ones():
    row = lax.broadcasted_iota(jnp.int32, (128, 128), 0)
    col = lax.broadcasted_iota(jnp.int32, (128, 128), 1)
    return (jnp.right_shift(row, 6) == jnp.right_shift(col, 6)).astype(MXU)


def _head_rms(x, g, ones):
    rstd = lax.rsqrt(_split_dot(x * x, ones, 3) * (1.0 / HEAD_DIM) + EPS)
    xhat = x * rstd
    return xhat * g, xhat, rstd


def _head_rms_bwd(dh, xhat, rstd, g, ones):
    dxh = dh * g
    mean = _split_dot(dxh * xhat, ones, 3) * (1.0 / HEAD_DIM)
    return rstd * (dxh - xhat * mean), jnp.sum(dh * xhat, axis=0, keepdims=True)


def _qkv_fwd(x, ln_q, ln_kv, g_q, g_k, w_q, w_kv, name):
    T, D = x.shape
    tm = min(TOKEN_TILE, T)
    nk = w_kv.shape[2]
    half = NDEV // 2

    def body(x_ref, lnq_ref, lnkv_ref, gq_ref, gk_ref, wq_ref, wkv_ref,
             q_ref, k_ref, v_ref, qpre_ref, kpre_ref, hq_ref, hkv_ref):
        xv = x_ref[...]
        _, xhat, _ = _rms(xv, lnq_ref[...])
        hq = (xhat * lnq_ref[...]).astype(MXU)
        hkv = (xhat * lnkv_ref[...]).astype(MXU)
        hq_ref[...] = hq
        hkv_ref[...] = hkv
        qpre_ref[...] = _mm(hq, wq_ref[...])
        for j in range(half):
            kpre_ref[:, j * nk:(j + 1) * nk] = _mm(hkv, wkv_ref[j])
            v_ref[:, j * nk:(j + 1) * nk] = _mm(hkv, wkv_ref[half + j]).astype(MXU)
        ones = _head_ones()
        for b in range(D // 128):
            cols = slice(b * 128, (b + 1) * 128)
            qn, _, _ = _head_rms(qpre_ref[:, cols], gq_ref[:, cols], ones)
            q_ref[:, cols] = (qn * SCALE).astype(MXU)
            kn, _, _ = _head_rms(kpre_ref[:, cols], gk_ref[:, cols], ones)
            k_ref[:, cols] = kn.astype(MXU)

    return _call(
        body, name=name, grid=(T // tm,),
        out_shape=[jax.ShapeDtypeStruct((T, D), MXU)] * 3 + [jax.ShapeDtypeStruct((T, D), F32)] * 2
        + [jax.ShapeDtypeStruct((T, D), MXU)] * 2,
        in_specs=[_tile(tm, D), _whole(ln_q.shape), _whole(ln_kv.shape), _whole(g_q.shape), _whole(g_k.shape),
                  _whole(w_q.shape), _whole(w_kv.shape)],
        out_specs=[_tile(tm, D)] * 7,
        compiler_params=_params(40),
    )(x, ln_q, ln_kv, g_q, g_k, w_q, w_kv)


SB_KEYS = 2 * QBLK


def _sb_consts():
    row = lax.broadcasted_iota(jnp.int32, (QBLK, QBLK), 0)
    col = lax.broadcasted_iota(jnp.int32, (QBLK, QBLK), 1)
    lane = lax.broadcasted_iota(jnp.int32, (QBLK, 128), 1)
    ones = jnp.ones((QBLK, QBLK), MXU)
    later = jnp.concatenate([(row > col).astype(MXU), ones], axis=1)
    later_eq = jnp.concatenate([(row >= col).astype(MXU), ones], axis=1)
    return later, later_eq, lane < HEAD_DIM


MASKED_LOG = -1e30


def _sb_window(i, w):
    upper = (i + 1) * QBLK - w * SB_KEYS
    start = pl.multiple_of(jnp.maximum(upper - SB_KEYS, 0), QBLK)
    key = lax.broadcasted_iota(jnp.int32, (2 * QBLK, SB_KEYS), 1) + start
    return start, key < upper


def _sb_diagonal():
    row = jnp.bitwise_and(lax.broadcasted_iota(jnp.int32, (2 * QBLK, SB_KEYS), 0), QBLK - 1)
    key = lax.broadcasted_iota(jnp.int32, (2 * QBLK, SB_KEYS), 1)
    cases = []
    for shift in (0, QBLK):
        seen = key < row + shift
        cases.append(jnp.stack([jnp.where(seen, 1.0, 0.0), jnp.where(seen, 0.0, MASKED_LOG)]))
    return jnp.stack(cases).astype(F32)


_SB_DIAG_SPEC = pl.BlockSpec((None, 2, 2 * QBLK, SB_KEYS), lambda h, i: (jnp.minimum(i, 1), 0, 0, 0))


def _sb_terms(x, terms):
    x = jnp.concatenate([x[:, :QBLK], x[:, QBLK:]], axis=0)
    out = []
    for _ in range(terms):
        part = x.astype(MXU)
        x = x - part.astype(F32)
        out.append(part)
    return tuple(out)


def _sb_suffix(parts, ones, carry):
    s = jnp.dot(jnp.concatenate(parts[:2], axis=1), jnp.concatenate([ones, ones], axis=0),
                preferred_element_type=F32)
    for part in parts[2:]:
        s = s + jnp.dot(part, ones, preferred_element_type=F32)
    rows = s.shape[0] // 2
    s_lo, sum_lo, s_hi, sum_hi = s[:rows, :QBLK], s[:rows, QBLK:], s[rows:, :QBLK], s[rows:, QBLK:]
    return jnp.concatenate([s_lo + (carry + sum_hi), s_hi + carry], axis=1), carry + (sum_lo + sum_hi)


def _sb_scores(z, mask):
    sp = _softplus(z)
    l, log_sig = -sp, z - sp
    if isinstance(mask, tuple):
        keep, bias = mask
        l, log_sig = l * keep, log_sig + bias
    else:
        l = jnp.where(mask, l, 0.0)
        log_sig = jnp.where(mask, log_sig, MASKED_LOG)
    return log_sig, _sb_terms(l, 2)


def _sb_weights(staged, later, c_l):
    log_sig, parts = staged
    b, c_l = _sb_suffix(parts, later, c_l)
    return jnp.exp(log_sig + b), c_l


DEAD_LOG = -88.0


def _sb_alive(carry):
    return (jnp.max(carry[0]) > DEAD_LOG).astype(jnp.int32)


def _ride_along(refs, n, scatter, rank=2):
    if n == 0:
        return lambda: None
    step, steps = 0, 1
    for d in range(rank):
        step = step * pl.num_programs(d) + pl.program_id(d)
        steps = steps * pl.num_programs(d)
    srcs, lands, sems = refs[:n], refs[n:2 * n], refs[2 * n:]

    @pl.when(step == 0)
    def _():
        _exchange_start(srcs, lands, sems, scatter)

    def finish():
        @pl.when(step == steps - 1)
        def _():
            _exchange_finish(srcs, lands, sems, scatter)

    return finish


def _gather_ride_shapes(arrs):
    n = len(arrs)
    if n == 0:
        return [], []
    return ([jax.ShapeDtypeStruct((NDEV,) + a.shape, a.dtype) for a in arrs],
            [pltpu.SemaphoreType.DMA((n, NDEV - 1)), pltpu.SemaphoreType.DMA((n, NDEV - 1)),
             pltpu.SemaphoreType.DMA((n,))])


def _ride_along_gather(refs, n, rank):
    if n == 0:
        return lambda: None
    step, steps = 0, 1
    for d in range(rank):
        step = step * pl.num_programs(d) + pl.program_id(d)
        steps = steps * pl.num_programs(d)
    srcs, outs = refs[:n], refs[n:2 * n]
    send_sems, recv_sems, local_sems = refs[2 * n:]
    x, y, c = lax.axis_index("x"), lax.axis_index("y"), lax.axis_index("c")
    me, sibling = (x, y, c), (x, y, 1 - c)
    chips = [(1 - x, y), (x, 1 - y), (1 - x, 1 - y)]

    def index(dev):
        return 4 * dev[0] + 2 * dev[1] + dev[2]

    def copy(a, k, block, to, src=None):
        dst = outs[a].at[index(block)]
        return pltpu.make_async_remote_copy(
            src_ref=dst if src is None else src, dst_ref=dst, send_sem=send_sems.at[a, k],
            recv_sem=recv_sems.at[a, k], device_id=to, device_id_type=MESH)

    def mine(a):
        return pltpu.make_async_copy(srcs[a], outs[a].at[index(me)], local_sems.at[a])

    def first(a):
        return [copy(a, 0, me, sibling, src=srcs[a])] + [copy(a, 1 + j, me, (*chip, c), src=srcs[a])
                                                         for j, chip in enumerate(chips)]

    @pl.when(step == 0)
    def _():
        for a in range(n):
            mine(a).start()
            for cp in first(a):
                cp.start()

    @pl.when(step == steps // 2)
    def _():
        for a in range(n):
            for j, chip in enumerate(chips):
                copy(a, 1 + j, (*chip, c), me).wait_recv()
                copy(a, 4 + j, (*chip, c), sibling).start()

    def finish():
        @pl.when(step == steps - 1)
        def _():
            for a in range(n):
                copy(a, 0, sibling, me).wait_recv()
                for j, chip in enumerate(chips):
                    copy(a, 4 + j, (*chip, 1 - c), me).wait_recv()
                for cp in first(a):
                    cp.wait_send()
                for j, chip in enumerate(chips):
                    copy(a, 4 + j, (*chip, c), sibling).wait_send()
                mine(a).wait()

    return finish


def _sb_fwd(q, k, v, cargo, name):
    T, D = q.shape
    nc = len(cargo)
    lands, sems = _gather_ride_shapes(cargo)

    def body(diag_ref, q_ref, k_ref, v_ref, *rest):
        o_ref = rest[nc]
        finish = _ride_along_gather(rest[:nc] + rest[nc + 1:], nc, 2)
        i = pl.program_id(1)
        n_steps = (i + 2) // 2
        later, _, first = _sb_consts()
        qv = q_ref[...]
        zero = jnp.zeros_like(qv)
        q2 = jnp.concatenate([jnp.where(first, qv, zero), jnp.where(first, zero, qv)], axis=0)

        def window(w, carry, diagonal):
            start, mask = _sb_window(i, w)
            if diagonal:
                mask = (diag_ref[0], diag_ref[1])
            kw = k_ref[pl.ds(start, SB_KEYS), :]
            vw = v_ref[pl.ds(start, SB_KEYS), :]
            c_l, acc = carry
            a, c_l = _sb_weights(_sb_scores(_mm_nt(q2, kw), mask), later, c_l)
            return c_l, acc + _mm(a, vw)

        def step(state):
            w, _, carry = state
            carry = window(w, carry, False)
            return w + 1, _sb_alive(carry), carry

        carry = window(0, (jnp.zeros((2 * QBLK, 128), F32),) * 2, True)
        _, _, carry = lax.while_loop(lambda s: (s[0] < n_steps) & (s[1] > 0), step,
                                     (jnp.int32(1), _sb_alive(carry), carry))
        o_ref[...] = jnp.where(first, carry[1][:QBLK], carry[1][QBLK:])
        finish()

    qblk = pl.BlockSpec((QBLK, 128), lambda h, i: (i, h))
    kblk = pl.BlockSpec((T, 128), lambda h, i: (0, h))
    outs = _call(
        body, name=name, grid=(D // 128, T // QBLK), out_shape=[jax.ShapeDtypeStruct((T, D), F32)] + lands,
        in_specs=[_SB_DIAG_SPEC, qblk, kblk, kblk] + [ANY_SPEC] * nc, out_specs=[qblk] + [ANY_SPEC] * nc,
        scratch_shapes=sems, compiler_params=_params(32, 2),
    )(_sb_diagonal(), q, k, v, *cargo)
    return outs[0], outs[1:]


def _sb_bwd(q, k, v, o, do, cargo, name):
    T, D = q.shape
    nc = len(cargo)
    lands, sems = _exchange_shapes(cargo, True)

    def body(diag_ref, q_ref, k_ref, v_ref, o_ref, do_ref, *rest):
        dq_ref, dk_ref, dv_ref = rest[nc:nc + 3]
        finish = _ride_along(rest[:nc] + rest[nc + 3:], nc, True)
        i = pl.program_id(1)

        @pl.when(i == 0)
        def _():
            dk_ref[...] = jnp.zeros_like(dk_ref)
            dv_ref[...] = jnp.zeros_like(dv_ref)

        n_steps = (i + 2) // 2
        later, later_eq, first = _sb_consts()
        qv = q_ref[...]
        dob = do_ref[...].astype(MXU)
        zero = jnp.zeros_like(qv)
        q2 = jnp.concatenate([jnp.where(first, qv, zero), jnp.where(first, zero, qv)], axis=0)
        do2 = jnp.concatenate([jnp.where(first, dob, zero), jnp.where(first, zero, dob)], axis=0)
        prod = o_ref[...] * dob.astype(F32)
        prod2 = jnp.concatenate([jnp.where(first, prod, 0.0), jnp.where(first, 0.0, prod)], axis=0)
        total = _split_dot(prod2, jnp.ones((128, 128), MXU), 3)
        total = jnp.concatenate([total, total], axis=1)

        def window(w, carry, diagonal):
            start, mask = _sb_window(i, w)
            if diagonal:
                mask = (diag_ref[0], diag_ref[1])
            kw = k_ref[pl.ds(start, SB_KEYS), :]
            vw = v_ref[pl.ds(start, SB_KEYS), :]
            c_l, c_e, dq = carry
            log_sig, parts = _sb_scores(_mm_nt(q2, kw), mask)
            a, c_l = _sb_weights((log_sig, parts), later, c_l)
            ab = a.astype(MXU)
            e = ab.astype(F32) * _mm_nt(do2, vw)
            from_here, c_e = _sb_suffix(_sb_terms(e, 2), later_eq, c_e)
            sig = jnp.exp(log_sig)
            dzb = (e * (1.0 - sig) - sig * (total - from_here)).astype(MXU)
            dk_ref[pl.ds(start, SB_KEYS), :] += _mm_tn(dzb, q2)
            dv_ref[pl.ds(start, SB_KEYS), :] += _mm_tn(ab, do2)
            return c_l, c_e, dq + _mm(dzb, kw)

        def step(state):
            w, _, carry = state
            carry = window(w, carry, False)
            return w + 1, _sb_alive(carry), carry

        carry = window(0, (jnp.zeros((2 * QBLK, 128), F32),) * 3, True)
        _, _, carry = lax.while_loop(lambda s: (s[0] < n_steps) & (s[1] > 0), step,
                                     (jnp.int32(1), _sb_alive(carry), carry))
        dq_ref[...] = jnp.where(first, carry[2][:QBLK], carry[2][QBLK:]) * SCALE
        finish()

    qblk = pl.BlockSpec((QBLK, 128), lambda h, i: (i, h))
    kblk = pl.BlockSpec((T, 128), lambda h, i: (0, h))
    full = jax.ShapeDtypeStruct((T, D), F32)
    outs = _call(
        body, name=name, grid=(D // 128, T // QBLK), out_shape=[full, full, full] + lands,
        in_specs=[_SB_DIAG_SPEC, qblk, kblk, kblk, qblk, qblk] + [ANY_SPEC] * nc,
        out_specs=[qblk, kblk, kblk] + [ANY_SPEC] * nc, scratch_shapes=sems, compiler_params=_params(32, 2),
    )(_sb_diagonal(), q, k, v, o, do, *cargo)
    return outs[0], outs[1], outs[2], outs[3:]


def _proj_res(x, a, w, name):
    T, D = x.shape
    tm = min(TOKEN_TILE, T)

    def body(x_ref, a_ref, w_ref, o_ref):
        o_ref[...] = x_ref[...] + _mm(a_ref[...], w_ref[...])

    return _call(
        body, name=name, grid=(T // tm,), out_shape=jax.ShapeDtypeStruct((T, D), F32),
        in_specs=[_tile(tm, D), _tile(tm, a.shape[1]), _whole(w.shape)], out_specs=_tile(tm, D),
        compiler_params=_params(32),
    )(x, a, w)


def _proj_nt(g, w, name):
    T = g.shape[0]
    K = w.shape[0]
    tm = min(TOKEN_TILE, T)

    def body(g_ref, w_ref, o_ref):
        o_ref[...] = _mm_nt(g_ref[...], w_ref[...])

    return _call(
        body, name=name, grid=(T // tm,), out_shape=jax.ShapeDtypeStruct((T, K), F32),
        in_specs=[_tile(tm, g.shape[1]), _whole(w.shape)], out_specs=_tile(tm, K),
        compiler_params=_params(32),
    )(g, w)


def _loss_grad(y, tgt, name):
    T, D = y.shape
    tm = min(TOKEN_TILE, T)

    def body(y_ref, t_ref, dy_ref, loss_ref):
        @pl.when(pl.program_id(0) == 0)
        def _():
            loss_ref[...] = jnp.zeros_like(loss_ref)
        diff = y_ref[...] - t_ref[...]
        dy_ref[...] = diff * (1.0 / D)
        rows = jnp.sum(diff * diff, axis=1, keepdims=True) * (1.0 / D)
        loss_ref[...] += 0.5 * jnp.sum(rows, axis=0, keepdims=True)

    return _call(
        body, name=name, grid=(T // tm,),
        out_shape=[jax.ShapeDtypeStruct((T, D), F32), jax.ShapeDtypeStruct((1, 1), F32)],
        in_specs=[_tile(tm, D), _tile(tm, D)], out_specs=[_tile(tm, D), _acc((1, 1))],
        compiler_params=_params(32),
    )(y, tgt)


def _ple_bwd(dx, x, gate, pp, ln, w_g, name):
    T, D = x.shape
    tm = min(TOKEN_TILE, T)

    def body(dx_ref, x_ref, gate_ref, pp_ref, ln_ref, wg_ref, dxo_ref, dpp_ref, dgp_ref, dln_ref):
        @pl.when(pl.program_id(0) == 0)
        def _():
            dln_ref[...] = jnp.zeros_like(dln_ref)
        dxv = dx_ref[...]
        gate = gate_ref[...]
        _, xhat, rstd = _rms(x_ref[...], ln_ref[...])
        dpp_ref[...] = (dxv * gate).astype(MXU)
        dgp = (dxv * pp_ref[...] * gate * (1.0 - gate)).astype(MXU)
        dgp_ref[...] = dgp
        dxn, dln = _rms_bwd(_mm_nt(dgp, wg_ref[...]), xhat, rstd, ln_ref[...])
        dln_ref[...] += dln
        dxo_ref[...] = dxn + dxv

    return _call(
        body, name=name, grid=(T // tm,),
        out_shape=[jax.ShapeDtypeStruct((T, D), F32), jax.ShapeDtypeStruct((T, D), MXU),
                   jax.ShapeDtypeStruct((T, D), MXU), jax.ShapeDtypeStruct(ln.shape, F32)],
        in_specs=[_tile(tm, D)] * 4 + [_whole(ln.shape), _whole(w_g.shape)],
        out_specs=[_tile(tm, D), _tile(tm, D), _tile(tm, D), _acc(ln.shape)],
        compiler_params=_params(32),
    )(dx, x, gate, pp, ln, w_g)


def _mlp_bwd(dx, x, pre, ln, w_up, w_down, cargo, name):
    T, D = x.shape
    tm = min(TOKEN_TILE, T)
    nf = w_up.shape[2]
    F = nf * NDEV
    nc = len(cargo)
    lands, sems = _exchange_shapes(cargo, True)

    def body(dx_ref, x_ref, pre_ref, ln_ref, wup_ref, wdown_ref, *rest):
        dxo_ref, dpre_ref, s_ref, dln_ref = rest[nc:nc + 4]
        finish = _ride_along(rest[:nc] + rest[nc + 4:], nc, True, rank=1)

        @pl.when(pl.program_id(0) == 0)
        def _():
            dln_ref[...] = jnp.zeros_like(dln_ref)
        dxv = dx_ref[...]
        _, xhat, rstd = _rms(x_ref[...], ln_ref[...])
        a = jnp.maximum(pre_ref[...], 0.0)
        s_ref[...] = (a * a).astype(MXU)
        dpre_ref[...] = (_mm_nt(dxv, wdown_ref[...]) * (2.0 * a)).astype(MXU)
        dh = _mm_nt(dpre_ref[:, :nf], wup_ref[0])
        for j in range(1, NDEV):
            dh += _mm_nt(dpre_ref[:, j * nf:(j + 1) * nf], wup_ref[j])
        dxn, dln = _rms_bwd(dh, xhat, rstd, ln_ref[...])
        dln_ref[...] += dln
        dxo_ref[...] = dxn + dxv
        finish()

    outs = _call(
        body, name=name, grid=(T // tm,),
        out_shape=[jax.ShapeDtypeStruct((T, D), F32), jax.ShapeDtypeStruct((T, F), MXU),
                   jax.ShapeDtypeStruct((T, F), MXU), jax.ShapeDtypeStruct(ln.shape, F32)] + lands,
        in_specs=[_tile(tm, D), _tile(tm, D), _tile(tm, F), _whole(ln.shape), _whole(w_up.shape),
                  _whole(w_down.shape)] + [ANY_SPEC] * nc,
        out_specs=[_tile(tm, D), _tile(tm, F), _tile(tm, F), _acc(ln.shape)] + [ANY_SPEC] * nc,
        scratch_shapes=sems, compiler_params=_params(56),
    )(dx, x, pre, ln, w_up, w_down, *cargo)
    return outs[0], outs[1], outs[2], outs[3], outs[4:]


def _qkv_bwd(dx, x, dq, dk, dv, q_pre, k_pre, ln_q, ln_kv, g_q, g_k, w_q, w_kv, cargo, name):
    T, D = x.shape
    tm = min(TOKEN_TILE, T)
    nk = w_kv.shape[2]
    n_tiles = T // tm
    nc = len(cargo)
    lands, sems = _exchange_shapes(cargo, True)

    def body(dx_ref, x_ref, dq_ref, dk_ref, dv_ref, qpre_ref, kpre_ref, lnq_ref, lnkv_ref, gq_ref, gk_ref,
             wq_ref, wkv_ref, *rest):
        dxo_ref, dqp_ref, dkv_ref, dlnq_ref, dlnkv_ref, dgq_ref, dgk_ref = rest[nc:nc + 7]
        gq_acc, gk_acc = rest[2 * nc + 7:2 * nc + 9]
        finish = _ride_along(rest[:nc] + rest[nc + 7:2 * nc + 7] + rest[2 * nc + 9:], nc, True, rank=1)
        i = pl.program_id(0)

        @pl.when(i == 0)
        def _():
            dlnq_ref[...] = jnp.zeros_like(dlnq_ref)
            dlnkv_ref[...] = jnp.zeros_like(dlnkv_ref)
            gq_acc[...] = jnp.zeros_like(gq_acc)
            gk_acc[...] = jnp.zeros_like(gk_acc)

        ones = _head_ones()
        for b in range(D // 128):
            cols = slice(b * 128, (b + 1) * 128)
            _, xh, rs = _head_rms(qpre_ref[:, cols], gq_ref[:, cols], ones)
            d, dg = _head_rms_bwd(dq_ref[:, cols], xh, rs, gq_ref[:, cols], ones)
            dqp_ref[:, cols] = d.astype(MXU)
            gq_acc[:, cols] += dg
            _, xh, rs = _head_rms(kpre_ref[:, cols], gk_ref[:, cols], ones)
            d, dg = _head_rms_bwd(dk_ref[:, cols], xh, rs, gk_ref[:, cols], ones)
            dkv_ref[:, cols] = d.astype(MXU)
            gk_acc[:, cols] += dg
        dkv_ref[:, D:] = dv_ref[...].astype(MXU)

        _, xhat, rstd = _rms(x_ref[...], lnq_ref[...])
        dhq = _mm_nt(dqp_ref[...], wq_ref[...])
        dhkv = _mm_nt(dkv_ref[:, :nk], wkv_ref[0])
        for j in range(1, NDEV):
            dhkv += _mm_nt(dkv_ref[:, j * nk:(j + 1) * nk], wkv_ref[j])
        dxq, dlnq = _rms_bwd(dhq, xhat, rstd, lnq_ref[...])
        dxkv, dlnkv = _rms_bwd(dhkv, xhat, rstd, lnkv_ref[...])
        dlnq_ref[...] += dlnq
        dlnkv_ref[...] += dlnkv
        dxo_ref[...] = dx_ref[...] + dxq + dxkv

        @pl.when(i == n_tiles - 1)
        def _():
            row = lax.broadcasted_iota(jnp.int32, (D, 128), 0)
            col = lax.broadcasted_iota(jnp.int32, (D, 128), 1)
            fold = (jnp.bitwise_and(row, HEAD_DIM - 1) == col).astype(MXU)
            dgq_ref[...] = _split_dot(jnp.broadcast_to(gq_acc[...], (8, D)), fold, 3)
            dgk_ref[...] = _split_dot(jnp.broadcast_to(gk_acc[...], (8, D)), fold, 3)

        finish()

    small = jax.ShapeDtypeStruct((8, 128), F32)
    outs = _call(
        body, name=name, grid=(n_tiles,),
        out_shape=[jax.ShapeDtypeStruct((T, D), F32), jax.ShapeDtypeStruct((T, D), MXU),
                   jax.ShapeDtypeStruct((T, 2 * D), MXU), jax.ShapeDtypeStruct(ln_q.shape, F32),
                   jax.ShapeDtypeStruct(ln_kv.shape, F32), small, small] + lands,
        in_specs=[_tile(tm, D)] * 7 + [_whole(ln_q.shape), _whole(ln_kv.shape), _whole(g_q.shape),
                                       _whole(g_k.shape), _whole(w_q.shape), _whole(w_kv.shape)] + [ANY_SPEC] * nc,
        out_specs=[_tile(tm, D), _tile(tm, D), _tile(tm, 2 * D), _acc(ln_q.shape), _acc(ln_kv.shape),
                   _acc((8, 128)), _acc((8, 128))] + [ANY_SPEC] * nc,
        scratch_shapes=[pltpu.VMEM((1, D), F32), pltpu.VMEM((1, D), F32)] + sems,
        compiler_params=_params(48),
    )(dx, x, dq, dk, dv, q_pre, k_pre, ln_q, ln_kv, g_q, g_k, w_q, w_kv, *cargo)
    return outs[:7], outs[7:]


def _sgu_bwd(dx, x, z, ln, w_in, g_v, ws, wsT, bT, w_out, cargo, name):
    T, D = x.shape
    tm = min(TOKEN_TILE, T)
    nw = w_in.shape[2]
    nc = len(cargo)
    lands, sems = _exchange_shapes(cargo, True)

    def body(dx_ref, x_ref, z_ref, ln_ref, win_ref, gv_ref, ws_ref, wsT_ref, bT_ref, wout_ref, *rest):
        dxo_ref, dz_ref, dws_ref, dbT_ref, dln_ref, dgv_ref = rest[nc:nc + 6]
        mix_ref, dvn_ref = rest[2 * nc + 6:2 * nc + 8]
        finish = _ride_along(rest[:nc] + rest[nc + 6:2 * nc + 6] + rest[2 * nc + 8:], nc, True, rank=1)

        @pl.when(pl.program_id(0) == 0)
        def _():
            dws_ref[...] = jnp.zeros_like(dws_ref)
            dbT_ref[...] = jnp.zeros_like(dbT_ref)
            dln_ref[...] = jnp.zeros_like(dln_ref)
            dgv_ref[...] = jnp.zeros_like(dgv_ref)
        dxv = dx_ref[...]
        _, xhat, rstd = _rms(x_ref[...], ln_ref[...])
        u, du = _gelu_and_grad(z_ref[:, :D])
        gv, dgv = _gelu_and_grad(z_ref[:, D:])
        vn, vhat, rstd_v = _rms(gv, gv_ref[...])
        vnb = vn.astype(MXU)
        _spatial_mix(vnb, ws_ref, bT_ref, mix_ref, tm)
        dy = _mm_nt(dxv, wout_ref[...])
        d_u = dy * mix_ref[...]
        d_mix = dy * u
        dmb = d_mix.astype(MXU)
        tri, triT = _tril_mask()
        for g in range(GROUPS):
            wmT = jnp.where(triT, wsT_ref[g], 0.0).astype(MXU)
            cols = slice(g * CHUNK, (g + 1) * CHUNK)
            for ch in range(tm // CHUNK):
                rows = slice(ch * CHUNK, (ch + 1) * CHUNK)
                dm = dmb[rows, cols]
                dws_ref[g] += jnp.where(tri, _mm_nt(dm, vnb[rows, cols]), 0.0)
                dbT_ref[:, g:g + 1] += jnp.sum(d_mix[rows, cols], axis=1, keepdims=True)
                dvn_ref[rows, cols] = _mm(wmT, dm)
        d_gv, dg = _rms_bwd(dvn_ref[...], vhat, rstd_v, gv_ref[...])
        dgv_ref[...] += dg
        dz_ref[:, :D] = (d_u * du).astype(MXU)
        dz_ref[:, D:] = (d_gv * dgv).astype(MXU)
        dh = _mm_nt(dz_ref[:, :nw], win_ref[0])
        for j in range(1, NDEV):
            dh += _mm_nt(dz_ref[:, j * nw:(j + 1) * nw], win_ref[j])
        dxn, dln = _rms_bwd(dh, xhat, rstd, ln_ref[...])
        dln_ref[...] += dln
        dxo_ref[...] = dxn + dxv
        finish()

    outs = _call(
        body, name=name, grid=(T // tm,),
        out_shape=[jax.ShapeDtypeStruct((T, D), F32), jax.ShapeDtypeStruct((T, 2 * D), MXU),
                   jax.ShapeDtypeStruct(ws.shape, F32), jax.ShapeDtypeStruct(bT.shape, F32),
                   jax.ShapeDtypeStruct(ln.shape, F32), jax.ShapeDtypeStruct(g_v.shape, F32)] + lands,
        in_specs=[_tile(tm, D), _tile(tm, D), _tile(tm, 2 * D), _whole(ln.shape), _whole(w_in.shape),
                  _whole(g_v.shape), _whole(ws.shape), _whole(wsT.shape), _whole(bT.shape), _whole(w_out.shape)]
        + [ANY_SPEC] * nc,
        out_specs=[_tile(tm, D), _tile(tm, 2 * D), _acc(ws.shape), _acc(bT.shape), _acc(ln.shape),
                   _acc(g_v.shape)] + [ANY_SPEC] * nc,
        scratch_shapes=[pltpu.VMEM((tm, D), F32), pltpu.VMEM((tm, D), F32)] + sems,
        compiler_params=_params(48),
    )(dx, x, z, ln, w_in, g_v, ws, wsT, bT, w_out, *cargo)
    return outs[:6], outs[6:]


def _wgrad_rows(a, g, name):
    T, K = a.shape
    N = g.shape[1]
    kb = K // NDEV

    def body(a_ref, g_ref, o_ref):
        o_ref[...] = _mm_tn(a_ref[...], g_ref[...]).astype(COMM)

    return _call(
        body, name=name, grid=(NDEV,), out_shape=jax.ShapeDtypeStruct((K, N), COMM),
        in_specs=[pl.BlockSpec((T, kb), lambda j: (0, j)), _whole(g.shape)],
        out_specs=pl.BlockSpec((kb, N), lambda j: (j, 0)),
        compiler_params=_params(40),
    )(a, g).reshape(NDEV, kb, N)


def _wgrad_cols(a, g, name):
    T, K = a.shape
    N = g.shape[1]
    nb = N // NDEV

    def body(a_ref, g_ref, o_ref):
        o_ref[...] = _mm_tn(a_ref[...], g_ref[...]).astype(COMM)

    return _call(
        body, name=name, grid=(NDEV,), out_shape=jax.ShapeDtypeStruct((NDEV, K, nb), COMM),
        in_specs=[_whole(a.shape), pl.BlockSpec((T, nb), lambda j: (0, j))],
        out_specs=pl.BlockSpec((None, K, nb), lambda j: (j, 0, 0)),
        compiler_params=_params(40),
    )(a, g)


def _adamw_rows(R, C):
    tr = math.gcd(R, max(8, (128 * 1024) // C))
    return R if tr < 64 else tr


def _adamw_update(w_ref, m_ref, v_ref, s_ref, g_ref, d_ref, mo_ref, vo_ref):
    g = s_ref[0].astype(F32)
    for j in range(1, s_ref.shape[0]):
        g = g + s_ref[j].astype(F32)
    mn = ADAM_B1 * m_ref[...] + (1.0 - ADAM_B1) * g
    vn = ADAM_B2 * v_ref[...] + (1.0 - ADAM_B2) * (g * g)
    g_ref[...] = g
    mo_ref[...] = mn
    vo_ref[...] = vn
    m_hat = mn / (1.0 - ADAM_B1 ** ADAM_STEP)
    v_hat = vn / (1.0 - ADAM_B2 ** ADAM_STEP)
    d_ref[...] = -ADAM_LR * (m_hat / (jnp.sqrt(v_hat) + ADAM_EPS) + ADAM_WD * w_ref[...])


def _adamw_layers(w, m, v, slots0, slots1, name):
    _, R, C = w.shape
    tr = _adamw_rows(R, C)
    last = R // tr - 1

    def body(w_ref, m_ref, v_ref, s0_ref, s1_ref, *outs):
        @pl.when(pl.program_id(0) == 0)
        def _():
            _adamw_update(w_ref, m_ref, v_ref, s0_ref, *outs)

        @pl.when(pl.program_id(0) == 1)
        def _():
            _adamw_update(w_ref, m_ref, v_ref, s1_ref, *outs)

    blk = pl.BlockSpec((None, tr, C), lambda l, i: (l, i, 0))
    s0_blk = pl.BlockSpec((slots0.shape[0], tr, C), lambda l, i: (0, jnp.where(l == 0, i, last), 0))
    s1_blk = pl.BlockSpec((slots1.shape[0], tr, C), lambda l, i: (0, jnp.where(l == 1, i, 0), 0))
    out = jax.ShapeDtypeStruct(w.shape, F32)
    return _call(
        body, name=name, grid=(2, R // tr), out_shape=[out, out, out, out],
        in_specs=[blk, blk, blk, s0_blk, s1_blk], out_specs=[blk] * 4, compiler_params=_params(32, 2),
    )(w, m, v, slots0, slots1)


def _adamw(w, m, v, slots, name):
    R, C = w.shape
    n = slots.shape[0]
    tr = _adamw_rows(R, C)

    def body(*refs):
        _adamw_update(*refs)

    blk = pl.BlockSpec((tr, C), lambda i: (i, 0))
    out = jax.ShapeDtypeStruct((R, C), F32)
    return _call(
        body, name=name, grid=(R // tr,), out_shape=[out, out, out, out],
        in_specs=[blk, blk, blk, pl.BlockSpec((n, tr, C), lambda i: (0, i, 0))], out_specs=[blk] * 4,
        compiler_params=_params(32),
    )(w, m, v, slots)


def _rows128(a):
    flat = a.reshape(-1)
    rows = -(-flat.shape[0] // 1024) * 8
    flat = jnp.pad(flat, (0, rows * 128 - flat.shape[0]))
    return flat.reshape(rows, 128)


def kernel(x, p, ln_mix_a, w_in_a, g_v_a, w_spatial, b_spatial, w_out_a, ln_kv, w_kv, g_k, ln_mix_b, w_q, g_q, w_out_b, ln_mlp, w_up, w_down, ln_ple, w_ple_gate, w_ple_proj, loss_target, m_ln_mix_a, m_w_in_a, m_g_v_a, m_w_spatial, m_b_spatial, m_w_out_a, m_ln_kv, m_w_kv, m_g_k, m_ln_mix_b, m_w_q, m_g_q, m_w_out_b, m_ln_mlp, m_w_up, m_w_down, m_ln_ple, m_w_ple_gate, m_w_ple_proj, v_ln_mix_a, v_w_in_a, v_g_v_a, v_w_spatial, v_b_spatial, v_w_out_a, v_ln_kv, v_w_kv, v_g_k, v_ln_mix_b, v_w_q, v_g_q, v_w_out_b, v_ln_mlp, v_w_up, v_w_down, v_ln_ple, v_w_ple_gate, v_w_ple_proj):
    me = 4 * lax.axis_index("x") + 2 * lax.axis_index("y") + lax.axis_index("c")
    D = x.shape[2]
    x0, tgt = x[0], loss_target[0]
    n_layers = w_up.shape[0]

    c = lambda w: w.astype(COMM)
    first = [c(w_in_a[0]), c(w_out_a[0]), ln_mix_a, g_v_a, c(w_up[0]), c(w_down[0]), c(w_ple_gate[0]),
             c(w_ple_proj[0]), c(w_q[0]), c(w_kv)]
    second_small = [c(w_out_b[0]), c(w_ple_gate[1]), c(w_ple_proj[1])]
    second_big = [c(w_up[1]), c(w_down[1])]
    W_in, W_out_a, ln_a, gv_a, W_up0, W_down0, W_g0, W_pp0, W_q, W_kv = _gather_two_level(first, "gather_first")
    W_out_a, ln_a, gv_a = W_out_a.reshape(-1, D), ln_a.reshape(1, D), gv_a.reshape(1, D)
    W_down0, W_g0, W_q = W_down0.reshape(-1, D), W_g0.reshape(-1, D), W_q.reshape(-1, D)
    ws = w_spatial[0]
    wsT = jnp.swapaxes(ws, 1, 2)
    bT = b_spatial[0].T
    ln_kv2, ln_b = ln_kv.reshape(1, D), ln_mix_b
    gk2 = jnp.tile(g_k.reshape(1, HEAD_DIM), (1, D // HEAD_DIM))
    gq2 = jnp.tile(g_q, (1, D // HEAD_DIM))
    ln_m = [ln_mlp[l:l + 1] for l in range(n_layers)]
    ln_p = [ln_ple[l:l + 1] for l in range(n_layers)]

    x1, z, h_a, y_a = _sgu_fwd(x0, ln_a, W_in, gv_a, ws, bT, W_out_a, "sgu_fwd")
    x2, pre0, hm0, (W_out_b, W_g1, W_pp1) = _mlp_fwd(x1, ln_m[0], W_up0, W_down0, second_small, "mlp_fwd0")
    x3, gate0, pp0, hp0 = _ple_fwd(x2, p[0, 0], ln_p[0], W_g0, W_pp0, "ple_fwd0")
    qn, kn, vn, q_pre, k_pre, h_q, h_kv = _qkv_fwd(x3, ln_b, ln_kv2, gq2, gk2, W_q, W_kv, "qkv_fwd")
    o2d, (W_up1, W_down1) = _sb_fwd(qn, kn, vn, second_big, "sb_fwd")
    W_out_b, W_down1, W_g1 = W_out_b.reshape(-1, D), W_down1.reshape(-1, D), W_g1.reshape(-1, D)
    x4 = _proj_res(x3, o2d, W_out_b, "attn_out")
    x5, pre1, hm1, _ = _mlp_fwd(x4, ln_m[1], W_up1, W_down1, [], "mlp_fwd1")
    x6, gate1, pp1, hp1 = _ple_fwd(x5, p[1, 0], ln_p[1], W_g1, W_pp1, "ple_fwd1")
    dy, loss_part = _loss_grad(x6, tgt, "loss_grad")

    dx5, dpp1, dgp1, dlnp1 = _ple_bwd(dy, x5, gate1, pp1, ln_p[1], W_g1, "ple_bwd1")
    dx4, dpre1, s1, dlnm1, _ = _mlp_bwd(dx5, x4, pre1, ln_m[1], W_up1, W_down1, [], "mlp_bwd1")
    wg_big = [_wgrad_cols(hm1, dpre1, "wg_up1"), _wgrad_rows(s1, dx5, "wg_down1")]
    wg_small = [_wgrad_rows(hp1, dgp1, "wg_gate1"), _wgrad_cols(p[1, 0].astype(MXU), dpp1, "wg_proj1"),
                _wgrad_rows(o2d, dx4, "wg_out_b")]
    do2d = _proj_nt(dx4, W_out_b, "attn_out_bwd")
    dqn, dkn, dvn, (s_up1, s_down1) = _sb_bwd(qn, kn, vn, o2d, do2d, wg_big, "sb_bwd")
    (dx3, dq_pre, dkv, dlnb, dlnkv, dgq, dgk), (s_gate1, s_proj1, s_out_b) = _qkv_bwd(
        dx4, x3, dqn, dkn, dvn, q_pre, k_pre, ln_b, ln_kv2, gq2, gk2, W_q, W_kv, wg_small, "qkv_bwd")
    dgq, dgk = dgq[:1, :HEAD_DIM], dgk[:1, :HEAD_DIM]
    dx2, dpp0, dgp0, dlnp0 = _ple_bwd(dx3, x2, gate0, pp0, ln_p[0], W_g0, "ple_bwd0")
    wg_kv = [_wgrad_cols(h_kv, dkv, "wg_kv")]
    dx1, dpre0, s0, dlnm0, (s_kv,) = _mlp_bwd(dx2, x1, pre0, ln_m[0], W_up0, W_down0, wg_kv, "mlp_bwd0")
    wg_ple = [_wgrad_rows(hp0, dgp0, "wg_gate0"), _wgrad_cols(p[0, 0].astype(MXU), dpp0, "wg_proj0"),
              _wgrad_rows(h_q, dq_pre, "wg_q")]
    (dx0, dz, dws, dbT, dlna, dgva), (s_gate0, s_proj0, s_q) = _sgu_bwd(
        dx1, x0, z, ln_a, W_in, gv_a, ws, wsT, bT, W_out_a, wg_ple, "sgu_bwd")
    wg_first = [_wgrad_cols(hm0, dpre0, "wg_up0"), _wgrad_rows(s0, dx2, "wg_down0"), _wgrad_cols(h_a, dz, "wg_in_a"),
                _wgrad_rows(y_a, dx1, "wg_out_a")]

    small = [("b_spatial", dbT.T[None], b_spatial, m_b_spatial, v_b_spatial),
             ("ln_kv", dlnkv.reshape(-1), ln_kv, m_ln_kv, v_ln_kv),
             ("g_k", dgk.reshape(-1), g_k, m_g_k, v_g_k),
             ("ln_mix_b", dlnb, ln_mix_b, m_ln_mix_b, v_ln_mix_b),
             ("g_q", dgq, g_q, m_g_q, v_g_q),
             ("ln_mlp", jnp.concatenate([dlnm0, dlnm1]), ln_mlp, m_ln_mlp, v_ln_mlp),
             ("ln_ple", jnp.concatenate([dlnp0, dlnp1]), ln_ple, m_ln_ple, v_ln_ple)]
    sharded_vec = [("ln_mix_a", dlna, ln_mix_a, m_ln_mix_a, v_ln_mix_a),
                   ("g_v_a", dgva, g_v_a, m_g_v_a, v_g_v_a)]
    packs = [[], [], [], []]
    for _, g, w, m, v in small:
        for lst, a in zip(packs, (g, w, m, v)):
            lst.append(_rows128(a))
    for _, g, w, m, v in sharded_vec:
        packs[0].append(g.reshape(NDEV, -1))
        for lst, a in zip(packs[1:], (w, m, v)):
            lst.append(jnp.broadcast_to(a, (NDEV, a.shape[1])))
    packs[0].append(_rows128(loss_part))
    for lst in packs[1:]:
        lst.append(jnp.zeros((8, 128), F32))
    g_pack, w_pack, m_pack, v_pack = (jnp.concatenate(lst) for lst in packs)
    g_pack8 = jnp.broadcast_to(g_pack[None], (NDEV,) + g_pack.shape)
    dws8 = jnp.broadcast_to(dws.reshape(1, -1, 128).astype(COMM), (NDEV, dws.size // 128, 128))
    by_chip = [a.reshape((4, 2) + a.shape[1:]) for a in wg_first]
    from_sibling, (g_all, s_ws) = _scatter_pair(by_chip, [g_pack8, dws8], "scatter_pair")
    my_core = lax.axis_index("c")
    chip_sums = [_pair_sum(lax.dynamic_index_in_dim(a, my_core, 1, keepdims=False), o, f"pair_sum{j}")
                 for j, (a, o) in enumerate(zip(by_chip, from_sibling))]
    s_up0, s_down0, s_in_a, s_out_a = _scatter_chips(chip_sums, "scatter_chips")

    def upd(w, m, v, s, name):
        shape = w.shape
        outs = _adamw(w.reshape(-1, shape[-1]), m.reshape(-1, shape[-1]), v.reshape(-1, shape[-1]), s, name)
        return [o.reshape(shape) for o in outs]

    res = {}
    res["w_out_b"] = upd(w_out_b, m_w_out_b, v_w_out_b, s_out_b, "adam_out_b")
    res["w_q"] = upd(w_q, m_w_q, v_w_q, s_q, "adam_q")
    res["w_kv"] = upd(w_kv, m_w_kv, v_w_kv, s_kv, "adam_kv")
    res["w_in_a"] = upd(w_in_a, m_w_in_a, v_w_in_a, s_in_a, "adam_in_a")
    res["w_out_a"] = upd(w_out_a, m_w_out_a, v_w_out_a, s_out_a, "adam_out_a")
    res["w_up"] = _adamw_layers(w_up, m_w_up, v_w_up, s_up0, s_up1, "adam_up")
    res["w_down"] = _adamw_layers(w_down, m_w_down, v_w_down, s_down0, s_down1, "adam_down")
    res["w_ple_gate"] = _adamw_layers(w_ple_gate, m_w_ple_gate, v_w_ple_gate, s_gate0, s_gate1, "adam_gate")
    res["w_ple_proj"] = _adamw_layers(w_ple_proj, m_w_ple_proj, v_w_ple_proj, s_proj0, s_proj1, "adam_proj")

    res["w_spatial"] = [o.reshape(w_spatial.shape) for o in _adamw(
        w_spatial.reshape(-1, 128), m_w_spatial.reshape(-1, 128), v_w_spatial.reshape(-1, 128), s_ws, "adam_spatial")]
    outs = _adamw(w_pack, m_pack, v_pack, g_all, "adam_small")
    loss = outs[0][-8, 0]
    row = 0
    for nm, g, w, m, v in small:
        nrows = _rows128(w).shape[0]
        res[nm] = [o[row:row + nrows].reshape(-1)[:w.size].reshape(w.shape) for o in outs]
        row += nrows
    for nm, g, w, m, v in sharded_vec:
        res[nm] = [lax.dynamic_slice_in_dim(o[row:row + NDEV], me, 1, axis=0) for o in outs]
        row += NDEV

    names = ["ln_mix_a", "w_in_a", "g_v_a", "w_spatial", "b_spatial", "w_out_a", "ln_kv", "w_kv", "g_k", "ln_mix_b",
             "w_q", "g_q", "w_out_b", "ln_mlp", "w_up", "w_down", "ln_ple", "w_ple_gate", "w_ple_proj"]
    out = [loss, dx0[None]]
    for t in range(4):
        out += [res[nm][t] for nm in names]
    return tuple(out)
```

```python
import math

import jax
import jax.numpy as jnp
from jax import lax
from jax.experimental import pallas as pl
from jax.experimental.pallas import tpu as pltpu

F32 = jnp.float32
MXU = jnp.bfloat16
COMM = jnp.bfloat16
EPS = 1e-6
NDEV = 8
HEAD_DIM = 64
CHUNK = 128
GROUPS = 8
QBLK = 128
SCALE = HEAD_DIM ** -0.5
TOKEN_TILE = 256
ADAM_LR = 0.001
ADAM_B1 = 0.9
ADAM_B2 = 0.999
ADAM_EPS = 1e-08
ADAM_WD = 0.01
ADAM_STEP = 10
MESH = pl.DeviceIdType.MESH


def _call(body, **kw):
    return pl.pallas_call(body, **kw)


def _params(vmem_mb, n_axes=1):
    return pltpu.CompilerParams(dimension_semantics=("arbitrary",) * n_axes,
                                vmem_limit_bytes=vmem_mb << 20)


def _tile(tm, n):
    return pl.BlockSpec((tm, n), lambda i: (i, 0))


def _whole(shape):
    zeros = (0,) * len(shape)
    return pl.BlockSpec(shape, lambda i: zeros, pipeline_mode=pl.Buffered(1))


def _acc(shape):
    zeros = (0,) * len(shape)
    return pl.BlockSpec(shape, lambda i: zeros)


def _mm(a, b):
    return jnp.dot(a.astype(MXU), b.astype(MXU), preferred_element_type=F32)


def _mm_nt(a, b):
    return lax.dot_general(a.astype(MXU), b.astype(MXU), (((1,), (1,)), ((), ())),
                           preferred_element_type=F32)


def _mm_tn(a, b):
    return lax.dot_general(a.astype(MXU), b.astype(MXU), (((0,), (0,)), ((), ())),
                           preferred_element_type=F32)


def _split_dot(x, ones, terms=2):
    out = None
    for _ in range(terms):
        part = x.astype(MXU)
        x = x - part.astype(F32)
        d = jnp.dot(part, ones, preferred_element_type=F32)
        out = d if out is None else out + d
    return out


def _rms(x, g):
    rstd = lax.rsqrt(jnp.mean(x * x, axis=-1, keepdims=True) + EPS)
    xhat = x * rstd
    return xhat * g, xhat, rstd


def _rms_bwd(dh, xhat, rstd, g):
    dxh = dh * g
    dx = rstd * (dxh - xhat * jnp.mean(dxh * xhat, axis=-1, keepdims=True))
    dg = jnp.sum(dh * xhat, axis=0, keepdims=True)
    return dx, dg


_GELU_C = math.sqrt(2.0 / math.pi)


def _gelu(x):
    t = jnp.tanh(_GELU_C * (x + 0.044715 * (x * x * x)))
    return 0.5 * x * (1.0 + t)


def _gelu_and_grad(x):
    x2 = x * x
    t = jnp.tanh(_GELU_C * (x + 0.044715 * (x2 * x)))
    g = 0.5 * x * (1.0 + t)
    dg = 0.5 * (1.0 + t) + 0.5 * x * (1.0 - t * t) * (_GELU_C * (1.0 + 3.0 * 0.044715 * x2))
    return g, dg


def _softplus(z):
    return jnp.maximum(z, 0.0) + jnp.log(1.0 + jnp.exp(-jnp.abs(z)))


def _tril_mask():
    row = lax.broadcasted_iota(jnp.int32, (CHUNK, CHUNK), 0)
    col = lax.broadcasted_iota(jnp.int32, (CHUNK, CHUNK), 1)
    return row >= col, row <= col


ANY_SPEC = pl.BlockSpec(memory_space=pl.ANY)


def _my_index():
    return 4 * lax.axis_index("x") + 2 * lax.axis_index("y") + lax.axis_index("c")


def _exchange_copies(srcs, lands, send_sems, recv_sems, scatter, arriving):
    x, y, c = lax.axis_index("x"), lax.axis_index("y"), lax.axis_index("c")
    me = 4 * x + 2 * y + c
    out = []
    for a in range(len(srcs)):
        for k in range(NDEV - 1):
            bits = k + 1
            px = 1 - x if (bits >> 2) & 1 else x
            py = 1 - y if (bits >> 1) & 1 else y
            pc = 1 - c if bits & 1 else c
            peer = 4 * px + 2 * py + pc
            src = srcs[a].at[peer] if scatter else srcs[a]
            out.append(pltpu.make_async_remote_copy(
                src_ref=src, dst_ref=lands[a].at[peer if arriving else me],
                send_sem=send_sems.at[a * (NDEV - 1) + k], recv_sem=recv_sems.at[a * (NDEV - 1) + k],
                device_id=(px, py, pc), device_id_type=MESH))
    return out


def _exchange_shapes(arrs, scatter):
    n = len(arrs)
    if n == 0:
        return [], []
    lands = [jax.ShapeDtypeStruct(a.shape if scatter else (NDEV,) + a.shape, a.dtype) for a in arrs]
    sems = [pltpu.SemaphoreType.DMA((n * (NDEV - 1),)), pltpu.SemaphoreType.DMA((n * (NDEV - 1),)),
            pltpu.SemaphoreType.DMA((n,))]
    return lands, sems


def _exchange_start(srcs, lands, sems, scatter):
    send_sems, recv_sems, local_sems = sems
    me = _my_index()
    for a in range(len(srcs)):
        pltpu.make_async_copy(srcs[a].at[me] if scatter else srcs[a], lands[a].at[me], local_sems.at[a]).start()
    for send in _exchange_copies(srcs, lands, send_sems, recv_sems, scatter, False):
        send.start()


def _exchange_finish(srcs, lands, sems, scatter):
    send_sems, recv_sems, local_sems = sems
    me = _my_index()
    for arrive in _exchange_copies(srcs, lands, send_sems, recv_sems, scatter, True):
        arrive.wait_recv()
    for send in _exchange_copies(srcs, lands, send_sems, recv_sems, scatter, False):
        send.wait_send()
    for a in range(len(srcs)):
        pltpu.make_async_copy(srcs[a].at[me] if scatter else srcs[a], lands[a].at[me], local_sems.at[a]).wait()


def _gather_two_level(arrs, name):
    n = len(arrs)
    lands = [jax.ShapeDtypeStruct((NDEV,) + a.shape, a.dtype) for a in arrs]

    def body(*refs):
        srcs, outs = refs[:n], refs[n:2 * n]
        send_sems, recv_sems, local_sems = refs[2 * n:]
        x, y, c = lax.axis_index("x"), lax.axis_index("y"), lax.axis_index("c")
        me, sibling = (x, y, c), (x, y, 1 - c)
        chips = [(1 - x, y), (x, 1 - y), (1 - x, 1 - y)]

        def index(dev):
            return 4 * dev[0] + 2 * dev[1] + dev[2]

        def copy(a, k, block, to, src=None):
            dst = outs[a].at[index(block)]
            return pltpu.make_async_remote_copy(
                src_ref=dst if src is None else src, dst_ref=dst, send_sem=send_sems.at[a, k],
                recv_sem=recv_sems.at[a, k], device_id=to, device_id_type=MESH)

        mine, first, passed = [], [], []
        for a in range(n):
            cp = pltpu.make_async_copy(srcs[a], outs[a].at[index(me)], local_sems.at[a])
            cp.start()
            mine.append(cp)
            first.append(copy(a, 0, me, sibling, src=srcs[a]))
            first += [copy(a, 1 + j, me, (*chip, c), src=srcs[a]) for j, chip in enumerate(chips)]
        for cp in first:
            cp.start()
        for a in range(n):
            for j, chip in enumerate(chips):
                copy(a, 1 + j, (*chip, c), me).wait_recv()
                cp = copy(a, 4 + j, (*chip, c), sibling)
                cp.start()
                passed.append(cp)
        for a in range(n):
            copy(a, 0, sibling, me).wait_recv()
            for j, chip in enumerate(chips):
                copy(a, 4 + j, (*chip, 1 - c), me).wait_recv()
        for cp in first + passed:
            cp.wait_send()
        for cp in mine:
            cp.wait()

    return _call(body, name=name, out_shape=lands, in_specs=[ANY_SPEC] * n, out_specs=[ANY_SPEC] * n,
                 scratch_shapes=[pltpu.SemaphoreType.DMA((n, NDEV - 1)), pltpu.SemaphoreType.DMA((n, NDEV - 1)),
                                 pltpu.SemaphoreType.DMA((n,))])(*arrs)


def _scatter_pair(arrs, extra, name):
    n, ne = len(arrs), len(extra)
    lands = [jax.ShapeDtypeStruct((4,) + a.shape[2:], a.dtype) for a in arrs]
    extra_lands, extra_sems = _exchange_shapes(extra, True)

    def body(*refs):
        srcs, xsrc = refs[:n], refs[n:n + ne]
        outs, xout = refs[n + ne:2 * n + ne], refs[2 * n + ne:2 * (n + ne)]
        send_sems, recv_sems = refs[2 * (n + ne)], refs[2 * (n + ne) + 1]
        xsems = refs[2 * (n + ne) + 2:]
        x, y, c = lax.axis_index("x"), lax.axis_index("y"), lax.axis_index("c")
        _exchange_start(xsrc, xout, xsems, True)
        copies = [pltpu.make_async_remote_copy(
            src_ref=srcs[a].at[k, 1 - c], dst_ref=outs[a].at[k], send_sem=send_sems.at[a, k],
            recv_sem=recv_sems.at[a, k], device_id=(x, y, 1 - c), device_id_type=MESH)
            for a in range(n) for k in range(4)]
        for cp in copies:
            cp.start()
        for cp in copies:
            cp.wait()
        _exchange_finish(xsrc, xout, xsems, True)

    outs = _call(
        body, name=name, out_shape=lands + extra_lands, in_specs=[ANY_SPEC] * (n + ne), out_specs=[ANY_SPEC] * (n + ne),
        scratch_shapes=[pltpu.SemaphoreType.DMA((n, 4)), pltpu.SemaphoreType.DMA((n, 4))] + extra_sems,
    )(*arrs, *extra)
    return outs[:n], outs[n:]


def _pair_sum(own, other, name):
    _, R, C = own.shape
    tr = math.gcd(R, max(8, (128 * 1024) // C))

    def body(a_ref, b_ref, o_ref):
        o_ref[...] = (a_ref[...].astype(F32) + b_ref[...].astype(F32)).astype(COMM)

    blk = pl.BlockSpec((4, tr, C), lambda i: (0, i, 0))
    return _call(body, name=name, grid=(R // tr,), out_shape=jax.ShapeDtypeStruct(own.shape, COMM),
                 in_specs=[blk, blk], out_specs=blk, compiler_params=_params(32))(own, other)


def _scatter_chips(arrs, name):
    n = len(arrs)
    lands = [jax.ShapeDtypeStruct(a.shape, a.dtype) for a in arrs]

    def body(*refs):
        srcs, outs = refs[:n], refs[n:2 * n]
        send_sems, recv_sems, local_sems = refs[2 * n:]
        x, y, c = lax.axis_index("x"), lax.axis_index("y"), lax.axis_index("c")
        chip = 2 * x + y
        others = [(1 - x, y), (x, 1 - y), (1 - x, 1 - y)]
        local = [pltpu.make_async_copy(srcs[a].at[chip], outs[a].at[chip], local_sems.at[a]) for a in range(n)]
        for cp in local:
            cp.start()

        def copies(arriving):
            return [pltpu.make_async_remote_copy(
                src_ref=srcs[a].at[2 * px + py], dst_ref=outs[a].at[2 * px + py if arriving else chip],
                send_sem=send_sems.at[a, j], recv_sem=recv_sems.at[a, j], device_id=(px, py, c), device_id_type=MESH)
                for a in range(n) for j, (px, py) in enumerate(others)]

        for cp in copies(False):
            cp.start()
        for cp in copies(True):
            cp.wait_recv()
        for cp in copies(False):
            cp.wait_send()
        for cp in local:
            cp.wait()

    return _call(body, name=name, out_shape=lands, in_specs=[ANY_SPEC] * n, out_specs=[ANY_SPEC] * n,
                 scratch_shapes=[pltpu.SemaphoreType.DMA((n, 3)), pltpu.SemaphoreType.DMA((n, 3)),
                                 pltpu.SemaphoreType.DMA((n,))])(*arrs)


def _spatial_mix(vnb, ws_ref, bT_ref, mix_ref, tm):
    tri, _ = _tril_mask()
    for g in range(GROUPS):
        wm = jnp.where(tri, ws_ref[g], 0.0).astype(MXU)
        cols = slice(g * CHUNK, (g + 1) * CHUNK)
        for ch in range(tm // CHUNK):
            rows = slice(ch * CHUNK, (ch + 1) * CHUNK)
            mix_ref[rows, cols] = _mm(wm, vnb[rows, cols]) + bT_ref[:, g:g + 1]


def _sgu_fwd(x, ln, w_in, g_v, ws, bT, w_out, name):
    T, D = x.shape
    tm = min(TOKEN_TILE, T)
    nw = w_in.shape[2]

    def body(x_ref, ln_ref, win_ref, gv_ref, ws_ref, bT_ref, wout_ref, xo_ref, z_ref, h_ref, y_ref, mix_ref):
        xv = x_ref[...]
        h, _, _ = _rms(xv, ln_ref[...])
        hb = h.astype(MXU)
        h_ref[...] = hb
        for j in range(NDEV):
            z_ref[:, j * nw:(j + 1) * nw] = _mm(hb, win_ref[j])
        u = _gelu(z_ref[:, :D])
        gv = _gelu(z_ref[:, D:])
        vn, _, _ = _rms(gv, gv_ref[...])
        _spatial_mix(vn.astype(MXU), ws_ref, bT_ref, mix_ref, tm)
        y = (u * mix_ref[...]).astype(MXU)
        y_ref[...] = y
        xo_ref[...] = xv + _mm(y, wout_ref[...])

    return _call(
        body, name=name, grid=(T // tm,),
        out_shape=[jax.ShapeDtypeStruct((T, D), F32), jax.ShapeDtypeStruct((T, 2 * D), F32),
                   jax.ShapeDtypeStruct((T, D), MXU), jax.ShapeDtypeStruct((T, D), MXU)],
        in_specs=[_tile(tm, D), _whole(ln.shape), _whole(w_in.shape), _whole(g_v.shape), _whole(ws.shape),
                  _whole(bT.shape), _whole(w_out.shape)],
        out_specs=[_tile(tm, D), _tile(tm, 2 * D), _tile(tm, D), _tile(tm, D)],
        scratch_shapes=[pltpu.VMEM((tm, D), F32)],
        compiler_params=_params(40),
    )(x, ln, w_in, g_v, ws, bT, w_out)


def _mlp_fwd(x, ln, w_up, w_down, cargo, name):
    T, D = x.shape
    tm = min(TOKEN_TILE, T)
    nf = w_up.shape[2]
    F = nf * NDEV
    nc = len(cargo)
    lands, sems = _gather_ride_shapes(cargo)

    def body(x_ref, ln_ref, wup_ref, wdown_ref, *rest):
        xo_ref, pre_ref, h_ref = rest[nc:nc + 3]
        finish = _ride_along_gather(rest[:nc] + rest[nc + 3:], nc, 1)
        xv = x_ref[...]
        h, _, _ = _rms(xv, ln_ref[...])
        hb = h.astype(MXU)
        h_ref[...] = hb
        for j in range(NDEV):
            pre_ref[:, j * nf:(j + 1) * nf] = _mm(hb, wup_ref[j])
        a = jnp.maximum(pre_ref[...], 0.0)
        xo_ref[...] = xv + _mm(a * a, wdown_ref[...])
        finish()

    outs = _call(
        body, name=name, grid=(T // tm,),
        out_shape=[jax.ShapeDtypeStruct((T, D), F32), jax.ShapeDtypeStruct((T, F), F32),
                   jax.ShapeDtypeStruct((T, D), MXU)] + lands,
        in_specs=[_tile(tm, D), _whole(ln.shape), _whole(w_up.shape), _whole(w_down.shape)] + [ANY_SPEC] * nc,
        out_specs=[_tile(tm, D), _tile(tm, F), _tile(tm, D)] + [ANY_SPEC] * nc,
        scratch_shapes=sems, compiler_params=_params(52),
    )(x, ln, w_up, w_down, *cargo)
    return outs[0], outs[1], outs[2], outs[3:]


def _ple_fwd(x, p, ln, w_g, w_pp, name):
    T, D = x.shape
    tm = min(TOKEN_TILE, T)
    npp = w_pp.shape[2]

    def body(x_ref, p_ref, ln_ref, wg_ref, wpp_ref, xo_ref, gate_ref, pp_ref, h_ref):
        xv = x_ref[...]
        h, _, _ = _rms(xv, ln_ref[...])
        hb = h.astype(MXU)
        h_ref[...] = hb
        gate = jax.nn.sigmoid(_mm(hb, wg_ref[...]))
        gate_ref[...] = gate
        pb = p_ref[...].astype(MXU)
        for j in range(NDEV):
            pp_ref[:, j * npp:(j + 1) * npp] = _mm(pb, wpp_ref[j])
        xo_ref[...] = xv + pp_ref[...] * gate

    return _call(
        body, name=name, grid=(T // tm,),
        out_shape=[jax.ShapeDtypeStruct((T, D), F32), jax.ShapeDtypeStruct((T, D), F32),
                   jax.ShapeDtypeStruct((T, D), F32), jax.ShapeDtypeStruct((T, D), MXU)],
        in_specs=[_tile(tm, D), _tile(tm, p.shape[1]), _whole(ln.shape), _whole(w_g.shape), _whole(w_pp.shape)],
        out_specs=[_tile(tm, D), _tile(tm, D), _tile(tm, D), _tile(tm, D)],
        compiler_params=_params(32),
    )(x, p, ln, w_g, w_pp)


def _head_ones():
    row = lax.broadcasted_iota(jnp.int32, (128, 128), 0)
    col = lax.broadcasted_iota(jnp.int32, (128, 128), 1)
    return (jnp.right_shift(row, 6) == jnp.right_shift(col, 6)).astype(MXU)


def _head_rms(x, g, ones):
    rstd = lax.rsqrt(_split_dot(x * x, ones, 3) * (1.0 / HEAD_DIM) + EPS)
    xhat = x * rstd
    return xhat * g, xhat, rstd


def _head_rms_bwd(dh, xhat, rstd, g, ones):
    dxh = dh * g
    mean = _split_dot(dxh * xhat, ones, 3) * (1.0 / HEAD_DIM)
    return rstd * (dxh - xhat * mean), jnp.sum(dh * xhat, axis=0, keepdims=True)


def _qkv_fwd(x, ln_q, ln_kv, g_q, g_k, w_q, w_kv, name):
    T, D = x.shape
    tm = min(TOKEN_TILE, T)
    nk = w_kv.shape[2]
    half = NDEV // 2

    def body(x_ref, lnq_ref, lnkv_ref, gq_ref, gk_ref, wq_ref, wkv_ref,
             q_ref, k_ref, v_ref, qpre_ref, kpre_ref, hq_ref, hkv_ref):
        xv = x_ref[...]
        _, xhat, _ = _rms(xv, lnq_ref[...])
        hq = (xhat * lnq_ref[...]).astype(MXU)
        hkv = (xhat * lnkv_ref[...]).astype(MXU)
        hq_ref[...] = hq
        hkv_ref[...] = hkv
        qpre_ref[...] = _mm(hq, wq_ref[...])
        for j in range(half):
            kpre_ref[:, j * nk:(j + 1) * nk] = _mm(hkv, wkv_ref[j])
            v_ref[:, j * nk:(j + 1) * nk] = _mm(hkv, wkv_ref[half + j]).astype(MXU)
        ones = _head_ones()
        for b in range(D // 128):
            cols = slice(b * 128, (b + 1) * 128)
            qn, _, _ = _head_rms(qpre_ref[:, cols], gq_ref[:, cols], ones)
            q_ref[:, cols] = (qn * SCALE).astype(MXU)
            kn, _, _ = _head_rms(kpre_ref[:, cols], gk_ref[:, cols], ones)
            k_ref[:, cols] = kn.astype(MXU)

    return _call(
        body, name=name, grid=(T // tm,),
        out_shape=[jax.ShapeDtypeStruct((T, D), MXU)] * 3 + [jax.ShapeDtypeStruct((T, D), F32)] * 2
        + [jax.ShapeDtypeStruct((T, D), MXU)] * 2,
        in_specs=[_tile(tm, D), _whole(ln_q.shape), _whole(ln_kv.shape), _whole(g_q.shape), _whole(g_k.shape),
                  _whole(w_q.shape), _whole(w_kv.shape)],
        out_specs=[_tile(tm, D)] * 7,
        compiler_params=_params(40),
    )(x, ln_q, ln_kv, g_q, g_k, w_q, w_kv)


SB_KEYS = 2 * QBLK


def _sb_consts():
    row = lax.broadcasted_iota(jnp.int32, (QBLK, QBLK), 0)
    col = lax.broadcasted_iota(jnp.int32, (QBLK, QBLK), 1)
    lane = lax.broadcasted_iota(jnp.int32, (QBLK, 128), 1)
    ones = jnp.ones((QBLK, QBLK), MXU)
    later = jnp.concatenate([(row > col).astype(MXU), ones], axis=1)
    later_eq = jnp.concatenate([(row >= col).astype(MXU), ones], axis=1)
    return later, later_eq, lane < HEAD_DIM


MASKED_LOG = -1e30


def _sb_window(i, w):
    upper = (i + 1) * QBLK - w * SB_KEYS
    start = pl.multiple_of(jnp.maximum(upper - SB_KEYS, 0), QBLK)
    key = lax.broadcasted_iota(jnp.int32, (2 * QBLK, SB_KEYS), 1) + start
    return start, key < upper


def _sb_diagonal():
    row = jnp.bitwise_and(lax.broadcasted_iota(jnp.int32, (2 * QBLK, SB_KEYS), 0), QBLK - 1)
    key = lax.broadcasted_iota(jnp.int32, (2 * QBLK, SB_KEYS), 1)
    cases = []
    for shift in (0, QBLK):
        seen = key < row + shift
        cases.append(jnp.stack([jnp.where(seen, 1.0, 0.0), jnp.where(seen, 0.0, MASKED_LOG)]))
    return jnp.stack(cases).astype(F32)


_SB_DIAG_SPEC = pl.BlockSpec((None, 2, 2 * QBLK, SB_KEYS), lambda h, i: (jnp.minimum(i, 1), 0, 0, 0))


def _sb_terms(x, terms):
    x = jnp.concatenate([x[:, :QBLK], x[:, QBLK:]], axis=0)
    out = []
    for _ in range(terms):
        part = x.astype(MXU)
        x = x - part.astype(F32)
        out.append(part)
    return tuple(out)


def _sb_suffix(parts, ones, carry):
    s = jnp.dot(jnp.concatenate(parts[:2], axis=1), jnp.concatenate([ones, ones], axis=0),
                preferred_element_type=F32)
    for part in parts[2:]:
        s = s + jnp.dot(part, ones, preferred_element_type=F32)
    rows = s.shape[0] // 2
    s_lo, sum_lo, s_hi, sum_hi = s[:rows, :QBLK], s[:rows, QBLK:], s[rows:, :QBLK], s[rows:, QBLK:]
    return jnp.concatenate([s_lo + (carry + sum_hi), s_hi + carry], axis=1), carry + (sum_lo + sum_hi)


def _sb_scores(z, mask):
    sp = _softplus(z)
    l, log_sig = -sp, z - sp
    if isinstance(mask, tuple):
        keep, bias = mask
        l, log_sig = l * keep, log_sig + bias
    else:
        l = jnp.where(mask, l, 0.0)
        log_sig = jnp.where(mask, log_sig, MASKED_LOG)
    return log_sig, _sb_terms(l, 2)


def _sb_weights(staged, later, c_l):
    log_sig, parts = staged
    b, c_l = _sb_suffix(parts, later, c_l)
    return jnp.exp(log_sig + b), c_l


DEAD_LOG = -88.0


def _sb_alive(carry):
    return (jnp.max(carry[0]) > DEAD_LOG).astype(jnp.int32)


def _ride_along(refs, n, scatter, rank=2):
    if n == 0:
        return lambda: None
    step, steps = 0, 1
    for d in range(rank):
        step = step * pl.num_programs(d) + pl.program_id(d)
        steps = steps * pl.num_programs(d)
    srcs, lands, sems = refs[:n], refs[n:2 * n], refs[2 * n:]

    @pl.when(step == 0)
    def _():
        _exchange_start(srcs, lands, sems, scatter)

    def finish():
        @pl.when(step == steps - 1)
        def _():
            _exchange_finish(srcs, lands, sems, scatter)

    return finish


def _gather_ride_shapes(arrs):
    n = len(arrs)
    if n == 0:
        return [], []
    return ([jax.ShapeDtypeStruct((NDEV,) + a.shape, a.dtype) for a in arrs],
            [pltpu.SemaphoreType.DMA((n, NDEV - 1)), pltpu.SemaphoreType.DMA((n, NDEV - 1)),
             pltpu.SemaphoreType.DMA((n,))])


def _ride_along_gather(refs, n, rank):
    if n == 0:
        return lambda: None
    step, steps = 0, 1
    for d in range(rank):
        step = step * pl.num_programs(d) + pl.program_id(d)
        steps = steps * pl.num_programs(d)
    srcs, outs = refs[:n], refs[n:2 * n]
    send_sems, recv_sems, local_sems = refs[2 * n:]
    x, y, c = lax.axis_index("x"), lax.axis_index("y"), lax.axis_index("c")
    me, sibling = (x, y, c), (x, y, 1 - c)
    chips = [(1 - x, y), (x, 1 - y), (1 - x, 1 - y)]

    def index(dev):
        return 4 * dev[0] + 2 * dev[1] + dev[2]

    def copy(a, k, block, to, src=None):
        dst = outs[a].at[index(block)]
        return pltpu.make_async_remote_copy(
            src_ref=dst if src is None else src, dst_ref=dst, send_sem=send_sems.at[a, k],
            recv_sem=recv_sems.at[a, k], device_id=to, device_id_type=MESH)

    def mine(a):
        return pltpu.make_async_copy(srcs[a], outs[a].at[index(me)], local_sems.at[a])

    def first(a):
        return [copy(a, 0, me, sibling, src=srcs[a])] + [copy(a, 1 + j, me, (*chip, c), src=srcs[a])
                                                         for j, chip in enumerate(chips)]

    @pl.when(step == 0)
    def _():
        for a in range(n):
            mine(a).start()
            for cp in first(a):
                cp.start()

    @pl.when(step == (3 * steps) // 4)
    def _():
        for a in range(n):
            for j, chip in enumerate(chips):
                copy(a, 1 + j, (*chip, c), me).wait_recv()
                copy(a, 4 + j, (*chip, c), sibling).start()

    def finish():
        @pl.when(step == steps - 1)
        def _():
            for a in range(n):
                copy(a, 0, sibling, me).wait_recv()
                for j, chip in enumerate(chips):
                    copy(a, 4 + j, (*chip, 1 - c), me).wait_recv()
                for cp in first(a):
                    cp.wait_send()
                for j, chip in enumerate(chips):
                    copy(a, 4 + j, (*chip, c), sibling).wait_send()
                mine(a).wait()

    return finish


def _sb_fwd(q, k, v, cargo, name):
    T, D = q.shape
    nc = len(cargo)
    lands, sems = _gather_ride_shapes(cargo)

    def body(diag_ref, q_ref, k_ref, v_ref, *rest):
        o_ref = rest[nc]
        finish = _ride_along_gather(rest[:nc] + rest[nc + 1:], nc, 2)
        i = pl.program_id(1)
        n_steps = (i + 2) // 2
        later, _, first = _sb_consts()
        qv = q_ref[...]
        zero = jnp.zeros_like(qv)
        q2 = jnp.concatenate([jnp.where(first, qv, zero), jnp.where(first, zero, qv)], axis=0)

        def window(w, carry, diagonal):
            start, mask = _sb_window(i, w)
            if diagonal:
                mask = (diag_ref[0], diag_ref[1])
            kw = k_ref[pl.ds(start, SB_KEYS), :]
            vw = v_ref[pl.ds(start, SB_KEYS), :]
            c_l, acc = carry
            a, c_l = _sb_weights(_sb_scores(_mm_nt(q2, kw), mask), later, c_l)
            return c_l, acc + _mm(a, vw)

        def step(state):
            w, _, carry = state
            carry = window(w, carry, False)
            return w + 1, _sb_alive(carry), carry

        carry = window(0, (jnp.zeros((2 * QBLK, 128), F32),) * 2, True)
        _, _, carry = lax.while_loop(lambda s: (s[0] < n_steps) & (s[1] > 0), step,
                                     (jnp.int32(1), _sb_alive(carry), carry))
        o_ref[...] = jnp.where(first, carry[1][:QBLK], carry[1][QBLK:])
        finish()

    qblk = pl.BlockSpec((QBLK, 128), lambda h, i: (i, h))
    kblk = pl.BlockSpec((T, 128), lambda h, i: (0, h))
    outs = _call(
        body, name=name, grid=(D // 128, T // QBLK), out_shape=[jax.ShapeDtypeStruct((T, D), F32)] + lands,
        in_specs=[_SB_DIAG_SPEC, qblk, kblk, kblk] + [ANY_SPEC] * nc, out_specs=[qblk] + [ANY_SPEC] * nc,
        scratch_shapes=sems, compiler_params=_params(32, 2),
    )(_sb_diagonal(), q, k, v, *cargo)
    return outs[0], outs[1:]


def _sb_bwd(q, k, v, o, do, cargo, name):
    T, D = q.shape
    nc = len(cargo)
    lands, sems = _exchange_shapes(cargo, True)

    def body(diag_ref, q_ref, k_ref, v_ref, o_ref, do_ref, *rest):
        dq_ref, dk_ref, dv_ref = rest[nc:nc + 3]
        finish = _ride_along(rest[:nc] + rest[nc + 3:], nc, True)
        i = pl.program_id(1)

        @pl.when(i == 0)
        def _():
            dk_ref[...] = jnp.zeros_like(dk_ref)
            dv_ref[...] = jnp.zeros_like(dv_ref)

        n_steps = (i + 2) // 2
        later, later_eq, first = _sb_consts()
        qv = q_ref[...]
        dob = do_ref[...].astype(MXU)
        zero = jnp.zeros_like(qv)
        q2 = jnp.concatenate([jnp.where(first, qv, zero), jnp.where(first, zero, qv)], axis=0)
        do2 = jnp.concatenate([jnp.where(first, dob, zero), jnp.where(first, zero, dob)], axis=0)
        prod = o_ref[...] * dob.astype(F32)
        prod2 = jnp.concatenate([jnp.where(first, prod, 0.0), jnp.where(first, 0.0, prod)], axis=0)
        total = _split_dot(prod2, jnp.ones((128, 128), MXU), 3)
        total = jnp.concatenate([total, total], axis=1)

        def window(w, carry, diagonal):
            start, mask = _sb_window(i, w)
            if diagonal:
                mask = (diag_ref[0], diag_ref[1])
            kw = k_ref[pl.ds(start, SB_KEYS), :]
            vw = v_ref[pl.ds(start, SB_KEYS), :]
            c_l, c_e, dq = carry
            log_sig, parts = _sb_scores(_mm_nt(q2, kw), mask)
            a, c_l = _sb_weights((log_sig, parts), later, c_l)
            ab = a.astype(MXU)
            e = ab.astype(F32) * _mm_nt(do2, vw)
            from_here, c_e = _sb_suffix(_sb_terms(e, 2), later_eq, c_e)
            sig = jnp.exp(log_sig)
            dzb = (e * (1.0 - sig) - sig * (total - from_here)).astype(MXU)
            dk_ref[pl.ds(start, SB_KEYS), :] += _mm_tn(dzb, q2)
            dv_ref[pl.ds(start, SB_KEYS), :] += _mm_tn(ab, do2)
            return c_l, c_e, dq + _mm(dzb, kw)

        def step(state):
            w, _, carry = state
            carry = window(w, carry, False)
            return w + 1, _sb_alive(carry), carry

        carry = window(0, (jnp.zeros((2 * QBLK, 128), F32),) * 3, True)
        _, _, carry = lax.while_loop(lambda s: (s[0] < n_steps) & (s[1] > 0), step,
                                     (jnp.int32(1), _sb_alive(carry), carry))
        dq_ref[...] = jnp.where(first, carry[2][:QBLK], carry[2][QBLK:]) * SCALE
        finish()

    qblk = pl.BlockSpec((QBLK, 128), lambda h, i: (i, h))
    kblk = pl.BlockSpec((T, 128), lambda h, i: (0, h))
    full = jax.ShapeDtypeStruct((T, D), F32)
    outs = _call(
        body, name=name, grid=(D // 128, T // QBLK), out_shape=[full, full, full] + lands,
        in_specs=[_SB_DIAG_SPEC, qblk, kblk, kblk, qblk, qblk] + [ANY_SPEC] * nc,
        out_specs=[qblk, kblk, kblk] + [ANY_SPEC] * nc, scratch_shapes=sems, compiler_params=_params(32, 2),
    )(_sb_diagonal(), q, k, v, o, do, *cargo)
    return outs[0], outs[1], outs[2], outs[3:]


def _proj_res(x, a, w, name):
    T, D = x.shape
    tm = min(TOKEN_TILE, T)

    def body(x_ref, a_ref, w_ref, o_ref):
        o_ref[...] = x_ref[...] + _mm(a_ref[...], w_ref[...])

    return _call(
        body, name=name, grid=(T // tm,), out_shape=jax.ShapeDtypeStruct((T, D), F32),
        in_specs=[_tile(tm, D), _tile(tm, a.shape[1]), _whole(w.shape)], out_specs=_tile(tm, D),
        compiler_params=_params(32),
    )(x, a, w)


def _proj_nt(g, w, name):
    T = g.shape[0]
    K = w.shape[0]
    tm = min(TOKEN_TILE, T)

    def body(g_ref, w_ref, o_ref):
        o_ref[...] = _mm_nt(g_ref[...], w_ref[...])

    return _call(
        body, name=name, grid=(T // tm,), out_shape=jax.ShapeDtypeStruct((T, K), F32),
        in_specs=[_tile(tm, g.shape[1]), _whole(w.shape)], out_specs=_tile(tm, K),
        compiler_params=_params(32),
    )(g, w)


def _loss_grad(y, tgt, name):
    T, D = y.shape
    tm = min(TOKEN_TILE, T)

    def body(y_ref, t_ref, dy_ref, loss_ref):
        @pl.when(pl.program_id(0) == 0)
        def _():
            loss_ref[...] = jnp.zeros_like(loss_ref)
        diff = y_ref[...] - t_ref[...]
        dy_ref[...] = diff * (1.0 / D)
        rows = jnp.sum(diff * diff, axis=1, keepdims=True) * (1.0 / D)
        loss_ref[...] += 0.5 * jnp.sum(rows, axis=0, keepdims=True)

    return _call(
        body, name=name, grid=(T // tm,),
        out_shape=[jax.ShapeDtypeStruct((T, D), F32), jax.ShapeDtypeStruct((1, 1), F32)],
        in_specs=[_tile(tm, D), _tile(tm, D)], out_specs=[_tile(tm, D), _acc((1, 1))],
        compiler_params=_params(32),
    )(y, tgt)


def _ple_bwd(dx, x, gate, pp, ln, w_g, name):
    T, D = x.shape
    tm = min(TOKEN_TILE, T)

    def body(dx_ref, x_ref, gate_ref, pp_ref, ln_ref, wg_ref, dxo_ref, dpp_ref, dgp_ref, dln_ref):
        @pl.when(pl.program_id(0) == 0)
        def _():
            dln_ref[...] = jnp.zeros_like(dln_ref)
        dxv = dx_ref[...]
        gate = gate_ref[...]
        _, xhat, rstd = _rms(x_ref[...], ln_ref[...])
        dpp_ref[...] = (dxv * gate).astype(MXU)
        dgp = (dxv * pp_ref[...] * gate * (1.0 - gate)).astype(MXU)
        dgp_ref[...] = dgp
        dxn, dln = _rms_bwd(_mm_nt(dgp, wg_ref[...]), xhat, rstd, ln_ref[...])
        dln_ref[...] += dln
        dxo_ref[...] = dxn + dxv

    return _call(
        body, name=name, grid=(T // tm,),
        out_shape=[jax.ShapeDtypeStruct((T, D), F32), jax.ShapeDtypeStruct((T, D), MXU),
                   jax.ShapeDtypeStruct((T, D), MXU), jax.ShapeDtypeStruct(ln.shape, F32)],
        in_specs=[_tile(tm, D)] * 4 + [_whole(ln.shape), _whole(w_g.shape)],
        out_specs=[_tile(tm, D), _tile(tm, D), _tile(tm, D), _acc(ln.shape)],
        compiler_params=_params(32),
    )(dx, x, gate, pp, ln, w_g)


def _mlp_bwd(dx, x, pre, ln, w_up, w_down, cargo, name):
    T, D = x.shape
    tm = min(TOKEN_TILE, T)
    nf = w_up.shape[2]
    F = nf * NDEV
    nc = len(cargo)
    lands, sems = _exchange_shapes(cargo, True)

    def body(dx_ref, x_ref, pre_ref, ln_ref, wup_ref, wdown_ref, *rest):
        dxo_ref, dpre_ref, s_ref, dln_ref = rest[nc:nc + 4]
        finish = _ride_along(rest[:nc] + rest[nc + 4:], nc, True, rank=1)

        @pl.when(pl.program_id(0) == 0)
        def _():
            dln_ref[...] = jnp.zeros_like(dln_ref)
        dxv = dx_ref[...]
        _, xhat, rstd = _rms(x_ref[...], ln_ref[...])
        a = jnp.maximum(pre_ref[...], 0.0)
        s_ref[...] = (a * a).astype(MXU)
        dpre_ref[...] = (_mm_nt(dxv, wdown_ref[...]) * (2.0 * a)).astype(MXU)
        dh = _mm_nt(dpre_ref[:, :nf], wup_ref[0])
        for j in range(1, NDEV):
            dh += _mm_nt(dpre_ref[:, j * nf:(j + 1) * nf], wup_ref[j])
        dxn, dln = _rms_bwd(dh, xhat, rstd, ln_ref[...])
        dln_ref[...] += dln
        dxo_ref[...] = dxn + dxv
        finish()

    outs = _call(
        body, name=name, grid=(T // tm,),
        out_shape=[jax.ShapeDtypeStruct((T, D), F32), jax.ShapeDtypeStruct((T, F), MXU),
                   jax.ShapeDtypeStruct((T, F), MXU), jax.ShapeDtypeStruct(ln.shape, F32)] + lands,
        in_specs=[_tile(tm, D), _tile(tm, D), _tile(tm, F), _whole(ln.shape), _whole(w_up.shape),
                  _whole(w_down.shape)] + [ANY_SPEC] * nc,
        out_specs=[_tile(tm, D), _tile(tm, F), _tile(tm, F), _acc(ln.shape)] + [ANY_SPEC] * nc,
        scratch_shapes=sems, compiler_params=_params(56),
    )(dx, x, pre, ln, w_up, w_down, *cargo)
    return outs[0], outs[1], outs[2], outs[3], outs[4:]


def _qkv_bwd(dx, x, dq, dk, dv, q_pre, k_pre, ln_q, ln_kv, g_q, g_k, w_q, w_kv, cargo, name):
    T, D = x.shape
    tm = min(TOKEN_TILE, T)
    nk = w_kv.shape[2]
    n_tiles = T // tm
    nc = len(cargo)
    lands, sems = _exchange_shapes(cargo, True)

    def body(dx_ref, x_ref, dq_ref, dk_ref, dv_ref, qpre_ref, kpre_ref, lnq_ref, lnkv_ref, gq_ref, gk_ref,
             wq_ref, wkv_ref, *rest):
        dxo_ref, dqp_ref, dkv_ref, dlnq_ref, dlnkv_ref, dgq_ref, dgk_ref = rest[nc:nc + 7]
        gq_acc, gk_acc = rest[2 * nc + 7:2 * nc + 9]
        finish = _ride_along(rest[:nc] + rest[nc + 7:2 * nc + 7] + rest[2 * nc + 9:], nc, True, rank=1)
        i = pl.program_id(0)

        @pl.when(i == 0)
        def _():
            dlnq_ref[...] = jnp.zeros_like(dlnq_ref)
            dlnkv_ref[...] = jnp.zeros_like(dlnkv_ref)
            gq_acc[...] = jnp.zeros_like(gq_acc)
            gk_acc[...] = jnp.zeros_like(gk_acc)

        ones = _head_ones()
        for b in range(D // 128):
            cols = slice(b * 128, (b + 1) * 128)
            _, xh, rs = _head_rms(qpre_ref[:, cols], gq_ref[:, cols], ones)
            d, dg = _head_rms_bwd(dq_ref[:, cols], xh, rs, gq_ref[:, cols], ones)
            dqp_ref[:, cols] = d.astype(MXU)
            gq_acc[:, cols] += dg
            _, xh, rs = _head_rms(kpre_ref[:, cols], gk_ref[:, cols], ones)
            d, dg = _head_rms_bwd(dk_ref[:, cols], xh, rs, gk_ref[:, cols], ones)
            dkv_ref[:, cols] = d.astype(MXU)
            gk_acc[:, cols] += dg
        dkv_ref[:, D:] = dv_ref[...].astype(MXU)

        _, xhat, rstd = _rms(x_ref[...], lnq_ref[...])
        dhq = _mm_nt(dqp_ref[...], wq_ref[...])
        dhkv = _mm_nt(dkv_ref[:, :nk], wkv_ref[0])
        for j in range(1, NDEV):
            dhkv += _mm_nt(dkv_ref[:, j * nk:(j + 1) * nk], wkv_ref[j])
        dxq, dlnq = _rms_bwd(dhq, xhat, rstd, lnq_ref[...])
        dxkv, dlnkv = _rms_bwd(dhkv, xhat, rstd, lnkv_ref[...])
        dlnq_ref[...] += dlnq
        dlnkv_ref[...] += dlnkv
        dxo_ref[...] = dx_ref[...] + dxq + dxkv

        @pl.when(i == n_tiles - 1)
        def _():
            row = lax.broadcasted_iota(jnp.int32, (D, 128), 0)
            col = lax.broadcasted_iota(jnp.int32, (D, 128), 1)
            fold = (jnp.bitwise_and(row, HEAD_DIM - 1) == col).astype(MXU)
            dgq_ref[...] = _split_dot(jnp.broadcast_to(gq_acc[...], (8, D)), fold, 3)
            dgk_ref[...] = _split_dot(jnp.broadcast_to(gk_acc[...], (8, D)), fold, 3)

        finish()

    small = jax.ShapeDtypeStruct((8, 128), F32)
    outs = _call(
        body, name=name, grid=(n_tiles,),
        out_shape=[jax.ShapeDtypeStruct((T, D), F32), jax.ShapeDtypeStruct((T, D), MXU),
                   jax.ShapeDtypeStruct((T, 2 * D), MXU), jax.ShapeDtypeStruct(ln_q.shape, F32),
                   jax.ShapeDtypeStruct(ln_kv.shape, F32), small, small] + lands,
        in_specs=[_tile(tm, D)] * 7 + [_whole(ln_q.shape), _whole(ln_kv.shape), _whole(g_q.shape),
                                       _whole(g_k.shape), _whole(w_q.shape), _whole(w_kv.shape)] + [ANY_SPEC] * nc,
        out_specs=[_tile(tm, D), _tile(tm, D), _tile(tm, 2 * D), _acc(ln_q.shape), _acc(ln_kv.shape),
                   _acc((8, 128)), _acc((8, 128))] + [ANY_SPEC] * nc,
        scratch_shapes=[pltpu.VMEM((1, D), F32), pltpu.VMEM((1, D), F32)] + sems,
        compiler_params=_params(48),
    )(dx, x, dq, dk, dv, q_pre, k_pre, ln_q, ln_kv, g_q, g_k, w_q, w_kv, *cargo)
    return outs[:7], outs[7:]


def _sgu_bwd(dx, x, z, ln, w_in, g_v, ws, wsT, bT, w_out, cargo, name):
    T, D = x.shape
    tm = min(TOKEN_TILE, T)
    nw = w_in.shape[2]
    nc = len(cargo)
    lands, sems = _exchange_shapes(cargo, True)

    def body(dx_ref, x_ref, z_ref, ln_ref, win_ref, gv_ref, ws_ref, wsT_ref, bT_ref, wout_ref, *rest):
        dxo_ref, dz_ref, dws_ref, dbT_ref, dln_ref, dgv_ref = rest[nc:nc + 6]
        mix_ref, dvn_ref = rest[2 * nc + 6:2 * nc + 8]
        finish = _ride_along(rest[:nc] + rest[nc + 6:2 * nc + 6] + rest[2 * nc + 8:], nc, True, rank=1)

        @pl.when(pl.program_id(0) == 0)
        def _():
            dws_ref[...] = jnp.zeros_like(dws_ref)
            dbT_ref[...] = jnp.zeros_like(dbT_ref)
            dln_ref[...] = jnp.zeros_like(dln_ref)
            dgv_ref[...] = jnp.zeros_like(dgv_ref)
        dxv = dx_ref[...]
        _, xhat, rstd = _rms(x_ref[...], ln_ref[...])
        u, du = _gelu_and_grad(z_ref[:, :D])
        gv, dgv = _gelu_and_grad(z_ref[:, D:])
        vn, vhat, rstd_v = _rms(gv, gv_ref[...])
        vnb = vn.astype(MXU)
        _spatial_mix(vnb, ws_ref, bT_ref, mix_ref, tm)
        dy = _mm_nt(dxv, wout_ref[...])
        d_u = dy * mix_ref[...]
        d_mix = dy * u
        dmb = d_mix.astype(MXU)
        tri, triT = _tril_mask()
        for g in range(GROUPS):
            wmT = jnp.where(triT, wsT_ref[g], 0.0).astype(MXU)
            cols = slice(g * CHUNK, (g + 1) * CHUNK)
            for ch in range(tm // CHUNK):
                rows = slice(ch * CHUNK, (ch + 1) * CHUNK)
                dm = dmb[rows, cols]
                dws_ref[g] += jnp.where(tri, _mm_nt(dm, vnb[rows, cols]), 0.0)
                dbT_ref[:, g:g + 1] += jnp.sum(d_mix[rows, cols], axis=1, keepdims=True)
                dvn_ref[rows, cols] = _mm(wmT, dm)
        d_gv, dg = _rms_bwd(dvn_ref[...], vhat, rstd_v, gv_ref[...])
        dgv_ref[...] += dg
        dz_ref[:, :D] = (d_u * du).astype(MXU)
        dz_ref[:, D:] = (d_gv * dgv).astype(MXU)
        dh = _mm_nt(dz_ref[:, :nw], win_ref[0])
        for j in range(1, NDEV):
            dh += _mm_nt(dz_ref[:, j * nw:(j + 1) * nw], win_ref[j])
        dxn, dln = _rms_bwd(dh, xhat, rstd, ln_ref[...])
        dln_ref[...] += dln
        dxo_ref[...] = dxn + dxv
        finish()

    outs = _call(
        body, name=name, grid=(T // tm,),
        out_shape=[jax.ShapeDtypeStruct((T, D), F32), jax.ShapeDtypeStruct((T, 2 * D), MXU),
                   jax.ShapeDtypeStruct(ws.shape, F32), jax.ShapeDtypeStruct(bT.shape, F32),
                   jax.ShapeDtypeStruct(ln.shape, F32), jax.ShapeDtypeStruct(g_v.shape, F32)] + lands,
        in_specs=[_tile(tm, D), _tile(tm, D), _tile(tm, 2 * D), _whole(ln.shape), _whole(w_in.shape),
                  _whole(g_v.shape), _whole(ws.shape), _whole(wsT.shape), _whole(bT.shape), _whole(w_out.shape)]
        + [ANY_SPEC] * nc,
        out_specs=[_tile(tm, D), _tile(tm, 2 * D), _acc(ws.shape), _acc(bT.shape), _acc(ln.shape),
                   _acc(g_v.shape)] + [ANY_SPEC] * nc,
        scratch_shapes=[pltpu.VMEM((tm, D), F32), pltpu.VMEM((tm, D), F32)] + sems,
        compiler_params=_params(48),
    )(dx, x, z, ln, w_in, g_v, ws, wsT, bT, w_out, *cargo)
    return outs[:6], outs[6:]


def _wgrad_rows(a, g, name):
    T, K = a.shape
    N = g.shape[1]
    kb = K // NDEV

    def body(a_ref, g_ref, o_ref):
        o_ref[...] = _mm_tn(a_ref[...], g_ref[...]).astype(COMM)

    return _call(
        body, name=name, grid=(NDEV,), out_shape=jax.ShapeDtypeStruct((K, N), COMM),
        in_specs=[pl.BlockSpec((T, kb), lambda j: (0, j)), _whole(g.shape)],
        out_specs=pl.BlockSpec((kb, N), lambda j: (j, 0)),
        compiler_params=_params(40),
    )(a, g).reshape(NDEV, kb, N)


def _wgrad_cols(a, g, name):
    T, K = a.shape
    N = g.shape[1]
    nb = N // NDEV

    def body(a_ref, g_ref, o_ref):
        o_ref[...] = _mm_tn(a_ref[...], g_ref[...]).astype(COMM)

    return _call(
        body, name=name, grid=(NDEV,), out_shape=jax.ShapeDtypeStruct((NDEV, K, nb), COMM),
        in_specs=[_whole(a.shape), pl.BlockSpec((T, nb), lambda j: (0, j))],
        out_specs=pl.BlockSpec((None, K, nb), lambda j: (j, 0, 0)),
        compiler_params=_params(40),
    )(a, g)


def _adamw_rows(R, C):
    tr = math.gcd(R, max(8, (128 * 1024) // C))
    return R if tr < 64 else tr


def _adamw_update(w_ref, m_ref, v_ref, s_ref, g_ref, d_ref, mo_ref, vo_ref):
    g = s_ref[0].astype(F32)
    for j in range(1, s_ref.shape[0]):
        g = g + s_ref[j].astype(F32)
    mn = ADAM_B1 * m_ref[...] + (1.0 - ADAM_B1) * g
    vn = ADAM_B2 * v_ref[...] + (1.0 - ADAM_B2) * (g * g)
    g_ref[...] = g
    mo_ref[...] = mn
    vo_ref[...] = vn
    m_hat = mn / (1.0 - ADAM_B1 ** ADAM_STEP)
    v_hat = vn / (1.0 - ADAM_B2 ** ADAM_STEP)
    d_ref[...] = -ADAM_LR * (m_hat / (jnp.sqrt(v_hat) + ADAM_EPS) + ADAM_WD * w_ref[...])


def _adamw_layers(w, m, v, slots0, slots1, name):
    _, R, C = w.shape
    tr = _adamw_rows(R, C)
    last = R // tr - 1

    def body(w_ref, m_ref, v_ref, s0_ref, s1_ref, *outs):
        @pl.when(pl.program_id(0) == 0)
        def _():
            _adamw_update(w_ref, m_ref, v_ref, s0_ref, *outs)

        @pl.when(pl.program_id(0) == 1)
        def _():
            _adamw_update(w_ref, m_ref, v_ref, s1_ref, *outs)

    blk = pl.BlockSpec((None, tr, C), lambda l, i: (l, i, 0))
    s0_blk = pl.BlockSpec((slots0.shape[0], tr, C), lambda l, i: (0, jnp.where(l == 0, i, last), 0))
    s1_blk = pl.BlockSpec((slots1.shape[0], tr, C), lambda l, i: (0, jnp.where(l == 1, i, 0), 0))
    out = jax.ShapeDtypeStruct(w.shape, F32)
    return _call(
        body, name=name, grid=(2, R // tr), out_shape=[out, out, out, out],
        in_specs=[blk, blk, blk, s0_blk, s1_blk], out_specs=[blk] * 4, compiler_params=_params(32, 2),
    )(w, m, v, slots0, slots1)


def _adamw(w, m, v, slots, name):
    R, C = w.shape
    n = slots.shape[0]
    tr = _adamw_rows(R, C)

    def body(*refs):
        _adamw_update(*refs)

    blk = pl.BlockSpec((tr, C), lambda i: (i, 0))
    out = jax.ShapeDtypeStruct((R, C), F32)
    return _call(
        body, name=name, grid=(R // tr,), out_shape=[out, out, out, out],
        in_specs=[blk, blk, blk, pl.BlockSpec((n, tr, C), lambda i: (0, i, 0))], out_specs=[blk] * 4,
        compiler_params=_params(32),
    )(w, m, v, slots)


def _rows128(a):
    flat = a.reshape(-1)
    rows = -(-flat.shape[0] // 1024) * 8
    flat = jnp.pad(flat, (0, rows * 128 - flat.shape[0]))
    return flat.reshape(rows, 128)


def kernel(x, p, ln_mix_a, w_in_a, g_v_a, w_spatial, b_spatial, w_out_a, ln_kv, w_kv, g_k, ln_mix_b, w_q, g_q, w_out_b, ln_mlp, w_up, w_down, ln_ple, w_ple_gate, w_ple_proj, loss_target, m_ln_mix_a, m_w_in_a, m_g_v_a, m_w_spatial, m_b_spatial, m_w_out_a, m_ln_kv, m_w_kv, m_g_k, m_ln_mix_b, m_w_q, m_g_q, m_w_out_b, m_ln_mlp, m_w_up, m_w_down, m_ln_ple, m_w_ple_gate, m_w_ple_proj, v_ln_mix_a, v_w_in_a, v_g_v_a, v_w_spatial, v_b_spatial, v_w_out_a, v_ln_kv, v_w_kv, v_g_k, v_ln_mix_b, v_w_q, v_g_q, v_w_out_b, v_ln_mlp, v_w_up, v_w_down, v_ln_ple, v_w_ple_gate, v_w_ple_proj):
    me = 4 * lax.axis_index("x") + 2 * lax.axis_index("y") + lax.axis_index("c")
    D = x.shape[2]
    x0, tgt = x[0], loss_target[0]
    n_layers = w_up.shape[0]

    c = lambda w: w.astype(COMM)
    first = [c(w_in_a[0]), c(w_out_a[0]), ln_mix_a, g_v_a, c(w_up[0]), c(w_down[0]), c(w_ple_gate[0]),
             c(w_ple_proj[0]), c(w_q[0]), c(w_kv)]
    second_small = [c(w_out_b[0]), c(w_ple_gate[1]), c(w_ple_proj[1])]
    second_big = [c(w_up[1]), c(w_down[1])]
    W_in, W_out_a, ln_a, gv_a, W_up0, W_down0, W_g0, W_pp0, W_q, W_kv = _gather_two_level(first, "gather_first")
    W_out_a, ln_a, gv_a = W_out_a.reshape(-1, D), ln_a.reshape(1, D), gv_a.reshape(1, D)
    W_down0, W_g0, W_q = W_down0.reshape(-1, D), W_g0.reshape(-1, D), W_q.reshape(-1, D)
    ws = w_spatial[0]
    wsT = jnp.swapaxes(ws, 1, 2)
    bT = b_spatial[0].T
    ln_kv2, ln_b = ln_kv.reshape(1, D), ln_mix_b
    gk2 = jnp.tile(g_k.reshape(1, HEAD_DIM), (1, D // HEAD_DIM))
    gq2 = jnp.tile(g_q, (1, D // HEAD_DIM))
    ln_m = [ln_mlp[l:l + 1] for l in range(n_layers)]
    ln_p = [ln_ple[l:l + 1] for l in range(n_layers)]

    x1, z, h_a, y_a = _sgu_fwd(x0, ln_a, W_in, gv_a, ws, bT, W_out_a, "sgu_fwd")
    x2, pre0, hm0, _ = _mlp_fwd(x1, ln_m[0], W_up0, W_down0, [], "mlp_fwd0")
    x3, gate0, pp0, hp0 = _ple_fwd(x2, p[0, 0], ln_p[0], W_g0, W_pp0, "ple_fwd0")
    qn, kn, vn, q_pre, k_pre, h_q, h_kv = _qkv_fwd(x3, ln_b, ln_kv2, gq2, gk2, W_q, W_kv, "qkv_fwd")
    o2d, (W_out_b, W_g1, W_pp1, W_up1, W_down1) = _sb_fwd(qn, kn, vn, second_small + second_big, "sb_fwd")
    W_out_b, W_down1, W_g1 = W_out_b.reshape(-1, D), W_down1.reshape(-1, D), W_g1.reshape(-1, D)
    x4 = _proj_res(x3, o2d, W_out_b, "attn_out")
    x5, pre1, hm1, _ = _mlp_fwd(x4, ln_m[1], W_up1, W_down1, [], "mlp_fwd1")
    x6, gate1, pp1, hp1 = _ple_fwd(x5, p[1, 0], ln_p[1], W_g1, W_pp1, "ple_fwd1")
    dy, loss_part = _loss_grad(x6, tgt, "loss_grad")

    dx5, dpp1, dgp1, dlnp1 = _ple_bwd(dy, x5, gate1, pp1, ln_p[1], W_g1, "ple_bwd1")
    dx4, dpre1, s1, dlnm1, _ = _mlp_bwd(dx5, x4, pre1, ln_m[1], W_up1, W_down1, [], "mlp_bwd1")
    wg_big = [_wgrad_cols(hm1, dpre1, "wg_up1"), _wgrad_rows(s1, dx5, "wg_down1")]
    wg_small = [_wgrad_rows(hp1, dgp1, "wg_gate1"), _wgrad_cols(p[1, 0].astype(MXU), dpp1, "wg_proj1"),
                _wgrad_rows(o2d, dx4, "wg_out_b")]
    do2d = _proj_nt(dx4, W_out_b, "attn_out_bwd")
    dqn, dkn, dvn, (s_up1, s_down1, s_gate1, s_proj1, s_out_b) = _sb_bwd(qn, kn, vn, o2d, do2d, wg_big + wg_small,
                                                                         "sb_bwd")
    (dx3, dq_pre, dkv, dlnb, dlnkv, dgq, dgk), _ = _qkv_bwd(
        dx4, x3, dqn, dkn, dvn, q_pre, k_pre, ln_b, ln_kv2, gq2, gk2, W_q, W_kv, [], "qkv_bwd")
    dgq, dgk = dgq[:1, :HEAD_DIM], dgk[:1, :HEAD_DIM]
    dx2, dpp0, dgp0, dlnp0 = _ple_bwd(dx3, x2, gate0, pp0, ln_p[0], W_g0, "ple_bwd0")
    wg_kv = [_wgrad_cols(h_kv, dkv, "wg_kv")]
    dx1, dpre0, s0, dlnm0, (s_kv,) = _mlp_bwd(dx2, x1, pre0, ln_m[0], W_up0, W_down0, wg_kv, "mlp_bwd0")
    wg_ple = [_wgrad_rows(hp0, dgp0, "wg_gate0"), _wgrad_cols(p[0, 0].astype(MXU), dpp0, "wg_proj0"),
              _wgrad_rows(h_q, dq_pre, "wg_q")]
    (dx0, dz, dws, dbT, dlna, dgva), (s_gate0, s_proj0, s_q) = _sgu_bwd(
        dx1, x0, z, ln_a, W_in, gv_a, ws, wsT, bT, W_out_a, wg_ple, "sgu_bwd")
    wg_first = [_wgrad_cols(hm0, dpre0, "wg_up0"), _wgrad_rows(s0, dx2, "wg_down0"), _wgrad_cols(h_a, dz, "wg_in_a"),
                _wgrad_rows(y_a, dx1, "wg_out_a")]

    small = [("b_spatial", dbT.T[None], b_spatial, m_b_spatial, v_b_spatial),
             ("ln_kv", dlnkv.reshape(-1), ln_kv, m_ln_kv, v_ln_kv),
             ("g_k", dgk.reshape(-1), g_k, m_g_k, v_g_k),
             ("ln_mix_b", dlnb, ln_mix_b, m_ln_mix_b, v_ln_mix_b),
             ("g_q", dgq, g_q, m_g_q, v_g_q),
             ("ln_mlp", jnp.concatenate([dlnm0, dlnm1]), ln_mlp, m_ln_mlp, v_ln_mlp),
             ("ln_ple", jnp.concatenate([dlnp0, dlnp1]), ln_ple, m_ln_ple, v_ln_ple)]
    sharded_vec = [("ln_mix_a", dlna, ln_mix_a, m_ln_mix_a, v_ln_mix_a),
                   ("g_v_a", dgva, g_v_a, m_g_v_a, v_g_v_a)]
    packs = [[], [], [], []]
    for _, g, w, m, v in small:
        for lst, a in zip(packs, (g, w, m, v)):
            lst.append(_rows128(a))
    for _, g, w, m, v in sharded_vec:
        packs[0].append(g.reshape(NDEV, -1))
        for lst, a in zip(packs[1:], (w, m, v)):
            lst.append(jnp.broadcast_to(a, (NDEV, a.shape[1])))
    packs[0].append(_rows128(loss_part))
    for lst in packs[1:]:
        lst.append(jnp.zeros((8, 128), F32))
    g_pack, w_pack, m_pack, v_pack = (jnp.concatenate(lst) for lst in packs)
    g_pack8 = jnp.broadcast_to(g_pack[None], (NDEV,) + g_pack.shape)
    dws8 = jnp.broadcast_to(dws.reshape(1, -1, 128).astype(COMM), (NDEV, dws.size // 128, 128))
    by_chip = [a.reshape((4, 2) + a.shape[1:]) for a in wg_first]
    from_sibling, (g_all, s_ws) = _scatter_pair(by_chip, [g_pack8, dws8], "scatter_pair")
    my_core = lax.axis_index("c")
    chip_sums = [_pair_sum(lax.dynamic_index_in_dim(a, my_core, 1, keepdims=False), o, f"pair_sum{j}")
                 for j, (a, o) in enumerate(zip(by_chip, from_sibling))]
    s_up0, s_down0, s_in_a, s_out_a = _scatter_chips(chip_sums, "scatter_chips")

    def upd(w, m, v, s, name):
        shape = w.shape
        outs = _adamw(w.reshape(-1, shape[-1]), m.reshape(-1, shape[-1]), v.reshape(-1, shape[-1]), s, name)
        return [o.reshape(shape) for o in outs]

    res = {}
    res["w_out_b"] = upd(w_out_b, m_w_out_b, v_w_out_b, s_out_b, "adam_out_b")
    res["w_q"] = upd(w_q, m_w_q, v_w_q, s_q, "adam_q")
    res["w_kv"] = upd(w_kv, m_w_kv, v_w_kv, s_kv, "adam_kv")
    res["w_in_a"] = upd(w_in_a, m_w_in_a, v_w_in_a, s_in_a, "adam_in_a")
    res["w_out_a"] = upd(w_out_a, m_w_out_a, v_w_out_a, s_out_a, "adam_out_a")
    res["w_up"] = _adamw_layers(w_up, m_w_up, v_w_up, s_up0, s_up1, "adam_up")
    res["w_down"] = _adamw_layers(w_down, m_w_down, v_w_down, s_down0, s_down1, "adam_down")
    res["w_ple_gate"] = _adamw_layers(w_ple_gate, m_w_ple_gate, v_w_ple_gate, s_gate0, s_gate1, "adam_gate")
    res["w_ple_proj"] = _adamw_layers(w_ple_proj, m_w_ple_proj, v_w_ple_proj, s_proj0, s_proj1, "adam_proj")

    res["w_spatial"] = [o.reshape(w_spatial.shape) for o in _adamw(
        w_spatial.reshape(-1, 128), m_w_spatial.reshape(-1, 128), v_w_spatial.reshape(-1, 128), s_ws, "adam_spatial")]
    outs = _adamw(w_pack, m_pack, v_pack, g_all, "adam_small")
    loss = outs[0][-8, 0]
    row = 0
    for nm, g, w, m, v in small:
        nrows = _rows128(w).shape[0]
        res[nm] = [o[row:row + nrows].reshape(-1)[:w.size].reshape(w.shape) for o in outs]
        row += nrows
    for nm, g, w, m, v in sharded_vec:
        res[nm] = [lax.dynamic_slice_in_dim(o[row:row + NDEV], me, 1, axis=0) for o in outs]
        row += NDEV

    names = ["ln_mix_a", "w_in_a", "g_v_a", "w_spatial", "b_spatial", "w_out_a", "ln_kv", "w_kv", "g_k", "ln_mix_b",
             "w_q", "g_q", "w_out_b", "ln_mlp", "w_up", "w_down", "ln_ple", "w_ple_gate", "w_ple_proj"]
    out = [loss, dx0[None]]
    for t in range(4):
        out += [res[nm][t] for nm in names]
    return tuple(out)
```

```python
import math

import jax
import jax.numpy as jnp
from jax import lax
from jax.experimental import pallas as pl
from jax.experimental.pallas import tpu as pltpu

F32 = jnp.float32
MXU = jnp.bfloat16
COMM = jnp.bfloat16
EPS = 1e-6
NDEV = 8
HEAD_DIM = 64
CHUNK = 128
GROUPS = 8
QBLK = 128
SCALE = HEAD_DIM ** -0.5
TOKEN_TILE = 256
ADAM_LR = 0.001
ADAM_B1 = 0.9
ADAM_B2 = 0.999
ADAM_EPS = 1e-08
ADAM_WD = 0.01
ADAM_STEP = 10
MESH = pl.DeviceIdType.MESH


def _call(body, **kw):
    return pl.pallas_call(body, **kw)


def _params(vmem_mb, n_axes=1):
    return pltpu.CompilerParams(dimension_semantics=("arbitrary",) * n_axes,
                                vmem_limit_bytes=vmem_mb << 20)


def _tile(tm, n):
    return pl.BlockSpec((tm, n), lambda i: (i, 0))


def _whole(shape):
    zeros = (0,) * len(shape)
    return pl.BlockSpec(shape, lambda i: zeros, pipeline_mode=pl.Buffered(1))


def _acc(shape):
    zeros = (0,) * len(shape)
    return pl.BlockSpec(shape, lambda i: zeros)


def _mm(a, b):
    return jnp.dot(a.astype(MXU), b.astype(MXU), preferred_element_type=F32)


def _mm_nt(a, b):
    return lax.dot_general(a.astype(MXU), b.astype(MXU), (((1,), (1,)), ((), ())),
                           preferred_element_type=F32)


def _mm_tn(a, b):
    return lax.dot_general(a.astype(MXU), b.astype(MXU), (((0,), (0,)), ((), ())),
                           preferred_element_type=F32)


def _split_dot(x, ones, terms=2):
    out = None
    for _ in range(terms):
        part = x.astype(MXU)
        x = x - part.astype(F32)
        d = jnp.dot(part, ones, preferred_element_type=F32)
        out = d if out is None else out + d
    return out


def _rms(x, g):
    rstd = lax.rsqrt(jnp.mean(x * x, axis=-1, keepdims=True) + EPS)
    xhat = x * rstd
    return xhat * g, xhat, rstd


def _rms_bwd(dh, xhat, rstd, g):
    dxh = dh * g
    dx = rstd * (dxh - xhat * jnp.mean(dxh * xhat, axis=-1, keepdims=True))
    dg = jnp.sum(dh * xhat, axis=0, keepdims=True)
    return dx, dg


_GELU_C = math.sqrt(2.0 / math.pi)


def _gelu(x):
    t = jnp.tanh(_GELU_C * (x + 0.044715 * (x * x * x)))
    return 0.5 * x * (1.0 + t)


def _gelu_and_grad(x):
    x2 = x * x
    t = jnp.tanh(_GELU_C * (x + 0.044715 * (x2 * x)))
    g = 0.5 * x * (1.0 + t)
    dg = 0.5 * (1.0 + t) + 0.5 * x * (1.0 - t * t) * (_GELU_C * (1.0 + 3.0 * 0.044715 * x2))
    return g, dg


def _softplus(z):
    return jnp.maximum(z, 0.0) + jnp.log(1.0 + jnp.exp(-jnp.abs(z)))


def _tril_mask():
    row = lax.broadcasted_iota(jnp.int32, (CHUNK, CHUNK), 0)
    col = lax.broadcasted_iota(jnp.int32, (CHUNK, CHUNK), 1)
    return row >= col, row <= col


ANY_SPEC = pl.BlockSpec(memory_space=pl.ANY)


def _my_index():
    return 4 * lax.axis_index("x") + 2 * lax.axis_index("y") + lax.axis_index("c")


def _exchange_copies(srcs, lands, send_sems, recv_sems, scatter, arriving):
    x, y, c = lax.axis_index("x"), lax.axis_index("y"), lax.axis_index("c")
    me = 4 * x + 2 * y + c
    out = []
    for a in range(len(srcs)):
        for k in range(NDEV - 1):
            bits = k + 1
            px = 1 - x if (bits >> 2) & 1 else x
            py = 1 - y if (bits >> 1) & 1 else y
            pc = 1 - c if bits & 1 else c
            peer = 4 * px + 2 * py + pc
            src = srcs[a].at[peer] if scatter else srcs[a]
            out.append(pltpu.make_async_remote_copy(
                src_ref=src, dst_ref=lands[a].at[peer if arriving else me],
                send_sem=send_sems.at[a * (NDEV - 1) + k], recv_sem=recv_sems.at[a * (NDEV - 1) + k],
                device_id=(px, py, pc), device_id_type=MESH))
    return out


def _exchange_shapes(arrs, scatter):
    n = len(arrs)
    if n == 0:
        return [], []
    lands = [jax.ShapeDtypeStruct(a.shape if scatter else (NDEV,) + a.shape, a.dtype) for a in arrs]
    sems = [pltpu.SemaphoreType.DMA((n * (NDEV - 1),)), pltpu.SemaphoreType.DMA((n * (NDEV - 1),)),
            pltpu.SemaphoreType.DMA((n,))]
    return lands, sems


def _exchange_start(srcs, lands, sems, scatter):
    send_sems, recv_sems, local_sems = sems
    me = _my_index()
    for a in range(len(srcs)):
        pltpu.make_async_copy(srcs[a].at[me] if scatter else srcs[a], lands[a].at[me], local_sems.at[a]).start()
    for send in _exchange_copies(srcs, lands, send_sems, recv_sems, scatter, False):
        send.start()


def _exchange_finish(srcs, lands, sems, scatter):
    send_sems, recv_sems, local_sems = sems
    me = _my_index()
    for arrive in _exchange_copies(srcs, lands, send_sems, recv_sems, scatter, True):
        arrive.wait_recv()
    for send in _exchange_copies(srcs, lands, send_sems, recv_sems, scatter, False):
        send.wait_send()
    for a in range(len(srcs)):
        pltpu.make_async_copy(srcs[a].at[me] if scatter else srcs[a], lands[a].at[me], local_sems.at[a]).wait()


def _gather_two_level(arrs, name):
    n = len(arrs)
    lands = [jax.ShapeDtypeStruct((NDEV,) + a.shape, a.dtype) for a in arrs]

    def body(*refs):
        srcs, outs = refs[:n], refs[n:2 * n]
        send_sems, recv_sems, local_sems = refs[2 * n:]
        x, y, c = lax.axis_index("x"), lax.axis_index("y"), lax.axis_index("c")
        me, sibling = (x, y, c), (x, y, 1 - c)
        chips = [(1 - x, y), (x, 1 - y), (1 - x, 1 - y)]

        def index(dev):
            return 4 * dev[0] + 2 * dev[1] + dev[2]

        def copy(a, k, block, to, src=None):
            dst = outs[a].at[index(block)]
            return pltpu.make_async_remote_copy(
                src_ref=dst if src is None else src, dst_ref=dst, send_sem=send_sems.at[a, k],
                recv_sem=recv_sems.at[a, k], device_id=to, device_id_type=MESH)

        mine, first, passed = [], [], []
        for a in range(n):
            cp = pltpu.make_async_copy(srcs[a], outs[a].at[index(me)], local_sems.at[a])
            cp.start()
            mine.append(cp)
            first.append(copy(a, 0, me, sibling, src=srcs[a]))
            first += [copy(a, 1 + j, me, (*chip, c), src=srcs[a]) for j, chip in enumerate(chips)]
        for cp in first:
            cp.start()
        for a in range(n):
            for j, chip in enumerate(chips):
                copy(a, 1 + j, (*chip, c), me).wait_recv()
                cp = copy(a, 4 + j, (*chip, c), sibling)
                cp.start()
                passed.append(cp)
        for a in range(n):
            copy(a, 0, sibling, me).wait_recv()
            for j, chip in enumerate(chips):
                copy(a, 4 + j, (*chip, 1 - c), me).wait_recv()
        for cp in first + passed:
            cp.wait_send()
        for cp in mine:
            cp.wait()

    return _call(body, name=name, out_shape=lands, in_specs=[ANY_SPEC] * n, out_specs=[ANY_SPEC] * n,
                 scratch_shapes=[pltpu.SemaphoreType.DMA((n, NDEV - 1)), pltpu.SemaphoreType.DMA((n, NDEV - 1)),
                                 pltpu.SemaphoreType.DMA((n,))])(*arrs)


def _scatter_pair(arrs, extra, name):
    n, ne = len(arrs), len(extra)
    lands = [jax.ShapeDtypeStruct((4,) + a.shape[2:], a.dtype) for a in arrs]
    extra_lands, extra_sems = _exchange_shapes(extra, True)

    def body(*refs):
        srcs, xsrc = refs[:n], refs[n:n + ne]
        outs, xout = refs[n + ne:2 * n + ne], refs[2 * n + ne:2 * (n + ne)]
        send_sems, recv_sems = refs[2 * (n + ne)], refs[2 * (n + ne) + 1]
        xsems = refs[2 * (n + ne) + 2:]
        x, y, c = lax.axis_index("x"), lax.axis_index("y"), lax.axis_index("c")
        _exchange_start(xsrc, xout, xsems, True)
        copies = [pltpu.make_async_remote_copy(
            src_ref=srcs[a].at[k, 1 - c], dst_ref=outs[a].at[k], send_sem=send_sems.at[a, k],
            recv_sem=recv_sems.at[a, k], device_id=(x, y, 1 - c), device_id_type=MESH)
            for a in range(n) for k in range(4)]
        for cp in copies:
            cp.start()
        for cp in copies:
            cp.wait()
        _exchange_finish(xsrc, xout, xsems, True)

    outs = _call(
        body, name=name, out_shape=lands + extra_lands, in_specs=[ANY_SPEC] * (n + ne), out_specs=[ANY_SPEC] * (n + ne),
        scratch_shapes=[pltpu.SemaphoreType.DMA((n, 4)), pltpu.SemaphoreType.DMA((n, 4))] + extra_sems,
    )(*arrs, *extra)
    return outs[:n], outs[n:]


def _pair_sum(own, other, name):
    _, R, C = own.shape
    tr = math.gcd(R, max(8, (128 * 1024) // C))

    def body(a_ref, b_ref, o_ref):
        o_ref[...] = (a_ref[...].astype(F32) + b_ref[...].astype(F32)).astype(COMM)

    blk = pl.BlockSpec((4, tr, C), lambda i: (0, i, 0))
    return _call(body, name=name, grid=(R // tr,), out_shape=jax.ShapeDtypeStruct(own.shape, COMM),
                 in_specs=[blk, blk], out_specs=blk, compiler_params=_params(32))(own, other)


def _scatter_chips(arrs, name):
    n = len(arrs)
    lands = [jax.ShapeDtypeStruct(a.shape, a.dtype) for a in arrs]

    def body(*refs):
        srcs, outs = refs[:n], refs[n:2 * n]
        send_sems, recv_sems, local_sems = refs[2 * n:]
        x, y, c = lax.axis_index("x"), lax.axis_index("y"), lax.axis_index("c")
        chip = 2 * x + y
        others = [(1 - x, y), (x, 1 - y), (1 - x, 1 - y)]
        local = [pltpu.make_async_copy(srcs[a].at[chip], outs[a].at[chip], local_sems.at[a]) for a in range(n)]
        for cp in local:
            cp.start()

        def copies(arriving):
            return [pltpu.make_async_remote_copy(
                src_ref=srcs[a].at[2 * px + py], dst_ref=outs[a].at[2 * px + py if arriving else chip],
                send_sem=send_sems.at[a, j], recv_sem=recv_sems.at[a, j], device_id=(px, py, c), device_id_type=MESH)
                for a in range(n) for j, (px, py) in enumerate(others)]

        for cp in copies(False):
            cp.start()
        for cp in copies(True):
            cp.wait_recv()
        for cp in copies(False):
            cp.wait_send()
        for cp in local:
            cp.wait()

    return _call(body, name=name, out_shape=lands, in_specs=[ANY_SPEC] * n, out_specs=[ANY_SPEC] * n,
                 scratch_shapes=[pltpu.SemaphoreType.DMA((n, 3)), pltpu.SemaphoreType.DMA((n, 3)),
                                 pltpu.SemaphoreType.DMA((n,))])(*arrs)


def _spatial_mix(vnb, ws_ref, bT_ref, mix_ref, tm):
    tri, _ = _tril_mask()
    for g in range(GROUPS):
        wm = jnp.where(tri, ws_ref[g], 0.0).astype(MXU)
        cols = slice(g * CHUNK, (g + 1) * CHUNK)
        for ch in range(tm // CHUNK):
            rows = slice(ch * CHUNK, (ch + 1) * CHUNK)
            mix_ref[rows, cols] = _mm(wm, vnb[rows, cols]) + bT_ref[:, g:g + 1]


def _sgu_fwd(x, ln, w_in, g_v, ws, bT, w_out, name):
    T, D = x.shape
    tm = min(TOKEN_TILE, T)
    nw = w_in.shape[2]

    def body(x_ref, ln_ref, win_ref, gv_ref, ws_ref, bT_ref, wout_ref, xo_ref, z_ref, h_ref, y_ref, mix_ref):
        xv = x_ref[...]
        h, _, _ = _rms(xv, ln_ref[...])
        hb = h.astype(MXU)
        h_ref[...] = hb
        for j in range(NDEV):
            z_ref[:, j * nw:(j + 1) * nw] = _mm(hb, win_ref[j])
        u = _gelu(z_ref[:, :D])
        gv = _gelu(z_ref[:, D:])
        vn, _, _ = _rms(gv, gv_ref[...])
        _spatial_mix(vn.astype(MXU), ws_ref, bT_ref, mix_ref, tm)
        y = (u * mix_ref[...]).astype(MXU)
        y_ref[...] = y
        xo_ref[...] = xv + _mm(y, wout_ref[...])

    return _call(
        body, name=name, grid=(T // tm,),
        out_shape=[jax.ShapeDtypeStruct((T, D), F32), jax.ShapeDtypeStruct((T, 2 * D), F32),
                   jax.ShapeDtypeStruct((T, D), MXU), jax.ShapeDtypeStruct((T, D), MXU)],
        in_specs=[_tile(tm, D), _whole(ln.shape), _whole(w_in.shape), _whole(g_v.shape), _whole(ws.shape),
                  _whole(bT.shape), _whole(w_out.shape)],
        out_specs=[_tile(tm, D), _tile(tm, 2 * D), _tile(tm, D), _tile(tm, D)],
        scratch_shapes=[pltpu.VMEM((tm, D), F32)],
        compiler_params=_params(40),
    )(x, ln, w_in, g_v, ws, bT, w_out)


def _mlp_fwd(x, ln, w_up, w_down, cargo, name):
    T, D = x.shape
    tm = min(TOKEN_TILE, T)
    nf = w_up.shape[2]
    F = nf * NDEV
    nc = len(cargo)
    lands, sems = _gather_ride_shapes(cargo)

    def body(x_ref, ln_ref, wup_ref, wdown_ref, *rest):
        xo_ref, pre_ref, h_ref = rest[nc:nc + 3]
        finish = _ride_along_gather(rest[:nc] + rest[nc + 3:], nc, 1)
        xv = x_ref[...]
        h, _, _ = _rms(xv, ln_ref[...])
        hb = h.astype(MXU)
        h_ref[...] = hb
        for j in range(NDEV):
            pre_ref[:, j * nf:(j + 1) * nf] = _mm(hb, wup_ref[j])
        a = jnp.maximum(pre_ref[...], 0.0)
        xo_ref[...] = xv + _mm(a * a, wdown_ref[...])
        finish()

    outs = _call(
        body, name=name, grid=(T // tm,),
        out_shape=[jax.ShapeDtypeStruct((T, D), F32), jax.ShapeDtypeStruct((T, F), F32),
                   jax.ShapeDtypeStruct((T, D), MXU)] + lands,
        in_specs=[_tile(tm, D), _whole(ln.shape), _whole(w_up.shape), _whole(w_down.shape)] + [ANY_SPEC] * nc,
        out_specs=[_tile(tm, D), _tile(tm, F), _tile(tm, D)] + [ANY_SPEC] * nc,
        scratch_shapes=sems, compiler_params=_params(52),
    )(x, ln, w_up, w_down, *cargo)
    return outs[0], outs[1], outs[2], outs[3:]


def _ple_fwd(x, p, ln, w_g, w_pp, name):
    T, D = x.shape
    tm = min(TOKEN_TILE, T)
    npp = w_pp.shape[2]

    def body(x_ref, p_ref, ln_ref, wg_ref, wpp_ref, xo_ref, gate_ref, pp_ref, h_ref):
        xv = x_ref[...]
        h, _, _ = _rms(xv, ln_ref[...])
        hb = h.astype(MXU)
        h_ref[...] = hb
        gate = jax.nn.sigmoid(_mm(hb, wg_ref[...]))
        gate_ref[...] = gate
        pb = p_ref[...].astype(MXU)
        for j in range(NDEV):
            pp_ref[:, j * npp:(j + 1) * npp] = _mm(pb, wpp_ref[j])
        xo_ref[...] = xv + pp_ref[...] * gate

    return _call(
        body, name=name, grid=(T // tm,),
        out_shape=[jax.ShapeDtypeStruct((T, D), F32), jax.ShapeDtypeStruct((T, D), F32),
                   jax.ShapeDtypeStruct((T, D), F32), jax.ShapeDtypeStruct((T, D), MXU)],
        in_specs=[_tile(tm, D), _tile(tm, p.shape[1]), _whole(ln.shape), _whole(w_g.shape), _whole(w_pp.shape)],
        out_specs=[_tile(tm, D), _tile(tm, D), _tile(tm, D), _tile(tm, D)],
        compiler_params=_params(32),
    )(x, p, ln, w_g, w_pp)


def _head_ones():
    row = lax.broadcasted_iota(jnp.int32, (128, 128), 0)
    col = lax.broadcasted_iota(jnp.int32, (128, 128), 1)
    return (jnp.right_shift(row, 6) == jnp.right_shift(col, 6)).astype(MXU)


def _head_rms(x, g, ones):
    rstd = lax.rsqrt(_split_dot(x * x, ones, 3) * (1.0 / HEAD_DIM) + EPS)
    xhat = x * rstd
    return xhat * g, xhat, rstd


def _head_rms_bwd(dh, xhat, rstd, g, ones):
    dxh = dh * g
    mean = _split_dot(dxh * xhat, ones, 3) * (1.0 / HEAD_DIM)
    return rstd * (dxh - xhat * mean), jnp.sum(dh * xhat, axis=0, keepdims=True)


def _qkv_fwd(x, ln_q, ln_kv, g_q, g_k, w_q, w_kv, name):
    T, D = x.shape
    tm = min(TOKEN_TILE, T)
    nk = w_kv.shape[2]
    half = NDEV // 2

    def body(x_ref, lnq_ref, lnkv_ref, gq_ref, gk_ref, wq_ref, wkv_ref,
             q_ref, k_ref, v_ref, qpre_ref, kpre_ref, hq_ref, hkv_ref):
        xv = x_ref[...]
        _, xhat, _ = _rms(xv, lnq_ref[...])
        hq = (xhat * lnq_ref[...]).astype(MXU)
        hkv = (xhat * lnkv_ref[...]).astype(MXU)
        hq_ref[...] = hq
        hkv_ref[...] = hkv
        qpre_ref[...] = _mm(hq, wq_ref[...])
        for j in range(half):
            kpre_ref[:, j * nk:(j + 1) * nk] = _mm(hkv, wkv_ref[j])
            v_ref[:, j * nk:(j + 1) * nk] = _mm(hkv, wkv_ref[half + j]).astype(MXU)
        ones = _head_ones()
        for b in range(D // 128):
            cols = slice(b * 128, (b + 1) * 128)
            qn, _, _ = _head_rms(qpre_ref[:, cols], gq_ref[:, cols], ones)
            q_ref[:, cols] = (qn * SCALE).astype(MXU)
            kn, _, _ = _head_rms(kpre_ref[:, cols], gk_ref[:, cols], ones)
            k_ref[:, cols] = kn.astype(MXU)

    return _call(
        body, name=name, grid=(T // tm,),
        out_shape=[jax.ShapeDtypeStruct((T, D), MXU)] * 3 + [jax.ShapeDtypeStruct((T, D), F32)] * 2
        + [jax.ShapeDtypeStruct((T, D), MXU)] * 2,
        in_specs=[_tile(tm, D), _whole(ln_q.shape), _whole(ln_kv.shape), _whole(g_q.shape), _whole(g_k.shape),
                  _whole(w_q.shape), _whole(w_kv.shape)],
        out_specs=[_tile(tm, D)] * 7,
        compiler_params=_params(40),
    )(x, ln_q, ln_kv, g_q, g_k, w_q, w_kv)


SB_KEYS = 2 * QBLK


def _sb_consts():
    row = lax.broadcasted_iota(jnp.int32, (QBLK, QBLK), 0)
    col = lax.broadcasted_iota(jnp.int32, (QBLK, QBLK), 1)
    lane = lax.broadcasted_iota(jnp.int32, (QBLK, 128), 1)
    ones = jnp.ones((QBLK, QBLK), MXU)
    later = jnp.concatenate([(row > col).astype(MXU), ones], axis=1)
    later_eq = jnp.concatenate([(row >= col).astype(MXU), ones], axis=1)
    return later, later_eq, lane < HEAD_DIM


MASKED_LOG = -1e30


def _sb_window(i, w):
    upper = (i + 1) * QBLK - w * SB_KEYS
    start = pl.multiple_of(jnp.maximum(upper - SB_KEYS, 0), QBLK)
    key = lax.broadcasted_iota(jnp.int32, (2 * QBLK, SB_KEYS), 1) + start
    return start, key < upper


def _sb_diagonal():
    row = jnp.bitwise_and(lax.broadcasted_iota(jnp.int32, (2 * QBLK, SB_KEYS), 0), QBLK - 1)
    key = lax.broadcasted_iota(jnp.int32, (2 * QBLK, SB_KEYS), 1)
    cases = []
    for shift in (0, QBLK):
        seen = key < row + shift
        cases.append(jnp.stack([jnp.where(seen, 1.0, 0.0), jnp.where(seen, 0.0, MASKED_LOG)]))
    return jnp.stack(cases).astype(F32)


_SB_DIAG_SPEC = pl.BlockSpec((None, 2, 2 * QBLK, SB_KEYS), lambda h, i: (jnp.minimum(i, 1), 0, 0, 0))


def _sb_terms(x, terms):
    x = jnp.concatenate([x[:, :QBLK], x[:, QBLK:]], axis=0)
    out = []
    for _ in range(terms):
        part = x.astype(MXU)
        x = x - part.astype(F32)
        out.append(part)
    return tuple(out)


def _sb_suffix(parts, ones, carry):
    s = jnp.dot(jnp.concatenate(parts[:2], axis=1), jnp.concatenate([ones, ones], axis=0),
                preferred_element_type=F32)
    for part in parts[2:]:
        s = s + jnp.dot(part, ones, preferred_element_type=F32)
    rows = s.shape[0] // 2
    s_lo, sum_lo, s_hi, sum_hi = s[:rows, :QBLK], s[:rows, QBLK:], s[rows:, :QBLK], s[rows:, QBLK:]
    return jnp.concatenate([s_lo + (carry + sum_hi), s_hi + carry], axis=1), carry + (sum_lo + sum_hi)


def _sb_scores(z, mask):
    sp = _softplus(z)
    l, log_sig = -sp, z - sp
    if isinstance(mask, tuple):
        keep, bias = mask
        l, log_sig = l * keep, log_sig + bias
    else:
        l = jnp.where(mask, l, 0.0)
        log_sig = jnp.where(mask, log_sig, MASKED_LOG)
    return log_sig, _sb_terms(l, 2)


def _sb_weights(staged, later, c_l):
    log_sig, parts = staged
    b, c_l = _sb_suffix(parts, later, c_l)
    return jnp.exp(log_sig + b), c_l


DEAD_LOG = -88.0


def _sb_alive(carry):
    return (jnp.max(carry[0]) > DEAD_LOG).astype(jnp.int32)


def _ride_along(refs, n, scatter, rank=2):
    if n == 0:
        return lambda: None
    step, steps = 0, 1
    for d in range(rank):
        step = step * pl.num_programs(d) + pl.program_id(d)
        steps = steps * pl.num_programs(d)
    srcs, lands, sems = refs[:n], refs[n:2 * n], refs[2 * n:]

    @pl.when(step == 0)
    def _():
        _exchange_start(srcs, lands, sems, scatter)

    def finish():
        @pl.when(step == steps - 1)
        def _():
            _exchange_finish(srcs, lands, sems, scatter)

    return finish


def _gather_ride_shapes(arrs):
    n = len(arrs)
    if n == 0:
        return [], []
    return ([jax.ShapeDtypeStruct((NDEV,) + a.shape, a.dtype) for a in arrs],
            [pltpu.SemaphoreType.DMA((n, NDEV - 1)), pltpu.SemaphoreType.DMA((n, NDEV - 1)),
             pltpu.SemaphoreType.DMA((n,))])


def _ride_along_gather(refs, n, rank):
    if n == 0:
        return lambda: None
    step, steps = 0, 1
    for d in range(rank):
        step = step * pl.num_programs(d) + pl.program_id(d)
        steps = steps * pl.num_programs(d)
    srcs, outs = refs[:n], refs[n:2 * n]
    send_sems, recv_sems, local_sems = refs[2 * n:]
    x, y, c = lax.axis_index("x"), lax.axis_index("y"), lax.axis_index("c")
    me, sibling = (x, y, c), (x, y, 1 - c)
    chips = [(1 - x, y), (x, 1 - y), (1 - x, 1 - y)]

    def index(dev):
        return 4 * dev[0] + 2 * dev[1] + dev[2]

    def copy(a, k, block, to, src=None):
        dst = outs[a].at[index(block)]
        return pltpu.make_async_remote_copy(
            src_ref=dst if src is None else src, dst_ref=dst, send_sem=send_sems.at[a, k],
            recv_sem=recv_sems.at[a, k], device_id=to, device_id_type=MESH)

    def mine(a):
        return pltpu.make_async_copy(srcs[a], outs[a].at[index(me)], local_sems.at[a])

    def first(a):
        return [copy(a, 0, me, sibling, src=srcs[a])] + [copy(a, 1 + j, me, (*chip, c), src=srcs[a])
                                                         for j, chip in enumerate(chips)]

    @pl.when(step == 0)
    def _():
        for a in range(n):
            mine(a).start()
            for cp in first(a):
                cp.start()

    @pl.when(step == (3 * steps) // 4)
    def _():
        for a in range(n):
            for j, chip in enumerate(chips):
                copy(a, 1 + j, (*chip, c), me).wait_recv()
                copy(a, 4 + j, (*chip, c), sibling).start()

    def finish():
        @pl.when(step == steps - 1)
        def _():
            for a in range(n):
                copy(a, 0, sibling, me).wait_recv()
                for j, chip in enumerate(chips):
                    copy(a, 4 + j, (*chip, 1 - c), me).wait_recv()
                for cp in first(a):
                    cp.wait_send()
                for j, chip in enumerate(chips):
                    copy(a, 4 + j, (*chip, c), sibling).wait_send()
                mine(a).wait()

    return finish


def _sb_fwd(q, k, v, cargo, name):
    T, D = q.shape
    nc = len(cargo)
    lands, sems = _gather_ride_shapes(cargo)

    def body(diag_ref, q_ref, k_ref, v_ref, *rest):
        o_ref = rest[nc]
        finish = _ride_along_gather(rest[:nc] + rest[nc + 1:], nc, 2)
        i = pl.program_id(1)
        n_steps = (i + 2) // 2
        later, _, first = _sb_consts()
        qv = q_ref[...]
        zero = jnp.zeros_like(qv)
        q2 = jnp.concatenate([jnp.where(first, qv, zero), jnp.where(first, zero, qv)], axis=0)

        def window(w, carry, diagonal):
            start, mask = _sb_window(i, w)
            if diagonal:
                mask = (diag_ref[0], diag_ref[1])
            kw = k_ref[pl.ds(start, SB_KEYS), :]
            vw = v_ref[pl.ds(start, SB_KEYS), :]
            c_l, acc = carry
            a, c_l = _sb_weights(_sb_scores(_mm_nt(q2, kw), mask), later, c_l)
            return c_l, acc + _mm(a, vw)

        def step(state):
            w, _, carry = state
            carry = window(w, carry, False)
            return w + 1, _sb_alive(carry), carry

        carry = window(0, (jnp.zeros((2 * QBLK, 128), F32),) * 2, True)
        _, _, carry = lax.while_loop(lambda s: (s[0] < n_steps) & (s[1] > 0), step,
                                     (jnp.int32(1), _sb_alive(carry), carry))
        o_ref[...] = jnp.where(first, carry[1][:QBLK], carry[1][QBLK:])
        finish()

    qblk = pl.BlockSpec((QBLK, 128), lambda h, i: (i, h))
    kblk = pl.BlockSpec((T, 128), lambda h, i: (0, h))
    outs = _call(
        body, name=name, grid=(D // 128, T // QBLK), out_shape=[jax.ShapeDtypeStruct((T, D), F32)] + lands,
        in_specs=[_SB_DIAG_SPEC, qblk, kblk, kblk] + [ANY_SPEC] * nc, out_specs=[qblk] + [ANY_SPEC] * nc,
        scratch_shapes=sems, compiler_params=_params(32, 2),
    )(_sb_diagonal(), q, k, v, *cargo)
    return outs[0], outs[1:]


def _sb_bwd(q, k, v, o, do, cargo, name):
    T, D = q.shape
    nc = len(cargo)
    lands, sems = _exchange_shapes(cargo, True)

    def body(diag_ref, q_ref, k_ref, v_ref, o_ref, do_ref, *rest):
        dq_ref, dk_ref, dv_ref = rest[nc:nc + 3]
        finish = _ride_along(rest[:nc] + rest[nc + 3:], nc, True)
        i = pl.program_id(1)

        @pl.when(i == 0)
        def _():
            dk_ref[...] = jnp.zeros_like(dk_ref)
            dv_ref[...] = jnp.zeros_like(dv_ref)

        n_steps = (i + 2) // 2
        later, later_eq, first = _sb_consts()
        qv = q_ref[...]
        dob = do_ref[...].astype(MXU)
        zero = jnp.zeros_like(qv)
        q2 = jnp.concatenate([jnp.where(first, qv, zero), jnp.where(first, zero, qv)], axis=0)
        do2 = jnp.concatenate([jnp.where(first, dob, zero), jnp.where(first, zero, dob)], axis=0)
        prod = o_ref[...] * dob.astype(F32)
        prod2 = jnp.concatenate([jnp.where(first, prod, 0.0), jnp.where(first, 0.0, prod)], axis=0)
        total = _split_dot(prod2, jnp.ones((128, 128), MXU), 3)
        total = jnp.concatenate([total, total], axis=1)

        def window(w, carry, diagonal):
            start, mask = _sb_window(i, w)
            if diagonal:
                mask = (diag_ref[0], diag_ref[1])
            kw = k_ref[pl.ds(start, SB_KEYS), :]
            vw = v_ref[pl.ds(start, SB_KEYS), :]
            c_l, c_e, dq = carry
            log_sig, parts = _sb_scores(_mm_nt(q2, kw), mask)
            a, c_l = _sb_weights((log_sig, parts), later, c_l)
            ab = a.astype(MXU)
            e = ab.astype(F32) * _mm_nt(do2, vw)
            from_here, c_e = _sb_suffix(_sb_terms(e, 2), later_eq, c_e)
            sig = jnp.exp(log_sig)
            dzb = (e * (1.0 - sig) - sig * (total - from_here)).astype(MXU)
            dk_ref[pl.ds(start, SB_KEYS), :] += _mm_tn(dzb, q2)
            dv_ref[pl.ds(start, SB_KEYS), :] += _mm_tn(ab, do2)
            return c_l, c_e, dq + _mm(dzb, kw)

        def step(state):
            w, _, carry = state
            carry = window(w, carry, False)
            return w + 1, _sb_alive(carry), carry

        carry = window(0, (jnp.zeros((2 * QBLK, 128), F32),) * 3, True)
        _, _, carry = lax.while_loop(lambda s: (s[0] < n_steps) & (s[1] > 0), step,
                                     (jnp.int32(1), _sb_alive(carry), carry))
        dq_ref[...] = jnp.where(first, carry[2][:QBLK], carry[2][QBLK:]) * SCALE
        finish()

    qblk = pl.BlockSpec((QBLK, 128), lambda h, i: (i, h))
    kblk = pl.BlockSpec((T, 128), lambda h, i: (0, h))
    full = jax.ShapeDtypeStruct((T, D), F32)
    outs = _call(
        body, name=name, grid=(D // 128, T // QBLK), out_shape=[full, full, full] + lands,
        in_specs=[_SB_DIAG_SPEC, qblk, kblk, kblk, qblk, qblk] + [ANY_SPEC] * nc,
        out_specs=[qblk, kblk, kblk] + [ANY_SPEC] * nc, scratch_shapes=sems, compiler_params=_params(32, 2),
    )(_sb_diagonal(), q, k, v, o, do, *cargo)
    return outs[0], outs[1], outs[2], outs[3:]


def _proj_res(x, a, w, name):
    T, D = x.shape
    tm = min(TOKEN_TILE, T)

    def body(x_ref, a_ref, w_ref, o_ref):
        o_ref[...] = x_ref[...] + _mm(a_ref[...], w_ref[...])

    return _call(
        body, name=name, grid=(T // tm,), out_shape=jax.ShapeDtypeStruct((T, D), F32),
        in_specs=[_tile(tm, D), _tile(tm, a.shape[1]), _whole(w.shape)], out_specs=_tile(tm, D),
        compiler_params=_params(32),
    )(x, a, w)


def _proj_nt(g, w, name):
    T = g.shape[0]
    K = w.shape[0]
    tm = min(TOKEN_TILE, T)

    def body(g_ref, w_ref, o_ref):
        o_ref[...] = _mm_nt(g_ref[...], w_ref[...])

    return _call(
        body, name=name, grid=(T // tm,), out_shape=jax.ShapeDtypeStruct((T, K), F32),
        in_specs=[_tile(tm, g.shape[1]), _whole(w.shape)], out_specs=_tile(tm, K),
        compiler_params=_params(32),
    )(g, w)


def _loss_grad(y, tgt, name):
    T, D = y.shape
    tm = min(TOKEN_TILE, T)

    def body(y_ref, t_ref, dy_ref, loss_ref):
        @pl.when(pl.program_id(0) == 0)
        def _():
            loss_ref[...] = jnp.zeros_like(loss_ref)
        diff = y_ref[...] - t_ref[...]
        dy_ref[...] = diff * (1.0 / D)
        rows = jnp.sum(diff * diff, axis=1, keepdims=True) * (1.0 / D)
        loss_ref[...] += 0.5 * jnp.sum(rows, axis=0, keepdims=True)

    return _call(
        body, name=name, grid=(T // tm,),
        out_shape=[jax.ShapeDtypeStruct((T, D), F32), jax.ShapeDtypeStruct((1, 1), F32)],
        in_specs=[_tile(tm, D), _tile(tm, D)], out_specs=[_tile(tm, D), _acc((1, 1))],
        compiler_params=_params(32),
    )(y, tgt)


def _ple_bwd(dx, x, gate, pp, ln, w_g, name):
    T, D = x.shape
    tm = min(TOKEN_TILE, T)

    def body(dx_ref, x_ref, gate_ref, pp_ref, ln_ref, wg_ref, dxo_ref, dpp_ref, dgp_ref, dln_ref):
        @pl.when(pl.program_id(0) == 0)
        def _():
            dln_ref[...] = jnp.zeros_like(dln_ref)
        dxv = dx_ref[...]
        gate = gate_ref[...]
        _, xhat, rstd = _rms(x_ref[...], ln_ref[...])
        dpp_ref[...] = (dxv * gate).astype(MXU)
        dgp = (dxv * pp_ref[...] * gate * (1.0 - gate)).astype(MXU)
        dgp_ref[...] = dgp
        dxn, dln = _rms_bwd(_mm_nt(dgp, wg_ref[...]), xhat, rstd, ln_ref[...])
        dln_ref[...] += dln
        dxo_ref[...] = dxn + dxv

    return _call(
        body, name=name, grid=(T // tm,),
        out_shape=[jax.ShapeDtypeStruct((T, D), F32), jax.ShapeDtypeStruct((T, D), MXU),
                   jax.ShapeDtypeStruct((T, D), MXU), jax.ShapeDtypeStruct(ln.shape, F32)],
        in_specs=[_tile(tm, D)] * 4 + [_whole(ln.shape), _whole(w_g.shape)],
        out_specs=[_tile(tm, D), _tile(tm, D), _tile(tm, D), _acc(ln.shape)],
        compiler_params=_params(32),
    )(dx, x, gate, pp, ln, w_g)


def _mlp_bwd(dx, x, pre, ln, w_up, w_down, cargo, name):
    T, D = x.shape
    tm = min(TOKEN_TILE, T)
    nf = w_up.shape[2]
    F = nf * NDEV
    nc = len(cargo)
    lands, sems = _exchange_shapes(cargo, True)

    def body(dx_ref, x_ref, pre_ref, ln_ref, wup_ref, wdown_ref, *rest):
        dxo_ref, dpre_ref, s_ref, dln_ref = rest[nc:nc + 4]
        finish = _ride_along(rest[:nc] + rest[nc + 4:], nc, True, rank=1)

        @pl.when(pl.program_id(0) == 0)
        def _():
            dln_ref[...] = jnp.zeros_like(dln_ref)
        dxv = dx_ref[...]
        _, xhat, rstd = _rms(x_ref[...], ln_ref[...])
        a = jnp.maximum(pre_ref[...], 0.0)
        s_ref[...] = (a * a).astype(MXU)
        dpre_ref[...] = (_mm_nt(dxv, wdown_ref[...]) * (2.0 * a)).astype(MXU)
        dh = _mm_nt(dpre_ref[:, :nf], wup_ref[0])
        for j in range(1, NDEV):
            dh += _mm_nt(dpre_ref[:, j * nf:(j + 1) * nf], wup_ref[j])
        dxn, dln = _rms_bwd(dh, xhat, rstd, ln_ref[...])
        dln_ref[...] += dln
        dxo_ref[...] = dxn + dxv
        finish()

    outs = _call(
        body, name=name, grid=(T // tm,),
        out_shape=[jax.ShapeDtypeStruct((T, D), F32), jax.ShapeDtypeStruct((T, F), MXU),
                   jax.ShapeDtypeStruct((T, F), MXU), jax.ShapeDtypeStruct(ln.shape, F32)] + lands,
        in_specs=[_tile(tm, D), _tile(tm, D), _tile(tm, F), _whole(ln.shape), _whole(w_up.shape),
                  _whole(w_down.shape)] + [ANY_SPEC] * nc,
        out_specs=[_tile(tm, D), _tile(tm, F), _tile(tm, F), _acc(ln.shape)] + [ANY_SPEC] * nc,
        scratch_shapes=sems, compiler_params=_params(56),
    )(dx, x, pre, ln, w_up, w_down, *cargo)
    return outs[0], outs[1], outs[2], outs[3], outs[4:]


def _qkv_bwd(dx, x, dq, dk, dv, q_pre, k_pre, ln_q, ln_kv, g_q, g_k, w_q, w_kv, cargo, name):
    T, D = x.shape
    tm = min(TOKEN_TILE, T)
    nk = w_kv.shape[2]
    n_tiles = T // tm
    nc = len(cargo)
    lands, sems = _exchange_shapes(cargo, True)

    def body(dx_ref, x_ref, dq_ref, dk_ref, dv_ref, qpre_ref, kpre_ref, lnq_ref, lnkv_ref, gq_ref, gk_ref,
             wq_ref, wkv_ref, *rest):
        dxo_ref, dqp_ref, dkv_ref, dlnq_ref, dlnkv_ref, dgq_ref, dgk_ref = rest[nc:nc + 7]
        gq_acc, gk_acc = rest[2 * nc + 7:2 * nc + 9]
        finish = _ride_along(rest[:nc] + rest[nc + 7:2 * nc + 7] + rest[2 * nc + 9:], nc, True, rank=1)
        i = pl.program_id(0)

        @pl.when(i == 0)
        def _():
            dlnq_ref[...] = jnp.zeros_like(dlnq_ref)
            dlnkv_ref[...] = jnp.zeros_like(dlnkv_ref)
            gq_acc[...] = jnp.zeros_like(gq_acc)
            gk_acc[...] = jnp.zeros_like(gk_acc)

        ones = _head_ones()
        for b in range(D // 128):
            cols = slice(b * 128, (b + 1) * 128)
            _, xh, rs = _head_rms(qpre_ref[:, cols], gq_ref[:, cols], ones)
            d, dg = _head_rms_bwd(dq_ref[:, cols], xh, rs, gq_ref[:, cols], ones)
            dqp_ref[:, cols] = d.astype(MXU)
            gq_acc[:, cols] += dg
            _, xh, rs = _head_rms(kpre_ref[:, cols], gk_ref[:, cols], ones)
            d, dg = _head_rms_bwd(dk_ref[:, cols], xh, rs, gk_ref[:, cols], ones)
            dkv_ref[:, cols] = d.astype(MXU)
            gk_acc[:, cols] += dg
        dkv_ref[:, D:] = dv_ref[...].astype(MXU)

        _, xhat, rstd = _rms(x_ref[...], lnq_ref[...])
        dhq = _mm_nt(dqp_ref[...], wq_ref[...])
        dhkv = _mm_nt(dkv_ref[:, :nk], wkv_ref[0])
        for j in range(1, NDEV):
            dhkv += _mm_nt(dkv_ref[:, j * nk:(j + 1) * nk], wkv_ref[j])
        dxq, dlnq = _rms_bwd(dhq, xhat, rstd, lnq_ref[...])
        dxkv, dlnkv = _rms_bwd(dhkv, xhat, rstd, lnkv_ref[...])
        dlnq_ref[...] += dlnq
        dlnkv_ref[...] += dlnkv
        dxo_ref[...] = dx_ref[...] + dxq + dxkv

        @pl.when(i == n_tiles - 1)
        def _():
            row = lax.broadcasted_iota(jnp.int32, (D, 128), 0)
            col = lax.broadcasted_iota(jnp.int32, (D, 128), 1)
            fold = (jnp.bitwise_and(row, HEAD_DIM - 1) == col).astype(MXU)
            dgq_ref[...] = _split_dot(jnp.broadcast_to(gq_acc[...], (8, D)), fold, 3)
            dgk_ref[...] = _split_dot(jnp.broadcast_to(gk_acc[...], (8, D)), fold, 3)

        finish()

    small = jax.ShapeDtypeStruct((8, 128), F32)
    outs = _call(
        body, name=name, grid=(n_tiles,),
        out_shape=[jax.ShapeDtypeStruct((T, D), F32), jax.ShapeDtypeStruct((T, D), MXU),
                   jax.ShapeDtypeStruct((T, 2 * D), MXU), jax.ShapeDtypeStruct(ln_q.shape, F32),
                   jax.ShapeDtypeStruct(ln_kv.shape, F32), small, small] + lands,
        in_specs=[_tile(tm, D)] * 7 + [_whole(ln_q.shape), _whole(ln_kv.shape), _whole(g_q.shape),
                                       _whole(g_k.shape), _whole(w_q.shape), _whole(w_kv.shape)] + [ANY_SPEC] * nc,
        out_specs=[_tile(tm, D), _tile(tm, D), _tile(tm, 2 * D), _acc(ln_q.shape), _acc(ln_kv.shape),
                   _acc((8, 128)), _acc((8, 128))] + [ANY_SPEC] * nc,
        scratch_shapes=[pltpu.VMEM((1, D), F32), pltpu.VMEM((1, D), F32)] + sems,
        compiler_params=_params(48),
    )(dx, x, dq, dk, dv, q_pre, k_pre, ln_q, ln_kv, g_q, g_k, w_q, w_kv, *cargo)
    return outs[:7], outs[7:]


def _sgu_bwd(dx, x, z, ln, w_in, g_v, ws, wsT, bT, w_out, cargo, name):
    T, D = x.shape
    tm = min(TOKEN_TILE, T)
    nw = w_in.shape[2]
    nc = len(cargo)
    lands, sems = _exchange_shapes(cargo, True)

    def body(dx_ref, x_ref, z_ref, ln_ref, win_ref, gv_ref, ws_ref, wsT_ref, bT_ref, wout_ref, *rest):
        dxo_ref, dz_ref, dws_ref, dbT_ref, dln_ref, dgv_ref = rest[nc:nc + 6]
        mix_ref, dvn_ref = rest[2 * nc + 6:2 * nc + 8]
        finish = _ride_along(rest[:nc] + rest[nc + 6:2 * nc + 6] + rest[2 * nc + 8:], nc, True, rank=1)

        @pl.when(pl.program_id(0) == 0)
        def _():
            dws_ref[...] = jnp.zeros_like(dws_ref)
            dbT_ref[...] = jnp.zeros_like(dbT_ref)
            dln_ref[...] = jnp.zeros_like(dln_ref)
            dgv_ref[...] = jnp.zeros_like(dgv_ref)
        dxv = dx_ref[...]
        _, xhat, rstd = _rms(x_ref[...], ln_ref[...])
        u, du = _gelu_and_grad(z_ref[:, :D])
        gv, dgv = _gelu_and_grad(z_ref[:, D:])
        vn, vhat, rstd_v = _rms(gv, gv_ref[...])
        vnb = vn.astype(MXU)
        _spatial_mix(vnb, ws_ref, bT_ref, mix_ref, tm)
        dy = _mm_nt(dxv, wout_ref[...])
        d_u = dy * mix_ref[...]
        d_mix = dy * u
        dmb = d_mix.astype(MXU)
        tri, triT = _tril_mask()
        for g in range(GROUPS):
            wmT = jnp.where(triT, wsT_ref[g], 0.0).astype(MXU)
            cols = slice(g * CHUNK, (g + 1) * CHUNK)
            for ch in range(tm // CHUNK):
                rows = slice(ch * CHUNK, (ch + 1) * CHUNK)
                dm = dmb[rows, cols]
                dws_ref[g] += jnp.where(tri, _mm_nt(dm, vnb[rows, cols]), 0.0)
                dbT_ref[:, g:g + 1] += jnp.sum(d_mix[rows, cols], axis=1, keepdims=True)
                dvn_ref[rows, cols] = _mm(wmT, dm)
        d_gv, dg = _rms_bwd(dvn_ref[...], vhat, rstd_v, gv_ref[...])
        dgv_ref[...] += dg
        dz_ref[:, :D] = (d_u * du).astype(MXU)
        dz_ref[:, D:] = (d_gv * dgv).astype(MXU)
        dh = _mm_nt(dz_ref[:, :nw], win_ref[0])
        for j in range(1, NDEV):
            dh += _mm_nt(dz_ref[:, j * nw:(j + 1) * nw], win_ref[j])
        dxn, dln = _rms_bwd(dh, xhat, rstd, ln_ref[...])
        dln_ref[...] += dln
        dxo_ref[...] = dxn + dxv
        finish()

    outs = _call(
        body, name=name, grid=(T // tm,),
        out_shape=[jax.ShapeDtypeStruct((T, D), F32), jax.ShapeDtypeStruct((T, 2 * D), MXU),
                   jax.ShapeDtypeStruct(ws.shape, F32), jax.ShapeDtypeStruct(bT.shape, F32),
                   jax.ShapeDtypeStruct(ln.shape, F32), jax.ShapeDtypeStruct(g_v.shape, F32)] + lands,
        in_specs=[_tile(tm, D), _tile(tm, D), _tile(tm, 2 * D), _whole(ln.shape), _whole(w_in.shape),
                  _whole(g_v.shape), _whole(ws.shape), _whole(wsT.shape), _whole(bT.shape), _whole(w_out.shape)]
        + [ANY_SPEC] * nc,
        out_specs=[_tile(tm, D), _tile(tm, 2 * D), _acc(ws.shape), _acc(bT.shape), _acc(ln.shape),
                   _acc(g_v.shape)] + [ANY_SPEC] * nc,
        scratch_shapes=[pltpu.VMEM((tm, D), F32), pltpu.VMEM((tm, D), F32)] + sems,
        compiler_params=_params(48),
    )(dx, x, z, ln, w_in, g_v, ws, wsT, bT, w_out, *cargo)
    return outs[:6], outs[6:]


def _wgrad_rows(a, g, name):
    T, K = a.shape
    N = g.shape[1]
    kb = K // NDEV

    def body(a_ref, g_ref, o_ref):
        o_ref[...] = _mm_tn(a_ref[...], g_ref[...]).astype(COMM)

    return _call(
        body, name=name, grid=(NDEV,), out_shape=jax.ShapeDtypeStruct((K, N), COMM),
        in_specs=[pl.BlockSpec((T, kb), lambda j: (0, j)), _whole(g.shape)],
        out_specs=pl.BlockSpec((kb, N), lambda j: (j, 0)),
        compiler_params=_params(40),
    )(a, g).reshape(NDEV, kb, N)


def _wgrad_cols(a, g, name):
    T, K = a.shape
    N = g.shape[1]
    nb = N // NDEV

    def body(a_ref, g_ref, o_ref):
        o_ref[...] = _mm_tn(a_ref[...], g_ref[...]).astype(COMM)

    return _call(
        body, name=name, grid=(NDEV,), out_shape=jax.ShapeDtypeStruct((NDEV, K, nb), COMM),
        in_specs=[_whole(a.shape), pl.BlockSpec((T, nb), lambda j: (0, j))],
        out_specs=pl.BlockSpec((None, K, nb), lambda j: (j, 0, 0)),
        compiler_params=_params(40),
    )(a, g)


def _adamw_rows(R, C):
    tr = math.gcd(R, max(8, (128 * 1024) // C))
    return R if tr < 64 else tr


def _adamw_update(w_ref, m_ref, v_ref, s_ref, g_ref, d_ref, mo_ref, vo_ref):
    g = s_ref[0].astype(F32)
    for j in range(1, s_ref.shape[0]):
        g = g + s_ref[j].astype(F32)
    mn = ADAM_B1 * m_ref[...] + (1.0 - ADAM_B1) * g
    vn = ADAM_B2 * v_ref[...] + (1.0 - ADAM_B2) * (g * g)
    g_ref[...] = g
    mo_ref[...] = mn
    vo_ref[...] = vn
    m_hat = mn / (1.0 - ADAM_B1 ** ADAM_STEP)
    v_hat = vn / (1.0 - ADAM_B2 ** ADAM_STEP)
    d_ref[...] = -ADAM_LR * (m_hat / (jnp.sqrt(v_hat) + ADAM_EPS) + ADAM_WD * w_ref[...])


def _adamw_layers(w, m, v, slots0, slots1, name):
    _, R, C = w.shape
    tr = _adamw_rows(R, C)
    last = R // tr - 1

    def body(w_ref, m_ref, v_ref, s0_ref, s1_ref, *outs):
        @pl.when(pl.program_id(0) == 0)
        def _():
            _adamw_update(w_ref, m_ref, v_ref, s0_ref, *outs)

        @pl.when(pl.program_id(0) == 1)
        def _():
            _adamw_update(w_ref, m_ref, v_ref, s1_ref, *outs)

    blk = pl.BlockSpec((None, tr, C), lambda l, i: (l, i, 0))
    s0_blk = pl.BlockSpec((slots0.shape[0], tr, C), lambda l, i: (0, jnp.where(l == 0, i, last), 0))
    s1_blk = pl.BlockSpec((slots1.shape[0], tr, C), lambda l, i: (0, jnp.where(l == 1, i, 0), 0))
    out = jax.ShapeDtypeStruct(w.shape, F32)
    return _call(
        body, name=name, grid=(2, R // tr), out_shape=[out, out, out, out],
        in_specs=[blk, blk, blk, s0_blk, s1_blk], out_specs=[blk] * 4, compiler_params=_params(32, 2),
    )(w, m, v, slots0, slots1)


def _adamw(w, m, v, slots, name):
    R, C = w.shape
    n = slots.shape[0]
    tr = _adamw_rows(R, C)

    def body(*refs):
        _adamw_update(*refs)

    blk = pl.BlockSpec((tr, C), lambda i: (i, 0))
    out = jax.ShapeDtypeStruct((R, C), F32)
    return _call(
        body, name=name, grid=(R // tr,), out_shape=[out, out, out, out],
        in_specs=[blk, blk, blk, pl.BlockSpec((n, tr, C), lambda i: (0, i, 0))], out_specs=[blk] * 4,
        compiler_params=_params(32),
    )(w, m, v, slots)


def _rows128(a):
    flat = a.reshape(-1)
    rows = -(-flat.shape[0] // 1024) * 8
    flat = jnp.pad(flat, (0, rows * 128 - flat.shape[0]))
    return flat.reshape(rows, 128)


def kernel(x, p, ln_mix_a, w_in_a, g_v_a, w_spatial, b_spatial, w_out_a, ln_kv, w_kv, g_k, ln_mix_b, w_q, g_q, w_out_b, ln_mlp, w_up, w_down, ln_ple, w_ple_gate, w_ple_proj, loss_target, m_ln_mix_a, m_w_in_a, m_g_v_a, m_w_spatial, m_b_spatial, m_w_out_a, m_ln_kv, m_w_kv, m_g_k, m_ln_mix_b, m_w_q, m_g_q, m_w_out_b, m_ln_mlp, m_w_up, m_w_down, m_ln_ple, m_w_ple_gate, m_w_ple_proj, v_ln_mix_a, v_w_in_a, v_g_v_a, v_w_spatial, v_b_spatial, v_w_out_a, v_ln_kv, v_w_kv, v_g_k, v_ln_mix_b, v_w_q, v_g_q, v_w_out_b, v_ln_mlp, v_w_up, v_w_down, v_ln_ple, v_w_ple_gate, v_w_ple_proj):
    me = 4 * lax.axis_index("x") + 2 * lax.axis_index("y") + lax.axis_index("c")
    D = x.shape[2]
    x0, tgt = x[0], loss_target[0]
    n_layers = w_up.shape[0]

    c = lambda w: w.astype(COMM)
    first = [c(w_in_a[0]), c(w_out_a[0]), ln_mix_a, g_v_a, c(w_up[0]), c(w_down[0])]
    after_mlp = [c(w_ple_gate[0]), c(w_ple_proj[0]), c(w_q[0]), c(w_kv)]
    second_small = [c(w_out_b[0]), c(w_ple_gate[1]), c(w_ple_proj[1])]
    second_big = [c(w_up[1]), c(w_down[1])]
    W_in, W_out_a, ln_a, gv_a, W_up0, W_down0 = _gather_two_level(first, "gather_first")
    W_out_a, ln_a, gv_a = W_out_a.reshape(-1, D), ln_a.reshape(1, D), gv_a.reshape(1, D)
    W_down0 = W_down0.reshape(-1, D)
    ws = w_spatial[0]
    wsT = jnp.swapaxes(ws, 1, 2)
    bT = b_spatial[0].T
    ln_kv2, ln_b = ln_kv.reshape(1, D), ln_mix_b
    gk2 = jnp.tile(g_k.reshape(1, HEAD_DIM), (1, D // HEAD_DIM))
    gq2 = jnp.tile(g_q, (1, D // HEAD_DIM))
    ln_m = [ln_mlp[l:l + 1] for l in range(n_layers)]
    ln_p = [ln_ple[l:l + 1] for l in range(n_layers)]

    x1, z, h_a, y_a = _sgu_fwd(x0, ln_a, W_in, gv_a, ws, bT, W_out_a, "sgu_fwd")
    x2, pre0, hm0, (W_g0, W_pp0, W_q, W_kv) = _mlp_fwd(x1, ln_m[0], W_up0, W_down0, after_mlp, "mlp_fwd0")
    W_g0, W_q = W_g0.reshape(-1, D), W_q.reshape(-1, D)
    x3, gate0, pp0, hp0 = _ple_fwd(x2, p[0, 0], ln_p[0], W_g0, W_pp0, "ple_fwd0")
    qn, kn, vn, q_pre, k_pre, h_q, h_kv = _qkv_fwd(x3, ln_b, ln_kv2, gq2, gk2, W_q, W_kv, "qkv_fwd")
    o2d, (W_out_b, W_g1, W_pp1, W_up1, W_down1) = _sb_fwd(qn, kn, vn, second_small + second_big, "sb_fwd")
    W_out_b, W_down1, W_g1 = W_out_b.reshape(-1, D), W_down1.reshape(-1, D), W_g1.reshape(-1, D)
    x4 = _proj_res(x3, o2d, W_out_b, "attn_out")
    x5, pre1, hm1, _ = _mlp_fwd(x4, ln_m[1], W_up1, W_down1, [], "mlp_fwd1")
    x6, gate1, pp1, hp1 = _ple_fwd(x5, p[1, 0], ln_p[1], W_g1, W_pp1, "ple_fwd1")
    dy, loss_part = _loss_grad(x6, tgt, "loss_grad")

    dx5, dpp1, dgp1, dlnp1 = _ple_bwd(dy, x5, gate1, pp1, ln_p[1], W_g1, "ple_bwd1")
    dx4, dpre1, s1, dlnm1, _ = _mlp_bwd(dx5, x4, pre1, ln_m[1], W_up1, W_down1, [], "mlp_bwd1")
    wg_big = [_wgrad_cols(hm1, dpre1, "wg_up1"), _wgrad_rows(s1, dx5, "wg_down1")]
    wg_small = [_wgrad_rows(hp1, dgp1, "wg_gate1"), _wgrad_cols(p[1, 0].astype(MXU), dpp1, "wg_proj1"),
                _wgrad_rows(o2d, dx4, "wg_out_b")]
    do2d = _proj_nt(dx4, W_out_b, "attn_out_bwd")
    dqn, dkn, dvn, (s_up1, s_down1, s_gate1, s_proj1, s_out_b) = _sb_bwd(qn, kn, vn, o2d, do2d, wg_big + wg_small,
                                                                         "sb_bwd")
    (dx3, dq_pre, dkv, dlnb, dlnkv, dgq, dgk), _ = _qkv_bwd(
        dx4, x3, dqn, dkn, dvn, q_pre, k_pre, ln_b, ln_kv2, gq2, gk2, W_q, W_kv, [], "qkv_bwd")
    dgq, dgk = dgq[:1, :HEAD_DIM], dgk[:1, :HEAD_DIM]
    dx2, dpp0, dgp0, dlnp0 = _ple_bwd(dx3, x2, gate0, pp0, ln_p[0], W_g0, "ple_bwd0")
    wg_kv = [_wgrad_cols(h_kv, dkv, "wg_kv")]
    dx1, dpre0, s0, dlnm0, (s_kv,) = _mlp_bwd(dx2, x1, pre0, ln_m[0], W_up0, W_down0, wg_kv, "mlp_bwd0")
    wg_ple = [_wgrad_rows(hp0, dgp0, "wg_gate0"), _wgrad_cols(p[0, 0].astype(MXU), dpp0, "wg_proj0"),
              _wgrad_rows(h_q, dq_pre, "wg_q")]
    (dx0, dz, dws, dbT, dlna, dgva), (s_gate0, s_proj0, s_q) = _sgu_bwd(
        dx1, x0, z, ln_a, W_in, gv_a, ws, wsT, bT, W_out_a, wg_ple, "sgu_bwd")
    wg_first = [_wgrad_cols(hm0, dpre0, "wg_up0"), _wgrad_rows(s0, dx2, "wg_down0"), _wgrad_cols(h_a, dz, "wg_in_a"),
                _wgrad_rows(y_a, dx1, "wg_out_a")]

    small = [("b_spatial", dbT.T[None], b_spatial, m_b_spatial, v_b_spatial),
             ("ln_kv", dlnkv.reshape(-1), ln_kv, m_ln_kv, v_ln_kv),
             ("g_k", dgk.reshape(-1), g_k, m_g_k, v_g_k),
             ("ln_mix_b", dlnb, ln_mix_b, m_ln_mix_b, v_ln_mix_b),
             ("g_q", dgq, g_q, m_g_q, v_g_q),
             ("ln_mlp", jnp.concatenate([dlnm0, dlnm1]), ln_mlp, m_ln_mlp, v_ln_mlp),
             ("ln_ple", jnp.concatenate([dlnp0, dlnp1]), ln_ple, m_ln_ple, v_ln_ple)]
    sharded_vec = [("ln_mix_a", dlna, ln_mix_a, m_ln_mix_a, v_ln_mix_a),
                   ("g_v_a", dgva, g_v_a, m_g_v_a, v_g_v_a)]
    packs = [[], [], [], []]
    for _, g, w, m, v in small:
        for lst, a in zip(packs, (g, w, m, v)):
            lst.append(_rows128(a))
    for _, g, w, m, v in sharded_vec:
        packs[0].append(g.reshape(NDEV, -1))
        for lst, a in zip(packs[1:], (w, m, v)):
            lst.append(jnp.broadcast_to(a, (NDEV, a.shape[1])))
    packs[0].append(_rows128(loss_part))
    for lst in packs[1:]:
        lst.append(jnp.zeros((8, 128), F32))
    g_pack, w_pack, m_pack, v_pack = (jnp.concatenate(lst) for lst in packs)
    g_pack8 = jnp.broadcast_to(g_pack[None], (NDEV,) + g_pack.shape)
    dws8 = jnp.broadcast_to(dws.reshape(1, -1, 128).astype(COMM), (NDEV, dws.size // 128, 128))
    by_chip = [a.reshape((4, 2) + a.shape[1:]) for a in wg_first]
    from_sibling, (g_all, s_ws) = _scatter_pair(by_chip, [g_pack8, dws8], "scatter_pair")
    my_core = lax.axis_index("c")
    chip_sums = [_pair_sum(lax.dynamic_index_in_dim(a, my_core, 1, keepdims=False), o, f"pair_sum{j}")
                 for j, (a, o) in enumerate(zip(by_chip, from_sibling))]
    s_up0, s_down0, s_in_a, s_out_a = _scatter_chips(chip_sums, "scatter_chips")

    def upd(w, m, v, s, name):
        shape = w.shape
        outs = _adamw(w.reshape(-1, shape[-1]), m.reshape(-1, shape[-1]), v.reshape(-1, shape[-1]), s, name)
        return [o.reshape(shape) for o in outs]

    res = {}
    res["w_out_b"] = upd(w_out_b, m_w_out_b, v_w_out_b, s_out_b, "adam_out_b")
    res["w_q"] = upd(w_q, m_w_q, v_w_q, s_q, "adam_q")
    res["w_kv"] = upd(w_kv, m_w_kv, v_w_kv, s_kv, "adam_kv")
    res["w_in_a"] = upd(w_in_a, m_w_in_a, v_w_in_a, s_in_a, "adam_in_a")
    res["w_out_a"] = upd(w_out_a, m_w_out_a, v_w_out_a, s_out_a, "adam_out_a")
    res["w_up"] = _adamw_layers(w_up, m_w_up, v_w_up, s_up0, s_up1, "adam_up")
    res["w_down"] = _adamw_layers(w_down, m_w_down, v_w_down, s_down0, s_down1, "adam_down")
    res["w_ple_gate"] = _adamw_layers(w_ple_gate, m_w_ple_gate, v_w_ple_gate, s_gate0, s_gate1, "adam_gate")
    res["w_ple_proj"] = _adamw_layers(w_ple_proj, m_w_ple_proj, v_w_ple_proj, s_proj0, s_proj1, "adam_proj")

    res["w_spatial"] = [o.reshape(w_spatial.shape) for o in _adamw(
        w_spatial.reshape(-1, 128), m_w_spatial.reshape(-1, 128), v_w_spatial.reshape(-1, 128), s_ws, "adam_spatial")]
    outs = _adamw(w_pack, m_pack, v_pack, g_all, "adam_small")
    loss = outs[0][-8, 0]
    row = 0
    for nm, g, w, m, v in small:
        nrows = _rows128(w).shape[0]
        res[nm] = [o[row:row + nrows].reshape(-1)[:w.size].reshape(w.shape) for o in outs]
        row += nrows
    for nm, g, w, m, v in sharded_vec:
        res[nm] = [lax.dynamic_slice_in_dim(o[row:row + NDEV], me, 1, axis=0) for o in outs]
        row += NDEV

    names = ["ln_mix_a", "w_in_a", "g_v_a", "w_spatial", "b_spatial", "w_out_a", "ln_kv", "w_kv", "g_k", "ln_mix_b",
             "w_q", "g_q", "w_out_b", "ln_mlp", "w_up", "w_down", "ln_ple", "w_ple_gate", "w_ple_proj"]
    out = [loss, dx0[None]]
    for t in range(4):
        out += [res[nm][t] for nm in names]
    return tuple(out)
```

```python
import math

import jax
import jax.numpy as jnp
from jax import lax
from jax.experimental import pallas as pl
from jax.experimental.pallas import tpu as pltpu

F32 = jnp.float32
MXU = jnp.bfloat16
COMM = jnp.bfloat16
EPS = 1e-6
NDEV = 8
HEAD_DIM = 64
CHUNK = 128
GROUPS = 8
QBLK = 128
SCALE = HEAD_DIM ** -0.5
TOKEN_TILE = 256
ADAM_LR = 0.001
ADAM_B1 = 0.9
ADAM_B2 = 0.999
ADAM_EPS = 1e-08
ADAM_WD = 0.01
ADAM_STEP = 10
MESH = pl.DeviceIdType.MESH


def _call(body, **kw):
    return pl.pallas_call(body, **kw)


def _params(vmem_mb, n_axes=1):
    return pltpu.CompilerParams(dimension_semantics=("arbitrary",) * n_axes,
                                vmem_limit_bytes=vmem_mb << 20)


def _tile(tm, n):
    return pl.BlockSpec((tm, n), lambda i: (i, 0))


def _whole(shape):
    zeros = (0,) * len(shape)
    return pl.BlockSpec(shape, lambda i: zeros, pipeline_mode=pl.Buffered(1))


def _acc(shape):
    zeros = (0,) * len(shape)
    return pl.BlockSpec(shape, lambda i: zeros)


def _mm(a, b):
    return jnp.dot(a.astype(MXU), b.astype(MXU), preferred_element_type=F32)


def _mm_nt(a, b):
    return lax.dot_general(a.astype(MXU), b.astype(MXU), (((1,), (1,)), ((), ())),
                           preferred_element_type=F32)


def _mm_tn(a, b):
    return lax.dot_general(a.astype(MXU), b.astype(MXU), (((0,), (0,)), ((), ())),
                           preferred_element_type=F32)


def _split_dot(x, ones, terms=2):
    out = None
    for _ in range(terms):
        part = x.astype(MXU)
        x = x - part.astype(F32)
        d = jnp.dot(part, ones, preferred_element_type=F32)
        out = d if out is None else out + d
    return out


def _rms(x, g):
    rstd = lax.rsqrt(jnp.mean(x * x, axis=-1, keepdims=True) + EPS)
    xhat = x * rstd
    return xhat * g, xhat, rstd


def _rms_bwd(dh, xhat, rstd, g):
    dxh = dh * g
    dx = rstd * (dxh - xhat * jnp.mean(dxh * xhat, axis=-1, keepdims=True))
    dg = jnp.sum(dh * xhat, axis=0, keepdims=True)
    return dx, dg


_GELU_C = math.sqrt(2.0 / math.pi)


def _gelu(x):
    t = jnp.tanh(_GELU_C * (x + 0.044715 * (x * x * x)))
    return 0.5 * x * (1.0 + t)


def _gelu_and_grad(x):
    x2 = x * x
    t = jnp.tanh(_GELU_C * (x + 0.044715 * (x2 * x)))
    g = 0.5 * x * (1.0 + t)
    dg = 0.5 * (1.0 + t) + 0.5 * x * (1.0 - t * t) * (_GELU_C * (1.0 + 3.0 * 0.044715 * x2))
    return g, dg


def _softplus(z):
    return jnp.maximum(z, 0.0) + jnp.log(1.0 + jnp.exp(-jnp.abs(z)))


def _tril_mask():
    row = lax.broadcasted_iota(jnp.int32, (CHUNK, CHUNK), 0)
    col = lax.broadcasted_iota(jnp.int32, (CHUNK, CHUNK), 1)
    return row >= col, row <= col


ANY_SPEC = pl.BlockSpec(memory_space=pl.ANY)


def _my_index():
    return 4 * lax.axis_index("x") + 2 * lax.axis_index("y") + lax.axis_index("c")


def _exchange_copies(srcs, lands, send_sems, recv_sems, scatter, arriving):
    x, y, c = lax.axis_index("x"), lax.axis_index("y"), lax.axis_index("c")
    me = 4 * x + 2 * y + c
    out = []
    for a in range(len(srcs)):
        for k in range(NDEV - 1):
            bits = k + 1
            px = 1 - x if (bits >> 2) & 1 else x
            py = 1 - y if (bits >> 1) & 1 else y
            pc = 1 - c if bits & 1 else c
            peer = 4 * px + 2 * py + pc
            src = srcs[a].at[peer] if scatter else srcs[a]
            out.append(pltpu.make_async_remote_copy(
                src_ref=src, dst_ref=lands[a].at[peer if arriving else me],
                send_sem=send_sems.at[a * (NDEV - 1) + k], recv_sem=recv_sems.at[a * (NDEV - 1) + k],
                device_id=(px, py, pc), device_id_type=MESH))
    return out


def _exchange_shapes(arrs, scatter):
    n = len(arrs)
    if n == 0:
        return [], []
    lands = [jax.ShapeDtypeStruct(a.shape if scatter else (NDEV,) + a.shape, a.dtype) for a in arrs]
    sems = [pltpu.SemaphoreType.DMA((n * (NDEV - 1),)), pltpu.SemaphoreType.DMA((n * (NDEV - 1),)),
            pltpu.SemaphoreType.DMA((n,))]
    return lands, sems


def _exchange_start(srcs, lands, sems, scatter):
    send_sems, recv_sems, local_sems = sems
    me = _my_index()
    for a in range(len(srcs)):
        pltpu.make_async_copy(srcs[a].at[me] if scatter else srcs[a], lands[a].at[me], local_sems.at[a]).start()
    for send in _exchange_copies(srcs, lands, send_sems, recv_sems, scatter, False):
        send.start()


def _exchange_finish(srcs, lands, sems, scatter):
    send_sems, recv_sems, local_sems = sems
    me = _my_index()
    for arrive in _exchange_copies(srcs, lands, send_sems, recv_sems, scatter, True):
        arrive.wait_recv()
    for send in _exchange_copies(srcs, lands, send_sems, recv_sems, scatter, False):
        send.wait_send()
    for a in range(len(srcs)):
        pltpu.make_async_copy(srcs[a].at[me] if scatter else srcs[a], lands[a].at[me], local_sems.at[a]).wait()


def _gather_two_level(arrs, name):
    n = len(arrs)
    lands = [jax.ShapeDtypeStruct((NDEV,) + a.shape, a.dtype) for a in arrs]

    def body(*refs):
        srcs, outs = refs[:n], refs[n:2 * n]
        send_sems, recv_sems, local_sems = refs[2 * n:]
        x, y, c = lax.axis_index("x"), lax.axis_index("y"), lax.axis_index("c")
        me, sibling = (x, y, c), (x, y, 1 - c)
        chips = [(1 - x, y), (x, 1 - y), (1 - x, 1 - y)]

        def index(dev):
            return 4 * dev[0] + 2 * dev[1] + dev[2]

        def copy(a, k, block, to, src=None):
            dst = outs[a].at[index(block)]
            return pltpu.make_async_remote_copy(
                src_ref=dst if src is None else src, dst_ref=dst, send_sem=send_sems.at[a, k],
                recv_sem=recv_sems.at[a, k], device_id=to, device_id_type=MESH)

        mine, first, passed = [], [], []
        for a in range(n):
            cp = pltpu.make_async_copy(srcs[a], outs[a].at[index(me)], local_sems.at[a])
            cp.start()
            mine.append(cp)
            first.append(copy(a, 0, me, sibling, src=srcs[a]))
            first += [copy(a, 1 + j, me, (*chip, c), src=srcs[a]) for j, chip in enumerate(chips)]
        for cp in first:
            cp.start()
        for a in range(n):
            for j, chip in enumerate(chips):
                copy(a, 1 + j, (*chip, c), me).wait_recv()
                cp = copy(a, 4 + j, (*chip, c), sibling)
                cp.start()
                passed.append(cp)
        for a in range(n):
            copy(a, 0, sibling, me).wait_recv()
            for j, chip in enumerate(chips):
                copy(a, 4 + j, (*chip, 1 - c), me).wait_recv()
        for cp in first + passed:
            cp.wait_send()
        for cp in mine:
            cp.wait()

    return _call(body, name=name, out_shape=lands, in_specs=[ANY_SPEC] * n, out_specs=[ANY_SPEC] * n,
                 scratch_shapes=[pltpu.SemaphoreType.DMA((n, NDEV - 1)), pltpu.SemaphoreType.DMA((n, NDEV - 1)),
                                 pltpu.SemaphoreType.DMA((n,))])(*arrs)


def _scatter_pair(arrs, extra, name):
    n, ne = len(arrs), len(extra)
    lands = [jax.ShapeDtypeStruct((4,) + a.shape[2:], a.dtype) for a in arrs]
    extra_lands, extra_sems = _exchange_shapes(extra, True)

    def body(*refs):
        srcs, xsrc = refs[:n], refs[n:n + ne]
        outs, xout = refs[n + ne:2 * n + ne], refs[2 * n + ne:2 * (n + ne)]
        send_sems, recv_sems = refs[2 * (n + ne)], refs[2 * (n + ne) + 1]
        xsems = refs[2 * (n + ne) + 2:]
        x, y, c = lax.axis_index("x"), lax.axis_index("y"), lax.axis_index("c")
        _exchange_start(xsrc, xout, xsems, True)
        copies = [pltpu.make_async_remote_copy(
            src_ref=srcs[a].at[k, 1 - c], dst_ref=outs[a].at[k], send_sem=send_sems.at[a, k],
            recv_sem=recv_sems.at[a, k], device_id=(x, y, 1 - c), device_id_type=MESH)
            for a in range(n) for k in range(4)]
        for cp in copies:
            cp.start()
        for cp in copies:
            cp.wait()
        _exchange_finish(xsrc, xout, xsems, True)

    outs = _call(
        body, name=name, out_shape=lands + extra_lands, in_specs=[ANY_SPEC] * (n + ne), out_specs=[ANY_SPEC] * (n + ne),
        scratch_shapes=[pltpu.SemaphoreType.DMA((n, 4)), pltpu.SemaphoreType.DMA((n, 4))] + extra_sems,
    )(*arrs, *extra)
    return outs[:n], outs[n:]


def _pair_sum(own, other, name):
    _, R, C = own.shape
    tr = math.gcd(R, max(8, (128 * 1024) // C))

    def body(a_ref, b_ref, o_ref):
        o_ref[...] = (a_ref[...].astype(F32) + b_ref[...].astype(F32)).astype(COMM)

    blk = pl.BlockSpec((4, tr, C), lambda i: (0, i, 0))
    return _call(body, name=name, grid=(R // tr,), out_shape=jax.ShapeDtypeStruct(own.shape, COMM),
                 in_specs=[blk, blk], out_specs=blk, compiler_params=_params(32))(own, other)


def _scatter_chips(arrs, name):
    n = len(arrs)
    lands = [jax.ShapeDtypeStruct(a.shape, a.dtype) for a in arrs]

    def body(*refs):
        srcs, outs = refs[:n], refs[n:2 * n]
        send_sems, recv_sems, local_sems = refs[2 * n:]
        x, y, c = lax.axis_index("x"), lax.axis_index("y"), lax.axis_index("c")
        chip = 2 * x + y
        others = [(1 - x, y), (x, 1 - y), (1 - x, 1 - y)]
        local = [pltpu.make_async_copy(srcs[a].at[chip], outs[a].at[chip], local_sems.at[a]) for a in range(n)]
        for cp in local:
            cp.start()

        def copies(arriving):
            return [pltpu.make_async_remote_copy(
                src_ref=srcs[a].at[2 * px + py], dst_ref=outs[a].at[2 * px + py if arriving else chip],
                send_sem=send_sems.at[a, j], recv_sem=recv_sems.at[a, j], device_id=(px, py, c), device_id_type=MESH)
                for a in range(n) for j, (px, py) in enumerate(others)]

        for cp in copies(False):
            cp.start()
        for cp in copies(True):
            cp.wait_recv()
        for cp in copies(False):
            cp.wait_send()
        for cp in local:
            cp.wait()

    return _call(body, name=name, out_shape=lands, in_specs=[ANY_SPEC] * n, out_specs=[ANY_SPEC] * n,
                 scratch_shapes=[pltpu.SemaphoreType.DMA((n, 3)), pltpu.SemaphoreType.DMA((n, 3)),
                                 pltpu.SemaphoreType.DMA((n,))])(*arrs)


def _spatial_mix(vnb, ws_ref, bT_ref, mix_ref, tm):
    tri, _ = _tril_mask()
    for g in range(GROUPS):
        wm = jnp.where(tri, ws_ref[g], 0.0).astype(MXU)
        cols = slice(g * CHUNK, (g + 1) * CHUNK)
        for ch in range(tm // CHUNK):
            rows = slice(ch * CHUNK, (ch + 1) * CHUNK)
            mix_ref[rows, cols] = _mm(wm, vnb[rows, cols]) + bT_ref[:, g:g + 1]


def _sgu_fwd(x, ln, w_in, g_v, ws, bT, w_out, name):
    T, D = x.shape
    tm = min(TOKEN_TILE, T)
    nw = w_in.shape[2]

    def body(x_ref, ln_ref, win_ref, gv_ref, ws_ref, bT_ref, wout_ref, xo_ref, z_ref, h_ref, y_ref, mix_ref):
        xv = x_ref[...]
        h, _, _ = _rms(xv, ln_ref[...])
        hb = h.astype(MXU)
        h_ref[...] = hb
        for j in range(NDEV):
            z_ref[:, j * nw:(j + 1) * nw] = _mm(hb, win_ref[j])
        u = _gelu(z_ref[:, :D])
        gv = _gelu(z_ref[:, D:])
        vn, _, _ = _rms(gv, gv_ref[...])
        _spatial_mix(vn.astype(MXU), ws_ref, bT_ref, mix_ref, tm)
        y = (u * mix_ref[...]).astype(MXU)
        y_ref[...] = y
        xo_ref[...] = xv + _mm(y, wout_ref[...])

    return _call(
        body, name=name, grid=(T // tm,),
        out_shape=[jax.ShapeDtypeStruct((T, D), F32), jax.ShapeDtypeStruct((T, 2 * D), F32),
                   jax.ShapeDtypeStruct((T, D), MXU), jax.ShapeDtypeStruct((T, D), MXU)],
        in_specs=[_tile(tm, D), _whole(ln.shape), _whole(w_in.shape), _whole(g_v.shape), _whole(ws.shape),
                  _whole(bT.shape), _whole(w_out.shape)],
        out_specs=[_tile(tm, D), _tile(tm, 2 * D), _tile(tm, D), _tile(tm, D)],
        scratch_shapes=[pltpu.VMEM((tm, D), F32)],
        compiler_params=_params(40),
    )(x, ln, w_in, g_v, ws, bT, w_out)


def _mlp_fwd(x, ln, w_up, w_down, cargo, name):
    T, D = x.shape
    tm = min(TOKEN_TILE, T)
    nf = w_up.shape[2]
    F = nf * NDEV
    nc = len(cargo)
    lands, sems = _gather_ride_shapes(cargo)

    def body(x_ref, ln_ref, wup_ref, wdown_ref, *rest):
        xo_ref, pre_ref, h_ref = rest[nc:nc + 3]
        finish = _ride_along_gather(rest[:nc] + rest[nc + 3:], nc, 1)
        xv = x_ref[...]
        h, _, _ = _rms(xv, ln_ref[...])
        hb = h.astype(MXU)
        h_ref[...] = hb
        for j in range(NDEV):
            pre_ref[:, j * nf:(j + 1) * nf] = _mm(hb, wup_ref[j])
        a = jnp.maximum(pre_ref[...], 0.0)
        xo_ref[...] = xv + _mm(a * a, wdown_ref[...])
        finish()

    outs = _call(
        body, name=name, grid=(T // tm,),
        out_shape=[jax.ShapeDtypeStruct((T, D), F32), jax.ShapeDtypeStruct((T, F), F32),
                   jax.ShapeDtypeStruct((T, D), MXU)] + lands,
        in_specs=[_tile(tm, D), _whole(ln.shape), _whole(w_up.shape), _whole(w_down.shape)] + [ANY_SPEC] * nc,
        out_specs=[_tile(tm, D), _tile(tm, F), _tile(tm, D)] + [ANY_SPEC] * nc,
        scratch_shapes=sems, compiler_params=_params(52),
    )(x, ln, w_up, w_down, *cargo)
    return outs[0], outs[1], outs[2], outs[3:]


def _ple_fwd(x, p, ln, w_g, w_pp, name):
    T, D = x.shape
    tm = min(TOKEN_TILE, T)
    npp = w_pp.shape[2]

    def body(x_ref, p_ref, ln_ref, wg_ref, wpp_ref, xo_ref, gate_ref, pp_ref, h_ref):
        xv = x_ref[...]
        h, _, _ = _rms(xv, ln_ref[...])
        hb = h.astype(MXU)
        h_ref[...] = hb
        gate = jax.nn.sigmoid(_mm(hb, wg_ref[...]))
        gate_ref[...] = gate
        pb = p_ref[...].astype(MXU)
        for j in range(NDEV):
            pp_ref[:, j * npp:(j + 1) * npp] = _mm(pb, wpp_ref[j])
        xo_ref[...] = xv + pp_ref[...] * gate

    return _call(
        body, name=name, grid=(T // tm,),
        out_shape=[jax.ShapeDtypeStruct((T, D), F32), jax.ShapeDtypeStruct((T, D), F32),
                   jax.ShapeDtypeStruct((T, D), F32), jax.ShapeDtypeStruct((T, D), MXU)],
        in_specs=[_tile(tm, D), _tile(tm, p.shape[1]), _whole(ln.shape), _whole(w_g.shape), _whole(w_pp.shape)],
        out_specs=[_tile(tm, D), _tile(tm, D), _tile(tm, D), _tile(tm, D)],
        compiler_params=_params(32),
    )(x, p, ln, w_g, w_pp)


def _head_ones():
    row = lax.broadcasted_iota(jnp.int32, (128, 128), 0)
    col = lax.broadcasted_iota(jnp.int32, (128, 128), 1)
    return (jnp.right_shift(row, 6) == jnp.right_shift(col, 6)).astype(MXU)


def _head_rms(x, g, ones):
    rstd = lax.rsqrt(_split_dot(x * x, ones, 3) * (1.0 / HEAD_DIM) + EPS)
    xhat = x * rstd
    return xhat * g, xhat, rstd


def _head_rms_bwd(dh, xhat, rstd, g, ones):
    dxh = dh * g
    mean = _split_dot(dxh * xhat, ones, 3) * (1.0 / HEAD_DIM)
    return rstd * (dxh - xhat * mean), jnp.sum(dh * xhat, axis=0, keepdims=True)


def _qkv_fwd(x, ln_q, ln_kv, g_q, g_k, w_q, w_kv, name):
    T, D = x.shape
    tm = min(TOKEN_TILE, T)
    nk = w_kv.shape[2]
    half = NDEV // 2

    def body(x_ref, lnq_ref, lnkv_ref, gq_ref, gk_ref, wq_ref, wkv_ref,
             q_ref, k_ref, v_ref, qpre_ref, kpre_ref, hq_ref, hkv_ref):
        xv = x_ref[...]
        _, xhat, _ = _rms(xv, lnq_ref[...])
        hq = (xhat * lnq_ref[...]).astype(MXU)
        hkv = (xhat * lnkv_ref[...]).astype(MXU)
        hq_ref[...] = hq
        hkv_ref[...] = hkv
        qpre_ref[...] = _mm(hq, wq_ref[...])
        for j in range(half):
            kpre_ref[:, j * nk:(j + 1) * nk] = _mm(hkv, wkv_ref[j])
            v_ref[:, j * nk:(j + 1) * nk] = _mm(hkv, wkv_ref[half + j]).astype(MXU)
        ones = _head_ones()
        for b in range(D // 128):
            cols = slice(b * 128, (b + 1) * 128)
            qn, _, _ = _head_rms(qpre_ref[:, cols], gq_ref[:, cols], ones)
            q_ref[:, cols] = (qn * SCALE).astype(MXU)
            kn, _, _ = _head_rms(kpre_ref[:, cols], gk_ref[:, cols], ones)
            k_ref[:, cols] = kn.astype(MXU)

    return _call(
        body, name=name, grid=(T // tm,),
        out_shape=[jax.ShapeDtypeStruct((T, D), MXU)] * 3 + [jax.ShapeDtypeStruct((T, D), F32)] * 2
        + [jax.ShapeDtypeStruct((T, D), MXU)] * 2,
        in_specs=[_tile(tm, D), _whole(ln_q.shape), _whole(ln_kv.shape), _whole(g_q.shape), _whole(g_k.shape),
                  _whole(w_q.shape), _whole(w_kv.shape)],
        out_specs=[_tile(tm, D)] * 7,
        compiler_params=_params(40),
    )(x, ln_q, ln_kv, g_q, g_k, w_q, w_kv)


SB_KEYS = 2 * QBLK


def _sb_consts():
    row = lax.broadcasted_iota(jnp.int32, (QBLK, QBLK), 0)
    col = lax.broadcasted_iota(jnp.int32, (QBLK, QBLK), 1)
    lane = lax.broadcasted_iota(jnp.int32, (QBLK, 128), 1)
    ones = jnp.ones((QBLK, QBLK), MXU)
    later = jnp.concatenate([(row > col).astype(MXU), ones], axis=1)
    later_eq = jnp.concatenate([(row >= col).astype(MXU), ones], axis=1)
    return later, later_eq, lane < HEAD_DIM


MASKED_LOG = -1e30


def _sb_window(i, w):
    upper = (i + 1) * QBLK - w * SB_KEYS
    start = pl.multiple_of(jnp.maximum(upper - SB_KEYS, 0), QBLK)
    key = lax.broadcasted_iota(jnp.int32, (2 * QBLK, SB_KEYS), 1) + start
    return start, key < upper


def _sb_diagonal():
    row = jnp.bitwise_and(lax.broadcasted_iota(jnp.int32, (2 * QBLK, SB_KEYS), 0), QBLK - 1)
    key = lax.broadcasted_iota(jnp.int32, (2 * QBLK, SB_KEYS), 1)
    cases = []
    for shift in (0, QBLK):
        seen = key < row + shift
        cases.append(jnp.stack([jnp.where(seen, 1.0, 0.0), jnp.where(seen, 0.0, MASKED_LOG)]))
    return jnp.stack(cases).astype(F32)


_SB_DIAG_SPEC = pl.BlockSpec((None, 2, 2 * QBLK, SB_KEYS), lambda h, i: (jnp.minimum(i, 1), 0, 0, 0))


def _sb_terms(x, terms):
    x = jnp.concatenate([x[:, :QBLK], x[:, QBLK:]], axis=0)
    out = []
    for _ in range(terms):
        part = x.astype(MXU)
        x = x - part.astype(F32)
        out.append(part)
    return tuple(out)


def _sb_suffix(parts, ones, carry):
    s = jnp.dot(jnp.concatenate(parts[:2], axis=1), jnp.concatenate([ones, ones], axis=0),
                preferred_element_type=F32)
    for part in parts[2:]:
        s = s + jnp.dot(part, ones, preferred_element_type=F32)
    rows = s.shape[0] // 2
    s_lo, sum_lo, s_hi, sum_hi = s[:rows, :QBLK], s[:rows, QBLK:], s[rows:, :QBLK], s[rows:, QBLK:]
    return jnp.concatenate([s_lo + (carry + sum_hi), s_hi + carry], axis=1), carry + (sum_lo + sum_hi)


def _sb_scores(z, mask):
    sp = _softplus(z)
    l, log_sig = -sp, z - sp
    if isinstance(mask, tuple):
        keep, bias = mask
        l, log_sig = l * keep, log_sig + bias
    else:
        l = jnp.where(mask, l, 0.0)
        log_sig = jnp.where(mask, log_sig, MASKED_LOG)
    return log_sig, _sb_terms(l, 2)


def _sb_weights(staged, later, c_l):
    log_sig, parts = staged
    b, c_l = _sb_suffix(parts, later, c_l)
    return jnp.exp(log_sig + b), c_l


DEAD_LOG = -88.0


def _sb_alive(carry):
    return (jnp.max(carry[0]) > DEAD_LOG).astype(jnp.int32)


def _ride_along(refs, n, scatter, rank=2):
    if n == 0:
        return lambda: None
    step, steps = 0, 1
    for d in range(rank):
        step = step * pl.num_programs(d) + pl.program_id(d)
        steps = steps * pl.num_programs(d)
    srcs, lands, sems = refs[:n], refs[n:2 * n], refs[2 * n:]

    @pl.when(step == 0)
    def _():
        _exchange_start(srcs, lands, sems, scatter)

    def finish():
        @pl.when(step == steps - 1)
        def _():
            _exchange_finish(srcs, lands, sems, scatter)

    return finish


def _gather_ride_shapes(arrs):
    n = len(arrs)
    if n == 0:
        return [], []
    return ([jax.ShapeDtypeStruct((NDEV,) + a.shape, a.dtype) for a in arrs],
            [pltpu.SemaphoreType.DMA((n, NDEV - 1)), pltpu.SemaphoreType.DMA((n, NDEV - 1)),
             pltpu.SemaphoreType.DMA((n,))])


def _ride_along_gather(refs, n, rank):
    if n == 0:
        return lambda: None
    step, steps = 0, 1
    for d in range(rank):
        step = step * pl.num_programs(d) + pl.program_id(d)
        steps = steps * pl.num_programs(d)
    srcs, outs = refs[:n], refs[n:2 * n]
    send_sems, recv_sems, local_sems = refs[2 * n:]
    x, y, c = lax.axis_index("x"), lax.axis_index("y"), lax.axis_index("c")
    me, sibling = (x, y, c), (x, y, 1 - c)
    chips = [(1 - x, y), (x, 1 - y), (1 - x, 1 - y)]

    def index(dev):
        return 4 * dev[0] + 2 * dev[1] + dev[2]

    def copy(a, k, block, to, src=None):
        dst = outs[a].at[index(block)]
        return pltpu.make_async_remote_copy(
            src_ref=dst if src is None else src, dst_ref=dst, send_sem=send_sems.at[a, k],
            recv_sem=recv_sems.at[a, k], device_id=to, device_id_type=MESH)

    def mine(a):
        return pltpu.make_async_copy(srcs[a], outs[a].at[index(me)], local_sems.at[a])

    def first(a):
        return [copy(a, 0, me, sibling, src=srcs[a])] + [copy(a, 1 + j, me, (*chip, c), src=srcs[a])
                                                         for j, chip in enumerate(chips)]

    @pl.when(step == 0)
    def _():
        for a in range(n):
            mine(a).start()
            for cp in first(a):
                cp.start()

    @pl.when(step == ((3 * steps) // 4 if rank > 1 else steps - 1))
    def _():
        for a in range(n):
            for j, chip in enumerate(chips):
                copy(a, 1 + j, (*chip, c), me).wait_recv()
                copy(a, 4 + j, (*chip, c), sibling).start()

    def finish():
        @pl.when(step == steps - 1)
        def _():
            for a in range(n):
                copy(a, 0, sibling, me).wait_recv()
                for j, chip in enumerate(chips):
                    copy(a, 4 + j, (*chip, 1 - c), me).wait_recv()
                for cp in first(a):
                    cp.wait_send()
                for j, chip in enumerate(chips):
                    copy(a, 4 + j, (*chip, c), sibling).wait_send()
                mine(a).wait()

    return finish


def _sb_fwd(q, k, v, cargo, name):
    T, D = q.shape
    nc = len(cargo)
    lands, sems = _gather_ride_shapes(cargo)

    def body(diag_ref, q_ref, k_ref, v_ref, *rest):
        o_ref = rest[nc]
        finish = _ride_along_gather(rest[:nc] + rest[nc + 1:], nc, 2)
        i = pl.program_id(1)
        n_steps = (i + 2) // 2
        later, _, first = _sb_consts()
        qv = q_ref[...]
        zero = jnp.zeros_like(qv)
        q2 = jnp.concatenate([jnp.where(first, qv, zero), jnp.where(first, zero, qv)], axis=0)

        def window(w, carry, diagonal):
            start, mask = _sb_window(i, w)
            if diagonal:
                mask = (diag_ref[0], diag_ref[1])
            kw = k_ref[pl.ds(start, SB_KEYS), :]
            vw = v_ref[pl.ds(start, SB_KEYS), :]
            c_l, acc = carry
            a, c_l = _sb_weights(_sb_scores(_mm_nt(q2, kw), mask), later, c_l)
            return c_l, acc + _mm(a, vw)

        def step(state):
            w, _, carry = state
            carry = window(w, carry, False)
            return w + 1, _sb_alive(carry), carry

        carry = window(0, (jnp.zeros((2 * QBLK, 128), F32),) * 2, True)
        _, _, carry = lax.while_loop(lambda s: (s[0] < n_steps) & (s[1] > 0), step,
                                     (jnp.int32(1), _sb_alive(carry), carry))
        o_ref[...] = jnp.where(first, carry[1][:QBLK], carry[1][QBLK:])
        finish()

    qblk = pl.BlockSpec((QBLK, 128), lambda h, i: (i, h))
    kblk = pl.BlockSpec((T, 128), lambda h, i: (0, h))
    outs = _call(
        body, name=name, grid=(D // 128, T // QBLK), out_shape=[jax.ShapeDtypeStruct((T, D), F32)] + lands,
        in_specs=[_SB_DIAG_SPEC, qblk, kblk, kblk] + [ANY_SPEC] * nc, out_specs=[qblk] + [ANY_SPEC] * nc,
        scratch_shapes=sems, compiler_params=_params(32, 2),
    )(_sb_diagonal(), q, k, v, *cargo)
    return outs[0], outs[1:]


def _sb_bwd(q, k, v, o, do, cargo, name):
    T, D = q.shape
    nc = len(cargo)
    lands, sems = _exchange_shapes(cargo, True)

    def body(diag_ref, q_ref, k_ref, v_ref, o_ref, do_ref, *rest):
        dq_ref, dk_ref, dv_ref = rest[nc:nc + 3]
        finish = _ride_along(rest[:nc] + rest[nc + 3:], nc, True)
        i = pl.program_id(1)

        @pl.when(i == 0)
        def _():
            dk_ref[...] = jnp.zeros_like(dk_ref)
            dv_ref[...] = jnp.zeros_like(dv_ref)

        n_steps = (i + 2) // 2
        later, later_eq, first = _sb_consts()
        qv = q_ref[...]
        dob = do_ref[...].astype(MXU)
        zero = jnp.zeros_like(qv)
        q2 = jnp.concatenate([jnp.where(first, qv, zero), jnp.where(first, zero, qv)], axis=0)
        do2 = jnp.concatenate([jnp.where(first, dob, zero), jnp.where(first, zero, dob)], axis=0)
        prod = o_ref[...] * dob.astype(F32)
        prod2 = jnp.concatenate([jnp.where(first, prod, 0.0), jnp.where(first, 0.0, prod)], axis=0)
        total = _split_dot(prod2, jnp.ones((128, 128), MXU), 3)
        total = jnp.concatenate([total, total], axis=1)

        def window(w, carry, diagonal):
            start, mask = _sb_window(i, w)
            if diagonal:
                mask = (diag_ref[0], diag_ref[1])
            kw = k_ref[pl.ds(start, SB_KEYS), :]
            vw = v_ref[pl.ds(start, SB_KEYS), :]
            c_l, c_e, dq = carry
            log_sig, parts = _sb_scores(_mm_nt(q2, kw), mask)
            a, c_l = _sb_weights((log_sig, parts), later, c_l)
            ab = a.astype(MXU)
            e = ab.astype(F32) * _mm_nt(do2, vw)
            from_here, c_e = _sb_suffix(_sb_terms(e, 2), later_eq, c_e)
            sig = jnp.exp(log_sig)
            dzb = (e * (1.0 - sig) - sig * (total - from_here)).astype(MXU)
            dk_ref[pl.ds(start, SB_KEYS), :] += _mm_tn(dzb, q2)
            dv_ref[pl.ds(start, SB_KEYS), :] += _mm_tn(ab, do2)
            return c_l, c_e, dq + _mm(dzb, kw)

        def step(state):
            w, _, carry = state
            carry = window(w, carry, False)
            return w + 1, _sb_alive(carry), carry

        carry = window(0, (jnp.zeros((2 * QBLK, 128), F32),) * 3, True)
        _, _, carry = lax.while_loop(lambda s: (s[0] < n_steps) & (s[1] > 0), step,
                                     (jnp.int32(1), _sb_alive(carry), carry))
        dq_ref[...] = jnp.where(first, carry[2][:QBLK], carry[2][QBLK:]) * SCALE
        finish()

    qblk = pl.BlockSpec((QBLK, 128), lambda h, i: (i, h))
    kblk = pl.BlockSpec((T, 128), lambda h, i: (0, h))
    full = jax.ShapeDtypeStruct((T, D), F32)
    outs = _call(
        body, name=name, grid=(D // 128, T // QBLK), out_shape=[full, full, full] + lands,
        in_specs=[_SB_DIAG_SPEC, qblk, kblk, kblk, qblk, qblk] + [ANY_SPEC] * nc,
        out_specs=[qblk, kblk, kblk] + [ANY_SPEC] * nc, scratch_shapes=sems, compiler_params=_params(32, 2),
    )(_sb_diagonal(), q, k, v, o, do, *cargo)
    return outs[0], outs[1], outs[2], outs[3:]


def _proj_res(x, a, w, name):
    T, D = x.shape
    tm = min(TOKEN_TILE, T)

    def body(x_ref, a_ref, w_ref, o_ref):
        o_ref[...] = x_ref[...] + _mm(a_ref[...], w_ref[...])

    return _call(
        body, name=name, grid=(T // tm,), out_shape=jax.ShapeDtypeStruct((T, D), F32),
        in_specs=[_tile(tm, D), _tile(tm, a.shape[1]), _whole(w.shape)], out_specs=_tile(tm, D),
        compiler_params=_params(32),
    )(x, a, w)


def _proj_nt(g, w, name):
    T = g.shape[0]
    K = w.shape[0]
    tm = min(TOKEN_TILE, T)

    def body(g_ref, w_ref, o_ref):
        o_ref[...] = _mm_nt(g_ref[...], w_ref[...])

    return _call(
        body, name=name, grid=(T // tm,), out_shape=jax.ShapeDtypeStruct((T, K), F32),
        in_specs=[_tile(tm, g.shape[1]), _whole(w.shape)], out_specs=_tile(tm, K),
        compiler_params=_params(32),
    )(g, w)


def _loss_grad(y, tgt, name):
    T, D = y.shape
    tm = min(TOKEN_TILE, T)

    def body(y_ref, t_ref, dy_ref, loss_ref):
        @pl.when(pl.program_id(0) == 0)
        def _():
            loss_ref[...] = jnp.zeros_like(loss_ref)
        diff = y_ref[...] - t_ref[...]
        dy_ref[...] = diff * (1.0 / D)
        rows = jnp.sum(diff * diff, axis=1, keepdims=True) * (1.0 / D)
        loss_ref[...] += 0.5 * jnp.sum(rows, axis=0, keepdims=True)

    return _call(
        body, name=name, grid=(T // tm,),
        out_shape=[jax.ShapeDtypeStruct((T, D), F32), jax.ShapeDtypeStruct((1, 1), F32)],
        in_specs=[_tile(tm, D), _tile(tm, D)], out_specs=[_tile(tm, D), _acc((1, 1))],
        compiler_params=_params(32),
    )(y, tgt)


def _ple_bwd(dx, x, gate, pp, ln, w_g, name):
    T, D = x.shape
    tm = min(TOKEN_TILE, T)

    def body(dx_ref, x_ref, gate_ref, pp_ref, ln_ref, wg_ref, dxo_ref, dpp_ref, dgp_ref, dln_ref):
        @pl.when(pl.program_id(0) == 0)
        def _():
            dln_ref[...] = jnp.zeros_like(dln_ref)
        dxv = dx_ref[...]
        gate = gate_ref[...]
        _, xhat, rstd = _rms(x_ref[...], ln_ref[...])
        dpp_ref[...] = (dxv * gate).astype(MXU)
        dgp = (dxv * pp_ref[...] * gate * (1.0 - gate)).astype(MXU)
        dgp_ref[...] = dgp
        dxn, dln = _rms_bwd(_mm_nt(dgp, wg_ref[...]), xhat, rstd, ln_ref[...])
        dln_ref[...] += dln
        dxo_ref[...] = dxn + dxv

    return _call(
        body, name=name, grid=(T // tm,),
        out_shape=[jax.ShapeDtypeStruct((T, D), F32), jax.ShapeDtypeStruct((T, D), MXU),
                   jax.ShapeDtypeStruct((T, D), MXU), jax.ShapeDtypeStruct(ln.shape, F32)],
        in_specs=[_tile(tm, D)] * 4 + [_whole(ln.shape), _whole(w_g.shape)],
        out_specs=[_tile(tm, D), _tile(tm, D), _tile(tm, D), _acc(ln.shape)],
        compiler_params=_params(32),
    )(dx, x, gate, pp, ln, w_g)


def _mlp_bwd(dx, x, pre, ln, w_up, w_down, cargo, name):
    T, D = x.shape
    tm = min(TOKEN_TILE, T)
    nf = w_up.shape[2]
    F = nf * NDEV
    nc = len(cargo)
    lands, sems = _exchange_shapes(cargo, True)

    def body(dx_ref, x_ref, pre_ref, ln_ref, wup_ref, wdown_ref, *rest):
        dxo_ref, dpre_ref, s_ref, dln_ref = rest[nc:nc + 4]
        finish = _ride_along(rest[:nc] + rest[nc + 4:], nc, True, rank=1)

        @pl.when(pl.program_id(0) == 0)
        def _():
            dln_ref[...] = jnp.zeros_like(dln_ref)
        dxv = dx_ref[...]
        _, xhat, rstd = _rms(x_ref[...], ln_ref[...])
        a = jnp.maximum(pre_ref[...], 0.0)
        s_ref[...] = (a * a).astype(MXU)
        dpre_ref[...] = (_mm_nt(dxv, wdown_ref[...]) * (2.0 * a)).astype(MXU)
        dh = _mm_nt(dpre_ref[:, :nf], wup_ref[0])
        for j in range(1, NDEV):
            dh += _mm_nt(dpre_ref[:, j * nf:(j + 1) * nf], wup_ref[j])
        dxn, dln = _rms_bwd(dh, xhat, rstd, ln_ref[...])
        dln_ref[...] += dln
        dxo_ref[...] = dxn + dxv
        finish()

    outs = _call(
        body, name=name, grid=(T // tm,),
        out_shape=[jax.ShapeDtypeStruct((T, D), F32), jax.ShapeDtypeStruct((T, F), MXU),
                   jax.ShapeDtypeStruct((T, F), MXU), jax.ShapeDtypeStruct(ln.shape, F32)] + lands,
        in_specs=[_tile(tm, D), _tile(tm, D), _tile(tm, F), _whole(ln.shape), _whole(w_up.shape),
                  _whole(w_down.shape)] + [ANY_SPEC] * nc,
        out_specs=[_tile(tm, D), _tile(tm, F), _tile(tm, F), _acc(ln.shape)] + [ANY_SPEC] * nc,
        scratch_shapes=sems, compiler_params=_params(56),
    )(dx, x, pre, ln, w_up, w_down, *cargo)
    return outs[0], outs[1], outs[2], outs[3], outs[4:]


def _qkv_bwd(dx, x, dq, dk, dv, q_pre, k_pre, ln_q, ln_kv, g_q, g_k, w_q, w_kv, cargo, name):
    T, D = x.shape
    tm = min(TOKEN_TILE, T)
    nk = w_kv.shape[2]
    n_tiles = T // tm
    nc = len(cargo)
    lands, sems = _exchange_shapes(cargo, True)

    def body(dx_ref, x_ref, dq_ref, dk_ref, dv_ref, qpre_ref, kpre_ref, lnq_ref, lnkv_ref, gq_ref, gk_ref,
             wq_ref, wkv_ref, *rest):
        dxo_ref, dqp_ref, dkv_ref, dlnq_ref, dlnkv_ref, dgq_ref, dgk_ref = rest[nc:nc + 7]
        gq_acc, gk_acc = rest[2 * nc + 7:2 * nc + 9]
        finish = _ride_along(rest[:nc] + rest[nc + 7:2 * nc + 7] + rest[2 * nc + 9:], nc, True, rank=1)
        i = pl.program_id(0)

        @pl.when(i == 0)
        def _():
            dlnq_ref[...] = jnp.zeros_like(dlnq_ref)
            dlnkv_ref[...] = jnp.zeros_like(dlnkv_ref)
            gq_acc[...] = jnp.zeros_like(gq_acc)
            gk_acc[...] = jnp.zeros_like(gk_acc)

        ones = _head_ones()
        for b in range(D // 128):
            cols = slice(b * 128, (b + 1) * 128)
            _, xh, rs = _head_rms(qpre_ref[:, cols], gq_ref[:, cols], ones)
            d, dg = _head_rms_bwd(dq_ref[:, cols], xh, rs, gq_ref[:, cols], ones)
            dqp_ref[:, cols] = d.astype(MXU)
            gq_acc[:, cols] += dg
            _, xh, rs = _head_rms(kpre_ref[:, cols], gk_ref[:, cols], ones)
            d, dg = _head_rms_bwd(dk_ref[:, cols], xh, rs, gk_ref[:, cols], ones)
            dkv_ref[:, cols] = d.astype(MXU)
            gk_acc[:, cols] += dg
        dkv_ref[:, D:] = dv_ref[...].astype(MXU)

        _, xhat, rstd = _rms(x_ref[...], lnq_ref[...])
        dhq = _mm_nt(dqp_ref[...], wq_ref[...])
        dhkv = _mm_nt(dkv_ref[:, :nk], wkv_ref[0])
        for j in range(1, NDEV):
            dhkv += _mm_nt(dkv_ref[:, j * nk:(j + 1) * nk], wkv_ref[j])
        dxq, dlnq = _rms_bwd(dhq, xhat, rstd, lnq_ref[...])
        dxkv, dlnkv = _rms_bwd(dhkv, xhat, rstd, lnkv_ref[...])
        dlnq_ref[...] += dlnq
        dlnkv_ref[...] += dlnkv
        dxo_ref[...] = dx_ref[...] + dxq + dxkv

        @pl.when(i == n_tiles - 1)
        def _():
            row = lax.broadcasted_iota(jnp.int32, (D, 128), 0)
            col = lax.broadcasted_iota(jnp.int32, (D, 128), 1)
            fold = (jnp.bitwise_and(row, HEAD_DIM - 1) == col).astype(MXU)
            dgq_ref[...] = _split_dot(jnp.broadcast_to(gq_acc[...], (8, D)), fold, 3)
            dgk_ref[...] = _split_dot(jnp.broadcast_to(gk_acc[...], (8, D)), fold, 3)

        finish()

    small = jax.ShapeDtypeStruct((8, 128), F32)
    outs = _call(
        body, name=name, grid=(n_tiles,),
        out_shape=[jax.ShapeDtypeStruct((T, D), F32), jax.ShapeDtypeStruct((T, D), MXU),
                   jax.ShapeDtypeStruct((T, 2 * D), MXU), jax.ShapeDtypeStruct(ln_q.shape, F32),
                   jax.ShapeDtypeStruct(ln_kv.shape, F32), small, small] + lands,
        in_specs=[_tile(tm, D)] * 7 + [_whole(ln_q.shape), _whole(ln_kv.shape), _whole(g_q.shape),
                                       _whole(g_k.shape), _whole(w_q.shape), _whole(w_kv.shape)] + [ANY_SPEC] * nc,
        out_specs=[_tile(tm, D), _tile(tm, D), _tile(tm, 2 * D), _acc(ln_q.shape), _acc(ln_kv.shape),
                   _acc((8, 128)), _acc((8, 128))] + [ANY_SPEC] * nc,
        scratch_shapes=[pltpu.VMEM((1, D), F32), pltpu.VMEM((1, D), F32)] + sems,
        compiler_params=_params(48),
    )(dx, x, dq, dk, dv, q_pre, k_pre, ln_q, ln_kv, g_q, g_k, w_q, w_kv, *cargo)
    return outs[:7], outs[7:]


def _sgu_bwd(dx, x, z, ln, w_in, g_v, ws, wsT, bT, w_out, cargo, name):
    T, D = x.shape
    tm = min(TOKEN_TILE, T)
    nw = w_in.shape[2]
    nc = len(cargo)
    lands, sems = _exchange_shapes(cargo, True)

    def body(dx_ref, x_ref, z_ref, ln_ref, win_ref, gv_ref, ws_ref, wsT_ref, bT_ref, wout_ref, *rest):
        dxo_ref, dz_ref, dws_ref, dbT_ref, dln_ref, dgv_ref = rest[nc:nc + 6]
        mix_ref, dvn_ref = rest[2 * nc + 6:2 * nc + 8]
        finish = _ride_along(rest[:nc] + rest[nc + 6:2 * nc + 6] + rest[2 * nc + 8:], nc, True, rank=1)

        @pl.when(pl.program_id(0) == 0)
        def _():
            dws_ref[...] = jnp.zeros_like(dws_ref)
            dbT_ref[...] = jnp.zeros_like(dbT_ref)
            dln_ref[...] = jnp.zeros_like(dln_ref)
            dgv_ref[...] = jnp.zeros_like(dgv_ref)
        dxv = dx_ref[...]
        _, xhat, rstd = _rms(x_ref[...], ln_ref[...])
        u, du = _gelu_and_grad(z_ref[:, :D])
        gv, dgv = _gelu_and_grad(z_ref[:, D:])
        vn, vhat, rstd_v = _rms(gv, gv_ref[...])
        vnb = vn.astype(MXU)
        _spatial_mix(vnb, ws_ref, bT_ref, mix_ref, tm)
        dy = _mm_nt(dxv, wout_ref[...])
        d_u = dy * mix_ref[...]
        d_mix = dy * u
        dmb = d_mix.astype(MXU)
        tri, triT = _tril_mask()
        for g in range(GROUPS):
            wmT = jnp.where(triT, wsT_ref[g], 0.0).astype(MXU)
            cols = slice(g * CHUNK, (g + 1) * CHUNK)
            for ch in range(tm // CHUNK):
                rows = slice(ch * CHUNK, (ch + 1) * CHUNK)
                dm = dmb[rows, cols]
                dws_ref[g] += jnp.where(tri, _mm_nt(dm, vnb[rows, cols]), 0.0)
                dbT_ref[:, g:g + 1] += jnp.sum(d_mix[rows, cols], axis=1, keepdims=True)
                dvn_ref[rows, cols] = _mm(wmT, dm)
        d_gv, dg = _rms_bwd(dvn_ref[...], vhat, rstd_v, gv_ref[...])
        dgv_ref[...] += dg
        dz_ref[:, :D] = (d_u * du).astype(MXU)
        dz_ref[:, D:] = (d_gv * dgv).astype(MXU)
        dh = _mm_nt(dz_ref[:, :nw], win_ref[0])
        for j in range(1, NDEV):
            dh += _mm_nt(dz_ref[:, j * nw:(j + 1) * nw], win_ref[j])
        dxn, dln = _rms_bwd(dh, xhat, rstd, ln_ref[...])
        dln_ref[...] += dln
        dxo_ref[...] = dxn + dxv
        finish()

    outs = _call(
        body, name=name, grid=(T // tm,),
        out_shape=[jax.ShapeDtypeStruct((T, D), F32), jax.ShapeDtypeStruct((T, 2 * D), MXU),
                   jax.ShapeDtypeStruct(ws.shape, F32), jax.ShapeDtypeStruct(bT.shape, F32),
                   jax.ShapeDtypeStruct(ln.shape, F32), jax.ShapeDtypeStruct(g_v.shape, F32)] + lands,
        in_specs=[_tile(tm, D), _tile(tm, D), _tile(tm, 2 * D), _whole(ln.shape), _whole(w_in.shape),
                  _whole(g_v.shape), _whole(ws.shape), _whole(wsT.shape), _whole(bT.shape), _whole(w_out.shape)]
        + [ANY_SPEC] * nc,
        out_specs=[_tile(tm, D), _tile(tm, 2 * D), _acc(ws.shape), _acc(bT.shape), _acc(ln.shape),
                   _acc(g_v.shape)] + [ANY_SPEC] * nc,
        scratch_shapes=[pltpu.VMEM((tm, D), F32), pltpu.VMEM((tm, D), F32)] + sems,
        compiler_params=_params(48),
    )(dx, x, z, ln, w_in, g_v, ws, wsT, bT, w_out, *cargo)
    return outs[:6], outs[6:]


def _wgrad_rows(a, g, name):
    T, K = a.shape
    N = g.shape[1]
    kb = K // NDEV

    def body(a_ref, g_ref, o_ref):
        o_ref[...] = _mm_tn(a_ref[...], g_ref[...]).astype(COMM)

    return _call(
        body, name=name, grid=(NDEV,), out_shape=jax.ShapeDtypeStruct((K, N), COMM),
        in_specs=[pl.BlockSpec((T, kb), lambda j: (0, j)), _whole(g.shape)],
        out_specs=pl.BlockSpec((kb, N), lambda j: (j, 0)),
        compiler_params=_params(40),
    )(a, g).reshape(NDEV, kb, N)


def _wgrad_cols(a, g, name):
    T, K = a.shape
    N = g.shape[1]
    nb = N // NDEV

    def body(a_ref, g_ref, o_ref):
        o_ref[...] = _mm_tn(a_ref[...], g_ref[...]).astype(COMM)

    return _call(
        body, name=name, grid=(NDEV,), out_shape=jax.ShapeDtypeStruct((NDEV, K, nb), COMM),
        in_specs=[_whole(a.shape), pl.BlockSpec((T, nb), lambda j: (0, j))],
        out_specs=pl.BlockSpec((None, K, nb), lambda j: (j, 0, 0)),
        compiler_params=_params(40),
    )(a, g)


def _adamw_rows(R, C):
    tr = math.gcd(R, max(8, (128 * 1024) // C))
    return R if tr < 64 else tr


def _adamw_update(w_ref, m_ref, v_ref, s_ref, g_ref, d_ref, mo_ref, vo_ref):
    g = s_ref[0].astype(F32)
    for j in range(1, s_ref.shape[0]):
        g = g + s_ref[j].astype(F32)
    mn = ADAM_B1 * m_ref[...] + (1.0 - ADAM_B1) * g
    vn = ADAM_B2 * v_ref[...] + (1.0 - ADAM_B2) * (g * g)
    g_ref[...] = g
    mo_ref[...] = mn
    vo_ref[...] = vn
    m_hat = mn / (1.0 - ADAM_B1 ** ADAM_STEP)
    v_hat = vn / (1.0 - ADAM_B2 ** ADAM_STEP)
    d_ref[...] = -ADAM_LR * (m_hat / (jnp.sqrt(v_hat) + ADAM_EPS) + ADAM_WD * w_ref[...])


def _adamw_layers(w, m, v, slots0, slots1, name):
    _, R, C = w.shape
    tr = _adamw_rows(R, C)
    last = R // tr - 1

    def body(w_ref, m_ref, v_ref, s0_ref, s1_ref, *outs):
        @pl.when(pl.program_id(0) == 0)
        def _():
            _adamw_update(w_ref, m_ref, v_ref, s0_ref, *outs)

        @pl.when(pl.program_id(0) == 1)
        def _():
            _adamw_update(w_ref, m_ref, v_ref, s1_ref, *outs)

    blk = pl.BlockSpec((None, tr, C), lambda l, i: (l, i, 0))
    s0_blk = pl.BlockSpec((slots0.shape[0], tr, C), lambda l, i: (0, jnp.where(l == 0, i, last), 0))
    s1_blk = pl.BlockSpec((slots1.shape[0], tr, C), lambda l, i: (0, jnp.where(l == 1, i, 0), 0))
    out = jax.ShapeDtypeStruct(w.shape, F32)
    return _call(
        body, name=name, grid=(2, R // tr), out_shape=[out, out, out, out],
        in_specs=[blk, blk, blk, s0_blk, s1_blk], out_specs=[blk] * 4, compiler_params=_params(32, 2),
    )(w, m, v, slots0, slots1)


def _adamw(w, m, v, slots, name):
    R, C = w.shape
    n = slots.shape[0]
    tr = _adamw_rows(R, C)

    def body(*refs):
        _adamw_update(*refs)

    blk = pl.BlockSpec((tr, C), lambda i: (i, 0))
    out = jax.ShapeDtypeStruct((R, C), F32)
    return _call(
        body, name=name, grid=(R // tr,), out_shape=[out, out, out, out],
        in_specs=[blk, blk, blk, pl.BlockSpec((n, tr, C), lambda i: (0, i, 0))], out_specs=[blk] * 4,
        compiler_params=_params(32),
    )(w, m, v, slots)


def _rows128(a):
    flat = a.reshape(-1)
    rows = -(-flat.shape[0] // 1024) * 8
    flat = jnp.pad(flat, (0, rows * 128 - flat.shape[0]))
    return flat.reshape(rows, 128)


def kernel(x, p, ln_mix_a, w_in_a, g_v_a, w_spatial, b_spatial, w_out_a, ln_kv, w_kv, g_k, ln_mix_b, w_q, g_q, w_out_b, ln_mlp, w_up, w_down, ln_ple, w_ple_gate, w_ple_proj, loss_target, m_ln_mix_a, m_w_in_a, m_g_v_a, m_w_spatial, m_b_spatial, m_w_out_a, m_ln_kv, m_w_kv, m_g_k, m_ln_mix_b, m_w_q, m_g_q, m_w_out_b, m_ln_mlp, m_w_up, m_w_down, m_ln_ple, m_w_ple_gate, m_w_ple_proj, v_ln_mix_a, v_w_in_a, v_g_v_a, v_w_spatial, v_b_spatial, v_w_out_a, v_ln_kv, v_w_kv, v_g_k, v_ln_mix_b, v_w_q, v_g_q, v_w_out_b, v_ln_mlp, v_w_up, v_w_down, v_ln_ple, v_w_ple_gate, v_w_ple_proj):
    me = 4 * lax.axis_index("x") + 2 * lax.axis_index("y") + lax.axis_index("c")
    D = x.shape[2]
    x0, tgt = x[0], loss_target[0]
    n_layers = w_up.shape[0]

    c = lambda w: w.astype(COMM)
    first = [c(w_in_a[0]), c(w_out_a[0]), ln_mix_a, g_v_a, c(w_up[0]), c(w_down[0])]
    after_mlp = [c(w_ple_gate[0]), c(w_ple_proj[0]), c(w_q[0]), c(w_kv)]
    second_small = [c(w_out_b[0]), c(w_ple_gate[1]), c(w_ple_proj[1])]
    second_big = [c(w_up[1]), c(w_down[1])]
    W_in, W_out_a, ln_a, gv_a, W_up0, W_down0 = _gather_two_level(first, "gather_first")
    W_out_a, ln_a, gv_a = W_out_a.reshape(-1, D), ln_a.reshape(1, D), gv_a.reshape(1, D)
    W_down0 = W_down0.reshape(-1, D)
    ws = w_spatial[0]
    wsT = jnp.swapaxes(ws, 1, 2)
    bT = b_spatial[0].T
    ln_kv2, ln_b = ln_kv.reshape(1, D), ln_mix_b
    gk2 = jnp.tile(g_k.reshape(1, HEAD_DIM), (1, D // HEAD_DIM))
    gq2 = jnp.tile(g_q, (1, D // HEAD_DIM))
    ln_m = [ln_mlp[l:l + 1] for l in range(n_layers)]
    ln_p = [ln_ple[l:l + 1] for l in range(n_layers)]

    x1, z, h_a, y_a = _sgu_fwd(x0, ln_a, W_in, gv_a, ws, bT, W_out_a, "sgu_fwd")
    x2, pre0, hm0, (W_g0, W_pp0, W_q, W_kv) = _mlp_fwd(x1, ln_m[0], W_up0, W_down0, after_mlp, "mlp_fwd0")
    W_g0, W_q = W_g0.reshape(-1, D), W_q.reshape(-1, D)
    x3, gate0, pp0, hp0 = _ple_fwd(x2, p[0, 0], ln_p[0], W_g0, W_pp0, "ple_fwd0")
    qn, kn, vn, q_pre, k_pre, h_q, h_kv = _qkv_fwd(x3, ln_b, ln_kv2, gq2, gk2, W_q, W_kv, "qkv_fwd")
    o2d, (W_out_b, W_g1, W_pp1, W_up1, W_down1) = _sb_fwd(qn, kn, vn, second_small + second_big, "sb_fwd")
    W_out_b, W_down1, W_g1 = W_out_b.reshape(-1, D), W_down1.reshape(-1, D), W_g1.reshape(-1, D)
    x4 = _proj_res(x3, o2d, W_out_b, "attn_out")
    x5, pre1, hm1, _ = _mlp_fwd(x4, ln_m[1], W_up1, W_down1, [], "mlp_fwd1")
    x6, gate1, pp1, hp1 = _ple_fwd(x5, p[1, 0], ln_p[1], W_g1, W_pp1, "ple_fwd1")
    dy, loss_part = _loss_grad(x6, tgt, "loss_grad")

    dx5, dpp1, dgp1, dlnp1 = _ple_bwd(dy, x5, gate1, pp1, ln_p[1], W_g1, "ple_bwd1")
    dx4, dpre1, s1, dlnm1, _ = _mlp_bwd(dx5, x4, pre1, ln_m[1], W_up1, W_down1, [], "mlp_bwd1")
    wg_big = [_wgrad_cols(hm1, dpre1, "wg_up1"), _wgrad_rows(s1, dx5, "wg_down1")]
    wg_small = [_wgrad_rows(hp1, dgp1, "wg_gate1"), _wgrad_cols(p[1, 0].astype(MXU), dpp1, "wg_proj1"),
                _wgrad_rows(o2d, dx4, "wg_out_b")]
    do2d = _proj_nt(dx4, W_out_b, "attn_out_bwd")
    dqn, dkn, dvn, (s_up1, s_down1, s_gate1, s_proj1, s_out_b) = _sb_bwd(qn, kn, vn, o2d, do2d, wg_big + wg_small,
                                                                         "sb_bwd")
    (dx3, dq_pre, dkv, dlnb, dlnkv, dgq, dgk), _ = _qkv_bwd(
        dx4, x3, dqn, dkn, dvn, q_pre, k_pre, ln_b, ln_kv2, gq2, gk2, W_q, W_kv, [], "qkv_bwd")
    dgq, dgk = dgq[:1, :HEAD_DIM], dgk[:1, :HEAD_DIM]
    dx2, dpp0, dgp0, dlnp0 = _ple_bwd(dx3, x2, gate0, pp0, ln_p[0], W_g0, "ple_bwd0")
    wg_kv = [_wgrad_cols(h_kv, dkv, "wg_kv")]
    dx1, dpre0, s0, dlnm0, (s_kv,) = _mlp_bwd(dx2, x1, pre0, ln_m[0], W_up0, W_down0, wg_kv, "mlp_bwd0")
    wg_ple = [_wgrad_rows(hp0, dgp0, "wg_gate0"), _wgrad_cols(p[0, 0].astype(MXU), dpp0, "wg_proj0"),
              _wgrad_rows(h_q, dq_pre, "wg_q")]
    (dx0, dz, dws, dbT, dlna, dgva), (s_gate0, s_proj0, s_q) = _sgu_bwd(
        dx1, x0, z, ln_a, W_in, gv_a, ws, wsT, bT, W_out_a, wg_ple, "sgu_bwd")
    wg_first = [_wgrad_cols(hm0, dpre0, "wg_up0"), _wgrad_rows(s0, dx2, "wg_down0"), _wgrad_cols(h_a, dz, "wg_in_a"),
                _wgrad_rows(y_a, dx1, "wg_out_a")]

    small = [("b_spatial", dbT.T[None], b_spatial, m_b_spatial, v_b_spatial),
             ("ln_kv", dlnkv.reshape(-1), ln_kv, m_ln_kv, v_ln_kv),
             ("g_k", dgk.reshape(-1), g_k, m_g_k, v_g_k),
             ("ln_mix_b", dlnb, ln_mix_b, m_ln_mix_b, v_ln_mix_b),
             ("g_q", dgq, g_q, m_g_q, v_g_q),
             ("ln_mlp", jnp.concatenate([dlnm0, dlnm1]), ln_mlp, m_ln_mlp, v_ln_mlp),
             ("ln_ple", jnp.concatenate([dlnp0, dlnp1]), ln_ple, m_ln_ple, v_ln_ple)]
    sharded_vec = [("ln_mix_a", dlna, ln_mix_a, m_ln_mix_a, v_ln_mix_a),
                   ("g_v_a", dgva, g_v_a, m_g_v_a, v_g_v_a)]
    packs = [[], [], [], []]
    for _, g, w, m, v in small:
        for lst, a in zip(packs, (g, w, m, v)):
            lst.append(_rows128(a))
    for _, g, w, m, v in sharded_vec:
        packs[0].append(g.reshape(NDEV, -1))
        for lst, a in zip(packs[1:], (w, m, v)):
            lst.append(jnp.broadcast_to(a, (NDEV, a.shape[1])))
    packs[0].append(_rows128(loss_part))
    for lst in packs[1:]:
        lst.append(jnp.zeros((8, 128), F32))
    g_pack, w_pack, m_pack, v_pack = (jnp.concatenate(lst) for lst in packs)
    g_pack8 = jnp.broadcast_to(g_pack[None], (NDEV,) + g_pack.shape)
    dws8 = jnp.broadcast_to(dws.reshape(1, -1, 128).astype(COMM), (NDEV, dws.size // 128, 128))
    by_chip = [a.reshape((4, 2) + a.shape[1:]) for a in wg_first]
    from_sibling, (g_all, s_ws) = _scatter_pair(by_chip, [g_pack8, dws8], "scatter_pair")
    my_core = lax.axis_index("c")
    chip_sums = [_pair_sum(lax.dynamic_index_in_dim(a, my_core, 1, keepdims=False), o, f"pair_sum{j}")
                 for j, (a, o) in enumerate(zip(by_chip, from_sibling))]
    s_up0, s_down0, s_in_a, s_out_a = _scatter_chips(chip_sums, "scatter_chips")

    def upd(w, m, v, s, name):
        shape = w.shape
        outs = _adamw(w.reshape(-1, shape[-1]), m.reshape(-1, shape[-1]), v.reshape(-1, shape[-1]), s, name)
        return [o.reshape(shape) for o in outs]

    res = {}
    res["w_out_b"] = upd(w_out_b, m_w_out_b, v_w_out_b, s_out_b, "adam_out_b")
    res["w_q"] = upd(w_q, m_w_q, v_w_q, s_q, "adam_q")
    res["w_kv"] = upd(w_kv, m_w_kv, v_w_kv, s_kv, "adam_kv")
    res["w_in_a"] = upd(w_in_a, m_w_in_a, v_w_in_a, s_in_a, "adam_in_a")
    res["w_out_a"] = upd(w_out_a, m_w_out_a, v_w_out_a, s_out_a, "adam_out_a")
    res["w_up"] = _adamw_layers(w_up, m_w_up, v_w_up, s_up0, s_up1, "adam_up")
    res["w_down"] = _adamw_layers(w_down, m_w_down, v_w_down, s_down0, s_down1, "adam_down")
    res["w_ple_gate"] = _adamw_layers(w_ple_gate, m_w_ple_gate, v_w_ple_gate, s_gate0, s_gate1, "adam_gate")
    res["w_ple_proj"] = _adamw_layers(w_ple_proj, m_w_ple_proj, v_w_ple_proj, s_proj0, s_proj1, "adam_proj")

    res["w_spatial"] = [o.reshape(w_spatial.shape) for o in _adamw(
        w_spatial.reshape(-1, 128), m_w_spatial.reshape(-1, 128), v_w_spatial.reshape(-1, 128), s_ws, "adam_spatial")]
    outs = _adamw(w_pack, m_pack, v_pack, g_all, "adam_small")
    loss = outs[0][-8, 0]
    row = 0
    for nm, g, w, m, v in small:
        nrows = _rows128(w).shape[0]
        res[nm] = [o[row:row + nrows].reshape(-1)[:w.size].reshape(w.shape) for o in outs]
        row += nrows
    for nm, g, w, m, v in sharded_vec:
        res[nm] = [lax.dynamic_slice_in_dim(o[row:row + NDEV], me, 1, axis=0) for o in outs]
        row += NDEV

    names = ["ln_mix_a", "w_in_a", "g_v_a", "w_spatial", "b_spatial", "w_out_a", "ln_kv", "w_kv", "g_k", "ln_mix_b",
             "w_q", "g_q", "w_out_b", "ln_mlp", "w_up", "w_down", "ln_ple", "w_ple_gate", "w_ple_proj"]
    out = [loss, dx0[None]]
    for t in range(4):
        out += [res[nm][t] for nm in names]
    return tuple(out)
```
